```python
import jax, jax.numpy as jnp
from jax import lax
import numpy as np

D_MODEL = 2048
BATCH = 8
SEQ = 2048
DEPTH = 1

N_META = 16
POOL_WIDTH = D_MODEL // 2
POOL_WINDOWS = (2, 4, 8, 16)
N_POOL_GROUPS = len(POOL_WINDOWS)
POOL_GROUP_DIM = POOL_WIDTH // N_POOL_GROUPS
CONV_WIDTH = D_MODEL // 2
CONV_KERNEL = 31
D_FF = 4 * D_MODEL
IN_COLS = POOL_WIDTH + 2 * CONV_WIDTH + 2 * D_MODEL
RMS_EPS = 1e-6
LN_EPS = 1e-5

kernel_name = "hybrid_pool_conformer_gated_block"


def rms_norm(x, g):
    xf = x.astype(jnp.float32)
    y = xf * lax.rsqrt(jnp.mean(xf * xf, axis=-1, keepdims=True) + RMS_EPS)
    return (y * g.astype(jnp.float32)).astype(x.dtype)


def layer_norm(x, g, b):
    xf = x.astype(jnp.float32)
    mu = jnp.mean(xf, axis=-1, keepdims=True)
    var = jnp.mean(jnp.square(xf - mu), axis=-1, keepdims=True)
    y = (xf - mu) * lax.rsqrt(var + LN_EPS)
    return (y * g.astype(jnp.float32) + b.astype(jnp.float32)).astype(x.dtype)


def causal_multiscale_pool(z, w_grp, scale):
    B, L, _ = z.shape
    zf = z.astype(jnp.float32).reshape(B, L, N_POOL_GROUPS, POOL_GROUP_DIM)
    cs = jnp.cumsum(zf, axis=1)
    pos = jnp.arange(L)
    means = []
    for g, w in enumerate(POOL_WINDOWS):
        csg = cs[:, :, g]
        lag = jnp.pad(csg[:, : L - w], ((0, 0), (w, 0), (0, 0)))
        cnt = jnp.minimum(pos + 1, w).astype(jnp.float32)[None, :, None]
        means.append((csg - lag) / cnt)
    pooled = jnp.stack(means, axis=2)
    d = (pooled - zf).astype(z.dtype)
    y = jnp.einsum('blgc,gcd->blgd', d, w_grp).reshape(B, L, POOL_WIDTH)
    return y * scale


def conformer_conv(v, gate, w_dw, b_dw, ln_g, ln_b):
    a = v * jax.nn.sigmoid(gate)
    c = lax.conv_general_dilated(
        a, w_dw[:, None, :], window_strides=(1,),
        padding=[(CONV_KERNEL - 1, 0)],
        dimension_numbers=('NWC', 'WIO', 'NWC'),
        feature_group_count=CONV_WIDTH) + b_dw
    return jax.nn.silu(layer_norm(c, ln_g, ln_b))


def _fwd_setup_inputs(seed: int = 0) -> dict:
    key = jax.random.key(seed)
    ks = jax.random.split(key, 24)
    f32 = jnp.float32
    n = lambda k, shape, s: jax.random.normal(k, shape, f32) * s
    gain = lambda k, shape: 1.0 + 0.05 * jax.random.normal(k, shape, f32)
    return {
        "x": jax.random.normal(ks[0], (BATCH, SEQ, D_MODEL), f32),
        "meta": n(ks[1], (N_META, D_MODEL), 1.0),
        "g_pre_mix": gain(ks[2], (DEPTH, D_MODEL)),
        "w_in": n(ks[3], (DEPTH, D_MODEL, IN_COLS), D_MODEL ** -0.5),
        "w_pool_grp": n(ks[4], (DEPTH, N_POOL_GROUPS, POOL_GROUP_DIM, POOL_GROUP_DIM), POOL_GROUP_DIM ** -0.5),
        "pool_scale": gain(ks[5], (DEPTH, POOL_WIDTH)),
        "w_pool_out": n(ks[6], (DEPTH, POOL_WIDTH, D_MODEL), POOL_WIDTH ** -0.5),
        "w_dw": n(ks[7], (DEPTH, CONV_KERNEL, CONV_WIDTH), CONV_KERNEL ** -0.5),
        "b_dw": n(ks[8], (DEPTH, CONV_WIDTH), 0.02),
        "conv_ln_g": gain(ks[9], (DEPTH, CONV_WIDTH)),
        "conv_ln_b": n(ks[10], (DEPTH, CONV_WIDTH), 0.02),
        "w_conv_out": n(ks[11], (DEPTH, CONV_WIDTH, D_MODEL), CONV_WIDTH ** -0.5),
        "w_o": n(ks[12], (DEPTH, D_MODEL, D_MODEL), D_MODEL ** -0.5),
        "g_post_mix": gain(ks[13], (DEPTH, D_MODEL)),
        "g_pre_mlp": gain(ks[14], (DEPTH, D_MODEL)),
        "w_up": n(ks[15], (DEPTH, D_MODEL, D_FF), D_MODEL ** -0.5),
        "w_down": n(ks[16], (DEPTH, D_FF, D_MODEL), D_FF ** -0.5),
        "g_post_mlp": gain(ks[17], (DEPTH, D_MODEL)),
    }


def _fwd_reference(x, meta, g_pre_mix, w_in, w_pool_grp, pool_scale, w_pool_out,
              w_dw, b_dw, conv_ln_g, conv_ln_b, w_conv_out, w_o, g_post_mix,
              g_pre_mlp, w_up, w_down, g_post_mlp):
    B = x.shape[0]
    meta_b = jnp.broadcast_to(meta[None].astype(x.dtype), (B, N_META, D_MODEL))
    h = jnp.concatenate([meta_b, x], axis=1)
    splits = np.cumsum([POOL_WIDTH, CONV_WIDTH, CONV_WIDTH, D_MODEL]).tolist()
    for l in range(DEPTH):
        u = rms_norm(h, g_pre_mix[l])
        proj = u @ w_in[l]
        z_pool, v_conv, g_conv, gate_a, gate_b = jnp.split(proj, splits, axis=-1)
        y_a = causal_multiscale_pool(z_pool, w_pool_grp[l], pool_scale[l]) @ w_pool_out[l]
        y_b = conformer_conv(v_conv, g_conv, w_dw[l], b_dw[l],
                             conv_ln_g[l], conv_ln_b[l]) @ w_conv_out[l]
        m = jax.nn.sigmoid(gate_a) * y_a + jax.nn.sigmoid(gate_b) * y_b
        h = h + rms_norm(m @ w_o[l], g_post_mix[l])
        u = rms_norm(h, g_pre_mlp[l])
        f = jnp.square(jax.nn.relu(u @ w_up[l])) @ w_down[l]
        h = h + rms_norm(f, g_post_mlp[l])
    return h[:, N_META:]


import jax as _jax
import jax.numpy as _jnp

TWIN_FORMAT = 'train_step'
FWD_PARAMS = ['x', 'meta', 'g_pre_mix', 'w_in', 'w_pool_grp', 'pool_scale', 'w_pool_out', 'w_dw', 'b_dw', 'conv_ln_g', 'conv_ln_b', 'w_conv_out', 'w_o', 'g_post_mix', 'g_pre_mlp', 'w_up', 'w_down', 'g_post_mlp']
TWIN_WEIGHTS = ['meta', 'g_pre_mix', 'w_in', 'w_pool_grp', 'pool_scale', 'w_pool_out', 'w_dw', 'b_dw', 'conv_ln_g', 'conv_ln_b', 'w_conv_out', 'w_o', 'g_post_mix', 'g_pre_mlp', 'w_up', 'w_down', 'g_post_mlp']
TWIN_DIFF_INPUT = 'x'
TWIN_INPUTS = ['x', 'meta', 'g_pre_mix', 'w_in', 'w_pool_grp', 'pool_scale', 'w_pool_out', 'w_dw', 'b_dw', 'conv_ln_g', 'conv_ln_b', 'w_conv_out', 'w_o', 'g_post_mix', 'g_pre_mlp', 'w_up', 'w_down', 'g_post_mlp', 'loss_target', 'm_meta', 'm_g_pre_mix', 'm_w_in', 'm_w_pool_grp', 'm_pool_scale', 'm_w_pool_out', 'm_w_dw', 'm_b_dw', 'm_conv_ln_g', 'm_conv_ln_b', 'm_w_conv_out', 'm_w_o', 'm_g_post_mix', 'm_g_pre_mlp', 'm_w_up', 'm_w_down', 'm_g_post_mlp', 'v_meta', 'v_g_pre_mix', 'v_w_in', 'v_w_pool_grp', 'v_pool_scale', 'v_w_pool_out', 'v_w_dw', 'v_b_dw', 'v_conv_ln_g', 'v_conv_ln_b', 'v_w_conv_out', 'v_w_o', 'v_g_post_mix', 'v_g_pre_mlp', 'v_w_up', 'v_w_down', 'v_g_post_mlp']
TWIN_OUTPUTS = ['loss', 'grad_x', 'grad_meta', 'grad_g_pre_mix', 'grad_w_in', 'grad_w_pool_grp', 'grad_pool_scale', 'grad_w_pool_out', 'grad_w_dw', 'grad_b_dw', 'grad_conv_ln_g', 'grad_conv_ln_b', 'grad_w_conv_out', 'grad_w_o', 'grad_g_post_mix', 'grad_g_pre_mlp', 'grad_w_up', 'grad_w_down', 'grad_g_post_mlp', 'delta_meta', 'delta_g_pre_mix', 'delta_w_in', 'delta_w_pool_grp', 'delta_pool_scale', 'delta_w_pool_out', 'delta_w_dw', 'delta_b_dw', 'delta_conv_ln_g', 'delta_conv_ln_b', 'delta_w_conv_out', 'delta_w_o', 'delta_g_post_mix', 'delta_g_pre_mlp', 'delta_w_up', 'delta_w_down', 'delta_g_post_mlp', 'new_m_meta', 'new_m_g_pre_mix', 'new_m_w_in', 'new_m_w_pool_grp', 'new_m_pool_scale', 'new_m_w_pool_out', 'new_m_w_dw', 'new_m_b_dw', 'new_m_conv_ln_g', 'new_m_conv_ln_b', 'new_m_w_conv_out', 'new_m_w_o', 'new_m_g_post_mix', 'new_m_g_pre_mlp', 'new_m_w_up', 'new_m_w_down', 'new_m_g_post_mlp', 'new_v_meta', 'new_v_g_pre_mix', 'new_v_w_in', 'new_v_w_pool_grp', 'new_v_pool_scale', 'new_v_w_pool_out', 'new_v_w_dw', 'new_v_b_dw', 'new_v_conv_ln_g', 'new_v_conv_ln_b', 'new_v_w_conv_out', 'new_v_w_o', 'new_v_g_post_mix', 'new_v_g_pre_mlp', 'new_v_w_up', 'new_v_w_down', 'new_v_g_post_mlp']
TWIN_LEAF_KINDS = {'loss': 'loss', 'grad_x': 'grad_x', 'grad_meta': 'grad_w', 'grad_g_pre_mix': 'grad_w', 'grad_w_in': 'grad_w', 'grad_w_pool_grp': 'grad_w', 'grad_pool_scale': 'grad_w', 'grad_w_pool_out': 'grad_w', 'grad_w_dw': 'grad_w', 'grad_b_dw': 'grad_w', 'grad_conv_ln_g': 'grad_w', 'grad_conv_ln_b': 'grad_w', 'grad_w_conv_out': 'grad_w', 'grad_w_o': 'grad_w', 'grad_g_post_mix': 'grad_w', 'grad_g_pre_mlp': 'grad_w', 'grad_w_up': 'grad_w', 'grad_w_down': 'grad_w', 'grad_g_post_mlp': 'grad_w', 'delta_meta': 'delta_w', 'delta_g_pre_mix': 'delta_w', 'delta_w_in': 'delta_w', 'delta_w_pool_grp': 'delta_w', 'delta_pool_scale': 'delta_w', 'delta_w_pool_out': 'delta_w', 'delta_w_dw': 'delta_w', 'delta_b_dw': 'delta_w', 'delta_conv_ln_g': 'delta_w', 'delta_conv_ln_b': 'delta_w', 'delta_w_conv_out': 'delta_w', 'delta_w_o': 'delta_w', 'delta_g_post_mix': 'delta_w', 'delta_g_pre_mlp': 'delta_w', 'delta_w_up': 'delta_w', 'delta_w_down': 'delta_w', 'delta_g_post_mlp': 'delta_w', 'new_m_meta': 'new_m', 'new_m_g_pre_mix': 'new_m', 'new_m_w_in': 'new_m', 'new_m_w_pool_grp': 'new_m', 'new_m_pool_scale': 'new_m', 'new_m_w_pool_out': 'new_m', 'new_m_w_dw': 'new_m', 'new_m_b_dw': 'new_m', 'new_m_conv_ln_g': 'new_m', 'new_m_conv_ln_b': 'new_m', 'new_m_w_conv_out': 'new_m', 'new_m_w_o': 'new_m', 'new_m_g_post_mix': 'new_m', 'new_m_g_pre_mlp': 'new_m', 'new_m_w_up': 'new_m', 'new_m_w_down': 'new_m', 'new_m_g_post_mlp': 'new_m', 'new_v_meta': 'new_v', 'new_v_g_pre_mix': 'new_v', 'new_v_w_in': 'new_v', 'new_v_w_pool_grp': 'new_v', 'new_v_pool_scale': 'new_v', 'new_v_w_pool_out': 'new_v', 'new_v_w_dw': 'new_v', 'new_v_b_dw': 'new_v', 'new_v_conv_ln_g': 'new_v', 'new_v_conv_ln_b': 'new_v', 'new_v_w_conv_out': 'new_v', 'new_v_w_o': 'new_v', 'new_v_g_post_mix': 'new_v', 'new_v_g_pre_mlp': 'new_v', 'new_v_w_up': 'new_v', 'new_v_w_down': 'new_v', 'new_v_g_post_mlp': 'new_v'}


def _forward(args):
    return _fwd_reference(*[args[k] for k in FWD_PARAMS])


def _output_shape():
    out = _jax.eval_shape(lambda: _forward(_fwd_setup_inputs(0)))
    return out.shape, out.dtype

N_MICROBATCH = 1
ADAM_LR = 0.001
ADAM_B1 = 0.9
ADAM_B2 = 0.999
ADAM_EPS = 1e-08
ADAM_WD = 0.01
ADAM_STEP = 10
PER_EXAMPLE_BATCH_AXIS = {'x': 0, 'loss_target': 0}
SHARED_INPUTS = []
_WEIGHT_DTYPES = {'meta': _jnp.float32, 'g_pre_mix': _jnp.float32, 'w_in': _jnp.float32, 'w_pool_grp': _jnp.float32, 'pool_scale': _jnp.float32, 'w_pool_out': _jnp.float32, 'w_dw': _jnp.float32, 'b_dw': _jnp.float32, 'conv_ln_g': _jnp.float32, 'conv_ln_b': _jnp.float32, 'w_conv_out': _jnp.float32, 'w_o': _jnp.float32, 'g_post_mix': _jnp.float32, 'g_pre_mlp': _jnp.float32, 'w_up': _jnp.float32, 'w_down': _jnp.float32, 'g_post_mlp': _jnp.float32}
MOMENT_SCALE = {'meta': 5.833265e-03, 'g_pre_mix': 2.038320e-01, 'w_in': 1.026096e-01, 'w_pool_grp': 2.505700e-01, 'pool_scale': 2.816727e-01, 'w_pool_out': 1.931721e-01, 'w_dw': 2.333391e-01, 'b_dw': 3.852624e+00, 'conv_ln_g': 1.462962e+00, 'conv_ln_b': 2.094736e+00, 'w_conv_out': 6.232173e-01, 'w_o': 6.230320e-01, 'g_post_mix': 8.072948e+00, 'g_pre_mlp': 2.449609e-01, 'w_up': 1.208049e-01, 'w_down': 6.386936e-01, 'g_post_mlp': 8.295762e+00}


def _to_microbatches(a, axis):
    t = _jnp.moveaxis(a, axis, 0)
    t = t.reshape((N_MICROBATCH, t.shape[0] // N_MICROBATCH) + t.shape[1:])
    return _jnp.moveaxis(t, 1, axis + 1)


def setup_inputs(seed: int = 0) -> dict:
    inp = _fwd_setup_inputs(seed)
    key = _jax.random.fold_in(_jax.random.key(seed), 7919)
    shape, _ = _output_shape()
    out = dict(inp)
    out["loss_target"] = _jax.random.normal(_jax.random.fold_in(key, 0), shape, _jnp.float32)
    for i, name in enumerate(TWIN_WEIGHTS):
        w = inp[name].astype(_jnp.float32)
        if MOMENT_SCALE is None:
            s = _jnp.sqrt(_jnp.mean(_jnp.square(w)) + 1e-30)
        else:
            s = MOMENT_SCALE[name]
        km, kv = _jax.random.split(_jax.random.fold_in(key, i + 1))
        out[name] = w
        out["m_" + name] = s * _jax.random.normal(km, w.shape, _jnp.float32)
        out["v_" + name] = (s * s) * _jax.random.uniform(kv, w.shape, _jnp.float32, 0.5, 1.5)
    if N_MICROBATCH > 1:
        for name, axis in PER_EXAMPLE_BATCH_AXIS.items():
            out[name] = _to_microbatches(out[name], axis)
    return {'x': out['x'], 'meta': out['meta'], 'g_pre_mix': out['g_pre_mix'], 'w_in': out['w_in'], 'w_pool_grp': out['w_pool_grp'], 'pool_scale': out['pool_scale'], 'w_pool_out': out['w_pool_out'], 'w_dw': out['w_dw'], 'b_dw': out['b_dw'], 'conv_ln_g': out['conv_ln_g'], 'conv_ln_b': out['conv_ln_b'], 'w_conv_out': out['w_conv_out'], 'w_o': out['w_o'], 'g_post_mix': out['g_post_mix'], 'g_pre_mlp': out['g_pre_mlp'], 'w_up': out['w_up'], 'w_down': out['w_down'], 'g_post_mlp': out['g_post_mlp'], 'loss_target': out['loss_target'], 'm_meta': out['m_meta'], 'm_g_pre_mix': out['m_g_pre_mix'], 'm_w_in': out['m_w_in'], 'm_w_pool_grp': out['m_w_pool_grp'], 'm_pool_scale': out['m_pool_scale'], 'm_w_pool_out': out['m_w_pool_out'], 'm_w_dw': out['m_w_dw'], 'm_b_dw': out['m_b_dw'], 'm_conv_ln_g': out['m_conv_ln_g'], 'm_conv_ln_b': out['m_conv_ln_b'], 'm_w_conv_out': out['m_w_conv_out'], 'm_w_o': out['m_w_o'], 'm_g_post_mix': out['m_g_post_mix'], 'm_g_pre_mlp': out['m_g_pre_mlp'], 'm_w_up': out['m_w_up'], 'm_w_down': out['m_w_down'], 'm_g_post_mlp': out['m_g_post_mlp'], 'v_meta': out['v_meta'], 'v_g_pre_mix': out['v_g_pre_mix'], 'v_w_in': out['v_w_in'], 'v_w_pool_grp': out['v_w_pool_grp'], 'v_pool_scale': out['v_pool_scale'], 'v_w_pool_out': out['v_w_pool_out'], 'v_w_dw': out['v_w_dw'], 'v_b_dw': out['v_b_dw'], 'v_conv_ln_g': out['v_conv_ln_g'], 'v_conv_ln_b': out['v_conv_ln_b'], 'v_w_conv_out': out['v_w_conv_out'], 'v_w_o': out['v_w_o'], 'v_g_post_mix': out['v_g_post_mix'], 'v_g_pre_mlp': out['v_g_pre_mlp'], 'v_w_up': out['v_w_up'], 'v_w_down': out['v_w_down'], 'v_g_post_mlp': out['v_g_post_mlp']}


def _loss(weights, diff, rest, loss_target):
    with _jax.named_scope("forward"):
        args = {**rest, TWIN_DIFF_INPUT: diff, **{k: w.astype(_WEIGHT_DTYPES[k]) for k, w in weights.items()}}
        y = _forward(args)
    with _jax.named_scope("loss_head"):
        err = _jnp.square(y.astype(_jnp.float32) - loss_target)
        return 0.5 * _jnp.sum(_jnp.mean(err, axis=-1)) if err.ndim else 0.5 * err


def _adamw(w, g, m, v):
    m = ADAM_B1 * m + (1.0 - ADAM_B1) * g
    v = ADAM_B2 * v + (1.0 - ADAM_B2) * _jnp.square(g)
    m_hat = m / (1.0 - ADAM_B1 ** ADAM_STEP)
    v_hat = v / (1.0 - ADAM_B2 ** ADAM_STEP)
    delta = -ADAM_LR * (m_hat / (_jnp.sqrt(v_hat) + ADAM_EPS) + ADAM_WD * w)
    return delta, m, v


def reference(x, meta, g_pre_mix, w_in, w_pool_grp, pool_scale, w_pool_out, w_dw, b_dw, conv_ln_g, conv_ln_b, w_conv_out, w_o, g_post_mix, g_pre_mlp, w_up, w_down, g_post_mlp, loss_target, m_meta, m_g_pre_mix, m_w_in, m_w_pool_grp, m_pool_scale, m_w_pool_out, m_w_dw, m_b_dw, m_conv_ln_g, m_conv_ln_b, m_w_conv_out, m_w_o, m_g_post_mix, m_g_pre_mlp, m_w_up, m_w_down, m_g_post_mlp, v_meta, v_g_pre_mix, v_w_in, v_w_pool_grp, v_pool_scale, v_w_pool_out, v_w_dw, v_b_dw, v_conv_ln_g, v_conv_ln_b, v_w_conv_out, v_w_o, v_g_post_mix, v_g_pre_mlp, v_w_up, v_w_down, v_g_post_mlp):
    given = dict(x=x, meta=meta, g_pre_mix=g_pre_mix, w_in=w_in, w_pool_grp=w_pool_grp, pool_scale=pool_scale, w_pool_out=w_pool_out, w_dw=w_dw, b_dw=b_dw, conv_ln_g=conv_ln_g, conv_ln_b=conv_ln_b, w_conv_out=w_conv_out, w_o=w_o, g_post_mix=g_post_mix, g_pre_mlp=g_pre_mlp, w_up=w_up, w_down=w_down, g_post_mlp=g_post_mlp, loss_target=loss_target, m_meta=m_meta, m_g_pre_mix=m_g_pre_mix, m_w_in=m_w_in, m_w_pool_grp=m_w_pool_grp, m_pool_scale=m_pool_scale, m_w_pool_out=m_w_pool_out, m_w_dw=m_w_dw, m_b_dw=m_b_dw, m_conv_ln_g=m_conv_ln_g, m_conv_ln_b=m_conv_ln_b, m_w_conv_out=m_w_conv_out, m_w_o=m_w_o, m_g_post_mix=m_g_post_mix, m_g_pre_mlp=m_g_pre_mlp, m_w_up=m_w_up, m_w_down=m_w_down, m_g_post_mlp=m_g_post_mlp, v_meta=v_meta, v_g_pre_mix=v_g_pre_mix, v_w_in=v_w_in, v_w_pool_grp=v_w_pool_grp, v_pool_scale=v_pool_scale, v_w_pool_out=v_w_pool_out, v_w_dw=v_w_dw, v_b_dw=v_b_dw, v_conv_ln_g=v_conv_ln_g, v_conv_ln_b=v_conv_ln_b, v_w_conv_out=v_w_conv_out, v_w_o=v_w_o, v_g_post_mix=v_g_post_mix, v_g_pre_mlp=v_g_pre_mlp, v_w_up=v_w_up, v_w_down=v_w_down, v_g_post_mlp=v_g_post_mlp)
    weights = {n: given[n] for n in TWIN_WEIGHTS}
    shared = {n: given[n] for n in SHARED_INPUTS}
    per_example = {n: given[n] for n in ['x']}
    grad_fn = _jax.value_and_grad(_loss, argnums=(0, 1))

    def one_microbatch(ex, loss_target):
        ex = dict(ex)
        diff = ex.pop(TWIN_DIFF_INPUT)
        return grad_fn(weights, diff, {**shared, **ex}, loss_target)

    if N_MICROBATCH == 1:
        loss, (grad_w, grad_x) = one_microbatch(per_example, given["loss_target"])
    else:
        def body(carry, xs):
            loss_sum, grad_sum = carry
            l_k, (gw_k, gx_k) = one_microbatch(xs[0], xs[1])
            with _jax.named_scope("update"):
                return (loss_sum + l_k, _jax.tree.map(_jnp.add, grad_sum, gw_k)), gx_k

        init = (_jnp.zeros((), _jnp.float32), _jax.tree.map(_jnp.zeros_like, weights))
        (loss, grad_w), grad_x = _jax.lax.scan(body, init, (per_example, given["loss_target"]))
    with _jax.named_scope("update"):
        delta_w, new_m, new_v = {}, {}, {}
        for n in TWIN_WEIGHTS:
            delta_w[n], new_m[n], new_v[n] = _adamw(weights[n], grad_w[n], given["m_" + n], given["v_" + n])
    return (loss, grad_x, *[grad_w[n] for n in TWIN_WEIGHTS], *[delta_w[n] for n in TWIN_WEIGHTS],
            *[new_m[n] for n in TWIN_WEIGHTS], *[new_v[n] for n in TWIN_WEIGHTS])
```

```python
import functools

import jax
import jax.numpy as jnp
from jax import lax
from jax.experimental import pallas as pl
from jax.experimental.pallas import tpu as pltpu

F32 = jnp.float32
BF16 = jnp.bfloat16
MESH = pl.DeviceIdType.MESH

N_DEV = 8
N_META = 16
POOL_WINDOWS = (2, 4, 8, 16)
CONV_KERNEL = 31
CONV_TAPS_PADDED = 32
RMS_EPS = 1e-6
LN_EPS = 1e-5
ADAM_LR = 0.001
ADAM_B1 = 0.9
ADAM_B2 = 0.999
ADAM_EPS = 1e-08
ADAM_WD = 0.01
ADAM_STEP = 10

LANES = 128
SUBLANES = 8
ROW_CHUNK = 128
HALO = 32
VMEM_LIMIT_BYTES = 56 * 1024 * 1024


def _cparams(*sem):
    return pltpu.CompilerParams(dimension_semantics=sem if sem else None, vmem_limit_bytes=VMEM_LIMIT_BYTES)


def _round_up(n, m):
    return (n + m - 1) // m * m


def _row_tile(rows, cols, max_elems=640 * 1024):
    best = None
    for t in range(16, rows + 1, 16):
        if rows % t == 0 and (best is None or t * cols <= max_elems):
            best = t
    assert best is not None, (rows, cols)
    return best


def _rowsum8(a):
    t, w = a.shape
    return a.reshape(t // SUBLANES, SUBLANES, w).sum(axis=0)


def _mesh_pos():
    return lax.axis_index("x"), lax.axis_index("y"), lax.axis_index("c")


def _all_gather(shards, name):
    n = len(shards)

    def body(*refs):
        ins, outs = refs[:n], refs[n:2 * n]
        send_sems, recv_sems, local_sems = refs[2 * n:]
        x, y, c = _mesh_pos()
        me, sibling = (x, y, c), (x, y, 1 - c)
        chips = [(1 - x, y), (x, 1 - y), (1 - x, 1 - y)]

        def slot(p):
            return 4 * p[0] + 2 * p[1] + p[2]

        def copy(t, k, block, to, src=None):
            dst = outs[t].at[slot(block)]
            return pltpu.make_async_remote_copy(
                src_ref=dst if src is None else src, dst_ref=dst,
                send_sem=send_sems.at[t, k], recv_sem=recv_sems.at[t, k],
                device_id=to, device_id_type=MESH)

        mine = [pltpu.make_async_copy(ins[t], outs[t].at[slot(me)], local_sems.at[t]) for t in range(n)]
        for cp in mine:
            cp.start()
        first = []
        for t in range(n):
            first.append(copy(t, 0, me, sibling, src=ins[t]))
            first += [copy(t, 1 + j, me, (*chip, c), src=ins[t]) for j, chip in enumerate(chips)]
        for cp in first:
            cp.start()
        passed = []
        for t in range(n):
            for j, chip in enumerate(chips):
                copy(t, 1 + j, (*chip, c), me).wait_recv()
                fwd = copy(t, 4 + j, (*chip, c), sibling)
                fwd.start()
                passed.append(fwd)
        for t in range(n):
            copy(t, 0, sibling, me).wait_recv()
            for j, chip in enumerate(chips):
                copy(t, 4 + j, (*chip, 1 - c), me).wait_recv()
        for cp in first + passed:
            cp.wait_send()
        for cp in mine:
            cp.wait()

    any_spec = pl.BlockSpec(memory_space=pl.ANY)
    return pl.pallas_call(
        body, name=name,
        out_shape=[jax.ShapeDtypeStruct((N_DEV,) + s.shape, s.dtype) for s in shards],
        in_specs=[any_spec] * n, out_specs=[any_spec] * n,
        scratch_shapes=[pltpu.SemaphoreType.DMA((n, 7)), pltpu.SemaphoreType.DMA((n, 7)),
                        pltpu.SemaphoreType.DMA((n,))],
    )(*shards)


def _pair_exchange(grads, name):
    n = len(grads)

    def body(*refs):
        ins, outs = refs[:n], refs[n:2 * n]
        send_sems, recv_sems = refs[2 * n:]
        x, y, c = _mesh_pos()
        sibling = (x, y, 1 - c)
        copies = []
        for t in range(n):
            for q in range(4):
                copies.append(pltpu.make_async_remote_copy(
                    src_ref=ins[t].at[2 * q + 1 - c], dst_ref=outs[t].at[q],
                    send_sem=send_sems.at[t, q], recv_sem=recv_sems.at[t, q],
                    device_id=sibling, device_id_type=MESH))
        for cp in copies:
            cp.start()
        for cp in copies:
            cp.wait()

    any_spec = pl.BlockSpec(memory_space=pl.ANY)
    return pl.pallas_call(
        body, name=name,
        out_shape=[jax.ShapeDtypeStruct((4,) + g.shape[1:], g.dtype) for g in grads],
        in_specs=[any_spec] * n, out_specs=[any_spec] * n,
        scratch_shapes=[pltpu.SemaphoreType.DMA((n, 4)), pltpu.SemaphoreType.DMA((n, 4))],
    )(*grads)


def _chip_exchange(sums, name):
    n = len(sums)

    def body(*refs):
        ins, outs = refs[:n], refs[n:2 * n]
        send_sems, recv_sems = refs[2 * n:]
        x, y, c = _mesh_pos()
        chips = [(1 - x, y), (x, 1 - y), (1 - x, 1 - y)]
        copies = []
        for t in range(n):
            for r, chip in enumerate(chips):
                copies.append(pltpu.make_async_remote_copy(
                    src_ref=ins[t].at[2 * chip[0] + chip[1]], dst_ref=outs[t].at[r],
                    send_sem=send_sems.at[t, r], recv_sem=recv_sems.at[t, r],
                    device_id=(*chip, c), device_id_type=MESH))
        for cp in copies:
            cp.start()
        for cp in copies:
            cp.wait()

    any_spec = pl.BlockSpec(memory_space=pl.ANY)
    return pl.pallas_call(
        body, name=name,
        out_shape=[jax.ShapeDtypeStruct((3,) + s.shape[1:], s.dtype) for s in sums],
        in_specs=[any_spec] * n, out_specs=[any_spec] * n,
        scratch_shapes=[pltpu.SemaphoreType.DMA((n, 3)), pltpu.SemaphoreType.DMA((n, 3))],
    )(*sums)


def _mm(a, b, *, mode, tm, tn, tk=None, b_blocked=False, out_blocked=False, out_dtypes=(F32,),
        epilogue=None, extras=(), name):
    if mode == "nn":
        m, k = a.shape
        n = b.shape[0] * b.shape[2] if b_blocked else b.shape[1]
        dims = (((1,), (0,)), ((), ()))
    elif mode == "nt":
        m, k = a.shape
        n = b.shape[1] if b_blocked else b.shape[0]
        if b_blocked:
            tk = b.shape[2]
        dims = (((1,), (1,)), ((), ()))
    else:
        k, m = a.shape
        n = b.shape[1]
        dims = (((0,), (0,)), ((), ()))
    tk = k if tk is None else tk
    assert m % tm == 0 and n % tn == 0 and k % tk == 0, (name, m, n, k, tm, tn, tk)
    gm, gn, gk = m // tm, n // tn, k // tk
    if b_blocked:
        assert (tn if mode == "nn" else tk) == b.shape[2], name

    if mode == "nn":
        a_spec = pl.BlockSpec((tm, tk), lambda i, j, kk: (i, kk))
        b_spec = (pl.BlockSpec((None, tk, tn), lambda i, j, kk: (j, kk, 0)) if b_blocked
                  else pl.BlockSpec((tk, tn), lambda i, j, kk: (kk, j)))
    elif mode == "nt":
        a_spec = pl.BlockSpec((tm, tk), lambda i, j, kk: (i, kk))
        b_spec = (pl.BlockSpec((None, tn, tk), lambda i, j, kk: (kk, j, 0)) if b_blocked
                  else pl.BlockSpec((tn, tk), lambda i, j, kk: (j, kk)))
    else:
        a_spec = pl.BlockSpec((tk, tm), lambda i, j, kk: (kk, i))
        b_spec = pl.BlockSpec((tk, tn), lambda i, j, kk: (kk, j))
    if out_blocked:
        out_spec = pl.BlockSpec((None, tm, tn), lambda i, j, kk: (j, i, 0))
        out_shape = (gn, m, tn)
    else:
        out_spec = pl.BlockSpec((tm, tn), lambda i, j, kk: (i, j))
        out_shape = (m, n)
    extra_specs = [pl.BlockSpec((tm, tn), functools.partial(lambda i, j, kk, off: (i, j + off), off=off))
                   for _, off in extras]
    n_extra, n_out = len(extras), len(out_dtypes)

    def body(a_ref, b_ref, *rest):
        extra_refs, out_refs = rest[:n_extra], rest[n_extra:n_extra + n_out]

        def finish(acc):
            if epilogue is None:
                res = (acc,)
            else:
                res = epilogue(acc, *[e[...] for e in extra_refs])
            for o_ref, r in zip(out_refs, res):
                o_ref[...] = r.astype(o_ref.dtype)

        part = lax.dot_general(a_ref[...], b_ref[...], dims, preferred_element_type=F32)
        if gk == 1:
            finish(part)
        else:
            acc_ref = rest[-1]
            kk = pl.program_id(2)

            @pl.when(kk == 0)
            def _():
                acc_ref[...] = part

            @pl.when(kk > 0)
            def _():
                acc_ref[...] += part

            @pl.when(kk == gk - 1)
            def _():
                finish(acc_ref[...])

    outs = pl.pallas_call(
        body, name=name, grid=(gm, gn, gk),
        in_specs=[a_spec, b_spec] + extra_specs,
        out_specs=[out_spec] * n_out,
        out_shape=[jax.ShapeDtypeStruct(out_shape, d) for d in out_dtypes],
        scratch_shapes=[pltpu.VMEM((tm, tn), F32)] if gk > 1 else [],
        compiler_params=_cparams("parallel", "parallel", "arbitrary"),
    )(a, b, *[e for e, _ in extras])
    return outs[0] if n_out == 1 else outs


def _gate_mix(ya_pre, s, wpo, wco, proj, d_model, name):
    lp, width = ya_pre.shape
    nb, _, bn = wpo.shape
    ga_off = (proj.shape[1] - 2 * d_model) // bn
    gb_off = (proj.shape[1] - d_model) // bn

    def body(ya_ref, s_ref, wpo_ref, wco_ref, ga_ref, gb_ref, m_ref, y_a_ref, y_b_ref):
        y_a = jnp.dot(ya_ref[...], wpo_ref[...], preferred_element_type=F32)
        y_b = jnp.dot(s_ref[...], wco_ref[...], preferred_element_type=F32)
        m = jax.nn.sigmoid(ga_ref[...]) * y_a + jax.nn.sigmoid(gb_ref[...]) * y_b
        m_ref[...] = m.astype(BF16)
        y_a_ref[...] = y_a.astype(BF16)
        y_b_ref[...] = y_b.astype(BF16)

    act_spec = pl.BlockSpec((lp, width), lambda j: (0, 0))
    w_spec = pl.BlockSpec((None, width, bn), lambda j: (j, 0, 0))
    out_spec = pl.BlockSpec((lp, bn), lambda j: (0, j))
    return pl.pallas_call(
        body, name=name, grid=(nb,),
        in_specs=[act_spec, act_spec, w_spec, w_spec,
                  pl.BlockSpec((lp, bn), lambda j: (0, j + ga_off)),
                  pl.BlockSpec((lp, bn), lambda j: (0, j + gb_off))],
        out_specs=[out_spec] * 3,
        out_shape=[jax.ShapeDtypeStruct((lp, nb * bn), BF16)] * 3,
        compiler_params=_cparams("parallel"),
    )(ya_pre, s, wpo, wco, proj, proj)


def _rms_stats(x):
    return lax.rsqrt(jnp.mean(x * x, axis=-1, keepdims=True) + RMS_EPS)


def _rms_bwd(x, g, dy):
    r = _rms_stats(x)
    nrm = x * r
    dn = dy * g
    dx = r * (dn - nrm * jnp.mean(dn * nrm, axis=-1, keepdims=True))
    return dx, dy * nrm


def _rowwise(body, ins, outs, accs, *, lp, name):
    tr = _row_tile(lp, max(a.shape[1] for a in ins))
    n_in, n_out, n_acc = len(ins), len(outs), len(accs)

    def kernel_body(*refs):
        i = pl.program_id(0)
        acc_refs = refs[n_in + n_out:]

        @pl.when(i == 0)
        def _():
            for r in acc_refs:
                r[...] = jnp.zeros_like(r)

        body(i * tr, refs[:n_in], refs[n_in:n_in + n_out], acc_refs)

    in_specs = []
    for a in ins:
        if a.shape[0] == lp:
            in_specs.append(pl.BlockSpec((tr, a.shape[1]), lambda i: (i, 0)))
        else:
            in_specs.append(pl.BlockSpec(a.shape, lambda i: (0, 0)))
    out_specs = [pl.BlockSpec((tr, w), lambda i: (i, 0)) for w, _ in outs]
    out_specs += [pl.BlockSpec((SUBLANES, w), lambda i: (0, 0)) for w in accs]
    out_shape = [jax.ShapeDtypeStruct((lp, w), d) for w, d in outs]
    out_shape += [jax.ShapeDtypeStruct((SUBLANES, w), F32) for w in accs]
    return pl.pallas_call(
        kernel_body, name=name, grid=(lp // tr,), in_specs=in_specs, out_specs=out_specs,
        out_shape=out_shape, compiler_params=_cparams("arbitrary"),
    )(*ins)


def _rms_pre(h0, g, lp):
    d = h0.shape[1]

    def body(row0, ins, outs, accs):
        h_ref, g_ref = ins
        x = h_ref[...]
        outs[0][...] = (x * _rms_stats(x) * g_ref[...]).astype(BF16)

    return _rowwise(body, [h0, g], [(d, BF16)], [], lp=lp, name="rms_pre")[0]


def _post_mix(o, h0, g_post_mix, g_pre_mlp, lp):
    d = h0.shape[1]

    def body(row0, ins, outs, accs):
        o_ref, h0_ref, g1_ref, g2_ref = ins
        o_v = o_ref[...]
        h1 = h0_ref[...] + o_v * _rms_stats(o_v) * g1_ref[...]
        outs[0][...] = h1
        outs[1][...] = (h1 * _rms_stats(h1) * g2_ref[...]).astype(BF16)

    return _rowwise(body, [o, h0, g_post_mix, g_pre_mlp], [(d, F32), (d, BF16)], [], lp=lp, name="post_mix")


def _loss_head(f, h1, target, g_post_mlp, lp, seq):
    d = f.shape[1]

    def body(row0, ins, outs, accs):
        f_ref, h1_ref, t_ref, g_ref = ins
        df_ref, dh_ref = outs
        dg_ref, loss_ref = accs
        f_v, g = f_ref[...], g_ref[...]
        r = _rms_stats(f_v)
        nrm = f_v * r
        rows = row0 + lax.broadcasted_iota(jnp.int32, (f_v.shape[0], 1), 0)
        valid = (rows >= N_META) & (rows < N_META + seq)
        err = jnp.where(valid, h1_ref[...] + nrm * g - t_ref[...], 0.0)
        loss_ref[...] += 0.5 * jnp.sum(jnp.mean(err * err, axis=-1, keepdims=True))
        dy = err * (1.0 / d)
        dn = dy * g
        df_ref[...] = (r * (dn - nrm * jnp.mean(dn * nrm, axis=-1, keepdims=True))).astype(BF16)
        dh_ref[...] = dy
        dg_ref[...] += _rowsum8(dy * nrm)

    return _rowwise(body, [f, h1, target, g_post_mlp], [(d, BF16), (d, F32)], [d, LANES], lp=lp, name="loss_head")


def _mid_bwd(du2, h1, dh, o, g_pre_mlp, g_post_mix, lp):
    d = h1.shape[1]

    def body(row0, ins, outs, accs):
        du2_ref, h1_ref, dh_ref, o_ref, g2_ref, g1_ref = ins
        dx2, dg2 = _rms_bwd(h1_ref[...], g2_ref[...], du2_ref[...])
        dh1 = dh_ref[...] + dx2
        do, dg1 = _rms_bwd(o_ref[...], g1_ref[...], dh1)
        outs[0][...] = dh1
        outs[1][...] = do.astype(BF16)
        accs[0][...] += _rowsum8(dg2)
        accs[1][...] += _rowsum8(dg1)

    return _rowwise(body, [du2, h1, dh, o, g_pre_mlp, g_post_mix], [(d, F32), (d, BF16)], [d, d], lp=lp,
                    name="mid_bwd")


def _pre_mix_bwd(du1, h0, dh1, g_pre_mix, lp):
    d = h0.shape[1]

    def body(row0, ins, outs, accs):
        du1_ref, h0_ref, dh1_ref, g_ref = ins
        dx, dg = _rms_bwd(h0_ref[...], g_ref[...], du1_ref[...])
        outs[0][...] = dh1_ref[...] + dx
        accs[0][...] += _rowsum8(dg)

    return _rowwise(body, [du1, h0, dh1, g_pre_mix], [(d, F32)], [d], lp=lp, name="pre_mix_bwd")


def _ln_stats(c):
    mu = jnp.mean(c, axis=-1, keepdims=True)
    var = jnp.mean(jnp.square(c - mu), axis=-1, keepdims=True)
    return mu, lax.rsqrt(var + LN_EPS)


def _ln_silu(c, ln_g, ln_b, lp):
    w = c.shape[1]

    def body(row0, ins, outs, accs):
        c_ref, g_ref, b_ref = ins
        c_v = c_ref[...]
        mu, rstd = _ln_stats(c_v)
        ln = (c_v - mu) * rstd * g_ref[...] + b_ref[...]
        outs[0][...] = (ln * jax.nn.sigmoid(ln)).astype(BF16)

    return _rowwise(body, [c, ln_g, ln_b], [(w, BF16)], [], lp=lp, name="ln_silu")[0]


def _ln_silu_bwd(c, ds, ln_g, ln_b, lp):
    w = c.shape[1]

    def body(row0, ins, outs, accs):
        c_ref, ds_ref, g_ref, b_ref = ins
        c_v, g = c_ref[...], g_ref[...]
        mu, rstd = _ln_stats(c_v)
        nrm = (c_v - mu) * rstd
        ln = nrm * g + b_ref[...]
        sig = jax.nn.sigmoid(ln)
        dln = ds_ref[...] * (sig * (1.0 + ln * (1.0 - sig)))
        dn = dln * g
        dc = rstd * (dn - jnp.mean(dn, axis=-1, keepdims=True) - nrm * jnp.mean(dn * nrm, axis=-1, keepdims=True))
        outs[0][...] = dc
        accs[0][...] += _rowsum8(dln * nrm)
        accs[1][...] += _rowsum8(dln)
        accs[2][...] += _rowsum8(dc)

    return _rowwise(body, [c, ds, ln_g, ln_b], [(w, F32)], [w, w, w], lp=lp, name="ln_silu_bwd")


def _chunk_with_history(ref, i, cols=slice(None)):
    t0 = pl.multiple_of(i * ROW_CHUNK, ROW_CHUNK)
    lo0 = pl.multiple_of(jnp.maximum(t0 - HALO, 0), SUBLANES)
    lo = jnp.where(i > 0, ref[pl.ds(lo0, HALO), cols], 0.0)
    return jnp.concatenate([lo, ref[pl.ds(t0, ROW_CHUNK), cols]], axis=0)


def _chunk_with_future(ref, i, n_chunks, cols=slice(None)):
    t0 = pl.multiple_of(i * ROW_CHUNK, ROW_CHUNK)
    hi0 = pl.multiple_of(jnp.minimum(t0 + ROW_CHUNK, (n_chunks - 1) * ROW_CHUNK), SUBLANES)
    hi = jnp.where(i < n_chunks - 1, ref[pl.ds(hi0, HALO), cols], 0.0)
    return jnp.concatenate([ref[pl.ds(t0, ROW_CHUNK), cols], hi], axis=0)


def _inv_count(t0, n_rows, window):
    pos = t0 + lax.broadcasted_iota(jnp.int32, (n_rows, 1), 0)
    return 1.0 / jnp.minimum(pos + 1, window).astype(F32)


def _pool_delta(z_hist, t0, window):
    s = z_hist
    sh = 1
    while sh < window:
        s = s + pltpu.roll(s, sh, 0)
        sh *= 2
    cur = z_hist[HALO:, :]
    return s[HALO:, :] * _inv_count(t0, ROW_CHUNK, window) - cur


def _pool_fwd(proj, wpg, pool_scale, lp):
    n_grp, gdim, _ = wpg.shape
    width = n_grp * gdim
    n_chunks = lp // ROW_CHUNK

    def body(z_ref, w_ref, sc_ref, out_ref):
        for g, window in enumerate(POOL_WINDOWS):
            cols = slice(g * gdim, (g + 1) * gdim)

            def chunk(i, carry, cols=cols, g=g, window=window):
                t0 = pl.multiple_of(i * ROW_CHUNK, ROW_CHUNK)
                d = _pool_delta(_chunk_with_history(z_ref, i, cols), t0, window)
                q = jnp.dot(d.astype(BF16), w_ref[g], preferred_element_type=F32)
                out_ref[pl.ds(t0, ROW_CHUNK), cols] = (q * sc_ref[:, cols]).astype(BF16)
                return carry

            lax.fori_loop(0, n_chunks, chunk, 0)

    return pl.pallas_call(
        body, name="pool_fwd", grid=(1,),
        in_specs=[pl.BlockSpec((lp, width), lambda i: (0, 0)),
                  pl.BlockSpec(wpg.shape, lambda i: (0, 0, 0)),
                  pl.BlockSpec(pool_scale.shape, lambda i: (0, 0))],
        out_specs=pl.BlockSpec((lp, width), lambda i: (0, 0)),
        out_shape=jax.ShapeDtypeStruct((lp, width), BF16),
        compiler_params=_cparams("arbitrary"),
    )(proj, wpg, pool_scale)


def _pool_bwd(proj, d_ya, wpg, pool_scale, lp):
    n_grp, gdim, _ = wpg.shape
    width = n_grp * gdim
    n_chunks = lp // ROW_CHUNK
    ext = ROW_CHUNK + HALO

    def body(z_ref, dya_ref, w_ref, sc_ref, dz_ref, dw_ref, dsc_ref):
        dw_ref[...] = jnp.zeros_like(dw_ref)
        dsc_ref[...] = jnp.zeros_like(dsc_ref)
        for g, window in enumerate(POOL_WINDOWS):
            cols = slice(g * gdim, (g + 1) * gdim)

            def chunk(i, carry, cols=cols, g=g, window=window):
                t0 = pl.multiple_of(i * ROW_CHUNK, ROW_CHUNK)
                w_g = w_ref[g]
                scale = sc_ref[:, cols]
                d = _pool_delta(_chunk_with_history(z_ref, i, cols), t0, window).astype(BF16)
                dya_ext = _chunk_with_future(dya_ref, i, n_chunks, cols)
                dya = dya_ext[:ROW_CHUNK, :]
                q = jnp.dot(d, w_g, preferred_element_type=F32)
                dsc_ref[:, cols] += _rowsum8(dya * q)
                e_ext = (dya_ext * scale).astype(BF16)
                dw_ref[g] += lax.dot_general(d, e_ext[:ROW_CHUNK, :], (((0,), (0,)), ((), ())),
                                             preferred_element_type=F32)
                dd_ext = lax.dot_general(e_ext, w_g, (((1,), (1,)), ((), ())), preferred_element_type=F32)
                s = dd_ext * _inv_count(t0, ext, window)
                sh = 1
                while sh < window:
                    s = s + pltpu.roll(s, ext - sh, 0)
                    sh *= 2
                dz_ref[pl.ds(t0, ROW_CHUNK), cols] = (s[:ROW_CHUNK, :] - dd_ext[:ROW_CHUNK, :]).astype(BF16)
                return carry

            lax.fori_loop(0, n_chunks, chunk, 0)

    blk = pl.BlockSpec((lp, width), lambda i: (0, 0))
    return pl.pallas_call(
        body, name="pool_bwd", grid=(1,),
        in_specs=[blk, blk, pl.BlockSpec(wpg.shape, lambda i: (0, 0, 0)),
                  pl.BlockSpec(pool_scale.shape, lambda i: (0, 0))],
        out_specs=[blk, pl.BlockSpec(wpg.shape, lambda i: (0, 0, 0)),
                   pl.BlockSpec((SUBLANES, width), lambda i: (0, 0))],
        out_shape=[jax.ShapeDtypeStruct((lp, width), BF16), jax.ShapeDtypeStruct(wpg.shape, F32),
                   jax.ShapeDtypeStruct((SUBLANES, width), F32)],
        compiler_params=_cparams("arbitrary"),
    )(proj, d_ya, wpg, pool_scale)


def _conv_fwd(proj, w_dw, b_dw, lp, width, v_col0):
    n_chunks = lp // ROW_CHUNK
    v_blk0, g_blk0 = v_col0 // LANES, (v_col0 + width) // LANES

    def body(v_ref, gc_ref, w_ref, b_ref, c_ref, a_pad):
        a_pad[pl.ds(0, HALO), :] = jnp.zeros((HALO, LANES), F32)
        a_pad[pl.ds(HALO, lp), :] = v_ref[...] * jax.nn.sigmoid(gc_ref[...])

        def chunk(i, carry):
            t0 = pl.multiple_of(i * ROW_CHUNK, ROW_CHUNK)
            hist = a_pad[pl.ds(t0, ROW_CHUNK + HALO), :]
            acc = jnp.zeros((ROW_CHUNK, LANES), F32)
            for k in range(CONV_KERNEL):
                acc = acc + w_ref[k:k + 1, :] * pltpu.roll(hist, CONV_KERNEL - 1 - k, 0)[HALO:, :]
            c_ref[pl.ds(t0, ROW_CHUNK), :] = acc + b_ref[...]
            return carry

        lax.fori_loop(0, n_chunks, chunk, 0)

    return pl.pallas_call(
        body, name="conv_fwd", grid=(width // LANES,),
        in_specs=[pl.BlockSpec((lp, LANES), lambda j: (0, j + v_blk0)),
                  pl.BlockSpec((lp, LANES), lambda j: (0, j + g_blk0)),
                  pl.BlockSpec((CONV_TAPS_PADDED, LANES), lambda j: (0, j)),
                  pl.BlockSpec((1, LANES), lambda j: (0, j))],
        out_specs=pl.BlockSpec((lp, LANES), lambda j: (0, j)),
        out_shape=jax.ShapeDtypeStruct((lp, width), F32),
        scratch_shapes=[pltpu.VMEM((lp + HALO, LANES), F32)],
        compiler_params=_cparams("parallel"),
    )(proj, proj, w_dw, b_dw)


def _conv_bwd(proj, dc, w_dw, lp, width, v_col0):
    n_chunks = lp // ROW_CHUNK
    ext = ROW_CHUNK + HALO
    v_blk0, g_blk0 = v_col0 // LANES, (v_col0 + width) // LANES

    def body(v_ref, gc_ref, dc_ref, w_ref, dv_ref, dgc_ref, dw_ref, a_pad, dc_pad, dw_acc):
        sig = jax.nn.sigmoid(gc_ref[...])
        a_pad[pl.ds(0, HALO), :] = jnp.zeros((HALO, LANES), F32)
        a_pad[pl.ds(HALO, lp), :] = v_ref[...] * sig
        dc_pad[pl.ds(0, lp), :] = dc_ref[...]
        dc_pad[pl.ds(lp, HALO), :] = jnp.zeros((HALO, LANES), F32)
        dw_acc[...] = jnp.zeros_like(dw_acc)

        def chunk(i, carry):
            t0 = pl.multiple_of(i * ROW_CHUNK, ROW_CHUNK)
            hist = a_pad[pl.ds(t0, ext), :]
            fut = dc_pad[pl.ds(t0, ext), :]
            dc_cur = fut[:ROW_CHUNK, :]
            da = jnp.zeros((ROW_CHUNK, LANES), F32)
            for k in range(CONV_KERNEL):
                lag = CONV_KERNEL - 1 - k
                da = da + w_ref[k:k + 1, :] * pltpu.roll(fut, (ext - lag) % ext, 0)[:ROW_CHUNK, :]
                dw_acc[pl.ds(SUBLANES * k, SUBLANES), :] += _rowsum8(dc_cur * pltpu.roll(hist, lag, 0)[HALO:, :])
            rows = pl.ds(t0, ROW_CHUNK)
            sg = jax.nn.sigmoid(gc_ref[rows, :])
            dv_ref[rows, :] = (da * sg).astype(BF16)
            dgc_ref[rows, :] = (da * v_ref[rows, :] * sg * (1.0 - sg)).astype(BF16)
            return carry

        lax.fori_loop(0, n_chunks, chunk, 0)
        dw_ref[...] = dw_acc[...].reshape(CONV_TAPS_PADDED, SUBLANES, LANES).sum(axis=1)

    col = lambda j: (0, j)
    return pl.pallas_call(
        body, name="conv_bwd", grid=(width // LANES,),
        in_specs=[pl.BlockSpec((lp, LANES), lambda j: (0, j + v_blk0)),
                  pl.BlockSpec((lp, LANES), lambda j: (0, j + g_blk0)),
                  pl.BlockSpec((lp, LANES), col),
                  pl.BlockSpec((CONV_TAPS_PADDED, LANES), col)],
        out_specs=[pl.BlockSpec((lp, LANES), col), pl.BlockSpec((lp, LANES), col),
                   pl.BlockSpec((CONV_TAPS_PADDED, LANES), col)],
        out_shape=[jax.ShapeDtypeStruct((lp, width), BF16), jax.ShapeDtypeStruct((lp, width), BF16),
                   jax.ShapeDtypeStruct((CONV_TAPS_PADDED, width), F32)],
        scratch_shapes=[pltpu.VMEM((lp + HALO, LANES), F32), pltpu.VMEM((lp + HALO, LANES), F32),
                        pltpu.VMEM((CONV_TAPS_PADDED * SUBLANES, LANES), F32)],
        compiler_params=_cparams("parallel"),
    )(proj, proj, dc, w_dw)


def _adamw_math(w, g, m, v):
    m = ADAM_B1 * m + (1.0 - ADAM_B1) * g
    v = ADAM_B2 * v + (1.0 - ADAM_B2) * jnp.square(g)
    m_hat = m / (1.0 - ADAM_B1 ** ADAM_STEP)
    v_hat = v / (1.0 - ADAM_B2 ** ADAM_STEP)
    delta = -ADAM_LR * (m_hat / (jnp.sqrt(v_hat) + ADAM_EPS) + ADAM_WD * w)
    return delta, m, v


def _pair_sum(own, recv, c_idx, name):
    _, _, rows, cols = own.shape
    tr = _row_tile(rows, cols, 1024 * 1024)

    def body(c_ref, own_ref, recv_ref, out_ref):
        out_ref[...] = (own_ref[...].astype(F32) + recv_ref[...].astype(F32)).astype(BF16)

    return pl.pallas_call(
        body, name=name,
        grid_spec=pltpu.PrefetchScalarGridSpec(
            num_scalar_prefetch=1, grid=(4, rows // tr),
            in_specs=[pl.BlockSpec((None, None, tr, cols), lambda q, i, c_ref: (q, c_ref[0], i, 0)),
                      pl.BlockSpec((None, tr, cols), lambda q, i, c_ref: (q, i, 0))],
            out_specs=pl.BlockSpec((None, tr, cols), lambda q, i, c_ref: (q, i, 0))),
        out_shape=jax.ShapeDtypeStruct((4, rows, cols), BF16),
        compiler_params=_cparams("parallel", "parallel"),
    )(c_idx, own, recv)


def _adamw_big(w, m, v, chip_sum, recv3, chip_idx, name):
    rows, cols = w.shape
    tr = _row_tile(rows, cols, 256 * 1024)

    def body(q_ref, w_ref, m_ref, v_ref, own_ref, r_ref, g_out, d_out, m_out, v_out):
        g = own_ref[...].astype(F32)
        for r in range(3):
            g = g + r_ref[r].astype(F32)
        delta, m_new, v_new = _adamw_math(w_ref[...], g, m_ref[...], v_ref[...])
        g_out[...] = g
        d_out[...] = delta
        m_out[...] = m_new
        v_out[...] = v_new

    blk = pl.BlockSpec((tr, cols), lambda i, q_ref: (i, 0))
    return pl.pallas_call(
        body, name=name,
        grid_spec=pltpu.PrefetchScalarGridSpec(
            num_scalar_prefetch=1, grid=(rows // tr,),
            in_specs=[blk, blk, blk,
                      pl.BlockSpec((None, tr, cols), lambda i, q_ref: (q_ref[0], i, 0)),
                      pl.BlockSpec((3, tr, cols), lambda i, q_ref: (0, i, 0))],
            out_specs=[blk] * 4),
        out_shape=[jax.ShapeDtypeStruct((rows, cols), F32)] * 4,
        compiler_params=_cparams("parallel"),
    )(chip_idx, w, m, v, chip_sum, recv3)


def _small_update(me_idx, rep_params, rep_grads, meta_wmv, meta_grads, wdw_wmv, wdw_grads, loss_all):
    n_rep = len(rep_params)
    meta_cols, wdw_cols = meta_wmv[0].shape[1], wdw_wmv[0].shape[1]

    def body(me_ref, *refs):
        pos = 0

        def take(k):
            nonlocal pos
            out = refs[pos:pos + k]
            pos += k
            return out

        rep_in = [take(3) for _ in range(n_rep)]
        rep_g = take(n_rep)
        meta_in, (meta_g,) = take(3), take(1)
        wdw_in, (wdw_g,) = take(3), take(1)
        (loss_ref,) = take(1)
        rep_out = [take(4) for _ in range(n_rep)]
        meta_out, wdw_out = take(4), take(4)
        (loss_out,) = take(1)

        def update(wmv, g, outs):
            delta, m_new, v_new = _adamw_math(wmv[0][...], g, wmv[1][...], wmv[2][...])
            for o_ref, val in zip(outs, (g, delta, m_new, v_new)):
                o_ref[...] = val

        for wmv, g_ref, outs in zip(rep_in, rep_g, rep_out):
            g = jnp.sum(g_ref[0], axis=0, keepdims=True)
            for j in range(1, N_DEV):
                g = g + jnp.sum(g_ref[j], axis=0, keepdims=True)
            update(wmv, g, outs)
        for wmv, g_ref, outs in ((meta_in, meta_g, meta_out), (wdw_in, wdw_g, wdw_out)):
            g = g_ref[0]
            for j in range(1, N_DEV):
                g = g + g_ref[j]
            update(wmv, g, outs)
        total = loss_ref[0]
        for j in range(1, N_DEV):
            total = total + loss_ref[j]
        loss_out[...] = total

    def whole(a):
        nd = a.ndim
        return pl.BlockSpec(a.shape, lambda i, me_ref, nd=nd: (0,) * nd)

    ins, in_specs = [], []
    for wmv in rep_params:
        ins += list(wmv)
        in_specs += [whole(a) for a in wmv]
    ins += list(rep_grads)
    in_specs += [whole(a) for a in rep_grads]
    ins += list(meta_wmv) + [meta_grads]
    in_specs += [whole(a) for a in meta_wmv]
    in_specs.append(pl.BlockSpec((N_DEV, meta_grads.shape[1], meta_cols), lambda i, me_ref: (0, 0, me_ref[0])))
    ins += list(wdw_wmv) + [wdw_grads]
    in_specs += [whole(a) for a in wdw_wmv]
    in_specs.append(pl.BlockSpec((N_DEV, wdw_grads.shape[1], wdw_cols), lambda i, me_ref: (0, 0, me_ref[0])))
    ins.append(loss_all)
    in_specs.append(whole(loss_all))

    out_shape, out_specs = [], []
    for wmv in list(rep_params) + [meta_wmv, wdw_wmv]:
        out_shape += [jax.ShapeDtypeStruct(wmv[0].shape, F32)] * 4
        out_specs += [whole(wmv[0])] * 4
    out_shape.append(jax.ShapeDtypeStruct(loss_all.shape[1:], F32))
    out_specs.append(pl.BlockSpec(loss_all.shape[1:], lambda i, me_ref: (0, 0)))

    outs = pl.pallas_call(
        body, name="small_update",
        grid_spec=pltpu.PrefetchScalarGridSpec(num_scalar_prefetch=1, grid=(1,), in_specs=in_specs,
                                               out_specs=out_specs),
        out_shape=out_shape, compiler_params=_cparams("arbitrary"),
    )(me_idx, *ins)
    groups = [outs[4 * i:4 * i + 4] for i in range(n_rep + 2)]
    return groups[:n_rep], groups[n_rep], groups[n_rep + 1], outs[-1]


def kernel(x, meta, g_pre_mix, w_in, w_pool_grp, pool_scale, w_pool_out, w_dw, b_dw, conv_ln_g, conv_ln_b, w_conv_out, w_o, g_post_mix, g_pre_mlp, w_up, w_down, g_post_mlp, loss_target, m_meta, m_g_pre_mix, m_w_in, m_w_pool_grp, m_pool_scale, m_w_pool_out, m_w_dw, m_b_dw, m_conv_ln_g, m_conv_ln_b, m_w_conv_out, m_w_o, m_g_post_mix, m_g_pre_mlp, m_w_up, m_w_down, m_g_post_mlp, v_meta, v_g_pre_mix, v_w_in, v_w_pool_grp, v_pool_scale, v_w_pool_out, v_w_dw, v_b_dw, v_conv_ln_g, v_conv_ln_b, v_w_conv_out, v_w_o, v_g_post_mix, v_g_pre_mlp, v_w_up, v_w_down, v_g_post_mlp):
    seq, d = x.shape[1], x.shape[2]
    pool_w = pool_scale.shape[1]
    conv_w = b_dw.shape[1]
    n_grp, grp_rows, gdim = w_pool_grp.shape[1:]
    lp = _round_up(N_META + seq, ROW_CHUNK)
    tm_half = lp // 2 if (lp // 2) % 16 == 0 else lp
    c_idx = lax.axis_index("c").astype(jnp.int32)
    chip_idx = (2 * lax.axis_index("x") + lax.axis_index("y")).astype(jnp.int32)
    me_idx = 2 * chip_idx + c_idx

    pad_taps = ((0, CONV_TAPS_PADDED - CONV_KERNEL), (0, 0))
    big = dict(w_in=w_in[0], w_pool_grp=w_pool_grp[0].reshape(n_grp * grp_rows, gdim), w_pool_out=w_pool_out[0],
               w_conv_out=w_conv_out[0], w_o=w_o[0], w_up=w_up[0], w_down=w_down[0])
    big_names = list(big)
    gathered = _all_gather([big[k].astype(BF16) for k in big_names] + [meta, jnp.pad(w_dw[0], pad_taps)],
                           "gather_weights")
    wg = dict(zip(big_names, gathered[:len(big_names)]))
    meta_all, wdw_all = gathered[-2:]
    meta_full = meta_all.transpose(1, 0, 2).reshape(N_META, d)
    wdw_full = wdw_all.transpose(1, 0, 2).reshape(CONV_TAPS_PADDED, conv_w)
    wpg_full = wg["w_pool_grp"].reshape(N_DEV, n_grp, grp_rows, gdim).transpose(1, 0, 2, 3).reshape(n_grp, gdim, gdim)
    w_o_full = wg["w_o"].reshape(d, d)
    w_down_full = wg["w_down"].reshape(-1, d)

    tail = lp - N_META - seq
    h0 = jnp.concatenate([meta_full, x[0], jnp.zeros((tail, d), F32)], axis=0)
    target = jnp.pad(loss_target[0], ((N_META, tail), (0, 0)))
    u1 = _rms_pre(h0, g_pre_mix, lp)
    proj = _mm(u1, wg["w_in"], mode="nn", tm=tm_half, tn=wg["w_in"].shape[2], b_blocked=True, name="mm_proj")
    ya_pre = _pool_fwd(proj, wpg_full, pool_scale, lp)
    conv_c = _conv_fwd(proj, wdw_full, b_dw, lp, conv_w, pool_w)
    s_act = _ln_silu(conv_c, conv_ln_g, conv_ln_b, lp)
    m_mix, y_a, y_b = _gate_mix(ya_pre, s_act, wg["w_pool_out"], wg["w_conv_out"], proj, d, "gate_mix")
    o = _mm(m_mix, w_o_full, mode="nn", tm=tm_half, tn=512, name="mm_o")
    h1, u2 = _post_mix(o, h0, g_post_mix, g_pre_mlp, lp)
    act = _mm(u2, wg["w_up"], mode="nn", tm=tm_half, tn=wg["w_up"].shape[2], b_blocked=True, out_dtypes=(BF16,),
              epilogue=lambda acc: (jnp.square(jnp.maximum(acc, 0.0)),), name="mm_up")
    f = _mm(act, w_down_full, mode="nn", tm=tm_half, tn=512, tk=2048, name="mm_down")

    df, dh, dg_post_mlp, loss_part = _loss_head(f, h1, target, g_post_mlp, lp, seq)
    d_up = _mm(df, w_down_full, mode="nt", tm=tm_half, tn=1024, out_dtypes=(BF16,), extras=[(act, 0)],
               epilogue=lambda acc, a: (acc * (2.0 * jnp.sqrt(a.astype(F32))),), name="mm_d_up")
    g_w_down = _mm(act, df, mode="tn", tm=1024, tn=1024, out_dtypes=(BF16,), name="mm_g_down")
    g_w_up = _mm(u2, d_up, mode="tn", tm=1024, tn=wg["w_up"].shape[2], out_blocked=True, out_dtypes=(BF16,),
                 name="mm_g_up")
    du2 = _mm(d_up, wg["w_up"], mode="nt", tm=tm_half, tn=1024, b_blocked=True, name="mm_du2")
    dh1, do, dg_pre_mlp, dg_post_mix = _mid_bwd(du2, h1, dh, o, g_pre_mlp, g_post_mix, lp)

    def gate_bwd(dm, ga, gb, ya, yb):
        sa, sb = jax.nn.sigmoid(ga), jax.nn.sigmoid(gb)
        return (dm * ya.astype(F32) * sa * (1.0 - sa), dm * yb.astype(F32) * sb * (1.0 - sb), dm * sa, dm * sb)

    gate_tn = 512
    d_ga, d_gb, d_ya, d_yb = _mm(
        do, w_o_full, mode="nt", tm=tm_half, tn=gate_tn, out_dtypes=(BF16,) * 4,
        extras=[(proj, (proj.shape[1] - 2 * d) // gate_tn), (proj, (proj.shape[1] - d) // gate_tn), (y_a, 0), (y_b, 0)],
        epilogue=gate_bwd, name="mm_dm")
    g_w_o = _mm(m_mix, do, mode="tn", tm=1024, tn=1024, out_dtypes=(BF16,), name="mm_g_o")
    bn_out = wg["w_pool_out"].shape[2]
    g_w_pool_out = _mm(ya_pre, d_ya, mode="tn", tm=pool_w, tn=bn_out, out_blocked=True, out_dtypes=(BF16,),
                       name="mm_g_pool_out")
    g_w_conv_out = _mm(s_act, d_yb, mode="tn", tm=conv_w, tn=bn_out, out_blocked=True, out_dtypes=(BF16,),
                       name="mm_g_conv_out")
    d_ya_pre = _mm(d_ya, wg["w_pool_out"], mode="nt", tm=tm_half, tn=pool_w, b_blocked=True, name="mm_d_ya_pre")
    d_s = _mm(d_yb, wg["w_conv_out"], mode="nt", tm=tm_half, tn=conv_w, b_blocked=True, name="mm_d_s")
    dz, g_wpg, d_scale = _pool_bwd(proj, d_ya_pre, wpg_full, pool_scale, lp)
    dc, d_ln_g, d_ln_b, d_b_dw = _ln_silu_bwd(conv_c, d_s, conv_ln_g, conv_ln_b, lp)
    dv, dgc, g_wdw = _conv_bwd(proj, dc, wdw_full, lp, conv_w, pool_w)
    dproj = jnp.concatenate([dz, dv, dgc, d_ga, d_gb], axis=1)
    g_w_in = _mm(u1, dproj, mode="tn", tm=1024, tn=wg["w_in"].shape[2], out_blocked=True, out_dtypes=(BF16,),
                 name="mm_g_in")
    du1 = _mm(dproj, wg["w_in"], mode="nt", tm=tm_half, tn=1024, b_blocked=True, name="mm_du1")
    dh0, dg_pre_mix = _pre_mix_bwd(du1, h0, dh1, g_pre_mix, lp)
    grad_x = dh0[N_META:N_META + seq][None]

    g_wpg_slots = g_wpg.astype(BF16).reshape(n_grp, N_DEV, grp_rows, gdim).transpose(1, 0, 2, 3)
    local = dict(
        w_in=g_w_in, w_pool_grp=g_wpg_slots.reshape(N_DEV, n_grp * grp_rows, gdim), w_pool_out=g_w_pool_out,
        w_conv_out=g_w_conv_out, w_o=g_w_o.reshape(N_DEV, -1, d), w_up=g_w_up, w_down=g_w_down.reshape(N_DEV, -1, d))
    slots = [local[k] for k in big_names]
    from_sibling = _pair_exchange(slots, "grads_to_sibling")
    chip_sums = [_pair_sum(g.reshape((4, 2) + g.shape[1:]), r, c_idx.reshape(1), "pair_sum_" + k)
                 for k, g, r in zip(big_names, slots, from_sibling)]
    from_chips = _chip_exchange(chip_sums, "grads_to_owner")
    moments = dict(w_in=(m_w_in, v_w_in), w_pool_grp=(m_w_pool_grp, v_w_pool_grp), w_pool_out=(m_w_pool_out, v_w_pool_out),
                   w_conv_out=(m_w_conv_out, v_w_conv_out), w_o=(m_w_o, v_w_o), w_up=(m_w_up, v_w_up),
                   w_down=(m_w_down, v_w_down))
    big_out = {}
    for k, cs, r3 in zip(big_names, chip_sums, from_chips):
        w2 = big[k]
        shape = moments[k][0].shape
        outs = _adamw_big(w2, moments[k][0].reshape(w2.shape), moments[k][1].reshape(w2.shape), cs, r3,
                          chip_idx.reshape(1), "adamw_" + k)
        big_out[k] = [a.reshape(shape) for a in outs]

    rep = dict(g_pre_mix=(g_pre_mix, m_g_pre_mix, v_g_pre_mix, dg_pre_mix),
               pool_scale=(pool_scale, m_pool_scale, v_pool_scale, d_scale),
               b_dw=(b_dw, m_b_dw, v_b_dw, d_b_dw),
               conv_ln_g=(conv_ln_g, m_conv_ln_g, v_conv_ln_g, d_ln_g),
               conv_ln_b=(conv_ln_b, m_conv_ln_b, v_conv_ln_b, d_ln_b),
               g_post_mix=(g_post_mix, m_g_post_mix, v_g_post_mix, dg_post_mix),
               g_pre_mlp=(g_pre_mlp, m_g_pre_mlp, v_g_pre_mlp, dg_pre_mlp),
               g_post_mlp=(g_post_mlp, m_g_post_mlp, v_g_post_mlp, dg_post_mlp))
    rep_names = list(rep)
    small_all = _all_gather([rep[k][3] for k in rep_names] + [dh0[:N_META], g_wdw, loss_part], "gather_small_grads")
    wdw_wmv = [jnp.pad(a[0], pad_taps) for a in (w_dw, m_w_dw, v_w_dw)]
    rep_out, meta_out, wdw_out, loss_blk = _small_update(
        me_idx.reshape(1), [rep[k][:3] for k in rep_names], small_all[:len(rep_names)],
        (meta, m_meta, v_meta), small_all[-3], wdw_wmv, small_all[-2], small_all[-1])
    small_out = dict(zip(rep_names, rep_out))
    small_out["meta"] = meta_out
    small_out["w_dw"] = [a[:CONV_KERNEL][None] for a in wdw_out]

    order = ["meta", "g_pre_mix", "w_in", "w_pool_grp", "pool_scale", "w_pool_out", "w_dw", "b_dw", "conv_ln_g",
             "conv_ln_b", "w_conv_out", "w_o", "g_post_mix", "g_pre_mlp", "w_up", "w_down", "g_post_mlp"]
    by_name = {**big_out, **small_out}
    result = [loss_blk[0, 0], grad_x]
    for kind in range(4):
        result += [by_name[k][kind] for k in order]
    return tuple(result)
```

```python
import functools

import jax
import jax.numpy as jnp
from jax import lax
from jax.experimental import pallas as pl
from jax.experimental.pallas import tpu as pltpu

F32 = jnp.float32
BF16 = jnp.bfloat16
MESH = pl.DeviceIdType.MESH

N_DEV = 8
N_META = 16
POOL_WINDOWS = (2, 4, 8, 16)
CONV_KERNEL = 31
CONV_TAPS_PADDED = 32
RMS_EPS = 1e-6
LN_EPS = 1e-5
ADAM_LR = 0.001
ADAM_B1 = 0.9
ADAM_B2 = 0.999
ADAM_EPS = 1e-08
ADAM_WD = 0.01
ADAM_STEP = 10

LANES = 128
SUBLANES = 8
ROW_CHUNK = 128
HALO = 32
VMEM_LIMIT_BYTES = 56 * 1024 * 1024


def _cparams(*sem):
    return pltpu.CompilerParams(dimension_semantics=sem if sem else None, vmem_limit_bytes=VMEM_LIMIT_BYTES)


def _round_up(n, m):
    return (n + m - 1) // m * m


def _row_tile(rows, cols, max_elems=640 * 1024):
    best = None
    for t in range(16, rows + 1, 16):
        if rows % t == 0 and (best is None or t * cols <= max_elems):
            best = t
    assert best is not None, (rows, cols)
    return best


def _rowsum8(a):
    t, w = a.shape
    return a.reshape(t // SUBLANES, SUBLANES, w).sum(axis=0)


def _mesh_pos():
    return lax.axis_index("x"), lax.axis_index("y"), lax.axis_index("c")


def _all_gather(shards, name):
    n = len(shards)

    def body(*refs):
        ins, outs = refs[:n], refs[n:2 * n]
        send_sems, recv_sems, local_sems = refs[2 * n:]
        x, y, c = _mesh_pos()
        me, sibling = (x, y, c), (x, y, 1 - c)
        chips = [(1 - x, y), (x, 1 - y), (1 - x, 1 - y)]

        def slot(p):
            return 4 * p[0] + 2 * p[1] + p[2]

        def copy(t, k, block, to, src=None):
            dst = outs[t].at[slot(block)]
            return pltpu.make_async_remote_copy(
                src_ref=dst if src is None else src, dst_ref=dst,
                send_sem=send_sems.at[t, k], recv_sem=recv_sems.at[t, k],
                device_id=to, device_id_type=MESH)

        mine = [pltpu.make_async_copy(ins[t], outs[t].at[slot(me)], local_sems.at[t]) for t in range(n)]
        for cp in mine:
            cp.start()
        first = []
        for t in range(n):
            first.append(copy(t, 0, me, sibling, src=ins[t]))
            first += [copy(t, 1 + j, me, (*chip, c), src=ins[t]) for j, chip in enumerate(chips)]
        for cp in first:
            cp.start()
        passed = []
        for t in range(n):
            for j, chip in enumerate(chips):
                copy(t, 1 + j, (*chip, c), me).wait_recv()
                fwd = copy(t, 4 + j, (*chip, c), sibling)
                fwd.start()
                passed.append(fwd)
        for t in range(n):
            copy(t, 0, sibling, me).wait_recv()
            for j, chip in enumerate(chips):
                copy(t, 4 + j, (*chip, 1 - c), me).wait_recv()
        for cp in first + passed:
            cp.wait_send()
        for cp in mine:
            cp.wait()

    any_spec = pl.BlockSpec(memory_space=pl.ANY)
    return pl.pallas_call(
        body, name=name,
        out_shape=[jax.ShapeDtypeStruct((N_DEV,) + s.shape, s.dtype) for s in shards],
        in_specs=[any_spec] * n, out_specs=[any_spec] * n,
        scratch_shapes=[pltpu.SemaphoreType.DMA((n, 7)), pltpu.SemaphoreType.DMA((n, 7)),
                        pltpu.SemaphoreType.DMA((n,))],
    )(*shards)


def _pair_exchange(grads, name):
    n = len(grads)

    def body(*refs):
        ins, outs = refs[:n], refs[n:2 * n]
        send_sems, recv_sems = refs[2 * n:]
        x, y, c = _mesh_pos()
        sibling = (x, y, 1 - c)
        copies = []
        for t in range(n):
            for q in range(4):
                copies.append(pltpu.make_async_remote_copy(
                    src_ref=ins[t].at[2 * q + 1 - c], dst_ref=outs[t].at[q],
                    send_sem=send_sems.at[t, q], recv_sem=recv_sems.at[t, q],
                    device_id=sibling, device_id_type=MESH))
        for cp in copies:
            cp.start()
        for cp in copies:
            cp.wait()

    any_spec = pl.BlockSpec(memory_space=pl.ANY)
    return pl.pallas_call(
        body, name=name,
        out_shape=[jax.ShapeDtypeStruct((4,) + g.shape[1:], g.dtype) for g in grads],
        in_specs=[any_spec] * n, out_specs=[any_spec] * n,
        scratch_shapes=[pltpu.SemaphoreType.DMA((n, 4)), pltpu.SemaphoreType.DMA((n, 4))],
    )(*grads)


def _chip_exchange(sums, name):
    n = len(sums)

    def body(*refs):
        ins, outs = refs[:n], refs[n:2 * n]
        send_sems, recv_sems = refs[2 * n:]
        x, y, c = _mesh_pos()
        chips = [(1 - x, y), (x, 1 - y), (1 - x, 1 - y)]
        copies = []
        for t in range(n):
            for r, chip in enumerate(chips):
                copies.append(pltpu.make_async_remote_copy(
                    src_ref=ins[t].at[2 * chip[0] + chip[1]], dst_ref=outs[t].at[r],
                    send_sem=send_sems.at[t, r], recv_sem=recv_sems.at[t, r],
                    device_id=(*chip, c), device_id_type=MESH))
        for cp in copies:
            cp.start()
        for cp in copies:
            cp.wait()

    any_spec = pl.BlockSpec(memory_space=pl.ANY)
    return pl.pallas_call(
        body, name=name,
        out_shape=[jax.ShapeDtypeStruct((3,) + s.shape[1:], s.dtype) for s in sums],
        in_specs=[any_spec] * n, out_specs=[any_spec] * n,
        scratch_shapes=[pltpu.SemaphoreType.DMA((n, 3)), pltpu.SemaphoreType.DMA((n, 3))],
    )(*sums)


HBM_SPEC = pl.BlockSpec(memory_space=pltpu.HBM)
SEM_SPEC = pl.BlockSpec(memory_space=pltpu.SEMAPHORE)
ANY_SPEC = pl.BlockSpec(memory_space=pl.ANY)
_DATAFLOW = pltpu.SideEffectType.DATAFLOW_SIDE_EFFECTING


def _hbm(a):
    return pltpu.with_memory_space_constraint(a, pltpu.HBM)


def _slot(p):
    return 4 * p[0] + 2 * p[1] + p[2]


def _gather_start(groups, name):
    flat = [pair for grp in groups for pair in grp]
    n, n_grp = len(flat), len(groups)

    def body(*refs):
        srcs, lands = refs[:n], refs[n:2 * n]
        sems = refs[2 * n:2 * n + 2 * n_grp]
        token = refs[-1]
        x, y, c = _mesh_pos()
        targets = [(x, y, 1 - c), (1 - x, y, c), (x, 1 - y, c), (1 - x, 1 - y, c)]
        t = 0
        for gi, grp in enumerate(groups):
            for ti in range(len(grp)):
                for k, to in enumerate(targets):
                    pltpu.make_async_remote_copy(
                        src_ref=srcs[t], dst_ref=lands[t].at[_slot((x, y, c))],
                        send_sem=sems[2 * gi].at[4 * ti + k], recv_sem=sems[2 * gi + 1].at[4 * ti + k],
                        device_id=to, device_id_type=MESH).start()
                t += 1
        token[...] = jnp.zeros_like(token)

    sem_shapes = []
    for grp in groups:
        sem_shapes += [pltpu.SemaphoreType.DMA((4 * len(grp),))] * 2
    thru = [pltpu.HBM(s.shape, s.dtype) for s, _ in flat] + [pltpu.HBM(g.shape, g.dtype) for _, g in flat]
    outs = pl.pallas_call(
        body, name=name,
        out_shape=tuple(sem_shapes + thru + [jax.ShapeDtypeStruct((SUBLANES, LANES), F32)]),
        in_specs=(HBM_SPEC,) * (2 * n),
        out_specs=tuple([SEM_SPEC] * (2 * n_grp) + [HBM_SPEC] * (2 * n) + [pl.BlockSpec(memory_space=pltpu.VMEM)]),
        input_output_aliases={i: 2 * n_grp + i for i in range(2 * n)},
        compiler_params=pltpu.CompilerParams(has_side_effects=_DATAFLOW),
    )(*[_hbm(s) for s, _ in flat], *[_hbm(g) for _, g in flat])
    sems, srcs, lands, token = outs[:2 * n_grp], outs[2 * n_grp:2 * n_grp + n], outs[2 * n_grp + n:-1], outs[-1]
    res, t = [], 0
    for gi, grp in enumerate(groups):
        res.append((sems[2 * gi], sems[2 * gi + 1], list(srcs[t:t + len(grp)]), list(lands[t:t + len(grp)])))
        t += len(grp)
    return res, token


def _gather_forward(lands, recv_sems, after, name):
    n = len(lands)

    def body(*refs):
        land_refs, recv, _ = refs[:n], refs[n], refs[n + 1]
        fsend, frecv = refs[n + 2], refs[n + 3]
        x, y, c = _mesh_pos()
        chips = [(1 - x, y), (x, 1 - y), (1 - x, 1 - y)]
        for t in range(n):
            for j, chip in enumerate(chips):
                blk = land_refs[t].at[_slot((*chip, c))]
                pltpu.make_async_remote_copy(src_ref=blk, dst_ref=blk, send_sem=fsend.at[3 * t + j],
                                             recv_sem=recv.at[4 * t + 1 + j],
                                             device_id=(x, y, 1 - c), device_id_type=MESH).wait_recv()
                pltpu.make_async_remote_copy(src_ref=blk, dst_ref=blk, send_sem=fsend.at[3 * t + j],
                                             recv_sem=frecv.at[3 * t + j],
                                             device_id=(x, y, 1 - c), device_id_type=MESH).start()

    outs = pl.pallas_call(
        body, name=name,
        out_shape=tuple([pltpu.SemaphoreType.DMA((3 * n,))] * 2 + [pltpu.HBM(g.shape, g.dtype) for g in lands]),
        in_specs=tuple([HBM_SPEC] * n + [SEM_SPEC, ANY_SPEC]),
        out_specs=tuple([SEM_SPEC] * 2 + [HBM_SPEC] * n),
        input_output_aliases={i: 2 + i for i in range(n)},
        compiler_params=pltpu.CompilerParams(has_side_effects=_DATAFLOW),
    )(*lands, recv_sems, after)
    return outs[0], outs[1], list(outs[2:])


def _gather_finish(srcs, lands, send_sems, recv_sems, fsend, frecv, after, name):
    n = len(lands)

    def body(*refs):
        src_refs, land_refs = refs[:n], refs[n:2 * n]
        send, recv, fs, fr = refs[2 * n:2 * n + 4]
        x, y, c = _mesh_pos()
        sibling = (x, y, 1 - c)
        chips = [(1 - x, y), (x, 1 - y), (1 - x, 1 - y)]

        def desc(src, dst, s_sem, r_sem):
            return pltpu.make_async_remote_copy(src_ref=src, dst_ref=dst, send_sem=s_sem, recv_sem=r_sem,
                                                device_id=sibling, device_id_type=MESH)

        for t in range(n):
            mine = land_refs[t].at[_slot((x, y, c))]
            desc(src_refs[t], land_refs[t].at[_slot(sibling)], send.at[4 * t], recv.at[4 * t]).wait_recv()
            for j, chip in enumerate(chips):
                got = land_refs[t].at[_slot((*chip, 1 - c))]
                desc(got, got, fs.at[3 * t + j], fr.at[3 * t + j]).wait_recv()
            for k in range(4):
                desc(src_refs[t], mine, send.at[4 * t + k], recv.at[4 * t + k]).wait_send()
            for j, chip in enumerate(chips):
                blk = land_refs[t].at[_slot((*chip, c))]
                desc(blk, blk, fs.at[3 * t + j], fr.at[3 * t + j]).wait_send()

    outs = pl.pallas_call(
        body, name=name,
        out_shape=tuple(pltpu.HBM(g.shape, g.dtype) for g in lands),
        in_specs=tuple([HBM_SPEC] * (2 * n) + [SEM_SPEC] * 4 + [ANY_SPEC]),
        out_specs=tuple([HBM_SPEC] * n),
        input_output_aliases={n + i: i for i in range(n)},
        compiler_params=pltpu.CompilerParams(has_side_effects=_DATAFLOW),
    )(*srcs, *lands, send_sems, recv_sems, fsend, frecv, after)
    return list(outs)


def _chip_exchange_start(sums, name):
    n = len(sums)

    def body(*refs):
        ins, lands = refs[:n], refs[n:2 * n]
        send, recv = refs[2 * n], refs[2 * n + 1]
        token = refs[-1]
        x, y, c = _mesh_pos()
        chips = [(1 - x, y), (x, 1 - y), (1 - x, 1 - y)]
        for t in range(n):
            for r, chip in enumerate(chips):
                pltpu.make_async_remote_copy(
                    src_ref=ins[t].at[2 * chip[0] + chip[1]], dst_ref=lands[t].at[r],
                    send_sem=send.at[3 * t + r], recv_sem=recv.at[3 * t + r],
                    device_id=(*chip, c), device_id_type=MESH).start()
        token[...] = jnp.zeros_like(token)

    land_shapes = [(3,) + s.shape[1:] for s in sums]
    outs = pl.pallas_call(
        body, name=name,
        out_shape=tuple([pltpu.SemaphoreType.DMA((3 * n,))] * 2 + [pltpu.HBM(s.shape, s.dtype) for s in sums]
                        + [pltpu.HBM(ls, s.dtype) for ls, s in zip(land_shapes, sums)]
                        + [jax.ShapeDtypeStruct((SUBLANES, LANES), F32)]),
        in_specs=(HBM_SPEC,) * (2 * n),
        out_specs=tuple([SEM_SPEC] * 2 + [HBM_SPEC] * (2 * n) + [pl.BlockSpec(memory_space=pltpu.VMEM)]),
        input_output_aliases={i: 2 + i for i in range(2 * n)},
        compiler_params=pltpu.CompilerParams(has_side_effects=_DATAFLOW),
    )(*[_hbm(s) for s in sums], *[_hbm(lax.empty(ls, s.dtype)) for ls, s in zip(land_shapes, sums)])
    return outs[0], outs[1], list(outs[2:2 + n]), list(outs[2 + n:2 + 2 * n]), outs[-1]


def _chip_exchange_finish(sums, lands, send_sems, recv_sems, after, name):
    n = len(sums)

    def body(*refs):
        ins, land_refs = refs[:n], refs[n:2 * n]
        send, recv = refs[2 * n], refs[2 * n + 1]
        x, y, c = _mesh_pos()
        for t in range(n):
            for r in range(3):
                cp = pltpu.make_async_remote_copy(
                    src_ref=ins[t].at[r], dst_ref=land_refs[t].at[r], send_sem=send.at[3 * t + r],
                    recv_sem=recv.at[3 * t + r],
                    device_id=(x, y, 1 - c), device_id_type=MESH)
                cp.wait_send()
                cp.wait_recv()

    outs = pl.pallas_call(
        body, name=name,
        out_shape=tuple(pltpu.HBM(g.shape, g.dtype) for g in lands),
        in_specs=tuple([HBM_SPEC] * (2 * n) + [SEM_SPEC] * 2 + [ANY_SPEC]),
        out_specs=tuple([HBM_SPEC] * n),
        input_output_aliases={n + i: i for i in range(n)},
        compiler_params=pltpu.CompilerParams(has_side_effects=_DATAFLOW),
    )(*sums, *lands, send_sems, recv_sems, after)
    return list(outs)


def _mm(a, b, *, mode, tm, tn, tk=None, b_blocked=False, out_blocked=False, out_dtypes=(F32,),
        epilogue=None, extras=(), after=None, name):
    if mode == "nn":
        m, k = a.shape
        n = b.shape[0] * b.shape[2] if b_blocked else b.shape[1]
        dims = (((1,), (0,)), ((), ()))
    elif mode == "nt":
        m, k = a.shape
        n = b.shape[1] if b_blocked else b.shape[0]
        if b_blocked:
            tk = b.shape[2]
        dims = (((1,), (1,)), ((), ()))
    else:
        k, m = a.shape
        n = b.shape[1]
        dims = (((0,), (0,)), ((), ()))
    tk = k if tk is None else tk
    assert m % tm == 0 and n % tn == 0 and k % tk == 0, (name, m, n, k, tm, tn, tk)
    gm, gn, gk = m // tm, n // tn, k // tk
    if b_blocked:
        assert (tn if mode == "nn" else tk) == b.shape[2], name

    if mode == "nn":
        a_spec = pl.BlockSpec((tm, tk), lambda i, j, kk: (i, kk))
        b_spec = (pl.BlockSpec((None, tk, tn), lambda i, j, kk: (j, kk, 0)) if b_blocked
                  else pl.BlockSpec((tk, tn), lambda i, j, kk: (kk, j)))
    elif mode == "nt":
        a_spec = pl.BlockSpec((tm, tk), lambda i, j, kk: (i, kk))
        b_spec = (pl.BlockSpec((None, tn, tk), lambda i, j, kk: (kk, j, 0)) if b_blocked
                  else pl.BlockSpec((tn, tk), lambda i, j, kk: (j, kk)))
    else:
        a_spec = pl.BlockSpec((tk, tm), lambda i, j, kk: (kk, i))
        b_spec = pl.BlockSpec((tk, tn), lambda i, j, kk: (kk, j))
    if out_blocked:
        out_spec = pl.BlockSpec((None, tm, tn), lambda i, j, kk: (j, i, 0))
        out_shape = (gn, m, tn)
    else:
        out_spec = pl.BlockSpec((tm, tn), lambda i, j, kk: (i, j))
        out_shape = (m, n)
    extra_specs = [pl.BlockSpec((tm, tn), functools.partial(lambda i, j, kk, off: (i, j + off), off=off))
                   for _, off in extras]
    n_extra, n_out = len(extras), len(out_dtypes)
    n_after = 0 if after is None else 1

    def body(a_ref, b_ref, *rest):
        extra_refs = rest[:n_extra]
        out_refs = rest[n_extra + n_after:n_extra + n_after + n_out]

        def finish(acc):
            if epilogue is None:
                res = (acc,)
            else:
                res = epilogue(acc, *[e[...] for e in extra_refs])
            for o_ref, r in zip(out_refs, res):
                o_ref[...] = r.astype(o_ref.dtype)

        part = lax.dot_general(a_ref[...], b_ref[...], dims, preferred_element_type=F32)
        if gk == 1:
            finish(part)
        else:
            acc_ref = rest[-1]
            kk = pl.program_id(2)

            @pl.when(kk == 0)
            def _():
                acc_ref[...] = part

            @pl.when(kk > 0)
            def _():
                acc_ref[...] += part

            @pl.when(kk == gk - 1)
            def _():
                finish(acc_ref[...])

    outs = pl.pallas_call(
        body, name=name, grid=(gm, gn, gk),
        in_specs=[a_spec, b_spec] + extra_specs + [ANY_SPEC] * n_after,
        out_specs=[out_spec] * n_out,
        out_shape=[jax.ShapeDtypeStruct(out_shape, d) for d in out_dtypes],
        scratch_shapes=[pltpu.VMEM((tm, tn), F32)] if gk > 1 else [],
        compiler_params=_cparams("parallel", "parallel", "arbitrary"),
    )(a, b, *[e for e, _ in extras], *([] if after is None else [after]))
    return outs[0] if n_out == 1 else outs


def _gate_mix(ya_pre, s, wpo, wco, proj, d_model, name):
    lp, width = ya_pre.shape
    nb, _, bn = wpo.shape
    ga_off = (proj.shape[1] - 2 * d_model) // bn
    gb_off = (proj.shape[1] - d_model) // bn

    def body(ya_ref, s_ref, wpo_ref, wco_ref, ga_ref, gb_ref, m_ref, y_a_ref, y_b_ref):
        y_a = jnp.dot(ya_ref[...], wpo_ref[...], preferred_element_type=F32)
        y_b = jnp.dot(s_ref[...], wco_ref[...], preferred_element_type=F32)
        m = jax.nn.sigmoid(ga_ref[...]) * y_a + jax.nn.sigmoid(gb_ref[...]) * y_b
        m_ref[...] = m.astype(BF16)
        y_a_ref[...] = y_a.astype(BF16)
        y_b_ref[...] = y_b.astype(BF16)

    act_spec = pl.BlockSpec((lp, width), lambda j: (0, 0))
    w_spec = pl.BlockSpec((None, width, bn), lambda j: (j, 0, 0))
    out_spec = pl.BlockSpec((lp, bn), lambda j: (0, j))
    return pl.pallas_call(
        body, name=name, grid=(nb,),
        in_specs=[act_spec, act_spec, w_spec, w_spec,
                  pl.BlockSpec((lp, bn), lambda j: (0, j + ga_off)),
                  pl.BlockSpec((lp, bn), lambda j: (0, j + gb_off))],
        out_specs=[out_spec] * 3,
        out_shape=[jax.ShapeDtypeStruct((lp, nb * bn), BF16)] * 3,
        compiler_params=_cparams("parallel"),
    )(ya_pre, s, wpo, wco, proj, proj)


def _rms_stats(x):
    return lax.rsqrt(jnp.mean(x * x, axis=-1, keepdims=True) + RMS_EPS)


def _rms_bwd(x, g, dy):
    r = _rms_stats(x)
    nrm = x * r
    dn = dy * g
    dx = r * (dn - nrm * jnp.mean(dn * nrm, axis=-1, keepdims=True))
    return dx, dy * nrm


def _rowwise(body, ins, outs, accs, *, lp, name):
    tr = _row_tile(lp, max(a.shape[1] for a in ins))
    n_in, n_out, n_acc = len(ins), len(outs), len(accs)

    def kernel_body(*refs):
        i = pl.program_id(0)
        acc_refs = refs[n_in + n_out:]

        @pl.when(i == 0)
        def _():
            for r in acc_refs:
                r[...] = jnp.zeros_like(r)

        body(i * tr, refs[:n_in], refs[n_in:n_in + n_out], acc_refs)

    in_specs = []
    for a in ins:
        if a.shape[0] == lp:
            in_specs.append(pl.BlockSpec((tr, a.shape[1]), lambda i: (i, 0)))
        else:
            in_specs.append(pl.BlockSpec(a.shape, lambda i: (0, 0)))
    out_specs = [pl.BlockSpec((tr, w), lambda i: (i, 0)) for w, _ in outs]
    out_specs += [pl.BlockSpec((SUBLANES, w), lambda i: (0, 0)) for w in accs]
    out_shape = [jax.ShapeDtypeStruct((lp, w), d) for w, d in outs]
    out_shape += [jax.ShapeDtypeStruct((SUBLANES, w), F32) for w in accs]
    return pl.pallas_call(
        kernel_body, name=name, grid=(lp // tr,), in_specs=in_specs, out_specs=out_specs,
        out_shape=out_shape, compiler_params=_cparams("arbitrary"),
    )(*ins)


def _rms_pre(h0, g, lp):
    d = h0.shape[1]

    def body(row0, ins, outs, accs):
        h_ref, g_ref = ins
        x = h_ref[...]
        outs[0][...] = (x * _rms_stats(x) * g_ref[...]).astype(BF16)

    return _rowwise(body, [h0, g], [(d, BF16)], [], lp=lp, name="rms_pre")[0]


def _post_mix(o, h0, g_post_mix, g_pre_mlp, lp):
    d = h0.shape[1]

    def body(row0, ins, outs, accs):
        o_ref, h0_ref, g1_ref, g2_ref = ins
        o_v = o_ref[...]
        h1 = h0_ref[...] + o_v * _rms_stats(o_v) * g1_ref[...]
        outs[0][...] = h1
        outs[1][...] = (h1 * _rms_stats(h1) * g2_ref[...]).astype(BF16)

    return _rowwise(body, [o, h0, g_post_mix, g_pre_mlp], [(d, F32), (d, BF16)], [], lp=lp, name="post_mix")


def _loss_head(f, h1, target, g_post_mlp, lp, seq):
    d = f.shape[1]

    def body(row0, ins, outs, accs):
        f_ref, h1_ref, t_ref, g_ref = ins
        df_ref, dh_ref = outs
        dg_ref, loss_ref = accs
        f_v, g = f_ref[...], g_ref[...]
        r = _rms_stats(f_v)
        nrm = f_v * r
        rows = row0 + lax.broadcasted_iota(jnp.int32, (f_v.shape[0], 1), 0)
        valid = (rows >= N_META) & (rows < N_META + seq)
        err = jnp.where(valid, h1_ref[...] + nrm * g - t_ref[...], 0.0)
        loss_ref[...] += 0.5 * jnp.sum(jnp.mean(err * err, axis=-1, keepdims=True))
        dy = err * (1.0 / d)
        dn = dy * g
        df_ref[...] = (r * (dn - nrm * jnp.mean(dn * nrm, axis=-1, keepdims=True))).astype(BF16)
        dh_ref[...] = dy
        dg_ref[...] += _rowsum8(dy * nrm)

    return _rowwise(body, [f, h1, target, g_post_mlp], [(d, BF16), (d, F32)], [d, LANES], lp=lp, name="loss_head")


def _mid_bwd(du2, h1, dh, o, g_pre_mlp, g_post_mix, lp):
    d = h1.shape[1]

    def body(row0, ins, outs, accs):
        du2_ref, h1_ref, dh_ref, o_ref, g2_ref, g1_ref = ins
        dx2, dg2 = _rms_bwd(h1_ref[...], g2_ref[...], du2_ref[...])
        dh1 = dh_ref[...] + dx2
        do, dg1 = _rms_bwd(o_ref[...], g1_ref[...], dh1)
        outs[0][...] = dh1
        outs[1][...] = do.astype(BF16)
        accs[0][...] += _rowsum8(dg2)
        accs[1][...] += _rowsum8(dg1)

    return _rowwise(body, [du2, h1, dh, o, g_pre_mlp, g_post_mix], [(d, F32), (d, BF16)], [d, d], lp=lp,
                    name="mid_bwd")


def _pre_mix_bwd(du1, h0, dh1, g_pre_mix, lp):
    d = h0.shape[1]

    def body(row0, ins, outs, accs):
        du1_ref, h0_ref, dh1_ref, g_ref = ins
        dx, dg = _rms_bwd(h0_ref[...], g_ref[...], du1_ref[...])
        outs[0][...] = dh1_ref[...] + dx
        accs[0][...] += _rowsum8(dg)

    return _rowwise(body, [du1, h0, dh1, g_pre_mix], [(d, F32)], [d], lp=lp, name="pre_mix_bwd")


def _ln_stats(c):
    mu = jnp.mean(c, axis=-1, keepdims=True)
    var = jnp.mean(jnp.square(c - mu), axis=-1, keepdims=True)
    return mu, lax.rsqrt(var + LN_EPS)


def _ln_silu(c, ln_g, ln_b, lp):
    w = c.shape[1]

    def body(row0, ins, outs, accs):
        c_ref, g_ref, b_ref = ins
        c_v = c_ref[...]
        mu, rstd = _ln_stats(c_v)
        ln = (c_v - mu) * rstd * g_ref[...] + b_ref[...]
        outs[0][...] = (ln * jax.nn.sigmoid(ln)).astype(BF16)

    return _rowwise(body, [c, ln_g, ln_b], [(w, BF16)], [], lp=lp, name="ln_silu")[0]


def _ln_silu_bwd(c, ds, ln_g, ln_b, lp):
    w = c.shape[1]

    def body(row0, ins, outs, accs):
        c_ref, ds_ref, g_ref, b_ref = ins
        c_v, g = c_ref[...], g_ref[...]
        mu, rstd = _ln_stats(c_v)
        nrm = (c_v - mu) * rstd
        ln = nrm * g + b_ref[...]
        sig = jax.nn.sigmoid(ln)
        dln = ds_ref[...] * (sig * (1.0 + ln * (1.0 - sig)))
        dn = dln * g
        dc = rstd * (dn - jnp.mean(dn, axis=-1, keepdims=True) - nrm * jnp.mean(dn * nrm, axis=-1, keepdims=True))
        outs[0][...] = dc
        accs[0][...] += _rowsum8(dln * nrm)
        accs[1][...] += _rowsum8(dln)
        accs[2][...] += _rowsum8(dc)

    return _rowwise(body, [c, ds, ln_g, ln_b], [(w, F32)], [w, w, w], lp=lp, name="ln_silu_bwd")


def _chunk_with_history(ref, i, cols=slice(None)):
    t0 = pl.multiple_of(i * ROW_CHUNK, ROW_CHUNK)
    lo0 = pl.multiple_of(jnp.maximum(t0 - HALO, 0), SUBLANES)
    lo = jnp.where(i > 0, ref[pl.ds(lo0, HALO), cols], 0.0)
    return jnp.concatenate([lo, ref[pl.ds(t0, ROW_CHUNK), cols]], axis=0)


def _chunk_with_future(ref, i, n_chunks, cols=slice(None)):
    t0 = pl.multiple_of(i * ROW_CHUNK, ROW_CHUNK)
    hi0 = pl.multiple_of(jnp.minimum(t0 + ROW_CHUNK, (n_chunks - 1) * ROW_CHUNK), SUBLANES)
    hi = jnp.where(i < n_chunks - 1, ref[pl.ds(hi0, HALO), cols], 0.0)
    return jnp.concatenate([ref[pl.ds(t0, ROW_CHUNK), cols], hi], axis=0)


def _inv_count(t0, n_rows, window):
    pos = t0 + lax.broadcasted_iota(jnp.int32, (n_rows, 1), 0)
    return 1.0 / jnp.minimum(pos + 1, window).astype(F32)


def _pool_delta(z_hist, t0, window):
    s = z_hist
    sh = 1
    while sh < window:
        s = s + pltpu.roll(s, sh, 0)
        sh *= 2
    cur = z_hist[HALO:, :]
    return s[HALO:, :] * _inv_count(t0, ROW_CHUNK, window) - cur


def _pool_fwd(proj, wpg, pool_scale, lp):
    n_grp, gdim, _ = wpg.shape
    width = n_grp * gdim
    n_chunks = lp // ROW_CHUNK

    def body(z_ref, w_ref, sc_ref, out_ref):
        for g, window in enumerate(POOL_WINDOWS):
            cols = slice(g * gdim, (g + 1) * gdim)

            def chunk(i, carry, cols=cols, g=g, window=window):
                t0 = pl.multiple_of(i * ROW_CHUNK, ROW_CHUNK)
                d = _pool_delta(_chunk_with_history(z_ref, i, cols), t0, window)
                q = jnp.dot(d.astype(BF16), w_ref[g], preferred_element_type=F32)
                out_ref[pl.ds(t0, ROW_CHUNK), cols] = (q * sc_ref[:, cols]).astype(BF16)
                return carry

            lax.fori_loop(0, n_chunks, chunk, 0)

    return pl.pallas_call(
        body, name="pool_fwd", grid=(1,),
        in_specs=[pl.BlockSpec((lp, width), lambda i: (0, 0)),
                  pl.BlockSpec(wpg.shape, lambda i: (0, 0, 0)),
                  pl.BlockSpec(pool_scale.shape, lambda i: (0, 0))],
        out_specs=pl.BlockSpec((lp, width), lambda i: (0, 0)),
        out_shape=jax.ShapeDtypeStruct((lp, width), BF16),
        compiler_params=_cparams("arbitrary"),
    )(proj, wpg, pool_scale)


def _pool_bwd(proj, d_ya, wpg, pool_scale, lp):
    n_grp, gdim, _ = wpg.shape
    width = n_grp * gdim
    n_chunks = lp // ROW_CHUNK
    ext = ROW_CHUNK + HALO

    def body(z_ref, dya_ref, w_ref, sc_ref, dz_ref, dw_ref, dsc_ref):
        dw_ref[...] = jnp.zeros_like(dw_ref)
        dsc_ref[...] = jnp.zeros_like(dsc_ref)
        for g, window in enumerate(POOL_WINDOWS):
            cols = slice(g * gdim, (g + 1) * gdim)

            def chunk(i, carry, cols=cols, g=g, window=window):
                t0 = pl.multiple_of(i * ROW_CHUNK, ROW_CHUNK)
                w_g = w_ref[g]
                scale = sc_ref[:, cols]
                d = _pool_delta(_chunk_with_history(z_ref, i, cols), t0, window).astype(BF16)
                dya_ext = _chunk_with_future(dya_ref, i, n_chunks, cols)
                dya = dya_ext[:ROW_CHUNK, :]
                q = jnp.dot(d, w_g, preferred_element_type=F32)
                dsc_ref[:, cols] += _rowsum8(dya * q)
                e_ext = (dya_ext * scale).astype(BF16)
                dw_ref[g] += lax.dot_general(d, e_ext[:ROW_CHUNK, :], (((0,), (0,)), ((), ())),
                                             preferred_element_type=F32)
                dd_ext = lax.dot_general(e_ext, w_g, (((1,), (1,)), ((), ())), preferred_element_type=F32)
                s = dd_ext * _inv_count(t0, ext, window)
                sh = 1
                while sh < window:
                    s = s + pltpu.roll(s, ext - sh, 0)
                    sh *= 2
                dz_ref[pl.ds(t0, ROW_CHUNK), cols] = (s[:ROW_CHUNK, :] - dd_ext[:ROW_CHUNK, :]).astype(BF16)
                return carry

            lax.fori_loop(0, n_chunks, chunk, 0)

    blk = pl.BlockSpec((lp, width), lambda i: (0, 0))
    return pl.pallas_call(
        body, name="pool_bwd", grid=(1,),
        in_specs=[blk, blk, pl.BlockSpec(wpg.shape, lambda i: (0, 0, 0)),
                  pl.BlockSpec(pool_scale.shape, lambda i: (0, 0))],
        out_specs=[blk, pl.BlockSpec(wpg.shape, lambda i: (0, 0, 0)),
                   pl.BlockSpec((SUBLANES, width), lambda i: (0, 0))],
        out_shape=[jax.ShapeDtypeStruct((lp, width), BF16), jax.ShapeDtypeStruct(wpg.shape, F32),
                   jax.ShapeDtypeStruct((SUBLANES, width), F32)],
        compiler_params=_cparams("arbitrary"),
    )(proj, d_ya, wpg, pool_scale)


def _conv_fwd(proj, w_dw, b_dw, lp, width, v_col0):
    n_chunks = lp // ROW_CHUNK
    v_blk0, g_blk0 = v_col0 // LANES, (v_col0 + width) // LANES

    def body(v_ref, gc_ref, w_ref, b_ref, c_ref, a_pad):
        a_pad[pl.ds(0, HALO), :] = jnp.zeros((HALO, LANES), F32)
        a_pad[pl.ds(HALO, lp), :] = v_ref[...] * jax.nn.sigmoid(gc_ref[...])

        def chunk(i, carry):
            t0 = pl.multiple_of(i * ROW_CHUNK, ROW_CHUNK)
            hist = a_pad[pl.ds(t0, ROW_CHUNK + HALO), :]
            acc = jnp.zeros((ROW_CHUNK, LANES), F32)
            for k in range(CONV_KERNEL):
                acc = acc + w_ref[k:k + 1, :] * pltpu.roll(hist, CONV_KERNEL - 1 - k, 0)[HALO:, :]
            c_ref[pl.ds(t0, ROW_CHUNK), :] = acc + b_ref[...]
            return carry

        lax.fori_loop(0, n_chunks, chunk, 0)

    return pl.pallas_call(
        body, name="conv_fwd", grid=(width // LANES,),
        in_specs=[pl.BlockSpec((lp, LANES), lambda j: (0, j + v_blk0)),
                  pl.BlockSpec((lp, LANES), lambda j: (0, j + g_blk0)),
                  pl.BlockSpec((CONV_TAPS_PADDED, LANES), lambda j: (0, j)),
                  pl.BlockSpec((1, LANES), lambda j: (0, j))],
        out_specs=pl.BlockSpec((lp, LANES), lambda j: (0, j)),
        out_shape=jax.ShapeDtypeStruct((lp, width), F32),
        scratch_shapes=[pltpu.VMEM((lp + HALO, LANES), F32)],
        compiler_params=_cparams("parallel"),
    )(proj, proj, w_dw, b_dw)


def _conv_bwd(proj, dc, w_dw, lp, width, v_col0):
    n_chunks = lp // ROW_CHUNK
    ext = ROW_CHUNK + HALO
    v_blk0, g_blk0 = v_col0 // LANES, (v_col0 + width) // LANES

    def body(v_ref, gc_ref, dc_ref, w_ref, dv_ref, dgc_ref, dw_ref, a_pad, dc_pad, dw_acc):
        sig = jax.nn.sigmoid(gc_ref[...])
        a_pad[pl.ds(0, HALO), :] = jnp.zeros((HALO, LANES), F32)
        a_pad[pl.ds(HALO, lp), :] = v_ref[...] * sig
        dc_pad[pl.ds(0, lp), :] = dc_ref[...]
        dc_pad[pl.ds(lp, HALO), :] = jnp.zeros((HALO, LANES), F32)
        dw_acc[...] = jnp.zeros_like(dw_acc)

        def chunk(i, carry):
            t0 = pl.multiple_of(i * ROW_CHUNK, ROW_CHUNK)
            hist = a_pad[pl.ds(t0, ext), :]
            fut = dc_pad[pl.ds(t0, ext), :]
            dc_cur = fut[:ROW_CHUNK, :]
            da = jnp.zeros((ROW_CHUNK, LANES), F32)
            for k in range(CONV_KERNEL):
                lag = CONV_KERNEL - 1 - k
                da = da + w_ref[k:k + 1, :] * pltpu.roll(fut, (ext - lag) % ext, 0)[:ROW_CHUNK, :]
                dw_acc[pl.ds(SUBLANES * k, SUBLANES), :] += _rowsum8(dc_cur * pltpu.roll(hist, lag, 0)[HALO:, :])
            rows = pl.ds(t0, ROW_CHUNK)
            sg = jax.nn.sigmoid(gc_ref[rows, :])
            dv_ref[rows, :] = (da * sg).astype(BF16)
            dgc_ref[rows, :] = (da * v_ref[rows, :] * sg * (1.0 - sg)).astype(BF16)
            return carry

        lax.fori_loop(0, n_chunks, chunk, 0)
        dw_ref[...] = dw_acc[...].reshape(CONV_TAPS_PADDED, SUBLANES, LANES).sum(axis=1)

    col = lambda j: (0, j)
    return pl.pallas_call(
        body, name="conv_bwd", grid=(width // LANES,),
        in_specs=[pl.BlockSpec((lp, LANES), lambda j: (0, j + v_blk0)),
                  pl.BlockSpec((lp, LANES), lambda j: (0, j + g_blk0)),
                  pl.BlockSpec((lp, LANES), col),
                  pl.BlockSpec((CONV_TAPS_PADDED, LANES), col)],
        out_specs=[pl.BlockSpec((lp, LANES), col), pl.BlockSpec((lp, LANES), col),
                   pl.BlockSpec((CONV_TAPS_PADDED, LANES), col)],
        out_shape=[jax.ShapeDtypeStruct((lp, width), BF16), jax.ShapeDtypeStruct((lp, width), BF16),
                   jax.ShapeDtypeStruct((CONV_TAPS_PADDED, width), F32)],
        scratch_shapes=[pltpu.VMEM((lp + HALO, LANES), F32), pltpu.VMEM((lp + HALO, LANES), F32),
                        pltpu.VMEM((CONV_TAPS_PADDED * SUBLANES, LANES), F32)],
        compiler_params=_cparams("parallel"),
    )(proj, proj, dc, w_dw)


def _adamw_math(w, g, m, v):
    m = ADAM_B1 * m + (1.0 - ADAM_B1) * g
    v = ADAM_B2 * v + (1.0 - ADAM_B2) * jnp.square(g)
    m_hat = m / (1.0 - ADAM_B1 ** ADAM_STEP)
    v_hat = v / (1.0 - ADAM_B2 ** ADAM_STEP)
    delta = -ADAM_LR * (m_hat / (jnp.sqrt(v_hat) + ADAM_EPS) + ADAM_WD * w)
    return delta, m, v


def _pair_sum(own, recv, c_idx, name):
    _, _, rows, cols = own.shape
    tr = _row_tile(rows, cols, 1024 * 1024)

    def body(c_ref, own_ref, recv_ref, out_ref):
        out_ref[...] = (own_ref[...].astype(F32) + recv_ref[...].astype(F32)).astype(BF16)

    return pl.pallas_call(
        body, name=name,
        grid_spec=pltpu.PrefetchScalarGridSpec(
            num_scalar_prefetch=1, grid=(4, rows // tr),
            in_specs=[pl.BlockSpec((None, None, tr, cols), lambda q, i, c_ref: (q, c_ref[0], i, 0)),
                      pl.BlockSpec((None, tr, cols), lambda q, i, c_ref: (q, i, 0))],
            out_specs=pl.BlockSpec((None, tr, cols), lambda q, i, c_ref: (q, i, 0))),
        out_shape=jax.ShapeDtypeStruct((4, rows, cols), BF16),
        compiler_params=_cparams("parallel", "parallel"),
    )(c_idx, own, recv)


def _adamw_big(w, m, v, chip_sum, recv3, chip_idx, name):
    rows, cols = w.shape
    tr = _row_tile(rows, cols, 256 * 1024)

    def body(q_ref, w_ref, m_ref, v_ref, own_ref, r_ref, g_out, d_out, m_out, v_out):
        g = own_ref[...].astype(F32)
        for r in range(3):
            g = g + r_ref[r].astype(F32)
        delta, m_new, v_new = _adamw_math(w_ref[...], g, m_ref[...], v_ref[...])
        g_out[...] = g
        d_out[...] = delta
        m_out[...] = m_new
        v_out[...] = v_new

    blk = pl.BlockSpec((tr, cols), lambda i, q_ref: (i, 0))
    return pl.pallas_call(
        body, name=name,
        grid_spec=pltpu.PrefetchScalarGridSpec(
            num_scalar_prefetch=1, grid=(rows // tr,),
            in_specs=[blk, blk, blk,
                      pl.BlockSpec((None, tr, cols), lambda i, q_ref: (q_ref[0], i, 0)),
                      pl.BlockSpec((3, tr, cols), lambda i, q_ref: (0, i, 0))],
            out_specs=[blk] * 4),
        out_shape=[jax.ShapeDtypeStruct((rows, cols), F32)] * 4,
        compiler_params=_cparams("parallel"),
    )(chip_idx, w, m, v, chip_sum, recv3)


def _small_update(me_idx, rep_params, rep_grads, meta_wmv, meta_grads, wdw_wmv, wdw_grads, loss_all):
    n_rep = len(rep_params)
    meta_cols, wdw_cols = meta_wmv[0].shape[1], wdw_wmv[0].shape[1]

    def body(me_ref, *refs):
        pos = 0

        def take(k):
            nonlocal pos
            out = refs[pos:pos + k]
            pos += k
            return out

        rep_in = [take(3) for _ in range(n_rep)]
        rep_g = take(n_rep)
        meta_in, (meta_g,) = take(3), take(1)
        wdw_in, (wdw_g,) = take(3), take(1)
        (loss_ref,) = take(1)
        rep_out = [take(4) for _ in range(n_rep)]
        meta_out, wdw_out = take(4), take(4)
        (loss_out,) = take(1)

        def update(wmv, g, outs):
            delta, m_new, v_new = _adamw_math(wmv[0][...], g, wmv[1][...], wmv[2][...])
            for o_ref, val in zip(outs, (g, delta, m_new, v_new)):
                o_ref[...] = val

        for wmv, g_ref, outs in zip(rep_in, rep_g, rep_out):
            g = jnp.sum(g_ref[0], axis=0, keepdims=True)
            for j in range(1, N_DEV):
                g = g + jnp.sum(g_ref[j], axis=0, keepdims=True)
            update(wmv, g, outs)
        for wmv, g_ref, outs in ((meta_in, meta_g, meta_out), (wdw_in, wdw_g, wdw_out)):
            g = g_ref[0]
            for j in range(1, N_DEV):
                g = g + g_ref[j]
            update(wmv, g, outs)
        total = loss_ref[0]
        for j in range(1, N_DEV):
            total = total + loss_ref[j]
        loss_out[...] = total

    def whole(a):
        nd = a.ndim
        return pl.BlockSpec(a.shape, lambda i, me_ref, nd=nd: (0,) * nd)

    ins, in_specs = [], []
    for wmv in rep_params:
        ins += list(wmv)
        in_specs += [whole(a) for a in wmv]
    ins += list(rep_grads)
    in_specs += [whole(a) for a in rep_grads]
    ins += list(meta_wmv) + [meta_grads]
    in_specs += [whole(a) for a in meta_wmv]
    in_specs.append(pl.BlockSpec((N_DEV, meta_grads.shape[1], meta_cols), lambda i, me_ref: (0, 0, me_ref[0])))
    ins += list(wdw_wmv) + [wdw_grads]
    in_specs += [whole(a) for a in wdw_wmv]
    in_specs.append(pl.BlockSpec((N_DEV, wdw_grads.shape[1], wdw_cols), lambda i, me_ref: (0, 0, me_ref[0])))
    ins.append(loss_all)
    in_specs.append(whole(loss_all))

    out_shape, out_specs = [], []
    for wmv in list(rep_params) + [meta_wmv, wdw_wmv]:
        out_shape += [jax.ShapeDtypeStruct(wmv[0].shape, F32)] * 4
        out_specs += [whole(wmv[0])] * 4
    out_shape.append(jax.ShapeDtypeStruct(loss_all.shape[1:], F32))
    out_specs.append(pl.BlockSpec(loss_all.shape[1:], lambda i, me_ref: (0, 0)))

    outs = pl.pallas_call(
        body, name="small_update",
        grid_spec=pltpu.PrefetchScalarGridSpec(num_scalar_prefetch=1, grid=(1,), in_specs=in_specs,
                                               out_specs=out_specs),
        out_shape=out_shape, compiler_params=_cparams("arbitrary"),
    )(me_idx, *ins)
    groups = [outs[4 * i:4 * i + 4] for i in range(n_rep + 2)]
    return groups[:n_rep], groups[n_rep], groups[n_rep + 1], outs[-1]


def kernel(x, meta, g_pre_mix, w_in, w_pool_grp, pool_scale, w_pool_out, w_dw, b_dw, conv_ln_g, conv_ln_b, w_conv_out, w_o, g_post_mix, g_pre_mlp, w_up, w_down, g_post_mlp, loss_target, m_meta, m_g_pre_mix, m_w_in, m_w_pool_grp, m_pool_scale, m_w_pool_out, m_w_dw, m_b_dw, m_conv_ln_g, m_conv_ln_b, m_w_conv_out, m_w_o, m_g_post_mix, m_g_pre_mlp, m_w_up, m_w_down, m_g_post_mlp, v_meta, v_g_pre_mix, v_w_in, v_w_pool_grp, v_pool_scale, v_w_pool_out, v_w_dw, v_b_dw, v_conv_ln_g, v_conv_ln_b, v_w_conv_out, v_w_o, v_g_post_mix, v_g_pre_mlp, v_w_up, v_w_down, v_g_post_mlp):
    seq, d = x.shape[1], x.shape[2]
    pool_w = pool_scale.shape[1]
    conv_w = b_dw.shape[1]
    n_grp, grp_rows, gdim = w_pool_grp.shape[1:]
    lp = _round_up(N_META + seq, ROW_CHUNK)
    tm_half = lp // 2 if (lp // 2) % 16 == 0 else lp
    c_idx = lax.axis_index("c").astype(jnp.int32)
    chip_idx = (2 * lax.axis_index("x") + lax.axis_index("y")).astype(jnp.int32)
    me_idx = 2 * chip_idx + c_idx

    pad_taps = ((0, CONV_TAPS_PADDED - CONV_KERNEL), (0, 0))
    big = dict(w_in=w_in[0], w_pool_grp=w_pool_grp[0].reshape(n_grp * grp_rows, gdim), w_pool_out=w_pool_out[0],
               w_conv_out=w_conv_out[0], w_o=w_o[0], w_up=w_up[0], w_down=w_down[0])
    big_names = list(big)
    moments = dict(w_in=(m_w_in, v_w_in), w_pool_grp=(m_w_pool_grp, v_w_pool_grp), w_pool_out=(m_w_pool_out, v_w_pool_out),
                   w_conv_out=(m_w_conv_out, v_w_conv_out), w_o=(m_w_o, v_w_o), w_up=(m_w_up, v_w_up),
                   w_down=(m_w_down, v_w_down))
    shard = {k: big[k].astype(BF16) for k in big_names}
    shard["meta"] = meta
    shard["w_dw"] = jnp.pad(w_dw[0], pad_taps)
    gather_groups = [["w_in", "meta", "w_dw"], ["w_pool_grp", "w_pool_out", "w_conv_out", "w_o"], ["w_up"], ["w_down"]]

    def landing(s):
        return lax.dynamic_update_index_in_dim(lax.empty((N_DEV,) + s.shape, s.dtype), s, me_idx, 0)

    started, gather_token = _gather_start([[(shard[k], landing(shard[k])) for k in names] for names in gather_groups],
                                          "gather_start")
    wg = {}

    def pass_on(gi, after):
        send, recv, srcs, lands = started[gi]
        fsend, frecv, lands = _gather_forward(lands, recv, after, f"gather_forward_{gi}")
        return send, recv, srcs, lands, fsend, frecv

    def complete(gi, handle, after):
        send, recv, srcs, lands, fsend, frecv = handle
        wg.update(zip(gather_groups[gi], _gather_finish(srcs, lands, send, recv, fsend, frecv, after,
                                                        f"gather_finish_{gi}")))

    complete(0, pass_on(0, gather_token), gather_token)
    meta_full = wg["meta"].transpose(1, 0, 2).reshape(N_META, d)
    wdw_full = wg["w_dw"].transpose(1, 0, 2).reshape(CONV_TAPS_PADDED, conv_w)
    tail = lp - N_META - seq
    h0 = jnp.concatenate([meta_full, x[0], jnp.zeros((tail, d), F32)], axis=0)
    target = jnp.pad(loss_target[0], ((N_META, tail), (0, 0)))
    u1 = _rms_pre(h0, g_pre_mix, lp)
    proj = _mm(u1, wg["w_in"], mode="nn", tm=tm_half, tn=wg["w_in"].shape[2], b_blocked=True, name="mm_proj")
    passed = pass_on(1, proj)
    conv_c = _conv_fwd(proj, wdw_full, b_dw, lp, conv_w, pool_w)
    s_act = _ln_silu(conv_c, conv_ln_g, conv_ln_b, lp)
    complete(1, passed, s_act)
    wpg_full = wg["w_pool_grp"].reshape(N_DEV, n_grp, grp_rows, gdim).transpose(1, 0, 2, 3).reshape(n_grp, gdim, gdim)
    w_o_full = wg["w_o"].reshape(d, d)
    ya_pre = _pool_fwd(proj, wpg_full, pool_scale, lp)
    m_mix, y_a, y_b = _gate_mix(ya_pre, s_act, wg["w_pool_out"], wg["w_conv_out"], proj, d, "gate_mix")
    o = _mm(m_mix, w_o_full, mode="nn", tm=tm_half, tn=512, name="mm_o")
    h1, u2 = _post_mix(o, h0, g_post_mix, g_pre_mlp, lp)
    complete(2, pass_on(2, u2), u2)
    act = _mm(u2, wg["w_up"], mode="nn", tm=tm_half, tn=wg["w_up"].shape[2], b_blocked=True, out_dtypes=(BF16,),
              epilogue=lambda acc: (jnp.square(jnp.maximum(acc, 0.0)),), name="mm_up")
    complete(3, pass_on(3, act), act)
    w_down_full = wg["w_down"].reshape(-1, d)
    f = _mm(act, w_down_full, mode="nn", tm=tm_half, tn=512, tk=2048, name="mm_down")

    big_out = {}

    def exchange(names, grads, tag):
        from_sibling = _pair_exchange(grads, "grads_to_sibling_" + tag)
        sums = [_pair_sum(g.reshape((4, 2) + g.shape[1:]), r, c_idx.reshape(1), "pair_sum_" + k)
                for k, g, r in zip(names, grads, from_sibling)]
        send, recv, sums, lands, token = _chip_exchange_start(sums, "grads_to_owner_start_" + tag)
        return (names, send, recv, sums, lands, tag), token

    def update(handle, after):
        names, send, recv, sums, lands, tag = handle
        got = _chip_exchange_finish(sums, lands, send, recv, after, "grads_to_owner_finish_" + tag)
        for k, cs, r3 in zip(names, sums, got):
            w2 = big[k]
            shape = moments[k][0].shape
            outs = _adamw_big(w2, moments[k][0].reshape(w2.shape), moments[k][1].reshape(w2.shape), cs, r3,
                              chip_idx.reshape(1), "adamw_" + k)
            big_out[k] = [a.reshape(shape) for a in outs]
        return big_out[names[-1]][0]

    df, dh, dg_post_mlp, loss_part = _loss_head(f, h1, target, g_post_mlp, lp, seq)
    d_up = _mm(df, w_down_full, mode="nt", tm=tm_half, tn=1024, out_dtypes=(BF16,), extras=[(act, 0)],
               epilogue=lambda acc, a: (acc * (2.0 * jnp.sqrt(a.astype(F32))),), name="mm_d_up")
    g_w_down = _mm(act, df, mode="tn", tm=1024, tn=1024, out_dtypes=(BF16,), name="mm_g_down")
    pending_down, token = exchange(["w_down"], [g_w_down.reshape(N_DEV, -1, d)], "down")
    g_w_up = _mm(u2, d_up, mode="tn", tm=1024, tn=wg["w_up"].shape[2], out_blocked=True, out_dtypes=(BF16,),
                 after=token, name="mm_g_up")
    pending_up, token = exchange(["w_up"], [g_w_up], "up")
    du2 = _mm(d_up, wg["w_up"], mode="nt", tm=tm_half, tn=1024, b_blocked=True, after=token, name="mm_du2")
    dh1, do, dg_pre_mlp, dg_post_mix = _mid_bwd(du2, h1, dh, o, g_pre_mlp, g_post_mix, lp)

    def gate_bwd(dm, ga, gb, ya, yb):
        sa, sb = jax.nn.sigmoid(ga), jax.nn.sigmoid(gb)
        return (dm * ya.astype(F32) * sa * (1.0 - sa), dm * yb.astype(F32) * sb * (1.0 - sb), dm * sa, dm * sb)

    gate_tn = 512
    d_ga, d_gb, d_ya, d_yb = _mm(
        do, w_o_full, mode="nt", tm=tm_half, tn=gate_tn, out_dtypes=(BF16,) * 4,
        extras=[(proj, (proj.shape[1] - 2 * d) // gate_tn), (proj, (proj.shape[1] - d) // gate_tn), (y_a, 0), (y_b, 0)],
        epilogue=gate_bwd, name="mm_dm")
    g_w_o = _mm(m_mix, do, mode="tn", tm=1024, tn=1024, out_dtypes=(BF16,), name="mm_g_o")
    bn_out = wg["w_pool_out"].shape[2]
    g_w_pool_out = _mm(ya_pre, d_ya, mode="tn", tm=pool_w, tn=bn_out, out_blocked=True, out_dtypes=(BF16,),
                       name="mm_g_pool_out")
    g_w_conv_out = _mm(s_act, d_yb, mode="tn", tm=conv_w, tn=bn_out, out_blocked=True, out_dtypes=(BF16,),
                       name="mm_g_conv_out")
    pending_mix, token = exchange(["w_o", "w_pool_out", "w_conv_out"],
                                  [g_w_o.reshape(N_DEV, -1, d), g_w_pool_out, g_w_conv_out], "mix")
    d_ya_pre = _mm(d_ya, wg["w_pool_out"], mode="nt", tm=tm_half, tn=pool_w, b_blocked=True, after=token,
                   name="mm_d_ya_pre")
    d_s = _mm(d_yb, wg["w_conv_out"], mode="nt", tm=tm_half, tn=conv_w, b_blocked=True, name="mm_d_s")
    dz, g_wpg, d_scale = _pool_bwd(proj, d_ya_pre, wpg_full, pool_scale, lp)
    dc, d_ln_g, d_ln_b, d_b_dw = _ln_silu_bwd(conv_c, d_s, conv_ln_g, conv_ln_b, lp)
    dv, dgc, g_wdw = _conv_bwd(proj, dc, wdw_full, lp, conv_w, pool_w)
    dproj = jnp.concatenate([dz, dv, dgc, d_ga, d_gb], axis=1)
    g_w_in = _mm(u1, dproj, mode="tn", tm=1024, tn=wg["w_in"].shape[2], out_blocked=True, out_dtypes=(BF16,),
                 name="mm_g_in")
    g_wpg_slots = g_wpg.astype(BF16).reshape(n_grp, N_DEV, grp_rows, gdim).transpose(1, 0, 2, 3)
    pending_in, token = exchange(["w_pool_grp", "w_in"],
                                 [g_wpg_slots.reshape(N_DEV, n_grp * grp_rows, gdim), g_w_in], "in")
    done = update(pending_down, token)
    done = update(pending_up, done)
    du1 = _mm(dproj, wg["w_in"], mode="nt", tm=tm_half, tn=1024, b_blocked=True, after=done, name="mm_du1")
    dh0, dg_pre_mix = _pre_mix_bwd(du1, h0, dh1, g_pre_mix, lp)
    grad_x = dh0[N_META:N_META + seq][None]
    done = update(pending_mix, dh0)
    update(pending_in, done)

    rep = dict(g_pre_mix=(g_pre_mix, m_g_pre_mix, v_g_pre_mix, dg_pre_mix),
               pool_scale=(pool_scale, m_pool_scale, v_pool_scale, d_scale),
               b_dw=(b_dw, m_b_dw, v_b_dw, d_b_dw),
               conv_ln_g=(conv_ln_g, m_conv_ln_g, v_conv_ln_g, d_ln_g),
               conv_ln_b=(conv_ln_b, m_conv_ln_b, v_conv_ln_b, d_ln_b),
               g_post_mix=(g_post_mix, m_g_post_mix, v_g_post_mix, dg_post_mix),
               g_pre_mlp=(g_pre_mlp, m_g_pre_mlp, v_g_pre_mlp, dg_pre_mlp),
               g_post_mlp=(g_post_mlp, m_g_post_mlp, v_g_post_mlp, dg_post_mlp))
    rep_names = list(rep)
    small_all = _all_gather([rep[k][3] for k in rep_names] + [dh0[:N_META], g_wdw, loss_part], "gather_small_grads")
    wdw_wmv = [jnp.pad(a[0], pad_taps) for a in (w_dw, m_w_dw, v_w_dw)]
    rep_out, meta_out, wdw_out, loss_blk = _small_update(
        me_idx.reshape(1), [rep[k][:3] for k in rep_names], small_all[:len(rep_names)],
        (meta, m_meta, v_meta), small_all[-3], wdw_wmv, small_all[-2], small_all[-1])
    small_out = dict(zip(rep_names, rep_out))
    small_out["meta"] = meta_out
    small_out["w_dw"] = [a[:CONV_KERNEL][None] for a in wdw_out]

    order = ["meta", "g_pre_mix", "w_in", "w_pool_grp", "pool_scale", "w_pool_out", "w_dw", "b_dw", "conv_ln_g",
             "conv_ln_b", "w_conv_out", "w_o", "g_post_mix", "g_pre_mlp", "w_up", "w_down", "g_post_mlp"]
    by_name = {**big_out, **small_out}
    result = [loss_blk[0, 0], grad_x]
    for kind in range(4):
        result += [by_name[k][kind] for k in order]
    return tuple(result)
```

```python
import functools

import jax
import jax.numpy as jnp
from jax import lax
from jax.experimental import pallas as pl
from jax.experimental.pallas import tpu as pltpu

F32 = jnp.float32
BF16 = jnp.bfloat16
MESH = pl.DeviceIdType.MESH

N_DEV = 8
N_META = 16
POOL_WINDOWS = (2, 4, 8, 16)
CONV_KERNEL = 31
CONV_TAPS_PADDED = 32
RMS_EPS = 1e-6
LN_EPS = 1e-5
ADAM_LR = 0.001
ADAM_B1 = 0.9
ADAM_B2 = 0.999
ADAM_EPS = 1e-08
ADAM_WD = 0.01
ADAM_STEP = 10

LANES = 128
SUBLANES = 8
ROW_CHUNK = 128
HALO = 32
VMEM_LIMIT_BYTES = 56 * 1024 * 1024


def _cparams(*sem):
    return pltpu.CompilerParams(dimension_semantics=sem if sem else None, vmem_limit_bytes=VMEM_LIMIT_BYTES)


def _round_up(n, m):
    return (n + m - 1) // m * m


def _row_tile(rows, cols, max_elems=640 * 1024):
    best = None
    for t in range(16, rows + 1, 16):
        if rows % t == 0 and (best is None or t * cols <= max_elems):
            best = t
    assert best is not None, (rows, cols)
    return best


def _rowsum8(a):
    t, w = a.shape
    return a.reshape(t // SUBLANES, SUBLANES, w).sum(axis=0)


def _mesh_pos():
    return lax.axis_index("x"), lax.axis_index("y"), lax.axis_index("c")


HBM_SPEC = pl.BlockSpec(memory_space=pltpu.HBM)
SEM_SPEC = pl.BlockSpec(memory_space=pltpu.SEMAPHORE)
ANY_SPEC = pl.BlockSpec(memory_space=pl.ANY)
_DATAFLOW = pltpu.SideEffectType.DATAFLOW_SIDE_EFFECTING


def _hbm(a):
    return pltpu.with_memory_space_constraint(a, pltpu.HBM)


def _slot(p):
    return 4 * p[0] + 2 * p[1] + p[2]


def _fill_slot(w, slot_idx, dtype, name):
    rows, cols = w.shape
    tr = _row_tile(rows, cols) if rows % 16 == 0 else rows

    def body(idx_ref, w_ref, out_ref):
        out_ref[...] = w_ref[...].astype(dtype)

    return pl.pallas_call(
        body, name=name,
        grid_spec=pltpu.PrefetchScalarGridSpec(
            num_scalar_prefetch=1, grid=(rows // tr,),
            in_specs=[pl.BlockSpec((tr, cols), lambda i, idx_ref: (i, 0))],
            out_specs=pl.BlockSpec((None, tr, cols), lambda i, idx_ref: (idx_ref[0], i, 0))),
        out_shape=jax.ShapeDtypeStruct((N_DEV, rows, cols), dtype),
        compiler_params=_cparams("parallel"),
    )(slot_idx, w)


def _gather_start(groups, after, name):
    flat = [g for grp in groups for g in grp]
    n, n_grp = len(flat), len(groups)

    def body(*refs):
        lands = refs[:n]
        sems = refs[n + 1:n + 1 + 2 * n_grp]
        token = refs[-1]
        x, y, c = _mesh_pos()
        targets = [(x, y, 1 - c), (1 - x, y, c), (x, 1 - y, c), (1 - x, 1 - y, c)]
        t = 0
        for gi, grp in enumerate(groups):
            for ti in range(len(grp)):
                mine = lands[t].at[_slot((x, y, c))]
                for k, to in enumerate(targets):
                    pltpu.make_async_remote_copy(
                        src_ref=mine, dst_ref=mine,
                        send_sem=sems[2 * gi].at[4 * ti + k], recv_sem=sems[2 * gi + 1].at[4 * ti + k],
                        device_id=to, device_id_type=MESH).start()
                t += 1
        token[...] = jnp.zeros_like(token)

    sem_shapes = []
    for grp in groups:
        sem_shapes += [pltpu.SemaphoreType.DMA((4 * len(grp),))] * 2
    outs = pl.pallas_call(
        body, name=name,
        out_shape=tuple(sem_shapes + [pltpu.HBM(g.shape, g.dtype) for g in flat]
                        + [jax.ShapeDtypeStruct((SUBLANES, LANES), F32)]),
        in_specs=tuple([HBM_SPEC] * n + [ANY_SPEC]),
        out_specs=tuple([SEM_SPEC] * (2 * n_grp) + [HBM_SPEC] * n + [pl.BlockSpec(memory_space=pltpu.VMEM)]),
        input_output_aliases={i: 2 * n_grp + i for i in range(n)},
        compiler_params=pltpu.CompilerParams(has_side_effects=_DATAFLOW),
    )(*[_hbm(g) for g in flat], after)
    sems, lands, token = outs[:2 * n_grp], outs[2 * n_grp:-1], outs[-1]
    res, t = [], 0
    for gi, grp in enumerate(groups):
        res.append((sems[2 * gi], sems[2 * gi + 1], list(lands[t:t + len(grp)])))
        t += len(grp)
    return res, token


def _gather_forward(lands, recv_sems, after, name):
    n = len(lands)

    def body(*refs):
        land_refs, recv, _ = refs[:n], refs[n], refs[n + 1]
        fsend, frecv = refs[n + 2], refs[n + 3]
        x, y, c = _mesh_pos()
        chips = [(1 - x, y), (x, 1 - y), (1 - x, 1 - y)]
        for t in range(n):
            for j, chip in enumerate(chips):
                blk = land_refs[t].at[_slot((*chip, c))]
                pltpu.make_async_remote_copy(src_ref=blk, dst_ref=blk, send_sem=fsend.at[3 * t + j],
                                             recv_sem=recv.at[4 * t + 1 + j],
                                             device_id=(x, y, 1 - c), device_id_type=MESH).wait_recv()
                pltpu.make_async_remote_copy(src_ref=blk, dst_ref=blk, send_sem=fsend.at[3 * t + j],
                                             recv_sem=frecv.at[3 * t + j],
                                             device_id=(x, y, 1 - c), device_id_type=MESH).start()

    outs = pl.pallas_call(
        body, name=name,
        out_shape=tuple([pltpu.SemaphoreType.DMA((3 * n,))] * 2 + [pltpu.HBM(g.shape, g.dtype) for g in lands]),
        in_specs=tuple([HBM_SPEC] * n + [SEM_SPEC, ANY_SPEC]),
        out_specs=tuple([SEM_SPEC] * 2 + [HBM_SPEC] * n),
        input_output_aliases={i: 2 + i for i in range(n)},
        compiler_params=pltpu.CompilerParams(has_side_effects=_DATAFLOW),
    )(*lands, recv_sems, after)
    return outs[0], outs[1], list(outs[2:])


def _gather_finish(lands, send_sems, recv_sems, fsend, frecv, after, name):
    n = len(lands)

    def body(*refs):
        land_refs = refs[:n]
        send, recv, fs, fr = refs[n:n + 4]
        x, y, c = _mesh_pos()
        sibling = (x, y, 1 - c)
        chips = [(1 - x, y), (x, 1 - y), (1 - x, 1 - y)]

        def desc(src, dst, s_sem, r_sem):
            return pltpu.make_async_remote_copy(src_ref=src, dst_ref=dst, send_sem=s_sem, recv_sem=r_sem,
                                                device_id=sibling, device_id_type=MESH)

        for t in range(n):
            mine = land_refs[t].at[_slot((x, y, c))]
            desc(mine, land_refs[t].at[_slot(sibling)], send.at[4 * t], recv.at[4 * t]).wait_recv()
            for j, chip in enumerate(chips):
                got = land_refs[t].at[_slot((*chip, 1 - c))]
                desc(got, got, fs.at[3 * t + j], fr.at[3 * t + j]).wait_recv()
            for k in range(4):
                desc(mine, mine, send.at[4 * t + k], recv.at[4 * t + k]).wait_send()
            for j, chip in enumerate(chips):
                blk = land_refs[t].at[_slot((*chip, c))]
                desc(blk, blk, fs.at[3 * t + j], fr.at[3 * t + j]).wait_send()

    outs = pl.pallas_call(
        body, name=name,
        out_shape=tuple(pltpu.HBM(g.shape, g.dtype) for g in lands),
        in_specs=tuple([HBM_SPEC] * n + [SEM_SPEC] * 4 + [ANY_SPEC]),
        out_specs=tuple([HBM_SPEC] * n),
        input_output_aliases={i: i for i in range(n)},
        compiler_params=pltpu.CompilerParams(has_side_effects=_DATAFLOW),
    )(*lands, send_sems, recv_sems, fsend, frecv, after)
    return list(outs)


def _pair_exchange_start(grads, after, name):
    n = len(grads)

    def body(*refs):
        ins, lands = refs[:n], refs[n:2 * n]
        send, recv = refs[2 * n + 1], refs[2 * n + 2]
        token = refs[-1]
        x, y, c = _mesh_pos()
        for t in range(n):
            for q in range(4):
                pltpu.make_async_remote_copy(
                    src_ref=ins[t].at[2 * q + 1 - c], dst_ref=lands[t].at[q],
                    send_sem=send.at[4 * t + q], recv_sem=recv.at[4 * t + q],
                    device_id=(x, y, 1 - c), device_id_type=MESH).start()
        token[...] = jnp.zeros_like(token)

    land_shapes = [(4,) + g.shape[1:] for g in grads]
    outs = pl.pallas_call(
        body, name=name,
        out_shape=tuple([pltpu.SemaphoreType.DMA((4 * n,))] * 2 + [pltpu.HBM(g.shape, g.dtype) for g in grads]
                        + [pltpu.HBM(ls, g.dtype) for ls, g in zip(land_shapes, grads)]
                        + [jax.ShapeDtypeStruct((SUBLANES, LANES), F32)]),
        in_specs=tuple([HBM_SPEC] * (2 * n) + [ANY_SPEC]),
        out_specs=tuple([SEM_SPEC] * 2 + [HBM_SPEC] * (2 * n) + [pl.BlockSpec(memory_space=pltpu.VMEM)]),
        input_output_aliases={i: 2 + i for i in range(2 * n)},
        compiler_params=pltpu.CompilerParams(has_side_effects=_DATAFLOW),
    )(*[_hbm(g) for g in grads], *[_hbm(lax.empty(ls, g.dtype)) for ls, g in zip(land_shapes, grads)], after)
    return outs[0], outs[1], list(outs[2:2 + n]), list(outs[2 + n:2 + 2 * n]), outs[-1]


def _pair_exchange_finish(grads, lands, send_sems, recv_sems, after, name):
    n = len(grads)

    def body(*refs):
        ins, land_refs = refs[:n], refs[n:2 * n]
        send, recv = refs[2 * n], refs[2 * n + 1]
        x, y, c = _mesh_pos()
        for t in range(n):
            for q in range(4):
                cp = pltpu.make_async_remote_copy(
                    src_ref=ins[t].at[q], dst_ref=land_refs[t].at[q], send_sem=send.at[4 * t + q],
                    recv_sem=recv.at[4 * t + q], device_id=(x, y, 1 - c), device_id_type=MESH)
                cp.wait_send()
                cp.wait_recv()

    outs = pl.pallas_call(
        body, name=name,
        out_shape=tuple([pltpu.HBM(g.shape, g.dtype) for g in grads] + [pltpu.HBM(g.shape, g.dtype) for g in lands]),
        in_specs=tuple([HBM_SPEC] * (2 * n) + [SEM_SPEC] * 2 + [ANY_SPEC]),
        out_specs=tuple([HBM_SPEC] * (2 * n)),
        input_output_aliases={i: i for i in range(2 * n)},
        compiler_params=pltpu.CompilerParams(has_side_effects=_DATAFLOW),
    )(*grads, *lands, send_sems, recv_sems, after)
    return list(outs[:n]), list(outs[n:])


def _chip_exchange_start(sums, after, name):
    n = len(sums)

    def body(*refs):
        ins, lands = refs[:n], refs[n:2 * n]
        send, recv = refs[2 * n + 1], refs[2 * n + 2]
        token = refs[-1]
        x, y, c = _mesh_pos()
        chips = [(1 - x, y), (x, 1 - y), (1 - x, 1 - y)]
        for t in range(n):
            for r, chip in enumerate(chips):
                pltpu.make_async_remote_copy(
                    src_ref=ins[t].at[2 * chip[0] + chip[1]], dst_ref=lands[t].at[r],
                    send_sem=send.at[3 * t + r], recv_sem=recv.at[3 * t + r],
                    device_id=(*chip, c), device_id_type=MESH).start()
        token[...] = jnp.zeros_like(token)

    land_shapes = [(3,) + s.shape[1:] for s in sums]
    outs = pl.pallas_call(
        body, name=name,
        out_shape=tuple([pltpu.SemaphoreType.DMA((3 * n,))] * 2 + [pltpu.HBM(s.shape, s.dtype) for s in sums]
                        + [pltpu.HBM(ls, s.dtype) for ls, s in zip(land_shapes, sums)]
                        + [jax.ShapeDtypeStruct((SUBLANES, LANES), F32)]),
        in_specs=tuple([HBM_SPEC] * (2 * n) + [ANY_SPEC]),
        out_specs=tuple([SEM_SPEC] * 2 + [HBM_SPEC] * (2 * n) + [pl.BlockSpec(memory_space=pltpu.VMEM)]),
        input_output_aliases={i: 2 + i for i in range(2 * n)},
        compiler_params=pltpu.CompilerParams(has_side_effects=_DATAFLOW),
    )(*[_hbm(s) for s in sums], *[_hbm(lax.empty(ls, s.dtype)) for ls, s in zip(land_shapes, sums)], after)
    return outs[0], outs[1], list(outs[2:2 + n]), list(outs[2 + n:2 + 2 * n]), outs[-1]


def _chip_exchange_finish(sums, lands, send_sems, recv_sems, after, name):
    n = len(sums)

    def body(*refs):
        ins, land_refs = refs[:n], refs[n:2 * n]
        send, recv = refs[2 * n], refs[2 * n + 1]
        x, y, c = _mesh_pos()
        for t in range(n):
            for r in range(3):
                cp = pltpu.make_async_remote_copy(
                    src_ref=ins[t].at[r], dst_ref=land_refs[t].at[r], send_sem=send.at[3 * t + r],
                    recv_sem=recv.at[3 * t + r],
                    device_id=(x, y, 1 - c), device_id_type=MESH)
                cp.wait_send()
                cp.wait_recv()

    outs = pl.pallas_call(
        body, name=name,
        out_shape=tuple(pltpu.HBM(g.shape, g.dtype) for g in lands),
        in_specs=tuple([HBM_SPEC] * (2 * n) + [SEM_SPEC] * 2 + [ANY_SPEC]),
        out_specs=tuple([HBM_SPEC] * n),
        input_output_aliases={n + i: i for i in range(n)},
        compiler_params=pltpu.CompilerParams(has_side_effects=_DATAFLOW),
    )(*sums, *lands, send_sems, recv_sems, after)
    return list(outs)


def _mm(a, b, *, mode, tm, tn, tk=None, b_blocked=False, out_blocked=False, out_dtypes=(F32,),
        epilogue=None, extras=(), after=None, kb=1, name):
    if mode == "nn":
        m, k = a.shape
        n = b.shape[0] * b.shape[2] if b_blocked else b.shape[1]
        dims = (((1,), (0,)), ((), ()))
    elif mode == "nt":
        m, k = a.shape
        n = b.shape[1] if b_blocked else b.shape[0]
        if b_blocked:
            tk = kb * b.shape[2]
        dims = (((1,), (1,)), ((), ()))
    else:
        k, m = a.shape
        n = b.shape[1]
        dims = (((0,), (0,)), ((), ()))
    tk = k if tk is None else tk
    assert m % tm == 0 and n % tn == 0 and k % tk == 0, (name, m, n, k, tm, tn, tk)
    gm, gn, gk = m // tm, n // tn, k // tk
    if b_blocked:
        assert (tn if mode == "nn" else tk) == kb * b.shape[2], name

    if mode == "nn":
        a_spec = pl.BlockSpec((tm, tk), lambda i, j, kk: (i, kk))
        b_spec = (pl.BlockSpec((None, tk, tn), lambda i, j, kk: (j, kk, 0)) if b_blocked
                  else pl.BlockSpec((tk, tn), lambda i, j, kk: (kk, j)))
    elif mode == "nt":
        a_spec = pl.BlockSpec((tm, tk), lambda i, j, kk: (i, kk))
        b_spec = (pl.BlockSpec((kb, tn, tk // kb), lambda i, j, kk: (kk, j, 0)) if b_blocked
                  else pl.BlockSpec((tn, tk), lambda i, j, kk: (j, kk)))
    else:
        a_spec = pl.BlockSpec((tk, tm), lambda i, j, kk: (kk, i))
        b_spec = pl.BlockSpec((tk, tn), lambda i, j, kk: (kk, j))
    if out_blocked:
        out_spec = pl.BlockSpec((None, tm, tn), lambda i, j, kk: (j, i, 0))
        out_shape = (gn, m, tn)
    else:
        out_spec = pl.BlockSpec((tm, tn), lambda i, j, kk: (i, j))
        out_shape = (m, n)
    extra_specs = [pl.BlockSpec((tm, tn), functools.partial(lambda i, j, kk, off: (i, j + off), off=off))
                   for _, off in extras]
    n_extra, n_out = len(extras), len(out_dtypes)
    n_after = 0 if after is None else 1

    def body(a_ref, b_ref, *rest):
        extra_refs = rest[:n_extra]
        out_refs = rest[n_extra + n_after:n_extra + n_after + n_out]

        def finish(acc):
            if epilogue is None:
                res = (acc,)
            else:
                res = epilogue(acc, *[e[...] for e in extra_refs])
            for o_ref, r in zip(out_refs, res):
                o_ref[...] = r.astype(o_ref.dtype)

        if mode == "nt" and b_blocked:
            bk = tk // kb
            part = lax.dot_general(a_ref[:, :bk], b_ref[0], dims, preferred_element_type=F32)
            for h in range(1, kb):
                part = part + lax.dot_general(a_ref[:, h * bk:(h + 1) * bk], b_ref[h], dims,
                                              preferred_element_type=F32)
        else:
            part = lax.dot_general(a_ref[...], b_ref[...], dims, preferred_element_type=F32)
        if gk == 1:
            finish(part)
        else:
            acc_ref = rest[-1]
            kk = pl.program_id(2)

            @pl.when(kk == 0)
            def _():
                acc_ref[...] = part

            @pl.when(kk > 0)
            def _():
                acc_ref[...] += part

            @pl.when(kk == gk - 1)
            def _():
                finish(acc_ref[...])

    outs = pl.pallas_call(
        body, name=name, grid=(gm, gn, gk),
        in_specs=[a_spec, b_spec] + extra_specs + [ANY_SPEC] * n_after,
        out_specs=[out_spec] * n_out,
        out_shape=[jax.ShapeDtypeStruct(out_shape, d) for d in out_dtypes],
        scratch_shapes=[pltpu.VMEM((tm, tn), F32)] if gk > 1 else [],
        compiler_params=_cparams("parallel", "parallel", "arbitrary"),
    )(a, b, *[e for e, _ in extras], *([] if after is None else [after]))
    return outs[0] if n_out == 1 else outs


def _gate_mix(ya_pre, s, wpo, wco, proj, d_model, name):
    lp, width = ya_pre.shape
    nb, _, bn = wpo.shape
    ga_off = (proj.shape[1] - 2 * d_model) // bn
    gb_off = (proj.shape[1] - d_model) // bn

    def body(ya_ref, s_ref, wpo_ref, wco_ref, ga_ref, gb_ref, m_ref, y_a_ref, y_b_ref):
        y_a = jnp.dot(ya_ref[...], wpo_ref[...], preferred_element_type=F32)
        y_b = jnp.dot(s_ref[...], wco_ref[...], preferred_element_type=F32)
        m = jax.nn.sigmoid(ga_ref[...]) * y_a + jax.nn.sigmoid(gb_ref[...]) * y_b
        m_ref[...] = m.astype(BF16)
        y_a_ref[...] = y_a.astype(BF16)
        y_b_ref[...] = y_b.astype(BF16)

    act_spec = pl.BlockSpec((lp, width), lambda j: (0, 0))
    w_spec = pl.BlockSpec((None, width, bn), lambda j: (j, 0, 0))
    out_spec = pl.BlockSpec((lp, bn), lambda j: (0, j))
    return pl.pallas_call(
        body, name=name, grid=(nb,),
        in_specs=[act_spec, act_spec, w_spec, w_spec,
                  pl.BlockSpec((lp, bn), lambda j: (0, j + ga_off)),
                  pl.BlockSpec((lp, bn), lambda j: (0, j + gb_off))],
        out_specs=[out_spec] * 3,
        out_shape=[jax.ShapeDtypeStruct((lp, nb * bn), BF16)] * 3,
        compiler_params=_cparams("parallel"),
    )(ya_pre, s, wpo, wco, proj, proj)


def _rms_stats(x):
    return lax.rsqrt(jnp.mean(x * x, axis=-1, keepdims=True) + RMS_EPS)


def _rms_bwd(x, g, dy):
    r = _rms_stats(x)
    nrm = x * r
    dn = dy * g
    dx = r * (dn - nrm * jnp.mean(dn * nrm, axis=-1, keepdims=True))
    return dx, dy * nrm


def _rowwise(body, ins, outs, accs, *, lp, name):
    tr = _row_tile(lp, max(a.shape[1] for a in ins))
    n_in, n_out, n_acc = len(ins), len(outs), len(accs)

    def kernel_body(*refs):
        i = pl.program_id(0)
        acc_refs = refs[n_in + n_out:]

        @pl.when(i == 0)
        def _():
            for r in acc_refs:
                r[...] = jnp.zeros_like(r)

        body(i * tr, refs[:n_in], refs[n_in:n_in + n_out], acc_refs)

    in_specs = []
    for a in ins:
        if a.shape[0] == lp:
            in_specs.append(pl.BlockSpec((tr, a.shape[1]), lambda i: (i, 0)))
        else:
            in_specs.append(pl.BlockSpec(a.shape, lambda i: (0, 0)))
    out_specs = [pl.BlockSpec((tr, w), lambda i: (i, 0)) for w, _ in outs]
    out_specs += [pl.BlockSpec((SUBLANES, w), lambda i: (0, 0)) for w in accs]
    out_shape = [jax.ShapeDtypeStruct((lp, w), d) for w, d in outs]
    out_shape += [jax.ShapeDtypeStruct((SUBLANES, w), F32) for w in accs]
    return pl.pallas_call(
        kernel_body, name=name, grid=(lp // tr,), in_specs=in_specs, out_specs=out_specs,
        out_shape=out_shape, compiler_params=_cparams("arbitrary"),
    )(*ins)


def _rms_pre(h0, g, lp):
    d = h0.shape[1]

    def body(row0, ins, outs, accs):
        h_ref, g_ref = ins
        x = h_ref[...]
        outs[0][...] = (x * _rms_stats(x) * g_ref[...]).astype(BF16)

    return _rowwise(body, [h0, g], [(d, BF16)], [], lp=lp, name="rms_pre")[0]


def _post_mix(o, h0, g_post_mix, g_pre_mlp, lp):
    d = h0.shape[1]

    def body(row0, ins, outs, accs):
        o_ref, h0_ref, g1_ref, g2_ref = ins
        o_v = o_ref[...]
        h1 = h0_ref[...] + o_v * _rms_stats(o_v) * g1_ref[...]
        outs[0][...] = h1
        outs[1][...] = (h1 * _rms_stats(h1) * g2_ref[...]).astype(BF16)

    return _rowwise(body, [o, h0, g_post_mix, g_pre_mlp], [(d, F32), (d, BF16)], [], lp=lp, name="post_mix")


def _loss_head(f, h1, target, g_post_mlp, lp, seq):
    d = f.shape[1]

    def body(row0, ins, outs, accs):
        f_ref, h1_ref, t_ref, g_ref = ins
        df_ref, dh_ref = outs
        dg_ref, loss_ref = accs
        f_v, g = f_ref[...], g_ref[...]
        r = _rms_stats(f_v)
        nrm = f_v * r
        rows = row0 + lax.broadcasted_iota(jnp.int32, (f_v.shape[0], 1), 0)
        valid = (rows >= N_META) & (rows < N_META + seq)
        err = jnp.where(valid, h1_ref[...] + nrm * g - t_ref[...], 0.0)
        loss_ref[...] += 0.5 * jnp.sum(jnp.mean(err * err, axis=-1, keepdims=True))
        dy = err * (1.0 / d)
        dn = dy * g
        df_ref[...] = (r * (dn - nrm * jnp.mean(dn * nrm, axis=-1, keepdims=True))).astype(BF16)
        dh_ref[...] = dy
        dg_ref[...] += _rowsum8(dy * nrm)

    return _rowwise(body, [f, h1, target, g_post_mlp], [(d, BF16), (d, F32)], [d, LANES], lp=lp, name="loss_head")


def _mid_bwd(du2, h1, dh, o, g_pre_mlp, g_post_mix, lp):
    d = h1.shape[1]

    def body(row0, ins, outs, accs):
        du2_ref, h1_ref, dh_ref, o_ref, g2_ref, g1_ref = ins
        dx2, dg2 = _rms_bwd(h1_ref[...], g2_ref[...], du2_ref[...])
        dh1 = dh_ref[...] + dx2
        do, dg1 = _rms_bwd(o_ref[...], g1_ref[...], dh1)
        outs[0][...] = dh1
        outs[1][...] = do.astype(BF16)
        accs[0][...] += _rowsum8(dg2)
        accs[1][...] += _rowsum8(dg1)

    return _rowwise(body, [du2, h1, dh, o, g_pre_mlp, g_post_mix], [(d, F32), (d, BF16)], [d, d], lp=lp,
                    name="mid_bwd")


def _pre_mix_bwd(du1, h0, dh1, g_pre_mix, lp):
    d = h0.shape[1]

    def body(row0, ins, outs, accs):
        du1_ref, h0_ref, dh1_ref, g_ref = ins
        dx, dg = _rms_bwd(h0_ref[...], g_ref[...], du1_ref[...])
        outs[0][...] = dh1_ref[...] + dx
        accs[0][...] += _rowsum8(dg)

    return _rowwise(body, [du1, h0, dh1, g_pre_mix], [(d, F32)], [d], lp=lp, name="pre_mix_bwd")


def _ln_stats(c):
    mu = jnp.mean(c, axis=-1, keepdims=True)
    var = jnp.mean(jnp.square(c - mu), axis=-1, keepdims=True)
    return mu, lax.rsqrt(var + LN_EPS)


def _ln_silu(c, ln_g, ln_b, lp):
    w = c.shape[1]

    def body(row0, ins, outs, accs):
        c_ref, g_ref, b_ref = ins
        c_v = c_ref[...]
        mu, rstd = _ln_stats(c_v)
        ln = (c_v - mu) * rstd * g_ref[...] + b_ref[...]
        outs[0][...] = (ln * jax.nn.sigmoid(ln)).astype(BF16)

    return _rowwise(body, [c, ln_g, ln_b], [(w, BF16)], [], lp=lp, name="ln_silu")[0]


def _ln_silu_bwd(c, ds, ln_g, ln_b, lp):
    w = c.shape[1]

    def body(row0, ins, outs, accs):
        c_ref, ds_ref, g_ref, b_ref = ins
        c_v, g = c_ref[...], g_ref[...]
        mu, rstd = _ln_stats(c_v)
        nrm = (c_v - mu) * rstd
        ln = nrm * g + b_ref[...]
        sig = jax.nn.sigmoid(ln)
        dln = ds_ref[...] * (sig * (1.0 + ln * (1.0 - sig)))
        dn = dln * g
        dc = rstd * (dn - jnp.mean(dn, axis=-1, keepdims=True) - nrm * jnp.mean(dn * nrm, axis=-1, keepdims=True))
        outs[0][...] = dc
        accs[0][...] += _rowsum8(dln * nrm)
        accs[1][...] += _rowsum8(dln)
        accs[2][...] += _rowsum8(dc)

    return _rowwise(body, [c, ds, ln_g, ln_b], [(w, F32)], [w, w, w], lp=lp, name="ln_silu_bwd")


def _chunk_with_history(ref, i, cols=slice(None)):
    t0 = pl.multiple_of(i * ROW_CHUNK, ROW_CHUNK)
    lo0 = pl.multiple_of(jnp.maximum(t0 - HALO, 0), SUBLANES)
    lo = jnp.where(i > 0, ref[pl.ds(lo0, HALO), cols], 0.0)
    return jnp.concatenate([lo, ref[pl.ds(t0, ROW_CHUNK), cols]], axis=0)


def _chunk_with_future(ref, i, n_chunks, cols=slice(None)):
    t0 = pl.multiple_of(i * ROW_CHUNK, ROW_CHUNK)
    hi0 = pl.multiple_of(jnp.minimum(t0 + ROW_CHUNK, (n_chunks - 1) * ROW_CHUNK), SUBLANES)
    hi = jnp.where(i < n_chunks - 1, ref[pl.ds(hi0, HALO), cols], 0.0)
    return jnp.concatenate([ref[pl.ds(t0, ROW_CHUNK), cols], hi], axis=0)


def _inv_count(t0, n_rows, window):
    pos = t0 + lax.broadcasted_iota(jnp.int32, (n_rows, 1), 0)
    return 1.0 / jnp.minimum(pos + 1, window).astype(F32)


def _pool_delta(z_hist, t0, window):
    s = z_hist
    sh = 1
    while sh < window:
        s = s + pltpu.roll(s, sh, 0)
        sh *= 2
    cur = z_hist[HALO:, :]
    return s[HALO:, :] * _inv_count(t0, ROW_CHUNK, window) - cur


def _pool_fwd(proj, wpg, pool_scale, lp):
    n_grp, gdim, _ = wpg.shape
    width = n_grp * gdim
    n_chunks = lp // ROW_CHUNK

    def body(z_ref, w_ref, sc_ref, out_ref):
        for g, window in enumerate(POOL_WINDOWS):
            cols = slice(g * gdim, (g + 1) * gdim)

            def chunk(i, carry, cols=cols, g=g, window=window):
                t0 = pl.multiple_of(i * ROW_CHUNK, ROW_CHUNK)
                d = _pool_delta(_chunk_with_history(z_ref, i, cols), t0, window)
                q = jnp.dot(d.astype(BF16), w_ref[g], preferred_element_type=F32)
                out_ref[pl.ds(t0, ROW_CHUNK), cols] = (q * sc_ref[:, cols]).astype(BF16)
                return carry

            lax.fori_loop(0, n_chunks, chunk, 0)

    return pl.pallas_call(
        body, name="pool_fwd", grid=(1,),
        in_specs=[pl.BlockSpec((lp, width), lambda i: (0, 0)),
                  pl.BlockSpec(wpg.shape, lambda i: (0, 0, 0)),
                  pl.BlockSpec(pool_scale.shape, lambda i: (0, 0))],
        out_specs=pl.BlockSpec((lp, width), lambda i: (0, 0)),
        out_shape=jax.ShapeDtypeStruct((lp, width), BF16),
        compiler_params=_cparams("arbitrary"),
    )(proj, wpg, pool_scale)


def _pool_bwd(proj, d_ya, wpg, pool_scale, lp):
    n_grp, gdim, _ = wpg.shape
    width = n_grp * gdim
    n_chunks = lp // ROW_CHUNK
    ext = ROW_CHUNK + HALO

    def body(z_ref, dya_ref, w_ref, sc_ref, dz_ref, dw_ref, dsc_ref):
        dw_ref[...] = jnp.zeros_like(dw_ref)
        dsc_ref[...] = jnp.zeros_like(dsc_ref)
        for g, window in enumerate(POOL_WINDOWS):
            cols = slice(g * gdim, (g + 1) * gdim)

            def chunk(i, carry, cols=cols, g=g, window=window):
                t0 = pl.multiple_of(i * ROW_CHUNK, ROW_CHUNK)
                w_g = w_ref[g]
                scale = sc_ref[:, cols]
                d = _pool_delta(_chunk_with_history(z_ref, i, cols), t0, window).astype(BF16)
                dya_ext = _chunk_with_future(dya_ref, i, n_chunks, cols)
                dya = dya_ext[:ROW_CHUNK, :]
                q = jnp.dot(d, w_g, preferred_element_type=F32)
                dsc_ref[:, cols] += _rowsum8(dya * q)
                e_ext = (dya_ext * scale).astype(BF16)
                dw_ref[g] += lax.dot_general(d, e_ext[:ROW_CHUNK, :], (((0,), (0,)), ((), ())),
                                             preferred_element_type=F32)
                dd_ext = lax.dot_general(e_ext, w_g, (((1,), (1,)), ((), ())), preferred_element_type=F32)
                s = dd_ext * _inv_count(t0, ext, window)
                sh = 1
                while sh < window:
                    s = s + pltpu.roll(s, ext - sh, 0)
                    sh *= 2
                dz_ref[pl.ds(t0, ROW_CHUNK), cols] = (s[:ROW_CHUNK, :] - dd_ext[:ROW_CHUNK, :]).astype(BF16)
                return carry

            lax.fori_loop(0, n_chunks, chunk, 0)

    blk = pl.BlockSpec((lp, width), lambda i: (0, 0))
    return pl.pallas_call(
        body, name="pool_bwd", grid=(1,),
        in_specs=[blk, blk, pl.BlockSpec(wpg.shape, lambda i: (0, 0, 0)),
                  pl.BlockSpec(pool_scale.shape, lambda i: (0, 0))],
        out_specs=[blk, pl.BlockSpec(wpg.shape, lambda i: (0, 0, 0)),
                   pl.BlockSpec((SUBLANES, width), lambda i: (0, 0))],
        out_shape=[jax.ShapeDtypeStruct((lp, width), BF16), jax.ShapeDtypeStruct(wpg.shape, F32),
                   jax.ShapeDtypeStruct((SUBLANES, width), F32)],
        compiler_params=_cparams("arbitrary"),
    )(proj, d_ya, wpg, pool_scale)


def _conv_fwd(proj, w_dw, b_dw, lp, width, v_col0):
    n_chunks = lp // ROW_CHUNK
    v_blk0, g_blk0 = v_col0 // LANES, (v_col0 + width) // LANES

    def body(v_ref, gc_ref, w_ref, b_ref, c_ref, a_pad):
        a_pad[pl.ds(0, HALO), :] = jnp.zeros((HALO, LANES), F32)
        a_pad[pl.ds(HALO, lp), :] = v_ref[...] * jax.nn.sigmoid(gc_ref[...])

        def chunk(i, carry):
            t0 = pl.multiple_of(i * ROW_CHUNK, ROW_CHUNK)
            hist = a_pad[pl.ds(t0, ROW_CHUNK + HALO), :]
            acc = jnp.zeros((ROW_CHUNK, LANES), F32)
            for k in range(CONV_KERNEL):
                acc = acc + w_ref[k:k + 1, :] * pltpu.roll(hist, CONV_KERNEL - 1 - k, 0)[HALO:, :]
            c_ref[pl.ds(t0, ROW_CHUNK), :] = acc + b_ref[...]
            return carry

        lax.fori_loop(0, n_chunks, chunk, 0)

    return pl.pallas_call(
        body, name="conv_fwd", grid=(width // LANES,),
        in_specs=[pl.BlockSpec((lp, LANES), lambda j: (0, j + v_blk0)),
                  pl.BlockSpec((lp, LANES), lambda j: (0, j + g_blk0)),
                  pl.BlockSpec((CONV_TAPS_PADDED, LANES), lambda j: (0, j)),
                  pl.BlockSpec((1, LANES), lambda j: (0, j))],
        out_specs=pl.BlockSpec((lp, LANES), lambda j: (0, j)),
        out_shape=jax.ShapeDtypeStruct((lp, width), F32),
        scratch_shapes=[pltpu.VMEM((lp + HALO, LANES), F32)],
        compiler_params=_cparams("parallel"),
    )(proj, proj, w_dw, b_dw)


def _conv_bwd(proj, dc, w_dw, lp, width, v_col0):
    n_chunks = lp // ROW_CHUNK
    ext = ROW_CHUNK + HALO
    v_blk0, g_blk0 = v_col0 // LANES, (v_col0 + width) // LANES

    def body(v_ref, gc_ref, dc_ref, w_ref, dv_ref, dgc_ref, dw_ref, a_pad, dc_pad, dw_acc):
        sig = jax.nn.sigmoid(gc_ref[...])
        a_pad[pl.ds(0, HALO), :] = jnp.zeros((HALO, LANES), F32)
        a_pad[pl.ds(HALO, lp), :] = v_ref[...] * sig
        dc_pad[pl.ds(0, lp), :] = dc_ref[...]
        dc_pad[pl.ds(lp, HALO), :] = jnp.zeros((HALO, LANES), F32)
        dw_acc[...] = jnp.zeros_like(dw_acc)

        def chunk(i, carry):
            t0 = pl.multiple_of(i * ROW_CHUNK, ROW_CHUNK)
            hist = a_pad[pl.ds(t0, ext), :]
            fut = dc_pad[pl.ds(t0, ext), :]
            dc_cur = fut[:ROW_CHUNK, :]
            da = jnp.zeros((ROW_CHUNK, LANES), F32)
            for k in range(CONV_KERNEL):
                lag = CONV_KERNEL - 1 - k
                da = da + w_ref[k:k + 1, :] * pltpu.roll(fut, (ext - lag) % ext, 0)[:ROW_CHUNK, :]
                dw_acc[pl.ds(SUBLANES * k, SUBLANES), :] += _rowsum8(dc_cur * pltpu.roll(hist, lag, 0)[HALO:, :])
            rows = pl.ds(t0, ROW_CHUNK)
            sg = jax.nn.sigmoid(gc_ref[rows, :])
            dv_ref[rows, :] = (da * sg).astype(BF16)
            dgc_ref[rows, :] = (da * v_ref[rows, :] * sg * (1.0 - sg)).astype(BF16)
            return carry

        lax.fori_loop(0, n_chunks, chunk, 0)
        dw_ref[...] = dw_acc[...].reshape(CONV_TAPS_PADDED, SUBLANES, LANES).sum(axis=1)

    col = lambda j: (0, j)
    return pl.pallas_call(
        body, name="conv_bwd", grid=(width // LANES,),
        in_specs=[pl.BlockSpec((lp, LANES), lambda j: (0, j + v_blk0)),
                  pl.BlockSpec((lp, LANES), lambda j: (0, j + g_blk0)),
                  pl.BlockSpec((lp, LANES), col),
                  pl.BlockSpec((CONV_TAPS_PADDED, LANES), col)],
        out_specs=[pl.BlockSpec((lp, LANES), col), pl.BlockSpec((lp, LANES), col),
                   pl.BlockSpec((CONV_TAPS_PADDED, LANES), col)],
        out_shape=[jax.ShapeDtypeStruct((lp, width), BF16), jax.ShapeDtypeStruct((lp, width), BF16),
                   jax.ShapeDtypeStruct((CONV_TAPS_PADDED, width), F32)],
        scratch_shapes=[pltpu.VMEM((lp + HALO, LANES), F32), pltpu.VMEM((lp + HALO, LANES), F32),
                        pltpu.VMEM((CONV_TAPS_PADDED * SUBLANES, LANES), F32)],
        compiler_params=_cparams("parallel"),
    )(proj, proj, dc, w_dw)


def _adamw_math(w, g, m, v):
    m = ADAM_B1 * m + (1.0 - ADAM_B1) * g
    v = ADAM_B2 * v + (1.0 - ADAM_B2) * jnp.square(g)
    m_hat = m / (1.0 - ADAM_B1 ** ADAM_STEP)
    v_hat = v / (1.0 - ADAM_B2 ** ADAM_STEP)
    delta = -ADAM_LR * (m_hat / (jnp.sqrt(v_hat) + ADAM_EPS) + ADAM_WD * w)
    return delta, m, v


def _pair_sum(own, recv, c_idx, name):
    _, _, rows, cols = own.shape
    tr = _row_tile(rows, cols, 1024 * 1024)

    def body(c_ref, own_ref, recv_ref, out_ref):
        out_ref[...] = (own_ref[...].astype(F32) + recv_ref[...].astype(F32)).astype(BF16)

    return pl.pallas_call(
        body, name=name,
        grid_spec=pltpu.PrefetchScalarGridSpec(
            num_scalar_prefetch=1, grid=(4, rows // tr),
            in_specs=[pl.BlockSpec((None, None, tr, cols), lambda q, i, c_ref: (q, c_ref[0], i, 0)),
                      pl.BlockSpec((None, tr, cols), lambda q, i, c_ref: (q, i, 0))],
            out_specs=pl.BlockSpec((None, tr, cols), lambda q, i, c_ref: (q, i, 0))),
        out_shape=jax.ShapeDtypeStruct((4, rows, cols), BF16),
        compiler_params=_cparams("parallel", "parallel"),
    )(c_idx, own, recv)


def _adamw_big(w, m, v, chip_sum, recv3, chip_idx, name):
    rows, cols = w.shape
    tr = _row_tile(rows, cols, 256 * 1024)

    def body(q_ref, w_ref, m_ref, v_ref, own_ref, r_ref, g_out, d_out, m_out, v_out):
        g = own_ref[...].astype(F32)
        for r in range(3):
            g = g + r_ref[r].astype(F32)
        delta, m_new, v_new = _adamw_math(w_ref[...], g, m_ref[...], v_ref[...])
        g_out[...] = g
        d_out[...] = delta
        m_out[...] = m_new
        v_out[...] = v_new

    blk = pl.BlockSpec((tr, cols), lambda i, q_ref: (i, 0))
    return pl.pallas_call(
        body, name=name,
        grid_spec=pltpu.PrefetchScalarGridSpec(
            num_scalar_prefetch=1, grid=(rows // tr,),
            in_specs=[blk, blk, blk,
                      pl.BlockSpec((None, tr, cols), lambda i, q_ref: (q_ref[0], i, 0)),
                      pl.BlockSpec((3, tr, cols), lambda i, q_ref: (0, i, 0))],
            out_specs=[blk] * 4),
        out_shape=[jax.ShapeDtypeStruct((rows, cols), F32)] * 4,
        compiler_params=_cparams("parallel"),
    )(chip_idx, w, m, v, chip_sum, recv3)


def _small_update(me_idx, packed, rep_params, rep_places, meta_wmv, meta_row0, wdw_wmv, wdw_row0, loss_row0):
    n_rep = len(rep_params)
    meta_rows, meta_cols = meta_wmv[0].shape
    wdw_rows, wdw_cols = wdw_wmv[0].shape

    def body(me_ref, *refs):
        pos = 0

        def take(k):
            nonlocal pos
            out = refs[pos:pos + k]
            pos += k
            return out

        rep_in = [take(3) for _ in range(n_rep)]
        rep_g = take(n_rep)
        meta_in, (meta_g,) = take(3), take(1)
        wdw_in, (wdw_g,) = take(3), take(1)
        (loss_ref,) = take(1)
        rep_out = [take(4) for _ in range(n_rep)]
        meta_out, wdw_out = take(4), take(4)
        (loss_out,) = take(1)

        def update(wmv, g, outs):
            delta, m_new, v_new = _adamw_math(wmv[0][...], g, wmv[1][...], wmv[2][...])
            for o_ref, val in zip(outs, (g, delta, m_new, v_new)):
                o_ref[...] = val

        for wmv, g_ref, outs in zip(rep_in, rep_g, rep_out):
            g = jnp.sum(g_ref[0], axis=0, keepdims=True)
            for j in range(1, N_DEV):
                g = g + jnp.sum(g_ref[j], axis=0, keepdims=True)
            update(wmv, g, outs)
        for wmv, g_ref, outs in ((meta_in, meta_g, meta_out), (wdw_in, wdw_g, wdw_out)):
            g = g_ref[0]
            for j in range(1, N_DEV):
                g = g + g_ref[j]
            update(wmv, g, outs)
        total = loss_ref[0]
        for j in range(1, N_DEV):
            total = total + loss_ref[j]
        loss_out[...] = total

    def whole(a):
        nd = a.ndim
        return pl.BlockSpec(a.shape, lambda i, me_ref, nd=nd: (0,) * nd)

    ins, in_specs = [], []
    for wmv in rep_params:
        ins += list(wmv)
        in_specs += [whole(a) for a in wmv]
    for wmv, (row0, col0) in zip(rep_params, rep_places):
        width = wmv[0].shape[1]
        ins.append(packed)
        in_specs.append(pl.BlockSpec((N_DEV, SUBLANES, width),
                                     lambda i, me_ref, rb=row0 // SUBLANES, cb=col0 // width: (0, rb, cb)))
    ins += list(meta_wmv) + [packed]
    in_specs += [whole(a) for a in meta_wmv]
    in_specs.append(pl.BlockSpec((N_DEV, meta_rows, meta_cols),
                                 lambda i, me_ref, rb=meta_row0 // meta_rows: (0, rb, me_ref[0])))
    ins += list(wdw_wmv) + [packed]
    in_specs += [whole(a) for a in wdw_wmv]
    in_specs.append(pl.BlockSpec((N_DEV, wdw_rows, wdw_cols),
                                 lambda i, me_ref, rb=wdw_row0 // wdw_rows: (0, rb, me_ref[0])))
    ins.append(packed)
    in_specs.append(pl.BlockSpec((N_DEV, SUBLANES, LANES), lambda i, me_ref, rb=loss_row0 // SUBLANES: (0, rb, 0)))

    out_shape, out_specs = [], []
    for wmv in list(rep_params) + [meta_wmv, wdw_wmv]:
        out_shape += [jax.ShapeDtypeStruct(wmv[0].shape, F32)] * 4
        out_specs += [whole(wmv[0])] * 4
    out_shape.append(jax.ShapeDtypeStruct((SUBLANES, LANES), F32))
    out_specs.append(pl.BlockSpec((SUBLANES, LANES), lambda i, me_ref: (0, 0)))

    outs = pl.pallas_call(
        body, name="small_update",
        grid_spec=pltpu.PrefetchScalarGridSpec(num_scalar_prefetch=1, grid=(1,), in_specs=in_specs,
                                               out_specs=out_specs),
        out_shape=out_shape, compiler_params=_cparams("arbitrary"),
    )(me_idx, *ins)
    groups = [outs[4 * i:4 * i + 4] for i in range(n_rep + 2)]
    return groups[:n_rep], groups[n_rep], groups[n_rep + 1], outs[-1]


def kernel(x, meta, g_pre_mix, w_in, w_pool_grp, pool_scale, w_pool_out, w_dw, b_dw, conv_ln_g, conv_ln_b, w_conv_out, w_o, g_post_mix, g_pre_mlp, w_up, w_down, g_post_mlp, loss_target, m_meta, m_g_pre_mix, m_w_in, m_w_pool_grp, m_pool_scale, m_w_pool_out, m_w_dw, m_b_dw, m_conv_ln_g, m_conv_ln_b, m_w_conv_out, m_w_o, m_g_post_mix, m_g_pre_mlp, m_w_up, m_w_down, m_g_post_mlp, v_meta, v_g_pre_mix, v_w_in, v_w_pool_grp, v_pool_scale, v_w_pool_out, v_w_dw, v_b_dw, v_conv_ln_g, v_conv_ln_b, v_w_conv_out, v_w_o, v_g_post_mix, v_g_pre_mlp, v_w_up, v_w_down, v_g_post_mlp):
    seq, d = x.shape[1], x.shape[2]
    pool_w = pool_scale.shape[1]
    conv_w = b_dw.shape[1]
    n_grp, grp_rows, gdim = w_pool_grp.shape[1:]
    lp = _round_up(N_META + seq, ROW_CHUNK)
    tm_half = lp // 2 if (lp // 2) % 16 == 0 else lp
    c_idx = lax.axis_index("c").astype(jnp.int32)
    chip_idx = (2 * lax.axis_index("x") + lax.axis_index("y")).astype(jnp.int32)
    me_idx = 2 * chip_idx + c_idx

    pad_taps = ((0, CONV_TAPS_PADDED - CONV_KERNEL), (0, 0))
    big = dict(w_in=w_in[0], w_pool_grp=w_pool_grp[0].reshape(n_grp * grp_rows, gdim), w_pool_out=w_pool_out[0],
               w_conv_out=w_conv_out[0], w_o=w_o[0], w_up=w_up[0], w_down=w_down[0])
    big_names = list(big)
    moments = dict(w_in=(m_w_in, v_w_in), w_pool_grp=(m_w_pool_grp, v_w_pool_grp), w_pool_out=(m_w_pool_out, v_w_pool_out),
                   w_conv_out=(m_w_conv_out, v_w_conv_out), w_o=(m_w_o, v_w_o), w_up=(m_w_up, v_w_up),
                   w_down=(m_w_down, v_w_down))
    slot_idx = me_idx.reshape(1)
    landing = {k: _fill_slot(big[k], slot_idx, BF16, "fill_" + k) for k in big_names}
    landing["meta"] = _fill_slot(meta, slot_idx, F32, "fill_meta")
    landing["w_dw"] = _fill_slot(jnp.pad(w_dw[0], pad_taps), slot_idx, F32, "fill_w_dw")
    gather_groups = [["w_in", "meta", "w_dw"], ["w_pool_grp", "w_pool_out", "w_conv_out", "w_o"], ["w_up"], ["w_down"]]
    started, gather_token = _gather_start([[landing[k] for k in names] for names in gather_groups], slot_idx,
                                          "gather_start")
    wg = {}

    def pass_on(gi, after):
        send, recv, lands = started[gi]
        fsend, frecv, lands = _gather_forward(lands, recv, after, f"gather_forward_{gi}")
        return send, recv, lands, fsend, frecv

    def complete(gi, handle, after):
        send, recv, lands, fsend, frecv = handle
        wg.update(zip(gather_groups[gi], _gather_finish(lands, send, recv, fsend, frecv, after,
                                                        f"gather_finish_{gi}")))

    complete(0, pass_on(0, gather_token), gather_token)
    meta_full = wg["meta"].transpose(1, 0, 2).reshape(N_META, d)
    wdw_full = wg["w_dw"].transpose(1, 0, 2).reshape(CONV_TAPS_PADDED, conv_w)
    tail = lp - N_META - seq
    h0 = jnp.concatenate([meta_full, x[0], jnp.zeros((tail, d), F32)], axis=0)
    target = jnp.pad(loss_target[0], ((N_META, tail), (0, 0)))
    u1 = _rms_pre(h0, g_pre_mix, lp)
    proj = _mm(u1, wg["w_in"], mode="nn", tm=tm_half, tn=wg["w_in"].shape[2], b_blocked=True, name="mm_proj")
    passed = pass_on(1, proj)
    conv_c = _conv_fwd(proj, wdw_full, b_dw, lp, conv_w, pool_w)
    s_act = _ln_silu(conv_c, conv_ln_g, conv_ln_b, lp)
    complete(1, passed, s_act)
    wpg_full = wg["w_pool_grp"].reshape(N_DEV, n_grp, grp_rows, gdim).transpose(1, 0, 2, 3).reshape(n_grp, gdim, gdim)
    w_o_full = wg["w_o"].reshape(d, d)
    ya_pre = _pool_fwd(proj, wpg_full, pool_scale, lp)
    m_mix, y_a, y_b = _gate_mix(ya_pre, s_act, wg["w_pool_out"], wg["w_conv_out"], proj, d, "gate_mix")
    o = _mm(m_mix, w_o_full, mode="nn", tm=tm_half, tn=512, name="mm_o")
    h1, u2 = _post_mix(o, h0, g_post_mix, g_pre_mlp, lp)
    complete(2, pass_on(2, u2), u2)
    act = _mm(u2, wg["w_up"], mode="nn", tm=tm_half, tn=wg["w_up"].shape[2], b_blocked=True, out_dtypes=(BF16,),
              epilogue=lambda acc: (jnp.square(jnp.maximum(acc, 0.0)),), name="mm_up")
    complete(3, pass_on(3, act), act)
    w_down_full = wg["w_down"].reshape(-1, d)
    f = _mm(act, w_down_full, mode="nn", tm=tm_half, tn=512, tk=4096, name="mm_down")

    big_out = {}

    def to_sibling(names, grads, after, tag):
        send, recv, grads, lands, token = _pair_exchange_start(grads, after, "grads_to_sibling_start_" + tag)
        return (names, send, recv, grads, lands, tag), token

    def to_owner(handle, after):
        names, send, recv, grads, lands, tag = handle
        grads, lands = _pair_exchange_finish(grads, lands, send, recv, after, "grads_to_sibling_finish_" + tag)
        sums = [_pair_sum(g.reshape((4, 2) + g.shape[1:]), r, c_idx.reshape(1), "pair_sum_" + k)
                for k, g, r in zip(names, grads, lands)]
        send, recv, sums, lands, token = _chip_exchange_start(sums, sums[-1], "grads_to_owner_start_" + tag)
        return (names, send, recv, sums, lands, tag), token

    def update(handle, after):
        names, send, recv, sums, lands, tag = handle
        got = _chip_exchange_finish(sums, lands, send, recv, after, "grads_to_owner_finish_" + tag)
        for k, cs, r3 in zip(names, sums, got):
            w2 = big[k]
            shape = moments[k][0].shape
            outs = _adamw_big(w2, moments[k][0].reshape(w2.shape), moments[k][1].reshape(w2.shape), cs, r3,
                              chip_idx.reshape(1), "adamw_" + k)
            big_out[k] = [a.reshape(shape) for a in outs]
        return big_out[names[-1]][0]

    df, dh, dg_post_mlp, loss_part = _loss_head(f, h1, target, g_post_mlp, lp, seq)
    d_up = _mm(df, w_down_full, mode="nt", tm=tm_half, tn=1024, out_dtypes=(BF16,), extras=[(act, 0)],
               epilogue=lambda acc, a: (acc * (2.0 * jnp.sqrt(a.astype(F32))),), name="mm_d_up")
    g_w_down = _mm(act, df, mode="tn", tm=1024, tn=1024, out_dtypes=(BF16,), name="mm_g_down")
    sib_down, token = to_sibling(["w_down"], [g_w_down.reshape(N_DEV, -1, d)], g_w_down, "down")
    g_w_up = _mm(u2, d_up, mode="tn", tm=1024, tn=wg["w_up"].shape[2], out_blocked=True, out_dtypes=(BF16,),
                 after=token, name="mm_g_up")
    pending_down, token = to_owner(sib_down, g_w_up)
    sib_up, token = to_sibling(["w_up"], [g_w_up], token, "up")
    du2 = _mm(d_up, wg["w_up"], mode="nt", tm=tm_half, tn=1024, b_blocked=True, kb=2, after=token, name="mm_du2")
    pending_up, token = to_owner(sib_up, du2)
    dh1, do, dg_pre_mlp, dg_post_mix = _mid_bwd(du2, h1, dh, o, g_pre_mlp, g_post_mix, lp)

    def gate_bwd(dm, ga, gb, ya, yb):
        sa, sb = jax.nn.sigmoid(ga), jax.nn.sigmoid(gb)
        return (dm * ya.astype(F32) * sa * (1.0 - sa), dm * yb.astype(F32) * sb * (1.0 - sb), dm * sa, dm * sb)

    gate_tn = 512
    d_ga, d_gb, d_ya, d_yb = _mm(
        do, w_o_full, mode="nt", tm=tm_half, tn=gate_tn, out_dtypes=(BF16,) * 4,
        extras=[(proj, (proj.shape[1] - 2 * d) // gate_tn), (proj, (proj.shape[1] - d) // gate_tn), (y_a, 0), (y_b, 0)],
        epilogue=gate_bwd, after=token, name="mm_dm")
    g_w_o = _mm(m_mix, do, mode="tn", tm=1024, tn=1024, out_dtypes=(BF16,), name="mm_g_o")
    bn_out = wg["w_pool_out"].shape[2]
    g_w_pool_out = _mm(ya_pre, d_ya, mode="tn", tm=pool_w, tn=bn_out, out_blocked=True, out_dtypes=(BF16,),
                       name="mm_g_pool_out")
    g_w_conv_out = _mm(s_act, d_yb, mode="tn", tm=conv_w, tn=bn_out, out_blocked=True, out_dtypes=(BF16,),
                       name="mm_g_conv_out")
    sib_mix, token = to_sibling(["w_o", "w_pool_out", "w_conv_out"],
                                [g_w_o.reshape(N_DEV, -1, d), g_w_pool_out, g_w_conv_out], g_w_conv_out, "mix")
    d_ya_pre = _mm(d_ya, wg["w_pool_out"], mode="nt", tm=tm_half, tn=pool_w, b_blocked=True, after=token,
                   name="mm_d_ya_pre")
    d_s = _mm(d_yb, wg["w_conv_out"], mode="nt", tm=tm_half, tn=conv_w, b_blocked=True, name="mm_d_s")
    pending_mix, token = to_owner(sib_mix, d_s)
    dz, g_wpg, d_scale = _pool_bwd(proj, d_ya_pre, wpg_full, pool_scale, lp)
    dc, d_ln_g, d_ln_b, d_b_dw = _ln_silu_bwd(conv_c, d_s, conv_ln_g, conv_ln_b, lp)
    dv, dgc, g_wdw = _conv_bwd(proj, dc, wdw_full, lp, conv_w, pool_w)
    dproj = jnp.concatenate([dz, dv, dgc, d_ga, d_gb], axis=1)
    g_w_in = _mm(u1, dproj, mode="tn", tm=1024, tn=wg["w_in"].shape[2], out_blocked=True, out_dtypes=(BF16,),
                 after=token, name="mm_g_in")
    g_wpg_slots = g_wpg.astype(BF16).reshape(n_grp, N_DEV, grp_rows, gdim).transpose(1, 0, 2, 3)
    sib_in, token = to_sibling(["w_pool_grp", "w_in"],
                               [g_wpg_slots.reshape(N_DEV, n_grp * grp_rows, gdim), g_w_in], g_w_in, "in")
    done = update(pending_down, token)
    done = update(pending_up, done)
    pending_in, token = to_owner(sib_in, done)
    du1 = _mm(dproj, wg["w_in"], mode="nt", tm=tm_half, tn=1024, b_blocked=True, kb=2, after=token, name="mm_du1")
    dh0, dg_pre_mix = _pre_mix_bwd(du1, h0, dh1, g_pre_mix, lp)
    grad_x = dh0[N_META:N_META + seq][None]

    assert pool_w + conv_w == d and conv_w <= d and LANES <= d
    widen = lambda a: jnp.pad(a, ((0, 0), (0, d - a.shape[1])))
    packed = jnp.concatenate([
        dg_pre_mix, dg_post_mix, dg_pre_mlp, dg_post_mlp,
        jnp.concatenate([d_scale, d_ln_g], axis=1), jnp.concatenate([d_ln_b, d_b_dw], axis=1),
        dh0[:N_META], widen(g_wdw), widen(loss_part)], axis=0)
    rep = dict(g_pre_mix=((g_pre_mix, m_g_pre_mix, v_g_pre_mix), (0, 0)),
               g_post_mix=((g_post_mix, m_g_post_mix, v_g_post_mix), (SUBLANES, 0)),
               g_pre_mlp=((g_pre_mlp, m_g_pre_mlp, v_g_pre_mlp), (2 * SUBLANES, 0)),
               g_post_mlp=((g_post_mlp, m_g_post_mlp, v_g_post_mlp), (3 * SUBLANES, 0)),
               pool_scale=((pool_scale, m_pool_scale, v_pool_scale), (4 * SUBLANES, 0)),
               conv_ln_g=((conv_ln_g, m_conv_ln_g, v_conv_ln_g), (4 * SUBLANES, pool_w)),
               conv_ln_b=((conv_ln_b, m_conv_ln_b, v_conv_ln_b), (5 * SUBLANES, 0)),
               b_dw=((b_dw, m_b_dw, v_b_dw), (5 * SUBLANES, conv_w)))
    meta_row0 = 6 * SUBLANES
    wdw_row0 = meta_row0 + N_META
    loss_row0 = wdw_row0 + CONV_TAPS_PADDED
    (small_started,), token = _gather_start([[_fill_slot(packed, slot_idx, F32, "fill_small")]], dh0,
                                            "gather_small_start")
    done = update(pending_mix, token)
    done = update(pending_in, done)
    send, recv, lands = small_started
    fsend, frecv, lands = _gather_forward(lands, recv, done, "gather_small_forward")
    (packed_all,) = _gather_finish(lands, send, recv, fsend, frecv, done, "gather_small_finish")
    rep_names = list(rep)
    wdw_wmv = [jnp.pad(a[0], pad_taps) for a in (w_dw, m_w_dw, v_w_dw)]
    rep_out, meta_out, wdw_out, loss_blk = _small_update(
        slot_idx, packed_all, [rep[k][0] for k in rep_names], [rep[k][1] for k in rep_names],
        (meta, m_meta, v_meta), meta_row0, wdw_wmv, wdw_row0, loss_row0)
    small_out = dict(zip(rep_names, rep_out))
    small_out["meta"] = meta_out
    small_out["w_dw"] = [a[:CONV_KERNEL][None] for a in wdw_out]

    order = ["meta", "g_pre_mix", "w_in", "w_pool_grp", "pool_scale", "w_pool_out", "w_dw", "b_dw", "conv_ln_g",
             "conv_ln_b", "w_conv_out", "w_o", "g_post_mix", "g_pre_mlp", "w_up", "w_down", "g_post_mlp"]
    by_name = {**big_out, **small_out}
    result = [loss_blk[0, 0], grad_x]
    for kind in range(4):
        result += [by_name[k][kind] for k in order]
    return tuple(result)
```

```python
import functools

import jax
import jax.numpy as jnp
from jax import lax
from jax.experimental import pallas as pl
from jax.experimental.pallas import tpu as pltpu

F32 = jnp.float32
BF16 = jnp.bfloat16
MESH = pl.DeviceIdType.MESH

N_DEV = 8
N_META = 16
POOL_WINDOWS = (2, 4, 8, 16)
CONV_KERNEL = 31
CONV_TAPS_PADDED = 32
RMS_EPS = 1e-6
LN_EPS = 1e-5
ADAM_LR = 0.001
ADAM_B1 = 0.9
ADAM_B2 = 0.999
ADAM_EPS = 1e-08
ADAM_WD = 0.01
ADAM_STEP = 10

LANES = 128
SUBLANES = 8
ROW_CHUNK = 128
HALO = 32
VMEM_LIMIT_BYTES = 56 * 1024 * 1024


def _cparams(*sem):
    return pltpu.CompilerParams(dimension_semantics=sem if sem else None, vmem_limit_bytes=VMEM_LIMIT_BYTES)


def _round_up(n, m):
    return (n + m - 1) // m * m


def _row_tile(rows, cols, max_elems=640 * 1024):
    best = None
    for t in range(16, rows + 1, 16):
        if rows % t == 0 and (best is None or t * cols <= max_elems):
            best = t
    assert best is not None, (rows, cols)
    return best


def _rowsum8(a):
    t, w = a.shape
    return a.reshape(t // SUBLANES, SUBLANES, w).sum(axis=0)


def _mesh_pos():
    return lax.axis_index("x"), lax.axis_index("y"), lax.axis_index("c")


HBM_SPEC = pl.BlockSpec(memory_space=pltpu.HBM)
SEM_SPEC = pl.BlockSpec(memory_space=pltpu.SEMAPHORE)
ANY_SPEC = pl.BlockSpec(memory_space=pl.ANY)
_DATAFLOW = pltpu.SideEffectType.DATAFLOW_SIDE_EFFECTING


def _hbm(a):
    return pltpu.with_memory_space_constraint(a, pltpu.HBM)


def _slot(p):
    return 4 * p[0] + 2 * p[1] + p[2]


def _fill_slot(w, slot_idx, dtype, name):
    rows, cols = w.shape
    tr = _row_tile(rows, cols) if rows % 16 == 0 else rows

    def body(idx_ref, w_ref, out_ref):
        out_ref[...] = w_ref[...].astype(dtype)

    return pl.pallas_call(
        body, name=name,
        grid_spec=pltpu.PrefetchScalarGridSpec(
            num_scalar_prefetch=1, grid=(rows // tr,),
            in_specs=[pl.BlockSpec((tr, cols), lambda i, idx_ref: (i, 0))],
            out_specs=pl.BlockSpec((None, tr, cols), lambda i, idx_ref: (idx_ref[0], i, 0))),
        out_shape=jax.ShapeDtypeStruct((N_DEV, rows, cols), dtype),
        compiler_params=_cparams("parallel"),
    )(slot_idx, w)


def _gather_start(groups, after, name):
    flat = [g for grp in groups for g in grp]
    n, n_grp = len(flat), len(groups)

    def body(*refs):
        lands = refs[:n]
        sems = refs[n + 1:n + 1 + 2 * n_grp]
        token = refs[-1]
        x, y, c = _mesh_pos()
        targets = [(x, y, 1 - c), (1 - x, y, c), (x, 1 - y, c), (1 - x, 1 - y, c)]
        t = 0
        for gi, grp in enumerate(groups):
            for ti in range(len(grp)):
                mine = lands[t].at[_slot((x, y, c))]
                for k, to in enumerate(targets):
                    pltpu.make_async_remote_copy(
                        src_ref=mine, dst_ref=mine,
                        send_sem=sems[2 * gi].at[4 * ti + k], recv_sem=sems[2 * gi + 1].at[4 * ti + k],
                        device_id=to, device_id_type=MESH).start()
                t += 1
        token[...] = jnp.zeros_like(token)

    sem_shapes = []
    for grp in groups:
        sem_shapes += [pltpu.SemaphoreType.DMA((4 * len(grp),))] * 2
    outs = pl.pallas_call(
        body, name=name,
        out_shape=tuple(sem_shapes + [pltpu.HBM(g.shape, g.dtype) for g in flat]
                        + [jax.ShapeDtypeStruct((SUBLANES, LANES), F32)]),
        in_specs=tuple([HBM_SPEC] * n + [ANY_SPEC]),
        out_specs=tuple([SEM_SPEC] * (2 * n_grp) + [HBM_SPEC] * n + [pl.BlockSpec(memory_space=pltpu.VMEM)]),
        input_output_aliases={i: 2 * n_grp + i for i in range(n)},
        compiler_params=pltpu.CompilerParams(has_side_effects=_DATAFLOW),
    )(*[_hbm(g) for g in flat], after)
    sems, lands, token = outs[:2 * n_grp], outs[2 * n_grp:-1], outs[-1]
    res, t = [], 0
    for gi, grp in enumerate(groups):
        res.append((sems[2 * gi], sems[2 * gi + 1], list(lands[t:t + len(grp)])))
        t += len(grp)
    return res, token


def _gather_forward(lands, recv_sems, after, name):
    n = len(lands)

    def body(*refs):
        land_refs, recv, _ = refs[:n], refs[n], refs[n + 1]
        fsend, frecv = refs[n + 2], refs[n + 3]
        x, y, c = _mesh_pos()
        chips = [(1 - x, y), (x, 1 - y), (1 - x, 1 - y)]
        for t in range(n):
            for j, chip in enumerate(chips):
                blk = land_refs[t].at[_slot((*chip, c))]
                pltpu.make_async_remote_copy(src_ref=blk, dst_ref=blk, send_sem=fsend.at[3 * t + j],
                                             recv_sem=recv.at[4 * t + 1 + j],
                                             device_id=(x, y, 1 - c), device_id_type=MESH).wait_recv()
                pltpu.make_async_remote_copy(src_ref=blk, dst_ref=blk, send_sem=fsend.at[3 * t + j],
                                             recv_sem=frecv.at[3 * t + j],
                                             device_id=(x, y, 1 - c), device_id_type=MESH).start()

    outs = pl.pallas_call(
        body, name=name,
        out_shape=tuple([pltpu.SemaphoreType.DMA((3 * n,))] * 2 + [pltpu.HBM(g.shape, g.dtype) for g in lands]),
        in_specs=tuple([HBM_SPEC] * n + [SEM_SPEC, ANY_SPEC]),
        out_specs=tuple([SEM_SPEC] * 2 + [HBM_SPEC] * n),
        input_output_aliases={i: 2 + i for i in range(n)},
        compiler_params=pltpu.CompilerParams(has_side_effects=_DATAFLOW),
    )(*lands, recv_sems, after)
    return outs[0], outs[1], list(outs[2:])


def _gather_finish(lands, send_sems, recv_sems, fsend, frecv, after, name):
    n = len(lands)

    def body(*refs):
        land_refs = refs[:n]
        send, recv, fs, fr = refs[n:n + 4]
        x, y, c = _mesh_pos()
        sibling = (x, y, 1 - c)
        chips = [(1 - x, y), (x, 1 - y), (1 - x, 1 - y)]

        def desc(src, dst, s_sem, r_sem):
            return pltpu.make_async_remote_copy(src_ref=src, dst_ref=dst, send_sem=s_sem, recv_sem=r_sem,
                                                device_id=sibling, device_id_type=MESH)

        for t in range(n):
            mine = land_refs[t].at[_slot((x, y, c))]
            desc(mine, land_refs[t].at[_slot(sibling)], send.at[4 * t], recv.at[4 * t]).wait_recv()
            for j, chip in enumerate(chips):
                got = land_refs[t].at[_slot((*chip, 1 - c))]
                desc(got, got, fs.at[3 * t + j], fr.at[3 * t + j]).wait_recv()
            for k in range(4):
                desc(mine, mine, send.at[4 * t + k], recv.at[4 * t + k]).wait_send()
            for j, chip in enumerate(chips):
                blk = land_refs[t].at[_slot((*chip, c))]
                desc(blk, blk, fs.at[3 * t + j], fr.at[3 * t + j]).wait_send()

    outs = pl.pallas_call(
        body, name=name,
        out_shape=tuple(pltpu.HBM(g.shape, g.dtype) for g in lands),
        in_specs=tuple([HBM_SPEC] * n + [SEM_SPEC] * 4 + [ANY_SPEC]),
        out_specs=tuple([HBM_SPEC] * n),
        input_output_aliases={i: i for i in range(n)},
        compiler_params=pltpu.CompilerParams(has_side_effects=_DATAFLOW),
    )(*lands, send_sems, recv_sems, fsend, frecv, after)
    return list(outs)


def _pair_exchange_start(grads, after, name):
    n = len(grads)

    def body(*refs):
        ins, lands = refs[:n], refs[n:2 * n]
        send, recv = refs[2 * n + 1], refs[2 * n + 2]
        token = refs[-1]
        x, y, c = _mesh_pos()
        for t in range(n):
            for q in range(4):
                pltpu.make_async_remote_copy(
                    src_ref=ins[t].at[2 * q + 1 - c], dst_ref=lands[t].at[q],
                    send_sem=send.at[4 * t + q], recv_sem=recv.at[4 * t + q],
                    device_id=(x, y, 1 - c), device_id_type=MESH).start()
        token[...] = jnp.zeros_like(token)

    land_shapes = [(4,) + g.shape[1:] for g in grads]
    outs = pl.pallas_call(
        body, name=name,
        out_shape=tuple([pltpu.SemaphoreType.DMA((4 * n,))] * 2 + [pltpu.HBM(g.shape, g.dtype) for g in grads]
                        + [pltpu.HBM(ls, g.dtype) for ls, g in zip(land_shapes, grads)]
                        + [jax.ShapeDtypeStruct((SUBLANES, LANES), F32)]),
        in_specs=tuple([HBM_SPEC] * (2 * n) + [ANY_SPEC]),
        out_specs=tuple([SEM_SPEC] * 2 + [HBM_SPEC] * (2 * n) + [pl.BlockSpec(memory_space=pltpu.VMEM)]),
        input_output_aliases={i: 2 + i for i in range(2 * n)},
        compiler_params=pltpu.CompilerParams(has_side_effects=_DATAFLOW),
    )(*[_hbm(g) for g in grads], *[_hbm(lax.empty(ls, g.dtype)) for ls, g in zip(land_shapes, grads)], after)
    return outs[0], outs[1], list(outs[2:2 + n]), list(outs[2 + n:2 + 2 * n]), outs[-1]


def _pair_exchange_finish(grads, lands, send_sems, recv_sems, after, name):
    n = len(grads)

    def body(*refs):
        ins, land_refs = refs[:n], refs[n:2 * n]
        send, recv = refs[2 * n], refs[2 * n + 1]
        x, y, c = _mesh_pos()
        for t in range(n):
            for q in range(4):
                cp = pltpu.make_async_remote_copy(
                    src_ref=ins[t].at[q], dst_ref=land_refs[t].at[q], send_sem=send.at[4 * t + q],
                    recv_sem=recv.at[4 * t + q], device_id=(x, y, 1 - c), device_id_type=MESH)
                cp.wait_send()
                cp.wait_recv()

    outs = pl.pallas_call(
        body, name=name,
        out_shape=tuple([pltpu.HBM(g.shape, g.dtype) for g in grads] + [pltpu.HBM(g.shape, g.dtype) for g in lands]),
        in_specs=tuple([HBM_SPEC] * (2 * n) + [SEM_SPEC] * 2 + [ANY_SPEC]),
        out_specs=tuple([HBM_SPEC] * (2 * n)),
        input_output_aliases={i: i for i in range(2 * n)},
        compiler_params=pltpu.CompilerParams(has_side_effects=_DATAFLOW),
    )(*grads, *lands, send_sems, recv_sems, after)
    return list(outs[:n]), list(outs[n:])


def _chip_exchange_start(sums, after, name):
    n = len(sums)

    def body(*refs):
        ins, lands = refs[:n], refs[n:2 * n]
        send, recv = refs[2 * n + 1], refs[2 * n + 2]
        token = refs[-1]
        x, y, c = _mesh_pos()
        chips = [(1 - x, y), (x, 1 - y), (1 - x, 1 - y)]
        for t in range(n):
            for r, chip in enumerate(chips):
                pltpu.make_async_remote_copy(
                    src_ref=ins[t].at[2 * chip[0] + chip[1]], dst_ref=lands[t].at[r],
                    send_sem=send.at[3 * t + r], recv_sem=recv.at[3 * t + r],
                    device_id=(*chip, c), device_id_type=MESH).start()
        token[...] = jnp.zeros_like(token)

    land_shapes = [(3,) + s.shape[1:] for s in sums]
    outs = pl.pallas_call(
        body, name=name,
        out_shape=tuple([pltpu.SemaphoreType.DMA((3 * n,))] * 2 + [pltpu.HBM(s.shape, s.dtype) for s in sums]
                        + [pltpu.HBM(ls, s.dtype) for ls, s in zip(land_shapes, sums)]
                        + [jax.ShapeDtypeStruct((SUBLANES, LANES), F32)]),
        in_specs=tuple([HBM_SPEC] * (2 * n) + [ANY_SPEC]),
        out_specs=tuple([SEM_SPEC] * 2 + [HBM_SPEC] * (2 * n) + [pl.BlockSpec(memory_space=pltpu.VMEM)]),
        input_output_aliases={i: 2 + i for i in range(2 * n)},
        compiler_params=pltpu.CompilerParams(has_side_effects=_DATAFLOW),
    )(*[_hbm(s) for s in sums], *[_hbm(lax.empty(ls, s.dtype)) for ls, s in zip(land_shapes, sums)], after)
    return outs[0], outs[1], list(outs[2:2 + n]), list(outs[2 + n:2 + 2 * n]), outs[-1]


def _chip_exchange_finish(sums, lands, send_sems, recv_sems, after, name):
    n = len(sums)

    def body(*refs):
        ins, land_refs = refs[:n], refs[n:2 * n]
        send, recv = refs[2 * n], refs[2 * n + 1]
        x, y, c = _mesh_pos()
        for t in range(n):
            for r in range(3):
                cp = pltpu.make_async_remote_copy(
                    src_ref=ins[t].at[r], dst_ref=land_refs[t].at[r], send_sem=send.at[3 * t + r],
                    recv_sem=recv.at[3 * t + r],
                    device_id=(x, y, 1 - c), device_id_type=MESH)
                cp.wait_send()
                cp.wait_recv()

    outs = pl.pallas_call(
        body, name=name,
        out_shape=tuple(pltpu.HBM(g.shape, g.dtype) for g in lands),
        in_specs=tuple([HBM_SPEC] * (2 * n) + [SEM_SPEC] * 2 + [ANY_SPEC]),
        out_specs=tuple([HBM_SPEC] * n),
        input_output_aliases={n + i: i for i in range(n)},
        compiler_params=pltpu.CompilerParams(has_side_effects=_DATAFLOW),
    )(*sums, *lands, send_sems, recv_sems, after)
    return list(outs)


def _mm(a, b, *, mode, tm, tn, tk=None, b_blocked=False, out_blocked=False, out_dtypes=(F32,),
        epilogue=None, extras=(), after=None, kb=1, name):
    if mode == "nn":
        m, k = a.shape
        n = b.shape[0] * b.shape[2] if b_blocked else b.shape[1]
        dims = (((1,), (0,)), ((), ()))
    elif mode == "nt":
        m, k = a.shape
        n = b.shape[1] if b_blocked else b.shape[0]
        if b_blocked:
            tk = kb * b.shape[2]
        dims = (((1,), (1,)), ((), ()))
    else:
        k, m = a.shape
        n = b.shape[1]
        dims = (((0,), (0,)), ((), ()))
    tk = k if tk is None else tk
    assert m % tm == 0 and n % tn == 0 and k % tk == 0, (name, m, n, k, tm, tn, tk)
    gm, gn, gk = m // tm, n // tn, k // tk
    if b_blocked:
        assert (tn if mode == "nn" else tk) == kb * b.shape[2], name

    if mode == "nn":
        a_spec = pl.BlockSpec((tm, tk), lambda i, j, kk: (i, kk))
        b_spec = (pl.BlockSpec((None, tk, tn), lambda i, j, kk: (j, kk, 0)) if b_blocked
                  else pl.BlockSpec((tk, tn), lambda i, j, kk: (kk, j)))
    elif mode == "nt":
        a_spec = pl.BlockSpec((tm, tk), lambda i, j, kk: (i, kk))
        b_spec = (pl.BlockSpec((kb, tn, tk // kb), lambda i, j, kk: (kk, j, 0)) if b_blocked
                  else pl.BlockSpec((tn, tk), lambda i, j, kk: (j, kk)))
    else:
        a_spec = pl.BlockSpec((tk, tm), lambda i, j, kk: (kk, i))
        b_spec = pl.BlockSpec((tk, tn), lambda i, j, kk: (kk, j))
    if out_blocked:
        out_spec = pl.BlockSpec((None, tm, tn), lambda i, j, kk: (j, i, 0))
        out_shape = (gn, m, tn)
    else:
        out_spec = pl.BlockSpec((tm, tn), lambda i, j, kk: (i, j))
        out_shape = (m, n)
    extra_specs = [pl.BlockSpec((tm, tn), functools.partial(lambda i, j, kk, off: (i, j + off), off=off))
                   for _, off in extras]
    n_extra, n_out = len(extras), len(out_dtypes)
    n_after = 0 if after is None else 1

    def body(a_ref, b_ref, *rest):
        extra_refs = rest[:n_extra]
        out_refs = rest[n_extra + n_after:n_extra + n_after + n_out]

        def finish(acc):
            if epilogue is None:
                res = (acc,)
            else:
                res = epilogue(acc, *[e[...] for e in extra_refs])
            for o_ref, r in zip(out_refs, res):
                o_ref[...] = r.astype(o_ref.dtype)

        if mode == "nt" and b_blocked:
            bk = tk // kb
            part = lax.dot_general(a_ref[:, :bk], b_ref[0], dims, preferred_element_type=F32)
            for h in range(1, kb):
                part = part + lax.dot_general(a_ref[:, h * bk:(h + 1) * bk], b_ref[h], dims,
                                              preferred_element_type=F32)
        else:
            part = lax.dot_general(a_ref[...], b_ref[...], dims, preferred_element_type=F32)
        if gk == 1:
            finish(part)
        else:
            acc_ref = rest[-1]
            kk = pl.program_id(2)

            @pl.when(kk == 0)
            def _():
                acc_ref[...] = part

            @pl.when(kk > 0)
            def _():
                acc_ref[...] += part

            @pl.when(kk == gk - 1)
            def _():
                finish(acc_ref[...])

    outs = pl.pallas_call(
        body, name=name, grid=(gm, gn, gk),
        in_specs=[a_spec, b_spec] + extra_specs + [ANY_SPEC] * n_after,
        out_specs=[out_spec] * n_out,
        out_shape=[jax.ShapeDtypeStruct(out_shape, d) for d in out_dtypes],
        scratch_shapes=[pltpu.VMEM((tm, tn), F32)] if gk > 1 else [],
        compiler_params=_cparams("parallel", "parallel", "arbitrary"),
    )(a, b, *[e for e, _ in extras], *([] if after is None else [after]))
    return outs[0] if n_out == 1 else outs


def _gate_mix(ya_pre, s, wpo, wco, proj, d_model, name):
    lp, width = ya_pre.shape
    nb, _, bn = wpo.shape
    ga_off = (proj.shape[1] - 2 * d_model) // bn
    gb_off = (proj.shape[1] - d_model) // bn

    def body(ya_ref, s_ref, wpo_ref, wco_ref, ga_ref, gb_ref, m_ref, y_a_ref, y_b_ref):
        y_a = jnp.dot(ya_ref[...], wpo_ref[...], preferred_element_type=F32)
        y_b = jnp.dot(s_ref[...], wco_ref[...], preferred_element_type=F32)
        m = jax.nn.sigmoid(ga_ref[...]) * y_a + jax.nn.sigmoid(gb_ref[...]) * y_b
        m_ref[...] = m.astype(BF16)
        y_a_ref[...] = y_a.astype(BF16)
        y_b_ref[...] = y_b.astype(BF16)

    act_spec = pl.BlockSpec((lp, width), lambda j: (0, 0))
    w_spec = pl.BlockSpec((None, width, bn), lambda j: (j, 0, 0))
    out_spec = pl.BlockSpec((lp, bn), lambda j: (0, j))
    return pl.pallas_call(
        body, name=name, grid=(nb,),
        in_specs=[act_spec, act_spec, w_spec, w_spec,
                  pl.BlockSpec((lp, bn), lambda j: (0, j + ga_off)),
                  pl.BlockSpec((lp, bn), lambda j: (0, j + gb_off))],
        out_specs=[out_spec] * 3,
        out_shape=[jax.ShapeDtypeStruct((lp, nb * bn), BF16)] * 3,
        compiler_params=_cparams("parallel"),
    )(ya_pre, s, wpo, wco, proj, proj)


def _gate_bwd(do, w_o, proj, y_a, y_b, *, tm, tn, after, name):
    m, d = do.shape
    n_j = d // tn
    gate_off = (proj.shape[1] - 2 * d) // tn

    def body(do_ref, w_ref, gate_ref, ya_ref, yb_ref, _, dgate_ref, dya_ref, dyb_ref, dm_ref):
        branch = pl.program_id(2)

        @pl.when(branch == 0)
        def _():
            dm_ref[...] = lax.dot_general(do_ref[...], w_ref[...], (((1,), (1,)), ((), ())),
                                          preferred_element_type=F32)

        dm = dm_ref[...]
        sig = jax.nn.sigmoid(gate_ref[...])
        y = jnp.where(branch == 0, ya_ref[...], yb_ref[...]).astype(F32)
        dgate_ref[...] = (dm * y * sig * (1.0 - sig)).astype(BF16)
        dy = (dm * sig).astype(BF16)

        @pl.when(branch == 0)
        def _():
            dya_ref[...] = dy

        @pl.when(branch == 1)
        def _():
            dyb_ref[...] = dy

    gate_spec = pl.BlockSpec((tm, tn), lambda i, j, b: (i, gate_off + b * n_j + j))
    tile = pl.BlockSpec((tm, tn), lambda i, j, b: (i, j))
    return pl.pallas_call(
        body, name=name, grid=(m // tm, n_j, 2),
        in_specs=[pl.BlockSpec((tm, d), lambda i, j, b: (i, 0)), pl.BlockSpec((tn, d), lambda i, j, b: (j, 0)),
                  gate_spec, tile, tile, ANY_SPEC],
        out_specs=[gate_spec, tile, tile],
        out_shape=[jax.ShapeDtypeStruct(proj.shape, BF16), jax.ShapeDtypeStruct((m, d), BF16),
                   jax.ShapeDtypeStruct((m, d), BF16)],
        scratch_shapes=[pltpu.VMEM((tm, tn), F32)],
        compiler_params=_cparams("parallel", "arbitrary", "arbitrary"),
    )(do, w_o, proj, y_a, y_b, after)


def _rms_stats(x):
    return lax.rsqrt(jnp.mean(x * x, axis=-1, keepdims=True) + RMS_EPS)


def _rms_bwd(x, g, dy):
    r = _rms_stats(x)
    nrm = x * r
    dn = dy * g
    dx = r * (dn - nrm * jnp.mean(dn * nrm, axis=-1, keepdims=True))
    return dx, dy * nrm


def _rowwise(body, ins, outs, accs, *, lp, name):
    tr = _row_tile(lp, max(a.shape[1] for a in ins))
    n_in, n_out, n_acc = len(ins), len(outs), len(accs)

    def kernel_body(*refs):
        i = pl.program_id(0)
        acc_refs = refs[n_in + n_out:]

        @pl.when(i == 0)
        def _():
            for r in acc_refs:
                r[...] = jnp.zeros_like(r)

        body(i * tr, refs[:n_in], refs[n_in:n_in + n_out], acc_refs)

    in_specs = []
    for a in ins:
        if a.shape[0] == lp:
            in_specs.append(pl.BlockSpec((tr, a.shape[1]), lambda i: (i, 0)))
        else:
            in_specs.append(pl.BlockSpec(a.shape, lambda i: (0, 0)))
    out_specs = [pl.BlockSpec((tr, w), lambda i: (i, 0)) for w, _ in outs]
    out_specs += [pl.BlockSpec((SUBLANES, w), lambda i: (0, 0)) for w in accs]
    out_shape = [jax.ShapeDtypeStruct((lp, w), d) for w, d in outs]
    out_shape += [jax.ShapeDtypeStruct((SUBLANES, w), F32) for w in accs]
    return pl.pallas_call(
        kernel_body, name=name, grid=(lp // tr,), in_specs=in_specs, out_specs=out_specs,
        out_shape=out_shape, compiler_params=_cparams("arbitrary"),
    )(*ins)


def _rms_pre(h0, g, lp):
    d = h0.shape[1]

    def body(row0, ins, outs, accs):
        h_ref, g_ref = ins
        x = h_ref[...]
        outs[0][...] = (x * _rms_stats(x) * g_ref[...]).astype(BF16)

    return _rowwise(body, [h0, g], [(d, BF16)], [], lp=lp, name="rms_pre")[0]


def _post_mix(o, h0, g_post_mix, g_pre_mlp, lp):
    d = h0.shape[1]

    def body(row0, ins, outs, accs):
        o_ref, h0_ref, g1_ref, g2_ref = ins
        o_v = o_ref[...]
        h1 = h0_ref[...] + o_v * _rms_stats(o_v) * g1_ref[...]
        outs[0][...] = h1
        outs[1][...] = (h1 * _rms_stats(h1) * g2_ref[...]).astype(BF16)

    return _rowwise(body, [o, h0, g_post_mix, g_pre_mlp], [(d, F32), (d, BF16)], [], lp=lp, name="post_mix")


def _loss_head(f, h1, target, g_post_mlp, lp, seq):
    d = f.shape[1]

    def body(row0, ins, outs, accs):
        f_ref, h1_ref, t_ref, g_ref = ins
        df_ref, dh_ref = outs
        dg_ref, loss_ref = accs
        f_v, g = f_ref[...], g_ref[...]
        r = _rms_stats(f_v)
        nrm = f_v * r
        rows = row0 + lax.broadcasted_iota(jnp.int32, (f_v.shape[0], 1), 0)
        valid = (rows >= N_META) & (rows < N_META + seq)
        err = jnp.where(valid, h1_ref[...] + nrm * g - t_ref[...], 0.0)
        loss_ref[...] += 0.5 * jnp.sum(jnp.mean(err * err, axis=-1, keepdims=True))
        dy = err * (1.0 / d)
        dn = dy * g
        df_ref[...] = (r * (dn - nrm * jnp.mean(dn * nrm, axis=-1, keepdims=True))).astype(BF16)
        dh_ref[...] = dy
        dg_ref[...] += _rowsum8(dy * nrm)

    return _rowwise(body, [f, h1, target, g_post_mlp], [(d, BF16), (d, F32)], [d, LANES], lp=lp, name="loss_head")


def _mid_bwd(du2, h1, dh, o, g_pre_mlp, g_post_mix, lp):
    d = h1.shape[1]

    def body(row0, ins, outs, accs):
        du2_ref, h1_ref, dh_ref, o_ref, g2_ref, g1_ref = ins
        dx2, dg2 = _rms_bwd(h1_ref[...], g2_ref[...], du2_ref[...])
        dh1 = dh_ref[...] + dx2
        do, dg1 = _rms_bwd(o_ref[...], g1_ref[...], dh1)
        outs[0][...] = dh1
        outs[1][...] = do.astype(BF16)
        accs[0][...] += _rowsum8(dg2)
        accs[1][...] += _rowsum8(dg1)

    return _rowwise(body, [du2, h1, dh, o, g_pre_mlp, g_post_mix], [(d, F32), (d, BF16)], [d, d], lp=lp,
                    name="mid_bwd")


def _pre_mix_bwd(du1, h0, dh1, g_pre_mix, lp):
    d = h0.shape[1]

    def body(row0, ins, outs, accs):
        du1_ref, h0_ref, dh1_ref, g_ref = ins
        dx, dg = _rms_bwd(h0_ref[...], g_ref[...], du1_ref[...])
        outs[0][...] = dh1_ref[...] + dx
        accs[0][...] += _rowsum8(dg)

    return _rowwise(body, [du1, h0, dh1, g_pre_mix], [(d, F32)], [d], lp=lp, name="pre_mix_bwd")


def _ln_stats(c):
    mu = jnp.mean(c, axis=-1, keepdims=True)
    var = jnp.mean(jnp.square(c - mu), axis=-1, keepdims=True)
    return mu, lax.rsqrt(var + LN_EPS)


def _ln_silu(c, ln_g, ln_b, lp):
    w = c.shape[1]

    def body(row0, ins, outs, accs):
        c_ref, g_ref, b_ref = ins
        c_v = c_ref[...]
        mu, rstd = _ln_stats(c_v)
        ln = (c_v - mu) * rstd * g_ref[...] + b_ref[...]
        outs[0][...] = (ln * jax.nn.sigmoid(ln)).astype(BF16)

    return _rowwise(body, [c, ln_g, ln_b], [(w, BF16)], [], lp=lp, name="ln_silu")[0]


def _ln_silu_bwd(c, ds, ln_g, ln_b, lp):
    w = c.shape[1]

    def body(row0, ins, outs, accs):
        c_ref, ds_ref, g_ref, b_ref = ins
        c_v, g = c_ref[...], g_ref[...]
        mu, rstd = _ln_stats(c_v)
        nrm = (c_v - mu) * rstd
        ln = nrm * g + b_ref[...]
        sig = jax.nn.sigmoid(ln)
        dln = ds_ref[...] * (sig * (1.0 + ln * (1.0 - sig)))
        dn = dln * g
        dc = rstd * (dn - jnp.mean(dn, axis=-1, keepdims=True) - nrm * jnp.mean(dn * nrm, axis=-1, keepdims=True))
        outs[0][...] = dc
        accs[0][...] += _rowsum8(dln * nrm)
        accs[1][...] += _rowsum8(dln)
        accs[2][...] += _rowsum8(dc)

    return _rowwise(body, [c, ds, ln_g, ln_b], [(w, F32)], [w, w, w], lp=lp, name="ln_silu_bwd")


def _chunk_with_history(ref, i, cols=slice(None)):
    t0 = pl.multiple_of(i * ROW_CHUNK, ROW_CHUNK)
    lo0 = pl.multiple_of(jnp.maximum(t0 - HALO, 0), SUBLANES)
    lo = jnp.where(i > 0, ref[pl.ds(lo0, HALO), cols], 0.0)
    return jnp.concatenate([lo, ref[pl.ds(t0, ROW_CHUNK), cols]], axis=0)


def _chunk_with_future(ref, i, n_chunks, cols=slice(None)):
    t0 = pl.multiple_of(i * ROW_CHUNK, ROW_CHUNK)
    hi0 = pl.multiple_of(jnp.minimum(t0 + ROW_CHUNK, (n_chunks - 1) * ROW_CHUNK), SUBLANES)
    hi = jnp.where(i < n_chunks - 1, ref[pl.ds(hi0, HALO), cols], 0.0)
    return jnp.concatenate([ref[pl.ds(t0, ROW_CHUNK), cols], hi], axis=0)


def _inv_count(t0, n_rows, window):
    pos = t0 + lax.broadcasted_iota(jnp.int32, (n_rows, 1), 0)
    return 1.0 / jnp.minimum(pos + 1, window).astype(F32)


def _pool_delta(z_hist, t0, window):
    s = z_hist
    sh = 1
    while sh < window:
        s = s + pltpu.roll(s, sh, 0)
        sh *= 2
    cur = z_hist[HALO:, :]
    return s[HALO:, :] * _inv_count(t0, ROW_CHUNK, window) - cur


def _pool_fwd(proj, wpg, pool_scale, lp):
    n_grp, gdim, _ = wpg.shape
    width = n_grp * gdim
    n_chunks = lp // ROW_CHUNK

    def body(z_ref, w_ref, sc_ref, out_ref):
        for g, window in enumerate(POOL_WINDOWS):
            cols = slice(g * gdim, (g + 1) * gdim)

            def chunk(i, carry, cols=cols, g=g, window=window):
                t0 = pl.multiple_of(i * ROW_CHUNK, ROW_CHUNK)
                d = _pool_delta(_chunk_with_history(z_ref, i, cols), t0, window)
                q = jnp.dot(d.astype(BF16), w_ref[g], preferred_element_type=F32)
                out_ref[pl.ds(t0, ROW_CHUNK), cols] = (q * sc_ref[:, cols]).astype(BF16)
                return carry

            lax.fori_loop(0, n_chunks, chunk, 0)

    return pl.pallas_call(
        body, name="pool_fwd", grid=(1,),
        in_specs=[pl.BlockSpec((lp, width), lambda i: (0, 0)),
                  pl.BlockSpec(wpg.shape, lambda i: (0, 0, 0)),
                  pl.BlockSpec(pool_scale.shape, lambda i: (0, 0))],
        out_specs=pl.BlockSpec((lp, width), lambda i: (0, 0)),
        out_shape=jax.ShapeDtypeStruct((lp, width), BF16),
        compiler_params=_cparams("arbitrary"),
    )(proj, wpg, pool_scale)


def _pool_bwd(proj, d_ya, wpg, pool_scale, dproj, lp):
    n_grp, gdim, _ = wpg.shape
    width = n_grp * gdim
    n_chunks = lp // ROW_CHUNK
    ext = ROW_CHUNK + HALO

    def body(z_ref, dya_ref, w_ref, sc_ref, _, dz_ref, dw_ref, dsc_ref):
        dw_ref[...] = jnp.zeros_like(dw_ref)
        dsc_ref[...] = jnp.zeros_like(dsc_ref)
        for g, window in enumerate(POOL_WINDOWS):
            cols = slice(g * gdim, (g + 1) * gdim)

            def chunk(i, carry, cols=cols, g=g, window=window):
                t0 = pl.multiple_of(i * ROW_CHUNK, ROW_CHUNK)
                w_g = w_ref[g]
                scale = sc_ref[:, cols]
                d = _pool_delta(_chunk_with_history(z_ref, i, cols), t0, window).astype(BF16)
                dya_ext = _chunk_with_future(dya_ref, i, n_chunks, cols)
                dya = dya_ext[:ROW_CHUNK, :]
                q = jnp.dot(d, w_g, preferred_element_type=F32)
                dsc_ref[:, cols] += _rowsum8(dya * q)
                e_ext = (dya_ext * scale).astype(BF16)
                dw_ref[g] += lax.dot_general(d, e_ext[:ROW_CHUNK, :], (((0,), (0,)), ((), ())),
                                             preferred_element_type=F32)
                dd_ext = lax.dot_general(e_ext, w_g, (((1,), (1,)), ((), ())), preferred_element_type=F32)
                s = dd_ext * _inv_count(t0, ext, window)
                sh = 1
                while sh < window:
                    s = s + pltpu.roll(s, ext - sh, 0)
                    sh *= 2
                dz_ref[pl.ds(t0, ROW_CHUNK), cols] = (s[:ROW_CHUNK, :] - dd_ext[:ROW_CHUNK, :]).astype(BF16)
                return carry

            lax.fori_loop(0, n_chunks, chunk, 0)

    blk = pl.BlockSpec((lp, width), lambda i: (0, 0))
    return pl.pallas_call(
        body, name="pool_bwd", grid=(1,),
        in_specs=[blk, blk, pl.BlockSpec(wpg.shape, lambda i: (0, 0, 0)),
                  pl.BlockSpec(pool_scale.shape, lambda i: (0, 0)), ANY_SPEC],
        out_specs=[blk, pl.BlockSpec(wpg.shape, lambda i: (0, 0, 0)),
                   pl.BlockSpec((SUBLANES, width), lambda i: (0, 0))],
        out_shape=[jax.ShapeDtypeStruct(dproj.shape, BF16), jax.ShapeDtypeStruct(wpg.shape, F32),
                   jax.ShapeDtypeStruct((SUBLANES, width), F32)],
        input_output_aliases={4: 0},
        compiler_params=_cparams("arbitrary"),
    )(proj, d_ya, wpg, pool_scale, dproj)


def _conv_fwd(proj, w_dw, b_dw, lp, width, v_col0):
    n_chunks = lp // ROW_CHUNK
    v_blk0, g_blk0 = v_col0 // LANES, (v_col0 + width) // LANES

    def body(v_ref, gc_ref, w_ref, b_ref, c_ref, a_pad):
        a_pad[pl.ds(0, HALO), :] = jnp.zeros((HALO, LANES), F32)
        a_pad[pl.ds(HALO, lp), :] = v_ref[...] * jax.nn.sigmoid(gc_ref[...])

        def chunk(i, carry):
            t0 = pl.multiple_of(i * ROW_CHUNK, ROW_CHUNK)
            hist = a_pad[pl.ds(t0, ROW_CHUNK + HALO), :]
            acc = jnp.zeros((ROW_CHUNK, LANES), F32)
            for k in range(CONV_KERNEL):
                acc = acc + w_ref[k:k + 1, :] * pltpu.roll(hist, CONV_KERNEL - 1 - k, 0)[HALO:, :]
            c_ref[pl.ds(t0, ROW_CHUNK), :] = acc + b_ref[...]
            return carry

        lax.fori_loop(0, n_chunks, chunk, 0)

    return pl.pallas_call(
        body, name="conv_fwd", grid=(width // LANES,),
        in_specs=[pl.BlockSpec((lp, LANES), lambda j: (0, j + v_blk0)),
                  pl.BlockSpec((lp, LANES), lambda j: (0, j + g_blk0)),
                  pl.BlockSpec((CONV_TAPS_PADDED, LANES), lambda j: (0, j)),
                  pl.BlockSpec((1, LANES), lambda j: (0, j))],
        out_specs=pl.BlockSpec((lp, LANES), lambda j: (0, j)),
        out_shape=jax.ShapeDtypeStruct((lp, width), F32),
        scratch_shapes=[pltpu.VMEM((lp + HALO, LANES), F32)],
        compiler_params=_cparams("parallel"),
    )(proj, proj, w_dw, b_dw)


def _conv_bwd(proj, dc, w_dw, dproj, lp, width, v_col0):
    n_chunks = lp // ROW_CHUNK
    ext = ROW_CHUNK + HALO
    v_blk0, g_blk0 = v_col0 // LANES, (v_col0 + width) // LANES
    n_tiles = width // LANES

    def body(v_ref, gc_ref, dc_ref, w_ref, _, out_ref, dw_ref, a_pad, dc_pad, dw_acc, dgc_stash):
        part = pl.program_id(1)

        @pl.when(part == 0)
        def _():
            sig = jax.nn.sigmoid(gc_ref[...])
            a_pad[pl.ds(0, HALO), :] = jnp.zeros((HALO, LANES), F32)
            a_pad[pl.ds(HALO, lp), :] = v_ref[...] * sig
            dc_pad[pl.ds(0, lp), :] = dc_ref[...]
            dc_pad[pl.ds(lp, HALO), :] = jnp.zeros((HALO, LANES), F32)
            dw_acc[...] = jnp.zeros_like(dw_acc)

            def chunk(i, carry):
                t0 = pl.multiple_of(i * ROW_CHUNK, ROW_CHUNK)
                hist = a_pad[pl.ds(t0, ext), :]
                fut = dc_pad[pl.ds(t0, ext), :]
                dc_cur = fut[:ROW_CHUNK, :]
                da = jnp.zeros((ROW_CHUNK, LANES), F32)
                for k in range(CONV_KERNEL):
                    lag = CONV_KERNEL - 1 - k
                    da = da + w_ref[k:k + 1, :] * pltpu.roll(fut, (ext - lag) % ext, 0)[:ROW_CHUNK, :]
                    dw_acc[pl.ds(SUBLANES * k, SUBLANES), :] += _rowsum8(
                        dc_cur * pltpu.roll(hist, lag, 0)[HALO:, :])
                rows = pl.ds(t0, ROW_CHUNK)
                sg = jax.nn.sigmoid(gc_ref[rows, :])
                out_ref[rows, :] = (da * sg).astype(BF16)
                dgc_stash[rows, :] = (da * v_ref[rows, :] * sg * (1.0 - sg)).astype(BF16)
                return carry

            lax.fori_loop(0, n_chunks, chunk, 0)
            dw_ref[...] = dw_acc[...].reshape(CONV_TAPS_PADDED, SUBLANES, LANES).sum(axis=1)

        @pl.when(part == 1)
        def _():
            out_ref[...] = dgc_stash[...]

    col = lambda j, p: (0, j)
    return pl.pallas_call(
        body, name="conv_bwd", grid=(n_tiles, 2),
        in_specs=[pl.BlockSpec((lp, LANES), lambda j, p: (0, j + v_blk0)),
                  pl.BlockSpec((lp, LANES), lambda j, p: (0, j + g_blk0)),
                  pl.BlockSpec((lp, LANES), col),
                  pl.BlockSpec((CONV_TAPS_PADDED, LANES), col), ANY_SPEC],
        out_specs=[pl.BlockSpec((lp, LANES), lambda j, p: (0, v_blk0 + p * n_tiles + j)),
                   pl.BlockSpec((CONV_TAPS_PADDED, LANES), col)],
        out_shape=[jax.ShapeDtypeStruct(dproj.shape, BF16), jax.ShapeDtypeStruct((CONV_TAPS_PADDED, width), F32)],
        scratch_shapes=[pltpu.VMEM((lp + HALO, LANES), F32), pltpu.VMEM((lp + HALO, LANES), F32),
                        pltpu.VMEM((CONV_TAPS_PADDED * SUBLANES, LANES), F32), pltpu.VMEM((lp, LANES), BF16)],
        input_output_aliases={4: 0},
        compiler_params=_cparams("arbitrary", "arbitrary"),
    )(proj, proj, dc, w_dw, dproj)


def _adamw_math(w, g, m, v):
    m = ADAM_B1 * m + (1.0 - ADAM_B1) * g
    v = ADAM_B2 * v + (1.0 - ADAM_B2) * jnp.square(g)
    m_hat = m / (1.0 - ADAM_B1 ** ADAM_STEP)
    v_hat = v / (1.0 - ADAM_B2 ** ADAM_STEP)
    delta = -ADAM_LR * (m_hat / (jnp.sqrt(v_hat) + ADAM_EPS) + ADAM_WD * w)
    return delta, m, v


def _pair_sum(own, recv, c_idx, name):
    _, _, rows, cols = own.shape
    tr = _row_tile(rows, cols, 1024 * 1024)

    def body(c_ref, own_ref, recv_ref, out_ref):
        out_ref[...] = (own_ref[...].astype(F32) + recv_ref[...].astype(F32)).astype(BF16)

    return pl.pallas_call(
        body, name=name,
        grid_spec=pltpu.PrefetchScalarGridSpec(
            num_scalar_prefetch=1, grid=(4, rows // tr),
            in_specs=[pl.BlockSpec((None, None, tr, cols), lambda q, i, c_ref: (q, c_ref[0], i, 0)),
                      pl.BlockSpec((None, tr, cols), lambda q, i, c_ref: (q, i, 0))],
            out_specs=pl.BlockSpec((None, tr, cols), lambda q, i, c_ref: (q, i, 0))),
        out_shape=jax.ShapeDtypeStruct((4, rows, cols), BF16),
        compiler_params=_cparams("parallel", "parallel"),
    )(c_idx, own, recv)


def _adamw_big(w, m, v, chip_sum, recv3, chip_idx, name):
    rows, cols = w.shape
    tr = _row_tile(rows, cols, 256 * 1024)

    def body(q_ref, w_ref, m_ref, v_ref, own_ref, r_ref, g_out, d_out, m_out, v_out):
        g = own_ref[...].astype(F32)
        for r in range(3):
            g = g + r_ref[r].astype(F32)
        delta, m_new, v_new = _adamw_math(w_ref[...], g, m_ref[...], v_ref[...])
        g_out[...] = g
        d_out[...] = delta
        m_out[...] = m_new
        v_out[...] = v_new

    blk = pl.BlockSpec((tr, cols), lambda i, q_ref: (i, 0))
    return pl.pallas_call(
        body, name=name,
        grid_spec=pltpu.PrefetchScalarGridSpec(
            num_scalar_prefetch=1, grid=(rows // tr,),
            in_specs=[blk, blk, blk,
                      pl.BlockSpec((None, tr, cols), lambda i, q_ref: (q_ref[0], i, 0)),
                      pl.BlockSpec((3, tr, cols), lambda i, q_ref: (0, i, 0))],
            out_specs=[blk] * 4),
        out_shape=[jax.ShapeDtypeStruct((rows, cols), F32)] * 4,
        compiler_params=_cparams("parallel"),
    )(chip_idx, w, m, v, chip_sum, recv3)


def _small_update(me_idx, packed, rep_params, rep_places, meta_wmv, meta_row0, wdw_wmv, wdw_row0, loss_row0):
    n_rep = len(rep_params)
    meta_rows, meta_cols = meta_wmv[0].shape
    wdw_rows, wdw_cols = wdw_wmv[0].shape

    def body(me_ref, *refs):
        pos = 0

        def take(k):
            nonlocal pos
            out = refs[pos:pos + k]
            pos += k
            return out

        rep_in = [take(3) for _ in range(n_rep)]
        rep_g = take(n_rep)
        meta_in, (meta_g,) = take(3), take(1)
        wdw_in, (wdw_g,) = take(3), take(1)
        (loss_ref,) = take(1)
        rep_out = [take(4) for _ in range(n_rep)]
        meta_out, wdw_out = take(4), take(4)
        (loss_out,) = take(1)

        def update(wmv, g, outs):
            delta, m_new, v_new = _adamw_math(wmv[0][...], g, wmv[1][...], wmv[2][...])
            for o_ref, val in zip(outs, (g, delta, m_new, v_new)):
                o_ref[...] = val

        for wmv, g_ref, outs in zip(rep_in, rep_g, rep_out):
            g = jnp.sum(g_ref[0], axis=0, keepdims=True)
            for j in range(1, N_DEV):
                g = g + jnp.sum(g_ref[j], axis=0, keepdims=True)
            update(wmv, g, outs)
        for wmv, g_ref, outs in ((meta_in, meta_g, meta_out), (wdw_in, wdw_g, wdw_out)):
            g = g_ref[0]
            for j in range(1, N_DEV):
                g = g + g_ref[j]
            update(wmv, g, outs)
        total = loss_ref[0]
        for j in range(1, N_DEV):
            total = total + loss_ref[j]
        loss_out[...] = total

    def whole(a):
        nd = a.ndim
        return pl.BlockSpec(a.shape, lambda i, me_ref, nd=nd: (0,) * nd)

    ins, in_specs = [], []
    for wmv in rep_params:
        ins += list(wmv)
        in_specs += [whole(a) for a in wmv]
    for wmv, (row0, col0) in zip(rep_params, rep_places):
        width = wmv[0].shape[1]
        ins.append(packed)
        in_specs.append(pl.BlockSpec((N_DEV, SUBLANES, width),
                                     lambda i, me_ref, rb=row0 // SUBLANES, cb=col0 // width: (0, rb, cb)))
    ins += list(meta_wmv) + [packed]
    in_specs += [whole(a) for a in meta_wmv]
    in_specs.append(pl.BlockSpec((N_DEV, meta_rows, meta_cols),
                                 lambda i, me_ref, rb=meta_row0 // meta_rows: (0, rb, me_ref[0])))
    ins += list(wdw_wmv) + [packed]
    in_specs += [whole(a) for a in wdw_wmv]
    in_specs.append(pl.BlockSpec((N_DEV, wdw_rows, wdw_cols),
                                 lambda i, me_ref, rb=wdw_row0 // wdw_rows: (0, rb, me_ref[0])))
    ins.append(packed)
    in_specs.append(pl.BlockSpec((N_DEV, SUBLANES, LANES), lambda i, me_ref, rb=loss_row0 // SUBLANES: (0, rb, 0)))

    out_shape, out_specs = [], []
    for wmv in list(rep_params) + [meta_wmv, wdw_wmv]:
        out_shape += [jax.ShapeDtypeStruct(wmv[0].shape, F32)] * 4
        out_specs += [whole(wmv[0])] * 4
    out_shape.append(jax.ShapeDtypeStruct((SUBLANES, LANES), F32))
    out_specs.append(pl.BlockSpec((SUBLANES, LANES), lambda i, me_ref: (0, 0)))

    outs = pl.pallas_call(
        body, name="small_update",
        grid_spec=pltpu.PrefetchScalarGridSpec(num_scalar_prefetch=1, grid=(1,), in_specs=in_specs,
                                               out_specs=out_specs),
        out_shape=out_shape, compiler_params=_cparams("arbitrary"),
    )(me_idx, *ins)
    groups = [outs[4 * i:4 * i + 4] for i in range(n_rep + 2)]
    return groups[:n_rep], groups[n_rep], groups[n_rep + 1], outs[-1]


def kernel(x, meta, g_pre_mix, w_in, w_pool_grp, pool_scale, w_pool_out, w_dw, b_dw, conv_ln_g, conv_ln_b, w_conv_out, w_o, g_post_mix, g_pre_mlp, w_up, w_down, g_post_mlp, loss_target, m_meta, m_g_pre_mix, m_w_in, m_w_pool_grp, m_pool_scale, m_w_pool_out, m_w_dw, m_b_dw, m_conv_ln_g, m_conv_ln_b, m_w_conv_out, m_w_o, m_g_post_mix, m_g_pre_mlp, m_w_up, m_w_down, m_g_post_mlp, v_meta, v_g_pre_mix, v_w_in, v_w_pool_grp, v_pool_scale, v_w_pool_out, v_w_dw, v_b_dw, v_conv_ln_g, v_conv_ln_b, v_w_conv_out, v_w_o, v_g_post_mix, v_g_pre_mlp, v_w_up, v_w_down, v_g_post_mlp):
    seq, d = x.shape[1], x.shape[2]
    pool_w = pool_scale.shape[1]
    conv_w = b_dw.shape[1]
    n_grp, grp_rows, gdim = w_pool_grp.shape[1:]
    lp = _round_up(N_META + seq, ROW_CHUNK)
    tm_half = lp // 2 if (lp // 2) % 16 == 0 else lp
    c_idx = lax.axis_index("c").astype(jnp.int32)
    chip_idx = (2 * lax.axis_index("x") + lax.axis_index("y")).astype(jnp.int32)
    me_idx = 2 * chip_idx + c_idx

    pad_taps = ((0, CONV_TAPS_PADDED - CONV_KERNEL), (0, 0))
    big = dict(w_in=w_in[0], w_pool_grp=w_pool_grp[0].reshape(n_grp * grp_rows, gdim), w_pool_out=w_pool_out[0],
               w_conv_out=w_conv_out[0], w_o=w_o[0], w_up=w_up[0], w_down=w_down[0])
    big_names = list(big)
    moments = dict(w_in=(m_w_in, v_w_in), w_pool_grp=(m_w_pool_grp, v_w_pool_grp), w_pool_out=(m_w_pool_out, v_w_pool_out),
                   w_conv_out=(m_w_conv_out, v_w_conv_out), w_o=(m_w_o, v_w_o), w_up=(m_w_up, v_w_up),
                   w_down=(m_w_down, v_w_down))
    slot_idx = me_idx.reshape(1)
    landing = {k: _fill_slot(big[k], slot_idx, BF16, "fill_" + k) for k in big_names}
    landing["meta"] = _fill_slot(meta, slot_idx, F32, "fill_meta")
    landing["w_dw"] = _fill_slot(jnp.pad(w_dw[0], pad_taps), slot_idx, F32, "fill_w_dw")
    gather_groups = [["w_in", "meta", "w_dw"], ["w_pool_grp", "w_pool_out", "w_conv_out", "w_o"], ["w_up"], ["w_down"]]
    started, gather_token = _gather_start([[landing[k] for k in names] for names in gather_groups], slot_idx,
                                          "gather_start")
    wg = {}

    def pass_on(gi, after):
        send, recv, lands = started[gi]
        fsend, frecv, lands = _gather_forward(lands, recv, after, f"gather_forward_{gi}")
        return send, recv, lands, fsend, frecv

    def complete(gi, handle, after):
        send, recv, lands, fsend, frecv = handle
        wg.update(zip(gather_groups[gi], _gather_finish(lands, send, recv, fsend, frecv, after,
                                                        f"gather_finish_{gi}")))

    complete(0, pass_on(0, gather_token), gather_token)
    meta_full = wg["meta"].transpose(1, 0, 2).reshape(N_META, d)
    wdw_full = wg["w_dw"].transpose(1, 0, 2).reshape(CONV_TAPS_PADDED, conv_w)
    tail = lp - N_META - seq
    h0 = jnp.concatenate([meta_full, x[0], jnp.zeros((tail, d), F32)], axis=0)
    target = jnp.pad(loss_target[0], ((N_META, tail), (0, 0)))
    u1 = _rms_pre(h0, g_pre_mix, lp)
    proj = _mm(u1, wg["w_in"], mode="nn", tm=tm_half, tn=wg["w_in"].shape[2], b_blocked=True, name="mm_proj")
    passed = pass_on(1, proj)
    conv_c = _conv_fwd(proj, wdw_full, b_dw, lp, conv_w, pool_w)
    s_act = _ln_silu(conv_c, conv_ln_g, conv_ln_b, lp)
    complete(1, passed, s_act)
    wpg_full = wg["w_pool_grp"].reshape(N_DEV, n_grp, grp_rows, gdim).transpose(1, 0, 2, 3).reshape(n_grp, gdim, gdim)
    w_o_full = wg["w_o"].reshape(d, d)
    ya_pre = _pool_fwd(proj, wpg_full, pool_scale, lp)
    m_mix, y_a, y_b = _gate_mix(ya_pre, s_act, wg["w_pool_out"], wg["w_conv_out"], proj, d, "gate_mix")
    o = _mm(m_mix, w_o_full, mode="nn", tm=tm_half, tn=512, name="mm_o")
    h1, u2 = _post_mix(o, h0, g_post_mix, g_pre_mlp, lp)
    complete(2, pass_on(2, u2), u2)
    act = _mm(u2, wg["w_up"], mode="nn", tm=tm_half, tn=wg["w_up"].shape[2], b_blocked=True, out_dtypes=(BF16,),
              epilogue=lambda acc: (jnp.square(jnp.maximum(acc, 0.0)),), name="mm_up")
    complete(3, pass_on(3, act), act)
    w_down_full = wg["w_down"].reshape(-1, d)
    f = _mm(act, w_down_full, mode="nn", tm=tm_half, tn=512, tk=4096, name="mm_down")

    big_out = {}

    def to_sibling(names, grads, after, tag):
        send, recv, grads, lands, token = _pair_exchange_start(grads, after, "grads_to_sibling_start_" + tag)
        return (names, send, recv, grads, lands, tag), token

    def to_owner(handle, after):
        names, send, recv, grads, lands, tag = handle
        grads, lands = _pair_exchange_finish(grads, lands, send, recv, after, "grads_to_sibling_finish_" + tag)
        sums = [_pair_sum(g.reshape((4, 2) + g.shape[1:]), r, c_idx.reshape(1), "pair_sum_" + k)
                for k, g, r in zip(names, grads, lands)]
        send, recv, sums, lands, token = _chip_exchange_start(sums, after, "grads_to_owner_start_" + tag)
        return (names, send, recv, sums, lands, tag), token

    def update(handle, after):
        names, send, recv, sums, lands, tag = handle
        got = _chip_exchange_finish(sums, lands, send, recv, after, "grads_to_owner_finish_" + tag)
        for k, cs, r3 in zip(names, sums, got):
            w2 = big[k]
            shape = moments[k][0].shape
            outs = _adamw_big(w2, moments[k][0].reshape(w2.shape), moments[k][1].reshape(w2.shape), cs, r3,
                              chip_idx.reshape(1), "adamw_" + k)
            big_out[k] = [a.reshape(shape) for a in outs]
        return big_out[names[-1]][0]

    df, dh, dg_post_mlp, loss_part = _loss_head(f, h1, target, g_post_mlp, lp, seq)
    d_up = _mm(df, w_down_full, mode="nt", tm=tm_half, tn=1024, out_dtypes=(BF16,), extras=[(act, 0)],
               epilogue=lambda acc, a: (acc * (2.0 * jnp.sqrt(a.astype(F32))),), name="mm_d_up")
    g_w_down = _mm(act, df, mode="tn", tm=1024, tn=1024, out_dtypes=(BF16,), name="mm_g_down")
    sib_down, token = to_sibling(["w_down"], [g_w_down.reshape(N_DEV, -1, d)], slot_idx, "down")
    g_w_up = _mm(u2, d_up, mode="tn", tm=1024, tn=wg["w_up"].shape[2], out_blocked=True, out_dtypes=(BF16,),
                 after=token, name="mm_g_up")
    sib_up, token = to_sibling(["w_up"], [g_w_up], slot_idx, "up")
    pending_down, token = to_owner(sib_down, token)
    du2 = _mm(d_up, wg["w_up"], mode="nt", tm=tm_half, tn=1024, b_blocked=True, kb=2, after=token, name="mm_du2")
    pending_up, token = to_owner(sib_up, du2)
    dh1, do, dg_pre_mlp, dg_post_mix = _mid_bwd(du2, h1, dh, o, g_pre_mlp, g_post_mix, lp)
    dproj, d_ya, d_yb = _gate_bwd(do, w_o_full, proj, y_a, y_b, tm=tm_half, tn=512, after=token, name="mm_dm")
    g_w_o = _mm(m_mix, do, mode="tn", tm=1024, tn=1024, out_dtypes=(BF16,), name="mm_g_o")
    bn_out = wg["w_pool_out"].shape[2]
    g_w_pool_out = _mm(ya_pre, d_ya, mode="tn", tm=pool_w, tn=bn_out, out_blocked=True, out_dtypes=(BF16,),
                       name="mm_g_pool_out")
    g_w_conv_out = _mm(s_act, d_yb, mode="tn", tm=conv_w, tn=bn_out, out_blocked=True, out_dtypes=(BF16,),
                       name="mm_g_conv_out")
    sib_mix, token = to_sibling(["w_o", "w_pool_out", "w_conv_out"],
                                [g_w_o.reshape(N_DEV, -1, d), g_w_pool_out, g_w_conv_out], slot_idx, "mix")
    d_ya_pre = _mm(d_ya, wg["w_pool_out"], mode="nt", tm=tm_half, tn=pool_w, b_blocked=True, after=token,
                   name="mm_d_ya_pre")
    d_s = _mm(d_yb, wg["w_conv_out"], mode="nt", tm=tm_half, tn=conv_w, b_blocked=True, name="mm_d_s")
    pending_mix, token = to_owner(sib_mix, d_s)
    dproj, g_wpg, d_scale = _pool_bwd(proj, d_ya_pre, wpg_full, pool_scale, dproj, lp)
    dc, d_ln_g, d_ln_b, d_b_dw = _ln_silu_bwd(conv_c, d_s, conv_ln_g, conv_ln_b, lp)
    dproj, g_wdw = _conv_bwd(proj, dc, wdw_full, dproj, lp, conv_w, pool_w)
    g_w_in = _mm(u1, dproj, mode="tn", tm=1024, tn=wg["w_in"].shape[2], out_blocked=True, out_dtypes=(BF16,),
                 after=token, name="mm_g_in")
    g_wpg_slots = g_wpg.astype(BF16).reshape(n_grp, N_DEV, grp_rows, gdim).transpose(1, 0, 2, 3)
    sib_in, token = to_sibling(["w_pool_grp", "w_in"],
                               [g_wpg_slots.reshape(N_DEV, n_grp * grp_rows, gdim), g_w_in], slot_idx, "in")
    done = update(pending_down, token)
    done = update(pending_up, done)
    pending_in, token = to_owner(sib_in, done)
    du1 = _mm(dproj, wg["w_in"], mode="nt", tm=tm_half, tn=1024, b_blocked=True, kb=2, after=token, name="mm_du1")
    dh0, dg_pre_mix = _pre_mix_bwd(du1, h0, dh1, g_pre_mix, lp)
    grad_x = dh0[N_META:N_META + seq][None]

    assert pool_w + conv_w == d and conv_w <= d and LANES <= d
    widen = lambda a: jnp.pad(a, ((0, 0), (0, d - a.shape[1])))
    packed = jnp.concatenate([
        dg_pre_mix, dg_post_mix, dg_pre_mlp, dg_post_mlp,
        jnp.concatenate([d_scale, d_ln_g], axis=1), jnp.concatenate([d_ln_b, d_b_dw], axis=1),
        dh0[:N_META], widen(g_wdw), widen(loss_part)], axis=0)
    rep = dict(g_pre_mix=((g_pre_mix, m_g_pre_mix, v_g_pre_mix), (0, 0)),
               g_post_mix=((g_post_mix, m_g_post_mix, v_g_post_mix), (SUBLANES, 0)),
               g_pre_mlp=((g_pre_mlp, m_g_pre_mlp, v_g_pre_mlp), (2 * SUBLANES, 0)),
               g_post_mlp=((g_post_mlp, m_g_post_mlp, v_g_post_mlp), (3 * SUBLANES, 0)),
               pool_scale=((pool_scale, m_pool_scale, v_pool_scale), (4 * SUBLANES, 0)),
               conv_ln_g=((conv_ln_g, m_conv_ln_g, v_conv_ln_g), (4 * SUBLANES, pool_w)),
               conv_ln_b=((conv_ln_b, m_conv_ln_b, v_conv_ln_b), (5 * SUBLANES, 0)),
               b_dw=((b_dw, m_b_dw, v_b_dw), (5 * SUBLANES, conv_w)))
    meta_row0 = 6 * SUBLANES
    wdw_row0 = meta_row0 + N_META
    loss_row0 = wdw_row0 + CONV_TAPS_PADDED
    (small_started,), token = _gather_start([[_fill_slot(packed, slot_idx, F32, "fill_small")]], dh0,
                                            "gather_small_start")
    done = update(pending_mix, token)
    done = update(pending_in, done)
    send, recv, lands = small_started
    fsend, frecv, lands = _gather_forward(lands, recv, done, "gather_small_forward")
    (packed_all,) = _gather_finish(lands, send, recv, fsend, frecv, done, "gather_small_finish")
    rep_names = list(rep)
    wdw_wmv = [jnp.pad(a[0], pad_taps) for a in (w_dw, m_w_dw, v_w_dw)]
    rep_out, meta_out, wdw_out, loss_blk = _small_update(
        slot_idx, packed_all, [rep[k][0] for k in rep_names], [rep[k][1] for k in rep_names],
        (meta, m_meta, v_meta), meta_row0, wdw_wmv, wdw_row0, loss_row0)
    small_out = dict(zip(rep_names, rep_out))
    small_out["meta"] = meta_out
    small_out["w_dw"] = [a[:CONV_KERNEL][None] for a in wdw_out]

    order = ["meta", "g_pre_mix", "w_in", "w_pool_grp", "pool_scale", "w_pool_out", "w_dw", "b_dw", "conv_ln_g",
             "conv_ln_b", "w_conv_out", "w_o", "g_post_mix", "g_pre_mlp", "w_up", "w_down", "g_post_mlp"]
    by_name = {**big_out, **small_out}
    result = [loss_blk[0, 0], grad_x]
    for kind in range(4):
        result += [by_name[k][kind] for k in order]
    return tuple(result)
```

```python
import functools

import jax
import jax.numpy as jnp
from jax import lax
from jax.experimental import pallas as pl
from jax.experimental.pallas import tpu as pltpu

F32 = jnp.float32
BF16 = jnp.bfloat16
MESH = pl.DeviceIdType.MESH

N_DEV = 8
N_META = 16
POOL_WINDOWS = (2, 4, 8, 16)
CONV_KERNEL = 31
CONV_TAPS_PADDED = 32
RMS_EPS = 1e-6
LN_EPS = 1e-5
ADAM_LR = 0.001
ADAM_B1 = 0.9
ADAM_B2 = 0.999
ADAM_EPS = 1e-08
ADAM_WD = 0.01
ADAM_STEP = 10

LANES = 128
SUBLANES = 8
ROW_CHUNK = 128
HALO = 32
VMEM_LIMIT_BYTES = 56 * 1024 * 1024


def _cparams(*sem):
    return pltpu.CompilerParams(dimension_semantics=sem if sem else None, vmem_limit_bytes=VMEM_LIMIT_BYTES)


def _round_up(n, m):
    return (n + m - 1) // m * m


def _row_tile(rows, cols, max_elems=640 * 1024):
    best = None
    for t in range(16, rows + 1, 16):
        if rows % t == 0 and (best is None or t * cols <= max_elems):
            best = t
    assert best is not None, (rows, cols)
    return best


def _rowsum8(a):
    t, w = a.shape
    return a.reshape(t // SUBLANES, SUBLANES, w).sum(axis=0)


def _mesh_pos():
    return lax.axis_index("x"), lax.axis_index("y"), lax.axis_index("c")


HBM_SPEC = pl.BlockSpec(memory_space=pltpu.HBM)
SEM_SPEC = pl.BlockSpec(memory_space=pltpu.SEMAPHORE)
ANY_SPEC = pl.BlockSpec(memory_space=pl.ANY)
_DATAFLOW = pltpu.SideEffectType.DATAFLOW_SIDE_EFFECTING


def _hbm(a):
    return pltpu.with_memory_space_constraint(a, pltpu.HBM)


def _slot(p):
    return 4 * p[0] + 2 * p[1] + p[2]


def _fill_slot(w, slot_idx, dtype, after, name):
    rows, cols = w.shape
    tr = _row_tile(rows, cols) if rows % 16 == 0 else rows

    def body(idx_ref, w_ref, _, out_ref):
        out_ref[...] = w_ref[...].astype(dtype)

    return pl.pallas_call(
        body, name=name,
        grid_spec=pltpu.PrefetchScalarGridSpec(
            num_scalar_prefetch=1, grid=(rows // tr,),
            in_specs=[pl.BlockSpec((tr, cols), lambda i, idx_ref: (i, 0)), ANY_SPEC],
            out_specs=pl.BlockSpec((None, tr, cols), lambda i, idx_ref: (idx_ref[0], i, 0))),
        out_shape=jax.ShapeDtypeStruct((N_DEV, rows, cols), dtype),
        compiler_params=_cparams("parallel"),
    )(slot_idx, w, after)


def _gather_start(groups, after, name):
    flat = [g for grp in groups for g in grp]
    n, n_grp = len(flat), len(groups)

    def body(*refs):
        lands = refs[:n]
        sems = refs[n + 1:n + 1 + 2 * n_grp]
        token = refs[-1]
        x, y, c = _mesh_pos()
        targets = [(x, y, 1 - c), (1 - x, y, c), (x, 1 - y, c), (1 - x, 1 - y, c)]
        t = 0
        for gi, grp in enumerate(groups):
            for ti in range(len(grp)):
                mine = lands[t].at[_slot((x, y, c))]
                for k, to in enumerate(targets):
                    pltpu.make_async_remote_copy(
                        src_ref=mine, dst_ref=mine,
                        send_sem=sems[2 * gi].at[4 * ti + k], recv_sem=sems[2 * gi + 1].at[4 * ti + k],
                        device_id=to, device_id_type=MESH).start()
                t += 1
        token[...] = jnp.zeros_like(token)

    sem_shapes = []
    for grp in groups:
        sem_shapes += [pltpu.SemaphoreType.DMA((4 * len(grp),))] * 2
    outs = pl.pallas_call(
        body, name=name,
        out_shape=tuple(sem_shapes + [pltpu.HBM(g.shape, g.dtype) for g in flat]
                        + [jax.ShapeDtypeStruct((SUBLANES, LANES), F32)]),
        in_specs=tuple([HBM_SPEC] * n + [ANY_SPEC]),
        out_specs=tuple([SEM_SPEC] * (2 * n_grp) + [HBM_SPEC] * n + [pl.BlockSpec(memory_space=pltpu.VMEM)]),
        input_output_aliases={i: 2 * n_grp + i for i in range(n)},
        compiler_params=pltpu.CompilerParams(has_side_effects=_DATAFLOW),
    )(*[_hbm(g) for g in flat], after)
    sems, lands, token = outs[:2 * n_grp], outs[2 * n_grp:-1], outs[-1]
    res, t = [], 0
    for gi, grp in enumerate(groups):
        res.append((sems[2 * gi], sems[2 * gi + 1], list(lands[t:t + len(grp)])))
        t += len(grp)
    return res, token


def _gather_forward(lands, recv_sems, after, name):
    n = len(lands)

    def body(*refs):
        land_refs, recv, _ = refs[:n], refs[n], refs[n + 1]
        fsend, frecv = refs[n + 2], refs[n + 3]
        x, y, c = _mesh_pos()
        chips = [(1 - x, y), (x, 1 - y), (1 - x, 1 - y)]
        for t in range(n):
            for j, chip in enumerate(chips):
                blk = land_refs[t].at[_slot((*chip, c))]
                pltpu.make_async_remote_copy(src_ref=blk, dst_ref=blk, send_sem=fsend.at[3 * t + j],
                                             recv_sem=recv.at[4 * t + 1 + j],
                                             device_id=(x, y, 1 - c), device_id_type=MESH).wait_recv()
                pltpu.make_async_remote_copy(src_ref=blk, dst_ref=blk, send_sem=fsend.at[3 * t + j],
                                             recv_sem=frecv.at[3 * t + j],
                                             device_id=(x, y, 1 - c), device_id_type=MESH).start()

    outs = pl.pallas_call(
        body, name=name,
        out_shape=tuple([pltpu.SemaphoreType.DMA((3 * n,))] * 2 + [pltpu.HBM(g.shape, g.dtype) for g in lands]),
        in_specs=tuple([HBM_SPEC] * n + [SEM_SPEC, ANY_SPEC]),
        out_specs=tuple([SEM_SPEC] * 2 + [HBM_SPEC] * n),
        input_output_aliases={i: 2 + i for i in range(n)},
        compiler_params=pltpu.CompilerParams(has_side_effects=_DATAFLOW),
    )(*lands, recv_sems, after)
    return outs[0], outs[1], list(outs[2:])


def _gather_finish(lands, send_sems, recv_sems, fsend, frecv, after, name):
    n = len(lands)

    def body(*refs):
        land_refs = refs[:n]
        send, recv, fs, fr = refs[n:n + 4]
        x, y, c = _mesh_pos()
        sibling = (x, y, 1 - c)
        chips = [(1 - x, y), (x, 1 - y), (1 - x, 1 - y)]

        def desc(src, dst, s_sem, r_sem):
            return pltpu.make_async_remote_copy(src_ref=src, dst_ref=dst, send_sem=s_sem, recv_sem=r_sem,
                                                device_id=sibling, device_id_type=MESH)

        for t in range(n):
            mine = land_refs[t].at[_slot((x, y, c))]
            desc(mine, land_refs[t].at[_slot(sibling)], send.at[4 * t], recv.at[4 * t]).wait_recv()
            for j, chip in enumerate(chips):
                got = land_refs[t].at[_slot((*chip, 1 - c))]
                desc(got, got, fs.at[3 * t + j], fr.at[3 * t + j]).wait_recv()
            for k in range(4):
                desc(mine, mine, send.at[4 * t + k], recv.at[4 * t + k]).wait_send()
            for j, chip in enumerate(chips):
                blk = land_refs[t].at[_slot((*chip, c))]
                desc(blk, blk, fs.at[3 * t + j], fr.at[3 * t + j]).wait_send()

    outs = pl.pallas_call(
        body, name=name,
        out_shape=tuple(pltpu.HBM(g.shape, g.dtype) for g in lands),
        in_specs=tuple([HBM_SPEC] * n + [SEM_SPEC] * 4 + [ANY_SPEC]),
        out_specs=tuple([HBM_SPEC] * n),
        input_output_aliases={i: i for i in range(n)},
        compiler_params=pltpu.CompilerParams(has_side_effects=_DATAFLOW),
    )(*lands, send_sems, recv_sems, fsend, frecv, after)
    return list(outs)


def _pair_exchange_start(grads, after, name):
    n = len(grads)

    def body(*refs):
        ins, lands = refs[:n], refs[n:2 * n]
        send, recv = refs[2 * n + 1], refs[2 * n + 2]
        token = refs[-1]
        x, y, c = _mesh_pos()
        for t in range(n):
            for q in range(4):
                pltpu.make_async_remote_copy(
                    src_ref=ins[t].at[2 * q + 1 - c], dst_ref=lands[t].at[q],
                    send_sem=send.at[4 * t + q], recv_sem=recv.at[4 * t + q],
                    device_id=(x, y, 1 - c), device_id_type=MESH).start()
        token[...] = jnp.zeros_like(token)

    land_shapes = [(4,) + g.shape[1:] for g in grads]
    outs = pl.pallas_call(
        body, name=name,
        out_shape=tuple([pltpu.SemaphoreType.DMA((4 * n,))] * 2 + [pltpu.HBM(g.shape, g.dtype) for g in grads]
                        + [pltpu.HBM(ls, g.dtype) for ls, g in zip(land_shapes, grads)]
                        + [jax.ShapeDtypeStruct((SUBLANES, LANES), F32)]),
        in_specs=tuple([HBM_SPEC] * (2 * n) + [ANY_SPEC]),
        out_specs=tuple([SEM_SPEC] * 2 + [HBM_SPEC] * (2 * n) + [pl.BlockSpec(memory_space=pltpu.VMEM)]),
        input_output_aliases={i: 2 + i for i in range(2 * n)},
        compiler_params=pltpu.CompilerParams(has_side_effects=_DATAFLOW),
    )(*[_hbm(g) for g in grads], *[_hbm(lax.empty(ls, g.dtype)) for ls, g in zip(land_shapes, grads)], after)
    return outs[0], outs[1], list(outs[2:2 + n]), list(outs[2 + n:2 + 2 * n]), outs[-1]


def _pair_exchange_finish(grads, lands, send_sems, recv_sems, after, name):
    n = len(grads)

    def body(*refs):
        ins, land_refs = refs[:n], refs[n:2 * n]
        send, recv = refs[2 * n], refs[2 * n + 1]
        x, y, c = _mesh_pos()
        for t in range(n):
            for q in range(4):
                cp = pltpu.make_async_remote_copy(
                    src_ref=ins[t].at[q], dst_ref=land_refs[t].at[q], send_sem=send.at[4 * t + q],
                    recv_sem=recv.at[4 * t + q], device_id=(x, y, 1 - c), device_id_type=MESH)
                cp.wait_send()
                cp.wait_recv()

    outs = pl.pallas_call(
        body, name=name,
        out_shape=tuple([pltpu.HBM(g.shape, g.dtype) for g in grads] + [pltpu.HBM(g.shape, g.dtype) for g in lands]),
        in_specs=tuple([HBM_SPEC] * (2 * n) + [SEM_SPEC] * 2 + [ANY_SPEC]),
        out_specs=tuple([HBM_SPEC] * (2 * n)),
        input_output_aliases={i: i for i in range(2 * n)},
        compiler_params=pltpu.CompilerParams(has_side_effects=_DATAFLOW),
    )(*grads, *lands, send_sems, recv_sems, after)
    return list(outs[:n]), list(outs[n:])


def _chip_exchange_start(sums, after, name):
    n = len(sums)

    def body(*refs):
        ins, lands = refs[:n], refs[n:2 * n]
        send, recv = refs[2 * n + 1], refs[2 * n + 2]
        token = refs[-1]
        x, y, c = _mesh_pos()
        chips = [(1 - x, y), (x, 1 - y), (1 - x, 1 - y)]
        for t in range(n):
            for r, chip in enumerate(chips):
                pltpu.make_async_remote_copy(
                    src_ref=ins[t].at[2 * chip[0] + chip[1]], dst_ref=lands[t].at[r],
                    send_sem=send.at[3 * t + r], recv_sem=recv.at[3 * t + r],
                    device_id=(*chip, c), device_id_type=MESH).start()
        token[...] = jnp.zeros_like(token)

    land_shapes = [(3,) + s.shape[1:] for s in sums]
    outs = pl.pallas_call(
        body, name=name,
        out_shape=tuple([pltpu.SemaphoreType.DMA((3 * n,))] * 2 + [pltpu.HBM(s.shape, s.dtype) for s in sums]
                        + [pltpu.HBM(ls, s.dtype) for ls, s in zip(land_shapes, sums)]
                        + [jax.ShapeDtypeStruct((SUBLANES, LANES), F32)]),
        in_specs=tuple([HBM_SPEC] * (2 * n) + [ANY_SPEC]),
        out_specs=tuple([SEM_SPEC] * 2 + [HBM_SPEC] * (2 * n) + [pl.BlockSpec(memory_space=pltpu.VMEM)]),
        input_output_aliases={i: 2 + i for i in range(2 * n)},
        compiler_params=pltpu.CompilerParams(has_side_effects=_DATAFLOW),
    )(*[_hbm(s) for s in sums], *[_hbm(lax.empty(ls, s.dtype)) for ls, s in zip(land_shapes, sums)], after)
    return outs[0], outs[1], list(outs[2:2 + n]), list(outs[2 + n:2 + 2 * n]), outs[-1]


def _chip_exchange_finish(sums, lands, send_sems, recv_sems, after, name):
    n = len(sums)

    def body(*refs):
        ins, land_refs = refs[:n], refs[n:2 * n]
        send, recv = refs[2 * n], refs[2 * n + 1]
        x, y, c = _mesh_pos()
        for t in range(n):
            for r in range(3):
                cp = pltpu.make_async_remote_copy(
                    src_ref=ins[t].at[r], dst_ref=land_refs[t].at[r], send_sem=send.at[3 * t + r],
                    recv_sem=recv.at[3 * t + r],
                    device_id=(x, y, 1 - c), device_id_type=MESH)
                cp.wait_send()
                cp.wait_recv()

    outs = pl.pallas_call(
        body, name=name,
        out_shape=tuple(pltpu.HBM(g.shape, g.dtype) for g in lands),
        in_specs=tuple([HBM_SPEC] * (2 * n) + [SEM_SPEC] * 2 + [ANY_SPEC]),
        out_specs=tuple([HBM_SPEC] * n),
        input_output_aliases={n + i: i for i in range(n)},
        compiler_params=pltpu.CompilerParams(has_side_effects=_DATAFLOW),
    )(*sums, *lands, send_sems, recv_sems, after)
    return list(outs)


def _mm(a, b, *, mode, tm, tn, tk=None, b_blocked=False, out_blocked=False, out_dtypes=(F32,),
        epilogue=None, extras=(), after=None, kb=1, out_places=None, name):
    if mode == "nn":
        m, k = a.shape
        n = b.shape[0] * b.shape[2] if b_blocked else b.shape[1]
        dims = (((1,), (0,)), ((), ()))
    elif mode == "nt":
        m, k = a.shape
        n = b.shape[1] if b_blocked else b.shape[0]
        if b_blocked:
            tk = kb * b.shape[2]
        dims = (((1,), (1,)), ((), ()))
    else:
        k, m = a.shape
        n = b.shape[1]
        dims = (((0,), (0,)), ((), ()))
    tk = k if tk is None else tk
    assert m % tm == 0 and n % tn == 0 and k % tk == 0, (name, m, n, k, tm, tn, tk)
    gm, gn, gk = m // tm, n // tn, k // tk
    if b_blocked:
        assert (tn if mode == "nn" else tk) == kb * b.shape[2], name

    if mode == "nn":
        a_spec = pl.BlockSpec((tm, tk), lambda i, j, kk: (i, kk))
        b_spec = (pl.BlockSpec((None, tk, tn), lambda i, j, kk: (j, kk, 0)) if b_blocked
                  else pl.BlockSpec((tk, tn), lambda i, j, kk: (kk, j)))
    elif mode == "nt":
        a_spec = pl.BlockSpec((tm, tk), lambda i, j, kk: (i, kk))
        b_spec = (pl.BlockSpec((kb, tn, tk // kb), lambda i, j, kk: (kk, j, 0)) if b_blocked
                  else pl.BlockSpec((tn, tk), lambda i, j, kk: (j, kk)))
    else:
        a_spec = pl.BlockSpec((tk, tm), lambda i, j, kk: (kk, i))
        b_spec = pl.BlockSpec((tk, tn), lambda i, j, kk: (kk, j))
    if out_blocked:
        out_spec = pl.BlockSpec((None, tm, tn), lambda i, j, kk: (j, i, 0))
        out_shape = (gn, m, tn)
    else:
        out_spec = pl.BlockSpec((tm, tn), lambda i, j, kk: (i, j))
        out_shape = (m, n)
    extra_specs = [pl.BlockSpec((tm, tn), functools.partial(lambda i, j, kk, off: (i, j + off), off=off))
                   for _, off in extras]
    n_extra, n_out = len(extras), len(out_dtypes)
    n_after = 0 if after is None else 1
    places = out_places if out_places is not None else (None,) * n_out

    def body(a_ref, b_ref, *rest):
        extra_refs = rest[:n_extra]
        out_refs = rest[n_extra + n_after:n_extra + n_after + n_out]

        def finish(acc):
            if epilogue is None:
                res = (acc,)
            else:
                res = epilogue(acc, *[e[...] for e in extra_refs])
            for o_ref, r in zip(out_refs, res):
                o_ref[...] = r.astype(o_ref.dtype)

        if mode == "nt" and b_blocked:
            bk = tk // kb
            part = lax.dot_general(a_ref[:, :bk], b_ref[0], dims, preferred_element_type=F32)
            for h in range(1, kb):
                part = part + lax.dot_general(a_ref[:, h * bk:(h + 1) * bk], b_ref[h], dims,
                                              preferred_element_type=F32)
        else:
            part = lax.dot_general(a_ref[...], b_ref[...], dims, preferred_element_type=F32)
        if gk == 1:
            finish(part)
        else:
            acc_ref = rest[-1]
            kk = pl.program_id(2)

            @pl.when(kk == 0)
            def _():
                acc_ref[...] = part

            @pl.when(kk > 0)
            def _():
                acc_ref[...] += part

            @pl.when(kk == gk - 1)
            def _():
                finish(acc_ref[...])

    outs = pl.pallas_call(
        body, name=name, grid=(gm, gn, gk),
        in_specs=[a_spec, b_spec] + extra_specs + [ANY_SPEC] * n_after,
        out_specs=[out_spec if place is None else
                   pl.BlockSpec((tm, tn), functools.partial(lambda i, j, kk, off: (i, j + off), off=place[1] // tn))
                   for place in places],
        out_shape=[jax.ShapeDtypeStruct(out_shape if place is None else (m, place[0]), dt)
                   for dt, place in zip(out_dtypes, places)],
        scratch_shapes=[pltpu.VMEM((tm, tn), F32)] if gk > 1 else [],
        compiler_params=_cparams("parallel", "parallel", "arbitrary"),
    )(a, b, *[e for e, _ in extras], *([] if after is None else [after]))
    return outs[0] if n_out == 1 else outs


def _gate_mix(ya_pre, s, wpo, wco, proj, d_model, name):
    lp, width = ya_pre.shape
    nb, _, bn = wpo.shape
    ga_off = (proj.shape[1] - 2 * d_model) // bn
    gb_off = (proj.shape[1] - d_model) // bn

    def body(ya_ref, s_ref, wpo_ref, wco_ref, ga_ref, gb_ref, m_ref, y_a_ref, y_b_ref):
        y_a = jnp.dot(ya_ref[...], wpo_ref[...], preferred_element_type=F32)
        y_b = jnp.dot(s_ref[...], wco_ref[...], preferred_element_type=F32)
        m = jax.nn.sigmoid(ga_ref[...]) * y_a + jax.nn.sigmoid(gb_ref[...]) * y_b
        m_ref[...] = m.astype(BF16)
        y_a_ref[...] = y_a.astype(BF16)
        y_b_ref[...] = y_b.astype(BF16)

    act_spec = pl.BlockSpec((lp, width), lambda j: (0, 0))
    w_spec = pl.BlockSpec((None, width, bn), lambda j: (j, 0, 0))
    out_spec = pl.BlockSpec((lp, bn), lambda j: (0, j))
    return pl.pallas_call(
        body, name=name, grid=(nb,),
        in_specs=[act_spec, act_spec, w_spec, w_spec,
                  pl.BlockSpec((lp, bn), lambda j: (0, j + ga_off)),
                  pl.BlockSpec((lp, bn), lambda j: (0, j + gb_off))],
        out_specs=[out_spec] * 3,
        out_shape=[jax.ShapeDtypeStruct((lp, nb * bn), BF16)] * 3,
        compiler_params=_cparams("parallel"),
    )(ya_pre, s, wpo, wco, proj, proj)


def _rms_stats(x):
    return lax.rsqrt(jnp.mean(x * x, axis=-1, keepdims=True) + RMS_EPS)


def _rms_bwd(x, g, dy):
    r = _rms_stats(x)
    nrm = x * r
    dn = dy * g
    dx = r * (dn - nrm * jnp.mean(dn * nrm, axis=-1, keepdims=True))
    return dx, dy * nrm


def _rowwise(body, ins, outs, accs, *, lp, name):
    tr = _row_tile(lp, max(a.shape[1] for a in ins))
    n_in, n_out, n_acc = len(ins), len(outs), len(accs)

    def kernel_body(*refs):
        i = pl.program_id(0)
        acc_refs = refs[n_in + n_out:]

        @pl.when(i == 0)
        def _():
            for r in acc_refs:
                r[...] = jnp.zeros_like(r)

        body(i * tr, refs[:n_in], refs[n_in:n_in + n_out], acc_refs)

    in_specs = []
    for a in ins:
        if a.shape[0] == lp:
            in_specs.append(pl.BlockSpec((tr, a.shape[1]), lambda i: (i, 0)))
        else:
            in_specs.append(pl.BlockSpec(a.shape, lambda i: (0, 0)))
    out_specs = [pl.BlockSpec((tr, w), lambda i: (i, 0)) for w, _ in outs]
    out_specs += [pl.BlockSpec((SUBLANES, w), lambda i: (0, 0)) for w in accs]
    out_shape = [jax.ShapeDtypeStruct((lp, w), d) for w, d in outs]
    out_shape += [jax.ShapeDtypeStruct((SUBLANES, w), F32) for w in accs]
    return pl.pallas_call(
        kernel_body, name=name, grid=(lp // tr,), in_specs=in_specs, out_specs=out_specs,
        out_shape=out_shape, compiler_params=_cparams("arbitrary"),
    )(*ins)


def _rms_pre(h0, g, lp):
    d = h0.shape[1]

    def body(row0, ins, outs, accs):
        h_ref, g_ref = ins
        x = h_ref[...]
        outs[0][...] = (x * _rms_stats(x) * g_ref[...]).astype(BF16)

    return _rowwise(body, [h0, g], [(d, BF16)], [], lp=lp, name="rms_pre")[0]


def _post_mix(o, h0, g_post_mix, g_pre_mlp, lp):
    d = h0.shape[1]

    def body(row0, ins, outs, accs):
        o_ref, h0_ref, g1_ref, g2_ref = ins
        o_v = o_ref[...]
        h1 = h0_ref[...] + o_v * _rms_stats(o_v) * g1_ref[...]
        outs[0][...] = h1
        outs[1][...] = (h1 * _rms_stats(h1) * g2_ref[...]).astype(BF16)

    return _rowwise(body, [o, h0, g_post_mix, g_pre_mlp], [(d, F32), (d, BF16)], [], lp=lp, name="post_mix")


def _loss_head(f, h1, target, g_post_mlp, lp, seq):
    d = f.shape[1]

    def body(row0, ins, outs, accs):
        f_ref, h1_ref, t_ref, g_ref = ins
        df_ref, dh_ref = outs
        dg_ref, loss_ref = accs
        f_v, g = f_ref[...], g_ref[...]
        r = _rms_stats(f_v)
        nrm = f_v * r
        rows = row0 + lax.broadcasted_iota(jnp.int32, (f_v.shape[0], 1), 0)
        valid = (rows >= N_META) & (rows < N_META + seq)
        err = jnp.where(valid, h1_ref[...] + nrm * g - t_ref[...], 0.0)
        loss_ref[...] += 0.5 * jnp.sum(jnp.mean(err * err, axis=-1, keepdims=True))
        dy = err * (1.0 / d)
        dn = dy * g
        df_ref[...] = (r * (dn - nrm * jnp.mean(dn * nrm, axis=-1, keepdims=True))).astype(BF16)
        dh_ref[...] = dy
        dg_ref[...] += _rowsum8(dy * nrm)

    return _rowwise(body, [f, h1, target, g_post_mlp], [(d, BF16), (d, F32)], [d, LANES], lp=lp, name="loss_head")


def _mid_bwd(du2, h1, dh, o, g_pre_mlp, g_post_mix, lp):
    d = h1.shape[1]

    def body(row0, ins, outs, accs):
        du2_ref, h1_ref, dh_ref, o_ref, g2_ref, g1_ref = ins
        dx2, dg2 = _rms_bwd(h1_ref[...], g2_ref[...], du2_ref[...])
        dh1 = dh_ref[...] + dx2
        do, dg1 = _rms_bwd(o_ref[...], g1_ref[...], dh1)
        outs[0][...] = dh1
        outs[1][...] = do.astype(BF16)
        accs[0][...] += _rowsum8(dg2)
        accs[1][...] += _rowsum8(dg1)

    return _rowwise(body, [du2, h1, dh, o, g_pre_mlp, g_post_mix], [(d, F32), (d, BF16)], [d, d], lp=lp,
                    name="mid_bwd")


def _pre_mix_bwd(du1, h0, dh1, g_pre_mix, lp):
    d = h0.shape[1]

    def body(row0, ins, outs, accs):
        du1_ref, h0_ref, dh1_ref, g_ref = ins
        dx, dg = _rms_bwd(h0_ref[...], g_ref[...], du1_ref[...])
        outs[0][...] = dh1_ref[...] + dx
        accs[0][...] += _rowsum8(dg)

    return _rowwise(body, [du1, h0, dh1, g_pre_mix], [(d, F32)], [d], lp=lp, name="pre_mix_bwd")


def _ln_stats(c):
    mu = jnp.mean(c, axis=-1, keepdims=True)
    var = jnp.mean(jnp.square(c - mu), axis=-1, keepdims=True)
    return mu, lax.rsqrt(var + LN_EPS)


def _ln_silu(c, ln_g, ln_b, lp):
    w = c.shape[1]

    def body(row0, ins, outs, accs):
        c_ref, g_ref, b_ref = ins
        c_v = c_ref[...]
        mu, rstd = _ln_stats(c_v)
        ln = (c_v - mu) * rstd * g_ref[...] + b_ref[...]
        outs[0][...] = (ln * jax.nn.sigmoid(ln)).astype(BF16)

    return _rowwise(body, [c, ln_g, ln_b], [(w, BF16)], [], lp=lp, name="ln_silu")[0]


def _ln_silu_bwd(c, ds, ln_g, ln_b, lp):
    w = c.shape[1]

    def body(row0, ins, outs, accs):
        c_ref, ds_ref, g_ref, b_ref = ins
        c_v, g = c_ref[...], g_ref[...]
        mu, rstd = _ln_stats(c_v)
        nrm = (c_v - mu) * rstd
        ln = nrm * g + b_ref[...]
        sig = jax.nn.sigmoid(ln)
        dln = ds_ref[...] * (sig * (1.0 + ln * (1.0 - sig)))
        dn = dln * g
        dc = rstd * (dn - jnp.mean(dn, axis=-1, keepdims=True) - nrm * jnp.mean(dn * nrm, axis=-1, keepdims=True))
        outs[0][...] = dc
        accs[0][...] += _rowsum8(dln * nrm)
        accs[1][...] += _rowsum8(dln)
        accs[2][...] += _rowsum8(dc)

    return _rowwise(body, [c, ds, ln_g, ln_b], [(w, F32)], [w, w, w], lp=lp, name="ln_silu_bwd")


def _chunk_with_history(ref, i, cols=slice(None)):
    t0 = pl.multiple_of(i * ROW_CHUNK, ROW_CHUNK)
    lo0 = pl.multiple_of(jnp.maximum(t0 - HALO, 0), SUBLANES)
    lo = jnp.where(i > 0, ref[pl.ds(lo0, HALO), cols], 0.0)
    return jnp.concatenate([lo, ref[pl.ds(t0, ROW_CHUNK), cols]], axis=0)


def _chunk_with_future(ref, i, n_chunks, cols=slice(None)):
    t0 = pl.multiple_of(i * ROW_CHUNK, ROW_CHUNK)
    hi0 = pl.multiple_of(jnp.minimum(t0 + ROW_CHUNK, (n_chunks - 1) * ROW_CHUNK), SUBLANES)
    hi = jnp.where(i < n_chunks - 1, ref[pl.ds(hi0, HALO), cols], 0.0)
    return jnp.concatenate([ref[pl.ds(t0, ROW_CHUNK), cols], hi], axis=0)


def _inv_count(t0, n_rows, window):
    pos = t0 + lax.broadcasted_iota(jnp.int32, (n_rows, 1), 0)
    return 1.0 / jnp.minimum(pos + 1, window).astype(F32)


def _pool_delta(z_hist, t0, window):
    s = z_hist
    sh = 1
    while sh < window:
        s = s + pltpu.roll(s, sh, 0)
        sh *= 2
    cur = z_hist[HALO:, :]
    return s[HALO:, :] * _inv_count(t0, ROW_CHUNK, window) - cur


def _pool_fwd(proj, wpg, pool_scale, lp):
    n_grp, gdim, _ = wpg.shape
    width = n_grp * gdim
    n_chunks = lp // ROW_CHUNK

    def body(z_ref, w_ref, sc_ref, out_ref):
        for g, window in enumerate(POOL_WINDOWS):
            cols = slice(g * gdim, (g + 1) * gdim)

            def chunk(i, carry, cols=cols, g=g, window=window):
                t0 = pl.multiple_of(i * ROW_CHUNK, ROW_CHUNK)
                d = _pool_delta(_chunk_with_history(z_ref, i, cols), t0, window)
                q = jnp.dot(d.astype(BF16), w_ref[g], preferred_element_type=F32)
                out_ref[pl.ds(t0, ROW_CHUNK), cols] = (q * sc_ref[:, cols]).astype(BF16)
                return carry

            lax.fori_loop(0, n_chunks, chunk, 0)

    return pl.pallas_call(
        body, name="pool_fwd", grid=(1,),
        in_specs=[pl.BlockSpec((lp, width), lambda i: (0, 0)),
                  pl.BlockSpec(wpg.shape, lambda i: (0, 0, 0)),
                  pl.BlockSpec(pool_scale.shape, lambda i: (0, 0))],
        out_specs=pl.BlockSpec((lp, width), lambda i: (0, 0)),
        out_shape=jax.ShapeDtypeStruct((lp, width), BF16),
        compiler_params=_cparams("arbitrary"),
    )(proj, wpg, pool_scale)


def _pool_bwd(proj, d_ya, wpg, pool_scale, dproj, lp):
    n_grp, gdim, _ = wpg.shape
    width = n_grp * gdim
    n_chunks = lp // ROW_CHUNK
    ext = ROW_CHUNK + HALO

    def body(z_ref, dya_ref, w_ref, sc_ref, _, dz_ref, dw_ref, dsc_ref):
        dw_ref[...] = jnp.zeros_like(dw_ref)
        dsc_ref[...] = jnp.zeros_like(dsc_ref)
        for g, window in enumerate(POOL_WINDOWS):
            cols = slice(g * gdim, (g + 1) * gdim)

            def chunk(i, carry, cols=cols, g=g, window=window):
                t0 = pl.multiple_of(i * ROW_CHUNK, ROW_CHUNK)
                w_g = w_ref[g]
                scale = sc_ref[:, cols]
                d = _pool_delta(_chunk_with_history(z_ref, i, cols), t0, window).astype(BF16)
                dya_ext = _chunk_with_future(dya_ref, i, n_chunks, cols)
                dya = dya_ext[:ROW_CHUNK, :]
                q = jnp.dot(d, w_g, preferred_element_type=F32)
                dsc_ref[:, cols] += _rowsum8(dya * q)
                e_ext = (dya_ext * scale).astype(BF16)
                dw_ref[g] += lax.dot_general(d, e_ext[:ROW_CHUNK, :], (((0,), (0,)), ((), ())),
                                             preferred_element_type=F32)
                dd_ext = lax.dot_general(e_ext, w_g, (((1,), (1,)), ((), ())), preferred_element_type=F32)
                s = dd_ext * _inv_count(t0, ext, window)
                sh = 1
                while sh < window:
                    s = s + pltpu.roll(s, ext - sh, 0)
                    sh *= 2
                dz_ref[pl.ds(t0, ROW_CHUNK), cols] = (s[:ROW_CHUNK, :] - dd_ext[:ROW_CHUNK, :]).astype(BF16)
                return carry

            lax.fori_loop(0, n_chunks, chunk, 0)

    blk = pl.BlockSpec((lp, width), lambda i: (0, 0))
    return pl.pallas_call(
        body, name="pool_bwd", grid=(1,),
        in_specs=[blk, blk, pl.BlockSpec(wpg.shape, lambda i: (0, 0, 0)),
                  pl.BlockSpec(pool_scale.shape, lambda i: (0, 0)), ANY_SPEC],
        out_specs=[blk, pl.BlockSpec(wpg.shape, lambda i: (0, 0, 0)),
                   pl.BlockSpec((SUBLANES, width), lambda i: (0, 0))],
        out_shape=[jax.ShapeDtypeStruct(dproj.shape, BF16), jax.ShapeDtypeStruct(wpg.shape, F32),
                   jax.ShapeDtypeStruct((SUBLANES, width), F32)],
        input_output_aliases={4: 0},
        compiler_params=_cparams("arbitrary"),
    )(proj, d_ya, wpg, pool_scale, dproj)


def _conv_fwd(proj, w_dw, b_dw, lp, width, v_col0):
    n_chunks = lp // ROW_CHUNK
    v_blk0, g_blk0 = v_col0 // LANES, (v_col0 + width) // LANES

    def body(v_ref, gc_ref, w_ref, b_ref, c_ref, a_pad):
        a_pad[pl.ds(0, HALO), :] = jnp.zeros((HALO, LANES), F32)
        a_pad[pl.ds(HALO, lp), :] = v_ref[...] * jax.nn.sigmoid(gc_ref[...])

        def chunk(i, carry):
            t0 = pl.multiple_of(i * ROW_CHUNK, ROW_CHUNK)
            hist = a_pad[pl.ds(t0, ROW_CHUNK + HALO), :]
            acc = jnp.zeros((ROW_CHUNK, LANES), F32)
            for k in range(CONV_KERNEL):
                acc = acc + w_ref[k:k + 1, :] * pltpu.roll(hist, CONV_KERNEL - 1 - k, 0)[HALO:, :]
            c_ref[pl.ds(t0, ROW_CHUNK), :] = acc + b_ref[...]
            return carry

        lax.fori_loop(0, n_chunks, chunk, 0)

    return pl.pallas_call(
        body, name="conv_fwd", grid=(width // LANES,),
        in_specs=[pl.BlockSpec((lp, LANES), lambda j: (0, j + v_blk0)),
                  pl.BlockSpec((lp, LANES), lambda j: (0, j + g_blk0)),
                  pl.BlockSpec((CONV_TAPS_PADDED, LANES), lambda j: (0, j)),
                  pl.BlockSpec((1, LANES), lambda j: (0, j))],
        out_specs=pl.BlockSpec((lp, LANES), lambda j: (0, j)),
        out_shape=jax.ShapeDtypeStruct((lp, width), F32),
        scratch_shapes=[pltpu.VMEM((lp + HALO, LANES), F32)],
        compiler_params=_cparams("parallel"),
    )(proj, proj, w_dw, b_dw)


def _conv_bwd(proj, dc, w_dw, dproj, lp, width, v_col0):
    n_chunks = lp // ROW_CHUNK
    ext = ROW_CHUNK + HALO
    v_blk0, g_blk0 = v_col0 // LANES, (v_col0 + width) // LANES

    def body(v_ref, gc_ref, dc_ref, w_ref, _, dv_ref, dgc_ref, dw_ref, a_pad, dc_pad, dw_acc):
        sig = jax.nn.sigmoid(gc_ref[...])
        a_pad[pl.ds(0, HALO), :] = jnp.zeros((HALO, LANES), F32)
        a_pad[pl.ds(HALO, lp), :] = v_ref[...] * sig
        dc_pad[pl.ds(0, lp), :] = dc_ref[...]
        dc_pad[pl.ds(lp, HALO), :] = jnp.zeros((HALO, LANES), F32)
        dw_acc[...] = jnp.zeros_like(dw_acc)

        def chunk(i, carry):
            t0 = pl.multiple_of(i * ROW_CHUNK, ROW_CHUNK)
            hist = a_pad[pl.ds(t0, ext), :]
            fut = dc_pad[pl.ds(t0, ext), :]
            dc_cur = fut[:ROW_CHUNK, :]
            da = jnp.zeros((ROW_CHUNK, LANES), F32)
            for k in range(CONV_KERNEL):
                lag = CONV_KERNEL - 1 - k
                da = da + w_ref[k:k + 1, :] * pltpu.roll(fut, (ext - lag) % ext, 0)[:ROW_CHUNK, :]
                dw_acc[pl.ds(SUBLANES * k, SUBLANES), :] += _rowsum8(dc_cur * pltpu.roll(hist, lag, 0)[HALO:, :])
            rows = pl.ds(t0, ROW_CHUNK)
            sg = jax.nn.sigmoid(gc_ref[rows, :])
            dv_ref[rows, :] = (da * sg).astype(BF16)
            dgc_ref[rows, :] = (da * v_ref[rows, :] * sg * (1.0 - sg)).astype(BF16)
            return carry

        lax.fori_loop(0, n_chunks, chunk, 0)
        dw_ref[...] = dw_acc[...].reshape(CONV_TAPS_PADDED, SUBLANES, LANES).sum(axis=1)

    col = lambda j: (0, j)
    return pl.pallas_call(
        body, name="conv_bwd", grid=(width // LANES,),
        in_specs=[pl.BlockSpec((lp, LANES), lambda j: (0, j + v_blk0)),
                  pl.BlockSpec((lp, LANES), lambda j: (0, j + g_blk0)),
                  pl.BlockSpec((lp, LANES), col),
                  pl.BlockSpec((CONV_TAPS_PADDED, LANES), col), ANY_SPEC],
        out_specs=[pl.BlockSpec((lp, LANES), lambda j: (0, j + v_blk0)), pl.BlockSpec((lp, LANES), col),
                   pl.BlockSpec((CONV_TAPS_PADDED, LANES), col)],
        out_shape=[jax.ShapeDtypeStruct(dproj.shape, BF16), jax.ShapeDtypeStruct((lp, width), BF16),
                   jax.ShapeDtypeStruct((CONV_TAPS_PADDED, width), F32)],
        scratch_shapes=[pltpu.VMEM((lp + HALO, LANES), F32), pltpu.VMEM((lp + HALO, LANES), F32),
                        pltpu.VMEM((CONV_TAPS_PADDED * SUBLANES, LANES), F32)],
        input_output_aliases={4: 0},
        compiler_params=_cparams("parallel"),
    )(proj, proj, dc, w_dw, dproj)


def _place_columns(dst, pieces, name):
    m = dst.shape[0]
    tile = 512
    counts = [p.shape[1] // tile for p, _ in pieces]
    starts = [sum(counts[:i]) for i in range(len(pieces))]
    n_steps = sum(counts)

    def local(s, i):
        return jnp.clip(s - starts[i], 0, counts[i] - 1)

    def out_index(s):
        blk = pieces[0][1] // tile + local(s, 0)
        for i in range(1, len(pieces)):
            blk = jnp.where(s >= starts[i], pieces[i][1] // tile + local(s, i), blk)
        return 0, blk

    def body(*refs):
        out_ref = refs[-1]
        s = pl.program_id(0)
        for i in range(len(pieces)):
            @pl.when((s >= starts[i]) & (s < starts[i] + counts[i]))
            def _(i=i):
                out_ref[...] = refs[i][...]

    return pl.pallas_call(
        body, name=name, grid=(n_steps,),
        in_specs=[pl.BlockSpec((m, tile), functools.partial(lambda s, i: (0, local(s, i)), i=i))
                  for i in range(len(pieces))] + [ANY_SPEC],
        out_specs=pl.BlockSpec((m, tile), out_index),
        out_shape=jax.ShapeDtypeStruct(dst.shape, dst.dtype),
        input_output_aliases={len(pieces): 0},
        compiler_params=_cparams("arbitrary"),
    )(*[p for p, _ in pieces], dst)


def _adamw_math(w, g, m, v):
    m = ADAM_B1 * m + (1.0 - ADAM_B1) * g
    v = ADAM_B2 * v + (1.0 - ADAM_B2) * jnp.square(g)
    m_hat = m / (1.0 - ADAM_B1 ** ADAM_STEP)
    v_hat = v / (1.0 - ADAM_B2 ** ADAM_STEP)
    delta = -ADAM_LR * (m_hat / (jnp.sqrt(v_hat) + ADAM_EPS) + ADAM_WD * w)
    return delta, m, v


def _pair_sum(own, recv, where, name):
    _, _, rows, cols = own.shape
    tr = _row_tile(rows, cols, 1024 * 1024)

    def body(where_ref, own_ref, recv_ref, out_ref):
        out_ref[...] = (own_ref[...].astype(F32) + recv_ref[...].astype(F32)).astype(BF16)

    return pl.pallas_call(
        body, name=name,
        grid_spec=pltpu.PrefetchScalarGridSpec(
            num_scalar_prefetch=1, grid=(3, rows // tr),
            in_specs=[pl.BlockSpec((None, None, tr, cols), lambda r, i, wh: (wh[2 + r], wh[0], i, 0)),
                      pl.BlockSpec((None, tr, cols), lambda r, i, wh: (wh[2 + r], i, 0))],
            out_specs=pl.BlockSpec((None, tr, cols), lambda r, i, wh: (wh[2 + r], i, 0))),
        out_shape=jax.ShapeDtypeStruct((4, rows, cols), BF16),
        compiler_params=_cparams("parallel", "parallel"),
    )(where, own, recv)


def _adamw_big(w, m, v, own, from_sibling, recv3, where, name):
    rows, cols = w.shape
    tr = _row_tile(rows, cols, 256 * 1024)

    def body(where_ref, w_ref, m_ref, v_ref, own_ref, sib_ref, r_ref, g_out, d_out, m_out, v_out):
        g = own_ref[...].astype(F32) + sib_ref[...].astype(F32)
        for r in range(3):
            g = g + r_ref[r].astype(F32)
        delta, m_new, v_new = _adamw_math(w_ref[...], g, m_ref[...], v_ref[...])
        g_out[...] = g
        d_out[...] = delta
        m_out[...] = m_new
        v_out[...] = v_new

    blk = pl.BlockSpec((tr, cols), lambda i, q_ref: (i, 0))
    return pl.pallas_call(
        body, name=name,
        grid_spec=pltpu.PrefetchScalarGridSpec(
            num_scalar_prefetch=1, grid=(rows // tr,),
            in_specs=[blk, blk, blk,
                      pl.BlockSpec((None, None, tr, cols), lambda i, wh: (wh[1], wh[0], i, 0)),
                      pl.BlockSpec((None, tr, cols), lambda i, wh: (wh[1], i, 0)),
                      pl.BlockSpec((3, tr, cols), lambda i, wh: (0, i, 0))],
            out_specs=[blk] * 4),
        out_shape=[jax.ShapeDtypeStruct((rows, cols), F32)] * 4,
        compiler_params=_cparams("parallel"),
    )(where, w, m, v, own, from_sibling, recv3)


def _small_update(me_idx, packed, rep_params, rep_places, meta_wmv, meta_row0, wdw_wmv, wdw_row0, loss_row0):
    n_rep = len(rep_params)
    meta_rows, meta_cols = meta_wmv[0].shape
    wdw_rows, wdw_cols = wdw_wmv[0].shape

    def body(me_ref, *refs):
        pos = 0

        def take(k):
            nonlocal pos
            out = refs[pos:pos + k]
            pos += k
            return out

        rep_in = [take(3) for _ in range(n_rep)]
        rep_g = take(n_rep)
        meta_in, (meta_g,) = take(3), take(1)
        wdw_in, (wdw_g,) = take(3), take(1)
        (loss_ref,) = take(1)
        rep_out = [take(4) for _ in range(n_rep)]
        meta_out, wdw_out = take(4), take(4)
        (loss_out,) = take(1)

        def update(wmv, g, outs):
            delta, m_new, v_new = _adamw_math(wmv[0][...], g, wmv[1][...], wmv[2][...])
            for o_ref, val in zip(outs, (g, delta, m_new, v_new)):
                o_ref[...] = val

        for wmv, g_ref, outs in zip(rep_in, rep_g, rep_out):
            g = jnp.sum(g_ref[0], axis=0, keepdims=True)
            for j in range(1, N_DEV):
                g = g + jnp.sum(g_ref[j], axis=0, keepdims=True)
            update(wmv, g, outs)
        for wmv, g_ref, outs in ((meta_in, meta_g, meta_out), (wdw_in, wdw_g, wdw_out)):
            g = g_ref[0]
            for j in range(1, N_DEV):
                g = g + g_ref[j]
            update(wmv, g, outs)
        total = loss_ref[0]
        for j in range(1, N_DEV):
            total = total + loss_ref[j]
        loss_out[...] = total

    def whole(a):
        nd = a.ndim
        return pl.BlockSpec(a.shape, lambda i, me_ref, nd=nd: (0,) * nd)

    ins, in_specs = [], []
    for wmv in rep_params:
        ins += list(wmv)
        in_specs += [whole(a) for a in wmv]
    for wmv, (row0, col0) in zip(rep_params, rep_places):
        width = wmv[0].shape[1]
        ins.append(packed)
        in_specs.append(pl.BlockSpec((N_DEV, SUBLANES, width),
                                     lambda i, me_ref, rb=row0 // SUBLANES, cb=col0 // width: (0, rb, cb)))
    ins += list(meta_wmv) + [packed]
    in_specs += [whole(a) for a in meta_wmv]
    in_specs.append(pl.BlockSpec((N_DEV, meta_rows, meta_cols),
                                 lambda i, me_ref, rb=meta_row0 // meta_rows: (0, rb, me_ref[0])))
    ins += list(wdw_wmv) + [packed]
    in_specs += [whole(a) for a in wdw_wmv]
    in_specs.append(pl.BlockSpec((N_DEV, wdw_rows, wdw_cols),
                                 lambda i, me_ref, rb=wdw_row0 // wdw_rows: (0, rb, me_ref[0])))
    ins.append(packed)
    in_specs.append(pl.BlockSpec((N_DEV, SUBLANES, LANES), lambda i, me_ref, rb=loss_row0 // SUBLANES: (0, rb, 0)))

    out_shape, out_specs = [], []
    for wmv in list(rep_params) + [meta_wmv, wdw_wmv]:
        out_shape += [jax.ShapeDtypeStruct(wmv[0].shape, F32)] * 4
        out_specs += [whole(wmv[0])] * 4
    out_shape.append(jax.ShapeDtypeStruct((SUBLANES, LANES), F32))
    out_specs.append(pl.BlockSpec((SUBLANES, LANES), lambda i, me_ref: (0, 0)))

    outs = pl.pallas_call(
        body, name="small_update",
        grid_spec=pltpu.PrefetchScalarGridSpec(num_scalar_prefetch=1, grid=(1,), in_specs=in_specs,
                                               out_specs=out_specs),
        out_shape=out_shape, compiler_params=_cparams("arbitrary"),
    )(me_idx, *ins)
    groups = [outs[4 * i:4 * i + 4] for i in range(n_rep + 2)]
    return groups[:n_rep], groups[n_rep], groups[n_rep + 1], outs[-1]


def kernel(x, meta, g_pre_mix, w_in, w_pool_grp, pool_scale, w_pool_out, w_dw, b_dw, conv_ln_g, conv_ln_b, w_conv_out, w_o, g_post_mix, g_pre_mlp, w_up, w_down, g_post_mlp, loss_target, m_meta, m_g_pre_mix, m_w_in, m_w_pool_grp, m_pool_scale, m_w_pool_out, m_w_dw, m_b_dw, m_conv_ln_g, m_conv_ln_b, m_w_conv_out, m_w_o, m_g_post_mix, m_g_pre_mlp, m_w_up, m_w_down, m_g_post_mlp, v_meta, v_g_pre_mix, v_w_in, v_w_pool_grp, v_pool_scale, v_w_pool_out, v_w_dw, v_b_dw, v_conv_ln_g, v_conv_ln_b, v_w_conv_out, v_w_o, v_g_post_mix, v_g_pre_mlp, v_w_up, v_w_down, v_g_post_mlp):
    seq, d = x.shape[1], x.shape[2]
    pool_w = pool_scale.shape[1]
    conv_w = b_dw.shape[1]
    n_grp, grp_rows, gdim = w_pool_grp.shape[1:]
    lp = _round_up(N_META + seq, ROW_CHUNK)
    tm_half = lp // 2 if (lp // 2) % 16 == 0 else lp
    c_idx = lax.axis_index("c").astype(jnp.int32)
    chip_idx = (2 * lax.axis_index("x") + lax.axis_index("y")).astype(jnp.int32)
    me_idx = 2 * chip_idx + c_idx

    pad_taps = ((0, CONV_TAPS_PADDED - CONV_KERNEL), (0, 0))
    big = dict(w_in=w_in[0], w_pool_grp=w_pool_grp[0].reshape(n_grp * grp_rows, gdim), w_pool_out=w_pool_out[0],
               w_conv_out=w_conv_out[0], w_o=w_o[0], w_up=w_up[0], w_down=w_down[0])
    big_names = list(big)
    moments = dict(w_in=(m_w_in, v_w_in), w_pool_grp=(m_w_pool_grp, v_w_pool_grp), w_pool_out=(m_w_pool_out, v_w_pool_out),
                   w_conv_out=(m_w_conv_out, v_w_conv_out), w_o=(m_w_o, v_w_o), w_up=(m_w_up, v_w_up),
                   w_down=(m_w_down, v_w_down))
    slot_idx = me_idx.reshape(1)
    sources = dict(big, meta=meta, w_dw=jnp.pad(w_dw[0], pad_taps))

    def fill(k, after):
        return _fill_slot(sources[k], slot_idx, BF16 if k in big else F32, after, "fill_" + k)

    gather_groups = [["meta", "w_dw", "w_in"], ["w_pool_grp", "w_pool_out", "w_conv_out", "w_o"], ["w_up"], ["w_down"]]
    started, gather_token = _gather_start([[fill(k, slot_idx) for k in gather_groups[0]]], slot_idx,
                                          "gather_start_first")
    started_rest, _ = _gather_start([[fill(k, gather_token) for k in names] for names in gather_groups[1:]],
                                    gather_token, "gather_start_rest")
    started += started_rest
    wg = {}

    def pass_on(gi, after):
        send, recv, lands = started[gi]
        fsend, frecv, lands = _gather_forward(lands, recv, after, f"gather_forward_{gi}")
        return send, recv, lands, fsend, frecv

    def complete(gi, handle, after):
        send, recv, lands, fsend, frecv = handle
        wg.update(zip(gather_groups[gi], _gather_finish(lands, send, recv, fsend, frecv, after,
                                                        f"gather_finish_{gi}")))

    complete(0, pass_on(0, gather_token), gather_token)
    meta_full = wg["meta"].transpose(1, 0, 2).reshape(N_META, d)
    wdw_full = wg["w_dw"].transpose(1, 0, 2).reshape(CONV_TAPS_PADDED, conv_w)
    tail = lp - N_META - seq
    h0 = jnp.concatenate([meta_full, x[0], jnp.zeros((tail, d), F32)], axis=0)
    target = jnp.pad(loss_target[0], ((N_META, tail), (0, 0)))
    u1 = _rms_pre(h0, g_pre_mix, lp)
    proj = _mm(u1, wg["w_in"], mode="nn", tm=tm_half, tn=wg["w_in"].shape[2], b_blocked=True, name="mm_proj")
    passed = pass_on(1, proj)
    conv_c = _conv_fwd(proj, wdw_full, b_dw, lp, conv_w, pool_w)
    s_act = _ln_silu(conv_c, conv_ln_g, conv_ln_b, lp)
    complete(1, passed, s_act)
    wpg_full = wg["w_pool_grp"].reshape(N_DEV, n_grp, grp_rows, gdim).transpose(1, 0, 2, 3).reshape(n_grp, gdim, gdim)
    w_o_full = wg["w_o"].reshape(d, d)
    ya_pre = _pool_fwd(proj, wpg_full, pool_scale, lp)
    m_mix, y_a, y_b = _gate_mix(ya_pre, s_act, wg["w_pool_out"], wg["w_conv_out"], proj, d, "gate_mix")
    o = _mm(m_mix, w_o_full, mode="nn", tm=tm_half, tn=512, name="mm_o")
    h1, u2 = _post_mix(o, h0, g_post_mix, g_pre_mlp, lp)
    complete(2, pass_on(2, u2), u2)
    act = _mm(u2, wg["w_up"], mode="nn", tm=tm_half, tn=wg["w_up"].shape[2], b_blocked=True, out_dtypes=(BF16,),
              epilogue=lambda acc: (jnp.square(jnp.maximum(acc, 0.0)),), name="mm_up")
    complete(3, pass_on(3, act), act)
    w_down_full = wg["w_down"].reshape(-1, d)
    f = _mm(act, w_down_full, mode="nn", tm=tm_half, tn=512, tk=4096, name="mm_down")

    big_out = {}

    def to_sibling(names, grads, after, tag):
        send, recv, grads, lands, token = _pair_exchange_start(grads, after, "grads_to_sibling_start_" + tag)
        return (names, send, recv, grads, lands, tag), token

    x_idx, y_idx = lax.axis_index("x"), lax.axis_index("y")
    where = jnp.stack([c_idx, chip_idx, 2 * (1 - x_idx) + y_idx, 2 * x_idx + (1 - y_idx),
                       2 * (1 - x_idx) + (1 - y_idx)]).astype(jnp.int32)

    def to_owner(handle, after):
        names, send, recv, grads, from_sib, tag = handle
        grads, from_sib = _pair_exchange_finish(grads, from_sib, send, recv, after, "grads_to_sibling_finish_" + tag)
        own = [g.reshape((4, 2) + g.shape[1:]) for g in grads]
        sums = [_pair_sum(o, r, where, "pair_sum_" + k) for k, o, r in zip(names, own, from_sib)]
        send, recv, sums, lands, token = _chip_exchange_start(sums, after, "grads_to_owner_start_" + tag)
        return (names, send, recv, sums, lands, own, from_sib, tag), token

    def update(handle, after):
        names, send, recv, sums, lands, own, from_sib, tag = handle
        got = _chip_exchange_finish(sums, lands, send, recv, after, "grads_to_owner_finish_" + tag)
        for k, o, s, r3 in zip(names, own, from_sib, got):
            w2 = big[k]
            shape = moments[k][0].shape
            outs = _adamw_big(w2, moments[k][0].reshape(w2.shape), moments[k][1].reshape(w2.shape), o, s, r3,
                              where, "adamw_" + k)
            big_out[k] = [a.reshape(shape) for a in outs]
        return big_out[names[-1]][0]

    df, dh, dg_post_mlp, loss_part = _loss_head(f, h1, target, g_post_mlp, lp, seq)
    d_up = _mm(df, w_down_full, mode="nt", tm=tm_half, tn=1024, out_dtypes=(BF16,), extras=[(act, 0)],
               epilogue=lambda acc, a: (acc * (2.0 * jnp.sqrt(a.astype(F32))),), name="mm_d_up")
    g_w_down = _mm(act, df, mode="tn", tm=1024, tn=1024, out_dtypes=(BF16,), name="mm_g_down")
    sib_down, token = to_sibling(["w_down"], [g_w_down.reshape(N_DEV, -1, d)], slot_idx, "down")
    g_w_up = _mm(u2, d_up, mode="tn", tm=1024, tn=wg["w_up"].shape[2], out_blocked=True, out_dtypes=(BF16,),
                 after=token, name="mm_g_up")
    sib_up, token = to_sibling(["w_up"], [g_w_up], slot_idx, "up")
    pending_down, token = to_owner(sib_down, token)
    du2 = _mm(d_up, wg["w_up"], mode="nt", tm=tm_half, tn=1024, b_blocked=True, kb=2, after=token, name="mm_du2")
    pending_up, token = to_owner(sib_up, du2)
    dh1, do, dg_pre_mlp, dg_post_mix = _mid_bwd(du2, h1, dh, o, g_pre_mlp, g_post_mix, lp)

    def gate_bwd(dm, ga, gb, ya, yb):
        sa, sb = jax.nn.sigmoid(ga), jax.nn.sigmoid(gb)
        return (dm * ya.astype(F32) * sa * (1.0 - sa), dm * yb.astype(F32) * sb * (1.0 - sb), dm * sa, dm * sb)

    gate_tn = 512
    ga_col0, gb_col0 = proj.shape[1] - 2 * d, proj.shape[1] - d
    dproj, d_gb, d_ya, d_yb = _mm(
        do, w_o_full, mode="nt", tm=tm_half, tn=gate_tn, out_dtypes=(BF16,) * 4,
        extras=[(proj, ga_col0 // gate_tn), (proj, gb_col0 // gate_tn), (y_a, 0), (y_b, 0)], epilogue=gate_bwd,
        out_places=((proj.shape[1], ga_col0), None, None, None), after=token, name="mm_dm")
    g_w_o = _mm(m_mix, do, mode="tn", tm=1024, tn=1024, out_dtypes=(BF16,), name="mm_g_o")
    bn_out = wg["w_pool_out"].shape[2]
    g_w_pool_out = _mm(ya_pre, d_ya, mode="tn", tm=pool_w, tn=bn_out, out_blocked=True, out_dtypes=(BF16,),
                       name="mm_g_pool_out")
    g_w_conv_out = _mm(s_act, d_yb, mode="tn", tm=conv_w, tn=bn_out, out_blocked=True, out_dtypes=(BF16,),
                       name="mm_g_conv_out")
    sib_mix, token = to_sibling(["w_o", "w_pool_out", "w_conv_out"],
                                [g_w_o.reshape(N_DEV, -1, d), g_w_pool_out, g_w_conv_out], slot_idx, "mix")
    d_ya_pre = _mm(d_ya, wg["w_pool_out"], mode="nt", tm=tm_half, tn=pool_w, b_blocked=True, after=token,
                   name="mm_d_ya_pre")
    d_s = _mm(d_yb, wg["w_conv_out"], mode="nt", tm=tm_half, tn=conv_w, b_blocked=True, name="mm_d_s")
    pending_mix, token = to_owner(sib_mix, d_s)
    dproj, g_wpg, d_scale = _pool_bwd(proj, d_ya_pre, wpg_full, pool_scale, dproj, lp)
    dc, d_ln_g, d_ln_b, d_b_dw = _ln_silu_bwd(conv_c, d_s, conv_ln_g, conv_ln_b, lp)
    dproj, dgc, g_wdw = _conv_bwd(proj, dc, wdw_full, dproj, lp, conv_w, pool_w)
    dproj = _place_columns(dproj, [(dgc, pool_w + conv_w), (d_gb, gb_col0)], "place_dproj")
    g_w_in = _mm(u1, dproj, mode="tn", tm=1024, tn=wg["w_in"].shape[2], out_blocked=True, out_dtypes=(BF16,),
                 after=token, name="mm_g_in")
    g_wpg_slots = g_wpg.astype(BF16).reshape(n_grp, N_DEV, grp_rows, gdim).transpose(1, 0, 2, 3)
    sib_in, token = to_sibling(["w_pool_grp", "w_in"],
                               [g_wpg_slots.reshape(N_DEV, n_grp * grp_rows, gdim), g_w_in], slot_idx, "in")
    done = update(pending_down, token)
    pending_in, token = to_owner(sib_in, done)
    done = update(pending_up, token)
    du1 = _mm(dproj, wg["w_in"], mode="nt", tm=tm_half, tn=1024, b_blocked=True, kb=2, after=done, name="mm_du1")
    dh0, dg_pre_mix = _pre_mix_bwd(du1, h0, dh1, g_pre_mix, lp)
    grad_x = dh0[N_META:N_META + seq][None]

    assert pool_w + conv_w == d and conv_w <= d and LANES <= d
    widen = lambda a: jnp.pad(a, ((0, 0), (0, d - a.shape[1])))
    packed = jnp.concatenate([
        dg_pre_mix, dg_post_mix, dg_pre_mlp, dg_post_mlp,
        jnp.concatenate([d_scale, d_ln_g], axis=1), jnp.concatenate([d_ln_b, d_b_dw], axis=1),
        dh0[:N_META], widen(g_wdw), widen(loss_part)], axis=0)
    rep = dict(g_pre_mix=((g_pre_mix, m_g_pre_mix, v_g_pre_mix), (0, 0)),
               g_post_mix=((g_post_mix, m_g_post_mix, v_g_post_mix), (SUBLANES, 0)),
               g_pre_mlp=((g_pre_mlp, m_g_pre_mlp, v_g_pre_mlp), (2 * SUBLANES, 0)),
               g_post_mlp=((g_post_mlp, m_g_post_mlp, v_g_post_mlp), (3 * SUBLANES, 0)),
               pool_scale=((pool_scale, m_pool_scale, v_pool_scale), (4 * SUBLANES, 0)),
               conv_ln_g=((conv_ln_g, m_conv_ln_g, v_conv_ln_g), (4 * SUBLANES, pool_w)),
               conv_ln_b=((conv_ln_b, m_conv_ln_b, v_conv_ln_b), (5 * SUBLANES, 0)),
               b_dw=((b_dw, m_b_dw, v_b_dw), (5 * SUBLANES, conv_w)))
    meta_row0 = 6 * SUBLANES
    wdw_row0 = meta_row0 + N_META
    loss_row0 = wdw_row0 + CONV_TAPS_PADDED
    (small_started,), token = _gather_start([[_fill_slot(packed, slot_idx, F32, slot_idx, "fill_small")]], dh0,
                                            "gather_small_start")
    done = update(pending_mix, token)
    done = update(pending_in, done)
    send, recv, lands = small_started
    fsend, frecv, lands = _gather_forward(lands, recv, done, "gather_small_forward")
    (packed_all,) = _gather_finish(lands, send, recv, fsend, frecv, done, "gather_small_finish")
    rep_names = list(rep)
    wdw_wmv = [jnp.pad(a[0], pad_taps) for a in (w_dw, m_w_dw, v_w_dw)]
    rep_out, meta_out, wdw_out, loss_blk = _small_update(
        slot_idx, packed_all, [rep[k][0] for k in rep_names], [rep[k][1] for k in rep_names],
        (meta, m_meta, v_meta), meta_row0, wdw_wmv, wdw_row0, loss_row0)
    small_out = dict(zip(rep_names, rep_out))
    small_out["meta"] = meta_out
    small_out["w_dw"] = [a[:CONV_KERNEL][None] for a in wdw_out]

    order = ["meta", "g_pre_mix", "w_in", "w_pool_grp", "pool_scale", "w_pool_out", "w_dw", "b_dw", "conv_ln_g",
             "conv_ln_b", "w_conv_out", "w_o", "g_post_mix", "g_pre_mlp", "w_up", "w_down", "g_post_mlp"]
    by_name = {**big_out, **small_out}
    result = [loss_blk[0, 0], grad_x]
    for kind in range(4):
        result += [by_name[k][kind] for k in order]
    return tuple(result)
```

```python
import functools

import jax
import jax.numpy as jnp
from jax import lax
from jax.experimental import pallas as pl
from jax.experimental.pallas import tpu as pltpu

F32 = jnp.float32
BF16 = jnp.bfloat16
MESH = pl.DeviceIdType.MESH

N_DEV = 8
N_META = 16
POOL_WINDOWS = (2, 4, 8, 16)
CONV_KERNEL = 31
CONV_TAPS_PADDED = 32
RMS_EPS = 1e-6
LN_EPS = 1e-5
ADAM_LR = 0.001
ADAM_B1 = 0.9
ADAM_B2 = 0.999
ADAM_EPS = 1e-08
ADAM_WD = 0.01
ADAM_STEP = 10

LANES = 128
SUBLANES = 8
ROW_CHUNK = 128
HALO = 32
VMEM_LIMIT_BYTES = 56 * 1024 * 1024


def _cparams(*sem):
    return pltpu.CompilerParams(dimension_semantics=sem if sem else None, vmem_limit_bytes=VMEM_LIMIT_BYTES)


def _round_up(n, m):
    return (n + m - 1) // m * m


def _row_tile(rows, cols, max_elems=640 * 1024):
    best = None
    for t in range(16, rows + 1, 16):
        if rows % t == 0 and (best is None or t * cols <= max_elems):
            best = t
    assert best is not None, (rows, cols)
    return best


def _rowsum8(a):
    t, w = a.shape
    return a.reshape(t // SUBLANES, SUBLANES, w).sum(axis=0)


def _mesh_pos():
    return lax.axis_index("x"), lax.axis_index("y"), lax.axis_index("c")


HBM_SPEC = pl.BlockSpec(memory_space=pltpu.HBM)
SEM_SPEC = pl.BlockSpec(memory_space=pltpu.SEMAPHORE)
ANY_SPEC = pl.BlockSpec(memory_space=pl.ANY)
_DATAFLOW = pltpu.SideEffectType.DATAFLOW_SIDE_EFFECTING


def _hbm(a):
    return pltpu.with_memory_space_constraint(a, pltpu.HBM)


def _slot(p):
    return 4 * p[0] + 2 * p[1] + p[2]


def _fill_slot(w, slot_idx, dtype, after, name):
    rows, cols = w.shape
    tr = _row_tile(rows, cols) if rows % 16 == 0 else rows

    def body(idx_ref, w_ref, _, out_ref):
        out_ref[...] = w_ref[...].astype(dtype)

    return pl.pallas_call(
        body, name=name,
        grid_spec=pltpu.PrefetchScalarGridSpec(
            num_scalar_prefetch=1, grid=(rows // tr,),
            in_specs=[pl.BlockSpec((tr, cols), lambda i, idx_ref: (i, 0)), ANY_SPEC],
            out_specs=pl.BlockSpec((None, tr, cols), lambda i, idx_ref: (idx_ref[0], i, 0))),
        out_shape=jax.ShapeDtypeStruct((N_DEV, rows, cols), dtype),
        compiler_params=_cparams("parallel"),
    )(slot_idx, w, after)


def _gather_start(groups, after, name):
    flat = [g for grp in groups for g in grp]
    n, n_grp = len(flat), len(groups)

    def body(*refs):
        lands = refs[:n]
        sems = refs[n + 1:n + 1 + 2 * n_grp]
        token = refs[-1]
        x, y, c = _mesh_pos()
        targets = [(x, y, 1 - c), (1 - x, y, c), (x, 1 - y, c), (1 - x, 1 - y, c)]
        t = 0
        for gi, grp in enumerate(groups):
            for ti in range(len(grp)):
                mine = lands[t].at[_slot((x, y, c))]
                for k, to in enumerate(targets):
                    pltpu.make_async_remote_copy(
                        src_ref=mine, dst_ref=mine,
                        send_sem=sems[2 * gi].at[4 * ti + k], recv_sem=sems[2 * gi + 1].at[4 * ti + k],
                        device_id=to, device_id_type=MESH).start()
                t += 1
        token[...] = jnp.zeros_like(token)

    sem_shapes = []
    for grp in groups:
        sem_shapes += [pltpu.SemaphoreType.DMA((4 * len(grp),))] * 2
    outs = pl.pallas_call(
        body, name=name,
        out_shape=tuple(sem_shapes + [pltpu.HBM(g.shape, g.dtype) for g in flat]
                        + [jax.ShapeDtypeStruct((SUBLANES, LANES), F32)]),
        in_specs=tuple([HBM_SPEC] * n + [ANY_SPEC]),
        out_specs=tuple([SEM_SPEC] * (2 * n_grp) + [HBM_SPEC] * n + [pl.BlockSpec(memory_space=pltpu.VMEM)]),
        input_output_aliases={i: 2 * n_grp + i for i in range(n)},
        compiler_params=pltpu.CompilerParams(has_side_effects=_DATAFLOW),
    )(*[_hbm(g) for g in flat], after)
    sems, lands, token = outs[:2 * n_grp], outs[2 * n_grp:-1], outs[-1]
    res, t = [], 0
    for gi, grp in enumerate(groups):
        res.append((sems[2 * gi], sems[2 * gi + 1], list(lands[t:t + len(grp)])))
        t += len(grp)
    return res, token


def _gather_forward(lands, recv_sems, after, name):
    n = len(lands)

    def body(*refs):
        land_refs, recv, _ = refs[:n], refs[n], refs[n + 1]
        fsend, frecv = refs[n + 2], refs[n + 3]
        x, y, c = _mesh_pos()
        chips = [(1 - x, y), (x, 1 - y), (1 - x, 1 - y)]
        for t in range(n):
            for j, chip in enumerate(chips):
                blk = land_refs[t].at[_slot((*chip, c))]
                pltpu.make_async_remote_copy(src_ref=blk, dst_ref=blk, send_sem=fsend.at[3 * t + j],
                                             recv_sem=recv.at[4 * t + 1 + j],
                                             device_id=(x, y, 1 - c), device_id_type=MESH).wait_recv()
                pltpu.make_async_remote_copy(src_ref=blk, dst_ref=blk, send_sem=fsend.at[3 * t + j],
                                             recv_sem=frecv.at[3 * t + j],
                                             device_id=(x, y, 1 - c), device_id_type=MESH).start()

    outs = pl.pallas_call(
        body, name=name,
        out_shape=tuple([pltpu.SemaphoreType.DMA((3 * n,))] * 2 + [pltpu.HBM(g.shape, g.dtype) for g in lands]),
        in_specs=tuple([HBM_SPEC] * n + [SEM_SPEC, ANY_SPEC]),
        out_specs=tuple([SEM_SPEC] * 2 + [HBM_SPEC] * n),
        input_output_aliases={i: 2 + i for i in range(n)},
        compiler_params=pltpu.CompilerParams(has_side_effects=_DATAFLOW),
    )(*lands, recv_sems, after)
    return outs[0], outs[1], list(outs[2:])


def _gather_finish(lands, send_sems, recv_sems, fsend, frecv, after, name):
    n = len(lands)

    def body(*refs):
        land_refs = refs[:n]
        send, recv, fs, fr = refs[n:n + 4]
        x, y, c = _mesh_pos()
        sibling = (x, y, 1 - c)
        chips = [(1 - x, y), (x, 1 - y), (1 - x, 1 - y)]

        def desc(src, dst, s_sem, r_sem):
            return pltpu.make_async_remote_copy(src_ref=src, dst_ref=dst, send_sem=s_sem, recv_sem=r_sem,
                                                device_id=sibling, device_id_type=MESH)

        for t in range(n):
            mine = land_refs[t].at[_slot((x, y, c))]
            desc(mine, land_refs[t].at[_slot(sibling)], send.at[4 * t], recv.at[4 * t]).wait_recv()
            for j, chip in enumerate(chips):
                got = land_refs[t].at[_slot((*chip, 1 - c))]
                desc(got, got, fs.at[3 * t + j], fr.at[3 * t + j]).wait_recv()
            for k in range(4):
                desc(mine, mine, send.at[4 * t + k], recv.at[4 * t + k]).wait_send()
            for j, chip in enumerate(chips):
                blk = land_refs[t].at[_slot((*chip, c))]
                desc(blk, blk, fs.at[3 * t + j], fr.at[3 * t + j]).wait_send()

    outs = pl.pallas_call(
        body, name=name,
        out_shape=tuple(pltpu.HBM(g.shape, g.dtype) for g in lands),
        in_specs=tuple([HBM_SPEC] * n + [SEM_SPEC] * 4 + [ANY_SPEC]),
        out_specs=tuple([HBM_SPEC] * n),
        input_output_aliases={i: i for i in range(n)},
        compiler_params=pltpu.CompilerParams(has_side_effects=_DATAFLOW),
    )(*lands, send_sems, recv_sems, fsend, frecv, after)
    return list(outs)


def _neighbours():
    x, y, c = _mesh_pos()
    return (x, y, c), (x, y, 1 - c), (1 - x, y, c), (x, 1 - y, c), (1 - x, 1 - y, c)


def _halves(ref, slot):
    rows = ref.shape[1] // 2
    return ref.at[slot, pl.ds(0, rows)], ref.at[slot, pl.ds(rows, rows)]


def _ring_start(groups, after, name):
    flat = [g for grp in groups for g in grp]
    n, n_grp = len(flat), len(groups)

    def body(*refs):
        lands = refs[:n]
        sems = refs[n + 1:n + 1 + 2 * n_grp]
        token = refs[-1]
        me, sibling, x_nbr, y_nbr, _ = _neighbours()
        t = 0
        for gi, grp in enumerate(groups):
            for ti in range(len(grp)):
                mine = lands[t].at[_slot(me)]
                for k, to in enumerate((sibling, x_nbr, y_nbr)):
                    pltpu.make_async_remote_copy(
                        src_ref=mine, dst_ref=mine, send_sem=sems[2 * gi].at[3 * ti + k],
                        recv_sem=sems[2 * gi + 1].at[3 * ti + k], device_id=to, device_id_type=MESH).start()
                t += 1
        token[...] = jnp.zeros_like(token)

    sem_shapes = []
    for grp in groups:
        sem_shapes += [pltpu.SemaphoreType.DMA((3 * len(grp),))] * 2
    outs = pl.pallas_call(
        body, name=name,
        out_shape=tuple(sem_shapes + [pltpu.HBM(g.shape, g.dtype) for g in flat]
                        + [jax.ShapeDtypeStruct((SUBLANES, LANES), F32)]),
        in_specs=tuple([HBM_SPEC] * n + [ANY_SPEC]),
        out_specs=tuple([SEM_SPEC] * (2 * n_grp) + [HBM_SPEC] * n + [pl.BlockSpec(memory_space=pltpu.VMEM)]),
        input_output_aliases={i: 2 * n_grp + i for i in range(n)},
        compiler_params=pltpu.CompilerParams(has_side_effects=_DATAFLOW),
    )(*[_hbm(g) for g in flat], after)
    sems, lands, token = outs[:2 * n_grp], outs[2 * n_grp:-1], outs[-1]
    res, t = [], 0
    for gi, grp in enumerate(groups):
        res.append((sems[2 * gi], sems[2 * gi + 1], list(lands[t:t + len(grp)])))
        t += len(grp)
    return res, token


def _ring_relay(lands, recv_sems, after, name):
    n = len(lands)

    def body(*refs):
        land_refs, recv = refs[:n], refs[n]
        fsend, frecv, token = refs[n + 2], refs[n + 3], refs[-1]
        me, sibling, x_nbr, y_nbr, _ = _neighbours()

        def copy(ref, k, to):
            return pltpu.make_async_remote_copy(src_ref=ref, dst_ref=ref, send_sem=fsend.at[k], recv_sem=frecv.at[k],
                                                device_id=to, device_id_type=MESH)

        for t in range(n):
            for k, nbr, other, half in ((1, x_nbr, y_nbr, 0), (2, y_nbr, x_nbr, 1)):
                whole = land_refs[t].at[_slot(nbr)]
                pltpu.make_async_remote_copy(src_ref=whole, dst_ref=whole, send_sem=fsend.at[4 * t],
                                             recv_sem=recv.at[3 * t + k], device_id=sibling,
                                             device_id_type=MESH).wait_recv()
                copy(whole, 4 * t + 2 * (k - 1), sibling).start()
                copy(_halves(land_refs[t], _slot(nbr))[half], 4 * t + 2 * (k - 1) + 1, other).start()
        token[...] = jnp.zeros_like(token)

    outs = pl.pallas_call(
        body, name=name,
        out_shape=tuple([pltpu.SemaphoreType.DMA((4 * n,))] * 2 + [pltpu.HBM(g.shape, g.dtype) for g in lands]
                        + [jax.ShapeDtypeStruct((SUBLANES, LANES), F32)]),
        in_specs=tuple([HBM_SPEC] * n + [SEM_SPEC, ANY_SPEC]),
        out_specs=tuple([SEM_SPEC] * 2 + [HBM_SPEC] * n + [pl.BlockSpec(memory_space=pltpu.VMEM)]),
        input_output_aliases={i: 2 + i for i in range(n)},
        compiler_params=pltpu.CompilerParams(has_side_effects=_DATAFLOW),
    )(*lands, recv_sems, after)
    return outs[0], outs[1], list(outs[2:2 + n]), outs[-1]


def _ring_relay_far(lands, relay_recv, after, name):
    n = len(lands)

    def body(*refs):
        land_refs, frecv = refs[:n], refs[n]
        gsend, grecv, token = refs[n + 2], refs[n + 3], refs[-1]
        me, sibling, _, _, far = _neighbours()
        for t in range(n):
            for h, part in enumerate(_halves(land_refs[t], _slot(far))):
                pltpu.make_async_remote_copy(src_ref=part, dst_ref=part, send_sem=gsend.at[2 * t + h],
                                             recv_sem=frecv.at[4 * t + 2 * h + 1], device_id=sibling,
                                             device_id_type=MESH).wait_recv()
                pltpu.make_async_remote_copy(src_ref=part, dst_ref=part, send_sem=gsend.at[2 * t + h],
                                             recv_sem=grecv.at[2 * t + h], device_id=sibling,
                                             device_id_type=MESH).start()
        token[...] = jnp.zeros_like(token)

    outs = pl.pallas_call(
        body, name=name,
        out_shape=tuple([pltpu.SemaphoreType.DMA((2 * n,))] * 2 + [pltpu.HBM(g.shape, g.dtype) for g in lands]
                        + [jax.ShapeDtypeStruct((SUBLANES, LANES), F32)]),
        in_specs=tuple([HBM_SPEC] * n + [SEM_SPEC, ANY_SPEC]),
        out_specs=tuple([SEM_SPEC] * 2 + [HBM_SPEC] * n + [pl.BlockSpec(memory_space=pltpu.VMEM)]),
        input_output_aliases={i: 2 + i for i in range(n)},
        compiler_params=pltpu.CompilerParams(has_side_effects=_DATAFLOW),
    )(*lands, relay_recv, after)
    return outs[0], outs[1], list(outs[2:2 + n]), outs[-1]


def _ring_finish(lands, send, recv, fsend, frecv, gsend, grecv, after, name):
    n = len(lands)

    def body(*refs):
        land_refs = refs[:n]
        s1, r1, s2, r2, s3, r3 = refs[n:n + 6]
        (x, y, c), sibling, x_nbr, y_nbr, far = _neighbours()

        def desc(ref, s_sem, r_sem):
            return pltpu.make_async_remote_copy(src_ref=ref, dst_ref=ref, send_sem=s_sem, recv_sem=r_sem,
                                                device_id=sibling, device_id_type=MESH)

        other = lambda p: (p[0], p[1], 1 - c)
        for t in range(n):
            land = land_refs[t]
            mine = land.at[_slot((x, y, c))]
            desc(land.at[_slot(sibling)], s1.at[3 * t], r1.at[3 * t]).wait_recv()
            desc(land.at[_slot(other(x_nbr))], s2.at[4 * t], r2.at[4 * t]).wait_recv()
            desc(land.at[_slot(other(y_nbr))], s2.at[4 * t + 2], r2.at[4 * t + 2]).wait_recv()
            for h, part in enumerate(_halves(land, _slot(other(far)))):
                desc(part, s3.at[2 * t + h], r3.at[2 * t + h]).wait_recv()
            for k in range(3):
                desc(mine, s1.at[3 * t + k], r1.at[3 * t + k]).wait_send()
            x_halves, y_halves = _halves(land, _slot(x_nbr)), _halves(land, _slot(y_nbr))
            desc(land.at[_slot(x_nbr)], s2.at[4 * t], r2.at[4 * t]).wait_send()
            desc(x_halves[0], s2.at[4 * t + 1], r2.at[4 * t + 1]).wait_send()
            desc(land.at[_slot(y_nbr)], s2.at[4 * t + 2], r2.at[4 * t + 2]).wait_send()
            desc(y_halves[1], s2.at[4 * t + 3], r2.at[4 * t + 3]).wait_send()
            for h, part in enumerate(_halves(land, _slot(far))):
                desc(part, s3.at[2 * t + h], r3.at[2 * t + h]).wait_send()

    outs = pl.pallas_call(
        body, name=name,
        out_shape=tuple(pltpu.HBM(g.shape, g.dtype) for g in lands),
        in_specs=tuple([HBM_SPEC] * n + [SEM_SPEC] * 6 + [ANY_SPEC]),
        out_specs=tuple([HBM_SPEC] * n),
        input_output_aliases={i: i for i in range(n)},
        compiler_params=pltpu.CompilerParams(has_side_effects=_DATAFLOW),
    )(*lands, send, recv, fsend, frecv, gsend, grecv, after)
    return list(outs)


def _pair_exchange_start(grads, after, name):
    n = len(grads)

    def body(*refs):
        ins, lands = refs[:n], refs[n:2 * n]
        send, recv = refs[2 * n + 1], refs[2 * n + 2]
        token = refs[-1]
        x, y, c = _mesh_pos()
        for t in range(n):
            for q in range(4):
                pltpu.make_async_remote_copy(
                    src_ref=ins[t].at[2 * q + 1 - c], dst_ref=lands[t].at[q],
                    send_sem=send.at[4 * t + q], recv_sem=recv.at[4 * t + q],
                    device_id=(x, y, 1 - c), device_id_type=MESH).start()
        token[...] = jnp.zeros_like(token)

    land_shapes = [(4,) + g.shape[1:] for g in grads]
    outs = pl.pallas_call(
        body, name=name,
        out_shape=tuple([pltpu.SemaphoreType.DMA((4 * n,))] * 2 + [pltpu.HBM(g.shape, g.dtype) for g in grads]
                        + [pltpu.HBM(ls, g.dtype) for ls, g in zip(land_shapes, grads)]
                        + [jax.ShapeDtypeStruct((SUBLANES, LANES), F32)]),
        in_specs=tuple([HBM_SPEC] * (2 * n) + [ANY_SPEC]),
        out_specs=tuple([SEM_SPEC] * 2 + [HBM_SPEC] * (2 * n) + [pl.BlockSpec(memory_space=pltpu.VMEM)]),
        input_output_aliases={i: 2 + i for i in range(2 * n)},
        compiler_params=pltpu.CompilerParams(has_side_effects=_DATAFLOW),
    )(*[_hbm(g) for g in grads], *[_hbm(lax.empty(ls, g.dtype)) for ls, g in zip(land_shapes, grads)], after)
    return outs[0], outs[1], list(outs[2:2 + n]), list(outs[2 + n:2 + 2 * n]), outs[-1]


def _pair_exchange_finish(grads, lands, send_sems, recv_sems, after, name):
    n = len(grads)

    def body(*refs):
        ins, land_refs = refs[:n], refs[n:2 * n]
        send, recv = refs[2 * n], refs[2 * n + 1]
        x, y, c = _mesh_pos()
        for t in range(n):
            for q in range(4):
                cp = pltpu.make_async_remote_copy(
                    src_ref=ins[t].at[q], dst_ref=land_refs[t].at[q], send_sem=send.at[4 * t + q],
                    recv_sem=recv.at[4 * t + q], device_id=(x, y, 1 - c), device_id_type=MESH)
                cp.wait_send()
                cp.wait_recv()

    outs = pl.pallas_call(
        body, name=name,
        out_shape=tuple([pltpu.HBM(g.shape, g.dtype) for g in grads] + [pltpu.HBM(g.shape, g.dtype) for g in lands]),
        in_specs=tuple([HBM_SPEC] * (2 * n) + [SEM_SPEC] * 2 + [ANY_SPEC]),
        out_specs=tuple([HBM_SPEC] * (2 * n)),
        input_output_aliases={i: i for i in range(2 * n)},
        compiler_params=pltpu.CompilerParams(has_side_effects=_DATAFLOW),
    )(*grads, *lands, send_sems, recv_sems, after)
    return list(outs[:n]), list(outs[n:])


def _chip_exchange_start(sums, after, name):
    n = len(sums)

    def body(*refs):
        ins, lands = refs[:n], refs[n:2 * n]
        send, recv = refs[2 * n + 1], refs[2 * n + 2]
        token = refs[-1]
        x, y, c = _mesh_pos()
        chips = [(1 - x, y), (x, 1 - y), (1 - x, 1 - y)]
        for t in range(n):
            for r, chip in enumerate(chips):
                pltpu.make_async_remote_copy(
                    src_ref=ins[t].at[2 * chip[0] + chip[1]], dst_ref=lands[t].at[r],
                    send_sem=send.at[3 * t + r], recv_sem=recv.at[3 * t + r],
                    device_id=(*chip, c), device_id_type=MESH).start()
        token[...] = jnp.zeros_like(token)

    land_shapes = [(3,) + s.shape[1:] for s in sums]
    outs = pl.pallas_call(
        body, name=name,
        out_shape=tuple([pltpu.SemaphoreType.DMA((3 * n,))] * 2 + [pltpu.HBM(s.shape, s.dtype) for s in sums]
                        + [pltpu.HBM(ls, s.dtype) for ls, s in zip(land_shapes, sums)]
                        + [jax.ShapeDtypeStruct((SUBLANES, LANES), F32)]),
        in_specs=tuple([HBM_SPEC] * (2 * n) + [ANY_SPEC]),
        out_specs=tuple([SEM_SPEC] * 2 + [HBM_SPEC] * (2 * n) + [pl.BlockSpec(memory_space=pltpu.VMEM)]),
        input_output_aliases={i: 2 + i for i in range(2 * n)},
        compiler_params=pltpu.CompilerParams(has_side_effects=_DATAFLOW),
    )(*[_hbm(s) for s in sums], *[_hbm(lax.empty(ls, s.dtype)) for ls, s in zip(land_shapes, sums)], after)
    return outs[0], outs[1], list(outs[2:2 + n]), list(outs[2 + n:2 + 2 * n]), outs[-1]


def _chip_exchange_finish(sums, lands, send_sems, recv_sems, after, name):
    n = len(sums)

    def body(*refs):
        ins, land_refs = refs[:n], refs[n:2 * n]
        send, recv = refs[2 * n], refs[2 * n + 1]
        x, y, c = _mesh_pos()
        for t in range(n):
            for r in range(3):
                cp = pltpu.make_async_remote_copy(
                    src_ref=ins[t].at[r], dst_ref=land_refs[t].at[r], send_sem=send.at[3 * t + r],
                    recv_sem=recv.at[3 * t + r],
                    device_id=(x, y, 1 - c), device_id_type=MESH)
                cp.wait_send()
                cp.wait_recv()

    outs = pl.pallas_call(
        body, name=name,
        out_shape=tuple(pltpu.HBM(g.shape, g.dtype) for g in lands),
        in_specs=tuple([HBM_SPEC] * (2 * n) + [SEM_SPEC] * 2 + [ANY_SPEC]),
        out_specs=tuple([HBM_SPEC] * n),
        input_output_aliases={n + i: i for i in range(n)},
        compiler_params=pltpu.CompilerParams(has_side_effects=_DATAFLOW),
    )(*sums, *lands, send_sems, recv_sems, after)
    return list(outs)


def _mm(a, b, *, mode, tm, tn, tk=None, b_blocked=False, out_blocked=False, out_dtypes=(F32,),
        epilogue=None, extras=(), after=None, kb=1, out_places=None, name):
    if mode == "nn":
        m, k = a.shape
        n = b.shape[0] * b.shape[2] if b_blocked else b.shape[1]
        dims = (((1,), (0,)), ((), ()))
    elif mode == "nt":
        m, k = a.shape
        n = b.shape[1] if b_blocked else b.shape[0]
        if b_blocked:
            tk = kb * b.shape[2]
        dims = (((1,), (1,)), ((), ()))
    else:
        k, m = a.shape
        n = b.shape[1]
        dims = (((0,), (0,)), ((), ()))
    tk = k if tk is None else tk
    assert m % tm == 0 and n % tn == 0 and k % tk == 0, (name, m, n, k, tm, tn, tk)
    gm, gn, gk = m // tm, n // tn, k // tk
    if b_blocked:
        assert (tn if mode == "nn" else tk) == kb * b.shape[2], name

    if mode == "nn":
        a_spec = pl.BlockSpec((tm, tk), lambda i, j, kk: (i, kk))
        b_spec = (pl.BlockSpec((None, tk, tn), lambda i, j, kk: (j, kk, 0)) if b_blocked
                  else pl.BlockSpec((tk, tn), lambda i, j, kk: (kk, j)))
    elif mode == "nt":
        a_spec = pl.BlockSpec((tm, tk), lambda i, j, kk: (i, kk))
        b_spec = (pl.BlockSpec((kb, tn, tk // kb), lambda i, j, kk: (kk, j, 0)) if b_blocked
                  else pl.BlockSpec((tn, tk), lambda i, j, kk: (j, kk)))
    else:
        a_spec = pl.BlockSpec((tk, tm), lambda i, j, kk: (kk, i))
        b_spec = pl.BlockSpec((tk, tn), lambda i, j, kk: (kk, j))
    if out_blocked:
        out_spec = pl.BlockSpec((None, tm, tn), lambda i, j, kk: (j, i, 0))
        out_shape = (gn, m, tn)
    else:
        out_spec = pl.BlockSpec((tm, tn), lambda i, j, kk: (i, j))
        out_shape = (m, n)
    extra_specs = [pl.BlockSpec((tm, tn), functools.partial(lambda i, j, kk, off: (i, j + off), off=off))
                   for _, off in extras]
    n_extra, n_out = len(extras), len(out_dtypes)
    n_after = 0 if after is None else 1
    places = out_places if out_places is not None else (None,) * n_out

    def body(a_ref, b_ref, *rest):
        extra_refs = rest[:n_extra]
        out_refs = rest[n_extra + n_after:n_extra + n_after + n_out]

        def finish(acc):
            if epilogue is None:
                res = (acc,)
            else:
                res = epilogue(acc, *[e[...] for e in extra_refs])
            for o_ref, r in zip(out_refs, res):
                o_ref[...] = r.astype(o_ref.dtype)

        if mode == "nt" and b_blocked:
            bk = tk // kb
            part = lax.dot_general(a_ref[:, :bk], b_ref[0], dims, preferred_element_type=F32)
            for h in range(1, kb):
                part = part + lax.dot_general(a_ref[:, h * bk:(h + 1) * bk], b_ref[h], dims,
                                              preferred_element_type=F32)
        else:
            part = lax.dot_general(a_ref[...], b_ref[...], dims, preferred_element_type=F32)
        if gk == 1:
            finish(part)
        else:
            acc_ref = rest[-1]
            kk = pl.program_id(2)

            @pl.when(kk == 0)
            def _():
                acc_ref[...] = part

            @pl.when(kk > 0)
            def _():
                acc_ref[...] += part

            @pl.when(kk == gk - 1)
            def _():
                finish(acc_ref[...])

    outs = pl.pallas_call(
        body, name=name, grid=(gm, gn, gk),
        in_specs=[a_spec, b_spec] + extra_specs + [ANY_SPEC] * n_after,
        out_specs=[out_spec if place is None else
                   pl.BlockSpec((tm, tn), functools.partial(lambda i, j, kk, off: (i, j + off), off=place[1] // tn))
                   for place in places],
        out_shape=[jax.ShapeDtypeStruct(out_shape if place is None else (m, place[0]), dt)
                   for dt, place in zip(out_dtypes, places)],
        scratch_shapes=[pltpu.VMEM((tm, tn), F32)] if gk > 1 else [],
        compiler_params=_cparams("parallel", "parallel", "arbitrary"),
    )(a, b, *[e for e, _ in extras], *([] if after is None else [after]))
    return outs[0] if n_out == 1 else outs


def _gate_mix(ya_pre, s, wpo, wco, proj, d_model, name):
    lp, width = ya_pre.shape
    nb, _, bn = wpo.shape
    ga_off = (proj.shape[1] - 2 * d_model) // bn
    gb_off = (proj.shape[1] - d_model) // bn

    def body(ya_ref, s_ref, wpo_ref, wco_ref, ga_ref, gb_ref, m_ref, y_a_ref, y_b_ref):
        y_a = jnp.dot(ya_ref[...], wpo_ref[...], preferred_element_type=F32)
        y_b = jnp.dot(s_ref[...], wco_ref[...], preferred_element_type=F32)
        m = jax.nn.sigmoid(ga_ref[...]) * y_a + jax.nn.sigmoid(gb_ref[...]) * y_b
        m_ref[...] = m.astype(BF16)
        y_a_ref[...] = y_a.astype(BF16)
        y_b_ref[...] = y_b.astype(BF16)

    act_spec = pl.BlockSpec((lp, width), lambda j: (0, 0))
    w_spec = pl.BlockSpec((None, width, bn), lambda j: (j, 0, 0))
    out_spec = pl.BlockSpec((lp, bn), lambda j: (0, j))
    return pl.pallas_call(
        body, name=name, grid=(nb,),
        in_specs=[act_spec, act_spec, w_spec, w_spec,
                  pl.BlockSpec((lp, bn), lambda j: (0, j + ga_off)),
                  pl.BlockSpec((lp, bn), lambda j: (0, j + gb_off))],
        out_specs=[out_spec] * 3,
        out_shape=[jax.ShapeDtypeStruct((lp, nb * bn), BF16)] * 3,
        compiler_params=_cparams("parallel"),
    )(ya_pre, s, wpo, wco, proj, proj)


def _rms_stats(x):
    return lax.rsqrt(jnp.mean(x * x, axis=-1, keepdims=True) + RMS_EPS)


def _rms_bwd(x, g, dy):
    r = _rms_stats(x)
    nrm = x * r
    dn = dy * g
    dx = r * (dn - nrm * jnp.mean(dn * nrm, axis=-1, keepdims=True))
    return dx, dy * nrm


def _rowwise(body, ins, outs, accs, *, lp, name):
    tr = _row_tile(lp, max(a.shape[1] for a in ins))
    n_in, n_out, n_acc = len(ins), len(outs), len(accs)

    def kernel_body(*refs):
        i = pl.program_id(0)
        acc_refs = refs[n_in + n_out:]

        @pl.when(i == 0)
        def _():
            for r in acc_refs:
                r[...] = jnp.zeros_like(r)

        body(i * tr, refs[:n_in], refs[n_in:n_in + n_out], acc_refs)

    in_specs = []
    for a in ins:
        if a.shape[0] == lp:
            in_specs.append(pl.BlockSpec((tr, a.shape[1]), lambda i: (i, 0)))
        else:
            in_specs.append(pl.BlockSpec(a.shape, lambda i: (0, 0)))
    out_specs = [pl.BlockSpec((tr, w), lambda i: (i, 0)) for w, _ in outs]
    out_specs += [pl.BlockSpec((SUBLANES, w), lambda i: (0, 0)) for w in accs]
    out_shape = [jax.ShapeDtypeStruct((lp, w), d) for w, d in outs]
    out_shape += [jax.ShapeDtypeStruct((SUBLANES, w), F32) for w in accs]
    return pl.pallas_call(
        kernel_body, name=name, grid=(lp // tr,), in_specs=in_specs, out_specs=out_specs,
        out_shape=out_shape, compiler_params=_cparams("arbitrary"),
    )(*ins)


def _rms_pre(h0, g, lp):
    d = h0.shape[1]

    def body(row0, ins, outs, accs):
        h_ref, g_ref = ins
        x = h_ref[...]
        outs[0][...] = (x * _rms_stats(x) * g_ref[...]).astype(BF16)

    return _rowwise(body, [h0, g], [(d, BF16)], [], lp=lp, name="rms_pre")[0]


def _post_mix(o, h0, g_post_mix, g_pre_mlp, lp):
    d = h0.shape[1]

    def body(row0, ins, outs, accs):
        o_ref, h0_ref, g1_ref, g2_ref = ins
        o_v = o_ref[...]
        h1 = h0_ref[...] + o_v * _rms_stats(o_v) * g1_ref[...]
        outs[0][...] = h1
        outs[1][...] = (h1 * _rms_stats(h1) * g2_ref[...]).astype(BF16)

    return _rowwise(body, [o, h0, g_post_mix, g_pre_mlp], [(d, F32), (d, BF16)], [], lp=lp, name="post_mix")


def _loss_head(f, h1, target, g_post_mlp, lp, seq):
    d = f.shape[1]

    def body(row0, ins, outs, accs):
        f_ref, h1_ref, t_ref, g_ref = ins
        df_ref, dh_ref = outs
        dg_ref, loss_ref = accs
        f_v, g = f_ref[...], g_ref[...]
        r = _rms_stats(f_v)
        nrm = f_v * r
        rows = row0 + lax.broadcasted_iota(jnp.int32, (f_v.shape[0], 1), 0)
        valid = (rows >= N_META) & (rows < N_META + seq)
        err = jnp.where(valid, h1_ref[...] + nrm * g - t_ref[...], 0.0)
        loss_ref[...] += 0.5 * jnp.sum(jnp.mean(err * err, axis=-1, keepdims=True))
        dy = err * (1.0 / d)
        dn = dy * g
        df_ref[...] = (r * (dn - nrm * jnp.mean(dn * nrm, axis=-1, keepdims=True))).astype(BF16)
        dh_ref[...] = dy
        dg_ref[...] += _rowsum8(dy * nrm)

    return _rowwise(body, [f, h1, target, g_post_mlp], [(d, BF16), (d, F32)], [d, LANES], lp=lp, name="loss_head")


def _mid_bwd(du2, h1, dh, o, g_pre_mlp, g_post_mix, lp):
    d = h1.shape[1]

    def body(row0, ins, outs, accs):
        du2_ref, h1_ref, dh_ref, o_ref, g2_ref, g1_ref = ins
        dx2, dg2 = _rms_bwd(h1_ref[...], g2_ref[...], du2_ref[...])
        dh1 = dh_ref[...] + dx2
        do, dg1 = _rms_bwd(o_ref[...], g1_ref[...], dh1)
        outs[0][...] = dh1
        outs[1][...] = do.astype(BF16)
        accs[0][...] += _rowsum8(dg2)
        accs[1][...] += _rowsum8(dg1)

    return _rowwise(body, [du2, h1, dh, o, g_pre_mlp, g_post_mix], [(d, F32), (d, BF16)], [d, d], lp=lp,
                    name="mid_bwd")


def _pre_mix_bwd(du1, h0, dh1, g_pre_mix, lp):
    d = h0.shape[1]

    def body(row0, ins, outs, accs):
        du1_ref, h0_ref, dh1_ref, g_ref = ins
        dx, dg = _rms_bwd(h0_ref[...], g_ref[...], du1_ref[...])
        outs[0][...] = dh1_ref[...] + dx
        accs[0][...] += _rowsum8(dg)

    return _rowwise(body, [du1, h0, dh1, g_pre_mix], [(d, F32)], [d], lp=lp, name="pre_mix_bwd")


def _ln_stats(c):
    mu = jnp.mean(c, axis=-1, keepdims=True)
    var = jnp.mean(jnp.square(c - mu), axis=-1, keepdims=True)
    return mu, lax.rsqrt(var + LN_EPS)


def _ln_silu(c, ln_g, ln_b, lp):
    w = c.shape[1]

    def body(row0, ins, outs, accs):
        c_ref, g_ref, b_ref = ins
        c_v = c_ref[...]
        mu, rstd = _ln_stats(c_v)
        ln = (c_v - mu) * rstd * g_ref[...] + b_ref[...]
        outs[0][...] = (ln * jax.nn.sigmoid(ln)).astype(BF16)

    return _rowwise(body, [c, ln_g, ln_b], [(w, BF16)], [], lp=lp, name="ln_silu")[0]


def _ln_silu_bwd(c, ds, ln_g, ln_b, lp):
    w = c.shape[1]

    def body(row0, ins, outs, accs):
        c_ref, ds_ref, g_ref, b_ref = ins
        c_v, g = c_ref[...], g_ref[...]
        mu, rstd = _ln_stats(c_v)
        nrm = (c_v - mu) * rstd
        ln = nrm * g + b_ref[...]
        sig = jax.nn.sigmoid(ln)
        dln = ds_ref[...] * (sig * (1.0 + ln * (1.0 - sig)))
        dn = dln * g
        dc = rstd * (dn - jnp.mean(dn, axis=-1, keepdims=True) - nrm * jnp.mean(dn * nrm, axis=-1, keepdims=True))
        outs[0][...] = dc
        accs[0][...] += _rowsum8(dln * nrm)
        accs[1][...] += _rowsum8(dln)
        accs[2][...] += _rowsum8(dc)

    return _rowwise(body, [c, ds, ln_g, ln_b], [(w, F32)], [w, w, w], lp=lp, name="ln_silu_bwd")


def _chunk_with_history(ref, i, cols=slice(None)):
    t0 = pl.multiple_of(i * ROW_CHUNK, ROW_CHUNK)
    lo0 = pl.multiple_of(jnp.maximum(t0 - HALO, 0), SUBLANES)
    lo = jnp.where(i > 0, ref[pl.ds(lo0, HALO), cols], 0.0)
    return jnp.concatenate([lo, ref[pl.ds(t0, ROW_CHUNK), cols]], axis=0)


def _chunk_with_future(ref, i, n_chunks, cols=slice(None)):
    t0 = pl.multiple_of(i * ROW_CHUNK, ROW_CHUNK)
    hi0 = pl.multiple_of(jnp.minimum(t0 + ROW_CHUNK, (n_chunks - 1) * ROW_CHUNK), SUBLANES)
    hi = jnp.where(i < n_chunks - 1, ref[pl.ds(hi0, HALO), cols], 0.0)
    return jnp.concatenate([ref[pl.ds(t0, ROW_CHUNK), cols], hi], axis=0)


def _inv_count(t0, n_rows, window):
    pos = t0 + lax.broadcasted_iota(jnp.int32, (n_rows, 1), 0)
    return 1.0 / jnp.minimum(pos + 1, window).astype(F32)


def _pool_delta(z_hist, t0, window):
    s = z_hist
    sh = 1
    while sh < window:
        s = s + pltpu.roll(s, sh, 0)
        sh *= 2
    cur = z_hist[HALO:, :]
    return s[HALO:, :] * _inv_count(t0, ROW_CHUNK, window) - cur


def _pool_fwd(proj, wpg, pool_scale, lp):
    n_grp, gdim, _ = wpg.shape
    width = n_grp * gdim
    n_chunks = lp // ROW_CHUNK

    def body(z_ref, w_ref, sc_ref, out_ref):
        for g, window in enumerate(POOL_WINDOWS):
            cols = slice(g * gdim, (g + 1) * gdim)

            def chunk(i, carry, cols=cols, g=g, window=window):
                t0 = pl.multiple_of(i * ROW_CHUNK, ROW_CHUNK)
                d = _pool_delta(_chunk_with_history(z_ref, i, cols), t0, window)
                q = jnp.dot(d.astype(BF16), w_ref[g], preferred_element_type=F32)
                out_ref[pl.ds(t0, ROW_CHUNK), cols] = (q * sc_ref[:, cols]).astype(BF16)
                return carry

            lax.fori_loop(0, n_chunks, chunk, 0)

    return pl.pallas_call(
        body, name="pool_fwd", grid=(1,),
        in_specs=[pl.BlockSpec((lp, width), lambda i: (0, 0)),
                  pl.BlockSpec(wpg.shape, lambda i: (0, 0, 0)),
                  pl.BlockSpec(pool_scale.shape, lambda i: (0, 0))],
        out_specs=pl.BlockSpec((lp, width), lambda i: (0, 0)),
        out_shape=jax.ShapeDtypeStruct((lp, width), BF16),
        compiler_params=_cparams("arbitrary"),
    )(proj, wpg, pool_scale)


def _pool_bwd(proj, d_ya, wpg, pool_scale, dproj, lp):
    n_grp, gdim, _ = wpg.shape
    width = n_grp * gdim
    n_chunks = lp // ROW_CHUNK
    ext = ROW_CHUNK + HALO

    def body(z_ref, dya_ref, w_ref, sc_ref, _, dz_ref, dw_ref, dsc_ref):
        dw_ref[...] = jnp.zeros_like(dw_ref)
        dsc_ref[...] = jnp.zeros_like(dsc_ref)
        for g, window in enumerate(POOL_WINDOWS):
            cols = slice(g * gdim, (g + 1) * gdim)

            def chunk(i, carry, cols=cols, g=g, window=window):
                t0 = pl.multiple_of(i * ROW_CHUNK, ROW_CHUNK)
                w_g = w_ref[g]
                scale = sc_ref[:, cols]
                d = _pool_delta(_chunk_with_history(z_ref, i, cols), t0, window).astype(BF16)
                dya_ext = _chunk_with_future(dya_ref, i, n_chunks, cols)
                dya = dya_ext[:ROW_CHUNK, :]
                q = jnp.dot(d, w_g, preferred_element_type=F32)
                dsc_ref[:, cols] += _rowsum8(dya * q)
                e_ext = (dya_ext * scale).astype(BF16)
                dw_ref[g] += lax.dot_general(d, e_ext[:ROW_CHUNK, :], (((0,), (0,)), ((), ())),
                                             preferred_element_type=F32)
                dd_ext = lax.dot_general(e_ext, w_g, (((1,), (1,)), ((), ())), preferred_element_type=F32)
                s = dd_ext * _inv_count(t0, ext, window)
                sh = 1
                while sh < window:
                    s = s + pltpu.roll(s, ext - sh, 0)
                    sh *= 2
                dz_ref[pl.ds(t0, ROW_CHUNK), cols] = (s[:ROW_CHUNK, :] - dd_ext[:ROW_CHUNK, :]).astype(BF16)
                return carry

            lax.fori_loop(0, n_chunks, chunk, 0)

    blk = pl.BlockSpec((lp, width), lambda i: (0, 0))
    return pl.pallas_call(
        body, name="pool_bwd", grid=(1,),
        in_specs=[blk, blk, pl.BlockSpec(wpg.shape, lambda i: (0, 0, 0)),
                  pl.BlockSpec(pool_scale.shape, lambda i: (0, 0)), ANY_SPEC],
        out_specs=[blk, pl.BlockSpec(wpg.shape, lambda i: (0, 0, 0)),
                   pl.BlockSpec((SUBLANES, width), lambda i: (0, 0))],
        out_shape=[jax.ShapeDtypeStruct(dproj.shape, BF16), jax.ShapeDtypeStruct(wpg.shape, F32),
                   jax.ShapeDtypeStruct((SUBLANES, width), F32)],
        input_output_aliases={4: 0},
        compiler_params=_cparams("arbitrary"),
    )(proj, d_ya, wpg, pool_scale, dproj)


def _conv_fwd(proj, w_dw, b_dw, lp, width, v_col0):
    n_chunks = lp // ROW_CHUNK
    v_blk0, g_blk0 = v_col0 // LANES, (v_col0 + width) // LANES

    def body(v_ref, gc_ref, w_ref, b_ref, c_ref, a_pad):
        a_pad[pl.ds(0, HALO), :] = jnp.zeros((HALO, LANES), F32)
        a_pad[pl.ds(HALO, lp), :] = v_ref[...] * jax.nn.sigmoid(gc_ref[...])

        def chunk(i, carry):
            t0 = pl.multiple_of(i * ROW_CHUNK, ROW_CHUNK)
            hist = a_pad[pl.ds(t0, ROW_CHUNK + HALO), :]
            acc = jnp.zeros((ROW_CHUNK, LANES), F32)
            for k in range(CONV_KERNEL):
                acc = acc + w_ref[k:k + 1, :] * pltpu.roll(hist, CONV_KERNEL - 1 - k, 0)[HALO:, :]
            c_ref[pl.ds(t0, ROW_CHUNK), :] = acc + b_ref[...]
            return carry

        lax.fori_loop(0, n_chunks, chunk, 0)

    return pl.pallas_call(
        body, name="conv_fwd", grid=(width // LANES,),
        in_specs=[pl.BlockSpec((lp, LANES), lambda j: (0, j + v_blk0)),
                  pl.BlockSpec((lp, LANES), lambda j: (0, j + g_blk0)),
                  pl.BlockSpec((CONV_TAPS_PADDED, LANES), lambda j: (0, j)),
                  pl.BlockSpec((1, LANES), lambda j: (0, j))],
        out_specs=pl.BlockSpec((lp, LANES), lambda j: (0, j)),
        out_shape=jax.ShapeDtypeStruct((lp, width), F32),
        scratch_shapes=[pltpu.VMEM((lp + HALO, LANES), F32)],
        compiler_params=_cparams("parallel"),
    )(proj, proj, w_dw, b_dw)


def _conv_bwd(proj, dc, w_dw, dproj, lp, width, v_col0):
    n_chunks = lp // ROW_CHUNK
    ext = ROW_CHUNK + HALO
    v_blk0, g_blk0 = v_col0 // LANES, (v_col0 + width) // LANES

    def body(v_ref, gc_ref, dc_ref, w_ref, _, dv_ref, dgc_ref, dw_ref, a_pad, dc_pad, dw_acc):
        sig = jax.nn.sigmoid(gc_ref[...])
        a_pad[pl.ds(0, HALO), :] = jnp.zeros((HALO, LANES), F32)
        a_pad[pl.ds(HALO, lp), :] = v_ref[...] * sig
        dc_pad[pl.ds(0, lp), :] = dc_ref[...]
        dc_pad[pl.ds(lp, HALO), :] = jnp.zeros((HALO, LANES), F32)
        dw_acc[...] = jnp.zeros_like(dw_acc)

        def chunk(i, carry):
            t0 = pl.multiple_of(i * ROW_CHUNK, ROW_CHUNK)
            hist = a_pad[pl.ds(t0, ext), :]
            fut = dc_pad[pl.ds(t0, ext), :]
            dc_cur = fut[:ROW_CHUNK, :]
            da = jnp.zeros((ROW_CHUNK, LANES), F32)
            for k in range(CONV_KERNEL):
                lag = CONV_KERNEL - 1 - k
                da = da + w_ref[k:k + 1, :] * pltpu.roll(fut, (ext - lag) % ext, 0)[:ROW_CHUNK, :]
                dw_acc[pl.ds(SUBLANES * k, SUBLANES), :] += _rowsum8(dc_cur * pltpu.roll(hist, lag, 0)[HALO:, :])
            rows = pl.ds(t0, ROW_CHUNK)
            sg = jax.nn.sigmoid(gc_ref[rows, :])
            dv_ref[rows, :] = (da * sg).astype(BF16)
            dgc_ref[rows, :] = (da * v_ref[rows, :] * sg * (1.0 - sg)).astype(BF16)
            return carry

        lax.fori_loop(0, n_chunks, chunk, 0)
        dw_ref[...] = dw_acc[...].reshape(CONV_TAPS_PADDED, SUBLANES, LANES).sum(axis=1)

    col = lambda j: (0, j)
    return pl.pallas_call(
        body, name="conv_bwd", grid=(width // LANES,),
        in_specs=[pl.BlockSpec((lp, LANES), lambda j: (0, j + v_blk0)),
                  pl.BlockSpec((lp, LANES), lambda j: (0, j + g_blk0)),
                  pl.BlockSpec((lp, LANES), col),
                  pl.BlockSpec((CONV_TAPS_PADDED, LANES), col), ANY_SPEC],
        out_specs=[pl.BlockSpec((lp, LANES), lambda j: (0, j + v_blk0)), pl.BlockSpec((lp, LANES), col),
                   pl.BlockSpec((CONV_TAPS_PADDED, LANES), col)],
        out_shape=[jax.ShapeDtypeStruct(dproj.shape, BF16), jax.ShapeDtypeStruct((lp, width), BF16),
                   jax.ShapeDtypeStruct((CONV_TAPS_PADDED, width), F32)],
        scratch_shapes=[pltpu.VMEM((lp + HALO, LANES), F32), pltpu.VMEM((lp + HALO, LANES), F32),
                        pltpu.VMEM((CONV_TAPS_PADDED * SUBLANES, LANES), F32)],
        input_output_aliases={4: 0},
        compiler_params=_cparams("parallel"),
    )(proj, proj, dc, w_dw, dproj)


def _place_columns(dst, pieces, name):
    m = dst.shape[0]
    tile = 512
    counts = [p.shape[1] // tile for p, _ in pieces]
    starts = [sum(counts[:i]) for i in range(len(pieces))]
    n_steps = sum(counts)

    def local(s, i):
        return jnp.clip(s - starts[i], 0, counts[i] - 1)

    def out_index(s):
        blk = pieces[0][1] // tile + local(s, 0)
        for i in range(1, len(pieces)):
            blk = jnp.where(s >= starts[i], pieces[i][1] // tile + local(s, i), blk)
        return 0, blk

    def body(*refs):
        out_ref = refs[-1]
        s = pl.program_id(0)
        for i in range(len(pieces)):
            @pl.when((s >= starts[i]) & (s < starts[i] + counts[i]))
            def _(i=i):
                out_ref[...] = refs[i][...]

    return pl.pallas_call(
        body, name=name, grid=(n_steps,),
        in_specs=[pl.BlockSpec((m, tile), functools.partial(lambda s, i: (0, local(s, i)), i=i))
                  for i in range(len(pieces))] + [ANY_SPEC],
        out_specs=pl.BlockSpec((m, tile), out_index),
        out_shape=jax.ShapeDtypeStruct(dst.shape, dst.dtype),
        input_output_aliases={len(pieces): 0},
        compiler_params=_cparams("arbitrary"),
    )(*[p for p, _ in pieces], dst)


def _adamw_math(w, g, m, v):
    m = ADAM_B1 * m + (1.0 - ADAM_B1) * g
    v = ADAM_B2 * v + (1.0 - ADAM_B2) * jnp.square(g)
    m_hat = m / (1.0 - ADAM_B1 ** ADAM_STEP)
    v_hat = v / (1.0 - ADAM_B2 ** ADAM_STEP)
    delta = -ADAM_LR * (m_hat / (jnp.sqrt(v_hat) + ADAM_EPS) + ADAM_WD * w)
    return delta, m, v


def _pair_sum(own, recv, where, name):
    _, _, rows, cols = own.shape
    tr = _row_tile(rows, cols, 1024 * 1024)

    def body(where_ref, own_ref, recv_ref, out_ref):
        out_ref[...] = (own_ref[...].astype(F32) + recv_ref[...].astype(F32)).astype(BF16)

    return pl.pallas_call(
        body, name=name,
        grid_spec=pltpu.PrefetchScalarGridSpec(
            num_scalar_prefetch=1, grid=(3, rows // tr),
            in_specs=[pl.BlockSpec((None, None, tr, cols), lambda r, i, wh: (wh[2 + r], wh[0], i, 0)),
                      pl.BlockSpec((None, tr, cols), lambda r, i, wh: (wh[2 + r], i, 0))],
            out_specs=pl.BlockSpec((None, tr, cols), lambda r, i, wh: (wh[2 + r], i, 0))),
        out_shape=jax.ShapeDtypeStruct((4, rows, cols), BF16),
        compiler_params=_cparams("parallel", "parallel"),
    )(where, own, recv)


def _adamw_big(w, m, v, own, from_sibling, recv3, where, name):
    rows, cols = w.shape
    tr = _row_tile(rows, cols, 384 * 1024)

    def body(where_ref, w_ref, m_ref, v_ref, own_ref, sib_ref, r_ref, g_out, d_out, m_out, v_out):
        g = own_ref[...].astype(F32) + sib_ref[...].astype(F32)
        for r in range(3):
            g = g + r_ref[r].astype(F32)
        delta, m_new, v_new = _adamw_math(w_ref[...], g, m_ref[...], v_ref[...])
        g_out[...] = g
        d_out[...] = delta
        m_out[...] = m_new
        v_out[...] = v_new

    blk = pl.BlockSpec((tr, cols), lambda i, q_ref: (i, 0))
    return pl.pallas_call(
        body, name=name,
        grid_spec=pltpu.PrefetchScalarGridSpec(
            num_scalar_prefetch=1, grid=(rows // tr,),
            in_specs=[blk, blk, blk,
                      pl.BlockSpec((None, None, tr, cols), lambda i, wh: (wh[1], wh[0], i, 0)),
                      pl.BlockSpec((None, tr, cols), lambda i, wh: (wh[1], i, 0)),
                      pl.BlockSpec((3, tr, cols), lambda i, wh: (0, i, 0))],
            out_specs=[blk] * 4),
        out_shape=[jax.ShapeDtypeStruct((rows, cols), F32)] * 4,
        compiler_params=_cparams("parallel"),
    )(where, w, m, v, own, from_sibling, recv3)


def _small_update(me_idx, packed, rep_params, rep_places, meta_wmv, meta_row0, wdw_wmv, wdw_row0, loss_row0):
    n_rep = len(rep_params)
    meta_rows, meta_cols = meta_wmv[0].shape
    wdw_rows, wdw_cols = wdw_wmv[0].shape

    def body(me_ref, *refs):
        pos = 0

        def take(k):
            nonlocal pos
            out = refs[pos:pos + k]
            pos += k
            return out

        rep_in = [take(3) for _ in range(n_rep)]
        rep_g = take(n_rep)
        meta_in, (meta_g,) = take(3), take(1)
        wdw_in, (wdw_g,) = take(3), take(1)
        (loss_ref,) = take(1)
        rep_out = [take(4) for _ in range(n_rep)]
        meta_out, wdw_out = take(4), take(4)
        (loss_out,) = take(1)

        def update(wmv, g, outs):
            delta, m_new, v_new = _adamw_math(wmv[0][...], g, wmv[1][...], wmv[2][...])
            for o_ref, val in zip(outs, (g, delta, m_new, v_new)):
                o_ref[...] = val

        for wmv, g_ref, outs in zip(rep_in, rep_g, rep_out):
            g = jnp.sum(g_ref[0], axis=0, keepdims=True)
            for j in range(1, N_DEV):
                g = g + jnp.sum(g_ref[j], axis=0, keepdims=True)
            update(wmv, g, outs)
        for wmv, g_ref, outs in ((meta_in, meta_g, meta_out), (wdw_in, wdw_g, wdw_out)):
            g = g_ref[0]
            for j in range(1, N_DEV):
                g = g + g_ref[j]
            update(wmv, g, outs)
        total = loss_ref[0]
        for j in range(1, N_DEV):
            total = total + loss_ref[j]
        loss_out[...] = total

    def whole(a):
        nd = a.ndim
        return pl.BlockSpec(a.shape, lambda i, me_ref, nd=nd: (0,) * nd)

    ins, in_specs = [], []
    for wmv in rep_params:
        ins += list(wmv)
        in_specs += [whole(a) for a in wmv]
    for wmv, (row0, col0) in zip(rep_params, rep_places):
        width = wmv[0].shape[1]
        ins.append(packed)
        in_specs.append(pl.BlockSpec((N_DEV, SUBLANES, width),
                                     lambda i, me_ref, rb=row0 // SUBLANES, cb=col0 // width: (0, rb, cb)))
    ins += list(meta_wmv) + [packed]
    in_specs += [whole(a) for a in meta_wmv]
    in_specs.append(pl.BlockSpec((N_DEV, meta_rows, meta_cols),
                                 lambda i, me_ref, rb=meta_row0 // meta_rows: (0, rb, me_ref[0])))
    ins += list(wdw_wmv) + [packed]
    in_specs += [whole(a) for a in wdw_wmv]
    in_specs.append(pl.BlockSpec((N_DEV, wdw_rows, wdw_cols),
                                 lambda i, me_ref, rb=wdw_row0 // wdw_rows: (0, rb, me_ref[0])))
    ins.append(packed)
    in_specs.append(pl.BlockSpec((N_DEV, SUBLANES, LANES), lambda i, me_ref, rb=loss_row0 // SUBLANES: (0, rb, 0)))

    out_shape, out_specs = [], []
    for wmv in list(rep_params) + [meta_wmv, wdw_wmv]:
        out_shape += [jax.ShapeDtypeStruct(wmv[0].shape, F32)] * 4
        out_specs += [whole(wmv[0])] * 4
    out_shape.append(jax.ShapeDtypeStruct((SUBLANES, LANES), F32))
    out_specs.append(pl.BlockSpec((SUBLANES, LANES), lambda i, me_ref: (0, 0)))

    outs = pl.pallas_call(
        body, name="small_update",
        grid_spec=pltpu.PrefetchScalarGridSpec(num_scalar_prefetch=1, grid=(1,), in_specs=in_specs,
                                               out_specs=out_specs),
        out_shape=out_shape, compiler_params=_cparams("arbitrary"),
    )(me_idx, *ins)
    groups = [outs[4 * i:4 * i + 4] for i in range(n_rep + 2)]
    return groups[:n_rep], groups[n_rep], groups[n_rep + 1], outs[-1]


def kernel(x, meta, g_pre_mix, w_in, w_pool_grp, pool_scale, w_pool_out, w_dw, b_dw, conv_ln_g, conv_ln_b, w_conv_out, w_o, g_post_mix, g_pre_mlp, w_up, w_down, g_post_mlp, loss_target, m_meta, m_g_pre_mix, m_w_in, m_w_pool_grp, m_pool_scale, m_w_pool_out, m_w_dw, m_b_dw, m_conv_ln_g, m_conv_ln_b, m_w_conv_out, m_w_o, m_g_post_mix, m_g_pre_mlp, m_w_up, m_w_down, m_g_post_mlp, v_meta, v_g_pre_mix, v_w_in, v_w_pool_grp, v_pool_scale, v_w_pool_out, v_w_dw, v_b_dw, v_conv_ln_g, v_conv_ln_b, v_w_conv_out, v_w_o, v_g_post_mix, v_g_pre_mlp, v_w_up, v_w_down, v_g_post_mlp):
    seq, d = x.shape[1], x.shape[2]
    pool_w = pool_scale.shape[1]
    conv_w = b_dw.shape[1]
    n_grp, grp_rows, gdim = w_pool_grp.shape[1:]
    lp = _round_up(N_META + seq, ROW_CHUNK)
    tm_half = lp // 2 if (lp // 2) % 16 == 0 else lp
    c_idx = lax.axis_index("c").astype(jnp.int32)
    chip_idx = (2 * lax.axis_index("x") + lax.axis_index("y")).astype(jnp.int32)
    me_idx = 2 * chip_idx + c_idx

    pad_taps = ((0, CONV_TAPS_PADDED - CONV_KERNEL), (0, 0))
    big = dict(w_in=w_in[0], w_pool_grp=w_pool_grp[0].reshape(n_grp * grp_rows, gdim), w_pool_out=w_pool_out[0],
               w_conv_out=w_conv_out[0], w_o=w_o[0], w_up=w_up[0], w_down=w_down[0])
    big_names = list(big)
    moments = dict(w_in=(m_w_in, v_w_in), w_pool_grp=(m_w_pool_grp, v_w_pool_grp), w_pool_out=(m_w_pool_out, v_w_pool_out),
                   w_conv_out=(m_w_conv_out, v_w_conv_out), w_o=(m_w_o, v_w_o), w_up=(m_w_up, v_w_up),
                   w_down=(m_w_down, v_w_down))
    slot_idx = me_idx.reshape(1)
    sources = dict(big, meta=meta, w_dw=jnp.pad(w_dw[0], pad_taps))

    def fill(k, after):
        return _fill_slot(sources[k], slot_idx, BF16 if k in big else F32, after, "fill_" + k)

    gather_groups = [["meta", "w_dw", "w_in"], ["w_pool_grp", "w_pool_out", "w_conv_out", "w_o"], ["w_up"], ["w_down"]]
    ring = [dict() for _ in gather_groups]
    wg = {}

    def ring_start(gis, lands, after, tag):
        started, token = _ring_start(lands, after, "ring_start_" + tag)
        for gi, (send, recv, group_lands) in zip(gis, started):
            ring[gi].update(send=send, recv=recv, lands=group_lands)
        return token

    def ring_relay(gi, after):
        st = ring[gi]
        st["fsend"], st["frecv"], st["lands"], token = _ring_relay(st["lands"], st["recv"], after, f"ring_relay_{gi}")
        return token

    def ring_complete(gi, after):
        st = ring[gi]
        gsend, grecv, lands, token = _ring_relay_far(st["lands"], st["frecv"], after, f"ring_relay_far_{gi}")
        wg.update(zip(gather_groups[gi], _ring_finish(lands, st["send"], st["recv"], st["fsend"], st["frecv"],
                                                      gsend, grecv, token, f"ring_finish_{gi}")))

    token = ring_start([0], [[fill(k, slot_idx) for k in gather_groups[0]]], slot_idx, "0")
    later = [[fill(k, token) for k in names] for names in gather_groups[1:]]
    token = ring_relay(0, later[-1][-1])
    token = ring_start([1, 2], later[:2], token, "1")
    ring_complete(0, token)
    meta_full = wg["meta"].transpose(1, 0, 2).reshape(N_META, d)
    wdw_full = wg["w_dw"].transpose(1, 0, 2).reshape(CONV_TAPS_PADDED, conv_w)
    tail = lp - N_META - seq
    h0 = jnp.concatenate([meta_full, x[0], jnp.zeros((tail, d), F32)], axis=0)
    target = jnp.pad(loss_target[0], ((N_META, tail), (0, 0)))
    u1 = _rms_pre(h0, g_pre_mix, lp)
    proj = _mm(u1, wg["w_in"], mode="nn", tm=tm_half, tn=wg["w_in"].shape[2], b_blocked=True, name="mm_proj")
    token = ring_relay(1, proj)
    ring_start([3], later[2:], token, "2")
    conv_c = _conv_fwd(proj, wdw_full, b_dw, lp, conv_w, pool_w)
    s_act = _ln_silu(conv_c, conv_ln_g, conv_ln_b, lp)
    token = ring_relay(2, s_act)
    ring_complete(1, token)
    wpg_full = wg["w_pool_grp"].reshape(N_DEV, n_grp, grp_rows, gdim).transpose(1, 0, 2, 3).reshape(n_grp, gdim, gdim)
    w_o_full = wg["w_o"].reshape(d, d)
    ya_pre = _pool_fwd(proj, wpg_full, pool_scale, lp)
    m_mix, y_a, y_b = _gate_mix(ya_pre, s_act, wg["w_pool_out"], wg["w_conv_out"], proj, d, "gate_mix")
    o = _mm(m_mix, w_o_full, mode="nn", tm=tm_half, tn=512, name="mm_o")
    h1, u2 = _post_mix(o, h0, g_post_mix, g_pre_mlp, lp)
    token = ring_relay(3, u2)
    ring_complete(2, token)
    act = _mm(u2, wg["w_up"], mode="nn", tm=tm_half, tn=wg["w_up"].shape[2], b_blocked=True, out_dtypes=(BF16,),
              epilogue=lambda acc: (jnp.square(jnp.maximum(acc, 0.0)),), name="mm_up")
    ring_complete(3, act)
    w_down_full = wg["w_down"].reshape(-1, d)
    f = _mm(act, w_down_full, mode="nn", tm=tm_half, tn=512, tk=4096, name="mm_down")

    big_out = {}

    def to_sibling(names, grads, after, tag):
        send, recv, grads, lands, token = _pair_exchange_start(grads, after, "grads_to_sibling_start_" + tag)
        return (names, send, recv, grads, lands, tag), token

    x_idx, y_idx = lax.axis_index("x"), lax.axis_index("y")
    where = jnp.stack([c_idx, chip_idx, 2 * (1 - x_idx) + y_idx, 2 * x_idx + (1 - y_idx),
                       2 * (1 - x_idx) + (1 - y_idx)]).astype(jnp.int32)

    def to_owner(handle, after):
        names, send, recv, grads, from_sib, tag = handle
        grads, from_sib = _pair_exchange_finish(grads, from_sib, send, recv, after, "grads_to_sibling_finish_" + tag)
        own = [g.reshape((4, 2) + g.shape[1:]) for g in grads]
        sums = [_pair_sum(o, r, where, "pair_sum_" + k) for k, o, r in zip(names, own, from_sib)]
        send, recv, sums, lands, token = _chip_exchange_start(sums, after, "grads_to_owner_start_" + tag)
        return (names, send, recv, sums, lands, own, from_sib, tag), token

    def update(handle, after):
        names, send, recv, sums, lands, own, from_sib, tag = handle
        got = _chip_exchange_finish(sums, lands, send, recv, after, "grads_to_owner_finish_" + tag)
        for k, o, s, r3 in zip(names, own, from_sib, got):
            w2 = big[k]
            shape = moments[k][0].shape
            outs = _adamw_big(w2, moments[k][0].reshape(w2.shape), moments[k][1].reshape(w2.shape), o, s, r3,
                              where, "adamw_" + k)
            big_out[k] = [a.reshape(shape) for a in outs]
        return big_out[names[-1]][0]

    df, dh, dg_post_mlp, loss_part = _loss_head(f, h1, target, g_post_mlp, lp, seq)
    d_up = _mm(df, w_down_full, mode="nt", tm=tm_half, tn=1024, out_dtypes=(BF16,), extras=[(act, 0)],
               epilogue=lambda acc, a: (acc * (2.0 * jnp.sqrt(a.astype(F32))),), name="mm_d_up")
    g_w_down = _mm(act, df, mode="tn", tm=1024, tn=1024, out_dtypes=(BF16,), name="mm_g_down")
    sib_down, token = to_sibling(["w_down"], [g_w_down.reshape(N_DEV, -1, d)], slot_idx, "down")
    g_w_up = _mm(u2, d_up, mode="tn", tm=1024, tn=wg["w_up"].shape[2], out_blocked=True, out_dtypes=(BF16,),
                 after=token, name="mm_g_up")
    sib_up, token = to_sibling(["w_up"], [g_w_up], slot_idx, "up")
    pending_down, token = to_owner(sib_down, token)
    du2 = _mm(d_up, wg["w_up"], mode="nt", tm=tm_half, tn=1024, b_blocked=True, kb=2, after=token, name="mm_du2")
    pending_up, token = to_owner(sib_up, du2)
    dh1, do, dg_pre_mlp, dg_post_mix = _mid_bwd(du2, h1, dh, o, g_pre_mlp, g_post_mix, lp)

    def gate_bwd(dm, ga, gb, ya, yb):
        sa, sb = jax.nn.sigmoid(ga), jax.nn.sigmoid(gb)
        return (dm * ya.astype(F32) * sa * (1.0 - sa), dm * yb.astype(F32) * sb * (1.0 - sb), dm * sa, dm * sb)

    gate_tn = 512
    ga_col0, gb_col0 = proj.shape[1] - 2 * d, proj.shape[1] - d
    dproj, d_gb, d_ya, d_yb = _mm(
        do, w_o_full, mode="nt", tm=tm_half, tn=gate_tn, out_dtypes=(BF16,) * 4,
        extras=[(proj, ga_col0 // gate_tn), (proj, gb_col0 // gate_tn), (y_a, 0), (y_b, 0)], epilogue=gate_bwd,
        out_places=((proj.shape[1], ga_col0), None, None, None), after=token, name="mm_dm")
    g_w_o = _mm(m_mix, do, mode="tn", tm=1024, tn=1024, out_dtypes=(BF16,), name="mm_g_o")
    bn_out = wg["w_pool_out"].shape[2]
    g_w_pool_out = _mm(ya_pre, d_ya, mode="tn", tm=pool_w, tn=bn_out, out_blocked=True, out_dtypes=(BF16,),
                       name="mm_g_pool_out")
    g_w_conv_out = _mm(s_act, d_yb, mode="tn", tm=conv_w, tn=bn_out, out_blocked=True, out_dtypes=(BF16,),
                       name="mm_g_conv_out")
    sib_mix, token = to_sibling(["w_o", "w_pool_out", "w_conv_out"],
                                [g_w_o.reshape(N_DEV, -1, d), g_w_pool_out, g_w_conv_out], slot_idx, "mix")
    d_ya_pre = _mm(d_ya, wg["w_pool_out"], mode="nt", tm=tm_half, tn=pool_w, b_blocked=True, after=token,
                   name="mm_d_ya_pre")
    d_s = _mm(d_yb, wg["w_conv_out"], mode="nt", tm=tm_half, tn=conv_w, b_blocked=True, name="mm_d_s")
    pending_mix, token = to_owner(sib_mix, d_s)
    dproj, g_wpg, d_scale = _pool_bwd(proj, d_ya_pre, wpg_full, pool_scale, dproj, lp)
    dc, d_ln_g, d_ln_b, d_b_dw = _ln_silu_bwd(conv_c, d_s, conv_ln_g, conv_ln_b, lp)
    dproj, dgc, g_wdw = _conv_bwd(proj, dc, wdw_full, dproj, lp, conv_w, pool_w)
    dproj = _place_columns(dproj, [(dgc, pool_w + conv_w), (d_gb, gb_col0)], "place_dproj")
    g_w_in = _mm(u1, dproj, mode="tn", tm=1024, tn=wg["w_in"].shape[2], out_blocked=True, out_dtypes=(BF16,),
                 after=token, name="mm_g_in")
    g_wpg_slots = g_wpg.astype(BF16).reshape(n_grp, N_DEV, grp_rows, gdim).transpose(1, 0, 2, 3)
    sib_in, token = to_sibling(["w_pool_grp", "w_in"],
                               [g_wpg_slots.reshape(N_DEV, n_grp * grp_rows, gdim), g_w_in], slot_idx, "in")
    done = update(pending_down, token)
    pending_in, token = to_owner(sib_in, done)
    done = update(pending_up, token)
    du1 = _mm(dproj, wg["w_in"], mode="nt", tm=tm_half, tn=1024, b_blocked=True, kb=2, after=done, name="mm_du1")
    dh0, dg_pre_mix = _pre_mix_bwd(du1, h0, dh1, g_pre_mix, lp)
    grad_x = dh0[N_META:N_META + seq][None]

    assert pool_w + conv_w == d and conv_w <= d and LANES <= d
    widen = lambda a: jnp.pad(a, ((0, 0), (0, d - a.shape[1])))
    packed = jnp.concatenate([
        dg_pre_mix, dg_post_mix, dg_pre_mlp, dg_post_mlp,
        jnp.concatenate([d_scale, d_ln_g], axis=1), jnp.concatenate([d_ln_b, d_b_dw], axis=1),
        dh0[:N_META], widen(g_wdw), widen(loss_part)], axis=0)
    rep = dict(g_pre_mix=((g_pre_mix, m_g_pre_mix, v_g_pre_mix), (0, 0)),
               g_post_mix=((g_post_mix, m_g_post_mix, v_g_post_mix), (SUBLANES, 0)),
               g_pre_mlp=((g_pre_mlp, m_g_pre_mlp, v_g_pre_mlp), (2 * SUBLANES, 0)),
               g_post_mlp=((g_post_mlp, m_g_post_mlp, v_g_post_mlp), (3 * SUBLANES, 0)),
               pool_scale=((pool_scale, m_pool_scale, v_pool_scale), (4 * SUBLANES, 0)),
               conv_ln_g=((conv_ln_g, m_conv_ln_g, v_conv_ln_g), (4 * SUBLANES, pool_w)),
               conv_ln_b=((conv_ln_b, m_conv_ln_b, v_conv_ln_b), (5 * SUBLANES, 0)),
               b_dw=((b_dw, m_b_dw, v_b_dw), (5 * SUBLANES, conv_w)))
    meta_row0 = 6 * SUBLANES
    wdw_row0 = meta_row0 + N_META
    loss_row0 = wdw_row0 + CONV_TAPS_PADDED
    (small_started,), token = _gather_start([[_fill_slot(packed, slot_idx, F32, slot_idx, "fill_small")]], dh0,
                                            "gather_small_start")
    done = update(pending_mix, token)
    done = update(pending_in, done)
    send, recv, lands = small_started
    fsend, frecv, lands = _gather_forward(lands, recv, done, "gather_small_forward")
    (packed_all,) = _gather_finish(lands, send, recv, fsend, frecv, done, "gather_small_finish")
    rep_names = list(rep)
    wdw_wmv = [jnp.pad(a[0], pad_taps) for a in (w_dw, m_w_dw, v_w_dw)]
    rep_out, meta_out, wdw_out, loss_blk = _small_update(
        slot_idx, packed_all, [rep[k][0] for k in rep_names], [rep[k][1] for k in rep_names],
        (meta, m_meta, v_meta), meta_row0, wdw_wmv, wdw_row0, loss_row0)
    small_out = dict(zip(rep_names, rep_out))
    small_out["meta"] = meta_out
    small_out["w_dw"] = [a[:CONV_KERNEL][None] for a in wdw_out]

    order = ["meta", "g_pre_mix", "w_in", "w_pool_grp", "pool_scale", "w_pool_out", "w_dw", "b_dw", "conv_ln_g",
             "conv_ln_b", "w_conv_out", "w_o", "g_post_mix", "g_pre_mlp", "w_up", "w_down", "g_post_mlp"]
    by_name = {**big_out, **small_out}
    result = [loss_blk[0, 0], grad_x]
    for kind in range(4):
        result += [by_name[k][kind] for k in order]
    return tuple(result)
```

```python
import functools

import jax
import jax.numpy as jnp
from jax import lax
from jax.experimental import pallas as pl
from jax.experimental.pallas import tpu as pltpu

F32 = jnp.float32
BF16 = jnp.bfloat16
MESH = pl.DeviceIdType.MESH

N_DEV = 8
N_META = 16
POOL_WINDOWS = (2, 4, 8, 16)
CONV_KERNEL = 31
CONV_TAPS_PADDED = 32
RMS_EPS = 1e-6
LN_EPS = 1e-5
ADAM_LR = 0.001
ADAM_B1 = 0.9
ADAM_B2 = 0.999
ADAM_EPS = 1e-08
ADAM_WD = 0.01
ADAM_STEP = 10

LANES = 128
SUBLANES = 8
ROW_CHUNK = 128
HALO = 32
VMEM_LIMIT_BYTES = 56 * 1024 * 1024


def _cparams(*sem):
    return pltpu.CompilerParams(dimension_semantics=sem if sem else None, vmem_limit_bytes=VMEM_LIMIT_BYTES)


def _round_up(n, m):
    return (n + m - 1) // m * m


def _row_tile(rows, cols, max_elems=640 * 1024):
    best = None
    for t in range(16, rows + 1, 16):
        if rows % t == 0 and (best is None or t * cols <= max_elems):
            best = t
    assert best is not None, (rows, cols)
    return best


def _rowsum8(a):
    t, w = a.shape
    return a.reshape(t // SUBLANES, SUBLANES, w).sum(axis=0)


def _mesh_pos():
    return lax.axis_index("x"), lax.axis_index("y"), lax.axis_index("c")


HBM_SPEC = pl.BlockSpec(memory_space=pltpu.HBM)
SEM_SPEC = pl.BlockSpec(memory_space=pltpu.SEMAPHORE)
ANY_SPEC = pl.BlockSpec(memory_space=pl.ANY)
_DATAFLOW = pltpu.SideEffectType.DATAFLOW_SIDE_EFFECTING


def _hbm(a):
    return pltpu.with_memory_space_constraint(a, pltpu.HBM)


def _slot(p):
    return 4 * p[0] + 2 * p[1] + p[2]


def _fill_slot(w, slot_idx, dtype, after, name):
    rows, cols = w.shape
    tr = _row_tile(rows, cols) if rows % 16 == 0 else rows

    def body(idx_ref, w_ref, _, out_ref):
        out_ref[...] = w_ref[...].astype(dtype)

    return pl.pallas_call(
        body, name=name,
        grid_spec=pltpu.PrefetchScalarGridSpec(
            num_scalar_prefetch=1, grid=(rows // tr,),
            in_specs=[pl.BlockSpec((tr, cols), lambda i, idx_ref: (i, 0)), ANY_SPEC],
            out_specs=pl.BlockSpec((None, tr, cols), lambda i, idx_ref: (idx_ref[0], i, 0))),
        out_shape=jax.ShapeDtypeStruct((N_DEV, rows, cols), dtype),
        compiler_params=_cparams("parallel"),
    )(slot_idx, w, after)


def _gather_start(groups, after, name):
    flat = [g for grp in groups for g in grp]
    n, n_grp = len(flat), len(groups)

    def body(*refs):
        lands = refs[:n]
        sems = refs[n + 1:n + 1 + 2 * n_grp]
        token = refs[-1]
        x, y, c = _mesh_pos()
        targets = [(x, y, 1 - c), (1 - x, y, c), (x, 1 - y, c), (1 - x, 1 - y, c)]
        t = 0
        for gi, grp in enumerate(groups):
            for ti in range(len(grp)):
                mine = lands[t].at[_slot((x, y, c))]
                for k, to in enumerate(targets):
                    pltpu.make_async_remote_copy(
                        src_ref=mine, dst_ref=mine,
                        send_sem=sems[2 * gi].at[4 * ti + k], recv_sem=sems[2 * gi + 1].at[4 * ti + k],
                        device_id=to, device_id_type=MESH).start()
                t += 1
        token[...] = jnp.zeros_like(token)

    sem_shapes = []
    for grp in groups:
        sem_shapes += [pltpu.SemaphoreType.DMA((4 * len(grp),))] * 2
    outs = pl.pallas_call(
        body, name=name,
        out_shape=tuple(sem_shapes + [pltpu.HBM(g.shape, g.dtype) for g in flat]
                        + [jax.ShapeDtypeStruct((SUBLANES, LANES), F32)]),
        in_specs=tuple([HBM_SPEC] * n + [ANY_SPEC]),
        out_specs=tuple([SEM_SPEC] * (2 * n_grp) + [HBM_SPEC] * n + [pl.BlockSpec(memory_space=pltpu.VMEM)]),
        input_output_aliases={i: 2 * n_grp + i for i in range(n)},
        compiler_params=pltpu.CompilerParams(has_side_effects=_DATAFLOW),
    )(*[_hbm(g) for g in flat], after)
    sems, lands, token = outs[:2 * n_grp], outs[2 * n_grp:-1], outs[-1]
    res, t = [], 0
    for gi, grp in enumerate(groups):
        res.append((sems[2 * gi], sems[2 * gi + 1], list(lands[t:t + len(grp)])))
        t += len(grp)
    return res, token


NEAR, FAR, ALL_CHIPS = (0, 1), (2,), (0, 1, 2)


def _gather_forward(lands, recv_sems, after, name, which=ALL_CHIPS):
    n, nw = len(lands), len(which)

    def body(*refs):
        land_refs, recv, _ = refs[:n], refs[n], refs[n + 1]
        fsend, frecv = refs[n + 2], refs[n + 3]
        x, y, c = _mesh_pos()
        chips = [(1 - x, y), (x, 1 - y), (1 - x, 1 - y)]
        for t in range(n):
            for i, j in enumerate(which):
                blk = land_refs[t].at[_slot((*chips[j], c))]
                pltpu.make_async_remote_copy(src_ref=blk, dst_ref=blk, send_sem=fsend.at[nw * t + i],
                                             recv_sem=recv.at[4 * t + 1 + j],
                                             device_id=(x, y, 1 - c), device_id_type=MESH).wait_recv()
                pltpu.make_async_remote_copy(src_ref=blk, dst_ref=blk, send_sem=fsend.at[nw * t + i],
                                             recv_sem=frecv.at[nw * t + i],
                                             device_id=(x, y, 1 - c), device_id_type=MESH).start()

    outs = pl.pallas_call(
        body, name=name,
        out_shape=tuple([pltpu.SemaphoreType.DMA((nw * n,))] * 2 + [pltpu.HBM(g.shape, g.dtype) for g in lands]),
        in_specs=tuple([HBM_SPEC] * n + [SEM_SPEC, ANY_SPEC]),
        out_specs=tuple([SEM_SPEC] * 2 + [HBM_SPEC] * n),
        input_output_aliases={i: 2 + i for i in range(n)},
        compiler_params=pltpu.CompilerParams(has_side_effects=_DATAFLOW),
    )(*lands, recv_sems, after)
    return outs[0], outs[1], list(outs[2:])


def _gather_finish(lands, send_sems, recv_sems, arrivals, sends, after, name, own=True, direct_sends=True):
    n = len(lands)
    fwd = list(arrivals) + list(sends)
    sem_args = [send_sems, recv_sems]
    where = []
    for _, fs, fr in fwd:
        pos = []
        for arr in (fs, fr):
            hit = [i for i, have in enumerate(sem_args) if have is arr]
            if not hit:
                sem_args.append(arr)
                hit = [len(sem_args) - 1]
            pos.append(hit[0])
        where.append(pos)

    def body(*refs):
        land_refs = refs[:n]
        send, recv = refs[n], refs[n + 1]
        fwd_refs = [refs[n + p] for pos in where for p in pos]
        x, y, c = _mesh_pos()
        sibling = (x, y, 1 - c)
        chips = [(1 - x, y), (x, 1 - y), (1 - x, 1 - y)]

        def desc(ref, s_sem, r_sem):
            return pltpu.make_async_remote_copy(src_ref=ref, dst_ref=ref, send_sem=s_sem, recv_sem=r_sem,
                                                device_id=sibling, device_id_type=MESH)

        for t in range(n):
            mine = land_refs[t].at[_slot((x, y, c))]
            if own:
                desc(land_refs[t].at[_slot(sibling)], send.at[4 * t], recv.at[4 * t]).wait_recv()
            for a, (which, _, _) in enumerate(fwd):
                fs, fr = fwd_refs[2 * a], fwd_refs[2 * a + 1]
                for i, j in enumerate(which):
                    if a < len(arrivals):
                        desc(land_refs[t].at[_slot((*chips[j], 1 - c))], fs.at[len(which) * t + i],
                             fr.at[len(which) * t + i]).wait_recv()
                    else:
                        desc(land_refs[t].at[_slot((*chips[j], c))], fs.at[len(which) * t + i],
                             fr.at[len(which) * t + i]).wait_send()
            if direct_sends:
                for k in range(4):
                    desc(mine, send.at[4 * t + k], recv.at[4 * t + k]).wait_send()

    outs = pl.pallas_call(
        body, name=name,
        out_shape=tuple(pltpu.HBM(g.shape, g.dtype) for g in lands),
        in_specs=tuple([HBM_SPEC] * n + [SEM_SPEC] * len(sem_args) + [ANY_SPEC]),
        out_specs=tuple([HBM_SPEC] * n),
        input_output_aliases={i: i for i in range(n)},
        compiler_params=pltpu.CompilerParams(has_side_effects=_DATAFLOW),
    )(*lands, *sem_args, after)
    return list(outs)


def _pair_exchange_start(grads, after, name):
    n = len(grads)

    def body(*refs):
        ins, lands = refs[:n], refs[n:2 * n]
        send, recv = refs[2 * n + 1], refs[2 * n + 2]
        token = refs[-1]
        x, y, c = _mesh_pos()
        for t in range(n):
            for q in range(4):
                pltpu.make_async_remote_copy(
                    src_ref=ins[t].at[2 * q + 1 - c], dst_ref=lands[t].at[q],
                    send_sem=send.at[4 * t + q], recv_sem=recv.at[4 * t + q],
                    device_id=(x, y, 1 - c), device_id_type=MESH).start()
        token[...] = jnp.zeros_like(token)

    land_shapes = [(4,) + g.shape[1:] for g in grads]
    outs = pl.pallas_call(
        body, name=name,
        out_shape=tuple([pltpu.SemaphoreType.DMA((4 * n,))] * 2 + [pltpu.HBM(g.shape, g.dtype) for g in grads]
                        + [pltpu.HBM(ls, g.dtype) for ls, g in zip(land_shapes, grads)]
                        + [jax.ShapeDtypeStruct((SUBLANES, LANES), F32)]),
        in_specs=tuple([HBM_SPEC] * (2 * n) + [ANY_SPEC]),
        out_specs=tuple([SEM_SPEC] * 2 + [HBM_SPEC] * (2 * n) + [pl.BlockSpec(memory_space=pltpu.VMEM)]),
        input_output_aliases={i: 2 + i for i in range(2 * n)},
        compiler_params=pltpu.CompilerParams(has_side_effects=_DATAFLOW),
    )(*[_hbm(g) for g in grads], *[_hbm(lax.empty(ls, g.dtype)) for ls, g in zip(land_shapes, grads)], after)
    return outs[0], outs[1], list(outs[2:2 + n]), list(outs[2 + n:2 + 2 * n]), outs[-1]


def _pair_exchange_finish(grads, lands, send_sems, recv_sems, after, name):
    n = len(grads)

    def body(*refs):
        ins, land_refs = refs[:n], refs[n:2 * n]
        send, recv = refs[2 * n], refs[2 * n + 1]
        x, y, c = _mesh_pos()
        for t in range(n):
            for q in range(4):
                cp = pltpu.make_async_remote_copy(
                    src_ref=ins[t].at[q], dst_ref=land_refs[t].at[q], send_sem=send.at[4 * t + q],
                    recv_sem=recv.at[4 * t + q], device_id=(x, y, 1 - c), device_id_type=MESH)
                cp.wait_send()
                cp.wait_recv()

    outs = pl.pallas_call(
        body, name=name,
        out_shape=tuple([pltpu.HBM(g.shape, g.dtype) for g in grads] + [pltpu.HBM(g.shape, g.dtype) for g in lands]),
        in_specs=tuple([HBM_SPEC] * (2 * n) + [SEM_SPEC] * 2 + [ANY_SPEC]),
        out_specs=tuple([HBM_SPEC] * (2 * n)),
        input_output_aliases={i: i for i in range(2 * n)},
        compiler_params=pltpu.CompilerParams(has_side_effects=_DATAFLOW),
    )(*grads, *lands, send_sems, recv_sems, after)
    return list(outs[:n]), list(outs[n:])


def _chip_exchange_start(sums, after, name):
    n = len(sums)

    def body(*refs):
        ins, lands = refs[:n], refs[n:2 * n]
        send, recv = refs[2 * n + 1], refs[2 * n + 2]
        token = refs[-1]
        x, y, c = _mesh_pos()
        chips = [(1 - x, y), (x, 1 - y), (1 - x, 1 - y)]
        for t in range(n):
            for r, chip in enumerate(chips):
                pltpu.make_async_remote_copy(
                    src_ref=ins[t].at[2 * chip[0] + chip[1]], dst_ref=lands[t].at[r],
                    send_sem=send.at[3 * t + r], recv_sem=recv.at[3 * t + r],
                    device_id=(*chip, c), device_id_type=MESH).start()
        token[...] = jnp.zeros_like(token)

    land_shapes = [(3,) + s.shape[1:] for s in sums]
    outs = pl.pallas_call(
        body, name=name,
        out_shape=tuple([pltpu.SemaphoreType.DMA((3 * n,))] * 2 + [pltpu.HBM(s.shape, s.dtype) for s in sums]
                        + [pltpu.HBM(ls, s.dtype) for ls, s in zip(land_shapes, sums)]
                        + [jax.ShapeDtypeStruct((SUBLANES, LANES), F32)]),
        in_specs=tuple([HBM_SPEC] * (2 * n) + [ANY_SPEC]),
        out_specs=tuple([SEM_SPEC] * 2 + [HBM_SPEC] * (2 * n) + [pl.BlockSpec(memory_space=pltpu.VMEM)]),
        input_output_aliases={i: 2 + i for i in range(2 * n)},
        compiler_params=pltpu.CompilerParams(has_side_effects=_DATAFLOW),
    )(*[_hbm(s) for s in sums], *[_hbm(lax.empty(ls, s.dtype)) for ls, s in zip(land_shapes, sums)], after)
    return outs[0], outs[1], list(outs[2:2 + n]), list(outs[2 + n:2 + 2 * n]), outs[-1]


def _chip_exchange_finish(sums, lands, send_sems, recv_sems, after, name):
    n = len(sums)

    def body(*refs):
        ins, land_refs = refs[:n], refs[n:2 * n]
        send, recv = refs[2 * n], refs[2 * n + 1]
        x, y, c = _mesh_pos()
        for t in range(n):
            for r in range(3):
                cp = pltpu.make_async_remote_copy(
                    src_ref=ins[t].at[r], dst_ref=land_refs[t].at[r], send_sem=send.at[3 * t + r],
                    recv_sem=recv.at[3 * t + r],
                    device_id=(x, y, 1 - c), device_id_type=MESH)
                cp.wait_send()
                cp.wait_recv()

    outs = pl.pallas_call(
        body, name=name,
        out_shape=tuple(pltpu.HBM(g.shape, g.dtype) for g in lands),
        in_specs=tuple([HBM_SPEC] * (2 * n) + [SEM_SPEC] * 2 + [ANY_SPEC]),
        out_specs=tuple([HBM_SPEC] * n),
        input_output_aliases={n + i: i for i in range(n)},
        compiler_params=pltpu.CompilerParams(has_side_effects=_DATAFLOW),
    )(*sums, *lands, send_sems, recv_sems, after)
    return list(outs)


def _mm(a, b, *, mode, tm, tn, tk=None, b_blocked=False, out_blocked=False, out_dtypes=(F32,),
        epilogue=None, extras=(), after=None, kb=1, out_places=None, name):
    if mode == "nn":
        m, k = a.shape
        n = b.shape[0] * b.shape[2] if b_blocked else b.shape[1]
        dims = (((1,), (0,)), ((), ()))
    elif mode == "nt":
        m, k = a.shape
        n = b.shape[1] if b_blocked else b.shape[0]
        if b_blocked:
            tk = kb * b.shape[2]
        dims = (((1,), (1,)), ((), ()))
    else:
        k, m = a.shape
        n = b.shape[1]
        dims = (((0,), (0,)), ((), ()))
    tk = k if tk is None else tk
    assert m % tm == 0 and n % tn == 0 and k % tk == 0, (name, m, n, k, tm, tn, tk)
    gm, gn, gk = m // tm, n // tn, k // tk
    if b_blocked:
        assert (tn if mode == "nn" else tk) == kb * b.shape[2], name

    if mode == "nn":
        a_spec = pl.BlockSpec((tm, tk), lambda i, j, kk: (i, kk))
        b_spec = (pl.BlockSpec((None, tk, tn), lambda i, j, kk: (j, kk, 0)) if b_blocked
                  else pl.BlockSpec((tk, tn), lambda i, j, kk: (kk, j)))
    elif mode == "nt":
        a_spec = pl.BlockSpec((tm, tk), lambda i, j, kk: (i, kk))
        b_spec = (pl.BlockSpec((kb, tn, tk // kb), lambda i, j, kk: (kk, j, 0)) if b_blocked
                  else pl.BlockSpec((tn, tk), lambda i, j, kk: (j, kk)))
    else:
        a_spec = pl.BlockSpec((tk, tm), lambda i, j, kk: (kk, i))
        b_spec = pl.BlockSpec((tk, tn), lambda i, j, kk: (kk, j))
    if out_blocked:
        out_spec = pl.BlockSpec((None, tm, tn), lambda i, j, kk: (j, i, 0))
        out_shape = (gn, m, tn)
    else:
        out_spec = pl.BlockSpec((tm, tn), lambda i, j, kk: (i, j))
        out_shape = (m, n)
    extra_specs = [pl.BlockSpec((tm, tn), functools.partial(lambda i, j, kk, off: (i, j + off), off=off))
                   for _, off in extras]
    n_extra, n_out = len(extras), len(out_dtypes)
    n_after = 0 if after is None else 1
    places = out_places if out_places is not None else (None,) * n_out

    def body(a_ref, b_ref, *rest):
        extra_refs = rest[:n_extra]
        out_refs = rest[n_extra + n_after:n_extra + n_after + n_out]

        def finish(acc):
            if epilogue is None:
                res = (acc,)
            else:
                res = epilogue(acc, *[e[...] for e in extra_refs])
            for o_ref, r in zip(out_refs, res):
                o_ref[...] = r.astype(o_ref.dtype)

        if mode == "nt" and b_blocked:
            bk = tk // kb
            part = lax.dot_general(a_ref[:, :bk], b_ref[0], dims, preferred_element_type=F32)
            for h in range(1, kb):
                part = part + lax.dot_general(a_ref[:, h * bk:(h + 1) * bk], b_ref[h], dims,
                                              preferred_element_type=F32)
        else:
            part = lax.dot_general(a_ref[...], b_ref[...], dims, preferred_element_type=F32)
        if gk == 1:
            finish(part)
        else:
            acc_ref = rest[-1]
            kk = pl.program_id(2)

            @pl.when(kk == 0)
            def _():
                acc_ref[...] = part

            @pl.when(kk > 0)
            def _():
                acc_ref[...] += part

            @pl.when(kk == gk - 1)
            def _():
                finish(acc_ref[...])

    outs = pl.pallas_call(
        body, name=name, grid=(gm, gn, gk),
        in_specs=[a_spec, b_spec] + extra_specs + [ANY_SPEC] * n_after,
        out_specs=[out_spec if place is None else
                   pl.BlockSpec((tm, tn), functools.partial(lambda i, j, kk, off: (i, j + off), off=place[1] // tn))
                   for place in places],
        out_shape=[jax.ShapeDtypeStruct(out_shape if place is None else (m, place[0]), dt)
                   for dt, place in zip(out_dtypes, places)],
        scratch_shapes=[pltpu.VMEM((tm, tn), F32)] if gk > 1 else [],
        compiler_params=_cparams("parallel", "parallel", "arbitrary"),
    )(a, b, *[e for e, _ in extras], *([] if after is None else [after]))
    return outs[0] if n_out == 1 else outs


def _mm_slots(a, w, slots, *, over, tm, tn=None, out_dtype=F32, epilogue=None, base=None, name):
    m = a.shape[0]
    ns = slots.shape[0]
    n_slots, w1, w2 = w.shape
    assert m % tm == 0
    if over == "n":
        k, bn = w1, w2

        def body(slots_ref, a_ref, w_ref, *rest):
            out_ref = rest[-1]
            acc = jnp.dot(a_ref[...], w_ref[...], preferred_element_type=F32)
            out_ref[...] = (acc if epilogue is None else epilogue(acc)).astype(out_ref.dtype)

        in_specs = [pl.BlockSpec((tm, k), lambda i, j, s: (i, 0)),
                    pl.BlockSpec((None, k, bn), lambda i, j, s: (s[j], 0, 0))]
        args = [a, w]
        aliases = {}
        if base is not None:
            in_specs.append(ANY_SPEC)
            args.append(base)
            aliases = {3: 0}
        return pl.pallas_call(
            body, name=name,
            grid_spec=pltpu.PrefetchScalarGridSpec(
                num_scalar_prefetch=1, grid=(m // tm, ns), in_specs=in_specs,
                out_specs=pl.BlockSpec((tm, bn), lambda i, j, s: (i, s[j]))),
            out_shape=jax.ShapeDtypeStruct((m, n_slots * bn), out_dtype),
            input_output_aliases=aliases,
            compiler_params=_cparams("parallel", "arbitrary"),
        )(slots, *args)

    bk, n = w1, w2
    tn = n if tn is None else tn
    assert n % tn == 0

    def body(slots_ref, a_ref, w_ref, *rest):
        out_ref, acc_ref = rest[-2], rest[-1]
        kk = pl.program_id(2)
        part = jnp.dot(a_ref[...], w_ref[...], preferred_element_type=F32)

        @pl.when(kk == 0)
        def _():
            acc_ref[...] = part if base is None else part + rest[0][...]

        @pl.when(kk > 0)
        def _():
            acc_ref[...] += part

        @pl.when(kk == ns - 1)
        def _():
            out_ref[...] = acc_ref[...].astype(out_ref.dtype)

    in_specs = [pl.BlockSpec((tm, bk), lambda i, j, kk, s: (i, s[kk])),
                pl.BlockSpec((None, bk, tn), lambda i, j, kk, s: (s[kk], 0, j))]
    args = [a, w]
    if base is not None:
        in_specs.append(pl.BlockSpec((tm, tn), lambda i, j, kk, s: (i, j)))
        args.append(base)
    return pl.pallas_call(
        body, name=name,
        grid_spec=pltpu.PrefetchScalarGridSpec(
            num_scalar_prefetch=1, grid=(m // tm, n // tn, ns), in_specs=in_specs,
            out_specs=pl.BlockSpec((tm, tn), lambda i, j, kk, s: (i, j)),
            scratch_shapes=[pltpu.VMEM((tm, tn), F32)]),
        out_shape=jax.ShapeDtypeStruct((m, n), out_dtype),
        compiler_params=_cparams("parallel", "parallel", "arbitrary"),
    )(slots, *args)


def _gate_mix(ya_pre, s, wpo, wco, proj, d_model, name):
    lp, width = ya_pre.shape
    nb, _, bn = wpo.shape
    ga_off = (proj.shape[1] - 2 * d_model) // bn
    gb_off = (proj.shape[1] - d_model) // bn

    def body(ya_ref, s_ref, wpo_ref, wco_ref, ga_ref, gb_ref, m_ref, y_a_ref, y_b_ref):
        y_a = jnp.dot(ya_ref[...], wpo_ref[...], preferred_element_type=F32)
        y_b = jnp.dot(s_ref[...], wco_ref[...], preferred_element_type=F32)
        m = jax.nn.sigmoid(ga_ref[...]) * y_a + jax.nn.sigmoid(gb_ref[...]) * y_b
        m_ref[...] = m.astype(BF16)
        y_a_ref[...] = y_a.astype(BF16)
        y_b_ref[...] = y_b.astype(BF16)

    act_spec = pl.BlockSpec((lp, width), lambda j: (0, 0))
    w_spec = pl.BlockSpec((None, width, bn), lambda j: (j, 0, 0))
    out_spec = pl.BlockSpec((lp, bn), lambda j: (0, j))
    return pl.pallas_call(
        body, name=name, grid=(nb,),
        in_specs=[act_spec, act_spec, w_spec, w_spec,
                  pl.BlockSpec((lp, bn), lambda j: (0, j + ga_off)),
                  pl.BlockSpec((lp, bn), lambda j: (0, j + gb_off))],
        out_specs=[out_spec] * 3,
        out_shape=[jax.ShapeDtypeStruct((lp, nb * bn), BF16)] * 3,
        compiler_params=_cparams("parallel"),
    )(ya_pre, s, wpo, wco, proj, proj)


def _rms_stats(x):
    return lax.rsqrt(jnp.mean(x * x, axis=-1, keepdims=True) + RMS_EPS)


def _rms_bwd(x, g, dy):
    r = _rms_stats(x)
    nrm = x * r
    dn = dy * g
    dx = r * (dn - nrm * jnp.mean(dn * nrm, axis=-1, keepdims=True))
    return dx, dy * nrm


def _rowwise(body, ins, outs, accs, *, lp, name):
    tr = _row_tile(lp, max(a.shape[1] for a in ins))
    n_in, n_out, n_acc = len(ins), len(outs), len(accs)

    def kernel_body(*refs):
        i = pl.program_id(0)
        acc_refs = refs[n_in + n_out:]

        @pl.when(i == 0)
        def _():
            for r in acc_refs:
                r[...] = jnp.zeros_like(r)

        body(i * tr, refs[:n_in], refs[n_in:n_in + n_out], acc_refs)

    in_specs = []
    for a in ins:
        if a.shape[0] == lp:
            in_specs.append(pl.BlockSpec((tr, a.shape[1]), lambda i: (i, 0)))
        else:
            in_specs.append(pl.BlockSpec(a.shape, lambda i: (0, 0)))
    out_specs = [pl.BlockSpec((tr, w), lambda i: (i, 0)) for w, _ in outs]
    out_specs += [pl.BlockSpec((SUBLANES, w), lambda i: (0, 0)) for w in accs]
    out_shape = [jax.ShapeDtypeStruct((lp, w), d) for w, d in outs]
    out_shape += [jax.ShapeDtypeStruct((SUBLANES, w), F32) for w in accs]
    return pl.pallas_call(
        kernel_body, name=name, grid=(lp // tr,), in_specs=in_specs, out_specs=out_specs,
        out_shape=out_shape, compiler_params=_cparams("arbitrary"),
    )(*ins)


def _rms_pre(h0, g, lp):
    d = h0.shape[1]

    def body(row0, ins, outs, accs):
        h_ref, g_ref = ins
        x = h_ref[...]
        outs[0][...] = (x * _rms_stats(x) * g_ref[...]).astype(BF16)

    return _rowwise(body, [h0, g], [(d, BF16)], [], lp=lp, name="rms_pre")[0]


def _post_mix(o, h0, g_post_mix, g_pre_mlp, lp):
    d = h0.shape[1]

    def body(row0, ins, outs, accs):
        o_ref, h0_ref, g1_ref, g2_ref = ins
        o_v = o_ref[...]
        h1 = h0_ref[...] + o_v * _rms_stats(o_v) * g1_ref[...]
        outs[0][...] = h1
        outs[1][...] = (h1 * _rms_stats(h1) * g2_ref[...]).astype(BF16)

    return _rowwise(body, [o, h0, g_post_mix, g_pre_mlp], [(d, F32), (d, BF16)], [], lp=lp, name="post_mix")


def _loss_head(f, h1, target, g_post_mlp, lp, seq):
    d = f.shape[1]

    def body(row0, ins, outs, accs):
        f_ref, h1_ref, t_ref, g_ref = ins
        df_ref, dh_ref = outs
        dg_ref, loss_ref = accs
        f_v, g = f_ref[...], g_ref[...]
        r = _rms_stats(f_v)
        nrm = f_v * r
        rows = row0 + lax.broadcasted_iota(jnp.int32, (f_v.shape[0], 1), 0)
        valid = (rows >= N_META) & (rows < N_META + seq)
        err = jnp.where(valid, h1_ref[...] + nrm * g - t_ref[...], 0.0)
        loss_ref[...] += 0.5 * jnp.sum(jnp.mean(err * err, axis=-1, keepdims=True))
        dy = err * (1.0 / d)
        dn = dy * g
        df_ref[...] = (r * (dn - nrm * jnp.mean(dn * nrm, axis=-1, keepdims=True))).astype(BF16)
        dh_ref[...] = dy
        dg_ref[...] += _rowsum8(dy * nrm)

    return _rowwise(body, [f, h1, target, g_post_mlp], [(d, BF16), (d, F32)], [d, LANES], lp=lp, name="loss_head")


def _mid_bwd(du2, h1, dh, o, g_pre_mlp, g_post_mix, lp):
    d = h1.shape[1]

    def body(row0, ins, outs, accs):
        du2_ref, h1_ref, dh_ref, o_ref, g2_ref, g1_ref = ins
        dx2, dg2 = _rms_bwd(h1_ref[...], g2_ref[...], du2_ref[...])
        dh1 = dh_ref[...] + dx2
        do, dg1 = _rms_bwd(o_ref[...], g1_ref[...], dh1)
        outs[0][...] = dh1
        outs[1][...] = do.astype(BF16)
        accs[0][...] += _rowsum8(dg2)
        accs[1][...] += _rowsum8(dg1)

    return _rowwise(body, [du2, h1, dh, o, g_pre_mlp, g_post_mix], [(d, F32), (d, BF16)], [d, d], lp=lp,
                    name="mid_bwd")


def _pre_mix_bwd(du1, h0, dh1, g_pre_mix, lp):
    d = h0.shape[1]

    def body(row0, ins, outs, accs):
        du1_ref, h0_ref, dh1_ref, g_ref = ins
        dx, dg = _rms_bwd(h0_ref[...], g_ref[...], du1_ref[...])
        outs[0][...] = dh1_ref[...] + dx
        accs[0][...] += _rowsum8(dg)

    return _rowwise(body, [du1, h0, dh1, g_pre_mix], [(d, F32)], [d], lp=lp, name="pre_mix_bwd")


def _ln_stats(c):
    mu = jnp.mean(c, axis=-1, keepdims=True)
    var = jnp.mean(jnp.square(c - mu), axis=-1, keepdims=True)
    return mu, lax.rsqrt(var + LN_EPS)


def _ln_silu(c, ln_g, ln_b, lp):
    w = c.shape[1]

    def body(row0, ins, outs, accs):
        c_ref, g_ref, b_ref = ins
        c_v = c_ref[...]
        mu, rstd = _ln_stats(c_v)
        ln = (c_v - mu) * rstd * g_ref[...] + b_ref[...]
        outs[0][...] = (ln * jax.nn.sigmoid(ln)).astype(BF16)

    return _rowwise(body, [c, ln_g, ln_b], [(w, BF16)], [], lp=lp, name="ln_silu")[0]


def _ln_silu_bwd(c, ds, ln_g, ln_b, lp):
    w = c.shape[1]

    def body(row0, ins, outs, accs):
        c_ref, ds_ref, g_ref, b_ref = ins
        c_v, g = c_ref[...], g_ref[...]
        mu, rstd = _ln_stats(c_v)
        nrm = (c_v - mu) * rstd
        ln = nrm * g + b_ref[...]
        sig = jax.nn.sigmoid(ln)
        dln = ds_ref[...] * (sig * (1.0 + ln * (1.0 - sig)))
        dn = dln * g
        dc = rstd * (dn - jnp.mean(dn, axis=-1, keepdims=True) - nrm * jnp.mean(dn * nrm, axis=-1, keepdims=True))
        outs[0][...] = dc
        accs[0][...] += _rowsum8(dln * nrm)
        accs[1][...] += _rowsum8(dln)
        accs[2][...] += _rowsum8(dc)

    return _rowwise(body, [c, ds, ln_g, ln_b], [(w, F32)], [w, w, w], lp=lp, name="ln_silu_bwd")


def _chunk_with_history(ref, i, cols=slice(None)):
    t0 = pl.multiple_of(i * ROW_CHUNK, ROW_CHUNK)
    lo0 = pl.multiple_of(jnp.maximum(t0 - HALO, 0), SUBLANES)
    lo = jnp.where(i > 0, ref[pl.ds(lo0, HALO), cols], 0.0)
    return jnp.concatenate([lo, ref[pl.ds(t0, ROW_CHUNK), cols]], axis=0)


def _chunk_with_future(ref, i, n_chunks, cols=slice(None)):
    t0 = pl.multiple_of(i * ROW_CHUNK, ROW_CHUNK)
    hi0 = pl.multiple_of(jnp.minimum(t0 + ROW_CHUNK, (n_chunks - 1) * ROW_CHUNK), SUBLANES)
    hi = jnp.where(i < n_chunks - 1, ref[pl.ds(hi0, HALO), cols], 0.0)
    return jnp.concatenate([ref[pl.ds(t0, ROW_CHUNK), cols], hi], axis=0)


def _inv_count(t0, n_rows, window):
    pos = t0 + lax.broadcasted_iota(jnp.int32, (n_rows, 1), 0)
    return 1.0 / jnp.minimum(pos + 1, window).astype(F32)


def _pool_delta(z_hist, t0, window):
    s = z_hist
    sh = 1
    while sh < window:
        s = s + pltpu.roll(s, sh, 0)
        sh *= 2
    cur = z_hist[HALO:, :]
    return s[HALO:, :] * _inv_count(t0, ROW_CHUNK, window) - cur


def _pool_fwd(proj, wpg, pool_scale, lp):
    n_grp, gdim, _ = wpg.shape
    width = n_grp * gdim
    n_chunks = lp // ROW_CHUNK

    def body(z_ref, w_ref, sc_ref, out_ref):
        for g, window in enumerate(POOL_WINDOWS):
            cols = slice(g * gdim, (g + 1) * gdim)

            def chunk(i, carry, cols=cols, g=g, window=window):
                t0 = pl.multiple_of(i * ROW_CHUNK, ROW_CHUNK)
                d = _pool_delta(_chunk_with_history(z_ref, i, cols), t0, window)
                q = jnp.dot(d.astype(BF16), w_ref[g], preferred_element_type=F32)
                out_ref[pl.ds(t0, ROW_CHUNK), cols] = (q * sc_ref[:, cols]).astype(BF16)
                return carry

            lax.fori_loop(0, n_chunks, chunk, 0)

    return pl.pallas_call(
        body, name="pool_fwd", grid=(1,),
        in_specs=[pl.BlockSpec((lp, width), lambda i: (0, 0)),
                  pl.BlockSpec(wpg.shape, lambda i: (0, 0, 0)),
                  pl.BlockSpec(pool_scale.shape, lambda i: (0, 0))],
        out_specs=pl.BlockSpec((lp, width), lambda i: (0, 0)),
        out_shape=jax.ShapeDtypeStruct((lp, width), BF16),
        compiler_params=_cparams("arbitrary"),
    )(proj, wpg, pool_scale)


def _pool_bwd(proj, d_ya, wpg, pool_scale, dproj, lp):
    n_grp, gdim, _ = wpg.shape
    width = n_grp * gdim
    n_chunks = lp // ROW_CHUNK
    ext = ROW_CHUNK + HALO

    def body(z_ref, dya_ref, w_ref, sc_ref, _, dz_ref, dw_ref, dsc_ref):
        dw_ref[...] = jnp.zeros_like(dw_ref)
        dsc_ref[...] = jnp.zeros_like(dsc_ref)
        for g, window in enumerate(POOL_WINDOWS):
            cols = slice(g * gdim, (g + 1) * gdim)

            def chunk(i, carry, cols=cols, g=g, window=window):
                t0 = pl.multiple_of(i * ROW_CHUNK, ROW_CHUNK)
                w_g = w_ref[g]
                scale = sc_ref[:, cols]
                d = _pool_delta(_chunk_with_history(z_ref, i, cols), t0, window).astype(BF16)
                dya_ext = _chunk_with_future(dya_ref, i, n_chunks, cols)
                dya = dya_ext[:ROW_CHUNK, :]
                q = jnp.dot(d, w_g, preferred_element_type=F32)
                dsc_ref[:, cols] += _rowsum8(dya * q)
                e_ext = (dya_ext * scale).astype(BF16)
                dw_ref[g] += lax.dot_general(d, e_ext[:ROW_CHUNK, :], (((0,), (0,)), ((), ())),
                                             preferred_element_type=F32)
                dd_ext = lax.dot_general(e_ext, w_g, (((1,), (1,)), ((), ())), preferred_element_type=F32)
                s = dd_ext * _inv_count(t0, ext, window)
                sh = 1
                while sh < window:
                    s = s + pltpu.roll(s, ext - sh, 0)
                    sh *= 2
                dz_ref[pl.ds(t0, ROW_CHUNK), cols] = (s[:ROW_CHUNK, :] - dd_ext[:ROW_CHUNK, :]).astype(BF16)
                return carry

            lax.fori_loop(0, n_chunks, chunk, 0)

    blk = pl.BlockSpec((lp, width), lambda i: (0, 0))
    return pl.pallas_call(
        body, name="pool_bwd", grid=(1,),
        in_specs=[blk, blk, pl.BlockSpec(wpg.shape, lambda i: (0, 0, 0)),
                  pl.BlockSpec(pool_scale.shape, lambda i: (0, 0)), ANY_SPEC],
        out_specs=[blk, pl.BlockSpec(wpg.shape, lambda i: (0, 0, 0)),
                   pl.BlockSpec((SUBLANES, width), lambda i: (0, 0))],
        out_shape=[jax.ShapeDtypeStruct(dproj.shape, BF16), jax.ShapeDtypeStruct(wpg.shape, F32),
                   jax.ShapeDtypeStruct((SUBLANES, width), F32)],
        input_output_aliases={4: 0},
        compiler_params=_cparams("arbitrary"),
    )(proj, d_ya, wpg, pool_scale, dproj)


def _conv_fwd(proj, w_dw, b_dw, lp, width, v_col0):
    n_chunks = lp // ROW_CHUNK
    v_blk0, g_blk0 = v_col0 // LANES, (v_col0 + width) // LANES

    def body(v_ref, gc_ref, w_ref, b_ref, c_ref, a_pad):
        a_pad[pl.ds(0, HALO), :] = jnp.zeros((HALO, LANES), F32)
        a_pad[pl.ds(HALO, lp), :] = v_ref[...] * jax.nn.sigmoid(gc_ref[...])

        def chunk(i, carry):
            t0 = pl.multiple_of(i * ROW_CHUNK, ROW_CHUNK)
            hist = a_pad[pl.ds(t0, ROW_CHUNK + HALO), :]
            acc = jnp.zeros((ROW_CHUNK, LANES), F32)
            for k in range(CONV_KERNEL):
                acc = acc + w_ref[k:k + 1, :] * pltpu.roll(hist, CONV_KERNEL - 1 - k, 0)[HALO:, :]
            c_ref[pl.ds(t0, ROW_CHUNK), :] = acc + b_ref[...]
            return carry

        lax.fori_loop(0, n_chunks, chunk, 0)

    return pl.pallas_call(
        body, name="conv_fwd", grid=(width // LANES,),
        in_specs=[pl.BlockSpec((lp, LANES), lambda j: (0, j + v_blk0)),
                  pl.BlockSpec((lp, LANES), lambda j: (0, j + g_blk0)),
                  pl.BlockSpec((CONV_TAPS_PADDED, LANES), lambda j: (0, j)),
                  pl.BlockSpec((1, LANES), lambda j: (0, j))],
        out_specs=pl.BlockSpec((lp, LANES), lambda j: (0, j)),
        out_shape=jax.ShapeDtypeStruct((lp, width), F32),
        scratch_shapes=[pltpu.VMEM((lp + HALO, LANES), F32)],
        compiler_params=_cparams("parallel"),
    )(proj, proj, w_dw, b_dw)


def _conv_bwd(proj, dc, w_dw, dproj, lp, width, v_col0):
    n_chunks = lp // ROW_CHUNK
    ext = ROW_CHUNK + HALO
    v_blk0, g_blk0 = v_col0 // LANES, (v_col0 + width) // LANES

    def body(v_ref, gc_ref, dc_ref, w_ref, _, dv_ref, dgc_ref, dw_ref, a_pad, dc_pad, dw_acc):
        sig = jax.nn.sigmoid(gc_ref[...])
        a_pad[pl.ds(0, HALO), :] = jnp.zeros((HALO, LANES), F32)
        a_pad[pl.ds(HALO, lp), :] = v_ref[...] * sig
        dc_pad[pl.ds(0, lp), :] = dc_ref[...]
        dc_pad[pl.ds(lp, HALO), :] = jnp.zeros((HALO, LANES), F32)
        dw_acc[...] = jnp.zeros_like(dw_acc)

        def chunk(i, carry):
            t0 = pl.multiple_of(i * ROW_CHUNK, ROW_CHUNK)
            hist = a_pad[pl.ds(t0, ext), :]
            fut = dc_pad[pl.ds(t0, ext), :]
            dc_cur = fut[:ROW_CHUNK, :]
            da = jnp.zeros((ROW_CHUNK, LANES), F32)
            for k in range(CONV_KERNEL):
                lag = CONV_KERNEL - 1 - k
                da = da + w_ref[k:k + 1, :] * pltpu.roll(fut, (ext - lag) % ext, 0)[:ROW_CHUNK, :]
                dw_acc[pl.ds(SUBLANES * k, SUBLANES), :] += _rowsum8(dc_cur * pltpu.roll(hist, lag, 0)[HALO:, :])
            rows = pl.ds(t0, ROW_CHUNK)
            sg = jax.nn.sigmoid(gc_ref[rows, :])
            dv_ref[rows, :] = (da * sg).astype(BF16)
            dgc_ref[rows, :] = (da * v_ref[rows, :] * sg * (1.0 - sg)).astype(BF16)
            return carry

        lax.fori_loop(0, n_chunks, chunk, 0)
        dw_ref[...] = dw_acc[...].reshape(CONV_TAPS_PADDED, SUBLANES, LANES).sum(axis=1)

    col = lambda j: (0, j)
    return pl.pallas_call(
        body, name="conv_bwd", grid=(width // LANES,),
        in_specs=[pl.BlockSpec((lp, LANES), lambda j: (0, j + v_blk0)),
                  pl.BlockSpec((lp, LANES), lambda j: (0, j + g_blk0)),
                  pl.BlockSpec((lp, LANES), col),
                  pl.BlockSpec((CONV_TAPS_PADDED, LANES), col), ANY_SPEC],
        out_specs=[pl.BlockSpec((lp, LANES), lambda j: (0, j + v_blk0)), pl.BlockSpec((lp, LANES), col),
                   pl.BlockSpec((CONV_TAPS_PADDED, LANES), col)],
        out_shape=[jax.ShapeDtypeStruct(dproj.shape, BF16), jax.ShapeDtypeStruct((lp, width), BF16),
                   jax.ShapeDtypeStruct((CONV_TAPS_PADDED, width), F32)],
        scratch_shapes=[pltpu.VMEM((lp + HALO, LANES), F32), pltpu.VMEM((lp + HALO, LANES), F32),
                        pltpu.VMEM((CONV_TAPS_PADDED * SUBLANES, LANES), F32)],
        input_output_aliases={4: 0},
        compiler_params=_cparams("parallel"),
    )(proj, proj, dc, w_dw, dproj)


def _place_columns(dst, pieces, name):
    m = dst.shape[0]
    tile = 512
    counts = [p.shape[1] // tile for p, _ in pieces]
    starts = [sum(counts[:i]) for i in range(len(pieces))]
    n_steps = sum(counts)

    def local(s, i):
        return jnp.clip(s - starts[i], 0, counts[i] - 1)

    def out_index(s):
        blk = pieces[0][1] // tile + local(s, 0)
        for i in range(1, len(pieces)):
            blk = jnp.where(s >= starts[i], pieces[i][1] // tile + local(s, i), blk)
        return 0, blk

    def body(*refs):
        out_ref = refs[-1]
        s = pl.program_id(0)
        for i in range(len(pieces)):
            @pl.when((s >= starts[i]) & (s < starts[i] + counts[i]))
            def _(i=i):
                out_ref[...] = refs[i][...]

    return pl.pallas_call(
        body, name=name, grid=(n_steps,),
        in_specs=[pl.BlockSpec((m, tile), functools.partial(lambda s, i: (0, local(s, i)), i=i))
                  for i in range(len(pieces))] + [ANY_SPEC],
        out_specs=pl.BlockSpec((m, tile), out_index),
        out_shape=jax.ShapeDtypeStruct(dst.shape, dst.dtype),
        input_output_aliases={len(pieces): 0},
        compiler_params=_cparams("arbitrary"),
    )(*[p for p, _ in pieces], dst)


def _adamw_math(w, g, m, v):
    m = ADAM_B1 * m + (1.0 - ADAM_B1) * g
    v = ADAM_B2 * v + (1.0 - ADAM_B2) * jnp.square(g)
    m_hat = m / (1.0 - ADAM_B1 ** ADAM_STEP)
    v_hat = v / (1.0 - ADAM_B2 ** ADAM_STEP)
    delta = -ADAM_LR * (m_hat / (jnp.sqrt(v_hat) + ADAM_EPS) + ADAM_WD * w)
    return delta, m, v


def _pair_sum(own, recv, where, name):
    _, _, rows, cols = own.shape
    tr = _row_tile(rows, cols, 1024 * 1024)

    def body(where_ref, own_ref, recv_ref, out_ref):
        out_ref[...] = (own_ref[...].astype(F32) + recv_ref[...].astype(F32)).astype(BF16)

    return pl.pallas_call(
        body, name=name,
        grid_spec=pltpu.PrefetchScalarGridSpec(
            num_scalar_prefetch=1, grid=(3, rows // tr),
            in_specs=[pl.BlockSpec((None, None, tr, cols), lambda r, i, wh: (wh[2 + r], wh[0], i, 0)),
                      pl.BlockSpec((None, tr, cols), lambda r, i, wh: (wh[2 + r], i, 0))],
            out_specs=pl.BlockSpec((None, tr, cols), lambda r, i, wh: (wh[2 + r], i, 0))),
        out_shape=jax.ShapeDtypeStruct((4, rows, cols), BF16),
        compiler_params=_cparams("parallel", "parallel"),
    )(where, own, recv)


def _adamw_big(w, m, v, own, from_sibling, recv3, where, name):
    rows, cols = w.shape
    tr = _row_tile(rows, cols, 256 * 1024)

    def body(where_ref, w_ref, m_ref, v_ref, own_ref, sib_ref, r_ref, g_out, d_out, m_out, v_out):
        g = own_ref[...].astype(F32) + sib_ref[...].astype(F32)
        for r in range(3):
            g = g + r_ref[r].astype(F32)
        delta, m_new, v_new = _adamw_math(w_ref[...], g, m_ref[...], v_ref[...])
        g_out[...] = g
        d_out[...] = delta
        m_out[...] = m_new
        v_out[...] = v_new

    blk = pl.BlockSpec((tr, cols), lambda i, q_ref: (i, 0))
    return pl.pallas_call(
        body, name=name,
        grid_spec=pltpu.PrefetchScalarGridSpec(
            num_scalar_prefetch=1, grid=(rows // tr,),
            in_specs=[blk, blk, blk,
                      pl.BlockSpec((None, None, tr, cols), lambda i, wh: (wh[1], wh[0], i, 0)),
                      pl.BlockSpec((None, tr, cols), lambda i, wh: (wh[1], i, 0)),
                      pl.BlockSpec((3, tr, cols), lambda i, wh: (0, i, 0))],
            out_specs=[blk] * 4),
        out_shape=[jax.ShapeDtypeStruct((rows, cols), F32)] * 4,
        compiler_params=_cparams("parallel"),
    )(where, w, m, v, own, from_sibling, recv3)


def _small_update(me_idx, packed, rep_params, rep_places, meta_wmv, meta_row0, wdw_wmv, wdw_row0, loss_row0):
    n_rep = len(rep_params)
    meta_rows, meta_cols = meta_wmv[0].shape
    wdw_rows, wdw_cols = wdw_wmv[0].shape

    def body(me_ref, *refs):
        pos = 0

        def take(k):
            nonlocal pos
            out = refs[pos:pos + k]
            pos += k
            return out

        rep_in = [take(3) for _ in range(n_rep)]
        rep_g = take(n_rep)
        meta_in, (meta_g,) = take(3), take(1)
        wdw_in, (wdw_g,) = take(3), take(1)
        (loss_ref,) = take(1)
        rep_out = [take(4) for _ in range(n_rep)]
        meta_out, wdw_out = take(4), take(4)
        (loss_out,) = take(1)

        def update(wmv, g, outs):
            delta, m_new, v_new = _adamw_math(wmv[0][...], g, wmv[1][...], wmv[2][...])
            for o_ref, val in zip(outs, (g, delta, m_new, v_new)):
                o_ref[...] = val

        for wmv, g_ref, outs in zip(rep_in, rep_g, rep_out):
            g = jnp.sum(g_ref[0], axis=0, keepdims=True)
            for j in range(1, N_DEV):
                g = g + jnp.sum(g_ref[j], axis=0, keepdims=True)
            update(wmv, g, outs)
        for wmv, g_ref, outs in ((meta_in, meta_g, meta_out), (wdw_in, wdw_g, wdw_out)):
            g = g_ref[0]
            for j in range(1, N_DEV):
                g = g + g_ref[j]
            update(wmv, g, outs)
        total = loss_ref[0]
        for j in range(1, N_DEV):
            total = total + loss_ref[j]
        loss_out[...] = total

    def whole(a):
        nd = a.ndim
        return pl.BlockSpec(a.shape, lambda i, me_ref, nd=nd: (0,) * nd)

    ins, in_specs = [], []
    for wmv in rep_params:
        ins += list(wmv)
        in_specs += [whole(a) for a in wmv]
    for wmv, (row0, col0) in zip(rep_params, rep_places):
        width = wmv[0].shape[1]
        ins.append(packed)
        in_specs.append(pl.BlockSpec((N_DEV, SUBLANES, width),
                                     lambda i, me_ref, rb=row0 // SUBLANES, cb=col0 // width: (0, rb, cb)))
    ins += list(meta_wmv) + [packed]
    in_specs += [whole(a) for a in meta_wmv]
    in_specs.append(pl.BlockSpec((N_DEV, meta_rows, meta_cols),
                                 lambda i, me_ref, rb=meta_row0 // meta_rows: (0, rb, me_ref[0])))
    ins += list(wdw_wmv) + [packed]
    in_specs += [whole(a) for a in wdw_wmv]
    in_specs.append(pl.BlockSpec((N_DEV, wdw_rows, wdw_cols),
                                 lambda i, me_ref, rb=wdw_row0 // wdw_rows: (0, rb, me_ref[0])))
    ins.append(packed)
    in_specs.append(pl.BlockSpec((N_DEV, SUBLANES, LANES), lambda i, me_ref, rb=loss_row0 // SUBLANES: (0, rb, 0)))

    out_shape, out_specs = [], []
    for wmv in list(rep_params) + [meta_wmv, wdw_wmv]:
        out_shape += [jax.ShapeDtypeStruct(wmv[0].shape, F32)] * 4
        out_specs += [whole(wmv[0])] * 4
    out_shape.append(jax.ShapeDtypeStruct((SUBLANES, LANES), F32))
    out_specs.append(pl.BlockSpec((SUBLANES, LANES), lambda i, me_ref: (0, 0)))

    outs = pl.pallas_call(
        body, name="small_update",
        grid_spec=pltpu.PrefetchScalarGridSpec(num_scalar_prefetch=1, grid=(1,), in_specs=in_specs,
                                               out_specs=out_specs),
        out_shape=out_shape, compiler_params=_cparams("arbitrary"),
    )(me_idx, *ins)
    groups = [outs[4 * i:4 * i + 4] for i in range(n_rep + 2)]
    return groups[:n_rep], groups[n_rep], groups[n_rep + 1], outs[-1]


def kernel(x, meta, g_pre_mix, w_in, w_pool_grp, pool_scale, w_pool_out, w_dw, b_dw, conv_ln_g, conv_ln_b, w_conv_out, w_o, g_post_mix, g_pre_mlp, w_up, w_down, g_post_mlp, loss_target, m_meta, m_g_pre_mix, m_w_in, m_w_pool_grp, m_pool_scale, m_w_pool_out, m_w_dw, m_b_dw, m_conv_ln_g, m_conv_ln_b, m_w_conv_out, m_w_o, m_g_post_mix, m_g_pre_mlp, m_w_up, m_w_down, m_g_post_mlp, v_meta, v_g_pre_mix, v_w_in, v_w_pool_grp, v_pool_scale, v_w_pool_out, v_w_dw, v_b_dw, v_conv_ln_g, v_conv_ln_b, v_w_conv_out, v_w_o, v_g_post_mix, v_g_pre_mlp, v_w_up, v_w_down, v_g_post_mlp):
    seq, d = x.shape[1], x.shape[2]
    pool_w = pool_scale.shape[1]
    conv_w = b_dw.shape[1]
    n_grp, grp_rows, gdim = w_pool_grp.shape[1:]
    lp = _round_up(N_META + seq, ROW_CHUNK)
    tm_half = lp // 2 if (lp // 2) % 16 == 0 else lp
    c_idx = lax.axis_index("c").astype(jnp.int32)
    chip_idx = (2 * lax.axis_index("x") + lax.axis_index("y")).astype(jnp.int32)
    me_idx = 2 * chip_idx + c_idx

    pad_taps = ((0, CONV_TAPS_PADDED - CONV_KERNEL), (0, 0))
    big = dict(w_in=w_in[0], w_pool_grp=w_pool_grp[0].reshape(n_grp * grp_rows, gdim), w_pool_out=w_pool_out[0],
               w_conv_out=w_conv_out[0], w_o=w_o[0], w_up=w_up[0], w_down=w_down[0])
    big_names = list(big)
    moments = dict(w_in=(m_w_in, v_w_in), w_pool_grp=(m_w_pool_grp, v_w_pool_grp), w_pool_out=(m_w_pool_out, v_w_pool_out),
                   w_conv_out=(m_w_conv_out, v_w_conv_out), w_o=(m_w_o, v_w_o), w_up=(m_w_up, v_w_up),
                   w_down=(m_w_down, v_w_down))
    slot_idx = me_idx.reshape(1)
    sources = dict(big, meta=meta, w_dw=jnp.pad(w_dw[0], pad_taps))

    def fill(k, after):
        return _fill_slot(sources[k], slot_idx, BF16 if k in big else F32, after, "fill_" + k)

    gather_groups = [["meta", "w_dw"], ["w_in"], ["w_pool_grp", "w_pool_out", "w_conv_out", "w_o"], ["w_up"], ["w_down"]]
    started, token = _gather_start([[fill(k, slot_idx) for k in names] for names in gather_groups[:2]], slot_idx,
                                   "gather_start_first")
    started_rest, _ = _gather_start([[fill(k, token) for k in names] for names in gather_groups[2:]], token,
                                    "gather_start_rest")
    started += started_rest
    wg = {}
    x_idx, y_idx = lax.axis_index("x"), lax.axis_index("y")
    at = lambda px, py, pc: 4 * px + 2 * py + pc
    near_slots = jnp.stack([at(x_idx, y_idx, c_idx), at(x_idx, y_idx, 1 - c_idx), at(1 - x_idx, y_idx, c_idx),
                            at(x_idx, 1 - y_idx, c_idx), at(1 - x_idx, y_idx, 1 - c_idx),
                            at(x_idx, 1 - y_idx, 1 - c_idx)]).astype(jnp.int32)
    far_slots = jnp.stack([at(1 - x_idx, 1 - y_idx, c_idx), at(1 - x_idx, 1 - y_idx, 1 - c_idx)]).astype(jnp.int32)

    def gather_whole(gi, after_forward, after_finish):
        send, recv, lands = started[gi]
        fs, fr, lands = _gather_forward(lands, recv, after_forward(), f"gather_forward_{gi}")
        lands = _gather_finish(lands, send, recv, [(ALL_CHIPS, fs, fr)], [(ALL_CHIPS, fs, fr)], after_finish(),
                               f"gather_finish_{gi}")
        wg.update(zip(gather_groups[gi], lands))

    near_state = {}

    def gather_near(gi, after):
        send, recv, lands = started[gi]
        fs, fr, lands = _gather_forward(lands, recv, after, f"gather_forward_near_{gi}", which=NEAR)
        lands = _gather_finish(lands, send, recv, [(NEAR, fs, fr)], [], after, f"gather_finish_near_{gi}",
                               direct_sends=False)
        near_state[gi] = (fs, fr)
        return lands

    def gather_far(gi, lands, after):
        send, recv, _ = started[gi]
        fs, fr, lands = _gather_forward(lands, recv, after, f"gather_forward_far_{gi}", which=FAR)
        lands = _gather_finish(lands, send, recv, [(FAR, fs, fr)], [(NEAR,) + near_state[gi], (FAR, fs, fr)], after,
                               f"gather_finish_far_{gi}", own=False)
        wg.update(zip(gather_groups[gi], lands))
        return lands

    gather_whole(0, lambda: token, lambda: token)
    meta_full = wg["meta"].transpose(1, 0, 2).reshape(N_META, d)
    wdw_full = wg["w_dw"].transpose(1, 0, 2).reshape(CONV_TAPS_PADDED, conv_w)
    tail = lp - N_META - seq
    h0 = jnp.concatenate([meta_full, x[0], jnp.zeros((tail, d), F32)], axis=0)
    target = jnp.pad(loss_target[0], ((N_META, tail), (0, 0)))
    u1 = _rms_pre(h0, g_pre_mix, lp)
    (w_in_near,) = gather_near(1, u1)
    proj = _mm_slots(u1, w_in_near, near_slots, over="n", tm=tm_half, name="mm_proj_near")
    (w_in_all,) = gather_far(1, [w_in_near], proj)
    proj = _mm_slots(u1, w_in_all, far_slots, over="n", tm=tm_half, base=proj, name="mm_proj_far")
    conv_c = _conv_fwd(proj, wdw_full, b_dw, lp, conv_w, pool_w)
    s_act = _ln_silu(conv_c, conv_ln_g, conv_ln_b, lp)
    gather_whole(2, lambda: proj, lambda: s_act)
    wpg_full = wg["w_pool_grp"].reshape(N_DEV, n_grp, grp_rows, gdim).transpose(1, 0, 2, 3).reshape(n_grp, gdim, gdim)
    w_o_full = wg["w_o"].reshape(d, d)
    ya_pre = _pool_fwd(proj, wpg_full, pool_scale, lp)
    m_mix, y_a, y_b = _gate_mix(ya_pre, s_act, wg["w_pool_out"], wg["w_conv_out"], proj, d, "gate_mix")
    o = _mm(m_mix, w_o_full, mode="nn", tm=tm_half, tn=512, name="mm_o")
    h1, u2 = _post_mix(o, h0, g_post_mix, g_pre_mlp, lp)
    relu2 = lambda acc: jnp.square(jnp.maximum(acc, 0.0))
    (w_up_near,) = gather_near(3, u2)
    act = _mm_slots(u2, w_up_near, near_slots, over="n", tm=tm_half, out_dtype=BF16, epilogue=relu2, name="mm_up_near")
    (w_up_all,) = gather_far(3, [w_up_near], act)
    act = _mm_slots(u2, w_up_all, far_slots, over="n", tm=tm_half, out_dtype=BF16, epilogue=relu2, base=act,
                    name="mm_up_far")
    (w_down_near,) = gather_near(4, act)
    f = _mm_slots(act, w_down_near, near_slots, over="k", tm=tm_half, tn=1024, name="mm_down_near")
    (w_down_all,) = gather_far(4, [w_down_near], f)
    f = _mm_slots(act, w_down_all, far_slots, over="k", tm=tm_half, tn=1024, base=f, name="mm_down_far")
    w_down_full = w_down_all.reshape(-1, d)

    big_out = {}

    def to_sibling(names, grads, after, tag):
        send, recv, grads, lands, token = _pair_exchange_start(grads, after, "grads_to_sibling_start_" + tag)
        return (names, send, recv, grads, lands, tag), token

    where = jnp.stack([c_idx, chip_idx, 2 * (1 - x_idx) + y_idx, 2 * x_idx + (1 - y_idx),
                       2 * (1 - x_idx) + (1 - y_idx)]).astype(jnp.int32)

    def to_owner(handle, after):
        names, send, recv, grads, from_sib, tag = handle
        grads, from_sib = _pair_exchange_finish(grads, from_sib, send, recv, after, "grads_to_sibling_finish_" + tag)
        own = [g.reshape((4, 2) + g.shape[1:]) for g in grads]
        sums = [_pair_sum(o, r, where, "pair_sum_" + k) for k, o, r in zip(names, own, from_sib)]
        send, recv, sums, lands, token = _chip_exchange_start(sums, after, "grads_to_owner_start_" + tag)
        return (names, send, recv, sums, lands, own, from_sib, tag), token

    def update(handle, after):
        names, send, recv, sums, lands, own, from_sib, tag = handle
        got = _chip_exchange_finish(sums, lands, send, recv, after, "grads_to_owner_finish_" + tag)
        for k, o, s, r3 in zip(names, own, from_sib, got):
            w2 = big[k]
            shape = moments[k][0].shape
            outs = _adamw_big(w2, moments[k][0].reshape(w2.shape), moments[k][1].reshape(w2.shape), o, s, r3,
                              where, "adamw_" + k)
            big_out[k] = [a.reshape(shape) for a in outs]
        return big_out[names[-1]][0]

    df, dh, dg_post_mlp, loss_part = _loss_head(f, h1, target, g_post_mlp, lp, seq)
    d_up = _mm(df, w_down_full, mode="nt", tm=tm_half, tn=1024, out_dtypes=(BF16,), extras=[(act, 0)],
               epilogue=lambda acc, a: (acc * (2.0 * jnp.sqrt(a.astype(F32))),), name="mm_d_up")
    g_w_down = _mm(act, df, mode="tn", tm=1024, tn=1024, out_dtypes=(BF16,), name="mm_g_down")
    sib_down, token = to_sibling(["w_down"], [g_w_down.reshape(N_DEV, -1, d)], slot_idx, "down")
    g_w_up = _mm(u2, d_up, mode="tn", tm=1024, tn=wg["w_up"].shape[2], out_blocked=True, out_dtypes=(BF16,),
                 after=token, name="mm_g_up")
    sib_up, token = to_sibling(["w_up"], [g_w_up], slot_idx, "up")
    pending_down, token = to_owner(sib_down, token)
    du2 = _mm(d_up, wg["w_up"], mode="nt", tm=tm_half, tn=1024, b_blocked=True, kb=2, after=token, name="mm_du2")
    pending_up, token = to_owner(sib_up, du2)
    dh1, do, dg_pre_mlp, dg_post_mix = _mid_bwd(du2, h1, dh, o, g_pre_mlp, g_post_mix, lp)

    def gate_bwd(dm, ga, gb, ya, yb):
        sa, sb = jax.nn.sigmoid(ga), jax.nn.sigmoid(gb)
        return (dm * ya.astype(F32) * sa * (1.0 - sa), dm * yb.astype(F32) * sb * (1.0 - sb), dm * sa, dm * sb)

    gate_tn = 512
    ga_col0, gb_col0 = proj.shape[1] - 2 * d, proj.shape[1] - d
    dproj, d_gb, d_ya, d_yb = _mm(
        do, w_o_full, mode="nt", tm=tm_half, tn=gate_tn, out_dtypes=(BF16,) * 4,
        extras=[(proj, ga_col0 // gate_tn), (proj, gb_col0 // gate_tn), (y_a, 0), (y_b, 0)], epilogue=gate_bwd,
        out_places=((proj.shape[1], ga_col0), None, None, None), after=token, name="mm_dm")
    g_w_o = _mm(m_mix, do, mode="tn", tm=1024, tn=1024, out_dtypes=(BF16,), name="mm_g_o")
    bn_out = wg["w_pool_out"].shape[2]
    g_w_pool_out = _mm(ya_pre, d_ya, mode="tn", tm=pool_w, tn=bn_out, out_blocked=True, out_dtypes=(BF16,),
                       name="mm_g_pool_out")
    g_w_conv_out = _mm(s_act, d_yb, mode="tn", tm=conv_w, tn=bn_out, out_blocked=True, out_dtypes=(BF16,),
                       name="mm_g_conv_out")
    sib_mix, token = to_sibling(["w_o", "w_pool_out", "w_conv_out"],
                                [g_w_o.reshape(N_DEV, -1, d), g_w_pool_out, g_w_conv_out], slot_idx, "mix")
    d_ya_pre = _mm(d_ya, wg["w_pool_out"], mode="nt", tm=tm_half, tn=pool_w, b_blocked=True, after=token,
                   name="mm_d_ya_pre")
    d_s = _mm(d_yb, wg["w_conv_out"], mode="nt", tm=tm_half, tn=conv_w, b_blocked=True, name="mm_d_s")
    pending_mix, token = to_owner(sib_mix, d_s)
    dproj, g_wpg, d_scale = _pool_bwd(proj, d_ya_pre, wpg_full, pool_scale, dproj, lp)
    dc, d_ln_g, d_ln_b, d_b_dw = _ln_silu_bwd(conv_c, d_s, conv_ln_g, conv_ln_b, lp)
    dproj, dgc, g_wdw = _conv_bwd(proj, dc, wdw_full, dproj, lp, conv_w, pool_w)
    dproj = _place_columns(dproj, [(dgc, pool_w + conv_w), (d_gb, gb_col0)], "place_dproj")
    g_w_in = _mm(u1, dproj, mode="tn", tm=1024, tn=wg["w_in"].shape[2], out_blocked=True, out_dtypes=(BF16,),
                 after=token, name="mm_g_in")
    g_wpg_slots = g_wpg.astype(BF16).reshape(n_grp, N_DEV, grp_rows, gdim).transpose(1, 0, 2, 3)
    sib_in, token = to_sibling(["w_pool_grp", "w_in"],
                               [g_wpg_slots.reshape(N_DEV, n_grp * grp_rows, gdim), g_w_in], slot_idx, "in")
    done = update(pending_down, token)
    pending_in, token = to_owner(sib_in, done)
    done = update(pending_up, token)
    du1 = _mm(dproj, wg["w_in"], mode="nt", tm=tm_half, tn=1024, b_blocked=True, kb=2, after=done, name="mm_du1")
    dh0, dg_pre_mix = _pre_mix_bwd(du1, h0, dh1, g_pre_mix, lp)
    grad_x = dh0[N_META:N_META + seq][None]

    assert pool_w + conv_w == d and conv_w <= d and LANES <= d
    widen = lambda a: jnp.pad(a, ((0, 0), (0, d - a.shape[1])))
    packed = jnp.concatenate([
        dg_pre_mix, dg_post_mix, dg_pre_mlp, dg_post_mlp,
        jnp.concatenate([d_scale, d_ln_g], axis=1), jnp.concatenate([d_ln_b, d_b_dw], axis=1),
        dh0[:N_META], widen(g_wdw), widen(loss_part)], axis=0)
    rep = dict(g_pre_mix=((g_pre_mix, m_g_pre_mix, v_g_pre_mix), (0, 0)),
               g_post_mix=((g_post_mix, m_g_post_mix, v_g_post_mix), (SUBLANES, 0)),
               g_pre_mlp=((g_pre_mlp, m_g_pre_mlp, v_g_pre_mlp), (2 * SUBLANES, 0)),
               g_post_mlp=((g_post_mlp, m_g_post_mlp, v_g_post_mlp), (3 * SUBLANES, 0)),
               pool_scale=((pool_scale, m_pool_scale, v_pool_scale), (4 * SUBLANES, 0)),
               conv_ln_g=((conv_ln_g, m_conv_ln_g, v_conv_ln_g), (4 * SUBLANES, pool_w)),
               conv_ln_b=((conv_ln_b, m_conv_ln_b, v_conv_ln_b), (5 * SUBLANES, 0)),
               b_dw=((b_dw, m_b_dw, v_b_dw), (5 * SUBLANES, conv_w)))
    meta_row0 = 6 * SUBLANES
    wdw_row0 = meta_row0 + N_META
    loss_row0 = wdw_row0 + CONV_TAPS_PADDED
    (small_started,), token = _gather_start([[_fill_slot(packed, slot_idx, F32, slot_idx, "fill_small")]], dh0,
                                            "gather_small_start")
    done = update(pending_mix, token)
    done = update(pending_in, done)
    send, recv, lands = small_started
    fsend, frecv, lands = _gather_forward(lands, recv, done, "gather_small_forward")
    (packed_all,) = _gather_finish(lands, send, recv, [(ALL_CHIPS, fsend, frecv)], [(ALL_CHIPS, fsend, frecv)], done,
                                   "gather_small_finish")
    rep_names = list(rep)
    wdw_wmv = [jnp.pad(a[0], pad_taps) for a in (w_dw, m_w_dw, v_w_dw)]
    rep_out, meta_out, wdw_out, loss_blk = _small_update(
        slot_idx, packed_all, [rep[k][0] for k in rep_names], [rep[k][1] for k in rep_names],
        (meta, m_meta, v_meta), meta_row0, wdw_wmv, wdw_row0, loss_row0)
    small_out = dict(zip(rep_names, rep_out))
    small_out["meta"] = meta_out
    small_out["w_dw"] = [a[:CONV_KERNEL][None] for a in wdw_out]

    order = ["meta", "g_pre_mix", "w_in", "w_pool_grp", "pool_scale", "w_pool_out", "w_dw", "b_dw", "conv_ln_g",
             "conv_ln_b", "w_conv_out", "w_o", "g_post_mix", "g_pre_mlp", "w_up", "w_down", "g_post_mlp"]
    by_name = {**big_out, **small_out}
    result = [loss_blk[0, 0], grad_x]
    for kind in range(4):
        result += [by_name[k][kind] for k in order]
    return tuple(result)
```

```python
import functools

import jax
import jax.numpy as jnp
from jax import lax
from jax.experimental import pallas as pl
from jax.experimental.pallas import tpu as pltpu

F32 = jnp.float32
BF16 = jnp.bfloat16
MESH = pl.DeviceIdType.MESH

N_DEV = 8
N_META = 16
POOL_WINDOWS = (2, 4, 8, 16)
CONV_KERNEL = 31
CONV_TAPS_PADDED = 32
RMS_EPS = 1e-6
LN_EPS = 1e-5
ADAM_LR = 0.001
ADAM_B1 = 0.9
ADAM_B2 = 0.999
ADAM_EPS = 1e-08
ADAM_WD = 0.01
ADAM_STEP = 10

LANES = 128
SUBLANES = 8
ROW_CHUNK = 128
HALO = 32
VMEM_LIMIT_BYTES = 56 * 1024 * 1024


def _cparams(*sem):
    return pltpu.CompilerParams(dimension_semantics=sem if sem else None, vmem_limit_bytes=VMEM_LIMIT_BYTES)


def _round_up(n, m):
    return (n + m - 1) // m * m


def _row_tile(rows, cols, max_elems=640 * 1024):
    best = None
    for t in range(16, rows + 1, 16):
        if rows % t == 0 and (best is None or t * cols <= max_elems):
            best = t
    assert best is not None, (rows, cols)
    return best


def _rowsum8(a):
    t, w = a.shape
    return a.reshape(t // SUBLANES, SUBLANES, w).sum(axis=0)


def _mesh_pos():
    return lax.axis_index("x"), lax.axis_index("y"), lax.axis_index("c")


HBM_SPEC = pl.BlockSpec(memory_space=pltpu.HBM)
SEM_SPEC = pl.BlockSpec(memory_space=pltpu.SEMAPHORE)
ANY_SPEC = pl.BlockSpec(memory_space=pl.ANY)
_DATAFLOW = pltpu.SideEffectType.DATAFLOW_SIDE_EFFECTING


def _hbm(a):
    return pltpu.with_memory_space_constraint(a, pltpu.HBM)


def _slot(p):
    return 4 * p[0] + 2 * p[1] + p[2]


def _fill_slot(w, slot_idx, dtype, after, name):
    rows, cols = w.shape
    tr = _row_tile(rows, cols) if rows % 16 == 0 else rows

    def body(idx_ref, w_ref, _, out_ref):
        out_ref[...] = w_ref[...].astype(dtype)

    return pl.pallas_call(
        body, name=name,
        grid_spec=pltpu.PrefetchScalarGridSpec(
            num_scalar_prefetch=1, grid=(rows // tr,),
            in_specs=[pl.BlockSpec((tr, cols), lambda i, idx_ref: (i, 0)), ANY_SPEC],
            out_specs=pl.BlockSpec((None, tr, cols), lambda i, idx_ref: (idx_ref[0], i, 0))),
        out_shape=jax.ShapeDtypeStruct((N_DEV, rows, cols), dtype),
        compiler_params=_cparams("parallel"),
    )(slot_idx, w, after)


def _gather_start(groups, after, name):
    flat = [g for grp in groups for g in grp]
    n, n_grp = len(flat), len(groups)

    def body(*refs):
        lands = refs[:n]
        sems = refs[n + 1:n + 1 + 2 * n_grp]
        token = refs[-1]
        x, y, c = _mesh_pos()
        targets = [(x, y, 1 - c), (1 - x, y, c), (x, 1 - y, c), (1 - x, 1 - y, c)]
        t = 0
        for gi, grp in enumerate(groups):
            for ti in range(len(grp)):
                mine = lands[t].at[_slot((x, y, c))]
                for k, to in enumerate(targets):
                    pltpu.make_async_remote_copy(
                        src_ref=mine, dst_ref=mine,
                        send_sem=sems[2 * gi].at[4 * ti + k], recv_sem=sems[2 * gi + 1].at[4 * ti + k],
                        device_id=to, device_id_type=MESH).start()
                t += 1
        token[...] = jnp.zeros_like(token)

    sem_shapes = []
    for grp in groups:
        sem_shapes += [pltpu.SemaphoreType.DMA((4 * len(grp),))] * 2
    outs = pl.pallas_call(
        body, name=name,
        out_shape=tuple(sem_shapes + [pltpu.HBM(g.shape, g.dtype) for g in flat]
                        + [jax.ShapeDtypeStruct((SUBLANES, LANES), F32)]),
        in_specs=tuple([HBM_SPEC] * n + [ANY_SPEC]),
        out_specs=tuple([SEM_SPEC] * (2 * n_grp) + [HBM_SPEC] * n + [pl.BlockSpec(memory_space=pltpu.VMEM)]),
        input_output_aliases={i: 2 * n_grp + i for i in range(n)},
        compiler_params=pltpu.CompilerParams(has_side_effects=_DATAFLOW),
    )(*[_hbm(g) for g in flat], after)
    sems, lands, token = outs[:2 * n_grp], outs[2 * n_grp:-1], outs[-1]
    res, t = [], 0
    for gi, grp in enumerate(groups):
        res.append((sems[2 * gi], sems[2 * gi + 1], list(lands[t:t + len(grp)])))
        t += len(grp)
    return res, token


NEAR, FAR, ALL_CHIPS = (0, 1), (2,), (0, 1, 2)


def _gather_forward(lands, recv_sems, after, name, which=ALL_CHIPS):
    n, nw = len(lands), len(which)

    def body(*refs):
        land_refs, recv, _ = refs[:n], refs[n], refs[n + 1]
        fsend, frecv = refs[n + 2], refs[n + 3]
        x, y, c = _mesh_pos()
        chips = [(1 - x, y), (x, 1 - y), (1 - x, 1 - y)]
        for t in range(n):
            for i, j in enumerate(which):
                blk = land_refs[t].at[_slot((*chips[j], c))]
                pltpu.make_async_remote_copy(src_ref=blk, dst_ref=blk, send_sem=fsend.at[nw * t + i],
                                             recv_sem=recv.at[4 * t + 1 + j],
                                             device_id=(x, y, 1 - c), device_id_type=MESH).wait_recv()
                pltpu.make_async_remote_copy(src_ref=blk, dst_ref=blk, send_sem=fsend.at[nw * t + i],
                                             recv_sem=frecv.at[nw * t + i],
                                             device_id=(x, y, 1 - c), device_id_type=MESH).start()

    outs = pl.pallas_call(
        body, name=name,
        out_shape=tuple([pltpu.SemaphoreType.DMA((nw * n,))] * 2 + [pltpu.HBM(g.shape, g.dtype) for g in lands]),
        in_specs=tuple([HBM_SPEC] * n + [SEM_SPEC, ANY_SPEC]),
        out_specs=tuple([SEM_SPEC] * 2 + [HBM_SPEC] * n),
        input_output_aliases={i: 2 + i for i in range(n)},
        compiler_params=pltpu.CompilerParams(has_side_effects=_DATAFLOW),
    )(*lands, recv_sems, after)
    return outs[0], outs[1], list(outs[2:])


def _gather_finish(lands, send_sems, recv_sems, arrivals, sends, after, name, own=True, direct_sends=True):
    n = len(lands)
    fwd = list(arrivals) + list(sends)
    sem_args = [send_sems, recv_sems]
    where = []
    for _, fs, fr in fwd:
        pos = []
        for arr in (fs, fr):
            hit = [i for i, have in enumerate(sem_args) if have is arr]
            if not hit:
                sem_args.append(arr)
                hit = [len(sem_args) - 1]
            pos.append(hit[0])
        where.append(pos)

    def body(*refs):
        land_refs = refs[:n]
        send, recv = refs[n], refs[n + 1]
        fwd_refs = [refs[n + p] for pos in where for p in pos]
        x, y, c = _mesh_pos()
        sibling = (x, y, 1 - c)
        chips = [(1 - x, y), (x, 1 - y), (1 - x, 1 - y)]

        def desc(ref, s_sem, r_sem):
            return pltpu.make_async_remote_copy(src_ref=ref, dst_ref=ref, send_sem=s_sem, recv_sem=r_sem,
                                                device_id=sibling, device_id_type=MESH)

        for t in range(n):
            mine = land_refs[t].at[_slot((x, y, c))]
            if own:
                desc(land_refs[t].at[_slot(sibling)], send.at[4 * t], recv.at[4 * t]).wait_recv()
            for a, (which, _, _) in enumerate(fwd):
                fs, fr = fwd_refs[2 * a], fwd_refs[2 * a + 1]
                for i, j in enumerate(which):
                    if a < len(arrivals):
                        desc(land_refs[t].at[_slot((*chips[j], 1 - c))], fs.at[len(which) * t + i],
                             fr.at[len(which) * t + i]).wait_recv()
                    else:
                        desc(land_refs[t].at[_slot((*chips[j], c))], fs.at[len(which) * t + i],
                             fr.at[len(which) * t + i]).wait_send()
            if direct_sends:
                for k in range(4):
                    desc(mine, send.at[4 * t + k], recv.at[4 * t + k]).wait_send()

    outs = pl.pallas_call(
        body, name=name,
        out_shape=tuple(pltpu.HBM(g.shape, g.dtype) for g in lands),
        in_specs=tuple([HBM_SPEC] * n + [SEM_SPEC] * len(sem_args) + [ANY_SPEC]),
        out_specs=tuple([HBM_SPEC] * n),
        input_output_aliases={i: i for i in range(n)},
        compiler_params=pltpu.CompilerParams(has_side_effects=_DATAFLOW),
    )(*lands, *sem_args, after)
    return list(outs)


def _pair_exchange_start(grads, after, name):
    n = len(grads)

    def body(*refs):
        ins, lands = refs[:n], refs[n:2 * n]
        send, recv = refs[2 * n + 1], refs[2 * n + 2]
        token = refs[-1]
        x, y, c = _mesh_pos()
        for t in range(n):
            for q in range(4):
                pltpu.make_async_remote_copy(
                    src_ref=ins[t].at[2 * q + 1 - c], dst_ref=lands[t].at[q],
                    send_sem=send.at[4 * t + q], recv_sem=recv.at[4 * t + q],
                    device_id=(x, y, 1 - c), device_id_type=MESH).start()
        token[...] = jnp.zeros_like(token)

    land_shapes = [(4,) + g.shape[1:] for g in grads]
    outs = pl.pallas_call(
        body, name=name,
        out_shape=tuple([pltpu.SemaphoreType.DMA((4 * n,))] * 2 + [pltpu.HBM(g.shape, g.dtype) for g in grads]
                        + [pltpu.HBM(ls, g.dtype) for ls, g in zip(land_shapes, grads)]
                        + [jax.ShapeDtypeStruct((SUBLANES, LANES), F32)]),
        in_specs=tuple([HBM_SPEC] * (2 * n) + [ANY_SPEC]),
        out_specs=tuple([SEM_SPEC] * 2 + [HBM_SPEC] * (2 * n) + [pl.BlockSpec(memory_space=pltpu.VMEM)]),
        input_output_aliases={i: 2 + i for i in range(2 * n)},
        compiler_params=pltpu.CompilerParams(has_side_effects=_DATAFLOW),
    )(*[_hbm(g) for g in grads], *[_hbm(lax.empty(ls, g.dtype)) for ls, g in zip(land_shapes, grads)], after)
    return outs[0], outs[1], list(outs[2:2 + n]), list(outs[2 + n:2 + 2 * n]), outs[-1]


def _pair_exchange_finish(grads, lands, send_sems, recv_sems, after, name):
    n = len(grads)

    def body(*refs):
        ins, land_refs = refs[:n], refs[n:2 * n]
        send, recv = refs[2 * n], refs[2 * n + 1]
        x, y, c = _mesh_pos()
        for t in range(n):
            for q in range(4):
                cp = pltpu.make_async_remote_copy(
                    src_ref=ins[t].at[q], dst_ref=land_refs[t].at[q], send_sem=send.at[4 * t + q],
                    recv_sem=recv.at[4 * t + q], device_id=(x, y, 1 - c), device_id_type=MESH)
                cp.wait_send()
                cp.wait_recv()

    outs = pl.pallas_call(
        body, name=name,
        out_shape=tuple([pltpu.HBM(g.shape, g.dtype) for g in grads] + [pltpu.HBM(g.shape, g.dtype) for g in lands]),
        in_specs=tuple([HBM_SPEC] * (2 * n) + [SEM_SPEC] * 2 + [ANY_SPEC]),
        out_specs=tuple([HBM_SPEC] * (2 * n)),
        input_output_aliases={i: i for i in range(2 * n)},
        compiler_params=pltpu.CompilerParams(has_side_effects=_DATAFLOW),
    )(*grads, *lands, send_sems, recv_sems, after)
    return list(outs[:n]), list(outs[n:])


def _chip_exchange_start(sums, after, name):
    n = len(sums)

    def body(*refs):
        ins, lands = refs[:n], refs[n:2 * n]
        send, recv = refs[2 * n + 1], refs[2 * n + 2]
        token = refs[-1]
        x, y, c = _mesh_pos()
        chips = [(1 - x, y), (x, 1 - y), (1 - x, 1 - y)]
        for t in range(n):
            for r, chip in enumerate(chips):
                pltpu.make_async_remote_copy(
                    src_ref=ins[t].at[2 * chip[0] + chip[1]], dst_ref=lands[t].at[r],
                    send_sem=send.at[3 * t + r], recv_sem=recv.at[3 * t + r],
                    device_id=(*chip, c), device_id_type=MESH).start()
        token[...] = jnp.zeros_like(token)

    land_shapes = [(3,) + s.shape[1:] for s in sums]
    outs = pl.pallas_call(
        body, name=name,
        out_shape=tuple([pltpu.SemaphoreType.DMA((3 * n,))] * 2 + [pltpu.HBM(s.shape, s.dtype) for s in sums]
                        + [pltpu.HBM(ls, s.dtype) for ls, s in zip(land_shapes, sums)]
                        + [jax.ShapeDtypeStruct((SUBLANES, LANES), F32)]),
        in_specs=tuple([HBM_SPEC] * (2 * n) + [ANY_SPEC]),
        out_specs=tuple([SEM_SPEC] * 2 + [HBM_SPEC] * (2 * n) + [pl.BlockSpec(memory_space=pltpu.VMEM)]),
        input_output_aliases={i: 2 + i for i in range(2 * n)},
        compiler_params=pltpu.CompilerParams(has_side_effects=_DATAFLOW),
    )(*[_hbm(s) for s in sums], *[_hbm(lax.empty(ls, s.dtype)) for ls, s in zip(land_shapes, sums)], after)
    return outs[0], outs[1], list(outs[2:2 + n]), list(outs[2 + n:2 + 2 * n]), outs[-1]


def _chip_exchange_finish(sums, lands, send_sems, recv_sems, after, name):
    n = len(sums)

    def body(*refs):
        ins, land_refs = refs[:n], refs[n:2 * n]
        send, recv = refs[2 * n], refs[2 * n + 1]
        x, y, c = _mesh_pos()
        for t in range(n):
            for r in range(3):
                cp = pltpu.make_async_remote_copy(
                    src_ref=ins[t].at[r], dst_ref=land_refs[t].at[r], send_sem=send.at[3 * t + r],
                    recv_sem=recv.at[3 * t + r],
                    device_id=(x, y, 1 - c), device_id_type=MESH)
                cp.wait_send()
                cp.wait_recv()

    outs = pl.pallas_call(
        body, name=name,
        out_shape=tuple(pltpu.HBM(g.shape, g.dtype) for g in lands),
        in_specs=tuple([HBM_SPEC] * (2 * n) + [SEM_SPEC] * 2 + [ANY_SPEC]),
        out_specs=tuple([HBM_SPEC] * n),
        input_output_aliases={n + i: i for i in range(n)},
        compiler_params=pltpu.CompilerParams(has_side_effects=_DATAFLOW),
    )(*sums, *lands, send_sems, recv_sems, after)
    return list(outs)


def _mm(a, b, *, mode, tm, tn, tk=None, b_blocked=False, out_blocked=False, out_dtypes=(F32,),
        epilogue=None, extras=(), after=None, kb=1, out_places=None, name):
    if mode == "nn":
        m, k = a.shape
        n = b.shape[0] * b.shape[2] if b_blocked else b.shape[1]
        dims = (((1,), (0,)), ((), ()))
    elif mode == "nt":
        m, k = a.shape
        n = b.shape[1] if b_blocked else b.shape[0]
        if b_blocked:
            tk = kb * b.shape[2]
        dims = (((1,), (1,)), ((), ()))
    else:
        k, m = a.shape
        n = b.shape[1]
        dims = (((0,), (0,)), ((), ()))
    tk = k if tk is None else tk
    assert m % tm == 0 and n % tn == 0 and k % tk == 0, (name, m, n, k, tm, tn, tk)
    gm, gn, gk = m // tm, n // tn, k // tk
    if b_blocked:
        assert (tn if mode == "nn" else tk) == kb * b.shape[2], name

    if mode == "nn":
        a_spec = pl.BlockSpec((tm, tk), lambda i, j, kk: (i, kk))
        b_spec = (pl.BlockSpec((None, tk, tn), lambda i, j, kk: (j, kk, 0)) if b_blocked
                  else pl.BlockSpec((tk, tn), lambda i, j, kk: (kk, j)))
    elif mode == "nt":
        a_spec = pl.BlockSpec((tm, tk), lambda i, j, kk: (i, kk))
        b_spec = (pl.BlockSpec((kb, tn, tk // kb), lambda i, j, kk: (kk, j, 0)) if b_blocked
                  else pl.BlockSpec((tn, tk), lambda i, j, kk: (j, kk)))
    else:
        a_spec = pl.BlockSpec((tk, tm), lambda i, j, kk: (kk, i))
        b_spec = pl.BlockSpec((tk, tn), lambda i, j, kk: (kk, j))
    if out_blocked:
        out_spec = pl.BlockSpec((None, tm, tn), lambda i, j, kk: (j, i, 0))
        out_shape = (gn, m, tn)
    else:
        out_spec = pl.BlockSpec((tm, tn), lambda i, j, kk: (i, j))
        out_shape = (m, n)
    extra_specs = [pl.BlockSpec((tm, tn), functools.partial(lambda i, j, kk, off: (i, j + off), off=off))
                   for _, off in extras]
    n_extra, n_out = len(extras), len(out_dtypes)
    n_after = 0 if after is None else 1
    places = out_places if out_places is not None else (None,) * n_out

    def body(a_ref, b_ref, *rest):
        extra_refs = rest[:n_extra]
        out_refs = rest[n_extra + n_after:n_extra + n_after + n_out]

        def finish(acc):
            if epilogue is None:
                res = (acc,)
            else:
                res = epilogue(acc, *[e[...] for e in extra_refs])
            for o_ref, r in zip(out_refs, res):
                o_ref[...] = r.astype(o_ref.dtype)

        if mode == "nt" and b_blocked:
            bk = tk // kb
            part = lax.dot_general(a_ref[:, :bk], b_ref[0], dims, preferred_element_type=F32)
            for h in range(1, kb):
                part = part + lax.dot_general(a_ref[:, h * bk:(h + 1) * bk], b_ref[h], dims,
                                              preferred_element_type=F32)
        else:
            part = lax.dot_general(a_ref[...], b_ref[...], dims, preferred_element_type=F32)
        if gk == 1:
            finish(part)
        else:
            acc_ref = rest[-1]
            kk = pl.program_id(2)

            @pl.when(kk == 0)
            def _():
                acc_ref[...] = part

            @pl.when(kk > 0)
            def _():
                acc_ref[...] += part

            @pl.when(kk == gk - 1)
            def _():
                finish(acc_ref[...])

    outs = pl.pallas_call(
        body, name=name, grid=(gm, gn, gk),
        in_specs=[a_spec, b_spec] + extra_specs + [ANY_SPEC] * n_after,
        out_specs=[out_spec if place is None else
                   pl.BlockSpec((tm, tn), functools.partial(lambda i, j, kk, off: (i, j + off), off=place[1] // tn))
                   for place in places],
        out_shape=[jax.ShapeDtypeStruct(out_shape if place is None else (m, place[0]), dt)
                   for dt, place in zip(out_dtypes, places)],
        scratch_shapes=[pltpu.VMEM((tm, tn), F32)] if gk > 1 else [],
        compiler_params=_cparams("parallel", "parallel", "arbitrary"),
    )(a, b, *[e for e, _ in extras], *([] if after is None else [after]))
    return outs[0] if n_out == 1 else outs


def _mm_slots(a, w, slots, *, over, tm, tn=None, out_dtype=F32, epilogue=None, base=None, name):
    m = a.shape[0]
    ns = slots.shape[0]
    n_slots, w1, w2 = w.shape
    assert m % tm == 0
    if over == "n":
        k, bn = w1, w2

        def body(slots_ref, a_ref, w_ref, *rest):
            out_ref = rest[-1]
            acc = jnp.dot(a_ref[...], w_ref[...], preferred_element_type=F32)
            out_ref[...] = (acc if epilogue is None else epilogue(acc)).astype(out_ref.dtype)

        in_specs = [pl.BlockSpec((tm, k), lambda i, j, s: (i, 0)),
                    pl.BlockSpec((None, k, bn), lambda i, j, s: (s[j], 0, 0))]
        args = [a, w]
        aliases = {}
        if base is not None:
            in_specs.append(ANY_SPEC)
            args.append(base)
            aliases = {3: 0}
        return pl.pallas_call(
            body, name=name,
            grid_spec=pltpu.PrefetchScalarGridSpec(
                num_scalar_prefetch=1, grid=(m // tm, ns), in_specs=in_specs,
                out_specs=pl.BlockSpec((tm, bn), lambda i, j, s: (i, s[j]))),
            out_shape=jax.ShapeDtypeStruct((m, n_slots * bn), out_dtype),
            input_output_aliases=aliases,
            compiler_params=_cparams("parallel", "arbitrary"),
        )(slots, *args)

    bk, n = w1, w2
    tn = n if tn is None else tn
    assert n % tn == 0

    def body(slots_ref, a_ref, w_ref, *rest):
        out_ref, acc_ref = rest[-2], rest[-1]
        kk = pl.program_id(2)
        part = jnp.dot(a_ref[...], w_ref[...], preferred_element_type=F32)

        @pl.when(kk == 0)
        def _():
            acc_ref[...] = part if base is None else part + rest[0][...]

        @pl.when(kk > 0)
        def _():
            acc_ref[...] += part

        @pl.when(kk == ns - 1)
        def _():
            out_ref[...] = acc_ref[...].astype(out_ref.dtype)

    in_specs = [pl.BlockSpec((tm, bk), lambda i, j, kk, s: (i, s[kk])),
                pl.BlockSpec((None, bk, tn), lambda i, j, kk, s: (s[kk], 0, j))]
    args = [a, w]
    if base is not None:
        in_specs.append(pl.BlockSpec((tm, tn), lambda i, j, kk, s: (i, j)))
        args.append(base)
    return pl.pallas_call(
        body, name=name,
        grid_spec=pltpu.PrefetchScalarGridSpec(
            num_scalar_prefetch=1, grid=(m // tm, n // tn, ns), in_specs=in_specs,
            out_specs=pl.BlockSpec((tm, tn), lambda i, j, kk, s: (i, j)),
            scratch_shapes=[pltpu.VMEM((tm, tn), F32)]),
        out_shape=jax.ShapeDtypeStruct((m, n), out_dtype),
        compiler_params=_cparams("parallel", "parallel", "arbitrary"),
    )(slots, *args)


def _gate_mix(ya_pre, s, wpo, wco, proj, d_model, name):
    lp, width = ya_pre.shape
    nb, _, bn = wpo.shape
    ga_off = (proj.shape[1] - 2 * d_model) // bn
    gb_off = (proj.shape[1] - d_model) // bn

    def body(ya_ref, s_ref, wpo_ref, wco_ref, ga_ref, gb_ref, m_ref, y_a_ref, y_b_ref):
        y_a = jnp.dot(ya_ref[...], wpo_ref[...], preferred_element_type=F32)
        y_b = jnp.dot(s_ref[...], wco_ref[...], preferred_element_type=F32)
        m = jax.nn.sigmoid(ga_ref[...]) * y_a + jax.nn.sigmoid(gb_ref[...]) * y_b
        m_ref[...] = m.astype(BF16)
        y_a_ref[...] = y_a.astype(BF16)
        y_b_ref[...] = y_b.astype(BF16)

    act_spec = pl.BlockSpec((lp, width), lambda j: (0, 0))
    w_spec = pl.BlockSpec((None, width, bn), lambda j: (j, 0, 0))
    out_spec = pl.BlockSpec((lp, bn), lambda j: (0, j))
    return pl.pallas_call(
        body, name=name, grid=(nb,),
        in_specs=[act_spec, act_spec, w_spec, w_spec,
                  pl.BlockSpec((lp, bn), lambda j: (0, j + ga_off)),
                  pl.BlockSpec((lp, bn), lambda j: (0, j + gb_off))],
        out_specs=[out_spec] * 3,
        out_shape=[jax.ShapeDtypeStruct((lp, nb * bn), BF16)] * 3,
        compiler_params=_cparams("parallel"),
    )(ya_pre, s, wpo, wco, proj, proj)


def _rms_stats(x):
    return lax.rsqrt(jnp.mean(x * x, axis=-1, keepdims=True) + RMS_EPS)


def _rms_bwd(x, g, dy):
    r = _rms_stats(x)
    nrm = x * r
    dn = dy * g
    dx = r * (dn - nrm * jnp.mean(dn * nrm, axis=-1, keepdims=True))
    return dx, dy * nrm


def _rowwise(body, ins, outs, accs, *, lp, name):
    tr = _row_tile(lp, max(a.shape[1] for a in ins))
    n_in, n_out, n_acc = len(ins), len(outs), len(accs)

    def kernel_body(*refs):
        i = pl.program_id(0)
        acc_refs = refs[n_in + n_out:]

        @pl.when(i == 0)
        def _():
            for r in acc_refs:
                r[...] = jnp.zeros_like(r)

        body(i * tr, refs[:n_in], refs[n_in:n_in + n_out], acc_refs)

    in_specs = []
    for a in ins:
        if a.shape[0] == lp:
            in_specs.append(pl.BlockSpec((tr, a.shape[1]), lambda i: (i, 0)))
        else:
            in_specs.append(pl.BlockSpec(a.shape, lambda i: (0, 0)))
    out_specs = [pl.BlockSpec((tr, w), lambda i: (i, 0)) for w, _ in outs]
    out_specs += [pl.BlockSpec((SUBLANES, w), lambda i: (0, 0)) for w in accs]
    out_shape = [jax.ShapeDtypeStruct((lp, w), d) for w, d in outs]
    out_shape += [jax.ShapeDtypeStruct((SUBLANES, w), F32) for w in accs]
    return pl.pallas_call(
        kernel_body, name=name, grid=(lp // tr,), in_specs=in_specs, out_specs=out_specs,
        out_shape=out_shape, compiler_params=_cparams("arbitrary"),
    )(*ins)


def _rms_pre(h0, g, lp):
    d = h0.shape[1]

    def body(row0, ins, outs, accs):
        h_ref, g_ref = ins
        x = h_ref[...]
        outs[0][...] = (x * _rms_stats(x) * g_ref[...]).astype(BF16)

    return _rowwise(body, [h0, g], [(d, BF16)], [], lp=lp, name="rms_pre")[0]


def _post_mix(o, h0, g_post_mix, g_pre_mlp, lp):
    d = h0.shape[1]

    def body(row0, ins, outs, accs):
        o_ref, h0_ref, g1_ref, g2_ref = ins
        o_v = o_ref[...]
        h1 = h0_ref[...] + o_v * _rms_stats(o_v) * g1_ref[...]
        outs[0][...] = h1
        outs[1][...] = (h1 * _rms_stats(h1) * g2_ref[...]).astype(BF16)

    return _rowwise(body, [o, h0, g_post_mix, g_pre_mlp], [(d, F32), (d, BF16)], [], lp=lp, name="post_mix")


def _loss_head(f, h1, target, g_post_mlp, lp, seq):
    d = f.shape[1]

    def body(row0, ins, outs, accs):
        f_ref, h1_ref, t_ref, g_ref = ins
        df_ref, dh_ref = outs
        dg_ref, loss_ref = accs
        f_v, g = f_ref[...], g_ref[...]
        r = _rms_stats(f_v)
        nrm = f_v * r
        rows = row0 + lax.broadcasted_iota(jnp.int32, (f_v.shape[0], 1), 0)
        valid = (rows >= N_META) & (rows < N_META + seq)
        err = jnp.where(valid, h1_ref[...] + nrm * g - t_ref[...], 0.0)
        loss_ref[...] += 0.5 * jnp.sum(jnp.mean(err * err, axis=-1, keepdims=True))
        dy = err * (1.0 / d)
        dn = dy * g
        df_ref[...] = (r * (dn - nrm * jnp.mean(dn * nrm, axis=-1, keepdims=True))).astype(BF16)
        dh_ref[...] = dy
        dg_ref[...] += _rowsum8(dy * nrm)

    return _rowwise(body, [f, h1, target, g_post_mlp], [(d, BF16), (d, F32)], [d, LANES], lp=lp, name="loss_head")


def _mid_bwd(du2, h1, dh, o, g_pre_mlp, g_post_mix, lp):
    d = h1.shape[1]

    def body(row0, ins, outs, accs):
        du2_ref, h1_ref, dh_ref, o_ref, g2_ref, g1_ref = ins
        dx2, dg2 = _rms_bwd(h1_ref[...], g2_ref[...], du2_ref[...])
        dh1 = dh_ref[...] + dx2
        do, dg1 = _rms_bwd(o_ref[...], g1_ref[...], dh1)
        outs[0][...] = dh1
        outs[1][...] = do.astype(BF16)
        accs[0][...] += _rowsum8(dg2)
        accs[1][...] += _rowsum8(dg1)

    return _rowwise(body, [du2, h1, dh, o, g_pre_mlp, g_post_mix], [(d, F32), (d, BF16)], [d, d], lp=lp,
                    name="mid_bwd")


def _pre_mix_bwd(du1, h0, dh1, g_pre_mix, lp):
    d = h0.shape[1]

    def body(row0, ins, outs, accs):
        du1_ref, h0_ref, dh1_ref, g_ref = ins
        dx, dg = _rms_bwd(h0_ref[...], g_ref[...], du1_ref[...])
        outs[0][...] = dh1_ref[...] + dx
        accs[0][...] += _rowsum8(dg)

    return _rowwise(body, [du1, h0, dh1, g_pre_mix], [(d, F32)], [d], lp=lp, name="pre_mix_bwd")


def _ln_stats(c):
    mu = jnp.mean(c, axis=-1, keepdims=True)
    var = jnp.mean(jnp.square(c - mu), axis=-1, keepdims=True)
    return mu, lax.rsqrt(var + LN_EPS)


def _ln_silu(c, ln_g, ln_b, lp):
    w = c.shape[1]

    def body(row0, ins, outs, accs):
        c_ref, g_ref, b_ref = ins
        c_v = c_ref[...]
        mu, rstd = _ln_stats(c_v)
        ln = (c_v - mu) * rstd * g_ref[...] + b_ref[...]
        outs[0][...] = (ln * jax.nn.sigmoid(ln)).astype(BF16)

    return _rowwise(body, [c, ln_g, ln_b], [(w, BF16)], [], lp=lp, name="ln_silu")[0]


def _ln_silu_bwd(c, ds, ln_g, ln_b, lp):
    w = c.shape[1]

    def body(row0, ins, outs, accs):
        c_ref, ds_ref, g_ref, b_ref = ins
        c_v, g = c_ref[...], g_ref[...]
        mu, rstd = _ln_stats(c_v)
        nrm = (c_v - mu) * rstd
        ln = nrm * g + b_ref[...]
        sig = jax.nn.sigmoid(ln)
        dln = ds_ref[...] * (sig * (1.0 + ln * (1.0 - sig)))
        dn = dln * g
        dc = rstd * (dn - jnp.mean(dn, axis=-1, keepdims=True) - nrm * jnp.mean(dn * nrm, axis=-1, keepdims=True))
        outs[0][...] = dc
        accs[0][...] += _rowsum8(dln * nrm)
        accs[1][...] += _rowsum8(dln)
        accs[2][...] += _rowsum8(dc)

    return _rowwise(body, [c, ds, ln_g, ln_b], [(w, F32)], [w, w, w], lp=lp, name="ln_silu_bwd")


def _chunk_with_history(ref, i, cols=slice(None)):
    t0 = pl.multiple_of(i * ROW_CHUNK, ROW_CHUNK)
    lo0 = pl.multiple_of(jnp.maximum(t0 - HALO, 0), SUBLANES)
    lo = jnp.where(i > 0, ref[pl.ds(lo0, HALO), cols], 0.0)
    return jnp.concatenate([lo, ref[pl.ds(t0, ROW_CHUNK), cols]], axis=0)


def _chunk_with_future(ref, i, n_chunks, cols=slice(None)):
    t0 = pl.multiple_of(i * ROW_CHUNK, ROW_CHUNK)
    hi0 = pl.multiple_of(jnp.minimum(t0 + ROW_CHUNK, (n_chunks - 1) * ROW_CHUNK), SUBLANES)
    hi = jnp.where(i < n_chunks - 1, ref[pl.ds(hi0, HALO), cols], 0.0)
    return jnp.concatenate([ref[pl.ds(t0, ROW_CHUNK), cols], hi], axis=0)


def _inv_count(t0, n_rows, window):
    pos = t0 + lax.broadcasted_iota(jnp.int32, (n_rows, 1), 0)
    return 1.0 / jnp.minimum(pos + 1, window).astype(F32)


def _pool_delta(z_hist, t0, window):
    s = z_hist
    sh = 1
    while sh < window:
        s = s + pltpu.roll(s, sh, 0)
        sh *= 2
    cur = z_hist[HALO:, :]
    return s[HALO:, :] * _inv_count(t0, ROW_CHUNK, window) - cur


def _pool_fwd(proj, wpg, pool_scale, lp):
    n_grp, gdim, _ = wpg.shape
    width = n_grp * gdim
    n_chunks = lp // ROW_CHUNK

    def body(z_ref, w_ref, sc_ref, out_ref):
        for g, window in enumerate(POOL_WINDOWS):
            cols = slice(g * gdim, (g + 1) * gdim)

            def chunk(i, carry, cols=cols, g=g, window=window):
                t0 = pl.multiple_of(i * ROW_CHUNK, ROW_CHUNK)
                d = _pool_delta(_chunk_with_history(z_ref, i, cols), t0, window)
                q = jnp.dot(d.astype(BF16), w_ref[g], preferred_element_type=F32)
                out_ref[pl.ds(t0, ROW_CHUNK), cols] = (q * sc_ref[:, cols]).astype(BF16)
                return carry

            lax.fori_loop(0, n_chunks, chunk, 0)

    return pl.pallas_call(
        body, name="pool_fwd", grid=(1,),
        in_specs=[pl.BlockSpec((lp, width), lambda i: (0, 0)),
                  pl.BlockSpec(wpg.shape, lambda i: (0, 0, 0)),
                  pl.BlockSpec(pool_scale.shape, lambda i: (0, 0))],
        out_specs=pl.BlockSpec((lp, width), lambda i: (0, 0)),
        out_shape=jax.ShapeDtypeStruct((lp, width), BF16),
        compiler_params=_cparams("arbitrary"),
    )(proj, wpg, pool_scale)


def _pool_bwd(proj, d_ya, wpg, pool_scale, dproj, lp):
    n_grp, gdim, _ = wpg.shape
    width = n_grp * gdim
    n_chunks = lp // ROW_CHUNK
    ext = ROW_CHUNK + HALO

    def body(z_ref, dya_ref, w_ref, sc_ref, _, dz_ref, dw_ref, dsc_ref):
        dw_ref[...] = jnp.zeros_like(dw_ref)
        dsc_ref[...] = jnp.zeros_like(dsc_ref)
        for g, window in enumerate(POOL_WINDOWS):
            cols = slice(g * gdim, (g + 1) * gdim)

            def chunk(i, carry, cols=cols, g=g, window=window):
                t0 = pl.multiple_of(i * ROW_CHUNK, ROW_CHUNK)
                w_g = w_ref[g]
                scale = sc_ref[:, cols]
                d = _pool_delta(_chunk_with_history(z_ref, i, cols), t0, window).astype(BF16)
                dya_ext = _chunk_with_future(dya_ref, i, n_chunks, cols)
                dya = dya_ext[:ROW_CHUNK, :]
                q = jnp.dot(d, w_g, preferred_element_type=F32)
                dsc_ref[:, cols] += _rowsum8(dya * q)
                e_ext = (dya_ext * scale).astype(BF16)
                dw_ref[g] += lax.dot_general(d, e_ext[:ROW_CHUNK, :], (((0,), (0,)), ((), ())),
                                             preferred_element_type=F32)
                dd_ext = lax.dot_general(e_ext, w_g, (((1,), (1,)), ((), ())), preferred_element_type=F32)
                s = dd_ext * _inv_count(t0, ext, window)
                sh = 1
                while sh < window:
                    s = s + pltpu.roll(s, ext - sh, 0)
                    sh *= 2
                dz_ref[pl.ds(t0, ROW_CHUNK), cols] = (s[:ROW_CHUNK, :] - dd_ext[:ROW_CHUNK, :]).astype(BF16)
                return carry

            lax.fori_loop(0, n_chunks, chunk, 0)

    blk = pl.BlockSpec((lp, width), lambda i: (0, 0))
    return pl.pallas_call(
        body, name="pool_bwd", grid=(1,),
        in_specs=[blk, blk, pl.BlockSpec(wpg.shape, lambda i: (0, 0, 0)),
                  pl.BlockSpec(pool_scale.shape, lambda i: (0, 0)), ANY_SPEC],
        out_specs=[blk, pl.BlockSpec(wpg.shape, lambda i: (0, 0, 0)),
                   pl.BlockSpec((SUBLANES, width), lambda i: (0, 0))],
        out_shape=[jax.ShapeDtypeStruct(dproj.shape, BF16), jax.ShapeDtypeStruct(wpg.shape, F32),
                   jax.ShapeDtypeStruct((SUBLANES, width), F32)],
        input_output_aliases={4: 0},
        compiler_params=_cparams("arbitrary"),
    )(proj, d_ya, wpg, pool_scale, dproj)


def _conv_fwd(proj, w_dw, b_dw, lp, width, v_col0):
    n_chunks = lp // ROW_CHUNK
    v_blk0, g_blk0 = v_col0 // LANES, (v_col0 + width) // LANES

    def body(v_ref, gc_ref, w_ref, b_ref, c_ref, a_pad):
        a_pad[pl.ds(0, HALO), :] = jnp.zeros((HALO, LANES), F32)
        a_pad[pl.ds(HALO, lp), :] = v_ref[...] * jax.nn.sigmoid(gc_ref[...])

        def chunk(i, carry):
            t0 = pl.multiple_of(i * ROW_CHUNK, ROW_CHUNK)
            hist = a_pad[pl.ds(t0, ROW_CHUNK + HALO), :]
            acc = jnp.zeros((ROW_CHUNK, LANES), F32)
            for k in range(CONV_KERNEL):
                acc = acc + w_ref[k:k + 1, :] * pltpu.roll(hist, CONV_KERNEL - 1 - k, 0)[HALO:, :]
            c_ref[pl.ds(t0, ROW_CHUNK), :] = acc + b_ref[...]
            return carry

        lax.fori_loop(0, n_chunks, chunk, 0)

    return pl.pallas_call(
        body, name="conv_fwd", grid=(width // LANES,),
        in_specs=[pl.BlockSpec((lp, LANES), lambda j: (0, j + v_blk0)),
                  pl.BlockSpec((lp, LANES), lambda j: (0, j + g_blk0)),
                  pl.BlockSpec((CONV_TAPS_PADDED, LANES), lambda j: (0, j)),
                  pl.BlockSpec((1, LANES), lambda j: (0, j))],
        out_specs=pl.BlockSpec((lp, LANES), lambda j: (0, j)),
        out_shape=jax.ShapeDtypeStruct((lp, width), F32),
        scratch_shapes=[pltpu.VMEM((lp + HALO, LANES), F32)],
        compiler_params=_cparams("parallel"),
    )(proj, proj, w_dw, b_dw)


def _conv_bwd(proj, dc, w_dw, dproj, lp, width, v_col0):
    n_chunks = lp // ROW_CHUNK
    ext = ROW_CHUNK + HALO
    v_blk0, g_blk0 = v_col0 // LANES, (v_col0 + width) // LANES

    def body(v_ref, gc_ref, dc_ref, w_ref, _, dv_ref, dgc_ref, dw_ref, a_pad, dc_pad, dw_acc):
        sig = jax.nn.sigmoid(gc_ref[...])
        a_pad[pl.ds(0, HALO), :] = jnp.zeros((HALO, LANES), F32)
        a_pad[pl.ds(HALO, lp), :] = v_ref[...] * sig
        dc_pad[pl.ds(0, lp), :] = dc_ref[...]
        dc_pad[pl.ds(lp, HALO), :] = jnp.zeros((HALO, LANES), F32)
        dw_acc[...] = jnp.zeros_like(dw_acc)

        def chunk(i, carry):
            t0 = pl.multiple_of(i * ROW_CHUNK, ROW_CHUNK)
            hist = a_pad[pl.ds(t0, ext), :]
            fut = dc_pad[pl.ds(t0, ext), :]
            dc_cur = fut[:ROW_CHUNK, :]
            da = jnp.zeros((ROW_CHUNK, LANES), F32)
            for k in range(CONV_KERNEL):
                lag = CONV_KERNEL - 1 - k
                da = da + w_ref[k:k + 1, :] * pltpu.roll(fut, (ext - lag) % ext, 0)[:ROW_CHUNK, :]
                dw_acc[pl.ds(SUBLANES * k, SUBLANES), :] += _rowsum8(dc_cur * pltpu.roll(hist, lag, 0)[HALO:, :])
            rows = pl.ds(t0, ROW_CHUNK)
            sg = jax.nn.sigmoid(gc_ref[rows, :])
            dv_ref[rows, :] = (da * sg).astype(BF16)
            dgc_ref[rows, :] = (da * v_ref[rows, :] * sg * (1.0 - sg)).astype(BF16)
            return carry

        lax.fori_loop(0, n_chunks, chunk, 0)
        dw_ref[...] = dw_acc[...].reshape(CONV_TAPS_PADDED, SUBLANES, LANES).sum(axis=1)

    col = lambda j: (0, j)
    return pl.pallas_call(
        body, name="conv_bwd", grid=(width // LANES,),
        in_specs=[pl.BlockSpec((lp, LANES), lambda j: (0, j + v_blk0)),
                  pl.BlockSpec((lp, LANES), lambda j: (0, j + g_blk0)),
                  pl.BlockSpec((lp, LANES), col),
                  pl.BlockSpec((CONV_TAPS_PADDED, LANES), col), ANY_SPEC],
        out_specs=[pl.BlockSpec((lp, LANES), lambda j: (0, j + v_blk0)), pl.BlockSpec((lp, LANES), col),
                   pl.BlockSpec((CONV_TAPS_PADDED, LANES), col)],
        out_shape=[jax.ShapeDtypeStruct(dproj.shape, BF16), jax.ShapeDtypeStruct((lp, width), BF16),
                   jax.ShapeDtypeStruct((CONV_TAPS_PADDED, width), F32)],
        scratch_shapes=[pltpu.VMEM((lp + HALO, LANES), F32), pltpu.VMEM((lp + HALO, LANES), F32),
                        pltpu.VMEM((CONV_TAPS_PADDED * SUBLANES, LANES), F32)],
        input_output_aliases={4: 0},
        compiler_params=_cparams("parallel"),
    )(proj, proj, dc, w_dw, dproj)


def _place_columns(dst, pieces, name):
    m = dst.shape[0]
    tile = 512
    counts = [p.shape[1] // tile for p, _ in pieces]
    starts = [sum(counts[:i]) for i in range(len(pieces))]
    n_steps = sum(counts)

    def local(s, i):
        return jnp.clip(s - starts[i], 0, counts[i] - 1)

    def out_index(s):
        blk = pieces[0][1] // tile + local(s, 0)
        for i in range(1, len(pieces)):
            blk = jnp.where(s >= starts[i], pieces[i][1] // tile + local(s, i), blk)
        return 0, blk

    def body(*refs):
        out_ref = refs[-1]
        s = pl.program_id(0)
        for i in range(len(pieces)):
            @pl.when((s >= starts[i]) & (s < starts[i] + counts[i]))
            def _(i=i):
                out_ref[...] = refs[i][...]

    return pl.pallas_call(
        body, name=name, grid=(n_steps,),
        in_specs=[pl.BlockSpec((m, tile), functools.partial(lambda s, i: (0, local(s, i)), i=i))
                  for i in range(len(pieces))] + [ANY_SPEC],
        out_specs=pl.BlockSpec((m, tile), out_index),
        out_shape=jax.ShapeDtypeStruct(dst.shape, dst.dtype),
        input_output_aliases={len(pieces): 0},
        compiler_params=_cparams("arbitrary"),
    )(*[p for p, _ in pieces], dst)


def _adamw_math(w, g, m, v):
    m = ADAM_B1 * m + (1.0 - ADAM_B1) * g
    v = ADAM_B2 * v + (1.0 - ADAM_B2) * jnp.square(g)
    m_hat = m / (1.0 - ADAM_B1 ** ADAM_STEP)
    v_hat = v / (1.0 - ADAM_B2 ** ADAM_STEP)
    delta = -ADAM_LR * (m_hat / (jnp.sqrt(v_hat) + ADAM_EPS) + ADAM_WD * w)
    return delta, m, v


def _pair_sum(own, recv, where, name):
    _, _, rows, cols = own.shape
    tr = _row_tile(rows, cols, 1024 * 1024)

    def body(where_ref, own_ref, recv_ref, out_ref):
        out_ref[...] = (own_ref[...].astype(F32) + recv_ref[...].astype(F32)).astype(BF16)

    return pl.pallas_call(
        body, name=name,
        grid_spec=pltpu.PrefetchScalarGridSpec(
            num_scalar_prefetch=1, grid=(3, rows // tr),
            in_specs=[pl.BlockSpec((None, None, tr, cols), lambda r, i, wh: (wh[2 + r], wh[0], i, 0)),
                      pl.BlockSpec((None, tr, cols), lambda r, i, wh: (wh[2 + r], i, 0))],
            out_specs=pl.BlockSpec((None, tr, cols), lambda r, i, wh: (wh[2 + r], i, 0))),
        out_shape=jax.ShapeDtypeStruct((4, rows, cols), BF16),
        compiler_params=_cparams("parallel", "parallel"),
    )(where, own, recv)


def _adamw_big(w, m, v, own, from_sibling, recv3, where, name):
    rows, cols = w.shape
    tr = _row_tile(rows, cols, 256 * 1024)

    def body(where_ref, w_ref, m_ref, v_ref, own_ref, sib_ref, r_ref, g_out, d_out, m_out, v_out):
        g = own_ref[...].astype(F32) + sib_ref[...].astype(F32)
        for r in range(3):
            g = g + r_ref[r].astype(F32)
        delta, m_new, v_new = _adamw_math(w_ref[...], g, m_ref[...], v_ref[...])
        g_out[...] = g
        d_out[...] = delta
        m_out[...] = m_new
        v_out[...] = v_new

    blk = pl.BlockSpec((tr, cols), lambda i, q_ref: (i, 0))
    return pl.pallas_call(
        body, name=name,
        grid_spec=pltpu.PrefetchScalarGridSpec(
            num_scalar_prefetch=1, grid=(rows // tr,),
            in_specs=[blk, blk, blk,
                      pl.BlockSpec((None, None, tr, cols), lambda i, wh: (wh[1], wh[0], i, 0)),
                      pl.BlockSpec((None, tr, cols), lambda i, wh: (wh[1], i, 0)),
                      pl.BlockSpec((3, tr, cols), lambda i, wh: (0, i, 0))],
            out_specs=[blk] * 4),
        out_shape=[jax.ShapeDtypeStruct((rows, cols), F32)] * 4,
        compiler_params=_cparams("parallel"),
    )(where, w, m, v, own, from_sibling, recv3)


def _small_update(me_idx, packed, rep_params, rep_places, meta_wmv, meta_row0, wdw_wmv, wdw_row0, loss_row0):
    n_rep = len(rep_params)
    meta_rows, meta_cols = meta_wmv[0].shape
    wdw_rows, wdw_cols = wdw_wmv[0].shape

    def body(me_ref, *refs):
        pos = 0

        def take(k):
            nonlocal pos
            out = refs[pos:pos + k]
            pos += k
            return out

        rep_in = [take(3) for _ in range(n_rep)]
        rep_g = take(n_rep)
        meta_in, (meta_g,) = take(3), take(1)
        wdw_in, (wdw_g,) = take(3), take(1)
        (loss_ref,) = take(1)
        rep_out = [take(4) for _ in range(n_rep)]
        meta_out, wdw_out = take(4), take(4)
        (loss_out,) = take(1)

        def update(wmv, g, outs):
            delta, m_new, v_new = _adamw_math(wmv[0][...], g, wmv[1][...], wmv[2][...])
            for o_ref, val in zip(outs, (g, delta, m_new, v_new)):
                o_ref[...] = val

        for wmv, g_ref, outs in zip(rep_in, rep_g, rep_out):
            g = jnp.sum(g_ref[0], axis=0, keepdims=True)
            for j in range(1, N_DEV):
                g = g + jnp.sum(g_ref[j], axis=0, keepdims=True)
            update(wmv, g, outs)
        for wmv, g_ref, outs in ((meta_in, meta_g, meta_out), (wdw_in, wdw_g, wdw_out)):
            g = g_ref[0]
            for j in range(1, N_DEV):
                g = g + g_ref[j]
            update(wmv, g, outs)
        total = loss_ref[0]
        for j in range(1, N_DEV):
            total = total + loss_ref[j]
        loss_out[...] = total

    def whole(a):
        nd = a.ndim
        return pl.BlockSpec(a.shape, lambda i, me_ref, nd=nd: (0,) * nd)

    ins, in_specs = [], []
    for wmv in rep_params:
        ins += list(wmv)
        in_specs += [whole(a) for a in wmv]
    for wmv, (row0, col0) in zip(rep_params, rep_places):
        width = wmv[0].shape[1]
        ins.append(packed)
        in_specs.append(pl.BlockSpec((N_DEV, SUBLANES, width),
                                     lambda i, me_ref, rb=row0 // SUBLANES, cb=col0 // width: (0, rb, cb)))
    ins += list(meta_wmv) + [packed]
    in_specs += [whole(a) for a in meta_wmv]
    in_specs.append(pl.BlockSpec((N_DEV, meta_rows, meta_cols),
                                 lambda i, me_ref, rb=meta_row0 // meta_rows: (0, rb, me_ref[0])))
    ins += list(wdw_wmv) + [packed]
    in_specs += [whole(a) for a in wdw_wmv]
    in_specs.append(pl.BlockSpec((N_DEV, wdw_rows, wdw_cols),
                                 lambda i, me_ref, rb=wdw_row0 // wdw_rows: (0, rb, me_ref[0])))
    ins.append(packed)
    in_specs.append(pl.BlockSpec((N_DEV, SUBLANES, LANES), lambda i, me_ref, rb=loss_row0 // SUBLANES: (0, rb, 0)))

    out_shape, out_specs = [], []
    for wmv in list(rep_params) + [meta_wmv, wdw_wmv]:
        out_shape += [jax.ShapeDtypeStruct(wmv[0].shape, F32)] * 4
        out_specs += [whole(wmv[0])] * 4
    out_shape.append(jax.ShapeDtypeStruct((SUBLANES, LANES), F32))
    out_specs.append(pl.BlockSpec((SUBLANES, LANES), lambda i, me_ref: (0, 0)))

    outs = pl.pallas_call(
        body, name="small_update",
        grid_spec=pltpu.PrefetchScalarGridSpec(num_scalar_prefetch=1, grid=(1,), in_specs=in_specs,
                                               out_specs=out_specs),
        out_shape=out_shape, compiler_params=_cparams("arbitrary"),
    )(me_idx, *ins)
    groups = [outs[4 * i:4 * i + 4] for i in range(n_rep + 2)]
    return groups[:n_rep], groups[n_rep], groups[n_rep + 1], outs[-1]


def kernel(x, meta, g_pre_mix, w_in, w_pool_grp, pool_scale, w_pool_out, w_dw, b_dw, conv_ln_g, conv_ln_b, w_conv_out, w_o, g_post_mix, g_pre_mlp, w_up, w_down, g_post_mlp, loss_target, m_meta, m_g_pre_mix, m_w_in, m_w_pool_grp, m_pool_scale, m_w_pool_out, m_w_dw, m_b_dw, m_conv_ln_g, m_conv_ln_b, m_w_conv_out, m_w_o, m_g_post_mix, m_g_pre_mlp, m_w_up, m_w_down, m_g_post_mlp, v_meta, v_g_pre_mix, v_w_in, v_w_pool_grp, v_pool_scale, v_w_pool_out, v_w_dw, v_b_dw, v_conv_ln_g, v_conv_ln_b, v_w_conv_out, v_w_o, v_g_post_mix, v_g_pre_mlp, v_w_up, v_w_down, v_g_post_mlp):
    seq, d = x.shape[1], x.shape[2]
    pool_w = pool_scale.shape[1]
    conv_w = b_dw.shape[1]
    n_grp, grp_rows, gdim = w_pool_grp.shape[1:]
    lp = _round_up(N_META + seq, ROW_CHUNK)
    tm_half = lp // 2 if (lp // 2) % 16 == 0 else lp
    c_idx = lax.axis_index("c").astype(jnp.int32)
    chip_idx = (2 * lax.axis_index("x") + lax.axis_index("y")).astype(jnp.int32)
    me_idx = 2 * chip_idx + c_idx

    pad_taps = ((0, CONV_TAPS_PADDED - CONV_KERNEL), (0, 0))
    big = dict(w_in=w_in[0], w_pool_grp=w_pool_grp[0].reshape(n_grp * grp_rows, gdim), w_pool_out=w_pool_out[0],
               w_conv_out=w_conv_out[0], w_o=w_o[0], w_up=w_up[0], w_down=w_down[0])
    big_names = list(big)
    moments = dict(w_in=(m_w_in, v_w_in), w_pool_grp=(m_w_pool_grp, v_w_pool_grp), w_pool_out=(m_w_pool_out, v_w_pool_out),
                   w_conv_out=(m_w_conv_out, v_w_conv_out), w_o=(m_w_o, v_w_o), w_up=(m_w_up, v_w_up),
                   w_down=(m_w_down, v_w_down))
    slot_idx = me_idx.reshape(1)
    sources = dict(big, meta=meta, w_dw=jnp.pad(w_dw[0], pad_taps))

    def fill(k, after):
        return _fill_slot(sources[k], slot_idx, BF16 if k in big else F32, after, "fill_" + k)

    gather_groups = [["meta", "w_dw"], ["w_in"], ["w_pool_grp", "w_pool_out", "w_conv_out", "w_o"], ["w_up"], ["w_down"]]
    started, token = _gather_start([[fill(k, slot_idx) for k in names] for names in gather_groups[:2]], slot_idx,
                                   "gather_start_first")
    later_lands = [[fill(k, token) for k in names] for names in gather_groups[2:]]
    wg = {}
    x_idx, y_idx = lax.axis_index("x"), lax.axis_index("y")
    at = lambda px, py, pc: 4 * px + 2 * py + pc
    near_slots = jnp.stack([at(x_idx, y_idx, c_idx), at(x_idx, y_idx, 1 - c_idx), at(1 - x_idx, y_idx, c_idx),
                            at(x_idx, 1 - y_idx, c_idx), at(1 - x_idx, y_idx, 1 - c_idx),
                            at(x_idx, 1 - y_idx, 1 - c_idx)]).astype(jnp.int32)
    far_slots = jnp.stack([at(1 - x_idx, 1 - y_idx, c_idx), at(1 - x_idx, 1 - y_idx, 1 - c_idx)]).astype(jnp.int32)

    def gather_whole(gi, after_forward, after_finish):
        send, recv, lands = started[gi]
        fs, fr, lands = _gather_forward(lands, recv, after_forward(), f"gather_forward_{gi}")
        lands = _gather_finish(lands, send, recv, [(ALL_CHIPS, fs, fr)], [(ALL_CHIPS, fs, fr)], after_finish(),
                               f"gather_finish_{gi}")
        wg.update(zip(gather_groups[gi], lands))

    near_state = {}

    def gather_near(gi, after, after_finish=None):
        send, recv, lands = started[gi]
        fs, fr, lands = _gather_forward(lands, recv, after, f"gather_forward_near_{gi}", which=NEAR)
        lands = _gather_finish(lands, send, recv, [(NEAR, fs, fr)], [], after if after_finish is None else after_finish,
                               f"gather_finish_near_{gi}", direct_sends=False)
        near_state[gi] = (fs, fr)
        return lands

    def gather_far(gi, lands, after, between=None):
        send, recv, _ = started[gi]
        fs, fr, lands = _gather_forward(lands, recv, after, f"gather_forward_far_{gi}", which=FAR)
        if between is not None:
            after = between(lands[0])
        lands = _gather_finish(lands, send, recv, [(FAR, fs, fr)], [(NEAR,) + near_state[gi], (FAR, fs, fr)], after,
                               f"gather_finish_far_{gi}", own=False)
        wg.update(zip(gather_groups[gi], lands))
        return lands

    def start_later_groups(after):
        started_rest, rest_token = _gather_start(later_lands, after, "gather_start_rest")
        started.extend(started_rest)
        return rest_token

    gather_whole(0, lambda: token, lambda: token)
    meta_full = wg["meta"].transpose(1, 0, 2).reshape(N_META, d)
    wdw_full = wg["w_dw"].transpose(1, 0, 2).reshape(CONV_TAPS_PADDED, conv_w)
    tail = lp - N_META - seq
    h0 = jnp.concatenate([meta_full, x[0], jnp.zeros((tail, d), F32)], axis=0)
    target = jnp.pad(loss_target[0], ((N_META, tail), (0, 0)))
    u1 = _rms_pre(h0, g_pre_mix, lp)
    (w_in_near,) = gather_near(1, later_lands[-1][-1], u1)
    proj = _mm_slots(u1, w_in_near, near_slots, over="n", tm=tm_half, name="mm_proj_near")
    (w_in_all,) = gather_far(1, [w_in_near], proj, between=start_later_groups)
    proj = _mm_slots(u1, w_in_all, far_slots, over="n", tm=tm_half, base=proj, name="mm_proj_far")
    conv_c = _conv_fwd(proj, wdw_full, b_dw, lp, conv_w, pool_w)
    s_act = _ln_silu(conv_c, conv_ln_g, conv_ln_b, lp)
    gather_whole(2, lambda: proj, lambda: s_act)
    wpg_full = wg["w_pool_grp"].reshape(N_DEV, n_grp, grp_rows, gdim).transpose(1, 0, 2, 3).reshape(n_grp, gdim, gdim)
    w_o_full = wg["w_o"].reshape(d, d)
    ya_pre = _pool_fwd(proj, wpg_full, pool_scale, lp)
    m_mix, y_a, y_b = _gate_mix(ya_pre, s_act, wg["w_pool_out"], wg["w_conv_out"], proj, d, "gate_mix")
    o = _mm(m_mix, w_o_full, mode="nn", tm=tm_half, tn=512, name="mm_o")
    h1, u2 = _post_mix(o, h0, g_post_mix, g_pre_mlp, lp)
    relu2 = lambda acc: jnp.square(jnp.maximum(acc, 0.0))
    (w_up_near,) = gather_near(3, u2)
    act = _mm_slots(u2, w_up_near, near_slots, over="n", tm=tm_half, out_dtype=BF16, epilogue=relu2, name="mm_up_near")
    (w_up_all,) = gather_far(3, [w_up_near], act)
    act = _mm_slots(u2, w_up_all, far_slots, over="n", tm=tm_half, out_dtype=BF16, epilogue=relu2, base=act,
                    name="mm_up_far")
    (w_down_near,) = gather_near(4, act)
    f = _mm_slots(act, w_down_near, near_slots, over="k", tm=tm_half, tn=1024, name="mm_down_near")
    (w_down_all,) = gather_far(4, [w_down_near], f)
    f = _mm_slots(act, w_down_all, far_slots, over="k", tm=tm_half, tn=1024, base=f, name="mm_down_far")
    w_down_full = w_down_all.reshape(-1, d)

    big_out = {}

    def to_sibling(names, grads, after, tag):
        send, recv, grads, lands, token = _pair_exchange_start(grads, after, "grads_to_sibling_start_" + tag)
        return (names, send, recv, grads, lands, tag), token

    where = jnp.stack([c_idx, chip_idx, 2 * (1 - x_idx) + y_idx, 2 * x_idx + (1 - y_idx),
                       2 * (1 - x_idx) + (1 - y_idx)]).astype(jnp.int32)

    def to_owner(handle, after):
        names, send, recv, grads, from_sib, tag = handle
        grads, from_sib = _pair_exchange_finish(grads, from_sib, send, recv, after, "grads_to_sibling_finish_" + tag)
        own = [g.reshape((4, 2) + g.shape[1:]) for g in grads]
        sums = [_pair_sum(o, r, where, "pair_sum_" + k) for k, o, r in zip(names, own, from_sib)]
        send, recv, sums, lands, token = _chip_exchange_start(sums, after, "grads_to_owner_start_" + tag)
        return (names, send, recv, sums, lands, own, from_sib, tag), token

    def update(handle, after):
        names, send, recv, sums, lands, own, from_sib, tag = handle
        got = _chip_exchange_finish(sums, lands, send, recv, after, "grads_to_owner_finish_" + tag)
        for k, o, s, r3 in zip(names, own, from_sib, got):
            w2 = big[k]
            shape = moments[k][0].shape
            outs = _adamw_big(w2, moments[k][0].reshape(w2.shape), moments[k][1].reshape(w2.shape), o, s, r3,
                              where, "adamw_" + k)
            big_out[k] = [a.reshape(shape) for a in outs]
        return big_out[names[-1]][0]

    df, dh, dg_post_mlp, loss_part = _loss_head(f, h1, target, g_post_mlp, lp, seq)
    d_up = _mm(df, w_down_full, mode="nt", tm=tm_half, tn=1024, out_dtypes=(BF16,), extras=[(act, 0)],
               epilogue=lambda acc, a: (acc * (2.0 * jnp.sqrt(a.astype(F32))),), name="mm_d_up")
    g_w_down = _mm(act, df, mode="tn", tm=1024, tn=1024, out_dtypes=(BF16,), name="mm_g_down")
    sib_down, token = to_sibling(["w_down"], [g_w_down.reshape(N_DEV, -1, d)], slot_idx, "down")
    g_w_up = _mm(u2, d_up, mode="tn", tm=1024, tn=wg["w_up"].shape[2], out_blocked=True, out_dtypes=(BF16,),
                 after=token, name="mm_g_up")
    sib_up, token = to_sibling(["w_up"], [g_w_up], slot_idx, "up")
    pending_down, token = to_owner(sib_down, token)
    du2 = _mm(d_up, wg["w_up"], mode="nt", tm=tm_half, tn=1024, b_blocked=True, kb=2, after=token, name="mm_du2")
    pending_up, token = to_owner(sib_up, du2)
    dh1, do, dg_pre_mlp, dg_post_mix = _mid_bwd(du2, h1, dh, o, g_pre_mlp, g_post_mix, lp)

    def gate_bwd(dm, ga, gb, ya, yb):
        sa, sb = jax.nn.sigmoid(ga), jax.nn.sigmoid(gb)
        return (dm * ya.astype(F32) * sa * (1.0 - sa), dm * yb.astype(F32) * sb * (1.0 - sb), dm * sa, dm * sb)

    gate_tn = 512
    ga_col0, gb_col0 = proj.shape[1] - 2 * d, proj.shape[1] - d
    dproj, d_gb, d_ya, d_yb = _mm(
        do, w_o_full, mode="nt", tm=tm_half, tn=gate_tn, out_dtypes=(BF16,) * 4,
        extras=[(proj, ga_col0 // gate_tn), (proj, gb_col0 // gate_tn), (y_a, 0), (y_b, 0)], epilogue=gate_bwd,
        out_places=((proj.shape[1], ga_col0), None, None, None), after=token, name="mm_dm")
    g_w_o = _mm(m_mix, do, mode="tn", tm=1024, tn=1024, out_dtypes=(BF16,), name="mm_g_o")
    bn_out = wg["w_pool_out"].shape[2]
    g_w_pool_out = _mm(ya_pre, d_ya, mode="tn", tm=pool_w, tn=bn_out, out_blocked=True, out_dtypes=(BF16,),
                       name="mm_g_pool_out")
    g_w_conv_out = _mm(s_act, d_yb, mode="tn", tm=conv_w, tn=bn_out, out_blocked=True, out_dtypes=(BF16,),
                       name="mm_g_conv_out")
    sib_mix, token = to_sibling(["w_o", "w_pool_out", "w_conv_out"],
                                [g_w_o.reshape(N_DEV, -1, d), g_w_pool_out, g_w_conv_out], slot_idx, "mix")
    d_ya_pre = _mm(d_ya, wg["w_pool_out"], mode="nt", tm=tm_half, tn=pool_w, b_blocked=True, after=token,
                   name="mm_d_ya_pre")
    d_s = _mm(d_yb, wg["w_conv_out"], mode="nt", tm=tm_half, tn=conv_w, b_blocked=True, name="mm_d_s")
    pending_mix, token = to_owner(sib_mix, d_s)
    dproj, g_wpg, d_scale = _pool_bwd(proj, d_ya_pre, wpg_full, pool_scale, dproj, lp)
    dc, d_ln_g, d_ln_b, d_b_dw = _ln_silu_bwd(conv_c, d_s, conv_ln_g, conv_ln_b, lp)
    dproj, dgc, g_wdw = _conv_bwd(proj, dc, wdw_full, dproj, lp, conv_w, pool_w)
    dproj = _place_columns(dproj, [(dgc, pool_w + conv_w), (d_gb, gb_col0)], "place_dproj")
    g_w_in = _mm(u1, dproj, mode="tn", tm=1024, tn=wg["w_in"].shape[2], out_blocked=True, out_dtypes=(BF16,),
                 after=token, name="mm_g_in")
    g_wpg_slots = g_wpg.astype(BF16).reshape(n_grp, N_DEV, grp_rows, gdim).transpose(1, 0, 2, 3)
    sib_in, token = to_sibling(["w_pool_grp", "w_in"],
                               [g_wpg_slots.reshape(N_DEV, n_grp * grp_rows, gdim), g_w_in], slot_idx, "in")
    done = update(pending_down, token)
    pending_in, token = to_owner(sib_in, done)
    done = update(pending_up, token)
    du1 = _mm(dproj, wg["w_in"], mode="nt", tm=tm_half, tn=1024, b_blocked=True, kb=2, after=done, name="mm_du1")
    dh0, dg_pre_mix = _pre_mix_bwd(du1, h0, dh1, g_pre_mix, lp)
    grad_x = dh0[N_META:N_META + seq][None]

    assert pool_w + conv_w == d and conv_w <= d and LANES <= d
    widen = lambda a: jnp.pad(a, ((0, 0), (0, d - a.shape[1])))
    packed = jnp.concatenate([
        dg_pre_mix, dg_post_mix, dg_pre_mlp, dg_post_mlp,
        jnp.concatenate([d_scale, d_ln_g], axis=1), jnp.concatenate([d_ln_b, d_b_dw], axis=1),
        dh0[:N_META], widen(g_wdw), widen(loss_part)], axis=0)
    rep = dict(g_pre_mix=((g_pre_mix, m_g_pre_mix, v_g_pre_mix), (0, 0)),
               g_post_mix=((g_post_mix, m_g_post_mix, v_g_post_mix), (SUBLANES, 0)),
               g_pre_mlp=((g_pre_mlp, m_g_pre_mlp, v_g_pre_mlp), (2 * SUBLANES, 0)),
               g_post_mlp=((g_post_mlp, m_g_post_mlp, v_g_post_mlp), (3 * SUBLANES, 0)),
               pool_scale=((pool_scale, m_pool_scale, v_pool_scale), (4 * SUBLANES, 0)),
               conv_ln_g=((conv_ln_g, m_conv_ln_g, v_conv_ln_g), (4 * SUBLANES, pool_w)),
               conv_ln_b=((conv_ln_b, m_conv_ln_b, v_conv_ln_b), (5 * SUBLANES, 0)),
               b_dw=((b_dw, m_b_dw, v_b_dw), (5 * SUBLANES, conv_w)))
    meta_row0 = 6 * SUBLANES
    wdw_row0 = meta_row0 + N_META
    loss_row0 = wdw_row0 + CONV_TAPS_PADDED
    (small_started,), token = _gather_start([[_fill_slot(packed, slot_idx, F32, slot_idx, "fill_small")]], dh0,
                                            "gather_small_start")
    done = update(pending_mix, token)
    done = update(pending_in, done)
    send, recv, lands = small_started
    fsend, frecv, lands = _gather_forward(lands, recv, done, "gather_small_forward")
    (packed_all,) = _gather_finish(lands, send, recv, [(ALL_CHIPS, fsend, frecv)], [(ALL_CHIPS, fsend, frecv)], done,
                                   "gather_small_finish")
    rep_names = list(rep)
    wdw_wmv = [jnp.pad(a[0], pad_taps) for a in (w_dw, m_w_dw, v_w_dw)]
    rep_out, meta_out, wdw_out, loss_blk = _small_update(
        slot_idx, packed_all, [rep[k][0] for k in rep_names], [rep[k][1] for k in rep_names],
        (meta, m_meta, v_meta), meta_row0, wdw_wmv, wdw_row0, loss_row0)
    small_out = dict(zip(rep_names, rep_out))
    small_out["meta"] = meta_out
    small_out["w_dw"] = [a[:CONV_KERNEL][None] for a in wdw_out]

    order = ["meta", "g_pre_mix", "w_in", "w_pool_grp", "pool_scale", "w_pool_out", "w_dw", "b_dw", "conv_ln_g",
             "conv_ln_b", "w_conv_out", "w_o", "g_post_mix", "g_pre_mlp", "w_up", "w_down", "g_post_mlp"]
    by_name = {**big_out, **small_out}
    result = [loss_blk[0, 0], grad_x]
    for kind in range(4):
        result += [by_name[k][kind] for k in order]
    return tuple(result)
```

```python
import functools

import jax
import jax.numpy as jnp
from jax import lax
from jax.experimental import pallas as pl
from jax.experimental.pallas import tpu as pltpu

F32 = jnp.float32
BF16 = jnp.bfloat16
MESH = pl.DeviceIdType.MESH

N_DEV = 8
N_META = 16
POOL_WINDOWS = (2, 4, 8, 16)
CONV_KERNEL = 31
CONV_TAPS_PADDED = 32
RMS_EPS = 1e-6
LN_EPS = 1e-5
ADAM_LR = 0.001
ADAM_B1 = 0.9
ADAM_B2 = 0.999
ADAM_EPS = 1e-08
ADAM_WD = 0.01
ADAM_STEP = 10

LANES = 128
SUBLANES = 8
ROW_CHUNK = 128
HALO = 32
VMEM_LIMIT_BYTES = 56 * 1024 * 1024


def _cparams(*sem):
    return pltpu.CompilerParams(dimension_semantics=sem if sem else None, vmem_limit_bytes=VMEM_LIMIT_BYTES)


def _round_up(n, m):
    return (n + m - 1) // m * m


def _row_tile(rows, cols, max_elems=640 * 1024):
    best = None
    for t in range(16, rows + 1, 16):
        if rows % t == 0 and (best is None or t * cols <= max_elems):
            best = t
    assert best is not None, (rows, cols)
    return best


def _rowsum8(a):
    t, w = a.shape
    return a.reshape(t // SUBLANES, SUBLANES, w).sum(axis=0)


def _mesh_pos():
    return lax.axis_index("x"), lax.axis_index("y"), lax.axis_index("c")


HBM_SPEC = pl.BlockSpec(memory_space=pltpu.HBM)
SEM_SPEC = pl.BlockSpec(memory_space=pltpu.SEMAPHORE)
ANY_SPEC = pl.BlockSpec(memory_space=pl.ANY)
_DATAFLOW = pltpu.SideEffectType.DATAFLOW_SIDE_EFFECTING


def _hbm(a):
    return pltpu.with_memory_space_constraint(a, pltpu.HBM)


def _slot(p):
    return 4 * p[0] + 2 * p[1] + p[2]


def _fill_slot(w, slot_idx, dtype, after, name):
    rows, cols = w.shape
    tr = _row_tile(rows, cols) if rows % 16 == 0 else rows

    def body(idx_ref, w_ref, _, out_ref):
        out_ref[...] = w_ref[...].astype(dtype)

    return pl.pallas_call(
        body, name=name,
        grid_spec=pltpu.PrefetchScalarGridSpec(
            num_scalar_prefetch=1, grid=(rows // tr,),
            in_specs=[pl.BlockSpec((tr, cols), lambda i, idx_ref: (i, 0)), ANY_SPEC],
            out_specs=pl.BlockSpec((None, tr, cols), lambda i, idx_ref: (idx_ref[0], i, 0))),
        out_shape=jax.ShapeDtypeStruct((N_DEV, rows, cols), dtype),
        compiler_params=_cparams("parallel"),
    )(slot_idx, w, after)


def _gather_start(groups, after, name):
    flat = [g for grp in groups for g in grp]
    n, n_grp = len(flat), len(groups)

    def body(*refs):
        lands = refs[:n]
        sems = refs[n + 1:n + 1 + 2 * n_grp]
        token = refs[-1]
        x, y, c = _mesh_pos()
        targets = [(x, y, 1 - c), (1 - x, y, c), (x, 1 - y, c), (1 - x, 1 - y, c)]
        t = 0
        for gi, grp in enumerate(groups):
            for ti in range(len(grp)):
                mine = lands[t].at[_slot((x, y, c))]
                for k, to in enumerate(targets):
                    pltpu.make_async_remote_copy(
                        src_ref=mine, dst_ref=mine,
                        send_sem=sems[2 * gi].at[4 * ti + k], recv_sem=sems[2 * gi + 1].at[4 * ti + k],
                        device_id=to, device_id_type=MESH).start()
                t += 1
        token[...] = jnp.zeros_like(token)

    sem_shapes = []
    for grp in groups:
        sem_shapes += [pltpu.SemaphoreType.DMA((4 * len(grp),))] * 2
    outs = pl.pallas_call(
        body, name=name,
        out_shape=tuple(sem_shapes + [pltpu.HBM(g.shape, g.dtype) for g in flat]
                        + [jax.ShapeDtypeStruct((SUBLANES, LANES), F32)]),
        in_specs=tuple([HBM_SPEC] * n + [ANY_SPEC]),
        out_specs=tuple([SEM_SPEC] * (2 * n_grp) + [HBM_SPEC] * n + [pl.BlockSpec(memory_space=pltpu.VMEM)]),
        input_output_aliases={i: 2 * n_grp + i for i in range(n)},
        compiler_params=pltpu.CompilerParams(has_side_effects=_DATAFLOW),
    )(*[_hbm(g) for g in flat], after)
    sems, lands, token = outs[:2 * n_grp], outs[2 * n_grp:-1], outs[-1]
    res, t = [], 0
    for gi, grp in enumerate(groups):
        res.append((sems[2 * gi], sems[2 * gi + 1], list(lands[t:t + len(grp)])))
        t += len(grp)
    return res, token


NEAR, FAR, ALL_CHIPS = (0, 1), (2,), (0, 1, 2)


def _gather_forward(lands, recv_sems, after, name, which=ALL_CHIPS):
    n, nw = len(lands), len(which)

    def body(*refs):
        land_refs, recv, _ = refs[:n], refs[n], refs[n + 1]
        fsend, frecv = refs[n + 2], refs[n + 3]
        x, y, c = _mesh_pos()
        chips = [(1 - x, y), (x, 1 - y), (1 - x, 1 - y)]
        for t in range(n):
            for i, j in enumerate(which):
                blk = land_refs[t].at[_slot((*chips[j], c))]
                pltpu.make_async_remote_copy(src_ref=blk, dst_ref=blk, send_sem=fsend.at[nw * t + i],
                                             recv_sem=recv.at[4 * t + 1 + j],
                                             device_id=(x, y, 1 - c), device_id_type=MESH).wait_recv()
                pltpu.make_async_remote_copy(src_ref=blk, dst_ref=blk, send_sem=fsend.at[nw * t + i],
                                             recv_sem=frecv.at[nw * t + i],
                                             device_id=(x, y, 1 - c), device_id_type=MESH).start()

    outs = pl.pallas_call(
        body, name=name,
        out_shape=tuple([pltpu.SemaphoreType.DMA((nw * n,))] * 2 + [pltpu.HBM(g.shape, g.dtype) for g in lands]),
        in_specs=tuple([HBM_SPEC] * n + [SEM_SPEC, ANY_SPEC]),
        out_specs=tuple([SEM_SPEC] * 2 + [HBM_SPEC] * n),
        input_output_aliases={i: 2 + i for i in range(n)},
        compiler_params=pltpu.CompilerParams(has_side_effects=_DATAFLOW),
    )(*lands, recv_sems, after)
    return outs[0], outs[1], list(outs[2:])


def _gather_finish(lands, send_sems, recv_sems, arrivals, sends, after, name, own=True, direct_sends=True):
    n = len(lands)
    fwd = list(arrivals) + list(sends)
    sem_args = [send_sems, recv_sems]
    where = []
    for _, fs, fr in fwd:
        pos = []
        for arr in (fs, fr):
            hit = [i for i, have in enumerate(sem_args) if have is arr]
            if not hit:
                sem_args.append(arr)
                hit = [len(sem_args) - 1]
            pos.append(hit[0])
        where.append(pos)

    def body(*refs):
        land_refs = refs[:n]
        send, recv = refs[n], refs[n + 1]
        fwd_refs = [refs[n + p] for pos in where for p in pos]
        x, y, c = _mesh_pos()
        sibling = (x, y, 1 - c)
        chips = [(1 - x, y), (x, 1 - y), (1 - x, 1 - y)]

        def desc(ref, s_sem, r_sem):
            return pltpu.make_async_remote_copy(src_ref=ref, dst_ref=ref, send_sem=s_sem, recv_sem=r_sem,
                                                device_id=sibling, device_id_type=MESH)

        for t in range(n):
            mine = land_refs[t].at[_slot((x, y, c))]
            if own:
                desc(land_refs[t].at[_slot(sibling)], send.at[4 * t], recv.at[4 * t]).wait_recv()
            for a, (which, _, _) in enumerate(fwd):
                fs, fr = fwd_refs[2 * a], fwd_refs[2 * a + 1]
                for i, j in enumerate(which):
                    if a < len(arrivals):
                        desc(land_refs[t].at[_slot((*chips[j], 1 - c))], fs.at[len(which) * t + i],
                             fr.at[len(which) * t + i]).wait_recv()
                    else:
                        desc(land_refs[t].at[_slot((*chips[j], c))], fs.at[len(which) * t + i],
                             fr.at[len(which) * t + i]).wait_send()
            if direct_sends:
                for k in range(4):
                    desc(mine, send.at[4 * t + k], recv.at[4 * t + k]).wait_send()

    outs = pl.pallas_call(
        body, name=name,
        out_shape=tuple(pltpu.HBM(g.shape, g.dtype) for g in lands),
        in_specs=tuple([HBM_SPEC] * n + [SEM_SPEC] * len(sem_args) + [ANY_SPEC]),
        out_specs=tuple([HBM_SPEC] * n),
        input_output_aliases={i: i for i in range(n)},
        compiler_params=pltpu.CompilerParams(has_side_effects=_DATAFLOW),
    )(*lands, *sem_args, after)
    return list(outs)


def _pair_exchange_start(grads, after, name):
    n = len(grads)

    def body(*refs):
        ins, lands = refs[:n], refs[n:2 * n]
        send, recv = refs[2 * n + 1], refs[2 * n + 2]
        token = refs[-1]
        x, y, c = _mesh_pos()
        for t in range(n):
            for q in range(4):
                pltpu.make_async_remote_copy(
                    src_ref=ins[t].at[2 * q + 1 - c], dst_ref=lands[t].at[q],
                    send_sem=send.at[4 * t + q], recv_sem=recv.at[4 * t + q],
                    device_id=(x, y, 1 - c), device_id_type=MESH).start()
        token[...] = jnp.zeros_like(token)

    land_shapes = [(4,) + g.shape[1:] for g in grads]
    outs = pl.pallas_call(
        body, name=name,
        out_shape=tuple([pltpu.SemaphoreType.DMA((4 * n,))] * 2 + [pltpu.HBM(g.shape, g.dtype) for g in grads]
                        + [pltpu.HBM(ls, g.dtype) for ls, g in zip(land_shapes, grads)]
                        + [jax.ShapeDtypeStruct((SUBLANES, LANES), F32)]),
        in_specs=tuple([HBM_SPEC] * (2 * n) + [ANY_SPEC]),
        out_specs=tuple([SEM_SPEC] * 2 + [HBM_SPEC] * (2 * n) + [pl.BlockSpec(memory_space=pltpu.VMEM)]),
        input_output_aliases={i: 2 + i for i in range(2 * n)},
        compiler_params=pltpu.CompilerParams(has_side_effects=_DATAFLOW),
    )(*[_hbm(g) for g in grads], *[_hbm(lax.empty(ls, g.dtype)) for ls, g in zip(land_shapes, grads)], after)
    return outs[0], outs[1], list(outs[2:2 + n]), list(outs[2 + n:2 + 2 * n]), outs[-1]


def _pair_exchange_finish(grads, lands, send_sems, recv_sems, after, name):
    n = len(grads)

    def body(*refs):
        ins, land_refs = refs[:n], refs[n:2 * n]
        send, recv = refs[2 * n], refs[2 * n + 1]
        x, y, c = _mesh_pos()
        for t in range(n):
            for q in range(4):
                cp = pltpu.make_async_remote_copy(
                    src_ref=ins[t].at[q], dst_ref=land_refs[t].at[q], send_sem=send.at[4 * t + q],
                    recv_sem=recv.at[4 * t + q], device_id=(x, y, 1 - c), device_id_type=MESH)
                cp.wait_send()
                cp.wait_recv()

    outs = pl.pallas_call(
        body, name=name,
        out_shape=tuple([pltpu.HBM(g.shape, g.dtype) for g in grads] + [pltpu.HBM(g.shape, g.dtype) for g in lands]),
        in_specs=tuple([HBM_SPEC] * (2 * n) + [SEM_SPEC] * 2 + [ANY_SPEC]),
        out_specs=tuple([HBM_SPEC] * (2 * n)),
        input_output_aliases={i: i for i in range(2 * n)},
        compiler_params=pltpu.CompilerParams(has_side_effects=_DATAFLOW),
    )(*grads, *lands, send_sems, recv_sems, after)
    return list(outs[:n]), list(outs[n:])


def _chip_exchange_start(sums, after, name):
    n = len(sums)

    def body(*refs):
        ins, lands = refs[:n], refs[n:2 * n]
        send, recv = refs[2 * n + 1], refs[2 * n + 2]
        token = refs[-1]
        x, y, c = _mesh_pos()
        chips = [(1 - x, y), (x, 1 - y), (1 - x, 1 - y)]
        for t in range(n):
            for r, chip in enumerate(chips):
                pltpu.make_async_remote_copy(
                    src_ref=ins[t].at[2 * chip[0] + chip[1]], dst_ref=lands[t].at[r],
                    send_sem=send.at[3 * t + r], recv_sem=recv.at[3 * t + r],
                    device_id=(*chip, c), device_id_type=MESH).start()
        token[...] = jnp.zeros_like(token)

    land_shapes = [(3,) + s.shape[1:] for s in sums]
    outs = pl.pallas_call(
        body, name=name,
        out_shape=tuple([pltpu.SemaphoreType.DMA((3 * n,))] * 2 + [pltpu.HBM(s.shape, s.dtype) for s in sums]
                        + [pltpu.HBM(ls, s.dtype) for ls, s in zip(land_shapes, sums)]
                        + [jax.ShapeDtypeStruct((SUBLANES, LANES), F32)]),
        in_specs=tuple([HBM_SPEC] * (2 * n) + [ANY_SPEC]),
        out_specs=tuple([SEM_SPEC] * 2 + [HBM_SPEC] * (2 * n) + [pl.BlockSpec(memory_space=pltpu.VMEM)]),
        input_output_aliases={i: 2 + i for i in range(2 * n)},
        compiler_params=pltpu.CompilerParams(has_side_effects=_DATAFLOW),
    )(*[_hbm(s) for s in sums], *[_hbm(lax.empty(ls, s.dtype)) for ls, s in zip(land_shapes, sums)], after)
    return outs[0], outs[1], list(outs[2:2 + n]), list(outs[2 + n:2 + 2 * n]), outs[-1]


def _chip_exchange_finish(sums, lands, send_sems, recv_sems, after, name):
    n = len(sums)

    def body(*refs):
        ins, land_refs = refs[:n], refs[n:2 * n]
        send, recv = refs[2 * n], refs[2 * n + 1]
        x, y, c = _mesh_pos()
        for t in range(n):
            for r in range(3):
                cp = pltpu.make_async_remote_copy(
                    src_ref=ins[t].at[r], dst_ref=land_refs[t].at[r], send_sem=send.at[3 * t + r],
                    recv_sem=recv.at[3 * t + r],
                    device_id=(x, y, 1 - c), device_id_type=MESH)
                cp.wait_send()
                cp.wait_recv()

    outs = pl.pallas_call(
        body, name=name,
        out_shape=tuple(pltpu.HBM(g.shape, g.dtype) for g in lands),
        in_specs=tuple([HBM_SPEC] * (2 * n) + [SEM_SPEC] * 2 + [ANY_SPEC]),
        out_specs=tuple([HBM_SPEC] * n),
        input_output_aliases={n + i: i for i in range(n)},
        compiler_params=pltpu.CompilerParams(has_side_effects=_DATAFLOW),
    )(*sums, *lands, send_sems, recv_sems, after)
    return list(outs)


def _mm(a, b, *, mode, tm, tn, tk=None, b_blocked=False, out_blocked=False, out_dtypes=(F32,),
        epilogue=None, extras=(), after=None, kb=1, out_places=None, name):
    if mode == "nn":
        m, k = a.shape
        n = b.shape[0] * b.shape[2] if b_blocked else b.shape[1]
        dims = (((1,), (0,)), ((), ()))
    elif mode == "nt":
        m, k = a.shape
        n = b.shape[1] if b_blocked else b.shape[0]
        if b_blocked:
            tk = kb * b.shape[2]
        dims = (((1,), (1,)), ((), ()))
    else:
        k, m = a.shape
        n = b.shape[1]
        dims = (((0,), (0,)), ((), ()))
    tk = k if tk is None else tk
    assert m % tm == 0 and n % tn == 0 and k % tk == 0, (name, m, n, k, tm, tn, tk)
    gm, gn, gk = m // tm, n // tn, k // tk
    if b_blocked:
        assert (tn if mode == "nn" else tk) == kb * b.shape[2], name

    if mode == "nn":
        a_spec = pl.BlockSpec((tm, tk), lambda i, j, kk: (i, kk))
        b_spec = (pl.BlockSpec((None, tk, tn), lambda i, j, kk: (j, kk, 0)) if b_blocked
                  else pl.BlockSpec((tk, tn), lambda i, j, kk: (kk, j)))
    elif mode == "nt":
        a_spec = pl.BlockSpec((tm, tk), lambda i, j, kk: (i, kk))
        b_spec = (pl.BlockSpec((kb, tn, tk // kb), lambda i, j, kk: (kk, j, 0)) if b_blocked
                  else pl.BlockSpec((tn, tk), lambda i, j, kk: (j, kk)))
    else:
        a_spec = pl.BlockSpec((tk, tm), lambda i, j, kk: (kk, i))
        b_spec = pl.BlockSpec((tk, tn), lambda i, j, kk: (kk, j))
    if out_blocked:
        out_spec = pl.BlockSpec((None, tm, tn), lambda i, j, kk: (j, i, 0))
        out_shape = (gn, m, tn)
    else:
        out_spec = pl.BlockSpec((tm, tn), lambda i, j, kk: (i, j))
        out_shape = (m, n)
    extra_specs = [pl.BlockSpec((tm, tn), functools.partial(lambda i, j, kk, off: (i, j + off), off=off))
                   for _, off in extras]
    n_extra, n_out = len(extras), len(out_dtypes)
    n_after = 0 if after is None else 1
    places = out_places if out_places is not None else (None,) * n_out

    def body(a_ref, b_ref, *rest):
        extra_refs = rest[:n_extra]
        out_refs = rest[n_extra + n_after:n_extra + n_after + n_out]

        def finish(acc):
            if epilogue is None:
                res = (acc,)
            else:
                res = epilogue(acc, *[e[...] for e in extra_refs])
            for o_ref, r in zip(out_refs, res):
                o_ref[...] = r.astype(o_ref.dtype)

        if mode == "nt" and b_blocked:
            bk = tk // kb
            part = lax.dot_general(a_ref[:, :bk], b_ref[0], dims, preferred_element_type=F32)
            for h in range(1, kb):
                part = part + lax.dot_general(a_ref[:, h * bk:(h + 1) * bk], b_ref[h], dims,
                                              preferred_element_type=F32)
        else:
            part = lax.dot_general(a_ref[...], b_ref[...], dims, preferred_element_type=F32)
        if gk == 1:
            finish(part)
        else:
            acc_ref = rest[-1]
            kk = pl.program_id(2)

            @pl.when(kk == 0)
            def _():
                acc_ref[...] = part

            @pl.when(kk > 0)
            def _():
                acc_ref[...] += part

            @pl.when(kk == gk - 1)
            def _():
                finish(acc_ref[...])

    outs = pl.pallas_call(
        body, name=name, grid=(gm, gn, gk),
        in_specs=[a_spec, b_spec] + extra_specs + [ANY_SPEC] * n_after,
        out_specs=[out_spec if place is None else
                   pl.BlockSpec((tm, tn), functools.partial(lambda i, j, kk, off: (i, j + off), off=place[1] // tn))
                   for place in places],
        out_shape=[jax.ShapeDtypeStruct(out_shape if place is None else (m, place[0]), dt)
                   for dt, place in zip(out_dtypes, places)],
        scratch_shapes=[pltpu.VMEM((tm, tn), F32)] if gk > 1 else [],
        compiler_params=_cparams("parallel", "parallel", "arbitrary"),
    )(a, b, *[e for e, _ in extras], *([] if after is None else [after]))
    return outs[0] if n_out == 1 else outs


def _mm_slots(a, w, slots, *, over, tm, tn=None, out_dtype=F32, epilogue=None, base=None, name):
    m = a.shape[0]
    ns = slots.shape[0]
    n_slots, w1, w2 = w.shape
    assert m % tm == 0
    if over == "n":
        k, bn = w1, w2

        def body(slots_ref, a_ref, w_ref, *rest):
            out_ref = rest[-1]
            acc = jnp.dot(a_ref[...], w_ref[...], preferred_element_type=F32)
            out_ref[...] = (acc if epilogue is None else epilogue(acc)).astype(out_ref.dtype)

        in_specs = [pl.BlockSpec((tm, k), lambda i, j, s: (i, 0)),
                    pl.BlockSpec((None, k, bn), lambda i, j, s: (s[j], 0, 0))]
        args = [a, w]
        aliases = {}
        if base is not None:
            in_specs.append(ANY_SPEC)
            args.append(base)
            aliases = {3: 0}
        return pl.pallas_call(
            body, name=name,
            grid_spec=pltpu.PrefetchScalarGridSpec(
                num_scalar_prefetch=1, grid=(m // tm, ns), in_specs=in_specs,
                out_specs=pl.BlockSpec((tm, bn), lambda i, j, s: (i, s[j]))),
            out_shape=jax.ShapeDtypeStruct((m, n_slots * bn), out_dtype),
            input_output_aliases=aliases,
            compiler_params=_cparams("parallel", "arbitrary"),
        )(slots, *args)

    bk, n = w1, w2
    tn = n if tn is None else tn
    assert n % tn == 0

    def body(slots_ref, a_ref, w_ref, *rest):
        out_ref, acc_ref = rest[-2], rest[-1]
        kk = pl.program_id(2)
        part = jnp.dot(a_ref[...], w_ref[...], preferred_element_type=F32)

        @pl.when(kk == 0)
        def _():
            acc_ref[...] = part if base is None else part + rest[0][...]

        @pl.when(kk > 0)
        def _():
            acc_ref[...] += part

        @pl.when(kk == ns - 1)
        def _():
            out_ref[...] = acc_ref[...].astype(out_ref.dtype)

    in_specs = [pl.BlockSpec((tm, bk), lambda i, j, kk, s: (i, s[kk])),
                pl.BlockSpec((None, bk, tn), lambda i, j, kk, s: (s[kk], 0, j))]
    args = [a, w]
    if base is not None:
        in_specs.append(pl.BlockSpec((tm, tn), lambda i, j, kk, s: (i, j)))
        args.append(base)
    return pl.pallas_call(
        body, name=name,
        grid_spec=pltpu.PrefetchScalarGridSpec(
            num_scalar_prefetch=1, grid=(m // tm, n // tn, ns), in_specs=in_specs,
            out_specs=pl.BlockSpec((tm, tn), lambda i, j, kk, s: (i, j)),
            scratch_shapes=[pltpu.VMEM((tm, tn), F32)]),
        out_shape=jax.ShapeDtypeStruct((m, n), out_dtype),
        compiler_params=_cparams("parallel", "parallel", "arbitrary"),
    )(slots, *args)


def _gate_mix(ya_pre, s, wpo, wco, proj, d_model, name):
    lp, width = ya_pre.shape
    nb, _, bn = wpo.shape
    ga_off = (proj.shape[1] - 2 * d_model) // bn
    gb_off = (proj.shape[1] - d_model) // bn

    def body(ya_ref, s_ref, wpo_ref, wco_ref, ga_ref, gb_ref, m_ref, y_a_ref, y_b_ref):
        y_a = jnp.dot(ya_ref[...], wpo_ref[...], preferred_element_type=F32)
        y_b = jnp.dot(s_ref[...], wco_ref[...], preferred_element_type=F32)
        m = jax.nn.sigmoid(ga_ref[...]) * y_a + jax.nn.sigmoid(gb_ref[...]) * y_b
        m_ref[...] = m.astype(BF16)
        y_a_ref[...] = y_a.astype(BF16)
        y_b_ref[...] = y_b.astype(BF16)

    act_spec = pl.BlockSpec((lp, width), lambda j: (0, 0))
    w_spec = pl.BlockSpec((None, width, bn), lambda j: (j, 0, 0))
    out_spec = pl.BlockSpec((lp, bn), lambda j: (0, j))
    return pl.pallas_call(
        body, name=name, grid=(nb,),
        in_specs=[act_spec, act_spec, w_spec, w_spec,
                  pl.BlockSpec((lp, bn), lambda j: (0, j + ga_off)),
                  pl.BlockSpec((lp, bn), lambda j: (0, j + gb_off))],
        out_specs=[out_spec] * 3,
        out_shape=[jax.ShapeDtypeStruct((lp, nb * bn), BF16)] * 3,
        compiler_params=_cparams("parallel"),
    )(ya_pre, s, wpo, wco, proj, proj)


def _rms_stats(x):
    return lax.rsqrt(jnp.mean(x * x, axis=-1, keepdims=True) + RMS_EPS)


def _rms_bwd(x, g, dy):
    r = _rms_stats(x)
    nrm = x * r
    dn = dy * g
    dx = r * (dn - nrm * jnp.mean(dn * nrm, axis=-1, keepdims=True))
    return dx, dy * nrm


def _rowwise(body, ins, outs, accs, *, lp, name):
    tr = _row_tile(lp, max(a.shape[1] for a in ins))
    n_in, n_out, n_acc = len(ins), len(outs), len(accs)

    def kernel_body(*refs):
        i = pl.program_id(0)
        acc_refs = refs[n_in + n_out:]

        @pl.when(i == 0)
        def _():
            for r in acc_refs:
                r[...] = jnp.zeros_like(r)

        body(i * tr, refs[:n_in], refs[n_in:n_in + n_out], acc_refs)

    in_specs = []
    for a in ins:
        if a.shape[0] == lp:
            in_specs.append(pl.BlockSpec((tr, a.shape[1]), lambda i: (i, 0)))
        else:
            in_specs.append(pl.BlockSpec(a.shape, lambda i: (0, 0)))
    out_specs = [pl.BlockSpec((tr, w), lambda i: (i, 0)) for w, _ in outs]
    out_specs += [pl.BlockSpec((SUBLANES, w), lambda i: (0, 0)) for w in accs]
    out_shape = [jax.ShapeDtypeStruct((lp, w), d) for w, d in outs]
    out_shape += [jax.ShapeDtypeStruct((SUBLANES, w), F32) for w in accs]
    return pl.pallas_call(
        kernel_body, name=name, grid=(lp // tr,), in_specs=in_specs, out_specs=out_specs,
        out_shape=out_shape, compiler_params=_cparams("arbitrary"),
    )(*ins)


SHIFT_TILE = 128


def _shifted_specs(width, n_big, n_small):
    per = SHIFT_TILE // N_META
    small = pl.BlockSpec((N_META, width), lambda i: (jnp.clip(per * i - 1, 0, n_small - 1), 0))
    big = pl.BlockSpec((SHIFT_TILE, width), lambda i: (jnp.minimum(i, n_big - 1), 0))
    return small, big


def _rms_pre(meta_full, x2, g, lp, seq):
    d = x2.shape[1]
    assert seq % SHIFT_TILE == 0 and lp % SHIFT_TILE == 0 and SHIFT_TILE % N_META == 0

    def body(meta_ref, xs_ref, xb_ref, g_ref, h0_ref, u1_ref):
        i = pl.program_id(0)
        head = jnp.where(i == 0, meta_ref[...], xs_ref[...])
        rows = jnp.concatenate([head, xb_ref[:SHIFT_TILE - N_META, :]], axis=0)
        r = i * SHIFT_TILE + lax.broadcasted_iota(jnp.int32, (SHIFT_TILE, 1), 0)
        rows = jnp.where(r < N_META + seq, rows, 0.0)
        h0_ref[...] = rows
        u1_ref[...] = (rows * _rms_stats(rows) * g_ref[...]).astype(BF16)

    small, big = _shifted_specs(d, seq // SHIFT_TILE, seq // N_META)
    tile = pl.BlockSpec((SHIFT_TILE, d), lambda i: (i, 0))
    return pl.pallas_call(
        body, name="rms_pre", grid=(lp // SHIFT_TILE,),
        in_specs=[pl.BlockSpec((N_META, d), lambda i: (0, 0)), small, big, pl.BlockSpec((1, d), lambda i: (0, 0))],
        out_specs=[tile, tile],
        out_shape=[jax.ShapeDtypeStruct((lp, d), F32), jax.ShapeDtypeStruct((lp, d), BF16)],
        compiler_params=_cparams("parallel"),
    )(meta_full, x2, x2, g)


def _post_mix(o, h0, g_post_mix, g_pre_mlp, lp):
    d = h0.shape[1]

    def body(row0, ins, outs, accs):
        o_ref, h0_ref, g1_ref, g2_ref = ins
        o_v = o_ref[...]
        h1 = h0_ref[...] + o_v * _rms_stats(o_v) * g1_ref[...]
        outs[0][...] = h1
        outs[1][...] = (h1 * _rms_stats(h1) * g2_ref[...]).astype(BF16)

    return _rowwise(body, [o, h0, g_post_mix, g_pre_mlp], [(d, F32), (d, BF16)], [], lp=lp, name="post_mix")


def _loss_head(f, h1, target, g_post_mlp, lp, seq):
    d = f.shape[1]

    def body(f_ref, h1_ref, ts_ref, tb_ref, g_ref, df_ref, dh_ref, dg_ref, loss_ref):
        i = pl.program_id(0)

        @pl.when(i == 0)
        def _():
            dg_ref[...] = jnp.zeros_like(dg_ref)
            loss_ref[...] = jnp.zeros_like(loss_ref)

        f_v, g = f_ref[...], g_ref[...]
        r = _rms_stats(f_v)
        nrm = f_v * r
        rows = i * SHIFT_TILE + lax.broadcasted_iota(jnp.int32, (SHIFT_TILE, 1), 0)
        valid = (rows >= N_META) & (rows < N_META + seq)
        tgt = jnp.concatenate([ts_ref[...], tb_ref[:SHIFT_TILE - N_META, :]], axis=0)
        err = jnp.where(valid, h1_ref[...] + nrm * g - tgt, 0.0)
        loss_ref[...] += 0.5 * jnp.sum(jnp.mean(err * err, axis=-1, keepdims=True))
        dy = err * (1.0 / d)
        dn = dy * g
        df_ref[...] = (r * (dn - nrm * jnp.mean(dn * nrm, axis=-1, keepdims=True))).astype(BF16)
        dh_ref[...] = dy
        dg_ref[...] += _rowsum8(dy * nrm)

    small, big = _shifted_specs(d, seq // SHIFT_TILE, seq // N_META)
    tile = pl.BlockSpec((SHIFT_TILE, d), lambda i: (i, 0))
    return pl.pallas_call(
        body, name="loss_head", grid=(lp // SHIFT_TILE,),
        in_specs=[tile, tile, small, big, pl.BlockSpec((1, d), lambda i: (0, 0))],
        out_specs=[tile, tile, pl.BlockSpec((SUBLANES, d), lambda i: (0, 0)),
                   pl.BlockSpec((SUBLANES, LANES), lambda i: (0, 0))],
        out_shape=[jax.ShapeDtypeStruct((lp, d), BF16), jax.ShapeDtypeStruct((lp, d), F32),
                   jax.ShapeDtypeStruct((SUBLANES, d), F32), jax.ShapeDtypeStruct((SUBLANES, LANES), F32)],
        compiler_params=_cparams("arbitrary"),
    )(f, h1, target, target, g_post_mlp)


def _mid_bwd(du2, h1, dh, o, g_pre_mlp, g_post_mix, lp):
    d = h1.shape[1]

    def body(row0, ins, outs, accs):
        du2_ref, h1_ref, dh_ref, o_ref, g2_ref, g1_ref = ins
        dx2, dg2 = _rms_bwd(h1_ref[...], g2_ref[...], du2_ref[...])
        dh1 = dh_ref[...] + dx2
        do, dg1 = _rms_bwd(o_ref[...], g1_ref[...], dh1)
        outs[0][...] = dh1
        outs[1][...] = do.astype(BF16)
        accs[0][...] += _rowsum8(dg2)
        accs[1][...] += _rowsum8(dg1)

    return _rowwise(body, [du2, h1, dh, o, g_pre_mlp, g_post_mix], [(d, F32), (d, BF16)], [d, d], lp=lp,
                    name="mid_bwd")


def _pre_mix_bwd(du1, h0, dh1, g_pre_mix, seq):
    d = h0.shape[1]
    per = SHIFT_TILE // N_META
    assert seq % SHIFT_TILE == 0

    def body(du_b, h_b, dh_b, du_n, h_n, dh_n, du_m, h_m, dh_m, g_ref, gx_ref, gm_ref, dg_ref):
        i = pl.program_id(0)
        g = g_ref[...]

        @pl.when(i == 0)
        def _():
            dx, dg = _rms_bwd(h_m[...], g, du_m[...])
            gm_ref[...] = dh_m[...] + dx
            dg_ref[...] = _rowsum8(dg)

        rows = lambda big, nxt: jnp.concatenate([big[N_META:, :], nxt[...]], axis=0)
        dx, dg = _rms_bwd(rows(h_b, h_n), g, rows(du_b, du_n))
        gx_ref[...] = rows(dh_b, dh_n) + dx
        dg_ref[...] += _rowsum8(dg)

    big = pl.BlockSpec((SHIFT_TILE, d), lambda i: (i, 0))
    nxt = pl.BlockSpec((N_META, d), lambda i: (per * (i + 1), 0))
    first = pl.BlockSpec((N_META, d), lambda i: (0, 0))
    return pl.pallas_call(
        body, name="pre_mix_bwd", grid=(seq // SHIFT_TILE,),
        in_specs=[big] * 3 + [nxt] * 3 + [first] * 3 + [pl.BlockSpec((1, d), lambda i: (0, 0))],
        out_specs=[big, first, pl.BlockSpec((SUBLANES, d), lambda i: (0, 0))],
        out_shape=[jax.ShapeDtypeStruct((seq, d), F32), jax.ShapeDtypeStruct((N_META, d), F32),
                   jax.ShapeDtypeStruct((SUBLANES, d), F32)],
        compiler_params=_cparams("arbitrary"),
    )(du1, h0, dh1, du1, h0, dh1, du1, h0, dh1, g_pre_mix)


def _ln_stats(c):
    mu = jnp.mean(c, axis=-1, keepdims=True)
    var = jnp.mean(jnp.square(c - mu), axis=-1, keepdims=True)
    return mu, lax.rsqrt(var + LN_EPS)


def _ln_silu(c, ln_g, ln_b, lp):
    w = c.shape[1]

    def body(row0, ins, outs, accs):
        c_ref, g_ref, b_ref = ins
        c_v = c_ref[...]
        mu, rstd = _ln_stats(c_v)
        ln = (c_v - mu) * rstd * g_ref[...] + b_ref[...]
        outs[0][...] = (ln * jax.nn.sigmoid(ln)).astype(BF16)

    return _rowwise(body, [c, ln_g, ln_b], [(w, BF16)], [], lp=lp, name="ln_silu")[0]


def _ln_silu_bwd(c, ds, ln_g, ln_b, lp):
    w = c.shape[1]

    def body(row0, ins, outs, accs):
        c_ref, ds_ref, g_ref, b_ref = ins
        c_v, g = c_ref[...], g_ref[...]
        mu, rstd = _ln_stats(c_v)
        nrm = (c_v - mu) * rstd
        ln = nrm * g + b_ref[...]
        sig = jax.nn.sigmoid(ln)
        dln = ds_ref[...] * (sig * (1.0 + ln * (1.0 - sig)))
        dn = dln * g
        dc = rstd * (dn - jnp.mean(dn, axis=-1, keepdims=True) - nrm * jnp.mean(dn * nrm, axis=-1, keepdims=True))
        outs[0][...] = dc
        accs[0][...] += _rowsum8(dln * nrm)
        accs[1][...] += _rowsum8(dln)
        accs[2][...] += _rowsum8(dc)

    return _rowwise(body, [c, ds, ln_g, ln_b], [(w, F32)], [w, w, w], lp=lp, name="ln_silu_bwd")


def _chunk_with_history(ref, i, cols=slice(None)):
    t0 = pl.multiple_of(i * ROW_CHUNK, ROW_CHUNK)
    lo0 = pl.multiple_of(jnp.maximum(t0 - HALO, 0), SUBLANES)
    lo = jnp.where(i > 0, ref[pl.ds(lo0, HALO), cols], 0.0)
    return jnp.concatenate([lo, ref[pl.ds(t0, ROW_CHUNK), cols]], axis=0)


def _chunk_with_future(ref, i, n_chunks, cols=slice(None)):
    t0 = pl.multiple_of(i * ROW_CHUNK, ROW_CHUNK)
    hi0 = pl.multiple_of(jnp.minimum(t0 + ROW_CHUNK, (n_chunks - 1) * ROW_CHUNK), SUBLANES)
    hi = jnp.where(i < n_chunks - 1, ref[pl.ds(hi0, HALO), cols], 0.0)
    return jnp.concatenate([ref[pl.ds(t0, ROW_CHUNK), cols], hi], axis=0)


def _inv_count(t0, n_rows, window):
    pos = t0 + lax.broadcasted_iota(jnp.int32, (n_rows, 1), 0)
    return 1.0 / jnp.minimum(pos + 1, window).astype(F32)


def _pool_delta(z_hist, t0, window):
    s = z_hist
    sh = 1
    while sh < window:
        s = s + pltpu.roll(s, sh, 0)
        sh *= 2
    cur = z_hist[HALO:, :]
    return s[HALO:, :] * _inv_count(t0, ROW_CHUNK, window) - cur


def _pool_fwd(proj, wpg, pool_scale, lp):
    n_grp, gdim, _ = wpg.shape
    width = n_grp * gdim
    n_chunks = lp // ROW_CHUNK

    def body(z_ref, w_ref, sc_ref, out_ref):
        for g, window in enumerate(POOL_WINDOWS):
            cols = slice(g * gdim, (g + 1) * gdim)

            def chunk(i, carry, cols=cols, g=g, window=window):
                t0 = pl.multiple_of(i * ROW_CHUNK, ROW_CHUNK)
                d = _pool_delta(_chunk_with_history(z_ref, i, cols), t0, window)
                q = jnp.dot(d.astype(BF16), w_ref[g], preferred_element_type=F32)
                out_ref[pl.ds(t0, ROW_CHUNK), cols] = (q * sc_ref[:, cols]).astype(BF16)
                return carry

            lax.fori_loop(0, n_chunks, chunk, 0)

    return pl.pallas_call(
        body, name="pool_fwd", grid=(1,),
        in_specs=[pl.BlockSpec((lp, width), lambda i: (0, 0)),
                  pl.BlockSpec(wpg.shape, lambda i: (0, 0, 0)),
                  pl.BlockSpec(pool_scale.shape, lambda i: (0, 0))],
        out_specs=pl.BlockSpec((lp, width), lambda i: (0, 0)),
        out_shape=jax.ShapeDtypeStruct((lp, width), BF16),
        compiler_params=_cparams("arbitrary"),
    )(proj, wpg, pool_scale)


def _pool_bwd(proj, d_ya, wpg, pool_scale, dproj, lp):
    n_grp, gdim, _ = wpg.shape
    width = n_grp * gdim
    n_chunks = lp // ROW_CHUNK
    ext = ROW_CHUNK + HALO

    def body(z_ref, dya_ref, w_ref, sc_ref, _, dz_ref, dw_ref, dsc_ref):
        dw_ref[...] = jnp.zeros_like(dw_ref)
        dsc_ref[...] = jnp.zeros_like(dsc_ref)
        for g, window in enumerate(POOL_WINDOWS):
            cols = slice(g * gdim, (g + 1) * gdim)

            def chunk(i, carry, cols=cols, g=g, window=window):
                t0 = pl.multiple_of(i * ROW_CHUNK, ROW_CHUNK)
                w_g = w_ref[g]
                scale = sc_ref[:, cols]
                d = _pool_delta(_chunk_with_history(z_ref, i, cols), t0, window).astype(BF16)
                dya_ext = _chunk_with_future(dya_ref, i, n_chunks, cols)
                dya = dya_ext[:ROW_CHUNK, :]
                q = jnp.dot(d, w_g, preferred_element_type=F32)
                dsc_ref[:, cols] += _rowsum8(dya * q)
                e_ext = (dya_ext * scale).astype(BF16)
                dw_ref[g] += lax.dot_general(d, e_ext[:ROW_CHUNK, :], (((0,), (0,)), ((), ())),
                                             preferred_element_type=F32)
                dd_ext = lax.dot_general(e_ext, w_g, (((1,), (1,)), ((), ())), preferred_element_type=F32)
                s = dd_ext * _inv_count(t0, ext, window)
                sh = 1
                while sh < window:
                    s = s + pltpu.roll(s, ext - sh, 0)
                    sh *= 2
                dz_ref[pl.ds(t0, ROW_CHUNK), cols] = (s[:ROW_CHUNK, :] - dd_ext[:ROW_CHUNK, :]).astype(BF16)
                return carry

            lax.fori_loop(0, n_chunks, chunk, 0)

    blk = pl.BlockSpec((lp, width), lambda i: (0, 0))
    return pl.pallas_call(
        body, name="pool_bwd", grid=(1,),
        in_specs=[blk, blk, pl.BlockSpec(wpg.shape, lambda i: (0, 0, 0)),
                  pl.BlockSpec(pool_scale.shape, lambda i: (0, 0)), ANY_SPEC],
        out_specs=[blk, pl.BlockSpec(wpg.shape, lambda i: (0, 0, 0)),
                   pl.BlockSpec((SUBLANES, width), lambda i: (0, 0))],
        out_shape=[jax.ShapeDtypeStruct(dproj.shape, BF16), jax.ShapeDtypeStruct(wpg.shape, F32),
                   jax.ShapeDtypeStruct((SUBLANES, width), F32)],
        input_output_aliases={4: 0},
        compiler_params=_cparams("arbitrary"),
    )(proj, d_ya, wpg, pool_scale, dproj)


def _conv_fwd(proj, w_dw, b_dw, lp, width, v_col0):
    n_chunks = lp // ROW_CHUNK
    v_blk0, g_blk0 = v_col0 // LANES, (v_col0 + width) // LANES

    def body(v_ref, gc_ref, w_ref, b_ref, c_ref, a_pad):
        a_pad[pl.ds(0, HALO), :] = jnp.zeros((HALO, LANES), F32)
        a_pad[pl.ds(HALO, lp), :] = v_ref[...] * jax.nn.sigmoid(gc_ref[...])

        def chunk(i, carry):
            t0 = pl.multiple_of(i * ROW_CHUNK, ROW_CHUNK)
            hist = a_pad[pl.ds(t0, ROW_CHUNK + HALO), :]
            acc = jnp.zeros((ROW_CHUNK, LANES), F32)
            for k in range(CONV_KERNEL):
                acc = acc + w_ref[k:k + 1, :] * pltpu.roll(hist, CONV_KERNEL - 1 - k, 0)[HALO:, :]
            c_ref[pl.ds(t0, ROW_CHUNK), :] = acc + b_ref[...]
            return carry

        lax.fori_loop(0, n_chunks, chunk, 0)

    return pl.pallas_call(
        body, name="conv_fwd", grid=(width // LANES,),
        in_specs=[pl.BlockSpec((lp, LANES), lambda j: (0, j + v_blk0)),
                  pl.BlockSpec((lp, LANES), lambda j: (0, j + g_blk0)),
                  pl.BlockSpec((CONV_TAPS_PADDED, LANES), lambda j: (0, j)),
                  pl.BlockSpec((1, LANES), lambda j: (0, j))],
        out_specs=pl.BlockSpec((lp, LANES), lambda j: (0, j)),
        out_shape=jax.ShapeDtypeStruct((lp, width), F32),
        scratch_shapes=[pltpu.VMEM((lp + HALO, LANES), F32)],
        compiler_params=_cparams("parallel"),
    )(proj, proj, w_dw, b_dw)


def _conv_bwd(proj, dc, w_dw, dproj, lp, width, v_col0):
    n_chunks = lp // ROW_CHUNK
    ext = ROW_CHUNK + HALO
    v_blk0, g_blk0 = v_col0 // LANES, (v_col0 + width) // LANES

    def body(v_ref, gc_ref, dc_ref, w_ref, _, dv_ref, dgc_ref, dw_ref, a_pad, dc_pad, dw_acc):
        sig = jax.nn.sigmoid(gc_ref[...])
        a_pad[pl.ds(0, HALO), :] = jnp.zeros((HALO, LANES), F32)
        a_pad[pl.ds(HALO, lp), :] = v_ref[...] * sig
        dc_pad[pl.ds(0, lp), :] = dc_ref[...]
        dc_pad[pl.ds(lp, HALO), :] = jnp.zeros((HALO, LANES), F32)
        dw_acc[...] = jnp.zeros_like(dw_acc)

        def chunk(i, carry):
            t0 = pl.multiple_of(i * ROW_CHUNK, ROW_CHUNK)
            hist = a_pad[pl.ds(t0, ext), :]
            fut = dc_pad[pl.ds(t0, ext), :]
            dc_cur = fut[:ROW_CHUNK, :]
            da = jnp.zeros((ROW_CHUNK, LANES), F32)
            for k in range(CONV_KERNEL):
                lag = CONV_KERNEL - 1 - k
                da = da + w_ref[k:k + 1, :] * pltpu.roll(fut, (ext - lag) % ext, 0)[:ROW_CHUNK, :]
                dw_acc[pl.ds(SUBLANES * k, SUBLANES), :] += _rowsum8(dc_cur * pltpu.roll(hist, lag, 0)[HALO:, :])
            rows = pl.ds(t0, ROW_CHUNK)
            sg = jax.nn.sigmoid(gc_ref[rows, :])
            dv_ref[rows, :] = (da * sg).astype(BF16)
            dgc_ref[rows, :] = (da * v_ref[rows, :] * sg * (1.0 - sg)).astype(BF16)
            return carry

        lax.fori_loop(0, n_chunks, chunk, 0)
        dw_ref[...] = dw_acc[...].reshape(CONV_TAPS_PADDED, SUBLANES, LANES).sum(axis=1)

    col = lambda j: (0, j)
    return pl.pallas_call(
        body, name="conv_bwd", grid=(width // LANES,),
        in_specs=[pl.BlockSpec((lp, LANES), lambda j: (0, j + v_blk0)),
                  pl.BlockSpec((lp, LANES), lambda j: (0, j + g_blk0)),
                  pl.BlockSpec((lp, LANES), col),
                  pl.BlockSpec((CONV_TAPS_PADDED, LANES), col), ANY_SPEC],
        out_specs=[pl.BlockSpec((lp, LANES), lambda j: (0, j + v_blk0)), pl.BlockSpec((lp, LANES), col),
                   pl.BlockSpec((CONV_TAPS_PADDED, LANES), col)],
        out_shape=[jax.ShapeDtypeStruct(dproj.shape, BF16), jax.ShapeDtypeStruct((lp, width), BF16),
                   jax.ShapeDtypeStruct((CONV_TAPS_PADDED, width), F32)],
        scratch_shapes=[pltpu.VMEM((lp + HALO, LANES), F32), pltpu.VMEM((lp + HALO, LANES), F32),
                        pltpu.VMEM((CONV_TAPS_PADDED * SUBLANES, LANES), F32)],
        input_output_aliases={4: 0},
        compiler_params=_cparams("parallel"),
    )(proj, proj, dc, w_dw, dproj)


def _place_columns(dst, pieces, name):
    m = dst.shape[0]
    tile = 512
    counts = [p.shape[1] // tile for p, _ in pieces]
    starts = [sum(counts[:i]) for i in range(len(pieces))]
    n_steps = sum(counts)

    def local(s, i):
        return jnp.clip(s - starts[i], 0, counts[i] - 1)

    def out_index(s):
        blk = pieces[0][1] // tile + local(s, 0)
        for i in range(1, len(pieces)):
            blk = jnp.where(s >= starts[i], pieces[i][1] // tile + local(s, i), blk)
        return 0, blk

    def body(*refs):
        out_ref = refs[-1]
        s = pl.program_id(0)
        for i in range(len(pieces)):
            @pl.when((s >= starts[i]) & (s < starts[i] + counts[i]))
            def _(i=i):
                out_ref[...] = refs[i][...]

    return pl.pallas_call(
        body, name=name, grid=(n_steps,),
        in_specs=[pl.BlockSpec((m, tile), functools.partial(lambda s, i: (0, local(s, i)), i=i))
                  for i in range(len(pieces))] + [ANY_SPEC],
        out_specs=pl.BlockSpec((m, tile), out_index),
        out_shape=jax.ShapeDtypeStruct(dst.shape, dst.dtype),
        input_output_aliases={len(pieces): 0},
        compiler_params=_cparams("arbitrary"),
    )(*[p for p, _ in pieces], dst)


def _adamw_math(w, g, m, v):
    m = ADAM_B1 * m + (1.0 - ADAM_B1) * g
    v = ADAM_B2 * v + (1.0 - ADAM_B2) * jnp.square(g)
    m_hat = m / (1.0 - ADAM_B1 ** ADAM_STEP)
    v_hat = v / (1.0 - ADAM_B2 ** ADAM_STEP)
    delta = -ADAM_LR * (m_hat / (jnp.sqrt(v_hat) + ADAM_EPS) + ADAM_WD * w)
    return delta, m, v


def _pair_sum(own, recv, where, name):
    _, _, rows, cols = own.shape
    tr = _row_tile(rows, cols, 1024 * 1024)

    def body(where_ref, own_ref, recv_ref, out_ref):
        out_ref[...] = (own_ref[...].astype(F32) + recv_ref[...].astype(F32)).astype(BF16)

    return pl.pallas_call(
        body, name=name,
        grid_spec=pltpu.PrefetchScalarGridSpec(
            num_scalar_prefetch=1, grid=(3, rows // tr),
            in_specs=[pl.BlockSpec((None, None, tr, cols), lambda r, i, wh: (wh[2 + r], wh[0], i, 0)),
                      pl.BlockSpec((None, tr, cols), lambda r, i, wh: (wh[2 + r], i, 0))],
            out_specs=pl.BlockSpec((None, tr, cols), lambda r, i, wh: (wh[2 + r], i, 0))),
        out_shape=jax.ShapeDtypeStruct((4, rows, cols), BF16),
        compiler_params=_cparams("parallel", "parallel"),
    )(where, own, recv)


def _adamw_big(w, m, v, own, from_sibling, recv3, where, name):
    rows, cols = w.shape
    tr = _row_tile(rows, cols, 256 * 1024)

    def body(where_ref, w_ref, m_ref, v_ref, own_ref, sib_ref, r_ref, g_out, d_out, m_out, v_out):
        g = own_ref[...].astype(F32) + sib_ref[...].astype(F32)
        for r in range(3):
            g = g + r_ref[r].astype(F32)
        delta, m_new, v_new = _adamw_math(w_ref[...], g, m_ref[...], v_ref[...])
        g_out[...] = g
        d_out[...] = delta
        m_out[...] = m_new
        v_out[...] = v_new

    blk = pl.BlockSpec((tr, cols), lambda i, q_ref: (i, 0))
    return pl.pallas_call(
        body, name=name,
        grid_spec=pltpu.PrefetchScalarGridSpec(
            num_scalar_prefetch=1, grid=(rows // tr,),
            in_specs=[blk, blk, blk,
                      pl.BlockSpec((None, None, tr, cols), lambda i, wh: (wh[1], wh[0], i, 0)),
                      pl.BlockSpec((None, tr, cols), lambda i, wh: (wh[1], i, 0)),
                      pl.BlockSpec((3, tr, cols), lambda i, wh: (0, i, 0))],
            out_specs=[blk] * 4),
        out_shape=[jax.ShapeDtypeStruct((rows, cols), F32)] * 4,
        compiler_params=_cparams("parallel"),
    )(where, w, m, v, own, from_sibling, recv3)


def _small_update(me_idx, packed, rep_params, rep_places, meta_wmv, meta_row0, wdw_wmv, wdw_row0, loss_row0):
    n_rep = len(rep_params)
    meta_rows, meta_cols = meta_wmv[0].shape
    wdw_rows, wdw_cols = wdw_wmv[0].shape

    def body(me_ref, *refs):
        pos = 0

        def take(k):
            nonlocal pos
            out = refs[pos:pos + k]
            pos += k
            return out

        rep_in = [take(3) for _ in range(n_rep)]
        rep_g = take(n_rep)
        meta_in, (meta_g,) = take(3), take(1)
        wdw_in, (wdw_g,) = take(3), take(1)
        (loss_ref,) = take(1)
        rep_out = [take(4) for _ in range(n_rep)]
        meta_out, wdw_out = take(4), take(4)
        (loss_out,) = take(1)

        def update(wmv, g, outs):
            delta, m_new, v_new = _adamw_math(wmv[0][...], g, wmv[1][...], wmv[2][...])
            for o_ref, val in zip(outs, (g, delta, m_new, v_new)):
                o_ref[...] = val

        for wmv, g_ref, outs in zip(rep_in, rep_g, rep_out):
            g = jnp.sum(g_ref[0], axis=0, keepdims=True)
            for j in range(1, N_DEV):
                g = g + jnp.sum(g_ref[j], axis=0, keepdims=True)
            update(wmv, g, outs)
        for wmv, g_ref, outs in ((meta_in, meta_g, meta_out), (wdw_in, wdw_g, wdw_out)):
            g = g_ref[0]
            for j in range(1, N_DEV):
                g = g + g_ref[j]
            update(wmv, g, outs)
        total = loss_ref[0]
        for j in range(1, N_DEV):
            total = total + loss_ref[j]
        loss_out[...] = total

    def whole(a):
        nd = a.ndim
        return pl.BlockSpec(a.shape, lambda i, me_ref, nd=nd: (0,) * nd)

    ins, in_specs = [], []
    for wmv in rep_params:
        ins += list(wmv)
        in_specs += [whole(a) for a in wmv]
    for wmv, (row0, col0) in zip(rep_params, rep_places):
        width = wmv[0].shape[1]
        ins.append(packed)
        in_specs.append(pl.BlockSpec((N_DEV, SUBLANES, width),
                                     lambda i, me_ref, rb=row0 // SUBLANES, cb=col0 // width: (0, rb, cb)))
    ins += list(meta_wmv) + [packed]
    in_specs += [whole(a) for a in meta_wmv]
    in_specs.append(pl.BlockSpec((N_DEV, meta_rows, meta_cols),
                                 lambda i, me_ref, rb=meta_row0 // meta_rows: (0, rb, me_ref[0])))
    ins += list(wdw_wmv) + [packed]
    in_specs += [whole(a) for a in wdw_wmv]
    in_specs.append(pl.BlockSpec((N_DEV, wdw_rows, wdw_cols),
                                 lambda i, me_ref, rb=wdw_row0 // wdw_rows: (0, rb, me_ref[0])))
    ins.append(packed)
    in_specs.append(pl.BlockSpec((N_DEV, SUBLANES, LANES), lambda i, me_ref, rb=loss_row0 // SUBLANES: (0, rb, 0)))

    out_shape, out_specs = [], []
    for wmv in list(rep_params) + [meta_wmv, wdw_wmv]:
        out_shape += [jax.ShapeDtypeStruct(wmv[0].shape, F32)] * 4
        out_specs += [whole(wmv[0])] * 4
    out_shape.append(jax.ShapeDtypeStruct((SUBLANES, LANES), F32))
    out_specs.append(pl.BlockSpec((SUBLANES, LANES), lambda i, me_ref: (0, 0)))

    outs = pl.pallas_call(
        body, name="small_update",
        grid_spec=pltpu.PrefetchScalarGridSpec(num_scalar_prefetch=1, grid=(1,), in_specs=in_specs,
                                               out_specs=out_specs),
        out_shape=out_shape, compiler_params=_cparams("arbitrary"),
    )(me_idx, *ins)
    groups = [outs[4 * i:4 * i + 4] for i in range(n_rep + 2)]
    return groups[:n_rep], groups[n_rep], groups[n_rep + 1], outs[-1]


def kernel(x, meta, g_pre_mix, w_in, w_pool_grp, pool_scale, w_pool_out, w_dw, b_dw, conv_ln_g, conv_ln_b, w_conv_out, w_o, g_post_mix, g_pre_mlp, w_up, w_down, g_post_mlp, loss_target, m_meta, m_g_pre_mix, m_w_in, m_w_pool_grp, m_pool_scale, m_w_pool_out, m_w_dw, m_b_dw, m_conv_ln_g, m_conv_ln_b, m_w_conv_out, m_w_o, m_g_post_mix, m_g_pre_mlp, m_w_up, m_w_down, m_g_post_mlp, v_meta, v_g_pre_mix, v_w_in, v_w_pool_grp, v_pool_scale, v_w_pool_out, v_w_dw, v_b_dw, v_conv_ln_g, v_conv_ln_b, v_w_conv_out, v_w_o, v_g_post_mix, v_g_pre_mlp, v_w_up, v_w_down, v_g_post_mlp):
    seq, d = x.shape[1], x.shape[2]
    pool_w = pool_scale.shape[1]
    conv_w = b_dw.shape[1]
    n_grp, grp_rows, gdim = w_pool_grp.shape[1:]
    lp = _round_up(N_META + seq, ROW_CHUNK)
    tm_half = lp // 2 if (lp // 2) % 16 == 0 else lp
    c_idx = lax.axis_index("c").astype(jnp.int32)
    chip_idx = (2 * lax.axis_index("x") + lax.axis_index("y")).astype(jnp.int32)
    me_idx = 2 * chip_idx + c_idx

    pad_taps = ((0, CONV_TAPS_PADDED - CONV_KERNEL), (0, 0))
    big = dict(w_in=w_in[0], w_pool_grp=w_pool_grp[0].reshape(n_grp * grp_rows, gdim), w_pool_out=w_pool_out[0],
               w_conv_out=w_conv_out[0], w_o=w_o[0], w_up=w_up[0], w_down=w_down[0])
    big_names = list(big)
    moments = dict(w_in=(m_w_in, v_w_in), w_pool_grp=(m_w_pool_grp, v_w_pool_grp), w_pool_out=(m_w_pool_out, v_w_pool_out),
                   w_conv_out=(m_w_conv_out, v_w_conv_out), w_o=(m_w_o, v_w_o), w_up=(m_w_up, v_w_up),
                   w_down=(m_w_down, v_w_down))
    slot_idx = me_idx.reshape(1)
    sources = dict(big, meta=meta, w_dw=jnp.pad(w_dw[0], pad_taps))

    def fill(k, after):
        return _fill_slot(sources[k], slot_idx, BF16 if k in big else F32, after, "fill_" + k)

    gather_groups = [["meta", "w_dw"], ["w_in"], ["w_pool_grp", "w_pool_out", "w_conv_out", "w_o"], ["w_up"], ["w_down"]]
    started, token = _gather_start([[fill(k, slot_idx) for k in names] for names in gather_groups[:2]], slot_idx,
                                   "gather_start_first")
    started_rest, _ = _gather_start([[fill(k, token) for k in names] for names in gather_groups[2:]], token,
                                    "gather_start_rest")
    started += started_rest
    wg = {}
    x_idx, y_idx = lax.axis_index("x"), lax.axis_index("y")
    at = lambda px, py, pc: 4 * px + 2 * py + pc
    near_slots = jnp.stack([at(x_idx, y_idx, c_idx), at(x_idx, y_idx, 1 - c_idx), at(1 - x_idx, y_idx, c_idx),
                            at(x_idx, 1 - y_idx, c_idx), at(1 - x_idx, y_idx, 1 - c_idx),
                            at(x_idx, 1 - y_idx, 1 - c_idx)]).astype(jnp.int32)
    far_slots = jnp.stack([at(1 - x_idx, 1 - y_idx, c_idx), at(1 - x_idx, 1 - y_idx, 1 - c_idx)]).astype(jnp.int32)

    def gather_whole(gi, after_forward, after_finish):
        send, recv, lands = started[gi]
        fs, fr, lands = _gather_forward(lands, recv, after_forward(), f"gather_forward_{gi}")
        lands = _gather_finish(lands, send, recv, [(ALL_CHIPS, fs, fr)], [(ALL_CHIPS, fs, fr)], after_finish(),
                               f"gather_finish_{gi}")
        wg.update(zip(gather_groups[gi], lands))

    near_state = {}

    def gather_near(gi, after):
        send, recv, lands = started[gi]
        fs, fr, lands = _gather_forward(lands, recv, after, f"gather_forward_near_{gi}", which=NEAR)
        lands = _gather_finish(lands, send, recv, [(NEAR, fs, fr)], [], after, f"gather_finish_near_{gi}",
                               direct_sends=False)
        near_state[gi] = (fs, fr)
        return lands

    def gather_far(gi, lands, after):
        send, recv, _ = started[gi]
        fs, fr, lands = _gather_forward(lands, recv, after, f"gather_forward_far_{gi}", which=FAR)
        lands = _gather_finish(lands, send, recv, [(FAR, fs, fr)], [(NEAR,) + near_state[gi], (FAR, fs, fr)], after,
                               f"gather_finish_far_{gi}", own=False)
        wg.update(zip(gather_groups[gi], lands))
        return lands

    gather_whole(0, lambda: token, lambda: token)
    meta_full = wg["meta"].transpose(1, 0, 2).reshape(N_META, d)
    wdw_full = wg["w_dw"].transpose(1, 0, 2).reshape(CONV_TAPS_PADDED, conv_w)
    target = loss_target[0]
    h0, u1 = _rms_pre(meta_full, x[0], g_pre_mix, lp, seq)
    (w_in_near,) = gather_near(1, u1)
    proj = _mm_slots(u1, w_in_near, near_slots, over="n", tm=tm_half, name="mm_proj_near")
    (w_in_all,) = gather_far(1, [w_in_near], proj)
    proj = _mm_slots(u1, w_in_all, far_slots, over="n", tm=tm_half, base=proj, name="mm_proj_far")
    conv_c = _conv_fwd(proj, wdw_full, b_dw, lp, conv_w, pool_w)
    s_act = _ln_silu(conv_c, conv_ln_g, conv_ln_b, lp)
    gather_whole(2, lambda: proj, lambda: s_act)
    wpg_full = wg["w_pool_grp"].reshape(N_DEV, n_grp, grp_rows, gdim).transpose(1, 0, 2, 3).reshape(n_grp, gdim, gdim)
    w_o_full = wg["w_o"].reshape(d, d)
    ya_pre = _pool_fwd(proj, wpg_full, pool_scale, lp)
    m_mix, y_a, y_b = _gate_mix(ya_pre, s_act, wg["w_pool_out"], wg["w_conv_out"], proj, d, "gate_mix")
    o = _mm(m_mix, w_o_full, mode="nn", tm=tm_half, tn=512, name="mm_o")
    h1, u2 = _post_mix(o, h0, g_post_mix, g_pre_mlp, lp)
    relu2 = lambda acc: jnp.square(jnp.maximum(acc, 0.0))
    (w_up_near,) = gather_near(3, u2)
    act = _mm_slots(u2, w_up_near, near_slots, over="n", tm=tm_half, out_dtype=BF16, epilogue=relu2, name="mm_up_near")
    (w_up_all,) = gather_far(3, [w_up_near], act)
    act = _mm_slots(u2, w_up_all, far_slots, over="n", tm=tm_half, out_dtype=BF16, epilogue=relu2, base=act,
                    name="mm_up_far")
    (w_down_near,) = gather_near(4, act)
    f = _mm_slots(act, w_down_near, near_slots, over="k", tm=tm_half, tn=1024, name="mm_down_near")
    (w_down_all,) = gather_far(4, [w_down_near], f)
    f = _mm_slots(act, w_down_all, far_slots, over="k", tm=tm_half, tn=1024, base=f, name="mm_down_far")
    w_down_full = w_down_all.reshape(-1, d)

    big_out = {}

    def to_sibling(names, grads, after, tag):
        send, recv, grads, lands, token = _pair_exchange_start(grads, after, "grads_to_sibling_start_" + tag)
        return (names, send, recv, grads, lands, tag), token

    where = jnp.stack([c_idx, chip_idx, 2 * (1 - x_idx) + y_idx, 2 * x_idx + (1 - y_idx),
                       2 * (1 - x_idx) + (1 - y_idx)]).astype(jnp.int32)

    def to_owner(handle, after):
        names, send, recv, grads, from_sib, tag = handle
        grads, from_sib = _pair_exchange_finish(grads, from_sib, send, recv, after, "grads_to_sibling_finish_" + tag)
        own = [g.reshape((4, 2) + g.shape[1:]) for g in grads]
        sums = [_pair_sum(o, r, where, "pair_sum_" + k) for k, o, r in zip(names, own, from_sib)]
        send, recv, sums, lands, token = _chip_exchange_start(sums, after, "grads_to_owner_start_" + tag)
        return (names, send, recv, sums, lands, own, from_sib, tag), token

    def update(handle, after):
        names, send, recv, sums, lands, own, from_sib, tag = handle
        got = _chip_exchange_finish(sums, lands, send, recv, after, "grads_to_owner_finish_" + tag)
        for k, o, s, r3 in zip(names, own, from_sib, got):
            w2 = big[k]
            shape = moments[k][0].shape
            outs = _adamw_big(w2, moments[k][0].reshape(w2.shape), moments[k][1].reshape(w2.shape), o, s, r3,
                              where, "adamw_" + k)
            big_out[k] = [a.reshape(shape) for a in outs]
        return big_out[names[-1]][0]

    df, dh, dg_post_mlp, loss_part = _loss_head(f, h1, target, g_post_mlp, lp, seq)
    d_up = _mm(df, w_down_full, mode="nt", tm=tm_half, tn=1024, out_dtypes=(BF16,), extras=[(act, 0)],
               epilogue=lambda acc, a: (acc * (2.0 * jnp.sqrt(a.astype(F32))),), name="mm_d_up")
    g_w_down = _mm(act, df, mode="tn", tm=1024, tn=1024, out_dtypes=(BF16,), name="mm_g_down")
    sib_down, token = to_sibling(["w_down"], [g_w_down.reshape(N_DEV, -1, d)], slot_idx, "down")
    g_w_up = _mm(u2, d_up, mode="tn", tm=1024, tn=wg["w_up"].shape[2], out_blocked=True, out_dtypes=(BF16,),
                 after=token, name="mm_g_up")
    sib_up, token = to_sibling(["w_up"], [g_w_up], slot_idx, "up")
    pending_down, token = to_owner(sib_down, token)
    du2 = _mm(d_up, wg["w_up"], mode="nt", tm=tm_half, tn=1024, b_blocked=True, kb=2, after=token, name="mm_du2")
    pending_up, token = to_owner(sib_up, du2)
    dh1, do, dg_pre_mlp, dg_post_mix = _mid_bwd(du2, h1, dh, o, g_pre_mlp, g_post_mix, lp)

    def gate_bwd(dm, ga, gb, ya, yb):
        sa, sb = jax.nn.sigmoid(ga), jax.nn.sigmoid(gb)
        return (dm * ya.astype(F32) * sa * (1.0 - sa), dm * yb.astype(F32) * sb * (1.0 - sb), dm * sa, dm * sb)

    gate_tn = 512
    ga_col0, gb_col0 = proj.shape[1] - 2 * d, proj.shape[1] - d
    dproj, d_gb, d_ya, d_yb = _mm(
        do, w_o_full, mode="nt", tm=tm_half, tn=gate_tn, out_dtypes=(BF16,) * 4,
        extras=[(proj, ga_col0 // gate_tn), (proj, gb_col0 // gate_tn), (y_a, 0), (y_b, 0)], epilogue=gate_bwd,
        out_places=((proj.shape[1], ga_col0), None, None, None), after=token, name="mm_dm")
    g_w_o = _mm(m_mix, do, mode="tn", tm=1024, tn=1024, out_dtypes=(BF16,), name="mm_g_o")
    bn_out = wg["w_pool_out"].shape[2]
    g_w_pool_out = _mm(ya_pre, d_ya, mode="tn", tm=pool_w, tn=bn_out, out_blocked=True, out_dtypes=(BF16,),
                       name="mm_g_pool_out")
    g_w_conv_out = _mm(s_act, d_yb, mode="tn", tm=conv_w, tn=bn_out, out_blocked=True, out_dtypes=(BF16,),
                       name="mm_g_conv_out")
    sib_mix, token = to_sibling(["w_o", "w_pool_out", "w_conv_out"],
                                [g_w_o.reshape(N_DEV, -1, d), g_w_pool_out, g_w_conv_out], slot_idx, "mix")
    d_ya_pre = _mm(d_ya, wg["w_pool_out"], mode="nt", tm=tm_half, tn=pool_w, b_blocked=True, after=token,
                   name="mm_d_ya_pre")
    d_s = _mm(d_yb, wg["w_conv_out"], mode="nt", tm=tm_half, tn=conv_w, b_blocked=True, name="mm_d_s")
    pending_mix, token = to_owner(sib_mix, d_s)
    dproj, g_wpg, d_scale = _pool_bwd(proj, d_ya_pre, wpg_full, pool_scale, dproj, lp)
    dc, d_ln_g, d_ln_b, d_b_dw = _ln_silu_bwd(conv_c, d_s, conv_ln_g, conv_ln_b, lp)
    dproj, dgc, g_wdw = _conv_bwd(proj, dc, wdw_full, dproj, lp, conv_w, pool_w)
    dproj = _place_columns(dproj, [(dgc, pool_w + conv_w), (d_gb, gb_col0)], "place_dproj")
    g_w_in = _mm(u1, dproj, mode="tn", tm=1024, tn=wg["w_in"].shape[2], out_blocked=True, out_dtypes=(BF16,),
                 after=token, name="mm_g_in")
    g_wpg_slots = g_wpg.astype(BF16).reshape(n_grp, N_DEV, grp_rows, gdim).transpose(1, 0, 2, 3)
    sib_in, token = to_sibling(["w_pool_grp", "w_in"],
                               [g_wpg_slots.reshape(N_DEV, n_grp * grp_rows, gdim), g_w_in], slot_idx, "in")
    done = update(pending_down, token)
    pending_in, token = to_owner(sib_in, done)
    done = update(pending_up, token)
    du1 = _mm(dproj, wg["w_in"], mode="nt", tm=tm_half, tn=1024, b_blocked=True, kb=2, after=done, name="mm_du1")
    grad_x2, grad_meta_part, dg_pre_mix = _pre_mix_bwd(du1, h0, dh1, g_pre_mix, seq)
    grad_x = grad_x2[None]

    assert pool_w + conv_w == d and conv_w <= d and LANES <= d
    widen = lambda a: jnp.pad(a, ((0, 0), (0, d - a.shape[1])))
    packed = jnp.concatenate([
        dg_pre_mix, dg_post_mix, dg_pre_mlp, dg_post_mlp,
        jnp.concatenate([d_scale, d_ln_g], axis=1), jnp.concatenate([d_ln_b, d_b_dw], axis=1),
        grad_meta_part, widen(g_wdw), widen(loss_part)], axis=0)
    rep = dict(g_pre_mix=((g_pre_mix, m_g_pre_mix, v_g_pre_mix), (0, 0)),
               g_post_mix=((g_post_mix, m_g_post_mix, v_g_post_mix), (SUBLANES, 0)),
               g_pre_mlp=((g_pre_mlp, m_g_pre_mlp, v_g_pre_mlp), (2 * SUBLANES, 0)),
               g_post_mlp=((g_post_mlp, m_g_post_mlp, v_g_post_mlp), (3 * SUBLANES, 0)),
               pool_scale=((pool_scale, m_pool_scale, v_pool_scale), (4 * SUBLANES, 0)),
               conv_ln_g=((conv_ln_g, m_conv_ln_g, v_conv_ln_g), (4 * SUBLANES, pool_w)),
               conv_ln_b=((conv_ln_b, m_conv_ln_b, v_conv_ln_b), (5 * SUBLANES, 0)),
               b_dw=((b_dw, m_b_dw, v_b_dw), (5 * SUBLANES, conv_w)))
    meta_row0 = 6 * SUBLANES
    wdw_row0 = meta_row0 + N_META
    loss_row0 = wdw_row0 + CONV_TAPS_PADDED
    (small_started,), token = _gather_start([[_fill_slot(packed, slot_idx, F32, slot_idx, "fill_small")]], grad_x2,
                                            "gather_small_start")
    done = update(pending_mix, token)
    done = update(pending_in, done)
    send, recv, lands = small_started
    fsend, frecv, lands = _gather_forward(lands, recv, done, "gather_small_forward")
    (packed_all,) = _gather_finish(lands, send, recv, [(ALL_CHIPS, fsend, frecv)], [(ALL_CHIPS, fsend, frecv)], done,
                                   "gather_small_finish")
    rep_names = list(rep)
    wdw_wmv = [jnp.pad(a[0], pad_taps) for a in (w_dw, m_w_dw, v_w_dw)]
    rep_out, meta_out, wdw_out, loss_blk = _small_update(
        slot_idx, packed_all, [rep[k][0] for k in rep_names], [rep[k][1] for k in rep_names],
        (meta, m_meta, v_meta), meta_row0, wdw_wmv, wdw_row0, loss_row0)
    small_out = dict(zip(rep_names, rep_out))
    small_out["meta"] = meta_out
    small_out["w_dw"] = [a[:CONV_KERNEL][None] for a in wdw_out]

    order = ["meta", "g_pre_mix", "w_in", "w_pool_grp", "pool_scale", "w_pool_out", "w_dw", "b_dw", "conv_ln_g",
             "conv_ln_b", "w_conv_out", "w_o", "g_post_mix", "g_pre_mlp", "w_up", "w_down", "g_post_mlp"]
    by_name = {**big_out, **small_out}
    result = [loss_blk[0, 0], grad_x]
    for kind in range(4):
        result += [by_name[k][kind] for k in order]
    return tuple(result)
```

```python
import functools

import jax
import jax.numpy as jnp
from jax import lax
from jax.experimental import pallas as pl
from jax.experimental.pallas import tpu as pltpu

F32 = jnp.float32
BF16 = jnp.bfloat16
MESH = pl.DeviceIdType.MESH

N_DEV = 8
N_META = 16
POOL_WINDOWS = (2, 4, 8, 16)
CONV_KERNEL = 31
CONV_TAPS_PADDED = 32
RMS_EPS = 1e-6
LN_EPS = 1e-5
ADAM_LR = 0.001
ADAM_B1 = 0.9
ADAM_B2 = 0.999
ADAM_EPS = 1e-08
ADAM_WD = 0.01
ADAM_STEP = 10

LANES = 128
SUBLANES = 8
ROW_CHUNK = 128
HALO = 32
VMEM_LIMIT_BYTES = 56 * 1024 * 1024


def _cparams(*sem):
    return pltpu.CompilerParams(dimension_semantics=sem if sem else None, vmem_limit_bytes=VMEM_LIMIT_BYTES)


def _round_up(n, m):
    return (n + m - 1) // m * m


def _row_tile(rows, cols, max_elems=640 * 1024):
    best = None
    for t in range(16, rows + 1, 16):
        if rows % t == 0 and (best is None or t * cols <= max_elems):
            best = t
    assert best is not None, (rows, cols)
    return best


def _rowsum8(a):
    t, w = a.shape
    return a.reshape(t // SUBLANES, SUBLANES, w).sum(axis=0)


def _mesh_pos():
    return lax.axis_index("x"), lax.axis_index("y"), lax.axis_index("c")


HBM_SPEC = pl.BlockSpec(memory_space=pltpu.HBM)
SEM_SPEC = pl.BlockSpec(memory_space=pltpu.SEMAPHORE)
ANY_SPEC = pl.BlockSpec(memory_space=pl.ANY)
_DATAFLOW = pltpu.SideEffectType.DATAFLOW_SIDE_EFFECTING


def _hbm(a):
    return pltpu.with_memory_space_constraint(a, pltpu.HBM)


def _slot(p):
    return 4 * p[0] + 2 * p[1] + p[2]


def _fill_slot(w, slot_idx, dtype, after, name):
    rows, cols = w.shape
    tr = _row_tile(rows, cols) if rows % 16 == 0 else rows

    def body(idx_ref, w_ref, _, out_ref):
        out_ref[...] = w_ref[...].astype(dtype)

    return pl.pallas_call(
        body, name=name,
        grid_spec=pltpu.PrefetchScalarGridSpec(
            num_scalar_prefetch=1, grid=(rows // tr,),
            in_specs=[pl.BlockSpec((tr, cols), lambda i, idx_ref: (i, 0)), ANY_SPEC],
            out_specs=pl.BlockSpec((None, tr, cols), lambda i, idx_ref: (idx_ref[0], i, 0))),
        out_shape=jax.ShapeDtypeStruct((N_DEV, rows, cols), dtype),
        compiler_params=_cparams("parallel"),
    )(slot_idx, w, after)


def _gather_start(groups, after, name, issue_order=(0, 1, 2, 3)):
    flat = [g for grp in groups for g in grp]
    n, n_grp = len(flat), len(groups)

    def body(*refs):
        lands = refs[:n]
        sems = refs[n + 1:n + 1 + 2 * n_grp]
        token = refs[-1]
        x, y, c = _mesh_pos()
        targets = [(x, y, 1 - c), (1 - x, y, c), (x, 1 - y, c), (1 - x, 1 - y, c)]
        t = 0
        for gi, grp in enumerate(groups):
            for ti in range(len(grp)):
                mine = lands[t].at[_slot((x, y, c))]
                for k in issue_order:
                    pltpu.make_async_remote_copy(
                        src_ref=mine, dst_ref=mine,
                        send_sem=sems[2 * gi].at[4 * ti + k], recv_sem=sems[2 * gi + 1].at[4 * ti + k],
                        device_id=targets[k], device_id_type=MESH).start()
                t += 1
        token[...] = jnp.zeros_like(token)

    sem_shapes = []
    for grp in groups:
        sem_shapes += [pltpu.SemaphoreType.DMA((4 * len(grp),))] * 2
    outs = pl.pallas_call(
        body, name=name,
        out_shape=tuple(sem_shapes + [pltpu.HBM(g.shape, g.dtype) for g in flat]
                        + [jax.ShapeDtypeStruct((SUBLANES, LANES), F32)]),
        in_specs=tuple([HBM_SPEC] * n + [ANY_SPEC]),
        out_specs=tuple([SEM_SPEC] * (2 * n_grp) + [HBM_SPEC] * n + [pl.BlockSpec(memory_space=pltpu.VMEM)]),
        input_output_aliases={i: 2 * n_grp + i for i in range(n)},
        compiler_params=pltpu.CompilerParams(has_side_effects=_DATAFLOW),
    )(*[_hbm(g) for g in flat], after)
    sems, lands, token = outs[:2 * n_grp], outs[2 * n_grp:-1], outs[-1]
    res, t = [], 0
    for gi, grp in enumerate(groups):
        res.append((sems[2 * gi], sems[2 * gi + 1], list(lands[t:t + len(grp)])))
        t += len(grp)
    return res, token


NEAR, FAR, ALL_CHIPS = (0, 1), (2,), (0, 1, 2)


def _gather_forward(lands, recv_sems, after, name, which=ALL_CHIPS):
    n, nw = len(lands), len(which)

    def body(*refs):
        land_refs, recv, _ = refs[:n], refs[n], refs[n + 1]
        fsend, frecv = refs[n + 2], refs[n + 3]
        x, y, c = _mesh_pos()
        chips = [(1 - x, y), (x, 1 - y), (1 - x, 1 - y)]
        for t in range(n):
            for i, j in enumerate(which):
                blk = land_refs[t].at[_slot((*chips[j], c))]
                pltpu.make_async_remote_copy(src_ref=blk, dst_ref=blk, send_sem=fsend.at[nw * t + i],
                                             recv_sem=recv.at[4 * t + 1 + j],
                                             device_id=(x, y, 1 - c), device_id_type=MESH).wait_recv()
                pltpu.make_async_remote_copy(src_ref=blk, dst_ref=blk, send_sem=fsend.at[nw * t + i],
                                             recv_sem=frecv.at[nw * t + i],
                                             device_id=(x, y, 1 - c), device_id_type=MESH).start()

    outs = pl.pallas_call(
        body, name=name,
        out_shape=tuple([pltpu.SemaphoreType.DMA((nw * n,))] * 2 + [pltpu.HBM(g.shape, g.dtype) for g in lands]),
        in_specs=tuple([HBM_SPEC] * n + [SEM_SPEC, ANY_SPEC]),
        out_specs=tuple([SEM_SPEC] * 2 + [HBM_SPEC] * n),
        input_output_aliases={i: 2 + i for i in range(n)},
        compiler_params=pltpu.CompilerParams(has_side_effects=_DATAFLOW),
    )(*lands, recv_sems, after)
    return outs[0], outs[1], list(outs[2:])


def _gather_finish(lands, send_sems, recv_sems, arrivals, sends, after, name, own=True, direct_sends=True):
    n = len(lands)
    fwd = list(arrivals) + list(sends)
    sem_args = [send_sems, recv_sems]
    where = []
    for _, fs, fr in fwd:
        pos = []
        for arr in (fs, fr):
            hit = [i for i, have in enumerate(sem_args) if have is arr]
            if not hit:
                sem_args.append(arr)
                hit = [len(sem_args) - 1]
            pos.append(hit[0])
        where.append(pos)

    def body(*refs):
        land_refs = refs[:n]
        send, recv = refs[n], refs[n + 1]
        fwd_refs = [refs[n + p] for pos in where for p in pos]
        x, y, c = _mesh_pos()
        sibling = (x, y, 1 - c)
        chips = [(1 - x, y), (x, 1 - y), (1 - x, 1 - y)]

        def desc(ref, s_sem, r_sem):
            return pltpu.make_async_remote_copy(src_ref=ref, dst_ref=ref, send_sem=s_sem, recv_sem=r_sem,
                                                device_id=sibling, device_id_type=MESH)

        for t in range(n):
            mine = land_refs[t].at[_slot((x, y, c))]
            if own:
                desc(land_refs[t].at[_slot(sibling)], send.at[4 * t], recv.at[4 * t]).wait_recv()
            for a, (which, _, _) in enumerate(fwd):
                fs, fr = fwd_refs[2 * a], fwd_refs[2 * a + 1]
                for i, j in enumerate(which):
                    if a < len(arrivals):
                        desc(land_refs[t].at[_slot((*chips[j], 1 - c))], fs.at[len(which) * t + i],
                             fr.at[len(which) * t + i]).wait_recv()
                    else:
                        desc(land_refs[t].at[_slot((*chips[j], c))], fs.at[len(which) * t + i],
                             fr.at[len(which) * t + i]).wait_send()
            if direct_sends:
                for k in range(4):
                    desc(mine, send.at[4 * t + k], recv.at[4 * t + k]).wait_send()

    outs = pl.pallas_call(
        body, name=name,
        out_shape=tuple(pltpu.HBM(g.shape, g.dtype) for g in lands),
        in_specs=tuple([HBM_SPEC] * n + [SEM_SPEC] * len(sem_args) + [ANY_SPEC]),
        out_specs=tuple([HBM_SPEC] * n),
        input_output_aliases={i: i for i in range(n)},
        compiler_params=pltpu.CompilerParams(has_side_effects=_DATAFLOW),
    )(*lands, *sem_args, after)
    return list(outs)


def _pair_exchange_start(grads, after, name):
    n = len(grads)

    def body(*refs):
        ins, lands = refs[:n], refs[n:2 * n]
        send, recv = refs[2 * n + 1], refs[2 * n + 2]
        token = refs[-1]
        x, y, c = _mesh_pos()
        for t in range(n):
            for q in range(4):
                pltpu.make_async_remote_copy(
                    src_ref=ins[t].at[2 * q + 1 - c], dst_ref=lands[t].at[q],
                    send_sem=send.at[4 * t + q], recv_sem=recv.at[4 * t + q],
                    device_id=(x, y, 1 - c), device_id_type=MESH).start()
        token[...] = jnp.zeros_like(token)

    land_shapes = [(4,) + g.shape[1:] for g in grads]
    outs = pl.pallas_call(
        body, name=name,
        out_shape=tuple([pltpu.SemaphoreType.DMA((4 * n,))] * 2 + [pltpu.HBM(g.shape, g.dtype) for g in grads]
                        + [pltpu.HBM(ls, g.dtype) for ls, g in zip(land_shapes, grads)]
                        + [jax.ShapeDtypeStruct((SUBLANES, LANES), F32)]),
        in_specs=tuple([HBM_SPEC] * (2 * n) + [ANY_SPEC]),
        out_specs=tuple([SEM_SPEC] * 2 + [HBM_SPEC] * (2 * n) + [pl.BlockSpec(memory_space=pltpu.VMEM)]),
        input_output_aliases={i: 2 + i for i in range(2 * n)},
        compiler_params=pltpu.CompilerParams(has_side_effects=_DATAFLOW),
    )(*[_hbm(g) for g in grads], *[_hbm(lax.empty(ls, g.dtype)) for ls, g in zip(land_shapes, grads)], after)
    return outs[0], outs[1], list(outs[2:2 + n]), list(outs[2 + n:2 + 2 * n]), outs[-1]


def _pair_exchange_finish(grads, lands, send_sems, recv_sems, after, name):
    n = len(grads)

    def body(*refs):
        ins, land_refs = refs[:n], refs[n:2 * n]
        send, recv = refs[2 * n], refs[2 * n + 1]
        x, y, c = _mesh_pos()
        for t in range(n):
            for q in range(4):
                cp = pltpu.make_async_remote_copy(
                    src_ref=ins[t].at[q], dst_ref=land_refs[t].at[q], send_sem=send.at[4 * t + q],
                    recv_sem=recv.at[4 * t + q], device_id=(x, y, 1 - c), device_id_type=MESH)
                cp.wait_send()
                cp.wait_recv()

    outs = pl.pallas_call(
        body, name=name,
        out_shape=tuple([pltpu.HBM(g.shape, g.dtype) for g in grads] + [pltpu.HBM(g.shape, g.dtype) for g in lands]),
        in_specs=tuple([HBM_SPEC] * (2 * n) + [SEM_SPEC] * 2 + [ANY_SPEC]),
        out_specs=tuple([HBM_SPEC] * (2 * n)),
        input_output_aliases={i: i for i in range(2 * n)},
        compiler_params=pltpu.CompilerParams(has_side_effects=_DATAFLOW),
    )(*grads, *lands, send_sems, recv_sems, after)
    return list(outs[:n]), list(outs[n:])


def _chip_exchange_start(sums, after, name):
    n = len(sums)

    def body(*refs):
        ins, lands = refs[:n], refs[n:2 * n]
        send, recv = refs[2 * n + 1], refs[2 * n + 2]
        token = refs[-1]
        x, y, c = _mesh_pos()
        chips = [(1 - x, y), (x, 1 - y), (1 - x, 1 - y)]
        for t in range(n):
            for r, chip in enumerate(chips):
                pltpu.make_async_remote_copy(
                    src_ref=ins[t].at[2 * chip[0] + chip[1]], dst_ref=lands[t].at[r],
                    send_sem=send.at[3 * t + r], recv_sem=recv.at[3 * t + r],
                    device_id=(*chip, c), device_id_type=MESH).start()
        token[...] = jnp.zeros_like(token)

    land_shapes = [(3,) + s.shape[1:] for s in sums]
    outs = pl.pallas_call(
        body, name=name,
        out_shape=tuple([pltpu.SemaphoreType.DMA((3 * n,))] * 2 + [pltpu.HBM(s.shape, s.dtype) for s in sums]
                        + [pltpu.HBM(ls, s.dtype) for ls, s in zip(land_shapes, sums)]
                        + [jax.ShapeDtypeStruct((SUBLANES, LANES), F32)]),
        in_specs=tuple([HBM_SPEC] * (2 * n) + [ANY_SPEC]),
        out_specs=tuple([SEM_SPEC] * 2 + [HBM_SPEC] * (2 * n) + [pl.BlockSpec(memory_space=pltpu.VMEM)]),
        input_output_aliases={i: 2 + i for i in range(2 * n)},
        compiler_params=pltpu.CompilerParams(has_side_effects=_DATAFLOW),
    )(*[_hbm(s) for s in sums], *[_hbm(lax.empty(ls, s.dtype)) for ls, s in zip(land_shapes, sums)], after)
    return outs[0], outs[1], list(outs[2:2 + n]), list(outs[2 + n:2 + 2 * n]), outs[-1]


def _chip_exchange_finish(sums, lands, send_sems, recv_sems, after, name):
    n = len(sums)

    def body(*refs):
        ins, land_refs = refs[:n], refs[n:2 * n]
        send, recv = refs[2 * n], refs[2 * n + 1]
        x, y, c = _mesh_pos()
        for t in range(n):
            for r in range(3):
                cp = pltpu.make_async_remote_copy(
                    src_ref=ins[t].at[r], dst_ref=land_refs[t].at[r], send_sem=send.at[3 * t + r],
                    recv_sem=recv.at[3 * t + r],
                    device_id=(x, y, 1 - c), device_id_type=MESH)
                cp.wait_send()
                cp.wait_recv()

    outs = pl.pallas_call(
        body, name=name,
        out_shape=tuple(pltpu.HBM(g.shape, g.dtype) for g in lands),
        in_specs=tuple([HBM_SPEC] * (2 * n) + [SEM_SPEC] * 2 + [ANY_SPEC]),
        out_specs=tuple([HBM_SPEC] * n),
        input_output_aliases={n + i: i for i in range(n)},
        compiler_params=pltpu.CompilerParams(has_side_effects=_DATAFLOW),
    )(*sums, *lands, send_sems, recv_sems, after)
    return list(outs)


def _mm(a, b, *, mode, tm, tn, tk=None, b_blocked=False, out_blocked=False, out_dtypes=(F32,),
        epilogue=None, extras=(), after=None, kb=1, out_places=None, row_splits=1, name):
    if mode == "nn":
        m, k = a.shape
        n = b.shape[0] * b.shape[2] if b_blocked else b.shape[1]
        dims = (((1,), (0,)), ((), ()))
    elif mode == "nt":
        m, k = a.shape
        n = b.shape[1] if b_blocked else b.shape[0]
        if b_blocked:
            tk = kb * b.shape[2]
        dims = (((1,), (1,)), ((), ()))
    else:
        k, m = a.shape
        n = b.shape[1]
        dims = (((0,), (0,)), ((), ()))
    tk = k if tk is None else tk
    assert m % tm == 0 and n % tn == 0 and k % tk == 0, (name, m, n, k, tm, tn, tk)
    gm, gn, gk = m // tm, n // tn, k // tk
    if b_blocked:
        assert (tn if mode == "nn" else tk) == kb * b.shape[2], name
    if row_splits > 1:
        assert gk == 1 and epilogue is not None and mode != "tn" and not b_blocked and tm % (16 * row_splits) == 0, name

    if mode == "nn":
        a_spec = pl.BlockSpec((tm, tk), lambda i, j, kk: (i, kk))
        b_spec = (pl.BlockSpec((None, tk, tn), lambda i, j, kk: (j, kk, 0)) if b_blocked
                  else pl.BlockSpec((tk, tn), lambda i, j, kk: (kk, j)))
    elif mode == "nt":
        a_spec = pl.BlockSpec((tm, tk), lambda i, j, kk: (i, kk))
        b_spec = (pl.BlockSpec((kb, tn, tk // kb), lambda i, j, kk: (kk, j, 0)) if b_blocked
                  else pl.BlockSpec((tn, tk), lambda i, j, kk: (j, kk)))
    else:
        a_spec = pl.BlockSpec((tk, tm), lambda i, j, kk: (kk, i))
        b_spec = pl.BlockSpec((tk, tn), lambda i, j, kk: (kk, j))
    if out_blocked:
        out_spec = pl.BlockSpec((None, tm, tn), lambda i, j, kk: (j, i, 0))
        out_shape = (gn, m, tn)
    else:
        out_spec = pl.BlockSpec((tm, tn), lambda i, j, kk: (i, j))
        out_shape = (m, n)
    extra_specs = [pl.BlockSpec((tm, tn), functools.partial(lambda i, j, kk, off: (i, j + off), off=off))
                   for _, off in extras]
    n_extra, n_out = len(extras), len(out_dtypes)
    n_after = 0 if after is None else 1
    places = out_places if out_places is not None else (None,) * n_out

    def body(a_ref, b_ref, *rest):
        extra_refs = rest[:n_extra]
        out_refs = rest[n_extra + n_after:n_extra + n_after + n_out]

        def finish(acc):
            if epilogue is None:
                res = (acc,)
            else:
                res = epilogue(acc, *[e[...] for e in extra_refs])
            for o_ref, r in zip(out_refs, res):
                o_ref[...] = r.astype(o_ref.dtype)

        if row_splits > 1:
            strip = tm // row_splits
            for h in range(row_splits):
                rows = slice(h * strip, (h + 1) * strip)
                acc = lax.dot_general(a_ref[rows, :], b_ref[...], dims, preferred_element_type=F32)
                res = epilogue(acc, *[e[rows, :] for e in extra_refs])
                for o_ref, r in zip(out_refs, res):
                    o_ref[rows, :] = r.astype(o_ref.dtype)
            return
        if mode == "nt" and b_blocked:
            bk = tk // kb
            part = lax.dot_general(a_ref[:, :bk], b_ref[0], dims, preferred_element_type=F32)
            for h in range(1, kb):
                part = part + lax.dot_general(a_ref[:, h * bk:(h + 1) * bk], b_ref[h], dims,
                                              preferred_element_type=F32)
        else:
            part = lax.dot_general(a_ref[...], b_ref[...], dims, preferred_element_type=F32)
        if gk == 1:
            finish(part)
        else:
            acc_ref = rest[-1]
            kk = pl.program_id(2)

            @pl.when(kk == 0)
            def _():
                acc_ref[...] = part

            @pl.when(kk > 0)
            def _():
                acc_ref[...] += part

            @pl.when(kk == gk - 1)
            def _():
                finish(acc_ref[...])

    outs = pl.pallas_call(
        body, name=name, grid=(gm, gn, gk),
        in_specs=[a_spec, b_spec] + extra_specs + [ANY_SPEC] * n_after,
        out_specs=[out_spec if place is None else
                   pl.BlockSpec((tm, tn), functools.partial(lambda i, j, kk, off: (i, j + off), off=place[1] // tn))
                   for place in places],
        out_shape=[jax.ShapeDtypeStruct(out_shape if place is None else (m, place[0]), dt)
                   for dt, place in zip(out_dtypes, places)],
        scratch_shapes=[pltpu.VMEM((tm, tn), F32)] if gk > 1 else [],
        compiler_params=_cparams("parallel", "parallel", "arbitrary"),
    )(a, b, *[e for e, _ in extras], *([] if after is None else [after]))
    return outs[0] if n_out == 1 else outs


def _mm_slots(a, w, slots, *, over, tm, tn=None, out_dtype=F32, epilogue=None, base=None, name):
    m = a.shape[0]
    ns = slots.shape[0]
    n_slots, w1, w2 = w.shape
    assert m % tm == 0
    if over == "n":
        k, bn = w1, w2

        def body(slots_ref, a_ref, w_ref, *rest):
            out_ref = rest[-1]
            acc = jnp.dot(a_ref[...], w_ref[...], preferred_element_type=F32)
            out_ref[...] = (acc if epilogue is None else epilogue(acc)).astype(out_ref.dtype)

        in_specs = [pl.BlockSpec((tm, k), lambda i, j, s: (i, 0)),
                    pl.BlockSpec((None, k, bn), lambda i, j, s: (s[j], 0, 0))]
        args = [a, w]
        aliases = {}
        if base is not None:
            in_specs.append(ANY_SPEC)
            args.append(base)
            aliases = {3: 0}
        return pl.pallas_call(
            body, name=name,
            grid_spec=pltpu.PrefetchScalarGridSpec(
                num_scalar_prefetch=1, grid=(m // tm, ns), in_specs=in_specs,
                out_specs=pl.BlockSpec((tm, bn), lambda i, j, s: (i, s[j]))),
            out_shape=jax.ShapeDtypeStruct((m, n_slots * bn), out_dtype),
            input_output_aliases=aliases,
            compiler_params=_cparams("parallel", "arbitrary"),
        )(slots, *args)

    bk, n = w1, w2
    tn = n if tn is None else tn
    assert n % tn == 0

    def body(slots_ref, a_ref, w_ref, *rest):
        out_ref, acc_ref = rest[-2], rest[-1]
        kk = pl.program_id(2)
        part = jnp.dot(a_ref[...], w_ref[...], preferred_element_type=F32)

        @pl.when(kk == 0)
        def _():
            acc_ref[...] = part if base is None else part + rest[0][...]

        @pl.when(kk > 0)
        def _():
            acc_ref[...] += part

        @pl.when(kk == ns - 1)
        def _():
            out_ref[...] = acc_ref[...].astype(out_ref.dtype)

    in_specs = [pl.BlockSpec((tm, bk), lambda i, j, kk, s: (i, s[kk])),
                pl.BlockSpec((None, bk, tn), lambda i, j, kk, s: (s[kk], 0, j))]
    args = [a, w]
    if base is not None:
        in_specs.append(pl.BlockSpec((tm, tn), lambda i, j, kk, s: (i, j)))
        args.append(base)
    return pl.pallas_call(
        body, name=name,
        grid_spec=pltpu.PrefetchScalarGridSpec(
            num_scalar_prefetch=1, grid=(m // tm, n // tn, ns), in_specs=in_specs,
            out_specs=pl.BlockSpec((tm, tn), lambda i, j, kk, s: (i, j)),
            scratch_shapes=[pltpu.VMEM((tm, tn), F32)]),
        out_shape=jax.ShapeDtypeStruct((m, n), out_dtype),
        compiler_params=_cparams("parallel", "parallel", "arbitrary"),
    )(slots, *args)


def _gate_mix(ya_pre, s, wpo, wco, proj, d_model, name):
    lp, width = ya_pre.shape
    nb, _, bn = wpo.shape
    ga_off = (proj.shape[1] - 2 * d_model) // bn
    gb_off = (proj.shape[1] - d_model) // bn

    n_strips = 4 if lp % 64 == 0 else 1

    def body(ya_ref, s_ref, wpo_ref, wco_ref, ga_ref, gb_ref, m_ref, y_a_ref, y_b_ref):
        strip = lp // n_strips
        for h in range(n_strips):
            rows = slice(h * strip, (h + 1) * strip)
            y_a = jnp.dot(ya_ref[rows, :], wpo_ref[...], preferred_element_type=F32)
            y_b = jnp.dot(s_ref[rows, :], wco_ref[...], preferred_element_type=F32)
            m = jax.nn.sigmoid(ga_ref[rows, :]) * y_a + jax.nn.sigmoid(gb_ref[rows, :]) * y_b
            m_ref[rows, :] = m.astype(BF16)
            y_a_ref[rows, :] = y_a.astype(BF16)
            y_b_ref[rows, :] = y_b.astype(BF16)

    act_spec = pl.BlockSpec((lp, width), lambda j: (0, 0))
    w_spec = pl.BlockSpec((None, width, bn), lambda j: (j, 0, 0))
    out_spec = pl.BlockSpec((lp, bn), lambda j: (0, j))
    return pl.pallas_call(
        body, name=name, grid=(nb,),
        in_specs=[act_spec, act_spec, w_spec, w_spec,
                  pl.BlockSpec((lp, bn), lambda j: (0, j + ga_off)),
                  pl.BlockSpec((lp, bn), lambda j: (0, j + gb_off))],
        out_specs=[out_spec] * 3,
        out_shape=[jax.ShapeDtypeStruct((lp, nb * bn), BF16)] * 3,
        compiler_params=_cparams("parallel"),
    )(ya_pre, s, wpo, wco, proj, proj)


def _rms_stats(x):
    return lax.rsqrt(jnp.mean(x * x, axis=-1, keepdims=True) + RMS_EPS)


def _rms_bwd(x, g, dy):
    r = _rms_stats(x)
    nrm = x * r
    dn = dy * g
    dx = r * (dn - nrm * jnp.mean(dn * nrm, axis=-1, keepdims=True))
    return dx, dy * nrm


def _rowwise(body, ins, outs, accs, *, lp, name):
    tr = _row_tile(lp, max(a.shape[1] for a in ins))
    n_in, n_out, n_acc = len(ins), len(outs), len(accs)

    def kernel_body(*refs):
        i = pl.program_id(0)
        acc_refs = refs[n_in + n_out:]

        @pl.when(i == 0)
        def _():
            for r in acc_refs:
                r[...] = jnp.zeros_like(r)

        body(i * tr, refs[:n_in], refs[n_in:n_in + n_out], acc_refs)

    in_specs = []
    for a in ins:
        if a.shape[0] == lp:
            in_specs.append(pl.BlockSpec((tr, a.shape[1]), lambda i: (i, 0)))
        else:
            in_specs.append(pl.BlockSpec(a.shape, lambda i: (0, 0)))
    out_specs = [pl.BlockSpec((tr, w), lambda i: (i, 0)) for w, _ in outs]
    out_specs += [pl.BlockSpec((SUBLANES, w), lambda i: (0, 0)) for w in accs]
    out_shape = [jax.ShapeDtypeStruct((lp, w), d) for w, d in outs]
    out_shape += [jax.ShapeDtypeStruct((SUBLANES, w), F32) for w in accs]
    return pl.pallas_call(
        kernel_body, name=name, grid=(lp // tr,), in_specs=in_specs, out_specs=out_specs,
        out_shape=out_shape, compiler_params=_cparams("arbitrary"),
    )(*ins)


SHIFT_TILE = 128


def _shifted_specs(width, n_big, n_small):
    per = SHIFT_TILE // N_META
    small = pl.BlockSpec((N_META, width), lambda i: (jnp.clip(per * i - 1, 0, n_small - 1), 0))
    big = pl.BlockSpec((SHIFT_TILE, width), lambda i: (jnp.minimum(i, n_big - 1), 0))
    return small, big


def _rms_pre(meta_full, x2, g, lp, seq):
    d = x2.shape[1]
    assert seq % SHIFT_TILE == 0 and lp % SHIFT_TILE == 0 and SHIFT_TILE % N_META == 0

    def body(meta_ref, xs_ref, xb_ref, g_ref, h0_ref, u1_ref):
        i = pl.program_id(0)
        head = jnp.where(i == 0, meta_ref[...], xs_ref[...])
        rows = jnp.concatenate([head, xb_ref[:SHIFT_TILE - N_META, :]], axis=0)
        r = i * SHIFT_TILE + lax.broadcasted_iota(jnp.int32, (SHIFT_TILE, 1), 0)
        rows = jnp.where(r < N_META + seq, rows, 0.0)
        h0_ref[...] = rows
        u1_ref[...] = (rows * _rms_stats(rows) * g_ref[...]).astype(BF16)

    small, big = _shifted_specs(d, seq // SHIFT_TILE, seq // N_META)
    tile = pl.BlockSpec((SHIFT_TILE, d), lambda i: (i, 0))
    return pl.pallas_call(
        body, name="rms_pre", grid=(lp // SHIFT_TILE,),
        in_specs=[pl.BlockSpec((N_META, d), lambda i: (0, 0)), small, big, pl.BlockSpec((1, d), lambda i: (0, 0))],
        out_specs=[tile, tile],
        out_shape=[jax.ShapeDtypeStruct((lp, d), F32), jax.ShapeDtypeStruct((lp, d), BF16)],
        compiler_params=_cparams("parallel"),
    )(meta_full, x2, x2, g)


def _post_mix(o, h0, g_post_mix, g_pre_mlp, lp):
    d = h0.shape[1]

    def body(row0, ins, outs, accs):
        o_ref, h0_ref, g1_ref, g2_ref = ins
        o_v = o_ref[...]
        h1 = h0_ref[...] + o_v * _rms_stats(o_v) * g1_ref[...]
        outs[0][...] = h1
        outs[1][...] = (h1 * _rms_stats(h1) * g2_ref[...]).astype(BF16)

    return _rowwise(body, [o, h0, g_post_mix, g_pre_mlp], [(d, F32), (d, BF16)], [], lp=lp, name="post_mix")


def _loss_head(f, h1, target, g_post_mlp, lp, seq):
    d = f.shape[1]

    def body(f_ref, h1_ref, ts_ref, tb_ref, g_ref, df_ref, dh_ref, dg_ref, loss_ref):
        i = pl.program_id(0)

        @pl.when(i == 0)
        def _():
            dg_ref[...] = jnp.zeros_like(dg_ref)
            loss_ref[...] = jnp.zeros_like(loss_ref)

        f_v, g = f_ref[...], g_ref[...]
        r = _rms_stats(f_v)
        nrm = f_v * r
        rows = i * SHIFT_TILE + lax.broadcasted_iota(jnp.int32, (SHIFT_TILE, 1), 0)
        valid = (rows >= N_META) & (rows < N_META + seq)
        tgt = jnp.concatenate([ts_ref[...], tb_ref[:SHIFT_TILE - N_META, :]], axis=0)
        err = jnp.where(valid, h1_ref[...] + nrm * g - tgt, 0.0)
        loss_ref[...] += 0.5 * jnp.sum(jnp.mean(err * err, axis=-1, keepdims=True))
        dy = err * (1.0 / d)
        dn = dy * g
        df_ref[...] = (r * (dn - nrm * jnp.mean(dn * nrm, axis=-1, keepdims=True))).astype(BF16)
        dh_ref[...] = dy
        dg_ref[...] += _rowsum8(dy * nrm)

    small, big = _shifted_specs(d, seq // SHIFT_TILE, seq // N_META)
    tile = pl.BlockSpec((SHIFT_TILE, d), lambda i: (i, 0))
    return pl.pallas_call(
        body, name="loss_head", grid=(lp // SHIFT_TILE,),
        in_specs=[tile, tile, small, big, pl.BlockSpec((1, d), lambda i: (0, 0))],
        out_specs=[tile, tile, pl.BlockSpec((SUBLANES, d), lambda i: (0, 0)),
                   pl.BlockSpec((SUBLANES, LANES), lambda i: (0, 0))],
        out_shape=[jax.ShapeDtypeStruct((lp, d), BF16), jax.ShapeDtypeStruct((lp, d), F32),
                   jax.ShapeDtypeStruct((SUBLANES, d), F32), jax.ShapeDtypeStruct((SUBLANES, LANES), F32)],
        compiler_params=_cparams("arbitrary"),
    )(f, h1, target, target, g_post_mlp)


def _mid_bwd(du2, h1, dh, o, g_pre_mlp, g_post_mix, lp):
    d = h1.shape[1]

    def body(row0, ins, outs, accs):
        du2_ref, h1_ref, dh_ref, o_ref, g2_ref, g1_ref = ins
        dx2, dg2 = _rms_bwd(h1_ref[...], g2_ref[...], du2_ref[...])
        dh1 = dh_ref[...] + dx2
        do, dg1 = _rms_bwd(o_ref[...], g1_ref[...], dh1)
        outs[0][...] = dh1
        outs[1][...] = do.astype(BF16)
        accs[0][...] += _rowsum8(dg2)
        accs[1][...] += _rowsum8(dg1)

    return _rowwise(body, [du2, h1, dh, o, g_pre_mlp, g_post_mix], [(d, F32), (d, BF16)], [d, d], lp=lp,
                    name="mid_bwd")


def _pre_mix_bwd(du1, h0, dh1, g_pre_mix, seq):
    d = h0.shape[1]
    per = SHIFT_TILE // N_META
    assert seq % SHIFT_TILE == 0

    def body(du_b, h_b, dh_b, du_n, h_n, dh_n, du_m, h_m, dh_m, g_ref, gx_ref, gm_ref, dg_ref):
        i = pl.program_id(0)
        g = g_ref[...]

        @pl.when(i == 0)
        def _():
            dx, dg = _rms_bwd(h_m[...], g, du_m[...])
            gm_ref[...] = dh_m[...] + dx
            dg_ref[...] = _rowsum8(dg)

        rows = lambda big, nxt: jnp.concatenate([big[N_META:, :], nxt[...]], axis=0)
        dx, dg = _rms_bwd(rows(h_b, h_n), g, rows(du_b, du_n))
        gx_ref[...] = rows(dh_b, dh_n) + dx
        dg_ref[...] += _rowsum8(dg)

    big = pl.BlockSpec((SHIFT_TILE, d), lambda i: (i, 0))
    nxt = pl.BlockSpec((N_META, d), lambda i: (per * (i + 1), 0))
    first = pl.BlockSpec((N_META, d), lambda i: (0, 0))
    return pl.pallas_call(
        body, name="pre_mix_bwd", grid=(seq // SHIFT_TILE,),
        in_specs=[big] * 3 + [nxt] * 3 + [first] * 3 + [pl.BlockSpec((1, d), lambda i: (0, 0))],
        out_specs=[big, first, pl.BlockSpec((SUBLANES, d), lambda i: (0, 0))],
        out_shape=[jax.ShapeDtypeStruct((seq, d), F32), jax.ShapeDtypeStruct((N_META, d), F32),
                   jax.ShapeDtypeStruct((SUBLANES, d), F32)],
        compiler_params=_cparams("arbitrary"),
    )(du1, h0, dh1, du1, h0, dh1, du1, h0, dh1, g_pre_mix)


def _ln_stats(c):
    mu = jnp.mean(c, axis=-1, keepdims=True)
    var = jnp.mean(jnp.square(c - mu), axis=-1, keepdims=True)
    return mu, lax.rsqrt(var + LN_EPS)


def _ln_silu(c, ln_g, ln_b, lp):
    w = c.shape[1]

    def body(row0, ins, outs, accs):
        c_ref, g_ref, b_ref = ins
        c_v = c_ref[...]
        mu, rstd = _ln_stats(c_v)
        ln = (c_v - mu) * rstd * g_ref[...] + b_ref[...]
        outs[0][...] = (ln * jax.nn.sigmoid(ln)).astype(BF16)

    return _rowwise(body, [c, ln_g, ln_b], [(w, BF16)], [], lp=lp, name="ln_silu")[0]


def _ln_silu_bwd(c, ds, ln_g, ln_b, lp):
    w = c.shape[1]

    def body(row0, ins, outs, accs):
        c_ref, ds_ref, g_ref, b_ref = ins
        c_v, g = c_ref[...], g_ref[...]
        mu, rstd = _ln_stats(c_v)
        nrm = (c_v - mu) * rstd
        ln = nrm * g + b_ref[...]
        sig = jax.nn.sigmoid(ln)
        dln = ds_ref[...] * (sig * (1.0 + ln * (1.0 - sig)))
        dn = dln * g
        dc = rstd * (dn - jnp.mean(dn, axis=-1, keepdims=True) - nrm * jnp.mean(dn * nrm, axis=-1, keepdims=True))
        outs[0][...] = dc
        accs[0][...] += _rowsum8(dln * nrm)
        accs[1][...] += _rowsum8(dln)
        accs[2][...] += _rowsum8(dc)

    return _rowwise(body, [c, ds, ln_g, ln_b], [(w, F32)], [w, w, w], lp=lp, name="ln_silu_bwd")


def _chunk_with_history(ref, i, cols=slice(None)):
    t0 = pl.multiple_of(i * ROW_CHUNK, ROW_CHUNK)
    lo0 = pl.multiple_of(jnp.maximum(t0 - HALO, 0), SUBLANES)
    lo = jnp.where(i > 0, ref[pl.ds(lo0, HALO), cols], 0.0)
    return jnp.concatenate([lo, ref[pl.ds(t0, ROW_CHUNK), cols]], axis=0)


def _chunk_with_future(ref, i, n_chunks, cols=slice(None)):
    t0 = pl.multiple_of(i * ROW_CHUNK, ROW_CHUNK)
    hi0 = pl.multiple_of(jnp.minimum(t0 + ROW_CHUNK, (n_chunks - 1) * ROW_CHUNK), SUBLANES)
    hi = jnp.where(i < n_chunks - 1, ref[pl.ds(hi0, HALO), cols], 0.0)
    return jnp.concatenate([ref[pl.ds(t0, ROW_CHUNK), cols], hi], axis=0)


def _inv_count(t0, n_rows, window):
    pos = t0 + lax.broadcasted_iota(jnp.int32, (n_rows, 1), 0)
    return 1.0 / jnp.minimum(pos + 1, window).astype(F32)


def _pool_delta(z_hist, t0, window):
    s = z_hist
    sh = 1
    while sh < window:
        s = s + pltpu.roll(s, sh, 0)
        sh *= 2
    cur = z_hist[HALO:, :]
    return s[HALO:, :] * _inv_count(t0, ROW_CHUNK, window) - cur


def _pool_fwd(proj, wpg, pool_scale, lp):
    n_grp, gdim, _ = wpg.shape
    width = n_grp * gdim
    n_chunks = lp // ROW_CHUNK

    def body(z_ref, w_ref, sc_ref, out_ref):
        for g, window in enumerate(POOL_WINDOWS):
            cols = slice(g * gdim, (g + 1) * gdim)

            def chunk(i, carry, cols=cols, g=g, window=window):
                t0 = pl.multiple_of(i * ROW_CHUNK, ROW_CHUNK)
                d = _pool_delta(_chunk_with_history(z_ref, i, cols), t0, window)
                q = jnp.dot(d.astype(BF16), w_ref[g], preferred_element_type=F32)
                out_ref[pl.ds(t0, ROW_CHUNK), cols] = (q * sc_ref[:, cols]).astype(BF16)
                return carry

            lax.fori_loop(0, n_chunks, chunk, 0)

    return pl.pallas_call(
        body, name="pool_fwd", grid=(1,),
        in_specs=[pl.BlockSpec((lp, width), lambda i: (0, 0)),
                  pl.BlockSpec(wpg.shape, lambda i: (0, 0, 0)),
                  pl.BlockSpec(pool_scale.shape, lambda i: (0, 0))],
        out_specs=pl.BlockSpec((lp, width), lambda i: (0, 0)),
        out_shape=jax.ShapeDtypeStruct((lp, width), BF16),
        compiler_params=_cparams("arbitrary"),
    )(proj, wpg, pool_scale)


def _pool_bwd(proj, d_ya, wpg, pool_scale, dproj, lp):
    n_grp, gdim, _ = wpg.shape
    width = n_grp * gdim
    n_chunks = lp // ROW_CHUNK
    ext = ROW_CHUNK + HALO

    def body(z_ref, dya_ref, w_ref, sc_ref, _, dz_ref, dw_ref, dsc_ref):
        dw_ref[...] = jnp.zeros_like(dw_ref)
        dsc_ref[...] = jnp.zeros_like(dsc_ref)
        for g, window in enumerate(POOL_WINDOWS):
            cols = slice(g * gdim, (g + 1) * gdim)

            def chunk(i, carry, cols=cols, g=g, window=window):
                t0 = pl.multiple_of(i * ROW_CHUNK, ROW_CHUNK)
                w_g = w_ref[g]
                scale = sc_ref[:, cols]
                d = _pool_delta(_chunk_with_history(z_ref, i, cols), t0, window).astype(BF16)
                dya_ext = _chunk_with_future(dya_ref, i, n_chunks, cols)
                dya = dya_ext[:ROW_CHUNK, :]
                q = jnp.dot(d, w_g, preferred_element_type=F32)
                dsc_ref[:, cols] += _rowsum8(dya * q)
                e_ext = (dya_ext * scale).astype(BF16)
                dw_ref[g] += lax.dot_general(d, e_ext[:ROW_CHUNK, :], (((0,), (0,)), ((), ())),
                                             preferred_element_type=F32)
                dd_ext = lax.dot_general(e_ext, w_g, (((1,), (1,)), ((), ())), preferred_element_type=F32)
                s = dd_ext * _inv_count(t0, ext, window)
                sh = 1
                while sh < window:
                    s = s + pltpu.roll(s, ext - sh, 0)
                    sh *= 2
                dz_ref[pl.ds(t0, ROW_CHUNK), cols] = (s[:ROW_CHUNK, :] - dd_ext[:ROW_CHUNK, :]).astype(BF16)
                return carry

            lax.fori_loop(0, n_chunks, chunk, 0)

    blk = pl.BlockSpec((lp, width), lambda i: (0, 0))
    return pl.pallas_call(
        body, name="pool_bwd", grid=(1,),
        in_specs=[blk, blk, pl.BlockSpec(wpg.shape, lambda i: (0, 0, 0)),
                  pl.BlockSpec(pool_scale.shape, lambda i: (0, 0)), ANY_SPEC],
        out_specs=[blk, pl.BlockSpec(wpg.shape, lambda i: (0, 0, 0)),
                   pl.BlockSpec((SUBLANES, width), lambda i: (0, 0))],
        out_shape=[jax.ShapeDtypeStruct(dproj.shape, BF16), jax.ShapeDtypeStruct(wpg.shape, F32),
                   jax.ShapeDtypeStruct((SUBLANES, width), F32)],
        input_output_aliases={4: 0},
        compiler_params=_cparams("arbitrary"),
    )(proj, d_ya, wpg, pool_scale, dproj)


def _conv_fwd(proj, w_dw, b_dw, lp, width, v_col0):
    n_chunks = lp // ROW_CHUNK
    v_blk0, g_blk0 = v_col0 // LANES, (v_col0 + width) // LANES

    def body(v_ref, gc_ref, w_ref, b_ref, c_ref, a_pad):
        a_pad[pl.ds(0, HALO), :] = jnp.zeros((HALO, LANES), F32)
        a_pad[pl.ds(HALO, lp), :] = v_ref[...] * jax.nn.sigmoid(gc_ref[...])

        def chunk(i, carry):
            t0 = pl.multiple_of(i * ROW_CHUNK, ROW_CHUNK)
            hist = a_pad[pl.ds(t0, ROW_CHUNK + HALO), :]
            acc = jnp.zeros((ROW_CHUNK, LANES), F32)
            for k in range(CONV_KERNEL):
                acc = acc + w_ref[k:k + 1, :] * pltpu.roll(hist, CONV_KERNEL - 1 - k, 0)[HALO:, :]
            c_ref[pl.ds(t0, ROW_CHUNK), :] = acc + b_ref[...]
            return carry

        lax.fori_loop(0, n_chunks, chunk, 0)

    return pl.pallas_call(
        body, name="conv_fwd", grid=(width // LANES,),
        in_specs=[pl.BlockSpec((lp, LANES), lambda j: (0, j + v_blk0)),
                  pl.BlockSpec((lp, LANES), lambda j: (0, j + g_blk0)),
                  pl.BlockSpec((CONV_TAPS_PADDED, LANES), lambda j: (0, j)),
                  pl.BlockSpec((1, LANES), lambda j: (0, j))],
        out_specs=pl.BlockSpec((lp, LANES), lambda j: (0, j)),
        out_shape=jax.ShapeDtypeStruct((lp, width), F32),
        scratch_shapes=[pltpu.VMEM((lp + HALO, LANES), F32)],
        compiler_params=_cparams("parallel"),
    )(proj, proj, w_dw, b_dw)


def _conv_bwd(proj, dc, w_dw, dproj, lp, width, v_col0):
    n_chunks = lp // ROW_CHUNK
    ext = ROW_CHUNK + HALO
    v_blk0, g_blk0 = v_col0 // LANES, (v_col0 + width) // LANES

    def body(v_ref, gc_ref, dc_ref, w_ref, _, dv_ref, dgc_ref, dw_ref, a_pad, dc_pad, dw_acc):
        sig = jax.nn.sigmoid(gc_ref[...])
        a_pad[pl.ds(0, HALO), :] = jnp.zeros((HALO, LANES), F32)
        a_pad[pl.ds(HALO, lp), :] = v_ref[...] * sig
        dc_pad[pl.ds(0, lp), :] = dc_ref[...]
        dc_pad[pl.ds(lp, HALO), :] = jnp.zeros((HALO, LANES), F32)
        dw_acc[...] = jnp.zeros_like(dw_acc)

        def chunk(i, carry):
            t0 = pl.multiple_of(i * ROW_CHUNK, ROW_CHUNK)
            hist = a_pad[pl.ds(t0, ext), :]
            fut = dc_pad[pl.ds(t0, ext), :]
            dc_cur = fut[:ROW_CHUNK, :]
            da = jnp.zeros((ROW_CHUNK, LANES), F32)
            for k in range(CONV_KERNEL):
                lag = CONV_KERNEL - 1 - k
                da = da + w_ref[k:k + 1, :] * pltpu.roll(fut, (ext - lag) % ext, 0)[:ROW_CHUNK, :]
                dw_acc[pl.ds(SUBLANES * k, SUBLANES), :] += _rowsum8(dc_cur * pltpu.roll(hist, lag, 0)[HALO:, :])
            rows = pl.ds(t0, ROW_CHUNK)
            sg = jax.nn.sigmoid(gc_ref[rows, :])
            dv_ref[rows, :] = (da * sg).astype(BF16)
            dgc_ref[rows, :] = (da * v_ref[rows, :] * sg * (1.0 - sg)).astype(BF16)
            return carry

        lax.fori_loop(0, n_chunks, chunk, 0)
        dw_ref[...] = dw_acc[...].reshape(CONV_TAPS_PADDED, SUBLANES, LANES).sum(axis=1)

    col = lambda j: (0, j)
    return pl.pallas_call(
        body, name="conv_bwd", grid=(width // LANES,),
        in_specs=[pl.BlockSpec((lp, LANES), lambda j: (0, j + v_blk0)),
                  pl.BlockSpec((lp, LANES), lambda j: (0, j + g_blk0)),
                  pl.BlockSpec((lp, LANES), col),
                  pl.BlockSpec((CONV_TAPS_PADDED, LANES), col), ANY_SPEC],
        out_specs=[pl.BlockSpec((lp, LANES), lambda j: (0, j + v_blk0)), pl.BlockSpec((lp, LANES), col),
                   pl.BlockSpec((CONV_TAPS_PADDED, LANES), col)],
        out_shape=[jax.ShapeDtypeStruct(dproj.shape, BF16), jax.ShapeDtypeStruct((lp, width), BF16),
                   jax.ShapeDtypeStruct((CONV_TAPS_PADDED, width), F32)],
        scratch_shapes=[pltpu.VMEM((lp + HALO, LANES), F32), pltpu.VMEM((lp + HALO, LANES), F32),
                        pltpu.VMEM((CONV_TAPS_PADDED * SUBLANES, LANES), F32)],
        input_output_aliases={4: 0},
        compiler_params=_cparams("parallel"),
    )(proj, proj, dc, w_dw, dproj)


def _place_columns(dst, pieces, name):
    m = dst.shape[0]
    tile = 512
    counts = [p.shape[1] // tile for p, _ in pieces]
    starts = [sum(counts[:i]) for i in range(len(pieces))]
    n_steps = sum(counts)

    def local(s, i):
        return jnp.clip(s - starts[i], 0, counts[i] - 1)

    def out_index(s):
        blk = pieces[0][1] // tile + local(s, 0)
        for i in range(1, len(pieces)):
            blk = jnp.where(s >= starts[i], pieces[i][1] // tile + local(s, i), blk)
        return 0, blk

    def body(*refs):
        out_ref = refs[-1]
        s = pl.program_id(0)
        for i in range(len(pieces)):
            @pl.when((s >= starts[i]) & (s < starts[i] + counts[i]))
            def _(i=i):
                out_ref[...] = refs[i][...]

    return pl.pallas_call(
        body, name=name, grid=(n_steps,),
        in_specs=[pl.BlockSpec((m, tile), functools.partial(lambda s, i: (0, local(s, i)), i=i))
                  for i in range(len(pieces))] + [ANY_SPEC],
        out_specs=pl.BlockSpec((m, tile), out_index),
        out_shape=jax.ShapeDtypeStruct(dst.shape, dst.dtype),
        input_output_aliases={len(pieces): 0},
        compiler_params=_cparams("arbitrary"),
    )(*[p for p, _ in pieces], dst)


def _adamw_math(w, g, m, v):
    m = ADAM_B1 * m + (1.0 - ADAM_B1) * g
    v = ADAM_B2 * v + (1.0 - ADAM_B2) * jnp.square(g)
    m_hat = m / (1.0 - ADAM_B1 ** ADAM_STEP)
    v_hat = v / (1.0 - ADAM_B2 ** ADAM_STEP)
    delta = -ADAM_LR * (m_hat / (jnp.sqrt(v_hat) + ADAM_EPS) + ADAM_WD * w)
    return delta, m, v


def _pair_sum(own, recv, where, name):
    _, _, rows, cols = own.shape
    tr = _row_tile(rows, cols, 1024 * 1024)

    def body(where_ref, own_ref, recv_ref, out_ref):
        out_ref[...] = (own_ref[...].astype(F32) + recv_ref[...].astype(F32)).astype(BF16)

    return pl.pallas_call(
        body, name=name,
        grid_spec=pltpu.PrefetchScalarGridSpec(
            num_scalar_prefetch=1, grid=(3, rows // tr),
            in_specs=[pl.BlockSpec((None, None, tr, cols), lambda r, i, wh: (wh[2 + r], wh[0], i, 0)),
                      pl.BlockSpec((None, tr, cols), lambda r, i, wh: (wh[2 + r], i, 0))],
            out_specs=pl.BlockSpec((None, tr, cols), lambda r, i, wh: (wh[2 + r], i, 0))),
        out_shape=jax.ShapeDtypeStruct((4, rows, cols), BF16),
        compiler_params=_cparams("parallel", "parallel"),
    )(where, own, recv)


def _adamw_big(w, m, v, own, from_sibling, recv3, where, name):
    rows, cols = w.shape
    tr = _row_tile(rows, cols, 256 * 1024)

    def body(where_ref, w_ref, m_ref, v_ref, own_ref, sib_ref, r_ref, g_out, d_out, m_out, v_out):
        g = own_ref[...].astype(F32) + sib_ref[...].astype(F32)
        for r in range(3):
            g = g + r_ref[r].astype(F32)
        delta, m_new, v_new = _adamw_math(w_ref[...], g, m_ref[...], v_ref[...])
        g_out[...] = g
        d_out[...] = delta
        m_out[...] = m_new
        v_out[...] = v_new

    blk = pl.BlockSpec((tr, cols), lambda i, q_ref: (i, 0))
    return pl.pallas_call(
        body, name=name,
        grid_spec=pltpu.PrefetchScalarGridSpec(
            num_scalar_prefetch=1, grid=(rows // tr,),
            in_specs=[blk, blk, blk,
                      pl.BlockSpec((None, None, tr, cols), lambda i, wh: (wh[1], wh[0], i, 0)),
                      pl.BlockSpec((None, tr, cols), lambda i, wh: (wh[1], i, 0)),
                      pl.BlockSpec((3, tr, cols), lambda i, wh: (0, i, 0))],
            out_specs=[blk] * 4),
        out_shape=[jax.ShapeDtypeStruct((rows, cols), F32)] * 4,
        compiler_params=_cparams("parallel"),
    )(where, w, m, v, own, from_sibling, recv3)


def _small_update(me_idx, packed, rep_params, rep_places, meta_wmv, meta_row0, wdw_wmv, wdw_row0, loss_row0):
    n_rep = len(rep_params)
    meta_rows, meta_cols = meta_wmv[0].shape
    wdw_rows, wdw_cols = wdw_wmv[0].shape

    def body(me_ref, *refs):
        pos = 0

        def take(k):
            nonlocal pos
            out = refs[pos:pos + k]
            pos += k
            return out

        rep_in = [take(3) for _ in range(n_rep)]
        rep_g = take(n_rep)
        meta_in, (meta_g,) = take(3), take(1)
        wdw_in, (wdw_g,) = take(3), take(1)
        (loss_ref,) = take(1)
        rep_out = [take(4) for _ in range(n_rep)]
        meta_out, wdw_out = take(4), take(4)
        (loss_out,) = take(1)

        def update(wmv, g, outs):
            delta, m_new, v_new = _adamw_math(wmv[0][...], g, wmv[1][...], wmv[2][...])
            for o_ref, val in zip(outs, (g, delta, m_new, v_new)):
                o_ref[...] = val

        for wmv, g_ref, outs in zip(rep_in, rep_g, rep_out):
            g = jnp.sum(g_ref[0], axis=0, keepdims=True)
            for j in range(1, N_DEV):
                g = g + jnp.sum(g_ref[j], axis=0, keepdims=True)
            update(wmv, g, outs)
        for wmv, g_ref, outs in ((meta_in, meta_g, meta_out), (wdw_in, wdw_g, wdw_out)):
            g = g_ref[0]
            for j in range(1, N_DEV):
                g = g + g_ref[j]
            update(wmv, g, outs)
        total = loss_ref[0]
        for j in range(1, N_DEV):
            total = total + loss_ref[j]
        loss_out[...] = total

    def whole(a):
        nd = a.ndim
        return pl.BlockSpec(a.shape, lambda i, me_ref, nd=nd: (0,) * nd)

    ins, in_specs = [], []
    for wmv in rep_params:
        ins += list(wmv)
        in_specs += [whole(a) for a in wmv]
    for wmv, (row0, col0) in zip(rep_params, rep_places):
        width = wmv[0].shape[1]
        ins.append(packed)
        in_specs.append(pl.BlockSpec((N_DEV, SUBLANES, width),
                                     lambda i, me_ref, rb=row0 // SUBLANES, cb=col0 // width: (0, rb, cb)))
    ins += list(meta_wmv) + [packed]
    in_specs += [whole(a) for a in meta_wmv]
    in_specs.append(pl.BlockSpec((N_DEV, meta_rows, meta_cols),
                                 lambda i, me_ref, rb=meta_row0 // meta_rows: (0, rb, me_ref[0])))
    ins += list(wdw_wmv) + [packed]
    in_specs += [whole(a) for a in wdw_wmv]
    in_specs.append(pl.BlockSpec((N_DEV, wdw_rows, wdw_cols),
                                 lambda i, me_ref, rb=wdw_row0 // wdw_rows: (0, rb, me_ref[0])))
    ins.append(packed)
    in_specs.append(pl.BlockSpec((N_DEV, SUBLANES, LANES), lambda i, me_ref, rb=loss_row0 // SUBLANES: (0, rb, 0)))

    out_shape, out_specs = [], []
    for wmv in list(rep_params) + [meta_wmv, wdw_wmv]:
        out_shape += [jax.ShapeDtypeStruct(wmv[0].shape, F32)] * 4
        out_specs += [whole(wmv[0])] * 4
    out_shape.append(jax.ShapeDtypeStruct((SUBLANES, LANES), F32))
    out_specs.append(pl.BlockSpec((SUBLANES, LANES), lambda i, me_ref: (0, 0)))

    outs = pl.pallas_call(
        body, name="small_update",
        grid_spec=pltpu.PrefetchScalarGridSpec(num_scalar_prefetch=1, grid=(1,), in_specs=in_specs,
                                               out_specs=out_specs),
        out_shape=out_shape, compiler_params=_cparams("arbitrary"),
    )(me_idx, *ins)
    groups = [outs[4 * i:4 * i + 4] for i in range(n_rep + 2)]
    return groups[:n_rep], groups[n_rep], groups[n_rep + 1], outs[-1]


def kernel(x, meta, g_pre_mix, w_in, w_pool_grp, pool_scale, w_pool_out, w_dw, b_dw, conv_ln_g, conv_ln_b, w_conv_out, w_o, g_post_mix, g_pre_mlp, w_up, w_down, g_post_mlp, loss_target, m_meta, m_g_pre_mix, m_w_in, m_w_pool_grp, m_pool_scale, m_w_pool_out, m_w_dw, m_b_dw, m_conv_ln_g, m_conv_ln_b, m_w_conv_out, m_w_o, m_g_post_mix, m_g_pre_mlp, m_w_up, m_w_down, m_g_post_mlp, v_meta, v_g_pre_mix, v_w_in, v_w_pool_grp, v_pool_scale, v_w_pool_out, v_w_dw, v_b_dw, v_conv_ln_g, v_conv_ln_b, v_w_conv_out, v_w_o, v_g_post_mix, v_g_pre_mlp, v_w_up, v_w_down, v_g_post_mlp):
    seq, d = x.shape[1], x.shape[2]
    pool_w = pool_scale.shape[1]
    conv_w = b_dw.shape[1]
    n_grp, grp_rows, gdim = w_pool_grp.shape[1:]
    lp = _round_up(N_META + seq, ROW_CHUNK)
    tm_half = lp // 2 if (lp // 2) % 16 == 0 else lp
    c_idx = lax.axis_index("c").astype(jnp.int32)
    chip_idx = (2 * lax.axis_index("x") + lax.axis_index("y")).astype(jnp.int32)
    me_idx = 2 * chip_idx + c_idx

    pad_taps = ((0, CONV_TAPS_PADDED - CONV_KERNEL), (0, 0))
    big = dict(w_in=w_in[0], w_pool_grp=w_pool_grp[0].reshape(n_grp * grp_rows, gdim), w_pool_out=w_pool_out[0],
               w_conv_out=w_conv_out[0], w_o=w_o[0], w_up=w_up[0], w_down=w_down[0])
    big_names = list(big)
    moments = dict(w_in=(m_w_in, v_w_in), w_pool_grp=(m_w_pool_grp, v_w_pool_grp), w_pool_out=(m_w_pool_out, v_w_pool_out),
                   w_conv_out=(m_w_conv_out, v_w_conv_out), w_o=(m_w_o, v_w_o), w_up=(m_w_up, v_w_up),
                   w_down=(m_w_down, v_w_down))
    slot_idx = me_idx.reshape(1)
    sources = dict(big, meta=meta, w_dw=jnp.pad(w_dw[0], pad_taps))

    def fill(k, after):
        return _fill_slot(sources[k], slot_idx, BF16 if k in big else F32, after, "fill_" + k)

    gather_groups = [["meta", "w_dw"], ["w_in"], ["w_pool_grp", "w_pool_out", "w_conv_out", "w_o"], ["w_up"], ["w_down"]]
    started, token = _gather_start([[fill(k, slot_idx) for k in names] for names in gather_groups[:2]], slot_idx,
                                   "gather_start_first", issue_order=(0, 3, 1, 2))
    started_rest, _ = _gather_start([[fill(k, token) for k in names] for names in gather_groups[2:]], token,
                                    "gather_start_rest")
    started += started_rest
    wg = {}
    x_idx, y_idx = lax.axis_index("x"), lax.axis_index("y")
    at = lambda px, py, pc: 4 * px + 2 * py + pc
    near_slots = jnp.stack([at(x_idx, y_idx, c_idx), at(x_idx, y_idx, 1 - c_idx), at(1 - x_idx, y_idx, c_idx),
                            at(x_idx, 1 - y_idx, c_idx), at(1 - x_idx, y_idx, 1 - c_idx),
                            at(x_idx, 1 - y_idx, 1 - c_idx)]).astype(jnp.int32)
    far_slots = jnp.stack([at(1 - x_idx, 1 - y_idx, c_idx), at(1 - x_idx, 1 - y_idx, 1 - c_idx)]).astype(jnp.int32)

    def gather_whole(gi, after_forward, after_finish):
        send, recv, lands = started[gi]
        fs, fr, lands = _gather_forward(lands, recv, after_forward(), f"gather_forward_{gi}")
        lands = _gather_finish(lands, send, recv, [(ALL_CHIPS, fs, fr)], [(ALL_CHIPS, fs, fr)], after_finish(),
                               f"gather_finish_{gi}")
        wg.update(zip(gather_groups[gi], lands))

    near_state = {}

    def gather_near(gi, after):
        send, recv, lands = started[gi]
        fs, fr, lands = _gather_forward(lands, recv, after, f"gather_forward_near_{gi}", which=NEAR)
        lands = _gather_finish(lands, send, recv, [(NEAR, fs, fr)], [], after, f"gather_finish_near_{gi}",
                               direct_sends=False)
        near_state[gi] = (fs, fr)
        return lands

    def gather_far(gi, lands, after):
        send, recv, _ = started[gi]
        fs, fr, lands = _gather_forward(lands, recv, after, f"gather_forward_far_{gi}", which=FAR)
        lands = _gather_finish(lands, send, recv, [(FAR, fs, fr)], [(NEAR,) + near_state[gi], (FAR, fs, fr)], after,
                               f"gather_finish_far_{gi}", own=False)
        wg.update(zip(gather_groups[gi], lands))
        return lands

    gather_whole(0, lambda: token, lambda: token)
    meta_full = wg["meta"].transpose(1, 0, 2).reshape(N_META, d)
    wdw_full = wg["w_dw"].transpose(1, 0, 2).reshape(CONV_TAPS_PADDED, conv_w)
    target = loss_target[0]
    h0, u1 = _rms_pre(meta_full, x[0], g_pre_mix, lp, seq)
    gather_whole(1, lambda: u1, lambda: u1)
    proj = _mm(u1, wg["w_in"], mode="nn", tm=tm_half, tn=wg["w_in"].shape[2], b_blocked=True, name="mm_proj")
    conv_c = _conv_fwd(proj, wdw_full, b_dw, lp, conv_w, pool_w)
    s_act = _ln_silu(conv_c, conv_ln_g, conv_ln_b, lp)
    gather_whole(2, lambda: proj, lambda: s_act)
    wpg_full = wg["w_pool_grp"].reshape(N_DEV, n_grp, grp_rows, gdim).transpose(1, 0, 2, 3).reshape(n_grp, gdim, gdim)
    w_o_full = wg["w_o"].reshape(d, d)
    ya_pre = _pool_fwd(proj, wpg_full, pool_scale, lp)
    m_mix, y_a, y_b = _gate_mix(ya_pre, s_act, wg["w_pool_out"], wg["w_conv_out"], proj, d, "gate_mix")
    o = _mm(m_mix, w_o_full, mode="nn", tm=tm_half, tn=512, name="mm_o")
    h1, u2 = _post_mix(o, h0, g_post_mix, g_pre_mlp, lp)
    relu2 = lambda acc: jnp.square(jnp.maximum(acc, 0.0))
    (w_up_near,) = gather_near(3, u2)
    act = _mm_slots(u2, w_up_near, near_slots, over="n", tm=tm_half, out_dtype=BF16, epilogue=relu2, name="mm_up_near")
    (w_up_all,) = gather_far(3, [w_up_near], act)
    act = _mm_slots(u2, w_up_all, far_slots, over="n", tm=tm_half, out_dtype=BF16, epilogue=relu2, base=act,
                    name="mm_up_far")
    (w_down_near,) = gather_near(4, act)
    f = _mm_slots(act, w_down_near, near_slots, over="k", tm=tm_half, tn=1024, name="mm_down_near")
    (w_down_all,) = gather_far(4, [w_down_near], f)
    f = _mm_slots(act, w_down_all, far_slots, over="k", tm=tm_half, tn=1024, base=f, name="mm_down_far")
    w_down_full = w_down_all.reshape(-1, d)

    big_out = {}

    def to_sibling(names, grads, after, tag):
        send, recv, grads, lands, token = _pair_exchange_start(grads, after, "grads_to_sibling_start_" + tag)
        return (names, send, recv, grads, lands, tag), token

    where = jnp.stack([c_idx, chip_idx, 2 * (1 - x_idx) + y_idx, 2 * x_idx + (1 - y_idx),
                       2 * (1 - x_idx) + (1 - y_idx)]).astype(jnp.int32)

    def to_owner(handle, after):
        names, send, recv, grads, from_sib, tag = handle
        grads, from_sib = _pair_exchange_finish(grads, from_sib, send, recv, after, "grads_to_sibling_finish_" + tag)
        own = [g.reshape((4, 2) + g.shape[1:]) for g in grads]
        sums = [_pair_sum(o, r, where, "pair_sum_" + k) for k, o, r in zip(names, own, from_sib)]
        send, recv, sums, lands, token = _chip_exchange_start(sums, after, "grads_to_owner_start_" + tag)
        return (names, send, recv, sums, lands, own, from_sib, tag), token

    def update(handle, after):
        names, send, recv, sums, lands, own, from_sib, tag = handle
        got = _chip_exchange_finish(sums, lands, send, recv, after, "grads_to_owner_finish_" + tag)
        for k, o, s, r3 in zip(names, own, from_sib, got):
            w2 = big[k]
            shape = moments[k][0].shape
            outs = _adamw_big(w2, moments[k][0].reshape(w2.shape), moments[k][1].reshape(w2.shape), o, s, r3,
                              where, "adamw_" + k)
            big_out[k] = [a.reshape(shape) for a in outs]
        return big_out[names[-1]][0]

    df, dh, dg_post_mlp, loss_part = _loss_head(f, h1, target, g_post_mlp, lp, seq)
    d_up = _mm(df, w_down_full, mode="nt", tm=tm_half, tn=1024, out_dtypes=(BF16,), extras=[(act, 0)],
               epilogue=lambda acc, a: (acc * (2.0 * jnp.sqrt(a.astype(F32))),), row_splits=2, name="mm_d_up")
    g_w_down = _mm(act, df, mode="tn", tm=1024, tn=1024, out_dtypes=(BF16,), name="mm_g_down")
    sib_down, token = to_sibling(["w_down"], [g_w_down.reshape(N_DEV, -1, d)], slot_idx, "down")
    g_w_up = _mm(u2, d_up, mode="tn", tm=1024, tn=wg["w_up"].shape[2], out_blocked=True, out_dtypes=(BF16,),
                 after=token, name="mm_g_up")
    sib_up, token = to_sibling(["w_up"], [g_w_up], slot_idx, "up")
    pending_down, token = to_owner(sib_down, token)
    du2 = _mm(d_up, wg["w_up"], mode="nt", tm=tm_half, tn=1024, b_blocked=True, kb=2, after=token, name="mm_du2")
    pending_up, token = to_owner(sib_up, du2)
    dh1, do, dg_pre_mlp, dg_post_mix = _mid_bwd(du2, h1, dh, o, g_pre_mlp, g_post_mix, lp)

    def gate_bwd(dm, ga, gb, ya, yb):
        sa, sb = jax.nn.sigmoid(ga), jax.nn.sigmoid(gb)
        return (dm * ya.astype(F32) * sa * (1.0 - sa), dm * yb.astype(F32) * sb * (1.0 - sb), dm * sa, dm * sb)

    gate_tn = 512
    ga_col0, gb_col0 = proj.shape[1] - 2 * d, proj.shape[1] - d
    dproj, d_gb, d_ya, d_yb = _mm(
        do, w_o_full, mode="nt", tm=tm_half, tn=gate_tn, out_dtypes=(BF16,) * 4,
        extras=[(proj, ga_col0 // gate_tn), (proj, gb_col0 // gate_tn), (y_a, 0), (y_b, 0)], epilogue=gate_bwd,
        out_places=((proj.shape[1], ga_col0), None, None, None), after=token, row_splits=2, name="mm_dm")
    g_w_o = _mm(m_mix, do, mode="tn", tm=1024, tn=1024, out_dtypes=(BF16,), name="mm_g_o")
    bn_out = wg["w_pool_out"].shape[2]
    g_w_pool_out = _mm(ya_pre, d_ya, mode="tn", tm=pool_w, tn=bn_out, out_blocked=True, out_dtypes=(BF16,),
                       name="mm_g_pool_out")
    g_w_conv_out = _mm(s_act, d_yb, mode="tn", tm=conv_w, tn=bn_out, out_blocked=True, out_dtypes=(BF16,),
                       name="mm_g_conv_out")
    sib_mix, token = to_sibling(["w_o", "w_pool_out", "w_conv_out"],
                                [g_w_o.reshape(N_DEV, -1, d), g_w_pool_out, g_w_conv_out], slot_idx, "mix")
    d_ya_pre = _mm(d_ya, wg["w_pool_out"], mode="nt", tm=tm_half, tn=pool_w, b_blocked=True, after=token,
                   name="mm_d_ya_pre")
    d_s = _mm(d_yb, wg["w_conv_out"], mode="nt", tm=tm_half, tn=conv_w, b_blocked=True, name="mm_d_s")
    pending_mix, token = to_owner(sib_mix, d_s)
    dproj, g_wpg, d_scale = _pool_bwd(proj, d_ya_pre, wpg_full, pool_scale, dproj, lp)
    dc, d_ln_g, d_ln_b, d_b_dw = _ln_silu_bwd(conv_c, d_s, conv_ln_g, conv_ln_b, lp)
    dproj, dgc, g_wdw = _conv_bwd(proj, dc, wdw_full, dproj, lp, conv_w, pool_w)
    dproj = _place_columns(dproj, [(dgc, pool_w + conv_w), (d_gb, gb_col0)], "place_dproj")
    g_w_in = _mm(u1, dproj, mode="tn", tm=1024, tn=wg["w_in"].shape[2], out_blocked=True, out_dtypes=(BF16,),
                 after=token, name="mm_g_in")
    g_wpg_slots = g_wpg.astype(BF16).reshape(n_grp, N_DEV, grp_rows, gdim).transpose(1, 0, 2, 3)
    sib_in, token = to_sibling(["w_pool_grp", "w_in"],
                               [g_wpg_slots.reshape(N_DEV, n_grp * grp_rows, gdim), g_w_in], slot_idx, "in")
    done = update(pending_down, token)
    pending_in, token = to_owner(sib_in, done)
    done = update(pending_up, token)
    du1 = _mm(dproj, wg["w_in"], mode="nt", tm=tm_half, tn=1024, b_blocked=True, kb=2, after=done, name="mm_du1")
    grad_x2, grad_meta_part, dg_pre_mix = _pre_mix_bwd(du1, h0, dh1, g_pre_mix, seq)
    grad_x = grad_x2[None]

    assert pool_w + conv_w == d and conv_w <= d and LANES <= d
    widen = lambda a: jnp.pad(a, ((0, 0), (0, d - a.shape[1])))
    packed = jnp.concatenate([
        dg_pre_mix, dg_post_mix, dg_pre_mlp, dg_post_mlp,
        jnp.concatenate([d_scale, d_ln_g], axis=1), jnp.concatenate([d_ln_b, d_b_dw], axis=1),
        grad_meta_part, widen(g_wdw), widen(loss_part)], axis=0)
    rep = dict(g_pre_mix=((g_pre_mix, m_g_pre_mix, v_g_pre_mix), (0, 0)),
               g_post_mix=((g_post_mix, m_g_post_mix, v_g_post_mix), (SUBLANES, 0)),
               g_pre_mlp=((g_pre_mlp, m_g_pre_mlp, v_g_pre_mlp), (2 * SUBLANES, 0)),
               g_post_mlp=((g_post_mlp, m_g_post_mlp, v_g_post_mlp), (3 * SUBLANES, 0)),
               pool_scale=((pool_scale, m_pool_scale, v_pool_scale), (4 * SUBLANES, 0)),
               conv_ln_g=((conv_ln_g, m_conv_ln_g, v_conv_ln_g), (4 * SUBLANES, pool_w)),
               conv_ln_b=((conv_ln_b, m_conv_ln_b, v_conv_ln_b), (5 * SUBLANES, 0)),
               b_dw=((b_dw, m_b_dw, v_b_dw), (5 * SUBLANES, conv_w)))
    meta_row0 = 6 * SUBLANES
    wdw_row0 = meta_row0 + N_META
    loss_row0 = wdw_row0 + CONV_TAPS_PADDED
    (small_started,), token = _gather_start([[_fill_slot(packed, slot_idx, F32, slot_idx, "fill_small")]], grad_x2,
                                            "gather_small_start")
    done = update(pending_mix, token)
    done = update(pending_in, done)
    send, recv, lands = small_started
    fsend, frecv, lands = _gather_forward(lands, recv, done, "gather_small_forward")
    (packed_all,) = _gather_finish(lands, send, recv, [(ALL_CHIPS, fsend, frecv)], [(ALL_CHIPS, fsend, frecv)], done,
                                   "gather_small_finish")
    rep_names = list(rep)
    wdw_wmv = [jnp.pad(a[0], pad_taps) for a in (w_dw, m_w_dw, v_w_dw)]
    rep_out, meta_out, wdw_out, loss_blk = _small_update(
        slot_idx, packed_all, [rep[k][0] for k in rep_names], [rep[k][1] for k in rep_names],
        (meta, m_meta, v_meta), meta_row0, wdw_wmv, wdw_row0, loss_row0)
    small_out = dict(zip(rep_names, rep_out))
    small_out["meta"] = meta_out
    small_out["w_dw"] = [a[:CONV_KERNEL][None] for a in wdw_out]

    order = ["meta", "g_pre_mix", "w_in", "w_pool_grp", "pool_scale", "w_pool_out", "w_dw", "b_dw", "conv_ln_g",
             "conv_ln_b", "w_conv_out", "w_o", "g_post_mix", "g_pre_mlp", "w_up", "w_down", "g_post_mlp"]
    by_name = {**big_out, **small_out}
    result = [loss_blk[0, 0], grad_x]
    for kind in range(4):
        result += [by_name[k][kind] for k in order]
    return tuple(result)
```

```python
import functools

import jax
import jax.numpy as jnp
from jax import lax
from jax.experimental import pallas as pl
from jax.experimental.pallas import tpu as pltpu

F32 = jnp.float32
BF16 = jnp.bfloat16
MESH = pl.DeviceIdType.MESH

N_DEV = 8
N_META = 16
POOL_WINDOWS = (2, 4, 8, 16)
CONV_KERNEL = 31
CONV_TAPS_PADDED = 32
RMS_EPS = 1e-6
LN_EPS = 1e-5
ADAM_LR = 0.001
ADAM_B1 = 0.9
ADAM_B2 = 0.999
ADAM_EPS = 1e-08
ADAM_WD = 0.01
ADAM_STEP = 10

LANES = 128
SUBLANES = 8
ROW_CHUNK = 128
HALO = 32
VMEM_LIMIT_BYTES = 56 * 1024 * 1024


def _cparams(*sem):
    return pltpu.CompilerParams(dimension_semantics=sem if sem else None, vmem_limit_bytes=VMEM_LIMIT_BYTES)


def _round_up(n, m):
    return (n + m - 1) // m * m


def _row_tile(rows, cols, max_elems=640 * 1024):
    best = None
    for t in range(16, rows + 1, 16):
        if rows % t == 0 and (best is None or t * cols <= max_elems):
            best = t
    assert best is not None, (rows, cols)
    return best


def _rowsum8(a):
    t, w = a.shape
    return a.reshape(t // SUBLANES, SUBLANES, w).sum(axis=0)


def _mesh_pos():
    return lax.axis_index("x"), lax.axis_index("y"), lax.axis_index("c")


HBM_SPEC = pl.BlockSpec(memory_space=pltpu.HBM)
SEM_SPEC = pl.BlockSpec(memory_space=pltpu.SEMAPHORE)
ANY_SPEC = pl.BlockSpec(memory_space=pl.ANY)
_DATAFLOW = pltpu.SideEffectType.DATAFLOW_SIDE_EFFECTING


def _hbm(a):
    return pltpu.with_memory_space_constraint(a, pltpu.HBM)


def _slot(p):
    return 4 * p[0] + 2 * p[1] + p[2]


def _fill_slot(w, slot_idx, dtype, after, name):
    rows, cols = w.shape
    tr = _row_tile(rows, cols) if rows % 16 == 0 else rows

    def body(idx_ref, w_ref, _, out_ref):
        out_ref[...] = w_ref[...].astype(dtype)

    return pl.pallas_call(
        body, name=name,
        grid_spec=pltpu.PrefetchScalarGridSpec(
            num_scalar_prefetch=1, grid=(rows // tr,),
            in_specs=[pl.BlockSpec((tr, cols), lambda i, idx_ref: (i, 0)), ANY_SPEC],
            out_specs=pl.BlockSpec((None, tr, cols), lambda i, idx_ref: (idx_ref[0], i, 0))),
        out_shape=jax.ShapeDtypeStruct((N_DEV, rows, cols), dtype),
        compiler_params=_cparams("parallel"),
    )(slot_idx, w, after)


def _gather_start(groups, after, name, issue_order=(0, 1, 2, 3)):
    flat = [g for grp in groups for g in grp]
    n, n_grp = len(flat), len(groups)

    def body(*refs):
        lands = refs[:n]
        sems = refs[n + 1:n + 1 + 2 * n_grp]
        token = refs[-1]
        x, y, c = _mesh_pos()
        targets = [(x, y, 1 - c), (1 - x, y, c), (x, 1 - y, c), (1 - x, 1 - y, c)]
        t = 0
        for gi, grp in enumerate(groups):
            for ti in range(len(grp)):
                mine = lands[t].at[_slot((x, y, c))]
                for k in issue_order:
                    pltpu.make_async_remote_copy(
                        src_ref=mine, dst_ref=mine,
                        send_sem=sems[2 * gi].at[4 * ti + k], recv_sem=sems[2 * gi + 1].at[4 * ti + k],
                        device_id=targets[k], device_id_type=MESH).start()
                t += 1
        token[...] = jnp.zeros_like(token)

    sem_shapes = []
    for grp in groups:
        sem_shapes += [pltpu.SemaphoreType.DMA((4 * len(grp),))] * 2
    outs = pl.pallas_call(
        body, name=name,
        out_shape=tuple(sem_shapes + [pltpu.HBM(g.shape, g.dtype) for g in flat]
                        + [jax.ShapeDtypeStruct((SUBLANES, LANES), F32)]),
        in_specs=tuple([HBM_SPEC] * n + [ANY_SPEC]),
        out_specs=tuple([SEM_SPEC] * (2 * n_grp) + [HBM_SPEC] * n + [pl.BlockSpec(memory_space=pltpu.VMEM)]),
        input_output_aliases={i: 2 * n_grp + i for i in range(n)},
        compiler_params=pltpu.CompilerParams(has_side_effects=_DATAFLOW),
    )(*[_hbm(g) for g in flat], after)
    sems, lands, token = outs[:2 * n_grp], outs[2 * n_grp:-1], outs[-1]
    res, t = [], 0
    for gi, grp in enumerate(groups):
        res.append((sems[2 * gi], sems[2 * gi + 1], list(lands[t:t + len(grp)])))
        t += len(grp)
    return res, token


NEAR, FAR, ALL_CHIPS = (0, 1), (2,), (0, 1, 2)


def _gather_forward(lands, recv_sems, after, name, which=ALL_CHIPS):
    n, nw = len(lands), len(which)

    def body(*refs):
        land_refs, recv, _ = refs[:n], refs[n], refs[n + 1]
        fsend, frecv = refs[n + 2], refs[n + 3]
        x, y, c = _mesh_pos()
        chips = [(1 - x, y), (x, 1 - y), (1 - x, 1 - y)]
        for t in range(n):
            for i, j in enumerate(which):
                blk = land_refs[t].at[_slot((*chips[j], c))]
                pltpu.make_async_remote_copy(src_ref=blk, dst_ref=blk, send_sem=fsend.at[nw * t + i],
                                             recv_sem=recv.at[4 * t + 1 + j],
                                             device_id=(x, y, 1 - c), device_id_type=MESH).wait_recv()
                pltpu.make_async_remote_copy(src_ref=blk, dst_ref=blk, send_sem=fsend.at[nw * t + i],
                                             recv_sem=frecv.at[nw * t + i],
                                             device_id=(x, y, 1 - c), device_id_type=MESH).start()

    outs = pl.pallas_call(
        body, name=name,
        out_shape=tuple([pltpu.SemaphoreType.DMA((nw * n,))] * 2 + [pltpu.HBM(g.shape, g.dtype) for g in lands]),
        in_specs=tuple([HBM_SPEC] * n + [SEM_SPEC, ANY_SPEC]),
        out_specs=tuple([SEM_SPEC] * 2 + [HBM_SPEC] * n),
        input_output_aliases={i: 2 + i for i in range(n)},
        compiler_params=pltpu.CompilerParams(has_side_effects=_DATAFLOW),
    )(*lands, recv_sems, after)
    return outs[0], outs[1], list(outs[2:])


def _gather_finish(lands, send_sems, recv_sems, arrivals, sends, after, name, own=True, direct_sends=True):
    n = len(lands)
    fwd = list(arrivals) + list(sends)
    sem_args = [send_sems, recv_sems]
    where = []
    for _, fs, fr in fwd:
        pos = []
        for arr in (fs, fr):
            hit = [i for i, have in enumerate(sem_args) if have is arr]
            if not hit:
                sem_args.append(arr)
                hit = [len(sem_args) - 1]
            pos.append(hit[0])
        where.append(pos)

    def body(*refs):
        land_refs = refs[:n]
        send, recv = refs[n], refs[n + 1]
        fwd_refs = [refs[n + p] for pos in where for p in pos]
        x, y, c = _mesh_pos()
        sibling = (x, y, 1 - c)
        chips = [(1 - x, y), (x, 1 - y), (1 - x, 1 - y)]

        def desc(ref, s_sem, r_sem):
            return pltpu.make_async_remote_copy(src_ref=ref, dst_ref=ref, send_sem=s_sem, recv_sem=r_sem,
                                                device_id=sibling, device_id_type=MESH)

        for t in range(n):
            mine = land_refs[t].at[_slot((x, y, c))]
            if own:
                desc(land_refs[t].at[_slot(sibling)], send.at[4 * t], recv.at[4 * t]).wait_recv()
            for a, (which, _, _) in enumerate(fwd):
                fs, fr = fwd_refs[2 * a], fwd_refs[2 * a + 1]
                for i, j in enumerate(which):
                    if a < len(arrivals):
                        desc(land_refs[t].at[_slot((*chips[j], 1 - c))], fs.at[len(which) * t + i],
                             fr.at[len(which) * t + i]).wait_recv()
                    else:
                        desc(land_refs[t].at[_slot((*chips[j], c))], fs.at[len(which) * t + i],
                             fr.at[len(which) * t + i]).wait_send()
            if direct_sends:
                for k in range(4):
                    desc(mine, send.at[4 * t + k], recv.at[4 * t + k]).wait_send()

    outs = pl.pallas_call(
        body, name=name,
        out_shape=tuple(pltpu.HBM(g.shape, g.dtype) for g in lands),
        in_specs=tuple([HBM_SPEC] * n + [SEM_SPEC] * len(sem_args) + [ANY_SPEC]),
        out_specs=tuple([HBM_SPEC] * n),
        input_output_aliases={i: i for i in range(n)},
        compiler_params=pltpu.CompilerParams(has_side_effects=_DATAFLOW),
    )(*lands, *sem_args, after)
    return list(outs)


def _pair_exchange_start(grads, after, name):
    n = len(grads)

    def body(*refs):
        ins, lands = refs[:n], refs[n:2 * n]
        send, recv = refs[2 * n + 1], refs[2 * n + 2]
        token = refs[-1]
        x, y, c = _mesh_pos()
        for t in range(n):
            for q in range(4):
                pltpu.make_async_remote_copy(
                    src_ref=ins[t].at[2 * q + 1 - c], dst_ref=lands[t].at[q],
                    send_sem=send.at[4 * t + q], recv_sem=recv.at[4 * t + q],
                    device_id=(x, y, 1 - c), device_id_type=MESH).start()
        token[...] = jnp.zeros_like(token)

    land_shapes = [(4,) + g.shape[1:] for g in grads]
    outs = pl.pallas_call(
        body, name=name,
        out_shape=tuple([pltpu.SemaphoreType.DMA((4 * n,))] * 2 + [pltpu.HBM(g.shape, g.dtype) for g in grads]
                        + [pltpu.HBM(ls, g.dtype) for ls, g in zip(land_shapes, grads)]
                        + [jax.ShapeDtypeStruct((SUBLANES, LANES), F32)]),
        in_specs=tuple([HBM_SPEC] * (2 * n) + [ANY_SPEC]),
        out_specs=tuple([SEM_SPEC] * 2 + [HBM_SPEC] * (2 * n) + [pl.BlockSpec(memory_space=pltpu.VMEM)]),
        input_output_aliases={i: 2 + i for i in range(2 * n)},
        compiler_params=pltpu.CompilerParams(has_side_effects=_DATAFLOW),
    )(*[_hbm(g) for g in grads], *[_hbm(lax.empty(ls, g.dtype)) for ls, g in zip(land_shapes, grads)], after)
    return outs[0], outs[1], list(outs[2:2 + n]), list(outs[2 + n:2 + 2 * n]), outs[-1]


def _pair_exchange_finish(grads, lands, send_sems, recv_sems, after, name):
    n = len(grads)

    def body(*refs):
        ins, land_refs = refs[:n], refs[n:2 * n]
        send, recv = refs[2 * n], refs[2 * n + 1]
        x, y, c = _mesh_pos()
        for t in range(n):
            for q in range(4):
                cp = pltpu.make_async_remote_copy(
                    src_ref=ins[t].at[q], dst_ref=land_refs[t].at[q], send_sem=send.at[4 * t + q],
                    recv_sem=recv.at[4 * t + q], device_id=(x, y, 1 - c), device_id_type=MESH)
                cp.wait_send()
                cp.wait_recv()

    outs = pl.pallas_call(
        body, name=name,
        out_shape=tuple([pltpu.HBM(g.shape, g.dtype) for g in grads] + [pltpu.HBM(g.shape, g.dtype) for g in lands]),
        in_specs=tuple([HBM_SPEC] * (2 * n) + [SEM_SPEC] * 2 + [ANY_SPEC]),
        out_specs=tuple([HBM_SPEC] * (2 * n)),
        input_output_aliases={i: i for i in range(2 * n)},
        compiler_params=pltpu.CompilerParams(has_side_effects=_DATAFLOW),
    )(*grads, *lands, send_sems, recv_sems, after)
    return list(outs[:n]), list(outs[n:])


def _chip_exchange_start(sums, after, name):
    n = len(sums)

    def body(*refs):
        ins, lands = refs[:n], refs[n:2 * n]
        send, recv = refs[2 * n + 1], refs[2 * n + 2]
        token = refs[-1]
        x, y, c = _mesh_pos()
        chips = [(1 - x, y), (x, 1 - y), (1 - x, 1 - y)]
        for t in range(n):
            for r, chip in enumerate(chips):
                pltpu.make_async_remote_copy(
                    src_ref=ins[t].at[2 * chip[0] + chip[1]], dst_ref=lands[t].at[r],
                    send_sem=send.at[3 * t + r], recv_sem=recv.at[3 * t + r],
                    device_id=(*chip, c), device_id_type=MESH).start()
        token[...] = jnp.zeros_like(token)

    land_shapes = [(3,) + s.shape[1:] for s in sums]
    outs = pl.pallas_call(
        body, name=name,
        out_shape=tuple([pltpu.SemaphoreType.DMA((3 * n,))] * 2 + [pltpu.HBM(s.shape, s.dtype) for s in sums]
                        + [pltpu.HBM(ls, s.dtype) for ls, s in zip(land_shapes, sums)]
                        + [jax.ShapeDtypeStruct((SUBLANES, LANES), F32)]),
        in_specs=tuple([HBM_SPEC] * (2 * n) + [ANY_SPEC]),
        out_specs=tuple([SEM_SPEC] * 2 + [HBM_SPEC] * (2 * n) + [pl.BlockSpec(memory_space=pltpu.VMEM)]),
        input_output_aliases={i: 2 + i for i in range(2 * n)},
        compiler_params=pltpu.CompilerParams(has_side_effects=_DATAFLOW),
    )(*[_hbm(s) for s in sums], *[_hbm(lax.empty(ls, s.dtype)) for ls, s in zip(land_shapes, sums)], after)
    return outs[0], outs[1], list(outs[2:2 + n]), list(outs[2 + n:2 + 2 * n]), outs[-1]


def _chip_exchange_finish(sums, lands, send_sems, recv_sems, after, name):
    n = len(sums)

    def body(*refs):
        ins, land_refs = refs[:n], refs[n:2 * n]
        send, recv = refs[2 * n], refs[2 * n + 1]
        x, y, c = _mesh_pos()
        for t in range(n):
            for r in range(3):
                cp = pltpu.make_async_remote_copy(
                    src_ref=ins[t].at[r], dst_ref=land_refs[t].at[r], send_sem=send.at[3 * t + r],
                    recv_sem=recv.at[3 * t + r],
                    device_id=(x, y, 1 - c), device_id_type=MESH)
                cp.wait_send()
                cp.wait_recv()

    outs = pl.pallas_call(
        body, name=name,
        out_shape=tuple(pltpu.HBM(g.shape, g.dtype) for g in lands),
        in_specs=tuple([HBM_SPEC] * (2 * n) + [SEM_SPEC] * 2 + [ANY_SPEC]),
        out_specs=tuple([HBM_SPEC] * n),
        input_output_aliases={n + i: i for i in range(n)},
        compiler_params=pltpu.CompilerParams(has_side_effects=_DATAFLOW),
    )(*sums, *lands, send_sems, recv_sems, after)
    return list(outs)


def _mm(a, b, *, mode, tm, tn, tk=None, b_blocked=False, out_blocked=False, out_dtypes=(F32,),
        epilogue=None, extras=(), after=None, kb=1, out_places=None, row_splits=1, name):
    if mode == "nn":
        m, k = a.shape
        n = b.shape[0] * b.shape[2] if b_blocked else b.shape[1]
        dims = (((1,), (0,)), ((), ()))
    elif mode == "nt":
        m, k = a.shape
        n = b.shape[1] if b_blocked else b.shape[0]
        if b_blocked:
            tk = kb * b.shape[2]
        dims = (((1,), (1,)), ((), ()))
    else:
        k, m = a.shape
        n = b.shape[1]
        dims = (((0,), (0,)), ((), ()))
    tk = k if tk is None else tk
    assert m % tm == 0 and n % tn == 0 and k % tk == 0, (name, m, n, k, tm, tn, tk)
    gm, gn, gk = m // tm, n // tn, k // tk
    if b_blocked:
        assert (tn if mode == "nn" else tk) == kb * b.shape[2], name
    if row_splits > 1:
        assert gk == 1 and epilogue is not None and mode != "tn" and not b_blocked and tm % (16 * row_splits) == 0, name

    if mode == "nn":
        a_spec = pl.BlockSpec((tm, tk), lambda i, j, kk: (i, kk))
        b_spec = (pl.BlockSpec((None, tk, tn), lambda i, j, kk: (j, kk, 0)) if b_blocked
                  else pl.BlockSpec((tk, tn), lambda i, j, kk: (kk, j)))
    elif mode == "nt":
        a_spec = pl.BlockSpec((tm, tk), lambda i, j, kk: (i, kk))
        b_spec = (pl.BlockSpec((kb, tn, tk // kb), lambda i, j, kk: (kk, j, 0)) if b_blocked
                  else pl.BlockSpec((tn, tk), lambda i, j, kk: (j, kk)))
    else:
        a_spec = pl.BlockSpec((tk, tm), lambda i, j, kk: (kk, i))
        b_spec = pl.BlockSpec((tk, tn), lambda i, j, kk: (kk, j))
    if out_blocked:
        out_spec = pl.BlockSpec((None, tm, tn), lambda i, j, kk: (j, i, 0))
        out_shape = (gn, m, tn)
    else:
        out_spec = pl.BlockSpec((tm, tn), lambda i, j, kk: (i, j))
        out_shape = (m, n)
    extra_specs = [pl.BlockSpec((tm, tn), functools.partial(lambda i, j, kk, off: (i, j + off), off=off))
                   for _, off in extras]
    n_extra, n_out = len(extras), len(out_dtypes)
    n_after = 0 if after is None else 1
    places = out_places if out_places is not None else (None,) * n_out

    def body(a_ref, b_ref, *rest):
        extra_refs = rest[:n_extra]
        out_refs = rest[n_extra + n_after:n_extra + n_after + n_out]

        def finish(acc):
            if epilogue is None:
                res = (acc,)
            else:
                res = epilogue(acc, *[e[...] for e in extra_refs])
            for o_ref, r in zip(out_refs, res):
                o_ref[...] = r.astype(o_ref.dtype)

        if row_splits > 1:
            strip = tm // row_splits
            for h in range(row_splits):
                rows = slice(h * strip, (h + 1) * strip)
                acc = lax.dot_general(a_ref[rows, :], b_ref[...], dims, preferred_element_type=F32)
                res = epilogue(acc, *[e[rows, :] for e in extra_refs])
                for o_ref, r in zip(out_refs, res):
                    o_ref[rows, :] = r.astype(o_ref.dtype)
            return
        if mode == "nt" and b_blocked:
            bk = tk // kb
            part = lax.dot_general(a_ref[:, :bk], b_ref[0], dims, preferred_element_type=F32)
            for h in range(1, kb):
                part = part + lax.dot_general(a_ref[:, h * bk:(h + 1) * bk], b_ref[h], dims,
                                              preferred_element_type=F32)
        else:
            part = lax.dot_general(a_ref[...], b_ref[...], dims, preferred_element_type=F32)
        if gk == 1:
            finish(part)
        else:
            acc_ref = rest[-1]
            kk = pl.program_id(2)

            @pl.when(kk == 0)
            def _():
                acc_ref[...] = part

            @pl.when(kk > 0)
            def _():
                acc_ref[...] += part

            @pl.when(kk == gk - 1)
            def _():
                finish(acc_ref[...])

    outs = pl.pallas_call(
        body, name=name, grid=(gm, gn, gk),
        in_specs=[a_spec, b_spec] + extra_specs + [ANY_SPEC] * n_after,
        out_specs=[out_spec if place is None else
                   pl.BlockSpec((tm, tn), functools.partial(lambda i, j, kk, off: (i, j + off), off=place[1] // tn))
                   for place in places],
        out_shape=[jax.ShapeDtypeStruct(out_shape if place is None else (m, place[0]), dt)
                   for dt, place in zip(out_dtypes, places)],
        scratch_shapes=[pltpu.VMEM((tm, tn), F32)] if gk > 1 else [],
        compiler_params=_cparams("parallel", "parallel", "arbitrary"),
    )(a, b, *[e for e, _ in extras], *([] if after is None else [after]))
    return outs[0] if n_out == 1 else outs


def _mm_slots(a, w, slots, *, over, tm, tn=None, out_dtype=F32, epilogue=None, base=None, name):
    m = a.shape[0]
    ns = slots.shape[0]
    n_slots, w1, w2 = w.shape
    assert m % tm == 0
    if over == "n":
        k, bn = w1, w2

        def body(slots_ref, a_ref, w_ref, *rest):
            out_ref = rest[-1]
            acc = jnp.dot(a_ref[...], w_ref[...], preferred_element_type=F32)
            out_ref[...] = (acc if epilogue is None else epilogue(acc)).astype(out_ref.dtype)

        in_specs = [pl.BlockSpec((tm, k), lambda i, j, s: (i, 0)),
                    pl.BlockSpec((None, k, bn), lambda i, j, s: (s[j], 0, 0))]
        args = [a, w]
        aliases = {}
        if base is not None:
            in_specs.append(ANY_SPEC)
            args.append(base)
            aliases = {3: 0}
        return pl.pallas_call(
            body, name=name,
            grid_spec=pltpu.PrefetchScalarGridSpec(
                num_scalar_prefetch=1, grid=(m // tm, ns), in_specs=in_specs,
                out_specs=pl.BlockSpec((tm, bn), lambda i, j, s: (i, s[j]))),
            out_shape=jax.ShapeDtypeStruct((m, n_slots * bn), out_dtype),
            input_output_aliases=aliases,
            compiler_params=_cparams("parallel", "arbitrary"),
        )(slots, *args)

    bk, n = w1, w2
    tn = n if tn is None else tn
    assert n % tn == 0

    def body(slots_ref, a_ref, w_ref, *rest):
        out_ref, acc_ref = rest[-2], rest[-1]
        kk = pl.program_id(2)
        part = jnp.dot(a_ref[...], w_ref[...], preferred_element_type=F32)

        @pl.when(kk == 0)
        def _():
            acc_ref[...] = part if base is None else part + rest[0][...]

        @pl.when(kk > 0)
        def _():
            acc_ref[...] += part

        @pl.when(kk == ns - 1)
        def _():
            out_ref[...] = acc_ref[...].astype(out_ref.dtype)

    in_specs = [pl.BlockSpec((tm, bk), lambda i, j, kk, s: (i, s[kk])),
                pl.BlockSpec((None, bk, tn), lambda i, j, kk, s: (s[kk], 0, j))]
    args = [a, w]
    if base is not None:
        in_specs.append(pl.BlockSpec((tm, tn), lambda i, j, kk, s: (i, j)))
        args.append(base)
    return pl.pallas_call(
        body, name=name,
        grid_spec=pltpu.PrefetchScalarGridSpec(
            num_scalar_prefetch=1, grid=(m // tm, n // tn, ns), in_specs=in_specs,
            out_specs=pl.BlockSpec((tm, tn), lambda i, j, kk, s: (i, j)),
            scratch_shapes=[pltpu.VMEM((tm, tn), F32)]),
        out_shape=jax.ShapeDtypeStruct((m, n), out_dtype),
        compiler_params=_cparams("parallel", "parallel", "arbitrary"),
    )(slots, *args)


def _gate_mix(ya_pre, s, wpo, wco, proj, d_model, name):
    lp, width = ya_pre.shape
    nb, _, bn = wpo.shape
    ga_off = (proj.shape[1] - 2 * d_model) // bn
    gb_off = (proj.shape[1] - d_model) // bn

    n_strips = 4 if lp % 64 == 0 else 1

    def body(ya_ref, s_ref, wpo_ref, wco_ref, ga_ref, gb_ref, m_ref, y_a_ref, y_b_ref):
        strip = lp // n_strips
        for h in range(n_strips):
            rows = slice(h * strip, (h + 1) * strip)
            y_a = jnp.dot(ya_ref[rows, :], wpo_ref[...], preferred_element_type=F32)
            y_b = jnp.dot(s_ref[rows, :], wco_ref[...], preferred_element_type=F32)
            m = jax.nn.sigmoid(ga_ref[rows, :]) * y_a + jax.nn.sigmoid(gb_ref[rows, :]) * y_b
            m_ref[rows, :] = m.astype(BF16)
            y_a_ref[rows, :] = y_a.astype(BF16)
            y_b_ref[rows, :] = y_b.astype(BF16)

    act_spec = pl.BlockSpec((lp, width), lambda j: (0, 0))
    w_spec = pl.BlockSpec((None, width, bn), lambda j: (j, 0, 0))
    out_spec = pl.BlockSpec((lp, bn), lambda j: (0, j))
    return pl.pallas_call(
        body, name=name, grid=(nb,),
        in_specs=[act_spec, act_spec, w_spec, w_spec,
                  pl.BlockSpec((lp, bn), lambda j: (0, j + ga_off)),
                  pl.BlockSpec((lp, bn), lambda j: (0, j + gb_off))],
        out_specs=[out_spec] * 3,
        out_shape=[jax.ShapeDtypeStruct((lp, nb * bn), BF16)] * 3,
        compiler_params=_cparams("parallel"),
    )(ya_pre, s, wpo, wco, proj, proj)


def _rms_stats(x):
    return lax.rsqrt(jnp.mean(x * x, axis=-1, keepdims=True) + RMS_EPS)


def _rms_bwd(x, g, dy):
    r = _rms_stats(x)
    nrm = x * r
    dn = dy * g
    dx = r * (dn - nrm * jnp.mean(dn * nrm, axis=-1, keepdims=True))
    return dx, dy * nrm


def _rowwise(body, ins, outs, accs, *, lp, name):
    tr = _row_tile(lp, max(a.shape[1] for a in ins))
    n_in, n_out, n_acc = len(ins), len(outs), len(accs)

    def kernel_body(*refs):
        i = pl.program_id(0)
        acc_refs = refs[n_in + n_out:]

        @pl.when(i == 0)
        def _():
            for r in acc_refs:
                r[...] = jnp.zeros_like(r)

        body(i * tr, refs[:n_in], refs[n_in:n_in + n_out], acc_refs)

    in_specs = []
    for a in ins:
        if a.shape[0] == lp:
            in_specs.append(pl.BlockSpec((tr, a.shape[1]), lambda i: (i, 0)))
        else:
            in_specs.append(pl.BlockSpec(a.shape, lambda i: (0, 0)))
    out_specs = [pl.BlockSpec((tr, w), lambda i: (i, 0)) for w, _ in outs]
    out_specs += [pl.BlockSpec((SUBLANES, w), lambda i: (0, 0)) for w in accs]
    out_shape = [jax.ShapeDtypeStruct((lp, w), d) for w, d in outs]
    out_shape += [jax.ShapeDtypeStruct((SUBLANES, w), F32) for w in accs]
    return pl.pallas_call(
        kernel_body, name=name, grid=(lp // tr,), in_specs=in_specs, out_specs=out_specs,
        out_shape=out_shape, compiler_params=_cparams("arbitrary"),
    )(*ins)


SHIFT_TILE = 128


def _shifted_specs(width, n_big, n_small):
    per = SHIFT_TILE // N_META
    small = pl.BlockSpec((N_META, width), lambda i: (jnp.clip(per * i - 1, 0, n_small - 1), 0))
    big = pl.BlockSpec((SHIFT_TILE, width), lambda i: (jnp.minimum(i, n_big - 1), 0))
    return small, big


def _rms_pre(meta_full, x2, g, lp, seq):
    d = x2.shape[1]
    assert seq % SHIFT_TILE == 0 and lp % SHIFT_TILE == 0 and SHIFT_TILE % N_META == 0

    def body(meta_ref, xs_ref, xb_ref, g_ref, h0_ref, u1_ref):
        i = pl.program_id(0)
        head = jnp.where(i == 0, meta_ref[...], xs_ref[...])
        rows = jnp.concatenate([head, xb_ref[:SHIFT_TILE - N_META, :]], axis=0)
        r = i * SHIFT_TILE + lax.broadcasted_iota(jnp.int32, (SHIFT_TILE, 1), 0)
        rows = jnp.where(r < N_META + seq, rows, 0.0)
        h0_ref[...] = rows
        u1_ref[...] = (rows * _rms_stats(rows) * g_ref[...]).astype(BF16)

    small, big = _shifted_specs(d, seq // SHIFT_TILE, seq // N_META)
    tile = pl.BlockSpec((SHIFT_TILE, d), lambda i: (i, 0))
    return pl.pallas_call(
        body, name="rms_pre", grid=(lp // SHIFT_TILE,),
        in_specs=[pl.BlockSpec((N_META, d), lambda i: (0, 0)), small, big, pl.BlockSpec((1, d), lambda i: (0, 0))],
        out_specs=[tile, tile],
        out_shape=[jax.ShapeDtypeStruct((lp, d), F32), jax.ShapeDtypeStruct((lp, d), BF16)],
        compiler_params=_cparams("parallel"),
    )(meta_full, x2, x2, g)


def _post_mix(o, h0, g_post_mix, g_pre_mlp, lp):
    d = h0.shape[1]

    def body(row0, ins, outs, accs):
        o_ref, h0_ref, g1_ref, g2_ref = ins
        o_v = o_ref[...]
        h1 = h0_ref[...] + o_v * _rms_stats(o_v) * g1_ref[...]
        outs[0][...] = h1
        outs[1][...] = (h1 * _rms_stats(h1) * g2_ref[...]).astype(BF16)

    return _rowwise(body, [o, h0, g_post_mix, g_pre_mlp], [(d, F32), (d, BF16)], [], lp=lp, name="post_mix")


def _loss_head(f, h1, target, g_post_mlp, lp, seq):
    d = f.shape[1]

    def body(f_ref, h1_ref, ts_ref, tb_ref, g_ref, df_ref, dh_ref, dg_ref, loss_ref):
        i = pl.program_id(0)

        @pl.when(i == 0)
        def _():
            dg_ref[...] = jnp.zeros_like(dg_ref)
            loss_ref[...] = jnp.zeros_like(loss_ref)

        f_v, g = f_ref[...], g_ref[...]
        r = _rms_stats(f_v)
        nrm = f_v * r
        rows = i * SHIFT_TILE + lax.broadcasted_iota(jnp.int32, (SHIFT_TILE, 1), 0)
        valid = (rows >= N_META) & (rows < N_META + seq)
        tgt = jnp.concatenate([ts_ref[...], tb_ref[:SHIFT_TILE - N_META, :]], axis=0)
        err = jnp.where(valid, h1_ref[...] + nrm * g - tgt, 0.0)
        loss_ref[...] += 0.5 * jnp.sum(jnp.mean(err * err, axis=-1, keepdims=True))
        dy = err * (1.0 / d)
        dn = dy * g
        df_ref[...] = (r * (dn - nrm * jnp.mean(dn * nrm, axis=-1, keepdims=True))).astype(BF16)
        dh_ref[...] = dy
        dg_ref[...] += _rowsum8(dy * nrm)

    small, big = _shifted_specs(d, seq // SHIFT_TILE, seq // N_META)
    tile = pl.BlockSpec((SHIFT_TILE, d), lambda i: (i, 0))
    return pl.pallas_call(
        body, name="loss_head", grid=(lp // SHIFT_TILE,),
        in_specs=[tile, tile, small, big, pl.BlockSpec((1, d), lambda i: (0, 0))],
        out_specs=[tile, tile, pl.BlockSpec((SUBLANES, d), lambda i: (0, 0)),
                   pl.BlockSpec((SUBLANES, LANES), lambda i: (0, 0))],
        out_shape=[jax.ShapeDtypeStruct((lp, d), BF16), jax.ShapeDtypeStruct((lp, d), F32),
                   jax.ShapeDtypeStruct((SUBLANES, d), F32), jax.ShapeDtypeStruct((SUBLANES, LANES), F32)],
        compiler_params=_cparams("arbitrary"),
    )(f, h1, target, target, g_post_mlp)


def _mid_bwd(du2, h1, dh, o, g_pre_mlp, g_post_mix, lp):
    d = h1.shape[1]

    def body(row0, ins, outs, accs):
        du2_ref, h1_ref, dh_ref, o_ref, g2_ref, g1_ref = ins
        dx2, dg2 = _rms_bwd(h1_ref[...], g2_ref[...], du2_ref[...])
        dh1 = dh_ref[...] + dx2
        do, dg1 = _rms_bwd(o_ref[...], g1_ref[...], dh1)
        outs[0][...] = dh1
        outs[1][...] = do.astype(BF16)
        accs[0][...] += _rowsum8(dg2)
        accs[1][...] += _rowsum8(dg1)

    return _rowwise(body, [du2, h1, dh, o, g_pre_mlp, g_post_mix], [(d, F32), (d, BF16)], [d, d], lp=lp,
                    name="mid_bwd")


def _pre_mix_bwd(du1, h0, dh1, g_pre_mix, seq):
    d = h0.shape[1]
    per = SHIFT_TILE // N_META
    assert seq % SHIFT_TILE == 0

    def body(du_b, h_b, dh_b, du_n, h_n, dh_n, du_m, h_m, dh_m, g_ref, gx_ref, gm_ref, dg_ref):
        i = pl.program_id(0)
        g = g_ref[...]

        @pl.when(i == 0)
        def _():
            dx, dg = _rms_bwd(h_m[...], g, du_m[...])
            gm_ref[...] = dh_m[...] + dx
            dg_ref[...] = _rowsum8(dg)

        rows = lambda big, nxt: jnp.concatenate([big[N_META:, :], nxt[...]], axis=0)
        dx, dg = _rms_bwd(rows(h_b, h_n), g, rows(du_b, du_n))
        gx_ref[...] = rows(dh_b, dh_n) + dx
        dg_ref[...] += _rowsum8(dg)

    big = pl.BlockSpec((SHIFT_TILE, d), lambda i: (i, 0))
    nxt = pl.BlockSpec((N_META, d), lambda i: (per * (i + 1), 0))
    first = pl.BlockSpec((N_META, d), lambda i: (0, 0))
    return pl.pallas_call(
        body, name="pre_mix_bwd", grid=(seq // SHIFT_TILE,),
        in_specs=[big] * 3 + [nxt] * 3 + [first] * 3 + [pl.BlockSpec((1, d), lambda i: (0, 0))],
        out_specs=[big, first, pl.BlockSpec((SUBLANES, d), lambda i: (0, 0))],
        out_shape=[jax.ShapeDtypeStruct((seq, d), F32), jax.ShapeDtypeStruct((N_META, d), F32),
                   jax.ShapeDtypeStruct((SUBLANES, d), F32)],
        compiler_params=_cparams("arbitrary"),
    )(du1, h0, dh1, du1, h0, dh1, du1, h0, dh1, g_pre_mix)


def _ln_stats(c):
    mu = jnp.mean(c, axis=-1, keepdims=True)
    var = jnp.mean(jnp.square(c - mu), axis=-1, keepdims=True)
    return mu, lax.rsqrt(var + LN_EPS)


def _ln_silu(c, ln_g, ln_b, lp):
    w = c.shape[1]

    def body(row0, ins, outs, accs):
        c_ref, g_ref, b_ref = ins
        c_v = c_ref[...]
        mu, rstd = _ln_stats(c_v)
        ln = (c_v - mu) * rstd * g_ref[...] + b_ref[...]
        outs[0][...] = (ln * jax.nn.sigmoid(ln)).astype(BF16)

    return _rowwise(body, [c, ln_g, ln_b], [(w, BF16)], [], lp=lp, name="ln_silu")[0]


def _ln_silu_bwd(c, ds, ln_g, ln_b, lp):
    w = c.shape[1]

    def body(row0, ins, outs, accs):
        c_ref, ds_ref, g_ref, b_ref = ins
        c_v, g = c_ref[...], g_ref[...]
        mu, rstd = _ln_stats(c_v)
        nrm = (c_v - mu) * rstd
        ln = nrm * g + b_ref[...]
        sig = jax.nn.sigmoid(ln)
        dln = ds_ref[...] * (sig * (1.0 + ln * (1.0 - sig)))
        dn = dln * g
        dc = rstd * (dn - jnp.mean(dn, axis=-1, keepdims=True) - nrm * jnp.mean(dn * nrm, axis=-1, keepdims=True))
        outs[0][...] = dc
        accs[0][...] += _rowsum8(dln * nrm)
        accs[1][...] += _rowsum8(dln)
        accs[2][...] += _rowsum8(dc)

    return _rowwise(body, [c, ds, ln_g, ln_b], [(w, F32)], [w, w, w], lp=lp, name="ln_silu_bwd")


def _chunk_with_history(ref, i, cols=slice(None)):
    t0 = pl.multiple_of(i * ROW_CHUNK, ROW_CHUNK)
    lo0 = pl.multiple_of(jnp.maximum(t0 - HALO, 0), SUBLANES)
    lo = jnp.where(i > 0, ref[pl.ds(lo0, HALO), cols], 0.0)
    return jnp.concatenate([lo, ref[pl.ds(t0, ROW_CHUNK), cols]], axis=0)


def _chunk_with_future(ref, i, n_chunks, cols=slice(None)):
    t0 = pl.multiple_of(i * ROW_CHUNK, ROW_CHUNK)
    hi0 = pl.multiple_of(jnp.minimum(t0 + ROW_CHUNK, (n_chunks - 1) * ROW_CHUNK), SUBLANES)
    hi = jnp.where(i < n_chunks - 1, ref[pl.ds(hi0, HALO), cols], 0.0)
    return jnp.concatenate([ref[pl.ds(t0, ROW_CHUNK), cols], hi], axis=0)


def _inv_count(t0, n_rows, window):
    pos = t0 + lax.broadcasted_iota(jnp.int32, (n_rows, 1), 0)
    return 1.0 / jnp.minimum(pos + 1, window).astype(F32)


def _pool_delta(z_hist, t0, window):
    s = z_hist
    sh = 1
    while sh < window:
        s = s + pltpu.roll(s, sh, 0)
        sh *= 2
    cur = z_hist[HALO:, :]
    return s[HALO:, :] * _inv_count(t0, ROW_CHUNK, window) - cur


def _pool_fwd(proj, wpg, pool_scale, lp):
    n_grp, gdim, _ = wpg.shape
    width = n_grp * gdim
    n_chunks = lp // ROW_CHUNK

    def body(z_ref, w_ref, sc_ref, out_ref):
        for g, window in enumerate(POOL_WINDOWS):
            cols = slice(g * gdim, (g + 1) * gdim)

            def chunk(i, carry, cols=cols, g=g, window=window):
                t0 = pl.multiple_of(i * ROW_CHUNK, ROW_CHUNK)
                d = _pool_delta(_chunk_with_history(z_ref, i, cols), t0, window)
                q = jnp.dot(d.astype(BF16), w_ref[g], preferred_element_type=F32)
                out_ref[pl.ds(t0, ROW_CHUNK), cols] = (q * sc_ref[:, cols]).astype(BF16)
                return carry

            lax.fori_loop(0, n_chunks, chunk, 0)

    return pl.pallas_call(
        body, name="pool_fwd", grid=(1,),
        in_specs=[pl.BlockSpec((lp, width), lambda i: (0, 0)),
                  pl.BlockSpec(wpg.shape, lambda i: (0, 0, 0)),
                  pl.BlockSpec(pool_scale.shape, lambda i: (0, 0))],
        out_specs=pl.BlockSpec((lp, width), lambda i: (0, 0)),
        out_shape=jax.ShapeDtypeStruct((lp, width), BF16),
        compiler_params=_cparams("arbitrary"),
    )(proj, wpg, pool_scale)


def _pool_bwd(proj, d_ya, wpg, pool_scale, dproj, lp):
    n_grp, gdim, _ = wpg.shape
    width = n_grp * gdim
    n_chunks = lp // ROW_CHUNK
    ext = ROW_CHUNK + HALO

    def body(z_ref, dya_ref, w_ref, sc_ref, _, dz_ref, dw_ref, dsc_ref):
        dw_ref[...] = jnp.zeros_like(dw_ref)
        dsc_ref[...] = jnp.zeros_like(dsc_ref)
        for g, window in enumerate(POOL_WINDOWS):
            cols = slice(g * gdim, (g + 1) * gdim)

            def chunk(i, carry, cols=cols, g=g, window=window):
                t0 = pl.multiple_of(i * ROW_CHUNK, ROW_CHUNK)
                w_g = w_ref[g]
                scale = sc_ref[:, cols]
                d = _pool_delta(_chunk_with_history(z_ref, i, cols), t0, window).astype(BF16)
                dya_ext = _chunk_with_future(dya_ref, i, n_chunks, cols)
                dya = dya_ext[:ROW_CHUNK, :]
                q = jnp.dot(d, w_g, preferred_element_type=F32)
                dsc_ref[:, cols] += _rowsum8(dya * q)
                e_ext = (dya_ext * scale).astype(BF16)
                dw_ref[g] += lax.dot_general(d, e_ext[:ROW_CHUNK, :], (((0,), (0,)), ((), ())),
                                             preferred_element_type=F32)
                dd_ext = lax.dot_general(e_ext, w_g, (((1,), (1,)), ((), ())), preferred_element_type=F32)
                s = dd_ext * _inv_count(t0, ext, window)
                sh = 1
                while sh < window:
                    s = s + pltpu.roll(s, ext - sh, 0)
                    sh *= 2
                dz_ref[pl.ds(t0, ROW_CHUNK), cols] = (s[:ROW_CHUNK, :] - dd_ext[:ROW_CHUNK, :]).astype(BF16)
                return carry

            lax.fori_loop(0, n_chunks, chunk, 0)

    blk = pl.BlockSpec((lp, width), lambda i: (0, 0))
    return pl.pallas_call(
        body, name="pool_bwd", grid=(1,),
        in_specs=[blk, blk, pl.BlockSpec(wpg.shape, lambda i: (0, 0, 0)),
                  pl.BlockSpec(pool_scale.shape, lambda i: (0, 0)), ANY_SPEC],
        out_specs=[blk, pl.BlockSpec(wpg.shape, lambda i: (0, 0, 0)),
                   pl.BlockSpec((SUBLANES, width), lambda i: (0, 0))],
        out_shape=[jax.ShapeDtypeStruct(dproj.shape, BF16), jax.ShapeDtypeStruct(wpg.shape, F32),
                   jax.ShapeDtypeStruct((SUBLANES, width), F32)],
        input_output_aliases={4: 0},
        compiler_params=_cparams("arbitrary"),
    )(proj, d_ya, wpg, pool_scale, dproj)


def _conv_fwd(proj, w_dw, b_dw, lp, width, v_col0):
    n_chunks = lp // ROW_CHUNK
    v_blk0, g_blk0 = v_col0 // LANES, (v_col0 + width) // LANES

    def body(v_ref, gc_ref, w_ref, b_ref, c_ref, a_pad):
        a_pad[pl.ds(0, HALO), :] = jnp.zeros((HALO, LANES), F32)
        a_pad[pl.ds(HALO, lp), :] = v_ref[...] * jax.nn.sigmoid(gc_ref[...])

        def chunk(i, carry):
            t0 = pl.multiple_of(i * ROW_CHUNK, ROW_CHUNK)
            hist = a_pad[pl.ds(t0, ROW_CHUNK + HALO), :]
            acc = jnp.zeros((ROW_CHUNK, LANES), F32)
            for k in range(CONV_KERNEL):
                acc = acc + w_ref[k:k + 1, :] * pltpu.roll(hist, CONV_KERNEL - 1 - k, 0)[HALO:, :]
            c_ref[pl.ds(t0, ROW_CHUNK), :] = acc + b_ref[...]
            return carry

        lax.fori_loop(0, n_chunks, chunk, 0)

    return pl.pallas_call(
        body, name="conv_fwd", grid=(width // LANES,),
        in_specs=[pl.BlockSpec((lp, LANES), lambda j: (0, j + v_blk0)),
                  pl.BlockSpec((lp, LANES), lambda j: (0, j + g_blk0)),
                  pl.BlockSpec((CONV_TAPS_PADDED, LANES), lambda j: (0, j)),
                  pl.BlockSpec((1, LANES), lambda j: (0, j))],
        out_specs=pl.BlockSpec((lp, LANES), lambda j: (0, j)),
        out_shape=jax.ShapeDtypeStruct((lp, width), F32),
        scratch_shapes=[pltpu.VMEM((lp + HALO, LANES), F32)],
        compiler_params=_cparams("parallel"),
    )(proj, proj, w_dw, b_dw)


def _conv_bwd(proj, dc, w_dw, dproj, lp, width, v_col0):
    n_chunks = lp // ROW_CHUNK
    ext = ROW_CHUNK + HALO
    v_blk0, g_blk0 = v_col0 // LANES, (v_col0 + width) // LANES

    def body(v_ref, gc_ref, dc_ref, w_ref, _, dv_ref, dgc_ref, dw_ref, a_pad, dc_pad, dw_acc):
        sig = jax.nn.sigmoid(gc_ref[...])
        a_pad[pl.ds(0, HALO), :] = jnp.zeros((HALO, LANES), F32)
        a_pad[pl.ds(HALO, lp), :] = v_ref[...] * sig
        dc_pad[pl.ds(0, lp), :] = dc_ref[...]
        dc_pad[pl.ds(lp, HALO), :] = jnp.zeros((HALO, LANES), F32)
        dw_acc[...] = jnp.zeros_like(dw_acc)

        def chunk(i, carry):
            t0 = pl.multiple_of(i * ROW_CHUNK, ROW_CHUNK)
            hist = a_pad[pl.ds(t0, ext), :]
            fut = dc_pad[pl.ds(t0, ext), :]
            dc_cur = fut[:ROW_CHUNK, :]
            da = jnp.zeros((ROW_CHUNK, LANES), F32)
            for k in range(CONV_KERNEL):
                lag = CONV_KERNEL - 1 - k
                da = da + w_ref[k:k + 1, :] * pltpu.roll(fut, (ext - lag) % ext, 0)[:ROW_CHUNK, :]
                dw_acc[pl.ds(SUBLANES * k, SUBLANES), :] += _rowsum8(dc_cur * pltpu.roll(hist, lag, 0)[HALO:, :])
            rows = pl.ds(t0, ROW_CHUNK)
            sg = jax.nn.sigmoid(gc_ref[rows, :])
            dv_ref[rows, :] = (da * sg).astype(BF16)
            dgc_ref[rows, :] = (da * v_ref[rows, :] * sg * (1.0 - sg)).astype(BF16)
            return carry

        lax.fori_loop(0, n_chunks, chunk, 0)
        dw_ref[...] = dw_acc[...].reshape(CONV_TAPS_PADDED, SUBLANES, LANES).sum(axis=1)

    col = lambda j: (0, j)
    return pl.pallas_call(
        body, name="conv_bwd", grid=(width // LANES,),
        in_specs=[pl.BlockSpec((lp, LANES), lambda j: (0, j + v_blk0)),
                  pl.BlockSpec((lp, LANES), lambda j: (0, j + g_blk0)),
                  pl.BlockSpec((lp, LANES), col),
                  pl.BlockSpec((CONV_TAPS_PADDED, LANES), col), ANY_SPEC],
        out_specs=[pl.BlockSpec((lp, LANES), lambda j: (0, j + v_blk0)), pl.BlockSpec((lp, LANES), col),
                   pl.BlockSpec((CONV_TAPS_PADDED, LANES), col)],
        out_shape=[jax.ShapeDtypeStruct(dproj.shape, BF16), jax.ShapeDtypeStruct((lp, width), BF16),
                   jax.ShapeDtypeStruct((CONV_TAPS_PADDED, width), F32)],
        scratch_shapes=[pltpu.VMEM((lp + HALO, LANES), F32), pltpu.VMEM((lp + HALO, LANES), F32),
                        pltpu.VMEM((CONV_TAPS_PADDED * SUBLANES, LANES), F32)],
        input_output_aliases={4: 0},
        compiler_params=_cparams("parallel"),
    )(proj, proj, dc, w_dw, dproj)


def _place_columns(dst, pieces, name):
    m = dst.shape[0]
    tile = 512
    counts = [p.shape[1] // tile for p, _ in pieces]
    starts = [sum(counts[:i]) for i in range(len(pieces))]
    n_steps = sum(counts)

    def local(s, i):
        return jnp.clip(s - starts[i], 0, counts[i] - 1)

    def out_index(s):
        blk = pieces[0][1] // tile + local(s, 0)
        for i in range(1, len(pieces)):
            blk = jnp.where(s >= starts[i], pieces[i][1] // tile + local(s, i), blk)
        return 0, blk

    def body(*refs):
        out_ref = refs[-1]
        s = pl.program_id(0)
        for i in range(len(pieces)):
            @pl.when((s >= starts[i]) & (s < starts[i] + counts[i]))
            def _(i=i):
                out_ref[...] = refs[i][...]

    return pl.pallas_call(
        body, name=name, grid=(n_steps,),
        in_specs=[pl.BlockSpec((m, tile), functools.partial(lambda s, i: (0, local(s, i)), i=i))
                  for i in range(len(pieces))] + [ANY_SPEC],
        out_specs=pl.BlockSpec((m, tile), out_index),
        out_shape=jax.ShapeDtypeStruct(dst.shape, dst.dtype),
        input_output_aliases={len(pieces): 0},
        compiler_params=_cparams("arbitrary"),
    )(*[p for p, _ in pieces], dst)


def _adamw_math(w, g, m, v):
    m = ADAM_B1 * m + (1.0 - ADAM_B1) * g
    v = ADAM_B2 * v + (1.0 - ADAM_B2) * jnp.square(g)
    m_hat = m / (1.0 - ADAM_B1 ** ADAM_STEP)
    v_hat = v / (1.0 - ADAM_B2 ** ADAM_STEP)
    delta = -ADAM_LR * (m_hat / (jnp.sqrt(v_hat) + ADAM_EPS) + ADAM_WD * w)
    return delta, m, v


def _pair_sum(own, recv, where, name):
    _, _, rows, cols = own.shape
    tr = _row_tile(rows, cols, 1024 * 1024)

    def body(where_ref, own_ref, recv_ref, out_ref):
        out_ref[...] = (own_ref[...].astype(F32) + recv_ref[...].astype(F32)).astype(BF16)

    return pl.pallas_call(
        body, name=name,
        grid_spec=pltpu.PrefetchScalarGridSpec(
            num_scalar_prefetch=1, grid=(3, rows // tr),
            in_specs=[pl.BlockSpec((None, None, tr, cols), lambda r, i, wh: (wh[2 + r], wh[0], i, 0)),
                      pl.BlockSpec((None, tr, cols), lambda r, i, wh: (wh[2 + r], i, 0))],
            out_specs=pl.BlockSpec((None, tr, cols), lambda r, i, wh: (wh[2 + r], i, 0))),
        out_shape=jax.ShapeDtypeStruct((4, rows, cols), BF16),
        compiler_params=_cparams("parallel", "parallel"),
    )(where, own, recv)


def _adamw_big(w, m, v, own, from_sibling, recv3, where, name):
    rows, cols = w.shape
    tr = _row_tile(rows, cols, 256 * 1024)

    def body(where_ref, w_ref, m_ref, v_ref, own_ref, sib_ref, r_ref, g_out, d_out, m_out, v_out):
        g = own_ref[...].astype(F32) + sib_ref[...].astype(F32)
        for r in range(3):
            g = g + r_ref[r].astype(F32)
        delta, m_new, v_new = _adamw_math(w_ref[...], g, m_ref[...], v_ref[...])
        g_out[...] = g
        d_out[...] = delta
        m_out[...] = m_new
        v_out[...] = v_new

    blk = pl.BlockSpec((tr, cols), lambda i, q_ref: (i, 0))
    return pl.pallas_call(
        body, name=name,
        grid_spec=pltpu.PrefetchScalarGridSpec(
            num_scalar_prefetch=1, grid=(rows // tr,),
            in_specs=[blk, blk, blk,
                      pl.BlockSpec((None, None, tr, cols), lambda i, wh: (wh[1], wh[0], i, 0)),
                      pl.BlockSpec((None, tr, cols), lambda i, wh: (wh[1], i, 0)),
                      pl.BlockSpec((3, tr, cols), lambda i, wh: (0, i, 0))],
            out_specs=[blk] * 4),
        out_shape=[jax.ShapeDtypeStruct((rows, cols), F32)] * 4,
        compiler_params=_cparams("parallel"),
    )(where, w, m, v, own, from_sibling, recv3)


def _small_update(me_idx, packed, rep_params, rep_places, meta_wmv, meta_row0, wdw_wmv, wdw_row0, loss_row0):
    n_rep = len(rep_params)
    meta_rows, meta_cols = meta_wmv[0].shape
    wdw_rows, wdw_cols = wdw_wmv[0].shape

    def body(me_ref, *refs):
        pos = 0

        def take(k):
            nonlocal pos
            out = refs[pos:pos + k]
            pos += k
            return out

        rep_in = [take(3) for _ in range(n_rep)]
        rep_g = take(n_rep)
        meta_in, (meta_g,) = take(3), take(1)
        wdw_in, (wdw_g,) = take(3), take(1)
        (loss_ref,) = take(1)
        rep_out = [take(4) for _ in range(n_rep)]
        meta_out, wdw_out = take(4), take(4)
        (loss_out,) = take(1)

        def update(wmv, g, outs):
            delta, m_new, v_new = _adamw_math(wmv[0][...], g, wmv[1][...], wmv[2][...])
            for o_ref, val in zip(outs, (g, delta, m_new, v_new)):
                o_ref[...] = val

        for wmv, g_ref, outs in zip(rep_in, rep_g, rep_out):
            g = jnp.sum(g_ref[0], axis=0, keepdims=True)
            for j in range(1, N_DEV):
                g = g + jnp.sum(g_ref[j], axis=0, keepdims=True)
            update(wmv, g, outs)
        for wmv, g_ref, outs in ((meta_in, meta_g, meta_out), (wdw_in, wdw_g, wdw_out)):
            g = g_ref[0]
            for j in range(1, N_DEV):
                g = g + g_ref[j]
            update(wmv, g, outs)
        total = loss_ref[0]
        for j in range(1, N_DEV):
            total = total + loss_ref[j]
        loss_out[...] = total

    def whole(a):
        nd = a.ndim
        return pl.BlockSpec(a.shape, lambda i, me_ref, nd=nd: (0,) * nd)

    ins, in_specs = [], []
    for wmv in rep_params:
        ins += list(wmv)
        in_specs += [whole(a) for a in wmv]
    for wmv, (row0, col0) in zip(rep_params, rep_places):
        width = wmv[0].shape[1]
        ins.append(packed)
        in_specs.append(pl.BlockSpec((N_DEV, SUBLANES, width),
                                     lambda i, me_ref, rb=row0 // SUBLANES, cb=col0 // width: (0, rb, cb)))
    ins += list(meta_wmv) + [packed]
    in_specs += [whole(a) for a in meta_wmv]
    in_specs.append(pl.BlockSpec((N_DEV, meta_rows, meta_cols),
                                 lambda i, me_ref, rb=meta_row0 // meta_rows: (0, rb, me_ref[0])))
    ins += list(wdw_wmv) + [packed]
    in_specs += [whole(a) for a in wdw_wmv]
    in_specs.append(pl.BlockSpec((N_DEV, wdw_rows, wdw_cols),
                                 lambda i, me_ref, rb=wdw_row0 // wdw_rows: (0, rb, me_ref[0])))
    ins.append(packed)
    in_specs.append(pl.BlockSpec((N_DEV, SUBLANES, LANES), lambda i, me_ref, rb=loss_row0 // SUBLANES: (0, rb, 0)))

    out_shape, out_specs = [], []
    for wmv in list(rep_params) + [meta_wmv, wdw_wmv]:
        out_shape += [jax.ShapeDtypeStruct(wmv[0].shape, F32)] * 4
        out_specs += [whole(wmv[0])] * 4
    out_shape.append(jax.ShapeDtypeStruct((SUBLANES, LANES), F32))
    out_specs.append(pl.BlockSpec((SUBLANES, LANES), lambda i, me_ref: (0, 0)))

    outs = pl.pallas_call(
        body, name="small_update",
        grid_spec=pltpu.PrefetchScalarGridSpec(num_scalar_prefetch=1, grid=(1,), in_specs=in_specs,
                                               out_specs=out_specs),
        out_shape=out_shape, compiler_params=_cparams("arbitrary"),
    )(me_idx, *ins)
    groups = [outs[4 * i:4 * i + 4] for i in range(n_rep + 2)]
    return groups[:n_rep], groups[n_rep], groups[n_rep + 1], outs[-1]


def kernel(x, meta, g_pre_mix, w_in, w_pool_grp, pool_scale, w_pool_out, w_dw, b_dw, conv_ln_g, conv_ln_b, w_conv_out, w_o, g_post_mix, g_pre_mlp, w_up, w_down, g_post_mlp, loss_target, m_meta, m_g_pre_mix, m_w_in, m_w_pool_grp, m_pool_scale, m_w_pool_out, m_w_dw, m_b_dw, m_conv_ln_g, m_conv_ln_b, m_w_conv_out, m_w_o, m_g_post_mix, m_g_pre_mlp, m_w_up, m_w_down, m_g_post_mlp, v_meta, v_g_pre_mix, v_w_in, v_w_pool_grp, v_pool_scale, v_w_pool_out, v_w_dw, v_b_dw, v_conv_ln_g, v_conv_ln_b, v_w_conv_out, v_w_o, v_g_post_mix, v_g_pre_mlp, v_w_up, v_w_down, v_g_post_mlp):
    seq, d = x.shape[1], x.shape[2]
    pool_w = pool_scale.shape[1]
    conv_w = b_dw.shape[1]
    n_grp, grp_rows, gdim = w_pool_grp.shape[1:]
    lp = _round_up(N_META + seq, ROW_CHUNK)
    tm_half = lp // 2 if (lp // 2) % 16 == 0 else lp
    c_idx = lax.axis_index("c").astype(jnp.int32)
    chip_idx = (2 * lax.axis_index("x") + lax.axis_index("y")).astype(jnp.int32)
    me_idx = 2 * chip_idx + c_idx

    pad_taps = ((0, CONV_TAPS_PADDED - CONV_KERNEL), (0, 0))
    big = dict(w_in=w_in[0], w_pool_grp=w_pool_grp[0].reshape(n_grp * grp_rows, gdim), w_pool_out=w_pool_out[0],
               w_conv_out=w_conv_out[0], w_o=w_o[0], w_up=w_up[0], w_down=w_down[0])
    big_names = list(big)
    moments = dict(w_in=(m_w_in, v_w_in), w_pool_grp=(m_w_pool_grp, v_w_pool_grp), w_pool_out=(m_w_pool_out, v_w_pool_out),
                   w_conv_out=(m_w_conv_out, v_w_conv_out), w_o=(m_w_o, v_w_o), w_up=(m_w_up, v_w_up),
                   w_down=(m_w_down, v_w_down))
    slot_idx = me_idx.reshape(1)
    sources = dict(big, meta=meta, w_dw=jnp.pad(w_dw[0], pad_taps))

    def fill(k, after):
        return _fill_slot(sources[k], slot_idx, BF16 if k in big else F32, after, "fill_" + k)

    gather_groups = [["meta", "w_dw"], ["w_in"], ["w_pool_grp", "w_pool_out", "w_conv_out", "w_o"], ["w_up"], ["w_down"]]
    started, token = _gather_start([[fill(k, slot_idx) for k in names] for names in gather_groups[:2]], slot_idx,
                                   "gather_start_first", issue_order=(0, 3, 1, 2))
    started_rest, _ = _gather_start([[fill(k, token) for k in names] for names in gather_groups[2:]], token,
                                    "gather_start_rest")
    started += started_rest
    wg = {}
    x_idx, y_idx = lax.axis_index("x"), lax.axis_index("y")
    at = lambda px, py, pc: 4 * px + 2 * py + pc
    near_slots = jnp.stack([at(x_idx, y_idx, c_idx), at(x_idx, y_idx, 1 - c_idx), at(1 - x_idx, y_idx, c_idx),
                            at(x_idx, 1 - y_idx, c_idx), at(1 - x_idx, y_idx, 1 - c_idx),
                            at(x_idx, 1 - y_idx, 1 - c_idx)]).astype(jnp.int32)
    far_slots = jnp.stack([at(1 - x_idx, 1 - y_idx, c_idx), at(1 - x_idx, 1 - y_idx, 1 - c_idx)]).astype(jnp.int32)

    def gather_whole(gi, after_forward, after_finish):
        send, recv, lands = started[gi]
        fs, fr, lands = _gather_forward(lands, recv, after_forward(), f"gather_forward_{gi}")
        lands = _gather_finish(lands, send, recv, [(ALL_CHIPS, fs, fr)], [(ALL_CHIPS, fs, fr)], after_finish(),
                               f"gather_finish_{gi}")
        wg.update(zip(gather_groups[gi], lands))

    near_state = {}

    def gather_near(gi, after):
        send, recv, lands = started[gi]
        fs, fr, lands = _gather_forward(lands, recv, after, f"gather_forward_near_{gi}", which=NEAR)
        lands = _gather_finish(lands, send, recv, [(NEAR, fs, fr)], [], after, f"gather_finish_near_{gi}",
                               direct_sends=False)
        near_state[gi] = (fs, fr)
        return lands

    def gather_far(gi, lands, after):
        send, recv, _ = started[gi]
        fs, fr, lands = _gather_forward(lands, recv, after, f"gather_forward_far_{gi}", which=FAR)
        lands = _gather_finish(lands, send, recv, [(FAR, fs, fr)], [(NEAR,) + near_state[gi], (FAR, fs, fr)], after,
                               f"gather_finish_far_{gi}", own=False)
        wg.update(zip(gather_groups[gi], lands))
        return lands

    gather_whole(0, lambda: token, lambda: token)
    meta_full = wg["meta"].transpose(1, 0, 2).reshape(N_META, d)
    wdw_full = wg["w_dw"].transpose(1, 0, 2).reshape(CONV_TAPS_PADDED, conv_w)
    target = loss_target[0]
    h0, u1 = _rms_pre(meta_full, x[0], g_pre_mix, lp, seq)
    send, recv, w_in_lands = started[1]
    w_in_lands = _gather_finish(w_in_lands, send, recv, [], [], u1, "gather_finish_home_1", direct_sends=False)
    proj = _mm_slots(u1, w_in_lands[0], near_slots[:2], over="n", tm=tm_half, name="mm_proj_home")
    fs_far, fr_far, w_in_lands = _gather_forward(w_in_lands, recv, proj, "gather_forward_far_1", which=FAR)
    w_in_lands = _gather_finish(w_in_lands, send, recv, [(FAR, fs_far, fr_far)], [], proj, "gather_finish_far_1",
                                own=False, direct_sends=False)
    proj = _mm_slots(u1, w_in_lands[0], far_slots, over="n", tm=tm_half, base=proj, name="mm_proj_far")
    fs_near, fr_near, w_in_lands = _gather_forward(w_in_lands, recv, proj, "gather_forward_near_1", which=NEAR)
    w_in_lands = _gather_finish(w_in_lands, send, recv, [(NEAR, fs_near, fr_near)],
                                [(FAR, fs_far, fr_far), (NEAR, fs_near, fr_near)], proj, "gather_finish_near_1",
                                own=False)
    proj = _mm_slots(u1, w_in_lands[0], near_slots[2:], over="n", tm=tm_half, base=proj, name="mm_proj_near")
    wg["w_in"] = w_in_lands[0]
    conv_c = _conv_fwd(proj, wdw_full, b_dw, lp, conv_w, pool_w)
    s_act = _ln_silu(conv_c, conv_ln_g, conv_ln_b, lp)
    gather_whole(2, lambda: proj, lambda: s_act)
    wpg_full = wg["w_pool_grp"].reshape(N_DEV, n_grp, grp_rows, gdim).transpose(1, 0, 2, 3).reshape(n_grp, gdim, gdim)
    w_o_full = wg["w_o"].reshape(d, d)
    ya_pre = _pool_fwd(proj, wpg_full, pool_scale, lp)
    m_mix, y_a, y_b = _gate_mix(ya_pre, s_act, wg["w_pool_out"], wg["w_conv_out"], proj, d, "gate_mix")
    o = _mm(m_mix, w_o_full, mode="nn", tm=tm_half, tn=512, name="mm_o")
    h1, u2 = _post_mix(o, h0, g_post_mix, g_pre_mlp, lp)
    relu2 = lambda acc: jnp.square(jnp.maximum(acc, 0.0))
    (w_up_near,) = gather_near(3, u2)
    act = _mm_slots(u2, w_up_near, near_slots, over="n", tm=tm_half, out_dtype=BF16, epilogue=relu2, name="mm_up_near")
    (w_up_all,) = gather_far(3, [w_up_near], act)
    act = _mm_slots(u2, w_up_all, far_slots, over="n", tm=tm_half, out_dtype=BF16, epilogue=relu2, base=act,
                    name="mm_up_far")
    (w_down_near,) = gather_near(4, act)
    f = _mm_slots(act, w_down_near, near_slots, over="k", tm=tm_half, tn=d, name="mm_down_near")
    (w_down_all,) = gather_far(4, [w_down_near], f)
    f = _mm_slots(act, w_down_all, far_slots, over="k", tm=tm_half // 2, tn=d, base=f, name="mm_down_far")
    w_down_full = w_down_all.reshape(-1, d)

    big_out = {}

    def to_sibling(names, grads, after, tag):
        send, recv, grads, lands, token = _pair_exchange_start(grads, after, "grads_to_sibling_start_" + tag)
        return (names, send, recv, grads, lands, tag), token

    where = jnp.stack([c_idx, chip_idx, 2 * (1 - x_idx) + y_idx, 2 * x_idx + (1 - y_idx),
                       2 * (1 - x_idx) + (1 - y_idx)]).astype(jnp.int32)

    def to_owner(handle, after):
        names, send, recv, grads, from_sib, tag = handle
        grads, from_sib = _pair_exchange_finish(grads, from_sib, send, recv, after, "grads_to_sibling_finish_" + tag)
        own = [g.reshape((4, 2) + g.shape[1:]) for g in grads]
        sums = [_pair_sum(o, r, where, "pair_sum_" + k) for k, o, r in zip(names, own, from_sib)]
        send, recv, sums, lands, token = _chip_exchange_start(sums, after, "grads_to_owner_start_" + tag)
        return (names, send, recv, sums, lands, own, from_sib, tag), token

    def update(handle, after):
        names, send, recv, sums, lands, own, from_sib, tag = handle
        got = _chip_exchange_finish(sums, lands, send, recv, after, "grads_to_owner_finish_" + tag)
        for k, o, s, r3 in zip(names, own, from_sib, got):
            w2 = big[k]
            shape = moments[k][0].shape
            outs = _adamw_big(w2, moments[k][0].reshape(w2.shape), moments[k][1].reshape(w2.shape), o, s, r3,
                              where, "adamw_" + k)
            big_out[k] = [a.reshape(shape) for a in outs]
        return big_out[names[-1]][0]

    df, dh, dg_post_mlp, loss_part = _loss_head(f, h1, target, g_post_mlp, lp, seq)
    d_up = _mm(df, w_down_full, mode="nt", tm=tm_half, tn=1024, out_dtypes=(BF16,), extras=[(act, 0)],
               epilogue=lambda acc, a: (acc * (2.0 * jnp.sqrt(a.astype(F32))),), row_splits=2, name="mm_d_up")
    g_w_down = _mm(act, df, mode="tn", tm=1024, tn=1024, out_dtypes=(BF16,), name="mm_g_down")
    sib_down, token = to_sibling(["w_down"], [g_w_down.reshape(N_DEV, -1, d)], slot_idx, "down")
    g_w_up = _mm(u2, d_up, mode="tn", tm=1024, tn=wg["w_up"].shape[2], out_blocked=True, out_dtypes=(BF16,),
                 after=token, name="mm_g_up")
    sib_up, token = to_sibling(["w_up"], [g_w_up], slot_idx, "up")
    pending_down, token = to_owner(sib_down, token)
    du2 = _mm(d_up, wg["w_up"], mode="nt", tm=tm_half, tn=1024, b_blocked=True, kb=2, after=token, name="mm_du2")
    pending_up, token = to_owner(sib_up, du2)
    dh1, do, dg_pre_mlp, dg_post_mix = _mid_bwd(du2, h1, dh, o, g_pre_mlp, g_post_mix, lp)

    def gate_bwd(dm, ga, gb, ya, yb):
        sa, sb = jax.nn.sigmoid(ga), jax.nn.sigmoid(gb)
        return (dm * ya.astype(F32) * sa * (1.0 - sa), dm * yb.astype(F32) * sb * (1.0 - sb), dm * sa, dm * sb)

    gate_tn = 512
    ga_col0, gb_col0 = proj.shape[1] - 2 * d, proj.shape[1] - d
    dproj, d_gb, d_ya, d_yb = _mm(
        do, w_o_full, mode="nt", tm=tm_half, tn=gate_tn, out_dtypes=(BF16,) * 4,
        extras=[(proj, ga_col0 // gate_tn), (proj, gb_col0 // gate_tn), (y_a, 0), (y_b, 0)], epilogue=gate_bwd,
        out_places=((proj.shape[1], ga_col0), None, None, None), after=token, row_splits=2, name="mm_dm")
    g_w_o = _mm(m_mix, do, mode="tn", tm=1024, tn=1024, out_dtypes=(BF16,), name="mm_g_o")
    bn_out = wg["w_pool_out"].shape[2]
    g_w_pool_out = _mm(ya_pre, d_ya, mode="tn", tm=pool_w, tn=bn_out, out_blocked=True, out_dtypes=(BF16,),
                       name="mm_g_pool_out")
    g_w_conv_out = _mm(s_act, d_yb, mode="tn", tm=conv_w, tn=bn_out, out_blocked=True, out_dtypes=(BF16,),
                       name="mm_g_conv_out")
    sib_mix, token = to_sibling(["w_o", "w_pool_out", "w_conv_out"],
                                [g_w_o.reshape(N_DEV, -1, d), g_w_pool_out, g_w_conv_out], slot_idx, "mix")
    d_ya_pre = _mm(d_ya, wg["w_pool_out"], mode="nt", tm=tm_half, tn=pool_w, b_blocked=True, after=token,
                   name="mm_d_ya_pre")
    d_s = _mm(d_yb, wg["w_conv_out"], mode="nt", tm=tm_half, tn=conv_w, b_blocked=True, name="mm_d_s")
    pending_mix, token = to_owner(sib_mix, d_s)
    dproj, g_wpg, d_scale = _pool_bwd(proj, d_ya_pre, wpg_full, pool_scale, dproj, lp)
    dc, d_ln_g, d_ln_b, d_b_dw = _ln_silu_bwd(conv_c, d_s, conv_ln_g, conv_ln_b, lp)
    dproj, dgc, g_wdw = _conv_bwd(proj, dc, wdw_full, dproj, lp, conv_w, pool_w)
    dproj = _place_columns(dproj, [(dgc, pool_w + conv_w), (d_gb, gb_col0)], "place_dproj")
    g_w_in = _mm(u1, dproj, mode="tn", tm=1024, tn=wg["w_in"].shape[2], out_blocked=True, out_dtypes=(BF16,),
                 after=token, name="mm_g_in")
    g_wpg_slots = g_wpg.astype(BF16).reshape(n_grp, N_DEV, grp_rows, gdim).transpose(1, 0, 2, 3)
    sib_in, token = to_sibling(["w_pool_grp", "w_in"],
                               [g_wpg_slots.reshape(N_DEV, n_grp * grp_rows, gdim), g_w_in], slot_idx, "in")
    done = update(pending_down, token)
    pending_in, token = to_owner(sib_in, done)
    done = update(pending_up, token)
    du1 = _mm(dproj, wg["w_in"], mode="nt", tm=tm_half, tn=1024, b_blocked=True, kb=2, after=done, name="mm_du1")
    grad_x2, grad_meta_part, dg_pre_mix = _pre_mix_bwd(du1, h0, dh1, g_pre_mix, seq)
    grad_x = grad_x2[None]

    assert pool_w + conv_w == d and conv_w <= d and LANES <= d
    widen = lambda a: jnp.pad(a, ((0, 0), (0, d - a.shape[1])))
    packed = jnp.concatenate([
        dg_pre_mix, dg_post_mix, dg_pre_mlp, dg_post_mlp,
        jnp.concatenate([d_scale, d_ln_g], axis=1), jnp.concatenate([d_ln_b, d_b_dw], axis=1),
        grad_meta_part, widen(g_wdw), widen(loss_part)], axis=0)
    rep = dict(g_pre_mix=((g_pre_mix, m_g_pre_mix, v_g_pre_mix), (0, 0)),
               g_post_mix=((g_post_mix, m_g_post_mix, v_g_post_mix), (SUBLANES, 0)),
               g_pre_mlp=((g_pre_mlp, m_g_pre_mlp, v_g_pre_mlp), (2 * SUBLANES, 0)),
               g_post_mlp=((g_post_mlp, m_g_post_mlp, v_g_post_mlp), (3 * SUBLANES, 0)),
               pool_scale=((pool_scale, m_pool_scale, v_pool_scale), (4 * SUBLANES, 0)),
               conv_ln_g=((conv_ln_g, m_conv_ln_g, v_conv_ln_g), (4 * SUBLANES, pool_w)),
               conv_ln_b=((conv_ln_b, m_conv_ln_b, v_conv_ln_b), (5 * SUBLANES, 0)),
               b_dw=((b_dw, m_b_dw, v_b_dw), (5 * SUBLANES, conv_w)))
    meta_row0 = 6 * SUBLANES
    wdw_row0 = meta_row0 + N_META
    loss_row0 = wdw_row0 + CONV_TAPS_PADDED
    (small_started,), token = _gather_start([[_fill_slot(packed, slot_idx, F32, slot_idx, "fill_small")]], grad_x2,
                                            "gather_small_start")
    done = update(pending_mix, token)
    done = update(pending_in, done)
    send, recv, lands = small_started
    fsend, frecv, lands = _gather_forward(lands, recv, done, "gather_small_forward")
    (packed_all,) = _gather_finish(lands, send, recv, [(ALL_CHIPS, fsend, frecv)], [(ALL_CHIPS, fsend, frecv)], done,
                                   "gather_small_finish")
    rep_names = list(rep)
    wdw_wmv = [jnp.pad(a[0], pad_taps) for a in (w_dw, m_w_dw, v_w_dw)]
    rep_out, meta_out, wdw_out, loss_blk = _small_update(
        slot_idx, packed_all, [rep[k][0] for k in rep_names], [rep[k][1] for k in rep_names],
        (meta, m_meta, v_meta), meta_row0, wdw_wmv, wdw_row0, loss_row0)
    small_out = dict(zip(rep_names, rep_out))
    small_out["meta"] = meta_out
    small_out["w_dw"] = [a[:CONV_KERNEL][None] for a in wdw_out]

    order = ["meta", "g_pre_mix", "w_in", "w_pool_grp", "pool_scale", "w_pool_out", "w_dw", "b_dw", "conv_ln_g",
             "conv_ln_b", "w_conv_out", "w_o", "g_post_mix", "g_pre_mlp", "w_up", "w_down", "g_post_mlp"]
    by_name = {**big_out, **small_out}
    result = [loss_blk[0, 0], grad_x]
    for kind in range(4):
        result += [by_name[k][kind] for k in order]
    return tuple(result)
```

```python
import functools

import jax
import jax.numpy as jnp
from jax import lax
from jax.experimental import pallas as pl
from jax.experimental.pallas import tpu as pltpu

F32 = jnp.float32
BF16 = jnp.bfloat16
MESH = pl.DeviceIdType.MESH

N_DEV = 8
N_META = 16
POOL_WINDOWS = (2, 4, 8, 16)
CONV_KERNEL = 31
CONV_TAPS_PADDED = 32
RMS_EPS = 1e-6
LN_EPS = 1e-5
ADAM_LR = 0.001
ADAM_B1 = 0.9
ADAM_B2 = 0.999
ADAM_EPS = 1e-08
ADAM_WD = 0.01
ADAM_STEP = 10

LANES = 128
SUBLANES = 8
ROW_CHUNK = 128
HALO = 32
VMEM_LIMIT_BYTES = 56 * 1024 * 1024


def _cparams(*sem):
    return pltpu.CompilerParams(dimension_semantics=sem if sem else None, vmem_limit_bytes=VMEM_LIMIT_BYTES)


def _round_up(n, m):
    return (n + m - 1) // m * m


def _row_tile(rows, cols, max_elems=640 * 1024):
    best = None
    for t in range(16, rows + 1, 16):
        if rows % t == 0 and (best is None or t * cols <= max_elems):
            best = t
    assert best is not None, (rows, cols)
    return best


def _rowsum8(a):
    t, w = a.shape
    return a.reshape(t // SUBLANES, SUBLANES, w).sum(axis=0)


def _mesh_pos():
    return lax.axis_index("x"), lax.axis_index("y"), lax.axis_index("c")


HBM_SPEC = pl.BlockSpec(memory_space=pltpu.HBM)
SEM_SPEC = pl.BlockSpec(memory_space=pltpu.SEMAPHORE)
ANY_SPEC = pl.BlockSpec(memory_space=pl.ANY)
_DATAFLOW = pltpu.SideEffectType.DATAFLOW_SIDE_EFFECTING


def _hbm(a):
    return pltpu.with_memory_space_constraint(a, pltpu.HBM)


def _slot(p):
    return 4 * p[0] + 2 * p[1] + p[2]


def _fill_slot(w, slot_idx, dtype, after, name):
    rows, cols = w.shape
    tr = _row_tile(rows, cols) if rows % 16 == 0 else rows

    def body(idx_ref, w_ref, _, out_ref):
        out_ref[...] = w_ref[...].astype(dtype)

    return pl.pallas_call(
        body, name=name,
        grid_spec=pltpu.PrefetchScalarGridSpec(
            num_scalar_prefetch=1, grid=(rows // tr,),
            in_specs=[pl.BlockSpec((tr, cols), lambda i, idx_ref: (i, 0)), ANY_SPEC],
            out_specs=pl.BlockSpec((None, tr, cols), lambda i, idx_ref: (idx_ref[0], i, 0))),
        out_shape=jax.ShapeDtypeStruct((N_DEV, rows, cols), dtype),
        compiler_params=_cparams("parallel"),
    )(slot_idx, w, after)


def _gather_start(groups, after, name, issue_order=(0, 1, 2, 3)):
    flat = [g for grp in groups for g in grp]
    n, n_grp = len(flat), len(groups)

    def body(*refs):
        lands = refs[:n]
        sems = refs[n + 1:n + 1 + 2 * n_grp]
        token = refs[-1]
        x, y, c = _mesh_pos()
        targets = [(x, y, 1 - c), (1 - x, y, c), (x, 1 - y, c), (1 - x, 1 - y, c)]
        t = 0
        for gi, grp in enumerate(groups):
            for ti in range(len(grp)):
                mine = lands[t].at[_slot((x, y, c))]
                for k in issue_order:
                    pltpu.make_async_remote_copy(
                        src_ref=mine, dst_ref=mine,
                        send_sem=sems[2 * gi].at[4 * ti + k], recv_sem=sems[2 * gi + 1].at[4 * ti + k],
                        device_id=targets[k], device_id_type=MESH).start()
                t += 1
        token[...] = jnp.zeros_like(token)

    sem_shapes = []
    for grp in groups:
        sem_shapes += [pltpu.SemaphoreType.DMA((4 * len(grp),))] * 2
    outs = pl.pallas_call(
        body, name=name,
        out_shape=tuple(sem_shapes + [pltpu.HBM(g.shape, g.dtype) for g in flat]
                        + [jax.ShapeDtypeStruct((SUBLANES, LANES), F32)]),
        in_specs=tuple([HBM_SPEC] * n + [ANY_SPEC]),
        out_specs=tuple([SEM_SPEC] * (2 * n_grp) + [HBM_SPEC] * n + [pl.BlockSpec(memory_space=pltpu.VMEM)]),
        input_output_aliases={i: 2 * n_grp + i for i in range(n)},
        compiler_params=pltpu.CompilerParams(has_side_effects=_DATAFLOW),
    )(*[_hbm(g) for g in flat], after)
    sems, lands, token = outs[:2 * n_grp], outs[2 * n_grp:-1], outs[-1]
    res, t = [], 0
    for gi, grp in enumerate(groups):
        res.append((sems[2 * gi], sems[2 * gi + 1], list(lands[t:t + len(grp)])))
        t += len(grp)
    return res, token


NEAR, FAR, ALL_CHIPS = (0, 1), (2,), (0, 1, 2)


def _gather_forward(lands, recv_sems, after, name, which=ALL_CHIPS):
    n, nw = len(lands), len(which)

    def body(*refs):
        land_refs, recv, _ = refs[:n], refs[n], refs[n + 1]
        fsend, frecv = refs[n + 2], refs[n + 3]
        x, y, c = _mesh_pos()
        chips = [(1 - x, y), (x, 1 - y), (1 - x, 1 - y)]
        for t in range(n):
            for i, j in enumerate(which):
                blk = land_refs[t].at[_slot((*chips[j], c))]
                pltpu.make_async_remote_copy(src_ref=blk, dst_ref=blk, send_sem=fsend.at[nw * t + i],
                                             recv_sem=recv.at[4 * t + 1 + j],
                                             device_id=(x, y, 1 - c), device_id_type=MESH).wait_recv()
                pltpu.make_async_remote_copy(src_ref=blk, dst_ref=blk, send_sem=fsend.at[nw * t + i],
                                             recv_sem=frecv.at[nw * t + i],
                                             device_id=(x, y, 1 - c), device_id_type=MESH).start()

    outs = pl.pallas_call(
        body, name=name,
        out_shape=tuple([pltpu.SemaphoreType.DMA((nw * n,))] * 2 + [pltpu.HBM(g.shape, g.dtype) for g in lands]),
        in_specs=tuple([HBM_SPEC] * n + [SEM_SPEC, ANY_SPEC]),
        out_specs=tuple([SEM_SPEC] * 2 + [HBM_SPEC] * n),
        input_output_aliases={i: 2 + i for i in range(n)},
        compiler_params=pltpu.CompilerParams(has_side_effects=_DATAFLOW),
    )(*lands, recv_sems, after)
    return outs[0], outs[1], list(outs[2:])


def _gather_finish(lands, send_sems, recv_sems, arrivals, sends, after, name, own=True, direct_sends=True):
    n = len(lands)
    fwd = list(arrivals) + list(sends)
    sem_args = [send_sems, recv_sems]
    where = []
    for _, fs, fr in fwd:
        pos = []
        for arr in (fs, fr):
            hit = [i for i, have in enumerate(sem_args) if have is arr]
            if not hit:
                sem_args.append(arr)
                hit = [len(sem_args) - 1]
            pos.append(hit[0])
        where.append(pos)

    def body(*refs):
        land_refs = refs[:n]
        send, recv = refs[n], refs[n + 1]
        fwd_refs = [refs[n + p] for pos in where for p in pos]
        x, y, c = _mesh_pos()
        sibling = (x, y, 1 - c)
        chips = [(1 - x, y), (x, 1 - y), (1 - x, 1 - y)]

        def desc(ref, s_sem, r_sem):
            return pltpu.make_async_remote_copy(src_ref=ref, dst_ref=ref, send_sem=s_sem, recv_sem=r_sem,
                                                device_id=sibling, device_id_type=MESH)

        for t in range(n):
            mine = land_refs[t].at[_slot((x, y, c))]
            if own:
                desc(land_refs[t].at[_slot(sibling)], send.at[4 * t], recv.at[4 * t]).wait_recv()
            for a, (which, _, _) in enumerate(fwd):
                fs, fr = fwd_refs[2 * a], fwd_refs[2 * a + 1]
                for i, j in enumerate(which):
                    if a < len(arrivals):
                        desc(land_refs[t].at[_slot((*chips[j], 1 - c))], fs.at[len(which) * t + i],
                             fr.at[len(which) * t + i]).wait_recv()
                    else:
                        desc(land_refs[t].at[_slot((*chips[j], c))], fs.at[len(which) * t + i],
                             fr.at[len(which) * t + i]).wait_send()
            if direct_sends:
                for k in range(4):
                    desc(mine, send.at[4 * t + k], recv.at[4 * t + k]).wait_send()

    outs = pl.pallas_call(
        body, name=name,
        out_shape=tuple(pltpu.HBM(g.shape, g.dtype) for g in lands),
        in_specs=tuple([HBM_SPEC] * n + [SEM_SPEC] * len(sem_args) + [ANY_SPEC]),
        out_specs=tuple([HBM_SPEC] * n),
        input_output_aliases={i: i for i in range(n)},
        compiler_params=pltpu.CompilerParams(has_side_effects=_DATAFLOW),
    )(*lands, *sem_args, after)
    return list(outs)


def _pair_exchange_start(grads, after, name):
    n = len(grads)

    def body(*refs):
        ins, lands = refs[:n], refs[n:2 * n]
        send, recv = refs[2 * n + 1], refs[2 * n + 2]
        token = refs[-1]
        x, y, c = _mesh_pos()
        for t in range(n):
            for q in range(4):
                pltpu.make_async_remote_copy(
                    src_ref=ins[t].at[2 * q + 1 - c], dst_ref=lands[t].at[q],
                    send_sem=send.at[4 * t + q], recv_sem=recv.at[4 * t + q],
                    device_id=(x, y, 1 - c), device_id_type=MESH).start()
        token[...] = jnp.zeros_like(token)

    land_shapes = [(4,) + g.shape[1:] for g in grads]
    outs = pl.pallas_call(
        body, name=name,
        out_shape=tuple([pltpu.SemaphoreType.DMA((4 * n,))] * 2 + [pltpu.HBM(g.shape, g.dtype) for g in grads]
                        + [pltpu.HBM(ls, g.dtype) for ls, g in zip(land_shapes, grads)]
                        + [jax.ShapeDtypeStruct((SUBLANES, LANES), F32)]),
        in_specs=tuple([HBM_SPEC] * (2 * n) + [ANY_SPEC]),
        out_specs=tuple([SEM_SPEC] * 2 + [HBM_SPEC] * (2 * n) + [pl.BlockSpec(memory_space=pltpu.VMEM)]),
        input_output_aliases={i: 2 + i for i in range(2 * n)},
        compiler_params=pltpu.CompilerParams(has_side_effects=_DATAFLOW),
    )(*[_hbm(g) for g in grads], *[_hbm(lax.empty(ls, g.dtype)) for ls, g in zip(land_shapes, grads)], after)
    return outs[0], outs[1], list(outs[2:2 + n]), list(outs[2 + n:2 + 2 * n]), outs[-1]


def _pair_exchange_finish(grads, lands, send_sems, recv_sems, after, name):
    n = len(grads)

    def body(*refs):
        ins, land_refs = refs[:n], refs[n:2 * n]
        send, recv = refs[2 * n], refs[2 * n + 1]
        x, y, c = _mesh_pos()
        for t in range(n):
            for q in range(4):
                cp = pltpu.make_async_remote_copy(
                    src_ref=ins[t].at[q], dst_ref=land_refs[t].at[q], send_sem=send.at[4 * t + q],
                    recv_sem=recv.at[4 * t + q], device_id=(x, y, 1 - c), device_id_type=MESH)
                cp.wait_send()
                cp.wait_recv()

    outs = pl.pallas_call(
        body, name=name,
        out_shape=tuple([pltpu.HBM(g.shape, g.dtype) for g in grads] + [pltpu.HBM(g.shape, g.dtype) for g in lands]),
        in_specs=tuple([HBM_SPEC] * (2 * n) + [SEM_SPEC] * 2 + [ANY_SPEC]),
        out_specs=tuple([HBM_SPEC] * (2 * n)),
        input_output_aliases={i: i for i in range(2 * n)},
        compiler_params=pltpu.CompilerParams(has_side_effects=_DATAFLOW),
    )(*grads, *lands, send_sems, recv_sems, after)
    return list(outs[:n]), list(outs[n:])


def _chip_exchange_start(sums, after, name):
    n = len(sums)

    def body(*refs):
        ins, lands = refs[:n], refs[n:2 * n]
        send, recv = refs[2 * n + 1], refs[2 * n + 2]
        token = refs[-1]
        x, y, c = _mesh_pos()
        chips = [(1 - x, y), (x, 1 - y), (1 - x, 1 - y)]
        for t in range(n):
            for r, chip in enumerate(chips):
                pltpu.make_async_remote_copy(
                    src_ref=ins[t].at[2 * chip[0] + chip[1]], dst_ref=lands[t].at[r],
                    send_sem=send.at[3 * t + r], recv_sem=recv.at[3 * t + r],
                    device_id=(*chip, c), device_id_type=MESH).start()
        token[...] = jnp.zeros_like(token)

    land_shapes = [(3,) + s.shape[1:] for s in sums]
    outs = pl.pallas_call(
        body, name=name,
        out_shape=tuple([pltpu.SemaphoreType.DMA((3 * n,))] * 2 + [pltpu.HBM(s.shape, s.dtype) for s in sums]
                        + [pltpu.HBM(ls, s.dtype) for ls, s in zip(land_shapes, sums)]
                        + [jax.ShapeDtypeStruct((SUBLANES, LANES), F32)]),
        in_specs=tuple([HBM_SPEC] * (2 * n) + [ANY_SPEC]),
        out_specs=tuple([SEM_SPEC] * 2 + [HBM_SPEC] * (2 * n) + [pl.BlockSpec(memory_space=pltpu.VMEM)]),
        input_output_aliases={i: 2 + i for i in range(2 * n)},
        compiler_params=pltpu.CompilerParams(has_side_effects=_DATAFLOW),
    )(*[_hbm(s) for s in sums], *[_hbm(lax.empty(ls, s.dtype)) for ls, s in zip(land_shapes, sums)], after)
    return outs[0], outs[1], list(outs[2:2 + n]), list(outs[2 + n:2 + 2 * n]), outs[-1]


def _chip_exchange_finish(sums, lands, send_sems, recv_sems, after, name):
    n = len(sums)

    def body(*refs):
        ins, land_refs = refs[:n], refs[n:2 * n]
        send, recv = refs[2 * n], refs[2 * n + 1]
        x, y, c = _mesh_pos()
        for t in range(n):
            for r in range(3):
                cp = pltpu.make_async_remote_copy(
                    src_ref=ins[t].at[r], dst_ref=land_refs[t].at[r], send_sem=send.at[3 * t + r],
                    recv_sem=recv.at[3 * t + r],
                    device_id=(x, y, 1 - c), device_id_type=MESH)
                cp.wait_send()
                cp.wait_recv()

    outs = pl.pallas_call(
        body, name=name,
        out_shape=tuple(pltpu.HBM(g.shape, g.dtype) for g in lands),
        in_specs=tuple([HBM_SPEC] * (2 * n) + [SEM_SPEC] * 2 + [ANY_SPEC]),
        out_specs=tuple([HBM_SPEC] * n),
        input_output_aliases={n + i: i for i in range(n)},
        compiler_params=pltpu.CompilerParams(has_side_effects=_DATAFLOW),
    )(*sums, *lands, send_sems, recv_sems, after)
    return list(outs)


def _mm(a, b, *, mode, tm, tn, tk=None, b_blocked=False, out_blocked=False, out_dtypes=(F32,),
        epilogue=None, extras=(), after=None, kb=1, out_places=None, row_splits=1, name):
    if mode == "nn":
        m, k = a.shape
        n = b.shape[0] * b.shape[2] if b_blocked else b.shape[1]
        dims = (((1,), (0,)), ((), ()))
    elif mode == "nt":
        m, k = a.shape
        n = b.shape[1] if b_blocked else b.shape[0]
        if b_blocked:
            tk = kb * b.shape[2]
        dims = (((1,), (1,)), ((), ()))
    else:
        k, m = a.shape
        n = b.shape[1]
        dims = (((0,), (0,)), ((), ()))
    tk = k if tk is None else tk
    assert m % tm == 0 and n % tn == 0 and k % tk == 0, (name, m, n, k, tm, tn, tk)
    gm, gn, gk = m // tm, n // tn, k // tk
    if b_blocked:
        assert (tn if mode == "nn" else tk) == kb * b.shape[2], name
    if row_splits > 1:
        assert gk == 1 and epilogue is not None and mode != "tn" and not b_blocked and tm % (16 * row_splits) == 0, name

    if mode == "nn":
        a_spec = pl.BlockSpec((tm, tk), lambda i, j, kk: (i, kk))
        b_spec = (pl.BlockSpec((None, tk, tn), lambda i, j, kk: (j, kk, 0)) if b_blocked
                  else pl.BlockSpec((tk, tn), lambda i, j, kk: (kk, j)))
    elif mode == "nt":
        a_spec = pl.BlockSpec((tm, tk), lambda i, j, kk: (i, kk))
        b_spec = (pl.BlockSpec((kb, tn, tk // kb), lambda i, j, kk: (kk, j, 0)) if b_blocked
                  else pl.BlockSpec((tn, tk), lambda i, j, kk: (j, kk)))
    else:
        a_spec = pl.BlockSpec((tk, tm), lambda i, j, kk: (kk, i))
        b_spec = pl.BlockSpec((tk, tn), lambda i, j, kk: (kk, j))
    if out_blocked:
        out_spec = pl.BlockSpec((None, tm, tn), lambda i, j, kk: (j, i, 0))
        out_shape = (gn, m, tn)
    else:
        out_spec = pl.BlockSpec((tm, tn), lambda i, j, kk: (i, j))
        out_shape = (m, n)
    extra_specs = [pl.BlockSpec((tm, tn), functools.partial(lambda i, j, kk, off: (i, j + off), off=off))
                   for _, off in extras]
    n_extra, n_out = len(extras), len(out_dtypes)
    n_after = 0 if after is None else 1
    places = out_places if out_places is not None else (None,) * n_out

    def body(a_ref, b_ref, *rest):
        extra_refs = rest[:n_extra]
        out_refs = rest[n_extra + n_after:n_extra + n_after + n_out]

        def finish(acc):
            if epilogue is None:
                res = (acc,)
            else:
                res = epilogue(acc, *[e[...] for e in extra_refs])
            for o_ref, r in zip(out_refs, res):
                o_ref[...] = r.astype(o_ref.dtype)

        if row_splits > 1:
            strip = tm // row_splits
            for h in range(row_splits):
                rows = slice(h * strip, (h + 1) * strip)
                acc = lax.dot_general(a_ref[rows, :], b_ref[...], dims, preferred_element_type=F32)
                res = epilogue(acc, *[e[rows, :] for e in extra_refs])
                for o_ref, r in zip(out_refs, res):
                    o_ref[rows, :] = r.astype(o_ref.dtype)
            return
        if mode == "nt" and b_blocked:
            bk = tk // kb
            part = lax.dot_general(a_ref[:, :bk], b_ref[0], dims, preferred_element_type=F32)
            for h in range(1, kb):
                part = part + lax.dot_general(a_ref[:, h * bk:(h + 1) * bk], b_ref[h], dims,
                                              preferred_element_type=F32)
        else:
            part = lax.dot_general(a_ref[...], b_ref[...], dims, preferred_element_type=F32)
        if gk == 1:
            finish(part)
        else:
            acc_ref = rest[-1]
            kk = pl.program_id(2)

            @pl.when(kk == 0)
            def _():
                acc_ref[...] = part

            @pl.when(kk > 0)
            def _():
                acc_ref[...] += part

            @pl.when(kk == gk - 1)
            def _():
                finish(acc_ref[...])

    outs = pl.pallas_call(
        body, name=name, grid=(gm, gn, gk),
        in_specs=[a_spec, b_spec] + extra_specs + [ANY_SPEC] * n_after,
        out_specs=[out_spec if place is None else
                   pl.BlockSpec((tm, tn), functools.partial(lambda i, j, kk, off: (i, j + off), off=place[1] // tn))
                   for place in places],
        out_shape=[jax.ShapeDtypeStruct(out_shape if place is None else (m, place[0]), dt)
                   for dt, place in zip(out_dtypes, places)],
        scratch_shapes=[pltpu.VMEM((tm, tn), F32)] if gk > 1 else [],
        compiler_params=_cparams("parallel", "parallel", "arbitrary"),
    )(a, b, *[e for e, _ in extras], *([] if after is None else [after]))
    return outs[0] if n_out == 1 else outs


def _mm_slots(a, w, slots, *, over, tm, tn=None, out_dtype=F32, epilogue=None, base=None, name):
    m = a.shape[0]
    ns = slots.shape[0]
    n_slots, w1, w2 = w.shape
    assert m % tm == 0
    if over == "n":
        k, bn = w1, w2

        def body(slots_ref, a_ref, w_ref, *rest):
            out_ref = rest[-1]
            acc = jnp.dot(a_ref[...], w_ref[...], preferred_element_type=F32)
            out_ref[...] = (acc if epilogue is None else epilogue(acc)).astype(out_ref.dtype)

        in_specs = [pl.BlockSpec((tm, k), lambda i, j, s: (i, 0)),
                    pl.BlockSpec((None, k, bn), lambda i, j, s: (s[j], 0, 0))]
        args = [a, w]
        aliases = {}
        if base is not None:
            in_specs.append(ANY_SPEC)
            args.append(base)
            aliases = {3: 0}
        return pl.pallas_call(
            body, name=name,
            grid_spec=pltpu.PrefetchScalarGridSpec(
                num_scalar_prefetch=1, grid=(m // tm, ns), in_specs=in_specs,
                out_specs=pl.BlockSpec((tm, bn), lambda i, j, s: (i, s[j]))),
            out_shape=jax.ShapeDtypeStruct((m, n_slots * bn), out_dtype),
            input_output_aliases=aliases,
            compiler_params=_cparams("parallel", "arbitrary"),
        )(slots, *args)

    bk, n = w1, w2
    tn = n if tn is None else tn
    assert n % tn == 0

    def body(slots_ref, a_ref, w_ref, *rest):
        out_ref, acc_ref = rest[-2], rest[-1]
        kk = pl.program_id(2)
        part = jnp.dot(a_ref[...], w_ref[...], preferred_element_type=F32)

        @pl.when(kk == 0)
        def _():
            acc_ref[...] = part if base is None else part + rest[0][...]

        @pl.when(kk > 0)
        def _():
            acc_ref[...] += part

        @pl.when(kk == ns - 1)
        def _():
            out_ref[...] = acc_ref[...].astype(out_ref.dtype)

    in_specs = [pl.BlockSpec((tm, bk), lambda i, j, kk, s: (i, s[kk])),
                pl.BlockSpec((None, bk, tn), lambda i, j, kk, s: (s[kk], 0, j))]
    args = [a, w]
    if base is not None:
        in_specs.append(pl.BlockSpec((tm, tn), lambda i, j, kk, s: (i, j)))
        args.append(base)
    return pl.pallas_call(
        body, name=name,
        grid_spec=pltpu.PrefetchScalarGridSpec(
            num_scalar_prefetch=1, grid=(m // tm, n // tn, ns), in_specs=in_specs,
            out_specs=pl.BlockSpec((tm, tn), lambda i, j, kk, s: (i, j)),
            scratch_shapes=[pltpu.VMEM((tm, tn), F32)]),
        out_shape=jax.ShapeDtypeStruct((m, n), out_dtype),
        compiler_params=_cparams("parallel", "parallel", "arbitrary"),
    )(slots, *args)


def _gate_mix(ya_pre, s, wpo, wco, proj, d_model, name):
    lp, width = ya_pre.shape
    nb, _, bn = wpo.shape
    ga_off = (proj.shape[1] - 2 * d_model) // bn
    gb_off = (proj.shape[1] - d_model) // bn

    n_strips = 4 if lp % 64 == 0 else 1

    def body(ya_ref, s_ref, wpo_ref, wco_ref, ga_ref, gb_ref, m_ref, y_a_ref, y_b_ref):
        strip = lp // n_strips
        for h in range(n_strips):
            rows = slice(h * strip, (h + 1) * strip)
            y_a = jnp.dot(ya_ref[rows, :], wpo_ref[...], preferred_element_type=F32)
            y_b = jnp.dot(s_ref[rows, :], wco_ref[...], preferred_element_type=F32)
            m = jax.nn.sigmoid(ga_ref[rows, :]) * y_a + jax.nn.sigmoid(gb_ref[rows, :]) * y_b
            m_ref[rows, :] = m.astype(BF16)
            y_a_ref[rows, :] = y_a.astype(BF16)
            y_b_ref[rows, :] = y_b.astype(BF16)

    act_spec = pl.BlockSpec((lp, width), lambda j: (0, 0))
    w_spec = pl.BlockSpec((None, width, bn), lambda j: (j, 0, 0))
    out_spec = pl.BlockSpec((lp, bn), lambda j: (0, j))
    return pl.pallas_call(
        body, name=name, grid=(nb,),
        in_specs=[act_spec, act_spec, w_spec, w_spec,
                  pl.BlockSpec((lp, bn), lambda j: (0, j + ga_off)),
                  pl.BlockSpec((lp, bn), lambda j: (0, j + gb_off))],
        out_specs=[out_spec] * 3,
        out_shape=[jax.ShapeDtypeStruct((lp, nb * bn), BF16)] * 3,
        compiler_params=_cparams("parallel"),
    )(ya_pre, s, wpo, wco, proj, proj)


def _rms_stats(x):
    return lax.rsqrt(jnp.mean(x * x, axis=-1, keepdims=True) + RMS_EPS)


def _rms_bwd(x, g, dy):
    r = _rms_stats(x)
    nrm = x * r
    dn = dy * g
    dx = r * (dn - nrm * jnp.mean(dn * nrm, axis=-1, keepdims=True))
    return dx, dy * nrm


def _rowwise(body, ins, outs, accs, *, lp, name):
    tr = _row_tile(lp, max(a.shape[1] for a in ins))
    n_in, n_out, n_acc = len(ins), len(outs), len(accs)

    def kernel_body(*refs):
        i = pl.program_id(0)
        acc_refs = refs[n_in + n_out:]

        @pl.when(i == 0)
        def _():
            for r in acc_refs:
                r[...] = jnp.zeros_like(r)

        body(i * tr, refs[:n_in], refs[n_in:n_in + n_out], acc_refs)

    in_specs = []
    for a in ins:
        if a.shape[0] == lp:
            in_specs.append(pl.BlockSpec((tr, a.shape[1]), lambda i: (i, 0)))
        else:
            in_specs.append(pl.BlockSpec(a.shape, lambda i: (0, 0)))
    out_specs = [pl.BlockSpec((tr, w), lambda i: (i, 0)) for w, _ in outs]
    out_specs += [pl.BlockSpec((SUBLANES, w), lambda i: (0, 0)) for w in accs]
    out_shape = [jax.ShapeDtypeStruct((lp, w), d) for w, d in outs]
    out_shape += [jax.ShapeDtypeStruct((SUBLANES, w), F32) for w in accs]
    return pl.pallas_call(
        kernel_body, name=name, grid=(lp // tr,), in_specs=in_specs, out_specs=out_specs,
        out_shape=out_shape, compiler_params=_cparams("arbitrary"),
    )(*ins)


SHIFT_TILE = 128


def _shifted_specs(width, n_big, n_small):
    per = SHIFT_TILE // N_META
    small = pl.BlockSpec((N_META, width), lambda i: (jnp.clip(per * i - 1, 0, n_small - 1), 0))
    big = pl.BlockSpec((SHIFT_TILE, width), lambda i: (jnp.minimum(i, n_big - 1), 0))
    return small, big


def _rms_pre(meta_full, x2, g, lp, seq):
    d = x2.shape[1]
    assert seq % SHIFT_TILE == 0 and lp % SHIFT_TILE == 0 and SHIFT_TILE % N_META == 0

    def body(meta_ref, xs_ref, xb_ref, g_ref, h0_ref, u1_ref):
        i = pl.program_id(0)
        head = jnp.where(i == 0, meta_ref[...], xs_ref[...])
        rows = jnp.concatenate([head, xb_ref[:SHIFT_TILE - N_META, :]], axis=0)
        r = i * SHIFT_TILE + lax.broadcasted_iota(jnp.int32, (SHIFT_TILE, 1), 0)
        rows = jnp.where(r < N_META + seq, rows, 0.0)
        h0_ref[...] = rows
        u1_ref[...] = (rows * _rms_stats(rows) * g_ref[...]).astype(BF16)

    small, big = _shifted_specs(d, seq // SHIFT_TILE, seq // N_META)
    tile = pl.BlockSpec((SHIFT_TILE, d), lambda i: (i, 0))
    return pl.pallas_call(
        body, name="rms_pre", grid=(lp // SHIFT_TILE,),
        in_specs=[pl.BlockSpec((N_META, d), lambda i: (0, 0)), small, big, pl.BlockSpec((1, d), lambda i: (0, 0))],
        out_specs=[tile, tile],
        out_shape=[jax.ShapeDtypeStruct((lp, d), F32), jax.ShapeDtypeStruct((lp, d), BF16)],
        compiler_params=_cparams("parallel"),
    )(meta_full, x2, x2, g)


def _post_mix(o, h0, g_post_mix, g_pre_mlp, lp):
    d = h0.shape[1]

    def body(row0, ins, outs, accs):
        o_ref, h0_ref, g1_ref, g2_ref = ins
        o_v = o_ref[...]
        h1 = h0_ref[...] + o_v * _rms_stats(o_v) * g1_ref[...]
        outs[0][...] = h1
        outs[1][...] = (h1 * _rms_stats(h1) * g2_ref[...]).astype(BF16)

    return _rowwise(body, [o, h0, g_post_mix, g_pre_mlp], [(d, F32), (d, BF16)], [], lp=lp, name="post_mix")


def _loss_head(f, h1, target, g_post_mlp, lp, seq):
    d = f.shape[1]

    def body(f_ref, h1_ref, ts_ref, tb_ref, g_ref, df_ref, dh_ref, dg_ref, loss_ref):
        i = pl.program_id(0)

        @pl.when(i == 0)
        def _():
            dg_ref[...] = jnp.zeros_like(dg_ref)
            loss_ref[...] = jnp.zeros_like(loss_ref)

        f_v, g = f_ref[...], g_ref[...]
        r = _rms_stats(f_v)
        nrm = f_v * r
        rows = i * SHIFT_TILE + lax.broadcasted_iota(jnp.int32, (SHIFT_TILE, 1), 0)
        valid = (rows >= N_META) & (rows < N_META + seq)
        tgt = jnp.concatenate([ts_ref[...], tb_ref[:SHIFT_TILE - N_META, :]], axis=0)
        err = jnp.where(valid, h1_ref[...] + nrm * g - tgt, 0.0)
        loss_ref[...] += 0.5 * jnp.sum(jnp.mean(err * err, axis=-1, keepdims=True))
        dy = err * (1.0 / d)
        dn = dy * g
        df_ref[...] = (r * (dn - nrm * jnp.mean(dn * nrm, axis=-1, keepdims=True))).astype(BF16)
        dh_ref[...] = dy
        dg_ref[...] += _rowsum8(dy * nrm)

    small, big = _shifted_specs(d, seq // SHIFT_TILE, seq // N_META)
    tile = pl.BlockSpec((SHIFT_TILE, d), lambda i: (i, 0))
    return pl.pallas_call(
        body, name="loss_head", grid=(lp // SHIFT_TILE,),
        in_specs=[tile, tile, small, big, pl.BlockSpec((1, d), lambda i: (0, 0))],
        out_specs=[tile, tile, pl.BlockSpec((SUBLANES, d), lambda i: (0, 0)),
                   pl.BlockSpec((SUBLANES, LANES), lambda i: (0, 0))],
        out_shape=[jax.ShapeDtypeStruct((lp, d), BF16), jax.ShapeDtypeStruct((lp, d), F32),
                   jax.ShapeDtypeStruct((SUBLANES, d), F32), jax.ShapeDtypeStruct((SUBLANES, LANES), F32)],
        compiler_params=_cparams("arbitrary"),
    )(f, h1, target, target, g_post_mlp)


def _mid_bwd(du2, h1, dh, o, g_pre_mlp, g_post_mix, lp):
    d = h1.shape[1]

    def body(row0, ins, outs, accs):
        du2_ref, h1_ref, dh_ref, o_ref, g2_ref, g1_ref = ins
        dx2, dg2 = _rms_bwd(h1_ref[...], g2_ref[...], du2_ref[...])
        dh1 = dh_ref[...] + dx2
        do, dg1 = _rms_bwd(o_ref[...], g1_ref[...], dh1)
        outs[0][...] = dh1
        outs[1][...] = do.astype(BF16)
        accs[0][...] += _rowsum8(dg2)
        accs[1][...] += _rowsum8(dg1)

    return _rowwise(body, [du2, h1, dh, o, g_pre_mlp, g_post_mix], [(d, F32), (d, BF16)], [d, d], lp=lp,
                    name="mid_bwd")


def _pre_mix_bwd(du1, h0, dh1, g_pre_mix, seq):
    d = h0.shape[1]
    per = SHIFT_TILE // N_META
    assert seq % SHIFT_TILE == 0

    def body(du_b, h_b, dh_b, du_n, h_n, dh_n, du_m, h_m, dh_m, g_ref, gx_ref, gm_ref, dg_ref):
        i = pl.program_id(0)
        g = g_ref[...]

        @pl.when(i == 0)
        def _():
            dx, dg = _rms_bwd(h_m[...], g, du_m[...])
            gm_ref[...] = dh_m[...] + dx
            dg_ref[...] = _rowsum8(dg)

        rows = lambda big, nxt: jnp.concatenate([big[N_META:, :], nxt[...]], axis=0)
        dx, dg = _rms_bwd(rows(h_b, h_n), g, rows(du_b, du_n))
        gx_ref[...] = rows(dh_b, dh_n) + dx
        dg_ref[...] += _rowsum8(dg)

    big = pl.BlockSpec((SHIFT_TILE, d), lambda i: (i, 0))
    nxt = pl.BlockSpec((N_META, d), lambda i: (per * (i + 1), 0))
    first = pl.BlockSpec((N_META, d), lambda i: (0, 0))
    return pl.pallas_call(
        body, name="pre_mix_bwd", grid=(seq // SHIFT_TILE,),
        in_specs=[big] * 3 + [nxt] * 3 + [first] * 3 + [pl.BlockSpec((1, d), lambda i: (0, 0))],
        out_specs=[big, first, pl.BlockSpec((SUBLANES, d), lambda i: (0, 0))],
        out_shape=[jax.ShapeDtypeStruct((seq, d), F32), jax.ShapeDtypeStruct((N_META, d), F32),
                   jax.ShapeDtypeStruct((SUBLANES, d), F32)],
        compiler_params=_cparams("arbitrary"),
    )(du1, h0, dh1, du1, h0, dh1, du1, h0, dh1, g_pre_mix)


def _ln_stats(c):
    mu = jnp.mean(c, axis=-1, keepdims=True)
    var = jnp.mean(jnp.square(c - mu), axis=-1, keepdims=True)
    return mu, lax.rsqrt(var + LN_EPS)


def _ln_silu(c, ln_g, ln_b, lp):
    w = c.shape[1]

    def body(row0, ins, outs, accs):
        c_ref, g_ref, b_ref = ins
        c_v = c_ref[...]
        mu, rstd = _ln_stats(c_v)
        ln = (c_v - mu) * rstd * g_ref[...] + b_ref[...]
        outs[0][...] = (ln * jax.nn.sigmoid(ln)).astype(BF16)

    return _rowwise(body, [c, ln_g, ln_b], [(w, BF16)], [], lp=lp, name="ln_silu")[0]


def _ln_silu_bwd(c, ds, ln_g, ln_b, lp):
    w = c.shape[1]

    def body(row0, ins, outs, accs):
        c_ref, ds_ref, g_ref, b_ref = ins
        c_v, g = c_ref[...], g_ref[...]
        mu, rstd = _ln_stats(c_v)
        nrm = (c_v - mu) * rstd
        ln = nrm * g + b_ref[...]
        sig = jax.nn.sigmoid(ln)
        dln = ds_ref[...] * (sig * (1.0 + ln * (1.0 - sig)))
        dn = dln * g
        dc = rstd * (dn - jnp.mean(dn, axis=-1, keepdims=True) - nrm * jnp.mean(dn * nrm, axis=-1, keepdims=True))
        outs[0][...] = dc
        accs[0][...] += _rowsum8(dln * nrm)
        accs[1][...] += _rowsum8(dln)
        accs[2][...] += _rowsum8(dc)

    return _rowwise(body, [c, ds, ln_g, ln_b], [(w, F32)], [w, w, w], lp=lp, name="ln_silu_bwd")


def _chunk_with_history(ref, i, cols=slice(None)):
    t0 = pl.multiple_of(i * ROW_CHUNK, ROW_CHUNK)
    lo0 = pl.multiple_of(jnp.maximum(t0 - HALO, 0), SUBLANES)
    lo = jnp.where(i > 0, ref[pl.ds(lo0, HALO), cols], 0.0)
    return jnp.concatenate([lo, ref[pl.ds(t0, ROW_CHUNK), cols]], axis=0)


def _chunk_with_future(ref, i, n_chunks, cols=slice(None)):
    t0 = pl.multiple_of(i * ROW_CHUNK, ROW_CHUNK)
    hi0 = pl.multiple_of(jnp.minimum(t0 + ROW_CHUNK, (n_chunks - 1) * ROW_CHUNK), SUBLANES)
    hi = jnp.where(i < n_chunks - 1, ref[pl.ds(hi0, HALO), cols], 0.0)
    return jnp.concatenate([ref[pl.ds(t0, ROW_CHUNK), cols], hi], axis=0)


def _inv_count(t0, n_rows, window):
    pos = t0 + lax.broadcasted_iota(jnp.int32, (n_rows, 1), 0)
    return 1.0 / jnp.minimum(pos + 1, window).astype(F32)


def _pool_delta(z_hist, t0, window):
    s = z_hist
    sh = 1
    while sh < window:
        s = s + pltpu.roll(s, sh, 0)
        sh *= 2
    cur = z_hist[HALO:, :]
    return s[HALO:, :] * _inv_count(t0, ROW_CHUNK, window) - cur


def _pool_fwd(proj, wpg, pool_scale, lp):
    n_grp, gdim, _ = wpg.shape
    width = n_grp * gdim
    n_chunks = lp // ROW_CHUNK

    def body(z_ref, w_ref, sc_ref, out_ref):
        for g, window in enumerate(POOL_WINDOWS):
            cols = slice(g * gdim, (g + 1) * gdim)

            def chunk(i, carry, cols=cols, g=g, window=window):
                t0 = pl.multiple_of(i * ROW_CHUNK, ROW_CHUNK)
                d = _pool_delta(_chunk_with_history(z_ref, i, cols), t0, window)
                q = jnp.dot(d.astype(BF16), w_ref[g], preferred_element_type=F32)
                out_ref[pl.ds(t0, ROW_CHUNK), cols] = (q * sc_ref[:, cols]).astype(BF16)
                return carry

            lax.fori_loop(0, n_chunks, chunk, 0)

    return pl.pallas_call(
        body, name="pool_fwd", grid=(1,),
        in_specs=[pl.BlockSpec((lp, width), lambda i: (0, 0)),
                  pl.BlockSpec(wpg.shape, lambda i: (0, 0, 0)),
                  pl.BlockSpec(pool_scale.shape, lambda i: (0, 0))],
        out_specs=pl.BlockSpec((lp, width), lambda i: (0, 0)),
        out_shape=jax.ShapeDtypeStruct((lp, width), BF16),
        compiler_params=_cparams("arbitrary"),
    )(proj, wpg, pool_scale)


def _pool_bwd(proj, d_ya, wpg, pool_scale, dproj, lp):
    n_grp, gdim, _ = wpg.shape
    width = n_grp * gdim
    n_chunks = lp // ROW_CHUNK
    ext = ROW_CHUNK + HALO

    def body(z_ref, dya_ref, w_ref, sc_ref, _, dz_ref, dw_ref, dsc_ref):
        dw_ref[...] = jnp.zeros_like(dw_ref)
        dsc_ref[...] = jnp.zeros_like(dsc_ref)
        for g, window in enumerate(POOL_WINDOWS):
            cols = slice(g * gdim, (g + 1) * gdim)

            def chunk(i, carry, cols=cols, g=g, window=window):
                t0 = pl.multiple_of(i * ROW_CHUNK, ROW_CHUNK)
                w_g = w_ref[g]
                scale = sc_ref[:, cols]
                d = _pool_delta(_chunk_with_history(z_ref, i, cols), t0, window).astype(BF16)
                dya_ext = _chunk_with_future(dya_ref, i, n_chunks, cols)
                dya = dya_ext[:ROW_CHUNK, :]
                q = jnp.dot(d, w_g, preferred_element_type=F32)
                dsc_ref[:, cols] += _rowsum8(dya * q)
                e_ext = (dya_ext * scale).astype(BF16)
                dw_ref[g] += lax.dot_general(d, e_ext[:ROW_CHUNK, :], (((0,), (0,)), ((), ())),
                                             preferred_element_type=F32)
                dd_ext = lax.dot_general(e_ext, w_g, (((1,), (1,)), ((), ())), preferred_element_type=F32)
                s = dd_ext * _inv_count(t0, ext, window)
                sh = 1
                while sh < window:
                    s = s + pltpu.roll(s, ext - sh, 0)
                    sh *= 2
                dz_ref[pl.ds(t0, ROW_CHUNK), cols] = (s[:ROW_CHUNK, :] - dd_ext[:ROW_CHUNK, :]).astype(BF16)
                return carry

            lax.fori_loop(0, n_chunks, chunk, 0)

    blk = pl.BlockSpec((lp, width), lambda i: (0, 0))
    return pl.pallas_call(
        body, name="pool_bwd", grid=(1,),
        in_specs=[blk, blk, pl.BlockSpec(wpg.shape, lambda i: (0, 0, 0)),
                  pl.BlockSpec(pool_scale.shape, lambda i: (0, 0)), ANY_SPEC],
        out_specs=[blk, pl.BlockSpec(wpg.shape, lambda i: (0, 0, 0)),
                   pl.BlockSpec((SUBLANES, width), lambda i: (0, 0))],
        out_shape=[jax.ShapeDtypeStruct(dproj.shape, BF16), jax.ShapeDtypeStruct(wpg.shape, F32),
                   jax.ShapeDtypeStruct((SUBLANES, width), F32)],
        input_output_aliases={4: 0},
        compiler_params=_cparams("arbitrary"),
    )(proj, d_ya, wpg, pool_scale, dproj)


def _conv_fwd(proj, w_dw, b_dw, lp, width, v_col0):
    n_chunks = lp // ROW_CHUNK
    v_blk0, g_blk0 = v_col0 // LANES, (v_col0 + width) // LANES

    def body(v_ref, gc_ref, w_ref, b_ref, c_ref, a_pad):
        a_pad[pl.ds(0, HALO), :] = jnp.zeros((HALO, LANES), F32)
        a_pad[pl.ds(HALO, lp), :] = v_ref[...] * jax.nn.sigmoid(gc_ref[...])

        def chunk(i, carry):
            t0 = pl.multiple_of(i * ROW_CHUNK, ROW_CHUNK)
            hist = a_pad[pl.ds(t0, ROW_CHUNK + HALO), :]
            acc = jnp.zeros((ROW_CHUNK, LANES), F32)
            for k in range(CONV_KERNEL):
                acc = acc + w_ref[k:k + 1, :] * pltpu.roll(hist, CONV_KERNEL - 1 - k, 0)[HALO:, :]
            c_ref[pl.ds(t0, ROW_CHUNK), :] = acc + b_ref[...]
            return carry

        lax.fori_loop(0, n_chunks, chunk, 0)

    return pl.pallas_call(
        body, name="conv_fwd", grid=(width // LANES,),
        in_specs=[pl.BlockSpec((lp, LANES), lambda j: (0, j + v_blk0)),
                  pl.BlockSpec((lp, LANES), lambda j: (0, j + g_blk0)),
                  pl.BlockSpec((CONV_TAPS_PADDED, LANES), lambda j: (0, j)),
                  pl.BlockSpec((1, LANES), lambda j: (0, j))],
        out_specs=pl.BlockSpec((lp, LANES), lambda j: (0, j)),
        out_shape=jax.ShapeDtypeStruct((lp, width), F32),
        scratch_shapes=[pltpu.VMEM((lp + HALO, LANES), F32)],
        compiler_params=_cparams("parallel"),
    )(proj, proj, w_dw, b_dw)


def _conv_bwd(proj, dc, w_dw, dproj, lp, width, v_col0):
    n_chunks = lp // ROW_CHUNK
    ext = ROW_CHUNK + HALO
    v_blk0, g_blk0 = v_col0 // LANES, (v_col0 + width) // LANES

    def body(v_ref, gc_ref, dc_ref, w_ref, _, dv_ref, dgc_ref, dw_ref, a_pad, dc_pad, dw_acc):
        sig = jax.nn.sigmoid(gc_ref[...])
        a_pad[pl.ds(0, HALO), :] = jnp.zeros((HALO, LANES), F32)
        a_pad[pl.ds(HALO, lp), :] = v_ref[...] * sig
        dc_pad[pl.ds(0, lp), :] = dc_ref[...]
        dc_pad[pl.ds(lp, HALO), :] = jnp.zeros((HALO, LANES), F32)
        dw_acc[...] = jnp.zeros_like(dw_acc)

        def chunk(i, carry):
            t0 = pl.multiple_of(i * ROW_CHUNK, ROW_CHUNK)
            hist = a_pad[pl.ds(t0, ext), :]
            fut = dc_pad[pl.ds(t0, ext), :]
            dc_cur = fut[:ROW_CHUNK, :]
            da = jnp.zeros((ROW_CHUNK, LANES), F32)
            for k in range(CONV_KERNEL):
                lag = CONV_KERNEL - 1 - k
                da = da + w_ref[k:k + 1, :] * pltpu.roll(fut, (ext - lag) % ext, 0)[:ROW_CHUNK, :]
                dw_acc[pl.ds(SUBLANES * k, SUBLANES), :] += _rowsum8(dc_cur * pltpu.roll(hist, lag, 0)[HALO:, :])
            rows = pl.ds(t0, ROW_CHUNK)
            sg = jax.nn.sigmoid(gc_ref[rows, :])
            dv_ref[rows, :] = (da * sg).astype(BF16)
            dgc_ref[rows, :] = (da * v_ref[rows, :] * sg * (1.0 - sg)).astype(BF16)
            return carry

        lax.fori_loop(0, n_chunks, chunk, 0)
        dw_ref[...] = dw_acc[...].reshape(CONV_TAPS_PADDED, SUBLANES, LANES).sum(axis=1)

    col = lambda j: (0, j)
    return pl.pallas_call(
        body, name="conv_bwd", grid=(width // LANES,),
        in_specs=[pl.BlockSpec((lp, LANES), lambda j: (0, j + v_blk0)),
                  pl.BlockSpec((lp, LANES), lambda j: (0, j + g_blk0)),
                  pl.BlockSpec((lp, LANES), col),
                  pl.BlockSpec((CONV_TAPS_PADDED, LANES), col), ANY_SPEC],
        out_specs=[pl.BlockSpec((lp, LANES), lambda j: (0, j + v_blk0)), pl.BlockSpec((lp, LANES), col),
                   pl.BlockSpec((CONV_TAPS_PADDED, LANES), col)],
        out_shape=[jax.ShapeDtypeStruct(dproj.shape, BF16), jax.ShapeDtypeStruct((lp, width), BF16),
                   jax.ShapeDtypeStruct((CONV_TAPS_PADDED, width), F32)],
        scratch_shapes=[pltpu.VMEM((lp + HALO, LANES), F32), pltpu.VMEM((lp + HALO, LANES), F32),
                        pltpu.VMEM((CONV_TAPS_PADDED * SUBLANES, LANES), F32)],
        input_output_aliases={4: 0},
        compiler_params=_cparams("parallel"),
    )(proj, proj, dc, w_dw, dproj)


def _place_columns(dst, pieces, name):
    m = dst.shape[0]
    tile = 512
    counts = [p.shape[1] // tile for p, _ in pieces]
    starts = [sum(counts[:i]) for i in range(len(pieces))]
    n_steps = sum(counts)

    def local(s, i):
        return jnp.clip(s - starts[i], 0, counts[i] - 1)

    def out_index(s):
        blk = pieces[0][1] // tile + local(s, 0)
        for i in range(1, len(pieces)):
            blk = jnp.where(s >= starts[i], pieces[i][1] // tile + local(s, i), blk)
        return 0, blk

    def body(*refs):
        out_ref = refs[-1]
        s = pl.program_id(0)
        for i in range(len(pieces)):
            @pl.when((s >= starts[i]) & (s < starts[i] + counts[i]))
            def _(i=i):
                out_ref[...] = refs[i][...]

    return pl.pallas_call(
        body, name=name, grid=(n_steps,),
        in_specs=[pl.BlockSpec((m, tile), functools.partial(lambda s, i: (0, local(s, i)), i=i))
                  for i in range(len(pieces))] + [ANY_SPEC],
        out_specs=pl.BlockSpec((m, tile), out_index),
        out_shape=jax.ShapeDtypeStruct(dst.shape, dst.dtype),
        input_output_aliases={len(pieces): 0},
        compiler_params=_cparams("arbitrary"),
    )(*[p for p, _ in pieces], dst)


def _adamw_math(w, g, m, v):
    m = ADAM_B1 * m + (1.0 - ADAM_B1) * g
    v = ADAM_B2 * v + (1.0 - ADAM_B2) * jnp.square(g)
    m_hat = m / (1.0 - ADAM_B1 ** ADAM_STEP)
    v_hat = v / (1.0 - ADAM_B2 ** ADAM_STEP)
    delta = -ADAM_LR * (m_hat / (jnp.sqrt(v_hat) + ADAM_EPS) + ADAM_WD * w)
    return delta, m, v


def _pair_sum(own, recv, where, name):
    _, _, rows, cols = own.shape
    tr = _row_tile(rows, cols, 1024 * 1024)

    def body(where_ref, own_ref, recv_ref, out_ref):
        out_ref[...] = (own_ref[...].astype(F32) + recv_ref[...].astype(F32)).astype(BF16)

    return pl.pallas_call(
        body, name=name,
        grid_spec=pltpu.PrefetchScalarGridSpec(
            num_scalar_prefetch=1, grid=(3, rows // tr),
            in_specs=[pl.BlockSpec((None, None, tr, cols), lambda r, i, wh: (wh[2 + r], wh[0], i, 0)),
                      pl.BlockSpec((None, tr, cols), lambda r, i, wh: (wh[2 + r], i, 0))],
            out_specs=pl.BlockSpec((None, tr, cols), lambda r, i, wh: (wh[2 + r], i, 0))),
        out_shape=jax.ShapeDtypeStruct((4, rows, cols), BF16),
        compiler_params=_cparams("parallel", "parallel"),
    )(where, own, recv)


def _adamw_big(w, m, v, own, from_sibling, recv3, where, name):
    rows, cols = w.shape
    tr = _row_tile(rows, cols, 256 * 1024)

    def body(where_ref, w_ref, m_ref, v_ref, own_ref, sib_ref, r_ref, g_out, d_out, m_out, v_out):
        g = own_ref[...].astype(F32) + sib_ref[...].astype(F32)
        for r in range(3):
            g = g + r_ref[r].astype(F32)
        delta, m_new, v_new = _adamw_math(w_ref[...], g, m_ref[...], v_ref[...])
        g_out[...] = g
        d_out[...] = delta
        m_out[...] = m_new
        v_out[...] = v_new

    blk = pl.BlockSpec((tr, cols), lambda i, q_ref: (i, 0))
    return pl.pallas_call(
        body, name=name,
        grid_spec=pltpu.PrefetchScalarGridSpec(
            num_scalar_prefetch=1, grid=(rows // tr,),
            in_specs=[blk, blk, blk,
                      pl.BlockSpec((None, None, tr, cols), lambda i, wh: (wh[1], wh[0], i, 0)),
                      pl.BlockSpec((None, tr, cols), lambda i, wh: (wh[1], i, 0)),
                      pl.BlockSpec((3, tr, cols), lambda i, wh: (0, i, 0))],
            out_specs=[blk] * 4),
        out_shape=[jax.ShapeDtypeStruct((rows, cols), F32)] * 4,
        compiler_params=_cparams("parallel"),
    )(where, w, m, v, own, from_sibling, recv3)


def _small_update(me_idx, packed, rep_params, rep_places, meta_wmv, meta_row0, wdw_wmv, wdw_row0, loss_row0):
    n_rep = len(rep_params)
    meta_rows, meta_cols = meta_wmv[0].shape
    wdw_rows, wdw_cols = wdw_wmv[0].shape

    def body(me_ref, *refs):
        pos = 0

        def take(k):
            nonlocal pos
            out = refs[pos:pos + k]
            pos += k
            return out

        rep_in = [take(3) for _ in range(n_rep)]
        rep_g = take(n_rep)
        meta_in, (meta_g,) = take(3), take(1)
        wdw_in, (wdw_g,) = take(3), take(1)
        (loss_ref,) = take(1)
        rep_out = [take(4) for _ in range(n_rep)]
        meta_out, wdw_out = take(4), take(4)
        (loss_out,) = take(1)

        def update(wmv, g, outs):
            delta, m_new, v_new = _adamw_math(wmv[0][...], g, wmv[1][...], wmv[2][...])
            for o_ref, val in zip(outs, (g, delta, m_new, v_new)):
                o_ref[...] = val

        for wmv, g_ref, outs in zip(rep_in, rep_g, rep_out):
            g = jnp.sum(g_ref[0], axis=0, keepdims=True)
            for j in range(1, N_DEV):
                g = g + jnp.sum(g_ref[j], axis=0, keepdims=True)
            update(wmv, g, outs)
        for wmv, g_ref, outs in ((meta_in, meta_g, meta_out), (wdw_in, wdw_g, wdw_out)):
            g = g_ref[0]
            for j in range(1, N_DEV):
                g = g + g_ref[j]
            update(wmv, g, outs)
        total = loss_ref[0]
        for j in range(1, N_DEV):
            total = total + loss_ref[j]
        loss_out[...] = total

    def whole(a):
        nd = a.ndim
        return pl.BlockSpec(a.shape, lambda i, me_ref, nd=nd: (0,) * nd)

    ins, in_specs = [], []
    for wmv in rep_params:
        ins += list(wmv)
        in_specs += [whole(a) for a in wmv]
    for wmv, (row0, col0) in zip(rep_params, rep_places):
        width = wmv[0].shape[1]
        ins.append(packed)
        in_specs.append(pl.BlockSpec((N_DEV, SUBLANES, width),
                                     lambda i, me_ref, rb=row0 // SUBLANES, cb=col0 // width: (0, rb, cb)))
    ins += list(meta_wmv) + [packed]
    in_specs += [whole(a) for a in meta_wmv]
    in_specs.append(pl.BlockSpec((N_DEV, meta_rows, meta_cols),
                                 lambda i, me_ref, rb=meta_row0 // meta_rows: (0, rb, me_ref[0])))
    ins += list(wdw_wmv) + [packed]
    in_specs += [whole(a) for a in wdw_wmv]
    in_specs.append(pl.BlockSpec((N_DEV, wdw_rows, wdw_cols),
                                 lambda i, me_ref, rb=wdw_row0 // wdw_rows: (0, rb, me_ref[0])))
    ins.append(packed)
    in_specs.append(pl.BlockSpec((N_DEV, SUBLANES, LANES), lambda i, me_ref, rb=loss_row0 // SUBLANES: (0, rb, 0)))

    out_shape, out_specs = [], []
    for wmv in list(rep_params) + [meta_wmv, wdw_wmv]:
        out_shape += [jax.ShapeDtypeStruct(wmv[0].shape, F32)] * 4
        out_specs += [whole(wmv[0])] * 4
    out_shape.append(jax.ShapeDtypeStruct((SUBLANES, LANES), F32))
    out_specs.append(pl.BlockSpec((SUBLANES, LANES), lambda i, me_ref: (0, 0)))

    outs = pl.pallas_call(
        body, name="small_update",
        grid_spec=pltpu.PrefetchScalarGridSpec(num_scalar_prefetch=1, grid=(1,), in_specs=in_specs,
                                               out_specs=out_specs),
        out_shape=out_shape, compiler_params=_cparams("arbitrary"),
    )(me_idx, *ins)
    groups = [outs[4 * i:4 * i + 4] for i in range(n_rep + 2)]
    return groups[:n_rep], groups[n_rep], groups[n_rep + 1], outs[-1]


def kernel(x, meta, g_pre_mix, w_in, w_pool_grp, pool_scale, w_pool_out, w_dw, b_dw, conv_ln_g, conv_ln_b, w_conv_out, w_o, g_post_mix, g_pre_mlp, w_up, w_down, g_post_mlp, loss_target, m_meta, m_g_pre_mix, m_w_in, m_w_pool_grp, m_pool_scale, m_w_pool_out, m_w_dw, m_b_dw, m_conv_ln_g, m_conv_ln_b, m_w_conv_out, m_w_o, m_g_post_mix, m_g_pre_mlp, m_w_up, m_w_down, m_g_post_mlp, v_meta, v_g_pre_mix, v_w_in, v_w_pool_grp, v_pool_scale, v_w_pool_out, v_w_dw, v_b_dw, v_conv_ln_g, v_conv_ln_b, v_w_conv_out, v_w_o, v_g_post_mix, v_g_pre_mlp, v_w_up, v_w_down, v_g_post_mlp):
    seq, d = x.shape[1], x.shape[2]
    pool_w = pool_scale.shape[1]
    conv_w = b_dw.shape[1]
    n_grp, grp_rows, gdim = w_pool_grp.shape[1:]
    lp = _round_up(N_META + seq, ROW_CHUNK)
    tm_half = lp // 2 if (lp // 2) % 16 == 0 else lp
    c_idx = lax.axis_index("c").astype(jnp.int32)
    chip_idx = (2 * lax.axis_index("x") + lax.axis_index("y")).astype(jnp.int32)
    me_idx = 2 * chip_idx + c_idx

    pad_taps = ((0, CONV_TAPS_PADDED - CONV_KERNEL), (0, 0))
    big = dict(w_in=w_in[0], w_pool_grp=w_pool_grp[0].reshape(n_grp * grp_rows, gdim), w_pool_out=w_pool_out[0],
               w_conv_out=w_conv_out[0], w_o=w_o[0], w_up=w_up[0], w_down=w_down[0])
    big_names = list(big)
    moments = dict(w_in=(m_w_in, v_w_in), w_pool_grp=(m_w_pool_grp, v_w_pool_grp), w_pool_out=(m_w_pool_out, v_w_pool_out),
                   w_conv_out=(m_w_conv_out, v_w_conv_out), w_o=(m_w_o, v_w_o), w_up=(m_w_up, v_w_up),
                   w_down=(m_w_down, v_w_down))
    slot_idx = me_idx.reshape(1)
    sources = dict(big, meta=meta, w_dw=jnp.pad(w_dw[0], pad_taps))

    def fill(k, after):
        return _fill_slot(sources[k], slot_idx, BF16 if k in big else F32, after, "fill_" + k)

    gather_groups = [["meta", "w_dw"], ["w_in"], ["w_pool_grp", "w_pool_out", "w_conv_out", "w_o"], ["w_up"], ["w_down"]]
    started, token = _gather_start([[fill(k, slot_idx) for k in names] for names in gather_groups[:2]], slot_idx,
                                   "gather_start_first", issue_order=(0, 3, 1, 2))
    started_rest, _ = _gather_start([[fill(k, token) for k in names] for names in gather_groups[2:]], token,
                                    "gather_start_rest")
    started += started_rest
    wg = {}
    x_idx, y_idx = lax.axis_index("x"), lax.axis_index("y")
    at = lambda px, py, pc: 4 * px + 2 * py + pc
    near_slots = jnp.stack([at(x_idx, y_idx, c_idx), at(x_idx, y_idx, 1 - c_idx), at(1 - x_idx, y_idx, c_idx),
                            at(x_idx, 1 - y_idx, c_idx), at(1 - x_idx, y_idx, 1 - c_idx),
                            at(x_idx, 1 - y_idx, 1 - c_idx)]).astype(jnp.int32)
    far_slots = jnp.stack([at(1 - x_idx, 1 - y_idx, c_idx), at(1 - x_idx, 1 - y_idx, 1 - c_idx)]).astype(jnp.int32)

    def gather_whole(gi, after_forward, after_finish):
        send, recv, lands = started[gi]
        fs, fr, lands = _gather_forward(lands, recv, after_forward(), f"gather_forward_{gi}")
        lands = _gather_finish(lands, send, recv, [(ALL_CHIPS, fs, fr)], [(ALL_CHIPS, fs, fr)], after_finish(),
                               f"gather_finish_{gi}")
        wg.update(zip(gather_groups[gi], lands))

    near_state = {}

    def gather_near(gi, after):
        send, recv, lands = started[gi]
        fs, fr, lands = _gather_forward(lands, recv, after, f"gather_forward_near_{gi}", which=NEAR)
        lands = _gather_finish(lands, send, recv, [(NEAR, fs, fr)], [], after, f"gather_finish_near_{gi}",
                               direct_sends=False)
        near_state[gi] = (fs, fr)
        return lands

    def gather_far(gi, lands, after):
        send, recv, _ = started[gi]
        fs, fr, lands = _gather_forward(lands, recv, after, f"gather_forward_far_{gi}", which=FAR)
        lands = _gather_finish(lands, send, recv, [(FAR, fs, fr)], [(NEAR,) + near_state[gi], (FAR, fs, fr)], after,
                               f"gather_finish_far_{gi}", own=False)
        wg.update(zip(gather_groups[gi], lands))
        return lands

    gather_whole(0, lambda: token, lambda: token)
    meta_full = wg["meta"].transpose(1, 0, 2).reshape(N_META, d)
    wdw_full = wg["w_dw"].transpose(1, 0, 2).reshape(CONV_TAPS_PADDED, conv_w)
    target = loss_target[0]
    h0, u1 = _rms_pre(meta_full, x[0], g_pre_mix, lp, seq)
    send, recv, w_in_lands = started[1]
    w_in_lands = _gather_finish(w_in_lands, send, recv, [], [], u1, "gather_finish_home_1", direct_sends=False)
    proj = _mm_slots(u1, w_in_lands[0], near_slots[:2], over="n", tm=tm_half, name="mm_proj_home")
    fs_far, fr_far, w_in_lands = _gather_forward(w_in_lands, recv, proj, "gather_forward_far_1", which=FAR)
    w_in_lands = _gather_finish(w_in_lands, send, recv, [(FAR, fs_far, fr_far)], [], proj, "gather_finish_far_1",
                                own=False, direct_sends=False)
    proj = _mm_slots(u1, w_in_lands[0], far_slots, over="n", tm=tm_half, base=proj, name="mm_proj_far")
    fs_near, fr_near, w_in_lands = _gather_forward(w_in_lands, recv, proj, "gather_forward_near_1", which=NEAR)
    w_in_lands = _gather_finish(w_in_lands, send, recv, [(NEAR, fs_near, fr_near)],
                                [(FAR, fs_far, fr_far), (NEAR, fs_near, fr_near)], proj, "gather_finish_near_1",
                                own=False)
    proj = _mm_slots(u1, w_in_lands[0], near_slots[2:], over="n", tm=tm_half, base=proj, name="mm_proj_near")
    wg["w_in"] = w_in_lands[0]
    conv_c = _conv_fwd(proj, wdw_full, b_dw, lp, conv_w, pool_w)
    s_act = _ln_silu(conv_c, conv_ln_g, conv_ln_b, lp)
    gather_whole(2, lambda: proj, lambda: s_act)
    wpg_full = wg["w_pool_grp"].reshape(N_DEV, n_grp, grp_rows, gdim).transpose(1, 0, 2, 3).reshape(n_grp, gdim, gdim)
    w_o_full = wg["w_o"].reshape(d, d)
    ya_pre = _pool_fwd(proj, wpg_full, pool_scale, lp)
    m_mix, y_a, y_b = _gate_mix(ya_pre, s_act, wg["w_pool_out"], wg["w_conv_out"], proj, d, "gate_mix")
    o = _mm(m_mix, w_o_full, mode="nn", tm=tm_half, tn=512, name="mm_o")
    h1, u2 = _post_mix(o, h0, g_post_mix, g_pre_mlp, lp)
    relu2 = lambda acc: jnp.square(jnp.maximum(acc, 0.0))
    (w_up_near,) = gather_near(3, u2)
    act = _mm_slots(u2, w_up_near, near_slots, over="n", tm=tm_half, out_dtype=BF16, epilogue=relu2, name="mm_up_near")
    (w_up_all,) = gather_far(3, [w_up_near], act)
    act = _mm_slots(u2, w_up_all, far_slots, over="n", tm=tm_half, out_dtype=BF16, epilogue=relu2, base=act,
                    name="mm_up_far")
    (w_down_near,) = gather_near(4, act)
    f = _mm_slots(act, w_down_near, near_slots, over="k", tm=tm_half, tn=d, name="mm_down_near")
    (w_down_all,) = gather_far(4, [w_down_near], f)
    f = _mm_slots(act, w_down_all, far_slots, over="k", tm=tm_half // 2, tn=d, base=f, name="mm_down_far")
    w_down_full = w_down_all.reshape(-1, d)

    big_out = {}

    def to_sibling(names, grads, after, tag):
        send, recv, grads, lands, token = _pair_exchange_start(grads, after, "grads_to_sibling_start_" + tag)
        return (names, send, recv, grads, lands, tag), token

    where = jnp.stack([c_idx, chip_idx, 2 * (1 - x_idx) + y_idx, 2 * x_idx + (1 - y_idx),
                       2 * (1 - x_idx) + (1 - y_idx)]).astype(jnp.int32)

    def to_owner(handle, after):
        names, send, recv, grads, from_sib, tag = handle
        grads, from_sib = _pair_exchange_finish(grads, from_sib, send, recv, after, "grads_to_sibling_finish_" + tag)
        own = [g.reshape((4, 2) + g.shape[1:]) for g in grads]
        sums = [_pair_sum(o, r, where, "pair_sum_" + k) for k, o, r in zip(names, own, from_sib)]
        send, recv, sums, lands, token = _chip_exchange_start(sums, after, "grads_to_owner_start_" + tag)
        return (names, send, recv, sums, lands, own, from_sib, tag), token

    def update(handle, after):
        names, send, recv, sums, lands, own, from_sib, tag = handle
        got = _chip_exchange_finish(sums, lands, send, recv, after, "grads_to_owner_finish_" + tag)
        for k, o, s, r3 in zip(names, own, from_sib, got):
            w2 = big[k]
            shape = moments[k][0].shape
            outs = _adamw_big(w2, moments[k][0].reshape(w2.shape), moments[k][1].reshape(w2.shape), o, s, r3,
                              where, "adamw_" + k)
            big_out[k] = [a.reshape(shape) for a in outs]
        return big_out[names[-1]][0]

    df, dh, dg_post_mlp, loss_part = _loss_head(f, h1, target, g_post_mlp, lp, seq)
    d_up = _mm(df, w_down_full, mode="nt", tm=tm_half, tn=1024, out_dtypes=(BF16,), extras=[(act, 0)],
               epilogue=lambda acc, a: (acc * (2.0 * jnp.sqrt(a.astype(F32))),), row_splits=2, name="mm_d_up")
    g_w_down = _mm(act, df, mode="tn", tm=1024, tn=1024, out_dtypes=(BF16,), name="mm_g_down")
    sib_down, token = to_sibling(["w_down"], [g_w_down.reshape(N_DEV, -1, d)], slot_idx, "down")
    g_w_up = _mm(u2, d_up, mode="tn", tm=1024, tn=wg["w_up"].shape[2], out_blocked=True, out_dtypes=(BF16,),
                 after=token, name="mm_g_up")
    sib_up, token = to_sibling(["w_up"], [g_w_up], slot_idx, "up")
    pending_down, token = to_owner(sib_down, token)
    du2 = _mm(d_up, wg["w_up"], mode="nt", tm=tm_half, tn=1024, b_blocked=True, kb=2, after=token, name="mm_du2")
    pending_up, token = to_owner(sib_up, du2)
    dh1, do, dg_pre_mlp, dg_post_mix = _mid_bwd(du2, h1, dh, o, g_pre_mlp, g_post_mix, lp)

    def gate_bwd(dm, ga, gb, ya, yb):
        sa, sb = jax.nn.sigmoid(ga), jax.nn.sigmoid(gb)
        return (dm * ya.astype(F32) * sa * (1.0 - sa), dm * yb.astype(F32) * sb * (1.0 - sb), dm * sa, dm * sb)

    gate_tn = 512
    ga_col0, gb_col0 = proj.shape[1] - 2 * d, proj.shape[1] - d
    dproj, d_gb, d_ya, d_yb = _mm(
        do, w_o_full, mode="nt", tm=tm_half, tn=gate_tn, out_dtypes=(BF16,) * 4,
        extras=[(proj, ga_col0 // gate_tn), (proj, gb_col0 // gate_tn), (y_a, 0), (y_b, 0)], epilogue=gate_bwd,
        out_places=((proj.shape[1], ga_col0), None, None, None), after=token, row_splits=2, name="mm_dm")
    g_w_o = _mm(m_mix, do, mode="tn", tm=1024, tn=1024, out_dtypes=(BF16,), name="mm_g_o")
    bn_out = wg["w_pool_out"].shape[2]
    g_w_pool_out = _mm(ya_pre, d_ya, mode="tn", tm=pool_w, tn=bn_out, out_blocked=True, out_dtypes=(BF16,),
                       name="mm_g_pool_out")
    g_w_conv_out = _mm(s_act, d_yb, mode="tn", tm=conv_w, tn=bn_out, out_blocked=True, out_dtypes=(BF16,),
                       name="mm_g_conv_out")
    sib_mix, token = to_sibling(["w_o", "w_pool_out", "w_conv_out"],
                                [g_w_o.reshape(N_DEV, -1, d), g_w_pool_out, g_w_conv_out], slot_idx, "mix")
    d_ya_pre = _mm(d_ya, wg["w_pool_out"], mode="nt", tm=tm_half, tn=pool_w, b_blocked=True, kb=4, after=token,
                   name="mm_d_ya_pre")
    d_s = _mm(d_yb, wg["w_conv_out"], mode="nt", tm=tm_half, tn=conv_w, b_blocked=True, kb=4, name="mm_d_s")
    pending_mix, token = to_owner(sib_mix, d_s)
    dproj, g_wpg, d_scale = _pool_bwd(proj, d_ya_pre, wpg_full, pool_scale, dproj, lp)
    dc, d_ln_g, d_ln_b, d_b_dw = _ln_silu_bwd(conv_c, d_s, conv_ln_g, conv_ln_b, lp)
    dproj, dgc, g_wdw = _conv_bwd(proj, dc, wdw_full, dproj, lp, conv_w, pool_w)
    dproj = _place_columns(dproj, [(dgc, pool_w + conv_w), (d_gb, gb_col0)], "place_dproj")
    g_w_in = _mm(u1, dproj, mode="tn", tm=1024, tn=wg["w_in"].shape[2], out_blocked=True, out_dtypes=(BF16,),
                 after=token, name="mm_g_in")
    g_wpg_slots = g_wpg.astype(BF16).reshape(n_grp, N_DEV, grp_rows, gdim).transpose(1, 0, 2, 3)
    sib_in, token = to_sibling(["w_pool_grp", "w_in"],
                               [g_wpg_slots.reshape(N_DEV, n_grp * grp_rows, gdim), g_w_in], slot_idx, "in")
    done = update(pending_down, token)
    pending_in, token = to_owner(sib_in, done)
    done = update(pending_up, token)
    du1 = _mm(dproj, wg["w_in"], mode="nt", tm=tm_half, tn=1024, b_blocked=True, kb=2, after=done, name="mm_du1")
    grad_x2, grad_meta_part, dg_pre_mix = _pre_mix_bwd(du1, h0, dh1, g_pre_mix, seq)
    grad_x = grad_x2[None]

    assert pool_w + conv_w == d and conv_w <= d and LANES <= d
    widen = lambda a: jnp.pad(a, ((0, 0), (0, d - a.shape[1])))
    packed = jnp.concatenate([
        dg_pre_mix, dg_post_mix, dg_pre_mlp, dg_post_mlp,
        jnp.concatenate([d_scale, d_ln_g], axis=1), jnp.concatenate([d_ln_b, d_b_dw], axis=1),
        grad_meta_part, widen(g_wdw), widen(loss_part)], axis=0)
    rep = dict(g_pre_mix=((g_pre_mix, m_g_pre_mix, v_g_pre_mix), (0, 0)),
               g_post_mix=((g_post_mix, m_g_post_mix, v_g_post_mix), (SUBLANES, 0)),
               g_pre_mlp=((g_pre_mlp, m_g_pre_mlp, v_g_pre_mlp), (2 * SUBLANES, 0)),
               g_post_mlp=((g_post_mlp, m_g_post_mlp, v_g_post_mlp), (3 * SUBLANES, 0)),
               pool_scale=((pool_scale, m_pool_scale, v_pool_scale), (4 * SUBLANES, 0)),
               conv_ln_g=((conv_ln_g, m_conv_ln_g, v_conv_ln_g), (4 * SUBLANES, pool_w)),
               conv_ln_b=((conv_ln_b, m_conv_ln_b, v_conv_ln_b), (5 * SUBLANES, 0)),
               b_dw=((b_dw, m_b_dw, v_b_dw), (5 * SUBLANES, conv_w)))
    meta_row0 = 6 * SUBLANES
    wdw_row0 = meta_row0 + N_META
    loss_row0 = wdw_row0 + CONV_TAPS_PADDED
    (small_started,), token = _gather_start([[_fill_slot(packed, slot_idx, F32, slot_idx, "fill_small")]], grad_x2,
                                            "gather_small_start")
    done = update(pending_mix, token)
    done = update(pending_in, done)
    send, recv, lands = small_started
    fsend, frecv, lands = _gather_forward(lands, recv, done, "gather_small_forward")
    (packed_all,) = _gather_finish(lands, send, recv, [(ALL_CHIPS, fsend, frecv)], [(ALL_CHIPS, fsend, frecv)], done,
                                   "gather_small_finish")
    rep_names = list(rep)
    wdw_wmv = [jnp.pad(a[0], pad_taps) for a in (w_dw, m_w_dw, v_w_dw)]
    rep_out, meta_out, wdw_out, loss_blk = _small_update(
        slot_idx, packed_all, [rep[k][0] for k in rep_names], [rep[k][1] for k in rep_names],
        (meta, m_meta, v_meta), meta_row0, wdw_wmv, wdw_row0, loss_row0)
    small_out = dict(zip(rep_names, rep_out))
    small_out["meta"] = meta_out
    small_out["w_dw"] = [a[:CONV_KERNEL][None] for a in wdw_out]

    order = ["meta", "g_pre_mix", "w_in", "w_pool_grp", "pool_scale", "w_pool_out", "w_dw", "b_dw", "conv_ln_g",
             "conv_ln_b", "w_conv_out", "w_o", "g_post_mix", "g_pre_mlp", "w_up", "w_down", "g_post_mlp"]
    by_name = {**big_out, **small_out}
    result = [loss_blk[0, 0], grad_x]
    for kind in range(4):
        result += [by_name[k][kind] for k in order]
    return tuple(result)
```

```python
import functools

import jax
import jax.numpy as jnp
from jax import lax
from jax.experimental import pallas as pl
from jax.experimental.pallas import tpu as pltpu

F32 = jnp.float32
BF16 = jnp.bfloat16
MESH = pl.DeviceIdType.MESH

N_DEV = 8
N_META = 16
POOL_WINDOWS = (2, 4, 8, 16)
CONV_KERNEL = 31
CONV_TAPS_PADDED = 32
RMS_EPS = 1e-6
LN_EPS = 1e-5
ADAM_LR = 0.001
ADAM_B1 = 0.9
ADAM_B2 = 0.999
ADAM_EPS = 1e-08
ADAM_WD = 0.01
ADAM_STEP = 10

LANES = 128
SUBLANES = 8
ROW_CHUNK = 128
HALO = 32
VMEM_LIMIT_BYTES = 56 * 1024 * 1024


def _cparams(*sem):
    return pltpu.CompilerParams(dimension_semantics=sem if sem else None, vmem_limit_bytes=VMEM_LIMIT_BYTES)


def _round_up(n, m):
    return (n + m - 1) // m * m


def _row_tile(rows, cols, max_elems=640 * 1024):
    best = None
    for t in range(16, rows + 1, 16):
        if rows % t == 0 and (best is None or t * cols <= max_elems):
            best = t
    assert best is not None, (rows, cols)
    return best


def _rowsum8(a):
    t, w = a.shape
    return a.reshape(t // SUBLANES, SUBLANES, w).sum(axis=0)


def _mesh_pos():
    return lax.axis_index("x"), lax.axis_index("y"), lax.axis_index("c")


HBM_SPEC = pl.BlockSpec(memory_space=pltpu.HBM)
SEM_SPEC = pl.BlockSpec(memory_space=pltpu.SEMAPHORE)
ANY_SPEC = pl.BlockSpec(memory_space=pl.ANY)
_DATAFLOW = pltpu.SideEffectType.DATAFLOW_SIDE_EFFECTING


def _hbm(a):
    return pltpu.with_memory_space_constraint(a, pltpu.HBM)


def _slot(p):
    return 4 * p[0] + 2 * p[1] + p[2]


def _fill_slot(w, slot_idx, dtype, after, name):
    rows, cols = w.shape
    tr = _row_tile(rows, cols) if rows % 16 == 0 else rows

    def body(idx_ref, w_ref, _, out_ref):
        out_ref[...] = w_ref[...].astype(dtype)

    return pl.pallas_call(
        body, name=name,
        grid_spec=pltpu.PrefetchScalarGridSpec(
            num_scalar_prefetch=1, grid=(rows // tr,),
            in_specs=[pl.BlockSpec((tr, cols), lambda i, idx_ref: (i, 0)), ANY_SPEC],
            out_specs=pl.BlockSpec((None, tr, cols), lambda i, idx_ref: (idx_ref[0], i, 0))),
        out_shape=jax.ShapeDtypeStruct((N_DEV, rows, cols), dtype),
        compiler_params=_cparams("parallel"),
    )(slot_idx, w, after)


def _gather_start(groups, after, name, issue_order=(0, 1, 2, 3)):
    flat = [g for grp in groups for g in grp]
    n, n_grp = len(flat), len(groups)

    def body(*refs):
        lands = refs[:n]
        sems = refs[n + 1:n + 1 + 2 * n_grp]
        token = refs[-1]
        x, y, c = _mesh_pos()
        targets = [(x, y, 1 - c), (1 - x, y, c), (x, 1 - y, c), (1 - x, 1 - y, c)]
        t = 0
        for gi, grp in enumerate(groups):
            for ti in range(len(grp)):
                mine = lands[t].at[_slot((x, y, c))]
                for k in issue_order:
                    pltpu.make_async_remote_copy(
                        src_ref=mine, dst_ref=mine,
                        send_sem=sems[2 * gi].at[4 * ti + k], recv_sem=sems[2 * gi + 1].at[4 * ti + k],
                        device_id=targets[k], device_id_type=MESH).start()
                t += 1
        token[...] = jnp.zeros_like(token)

    sem_shapes = []
    for grp in groups:
        sem_shapes += [pltpu.SemaphoreType.DMA((4 * len(grp),))] * 2
    outs = pl.pallas_call(
        body, name=name,
        out_shape=tuple(sem_shapes + [pltpu.HBM(g.shape, g.dtype) for g in flat]
                        + [jax.ShapeDtypeStruct((SUBLANES, LANES), F32)]),
        in_specs=tuple([HBM_SPEC] * n + [ANY_SPEC]),
        out_specs=tuple([SEM_SPEC] * (2 * n_grp) + [HBM_SPEC] * n + [pl.BlockSpec(memory_space=pltpu.VMEM)]),
        input_output_aliases={i: 2 * n_grp + i for i in range(n)},
        compiler_params=pltpu.CompilerParams(has_side_effects=_DATAFLOW),
    )(*[_hbm(g) for g in flat], after)
    sems, lands, token = outs[:2 * n_grp], outs[2 * n_grp:-1], outs[-1]
    res, t = [], 0
    for gi, grp in enumerate(groups):
        res.append((sems[2 * gi], sems[2 * gi + 1], list(lands[t:t + len(grp)])))
        t += len(grp)
    return res, token


NEAR, FAR, ALL_CHIPS = (0, 1), (2,), (0, 1, 2)


def _gather_forward(lands, recv_sems, after, name, which=ALL_CHIPS):
    n, nw = len(lands), len(which)

    def body(*refs):
        land_refs, recv, _ = refs[:n], refs[n], refs[n + 1]
        fsend, frecv = refs[n + 2], refs[n + 3]
        x, y, c = _mesh_pos()
        chips = [(1 - x, y), (x, 1 - y), (1 - x, 1 - y)]
        for t in range(n):
            for i, j in enumerate(which):
                blk = land_refs[t].at[_slot((*chips[j], c))]
                pltpu.make_async_remote_copy(src_ref=blk, dst_ref=blk, send_sem=fsend.at[nw * t + i],
                                             recv_sem=recv.at[4 * t + 1 + j],
                                             device_id=(x, y, 1 - c), device_id_type=MESH).wait_recv()
                pltpu.make_async_remote_copy(src_ref=blk, dst_ref=blk, send_sem=fsend.at[nw * t + i],
                                             recv_sem=frecv.at[nw * t + i],
                                             device_id=(x, y, 1 - c), device_id_type=MESH).start()

    outs = pl.pallas_call(
        body, name=name,
        out_shape=tuple([pltpu.SemaphoreType.DMA((nw * n,))] * 2 + [pltpu.HBM(g.shape, g.dtype) for g in lands]),
        in_specs=tuple([HBM_SPEC] * n + [SEM_SPEC, ANY_SPEC]),
        out_specs=tuple([SEM_SPEC] * 2 + [HBM_SPEC] * n),
        input_output_aliases={i: 2 + i for i in range(n)},
        compiler_params=pltpu.CompilerParams(has_side_effects=_DATAFLOW),
    )(*lands, recv_sems, after)
    return outs[0], outs[1], list(outs[2:])


def _gather_finish(lands, send_sems, recv_sems, arrivals, sends, after, name, own=True, direct_sends=True):
    n = len(lands)
    fwd = list(arrivals) + list(sends)
    sem_args = [send_sems, recv_sems]
    where = []
    for _, fs, fr in fwd:
        pos = []
        for arr in (fs, fr):
            hit = [i for i, have in enumerate(sem_args) if have is arr]
            if not hit:
                sem_args.append(arr)
                hit = [len(sem_args) - 1]
            pos.append(hit[0])
        where.append(pos)

    def body(*refs):
        land_refs = refs[:n]
        send, recv = refs[n], refs[n + 1]
        fwd_refs = [refs[n + p] for pos in where for p in pos]
        x, y, c = _mesh_pos()
        sibling = (x, y, 1 - c)
        chips = [(1 - x, y), (x, 1 - y), (1 - x, 1 - y)]

        def desc(ref, s_sem, r_sem):
            return pltpu.make_async_remote_copy(src_ref=ref, dst_ref=ref, send_sem=s_sem, recv_sem=r_sem,
                                                device_id=sibling, device_id_type=MESH)

        for t in range(n):
            mine = land_refs[t].at[_slot((x, y, c))]
            if own:
                desc(land_refs[t].at[_slot(sibling)], send.at[4 * t], recv.at[4 * t]).wait_recv()
            for a, (which, _, _) in enumerate(fwd):
                fs, fr = fwd_refs[2 * a], fwd_refs[2 * a + 1]
                for i, j in enumerate(which):
                    if a < len(arrivals):
                        desc(land_refs[t].at[_slot((*chips[j], 1 - c))], fs.at[len(which) * t + i],
                             fr.at[len(which) * t + i]).wait_recv()
                    else:
                        desc(land_refs[t].at[_slot((*chips[j], c))], fs.at[len(which) * t + i],
                             fr.at[len(which) * t + i]).wait_send()
            if direct_sends:
                for k in range(4):
                    desc(mine, send.at[4 * t + k], recv.at[4 * t + k]).wait_send()

    outs = pl.pallas_call(
        body, name=name,
        out_shape=tuple(pltpu.HBM(g.shape, g.dtype) for g in lands),
        in_specs=tuple([HBM_SPEC] * n + [SEM_SPEC] * len(sem_args) + [ANY_SPEC]),
        out_specs=tuple([HBM_SPEC] * n),
        input_output_aliases={i: i for i in range(n)},
        compiler_params=pltpu.CompilerParams(has_side_effects=_DATAFLOW),
    )(*lands, *sem_args, after)
    return list(outs)


def _pair_exchange_start(grads, after, name):
    n = len(grads)

    def body(*refs):
        ins, lands = refs[:n], refs[n:2 * n]
        send, recv = refs[2 * n + 1], refs[2 * n + 2]
        token = refs[-1]
        x, y, c = _mesh_pos()
        for t in range(n):
            for q in range(4):
                pltpu.make_async_remote_copy(
                    src_ref=ins[t].at[2 * q + 1 - c], dst_ref=lands[t].at[q],
                    send_sem=send.at[4 * t + q], recv_sem=recv.at[4 * t + q],
                    device_id=(x, y, 1 - c), device_id_type=MESH).start()
        token[...] = jnp.zeros_like(token)

    land_shapes = [(4,) + g.shape[1:] for g in grads]
    outs = pl.pallas_call(
        body, name=name,
        out_shape=tuple([pltpu.SemaphoreType.DMA((4 * n,))] * 2 + [pltpu.HBM(g.shape, g.dtype) for g in grads]
                        + [pltpu.HBM(ls, g.dtype) for ls, g in zip(land_shapes, grads)]
                        + [jax.ShapeDtypeStruct((SUBLANES, LANES), F32)]),
        in_specs=tuple([HBM_SPEC] * (2 * n) + [ANY_SPEC]),
        out_specs=tuple([SEM_SPEC] * 2 + [HBM_SPEC] * (2 * n) + [pl.BlockSpec(memory_space=pltpu.VMEM)]),
        input_output_aliases={i: 2 + i for i in range(2 * n)},
        compiler_params=pltpu.CompilerParams(has_side_effects=_DATAFLOW),
    )(*[_hbm(g) for g in grads], *[_hbm(lax.empty(ls, g.dtype)) for ls, g in zip(land_shapes, grads)], after)
    return outs[0], outs[1], list(outs[2:2 + n]), list(outs[2 + n:2 + 2 * n]), outs[-1]


def _pair_exchange_finish(grads, lands, send_sems, recv_sems, after, name):
    n = len(grads)

    def body(*refs):
        ins, land_refs = refs[:n], refs[n:2 * n]
        send, recv = refs[2 * n], refs[2 * n + 1]
        x, y, c = _mesh_pos()
        for t in range(n):
            for q in range(4):
                cp = pltpu.make_async_remote_copy(
                    src_ref=ins[t].at[q], dst_ref=land_refs[t].at[q], send_sem=send.at[4 * t + q],
                    recv_sem=recv.at[4 * t + q], device_id=(x, y, 1 - c), device_id_type=MESH)
                cp.wait_send()
                cp.wait_recv()

    outs = pl.pallas_call(
        body, name=name,
        out_shape=tuple([pltpu.HBM(g.shape, g.dtype) for g in grads] + [pltpu.HBM(g.shape, g.dtype) for g in lands]),
        in_specs=tuple([HBM_SPEC] * (2 * n) + [SEM_SPEC] * 2 + [ANY_SPEC]),
        out_specs=tuple([HBM_SPEC] * (2 * n)),
        input_output_aliases={i: i for i in range(2 * n)},
        compiler_params=pltpu.CompilerParams(has_side_effects=_DATAFLOW),
    )(*grads, *lands, send_sems, recv_sems, after)
    return list(outs[:n]), list(outs[n:])


def _chip_exchange_start(sums, after, name):
    n = len(sums)

    def body(*refs):
        ins, lands = refs[:n], refs[n:2 * n]
        send, recv = refs[2 * n + 1], refs[2 * n + 2]
        token = refs[-1]
        x, y, c = _mesh_pos()
        chips = [(1 - x, y), (x, 1 - y), (1 - x, 1 - y)]
        for t in range(n):
            for r, chip in enumerate(chips):
                pltpu.make_async_remote_copy(
                    src_ref=ins[t].at[2 * chip[0] + chip[1]], dst_ref=lands[t].at[r],
                    send_sem=send.at[3 * t + r], recv_sem=recv.at[3 * t + r],
                    device_id=(*chip, c), device_id_type=MESH).start()
        token[...] = jnp.zeros_like(token)

    land_shapes = [(3,) + s.shape[1:] for s in sums]
    outs = pl.pallas_call(
        body, name=name,
        out_shape=tuple([pltpu.SemaphoreType.DMA((3 * n,))] * 2 + [pltpu.HBM(s.shape, s.dtype) for s in sums]
                        + [pltpu.HBM(ls, s.dtype) for ls, s in zip(land_shapes, sums)]
                        + [jax.ShapeDtypeStruct((SUBLANES, LANES), F32)]),
        in_specs=tuple([HBM_SPEC] * (2 * n) + [ANY_SPEC]),
        out_specs=tuple([SEM_SPEC] * 2 + [HBM_SPEC] * (2 * n) + [pl.BlockSpec(memory_space=pltpu.VMEM)]),
        input_output_aliases={i: 2 + i for i in range(2 * n)},
        compiler_params=pltpu.CompilerParams(has_side_effects=_DATAFLOW),
    )(*[_hbm(s) for s in sums], *[_hbm(lax.empty(ls, s.dtype)) for ls, s in zip(land_shapes, sums)], after)
    return outs[0], outs[1], list(outs[2:2 + n]), list(outs[2 + n:2 + 2 * n]), outs[-1]


def _chip_exchange_finish(sums, lands, send_sems, recv_sems, after, name):
    n = len(sums)

    def body(*refs):
        ins, land_refs = refs[:n], refs[n:2 * n]
        send, recv = refs[2 * n], refs[2 * n + 1]
        x, y, c = _mesh_pos()
        for t in range(n):
            for r in range(3):
                cp = pltpu.make_async_remote_copy(
                    src_ref=ins[t].at[r], dst_ref=land_refs[t].at[r], send_sem=send.at[3 * t + r],
                    recv_sem=recv.at[3 * t + r],
                    device_id=(x, y, 1 - c), device_id_type=MESH)
                cp.wait_send()
                cp.wait_recv()

    outs = pl.pallas_call(
        body, name=name,
        out_shape=tuple(pltpu.HBM(g.shape, g.dtype) for g in lands),
        in_specs=tuple([HBM_SPEC] * (2 * n) + [SEM_SPEC] * 2 + [ANY_SPEC]),
        out_specs=tuple([HBM_SPEC] * n),
        input_output_aliases={n + i: i for i in range(n)},
        compiler_params=pltpu.CompilerParams(has_side_effects=_DATAFLOW),
    )(*sums, *lands, send_sems, recv_sems, after)
    return list(outs)


def _mm(a, b, *, mode, tm, tn, tk=None, b_blocked=False, out_blocked=False, out_dtypes=(F32,),
        epilogue=None, extras=(), after=None, kb=1, out_places=None, row_splits=1, out_block=None, name):
    if mode == "nn":
        m, k = a.shape
        n = b.shape[0] * b.shape[2] if b_blocked else b.shape[1]
        dims = (((1,), (0,)), ((), ()))
    elif mode == "nt":
        m, k = a.shape
        n = b.shape[1] if b_blocked else b.shape[0]
        if b_blocked:
            tk = kb * b.shape[2]
        dims = (((1,), (1,)), ((), ()))
    else:
        k, m = a.shape
        n = b.shape[1]
        dims = (((0,), (0,)), ((), ()))
    tk = k if tk is None else tk
    assert m % tm == 0 and n % tn == 0 and k % tk == 0, (name, m, n, k, tm, tn, tk)
    gm, gn, gk = m // tm, n // tn, k // tk
    if b_blocked:
        assert (tn if mode == "nn" else tk) == kb * b.shape[2], name
    if row_splits > 1:
        assert gk == 1 and epilogue is not None and mode != "tn" and not b_blocked and tm % (16 * row_splits) == 0, name

    if mode == "nn":
        a_spec = pl.BlockSpec((tm, tk), lambda i, j, kk: (i, kk))
        b_spec = (pl.BlockSpec((None, tk, tn), lambda i, j, kk: (j, kk, 0)) if b_blocked
                  else pl.BlockSpec((tk, tn), lambda i, j, kk: (kk, j)))
    elif mode == "nt":
        a_spec = pl.BlockSpec((tm, tk), lambda i, j, kk: (i, kk))
        b_spec = (pl.BlockSpec((kb, tn, tk // kb), lambda i, j, kk: (kk, j, 0)) if b_blocked
                  else pl.BlockSpec((tn, tk), lambda i, j, kk: (j, kk)))
    else:
        a_spec = pl.BlockSpec((tk, tm), lambda i, j, kk: (kk, i))
        b_spec = pl.BlockSpec((tk, tn), lambda i, j, kk: (kk, j))
    out_pack = 1
    if out_blocked and out_block is not None and out_block != tn:
        assert tn % out_block == 0, name
        out_pack = tn // out_block
        out_spec = pl.BlockSpec((out_pack, tm, out_block), lambda i, j, kk: (j, i, 0))
        out_shape = (n // out_block, m, out_block)
    elif out_blocked:
        out_spec = pl.BlockSpec((None, tm, tn), lambda i, j, kk: (j, i, 0))
        out_shape = (gn, m, tn)
    else:
        out_spec = pl.BlockSpec((tm, tn), lambda i, j, kk: (i, j))
        out_shape = (m, n)
    extra_specs = [pl.BlockSpec((tm, tn), functools.partial(lambda i, j, kk, off: (i, j + off), off=off))
                   for _, off in extras]
    n_extra, n_out = len(extras), len(out_dtypes)
    n_after = 0 if after is None else 1
    places = out_places if out_places is not None else (None,) * n_out

    def body(a_ref, b_ref, *rest):
        extra_refs = rest[:n_extra]
        out_refs = rest[n_extra + n_after:n_extra + n_after + n_out]

        def finish(acc):
            if epilogue is None:
                res = (acc,)
            else:
                res = epilogue(acc, *[e[...] for e in extra_refs])
            for o_ref, r in zip(out_refs, res):
                if out_pack == 1:
                    o_ref[...] = r.astype(o_ref.dtype)
                else:
                    for h in range(out_pack):
                        o_ref[h] = r[:, h * out_block:(h + 1) * out_block].astype(o_ref.dtype)

        if row_splits > 1:
            strip = tm // row_splits
            for h in range(row_splits):
                rows = slice(h * strip, (h + 1) * strip)
                acc = lax.dot_general(a_ref[rows, :], b_ref[...], dims, preferred_element_type=F32)
                res = epilogue(acc, *[e[rows, :] for e in extra_refs])
                for o_ref, r in zip(out_refs, res):
                    o_ref[rows, :] = r.astype(o_ref.dtype)
            return
        if mode == "nt" and b_blocked:
            bk = tk // kb
            part = lax.dot_general(a_ref[:, :bk], b_ref[0], dims, preferred_element_type=F32)
            for h in range(1, kb):
                part = part + lax.dot_general(a_ref[:, h * bk:(h + 1) * bk], b_ref[h], dims,
                                              preferred_element_type=F32)
        else:
            part = lax.dot_general(a_ref[...], b_ref[...], dims, preferred_element_type=F32)
        if gk == 1:
            finish(part)
        else:
            acc_ref = rest[-1]
            kk = pl.program_id(2)

            @pl.when(kk == 0)
            def _():
                acc_ref[...] = part

            @pl.when(kk > 0)
            def _():
                acc_ref[...] += part

            @pl.when(kk == gk - 1)
            def _():
                finish(acc_ref[...])

    outs = pl.pallas_call(
        body, name=name, grid=(gm, gn, gk),
        in_specs=[a_spec, b_spec] + extra_specs + [ANY_SPEC] * n_after,
        out_specs=[out_spec if place is None else
                   pl.BlockSpec((tm, tn), functools.partial(lambda i, j, kk, off: (i, j + off), off=place[1] // tn))
                   for place in places],
        out_shape=[jax.ShapeDtypeStruct(out_shape if place is None else (m, place[0]), dt)
                   for dt, place in zip(out_dtypes, places)],
        scratch_shapes=[pltpu.VMEM((tm, tn), F32)] if gk > 1 else [],
        compiler_params=_cparams("parallel", "parallel", "arbitrary"),
    )(a, b, *[e for e, _ in extras], *([] if after is None else [after]))
    return outs[0] if n_out == 1 else outs


def _mm_slots(a, w, slots, *, over, tm, tn=None, out_dtype=F32, epilogue=None, base=None, name):
    m = a.shape[0]
    ns = slots.shape[0]
    n_slots, w1, w2 = w.shape
    assert m % tm == 0
    if over == "n":
        k, bn = w1, w2

        def body(slots_ref, a_ref, w_ref, *rest):
            out_ref = rest[-1]
            acc = jnp.dot(a_ref[...], w_ref[...], preferred_element_type=F32)
            out_ref[...] = (acc if epilogue is None else epilogue(acc)).astype(out_ref.dtype)

        in_specs = [pl.BlockSpec((tm, k), lambda i, j, s: (i, 0)),
                    pl.BlockSpec((None, k, bn), lambda i, j, s: (s[j], 0, 0))]
        args = [a, w]
        aliases = {}
        if base is not None:
            in_specs.append(ANY_SPEC)
            args.append(base)
            aliases = {3: 0}
        return pl.pallas_call(
            body, name=name,
            grid_spec=pltpu.PrefetchScalarGridSpec(
                num_scalar_prefetch=1, grid=(m // tm, ns), in_specs=in_specs,
                out_specs=pl.BlockSpec((tm, bn), lambda i, j, s: (i, s[j]))),
            out_shape=jax.ShapeDtypeStruct((m, n_slots * bn), out_dtype),
            input_output_aliases=aliases,
            compiler_params=_cparams("parallel", "arbitrary"),
        )(slots, *args)

    bk, n = w1, w2
    tn = n if tn is None else tn
    assert n % tn == 0

    def body(slots_ref, a_ref, w_ref, *rest):
        out_ref, acc_ref = rest[-2], rest[-1]
        kk = pl.program_id(2)
        part = jnp.dot(a_ref[...], w_ref[...], preferred_element_type=F32)

        @pl.when(kk == 0)
        def _():
            acc_ref[...] = part if base is None else part + rest[0][...]

        @pl.when(kk > 0)
        def _():
            acc_ref[...] += part

        @pl.when(kk == ns - 1)
        def _():
            out_ref[...] = acc_ref[...].astype(out_ref.dtype)

    in_specs = [pl.BlockSpec((tm, bk), lambda i, j, kk, s: (i, s[kk])),
                pl.BlockSpec((None, bk, tn), lambda i, j, kk, s: (s[kk], 0, j))]
    args = [a, w]
    if base is not None:
        in_specs.append(pl.BlockSpec((tm, tn), lambda i, j, kk, s: (i, j)))
        args.append(base)
    return pl.pallas_call(
        body, name=name,
        grid_spec=pltpu.PrefetchScalarGridSpec(
            num_scalar_prefetch=1, grid=(m // tm, n // tn, ns), in_specs=in_specs,
            out_specs=pl.BlockSpec((tm, tn), lambda i, j, kk, s: (i, j)),
            scratch_shapes=[pltpu.VMEM((tm, tn), F32)]),
        out_shape=jax.ShapeDtypeStruct((m, n), out_dtype),
        compiler_params=_cparams("parallel", "parallel", "arbitrary"),
    )(slots, *args)


def _gate_mix(ya_pre, s, wpo, wco, proj, d_model, name):
    lp, width = ya_pre.shape
    nb, _, bn = wpo.shape
    ga_off = (proj.shape[1] - 2 * d_model) // bn
    gb_off = (proj.shape[1] - d_model) // bn

    n_strips = 4 if lp % 64 == 0 else 1

    def body(ya_ref, s_ref, wpo_ref, wco_ref, ga_ref, gb_ref, m_ref, y_a_ref, y_b_ref):
        strip = lp // n_strips
        for h in range(n_strips):
            rows = slice(h * strip, (h + 1) * strip)
            y_a = jnp.dot(ya_ref[rows, :], wpo_ref[...], preferred_element_type=F32)
            y_b = jnp.dot(s_ref[rows, :], wco_ref[...], preferred_element_type=F32)
            m = jax.nn.sigmoid(ga_ref[rows, :]) * y_a + jax.nn.sigmoid(gb_ref[rows, :]) * y_b
            m_ref[rows, :] = m.astype(BF16)
            y_a_ref[rows, :] = y_a.astype(BF16)
            y_b_ref[rows, :] = y_b.astype(BF16)

    act_spec = pl.BlockSpec((lp, width), lambda j: (0, 0))
    w_spec = pl.BlockSpec((None, width, bn), lambda j: (j, 0, 0))
    out_spec = pl.BlockSpec((lp, bn), lambda j: (0, j))
    return pl.pallas_call(
        body, name=name, grid=(nb,),
        in_specs=[act_spec, act_spec, w_spec, w_spec,
                  pl.BlockSpec((lp, bn), lambda j: (0, j + ga_off)),
                  pl.BlockSpec((lp, bn), lambda j: (0, j + gb_off))],
        out_specs=[out_spec] * 3,
        out_shape=[jax.ShapeDtypeStruct((lp, nb * bn), BF16)] * 3,
        compiler_params=_cparams("parallel"),
    )(ya_pre, s, wpo, wco, proj, proj)


def _rms_stats(x):
    return lax.rsqrt(jnp.mean(x * x, axis=-1, keepdims=True) + RMS_EPS)


def _rms_bwd(x, g, dy):
    r = _rms_stats(x)
    nrm = x * r
    dn = dy * g
    dx = r * (dn - nrm * jnp.mean(dn * nrm, axis=-1, keepdims=True))
    return dx, dy * nrm


def _rowwise(body, ins, outs, accs, *, lp, name):
    tr = _row_tile(lp, max(a.shape[1] for a in ins))
    n_in, n_out, n_acc = len(ins), len(outs), len(accs)

    def kernel_body(*refs):
        i = pl.program_id(0)
        acc_refs = refs[n_in + n_out:]

        @pl.when(i == 0)
        def _():
            for r in acc_refs:
                r[...] = jnp.zeros_like(r)

        body(i * tr, refs[:n_in], refs[n_in:n_in + n_out], acc_refs)

    in_specs = []
    for a in ins:
        if a.shape[0] == lp:
            in_specs.append(pl.BlockSpec((tr, a.shape[1]), lambda i: (i, 0)))
        else:
            in_specs.append(pl.BlockSpec(a.shape, lambda i: (0, 0)))
    out_specs = [pl.BlockSpec((tr, w), lambda i: (i, 0)) for w, _ in outs]
    out_specs += [pl.BlockSpec((SUBLANES, w), lambda i: (0, 0)) for w in accs]
    out_shape = [jax.ShapeDtypeStruct((lp, w), d) for w, d in outs]
    out_shape += [jax.ShapeDtypeStruct((SUBLANES, w), F32) for w in accs]
    return pl.pallas_call(
        kernel_body, name=name, grid=(lp // tr,), in_specs=in_specs, out_specs=out_specs,
        out_shape=out_shape, compiler_params=_cparams("arbitrary"),
    )(*ins)


SHIFT_TILE = 128


def _shifted_specs(width, n_big, n_small):
    per = SHIFT_TILE // N_META
    small = pl.BlockSpec((N_META, width), lambda i: (jnp.clip(per * i - 1, 0, n_small - 1), 0))
    big = pl.BlockSpec((SHIFT_TILE, width), lambda i: (jnp.minimum(i, n_big - 1), 0))
    return small, big


def _rms_pre(meta_full, x2, g, lp, seq):
    d = x2.shape[1]
    assert seq % SHIFT_TILE == 0 and lp % SHIFT_TILE == 0 and SHIFT_TILE % N_META == 0

    def body(meta_ref, xs_ref, xb_ref, g_ref, h0_ref, u1_ref):
        i = pl.program_id(0)
        head = jnp.where(i == 0, meta_ref[...], xs_ref[...])
        rows = jnp.concatenate([head, xb_ref[:SHIFT_TILE - N_META, :]], axis=0)
        r = i * SHIFT_TILE + lax.broadcasted_iota(jnp.int32, (SHIFT_TILE, 1), 0)
        rows = jnp.where(r < N_META + seq, rows, 0.0)
        h0_ref[...] = rows
        u1_ref[...] = (rows * _rms_stats(rows) * g_ref[...]).astype(BF16)

    small, big = _shifted_specs(d, seq // SHIFT_TILE, seq // N_META)
    tile = pl.BlockSpec((SHIFT_TILE, d), lambda i: (i, 0))
    return pl.pallas_call(
        body, name="rms_pre", grid=(lp // SHIFT_TILE,),
        in_specs=[pl.BlockSpec((N_META, d), lambda i: (0, 0)), small, big, pl.BlockSpec((1, d), lambda i: (0, 0))],
        out_specs=[tile, tile],
        out_shape=[jax.ShapeDtypeStruct((lp, d), F32), jax.ShapeDtypeStruct((lp, d), BF16)],
        compiler_params=_cparams("parallel"),
    )(meta_full, x2, x2, g)


def _post_mix(o, h0, g_post_mix, g_pre_mlp, lp):
    d = h0.shape[1]

    def body(row0, ins, outs, accs):
        o_ref, h0_ref, g1_ref, g2_ref = ins
        o_v = o_ref[...]
        h1 = h0_ref[...] + o_v * _rms_stats(o_v) * g1_ref[...]
        outs[0][...] = h1
        outs[1][...] = (h1 * _rms_stats(h1) * g2_ref[...]).astype(BF16)

    return _rowwise(body, [o, h0, g_post_mix, g_pre_mlp], [(d, F32), (d, BF16)], [], lp=lp, name="post_mix")


def _loss_head(f, h1, target, g_post_mlp, lp, seq):
    d = f.shape[1]

    def body(f_ref, h1_ref, ts_ref, tb_ref, g_ref, df_ref, dh_ref, dg_ref, loss_ref):
        i = pl.program_id(0)

        @pl.when(i == 0)
        def _():
            dg_ref[...] = jnp.zeros_like(dg_ref)
            loss_ref[...] = jnp.zeros_like(loss_ref)

        f_v, g = f_ref[...], g_ref[...]
        r = _rms_stats(f_v)
        nrm = f_v * r
        rows = i * SHIFT_TILE + lax.broadcasted_iota(jnp.int32, (SHIFT_TILE, 1), 0)
        valid = (rows >= N_META) & (rows < N_META + seq)
        tgt = jnp.concatenate([ts_ref[...], tb_ref[:SHIFT_TILE - N_META, :]], axis=0)
        err = jnp.where(valid, h1_ref[...] + nrm * g - tgt, 0.0)
        loss_ref[...] += 0.5 * jnp.sum(jnp.mean(err * err, axis=-1, keepdims=True))
        dy = err * (1.0 / d)
        dn = dy * g
        df_ref[...] = (r * (dn - nrm * jnp.mean(dn * nrm, axis=-1, keepdims=True))).astype(BF16)
        dh_ref[...] = dy
        dg_ref[...] += _rowsum8(dy * nrm)

    small, big = _shifted_specs(d, seq // SHIFT_TILE, seq // N_META)
    tile = pl.BlockSpec((SHIFT_TILE, d), lambda i: (i, 0))
    return pl.pallas_call(
        body, name="loss_head", grid=(lp // SHIFT_TILE,),
        in_specs=[tile, tile, small, big, pl.BlockSpec((1, d), lambda i: (0, 0))],
        out_specs=[tile, tile, pl.BlockSpec((SUBLANES, d), lambda i: (0, 0)),
                   pl.BlockSpec((SUBLANES, LANES), lambda i: (0, 0))],
        out_shape=[jax.ShapeDtypeStruct((lp, d), BF16), jax.ShapeDtypeStruct((lp, d), F32),
                   jax.ShapeDtypeStruct((SUBLANES, d), F32), jax.ShapeDtypeStruct((SUBLANES, LANES), F32)],
        compiler_params=_cparams("arbitrary"),
    )(f, h1, target, target, g_post_mlp)


def _mid_bwd(du2, h1, dh, o, g_pre_mlp, g_post_mix, lp):
    d = h1.shape[1]

    def body(row0, ins, outs, accs):
        du2_ref, h1_ref, dh_ref, o_ref, g2_ref, g1_ref = ins
        dx2, dg2 = _rms_bwd(h1_ref[...], g2_ref[...], du2_ref[...])
        dh1 = dh_ref[...] + dx2
        do, dg1 = _rms_bwd(o_ref[...], g1_ref[...], dh1)
        outs[0][...] = dh1
        outs[1][...] = do.astype(BF16)
        accs[0][...] += _rowsum8(dg2)
        accs[1][...] += _rowsum8(dg1)

    return _rowwise(body, [du2, h1, dh, o, g_pre_mlp, g_post_mix], [(d, F32), (d, BF16)], [d, d], lp=lp,
                    name="mid_bwd")


def _pre_mix_bwd(du1, h0, dh1, g_pre_mix, seq):
    d = h0.shape[1]
    per = SHIFT_TILE // N_META
    assert seq % SHIFT_TILE == 0

    def body(du_b, h_b, dh_b, du_n, h_n, dh_n, du_m, h_m, dh_m, g_ref, gx_ref, gm_ref, dg_ref):
        i = pl.program_id(0)
        g = g_ref[...]

        @pl.when(i == 0)
        def _():
            dx, dg = _rms_bwd(h_m[...], g, du_m[...])
            gm_ref[...] = dh_m[...] + dx
            dg_ref[...] = _rowsum8(dg)

        rows = lambda big, nxt: jnp.concatenate([big[N_META:, :], nxt[...]], axis=0)
        dx, dg = _rms_bwd(rows(h_b, h_n), g, rows(du_b, du_n))
        gx_ref[...] = rows(dh_b, dh_n) + dx
        dg_ref[...] += _rowsum8(dg)

    big = pl.BlockSpec((SHIFT_TILE, d), lambda i: (i, 0))
    nxt = pl.BlockSpec((N_META, d), lambda i: (per * (i + 1), 0))
    first = pl.BlockSpec((N_META, d), lambda i: (0, 0))
    return pl.pallas_call(
        body, name="pre_mix_bwd", grid=(seq // SHIFT_TILE,),
        in_specs=[big] * 3 + [nxt] * 3 + [first] * 3 + [pl.BlockSpec((1, d), lambda i: (0, 0))],
        out_specs=[big, first, pl.BlockSpec((SUBLANES, d), lambda i: (0, 0))],
        out_shape=[jax.ShapeDtypeStruct((seq, d), F32), jax.ShapeDtypeStruct((N_META, d), F32),
                   jax.ShapeDtypeStruct((SUBLANES, d), F32)],
        compiler_params=_cparams("arbitrary"),
    )(du1, h0, dh1, du1, h0, dh1, du1, h0, dh1, g_pre_mix)


def _ln_stats(c):
    mu = jnp.mean(c, axis=-1, keepdims=True)
    var = jnp.mean(jnp.square(c - mu), axis=-1, keepdims=True)
    return mu, lax.rsqrt(var + LN_EPS)


def _ln_silu(c, ln_g, ln_b, lp):
    w = c.shape[1]

    def body(row0, ins, outs, accs):
        c_ref, g_ref, b_ref = ins
        c_v = c_ref[...]
        mu, rstd = _ln_stats(c_v)
        ln = (c_v - mu) * rstd * g_ref[...] + b_ref[...]
        outs[0][...] = (ln * jax.nn.sigmoid(ln)).astype(BF16)

    return _rowwise(body, [c, ln_g, ln_b], [(w, BF16)], [], lp=lp, name="ln_silu")[0]


def _ln_silu_bwd(c, ds, ln_g, ln_b, lp):
    w = c.shape[1]

    def body(row0, ins, outs, accs):
        c_ref, ds_ref, g_ref, b_ref = ins
        c_v, g = c_ref[...], g_ref[...]
        mu, rstd = _ln_stats(c_v)
        nrm = (c_v - mu) * rstd
        ln = nrm * g + b_ref[...]
        sig = jax.nn.sigmoid(ln)
        dln = ds_ref[...] * (sig * (1.0 + ln * (1.0 - sig)))
        dn = dln * g
        dc = rstd * (dn - jnp.mean(dn, axis=-1, keepdims=True) - nrm * jnp.mean(dn * nrm, axis=-1, keepdims=True))
        outs[0][...] = dc
        accs[0][...] += _rowsum8(dln * nrm)
        accs[1][...] += _rowsum8(dln)
        accs[2][...] += _rowsum8(dc)

    return _rowwise(body, [c, ds, ln_g, ln_b], [(w, F32)], [w, w, w], lp=lp, name="ln_silu_bwd")


def _chunk_with_history(ref, i, cols=slice(None)):
    t0 = pl.multiple_of(i * ROW_CHUNK, ROW_CHUNK)
    lo0 = pl.multiple_of(jnp.maximum(t0 - HALO, 0), SUBLANES)
    lo = jnp.where(i > 0, ref[pl.ds(lo0, HALO), cols], 0.0)
    return jnp.concatenate([lo, ref[pl.ds(t0, ROW_CHUNK), cols]], axis=0)


def _chunk_with_future(ref, i, n_chunks, cols=slice(None)):
    t0 = pl.multiple_of(i * ROW_CHUNK, ROW_CHUNK)
    hi0 = pl.multiple_of(jnp.minimum(t0 + ROW_CHUNK, (n_chunks - 1) * ROW_CHUNK), SUBLANES)
    hi = jnp.where(i < n_chunks - 1, ref[pl.ds(hi0, HALO), cols], 0.0)
    return jnp.concatenate([ref[pl.ds(t0, ROW_CHUNK), cols], hi], axis=0)


def _inv_count(t0, n_rows, window):
    pos = t0 + lax.broadcasted_iota(jnp.int32, (n_rows, 1), 0)
    return 1.0 / jnp.minimum(pos + 1, window).astype(F32)


def _pool_delta(z_hist, t0, window):
    s = z_hist
    sh = 1
    while sh < window:
        s = s + pltpu.roll(s, sh, 0)
        sh *= 2
    cur = z_hist[HALO:, :]
    return s[HALO:, :] * _inv_count(t0, ROW_CHUNK, window) - cur


def _pool_fwd(proj, wpg, pool_scale, lp):
    n_grp, gdim, _ = wpg.shape
    width = n_grp * gdim
    n_chunks = lp // ROW_CHUNK

    def body(z_ref, w_ref, sc_ref, out_ref):
        for g, window in enumerate(POOL_WINDOWS):
            cols = slice(g * gdim, (g + 1) * gdim)

            def chunk(i, carry, cols=cols, g=g, window=window):
                t0 = pl.multiple_of(i * ROW_CHUNK, ROW_CHUNK)
                d = _pool_delta(_chunk_with_history(z_ref, i, cols), t0, window)
                q = jnp.dot(d.astype(BF16), w_ref[g], preferred_element_type=F32)
                out_ref[pl.ds(t0, ROW_CHUNK), cols] = (q * sc_ref[:, cols]).astype(BF16)
                return carry

            lax.fori_loop(0, n_chunks, chunk, 0)

    return pl.pallas_call(
        body, name="pool_fwd", grid=(1,),
        in_specs=[pl.BlockSpec((lp, width), lambda i: (0, 0)),
                  pl.BlockSpec(wpg.shape, lambda i: (0, 0, 0)),
                  pl.BlockSpec(pool_scale.shape, lambda i: (0, 0))],
        out_specs=pl.BlockSpec((lp, width), lambda i: (0, 0)),
        out_shape=jax.ShapeDtypeStruct((lp, width), BF16),
        compiler_params=_cparams("arbitrary"),
    )(proj, wpg, pool_scale)


def _pool_bwd(proj, d_ya, wpg, pool_scale, dproj, lp):
    n_grp, gdim, _ = wpg.shape
    width = n_grp * gdim
    n_chunks = lp // ROW_CHUNK
    ext = ROW_CHUNK + HALO

    def body(z_ref, dya_ref, w_ref, sc_ref, _, dz_ref, dw_ref, dsc_ref):
        dw_ref[...] = jnp.zeros_like(dw_ref)
        dsc_ref[...] = jnp.zeros_like(dsc_ref)
        for g, window in enumerate(POOL_WINDOWS):
            cols = slice(g * gdim, (g + 1) * gdim)

            def chunk(i, carry, cols=cols, g=g, window=window):
                t0 = pl.multiple_of(i * ROW_CHUNK, ROW_CHUNK)
                w_g = w_ref[g]
                scale = sc_ref[:, cols]
                d = _pool_delta(_chunk_with_history(z_ref, i, cols), t0, window).astype(BF16)
                dya_ext = _chunk_with_future(dya_ref, i, n_chunks, cols)
                dya = dya_ext[:ROW_CHUNK, :]
                q = jnp.dot(d, w_g, preferred_element_type=F32)
                dsc_ref[:, cols] += _rowsum8(dya * q)
                e_ext = (dya_ext * scale).astype(BF16)
                dw_ref[g] += lax.dot_general(d, e_ext[:ROW_CHUNK, :], (((0,), (0,)), ((), ())),
                                             preferred_element_type=F32)
                dd_ext = lax.dot_general(e_ext, w_g, (((1,), (1,)), ((), ())), preferred_element_type=F32)
                s = dd_ext * _inv_count(t0, ext, window)
                sh = 1
                while sh < window:
                    s = s + pltpu.roll(s, ext - sh, 0)
                    sh *= 2
                dz_ref[pl.ds(t0, ROW_CHUNK), cols] = (s[:ROW_CHUNK, :] - dd_ext[:ROW_CHUNK, :]).astype(BF16)
                return carry

            lax.fori_loop(0, n_chunks, chunk, 0)

    blk = pl.BlockSpec((lp, width), lambda i: (0, 0))
    return pl.pallas_call(
        body, name="pool_bwd", grid=(1,),
        in_specs=[blk, blk, pl.BlockSpec(wpg.shape, lambda i: (0, 0, 0)),
                  pl.BlockSpec(pool_scale.shape, lambda i: (0, 0)), ANY_SPEC],
        out_specs=[blk, pl.BlockSpec(wpg.shape, lambda i: (0, 0, 0)),
                   pl.BlockSpec((SUBLANES, width), lambda i: (0, 0))],
        out_shape=[jax.ShapeDtypeStruct(dproj.shape, BF16), jax.ShapeDtypeStruct(wpg.shape, F32),
                   jax.ShapeDtypeStruct((SUBLANES, width), F32)],
        input_output_aliases={4: 0},
        compiler_params=_cparams("arbitrary"),
    )(proj, d_ya, wpg, pool_scale, dproj)


def _conv_fwd(proj, w_dw, b_dw, lp, width, v_col0):
    n_chunks = lp // ROW_CHUNK
    v_blk0, g_blk0 = v_col0 // LANES, (v_col0 + width) // LANES

    def body(v_ref, gc_ref, w_ref, b_ref, c_ref, a_pad):
        a_pad[pl.ds(0, HALO), :] = jnp.zeros((HALO, LANES), F32)
        a_pad[pl.ds(HALO, lp), :] = v_ref[...] * jax.nn.sigmoid(gc_ref[...])

        def chunk(i, carry):
            t0 = pl.multiple_of(i * ROW_CHUNK, ROW_CHUNK)
            hist = a_pad[pl.ds(t0, ROW_CHUNK + HALO), :]
            acc = jnp.zeros((ROW_CHUNK, LANES), F32)
            for k in range(CONV_KERNEL):
                acc = acc + w_ref[k:k + 1, :] * pltpu.roll(hist, CONV_KERNEL - 1 - k, 0)[HALO:, :]
            c_ref[pl.ds(t0, ROW_CHUNK), :] = acc + b_ref[...]
            return carry

        lax.fori_loop(0, n_chunks, chunk, 0)

    return pl.pallas_call(
        body, name="conv_fwd", grid=(width // LANES,),
        in_specs=[pl.BlockSpec((lp, LANES), lambda j: (0, j + v_blk0)),
                  pl.BlockSpec((lp, LANES), lambda j: (0, j + g_blk0)),
                  pl.BlockSpec((CONV_TAPS_PADDED, LANES), lambda j: (0, j)),
                  pl.BlockSpec((1, LANES), lambda j: (0, j))],
        out_specs=pl.BlockSpec((lp, LANES), lambda j: (0, j)),
        out_shape=jax.ShapeDtypeStruct((lp, width), F32),
        scratch_shapes=[pltpu.VMEM((lp + HALO, LANES), F32)],
        compiler_params=_cparams("parallel"),
    )(proj, proj, w_dw, b_dw)


def _conv_bwd(proj, dc, w_dw, dproj, lp, width, v_col0):
    n_chunks = lp // ROW_CHUNK
    ext = ROW_CHUNK + HALO
    v_blk0, g_blk0 = v_col0 // LANES, (v_col0 + width) // LANES

    def body(v_ref, gc_ref, dc_ref, w_ref, _, dv_ref, dgc_ref, dw_ref, a_pad, dc_pad, dw_acc):
        sig = jax.nn.sigmoid(gc_ref[...])
        a_pad[pl.ds(0, HALO), :] = jnp.zeros((HALO, LANES), F32)
        a_pad[pl.ds(HALO, lp), :] = v_ref[...] * sig
        dc_pad[pl.ds(0, lp), :] = dc_ref[...]
        dc_pad[pl.ds(lp, HALO), :] = jnp.zeros((HALO, LANES), F32)
        dw_acc[...] = jnp.zeros_like(dw_acc)

        def chunk(i, carry):
            t0 = pl.multiple_of(i * ROW_CHUNK, ROW_CHUNK)
            hist = a_pad[pl.ds(t0, ext), :]
            fut = dc_pad[pl.ds(t0, ext), :]
            dc_cur = fut[:ROW_CHUNK, :]
            da = jnp.zeros((ROW_CHUNK, LANES), F32)
            for k in range(CONV_KERNEL):
                lag = CONV_KERNEL - 1 - k
                da = da + w_ref[k:k + 1, :] * pltpu.roll(fut, (ext - lag) % ext, 0)[:ROW_CHUNK, :]
                dw_acc[pl.ds(SUBLANES * k, SUBLANES), :] += _rowsum8(dc_cur * pltpu.roll(hist, lag, 0)[HALO:, :])
            rows = pl.ds(t0, ROW_CHUNK)
            sg = jax.nn.sigmoid(gc_ref[rows, :])
            dv_ref[rows, :] = (da * sg).astype(BF16)
            dgc_ref[rows, :] = (da * v_ref[rows, :] * sg * (1.0 - sg)).astype(BF16)
            return carry

        lax.fori_loop(0, n_chunks, chunk, 0)
        dw_ref[...] = dw_acc[...].reshape(CONV_TAPS_PADDED, SUBLANES, LANES).sum(axis=1)

    col = lambda j: (0, j)
    return pl.pallas_call(
        body, name="conv_bwd", grid=(width // LANES,),
        in_specs=[pl.BlockSpec((lp, LANES), lambda j: (0, j + v_blk0)),
                  pl.BlockSpec((lp, LANES), lambda j: (0, j + g_blk0)),
                  pl.BlockSpec((lp, LANES), col),
                  pl.BlockSpec((CONV_TAPS_PADDED, LANES), col), ANY_SPEC],
        out_specs=[pl.BlockSpec((lp, LANES), lambda j: (0, j + v_blk0)), pl.BlockSpec((lp, LANES), col),
                   pl.BlockSpec((CONV_TAPS_PADDED, LANES), col)],
        out_shape=[jax.ShapeDtypeStruct(dproj.shape, BF16), jax.ShapeDtypeStruct((lp, width), BF16),
                   jax.ShapeDtypeStruct((CONV_TAPS_PADDED, width), F32)],
        scratch_shapes=[pltpu.VMEM((lp + HALO, LANES), F32), pltpu.VMEM((lp + HALO, LANES), F32),
                        pltpu.VMEM((CONV_TAPS_PADDED * SUBLANES, LANES), F32)],
        input_output_aliases={4: 0},
        compiler_params=_cparams("parallel"),
    )(proj, proj, dc, w_dw, dproj)


def _place_columns(dst, pieces, name):
    m = dst.shape[0]
    tile = 512
    counts = [p.shape[1] // tile for p, _ in pieces]
    starts = [sum(counts[:i]) for i in range(len(pieces))]
    n_steps = sum(counts)

    def local(s, i):
        return jnp.clip(s - starts[i], 0, counts[i] - 1)

    def out_index(s):
        blk = pieces[0][1] // tile + local(s, 0)
        for i in range(1, len(pieces)):
            blk = jnp.where(s >= starts[i], pieces[i][1] // tile + local(s, i), blk)
        return 0, blk

    def body(*refs):
        out_ref = refs[-1]
        s = pl.program_id(0)
        for i in range(len(pieces)):
            @pl.when((s >= starts[i]) & (s < starts[i] + counts[i]))
            def _(i=i):
                out_ref[...] = refs[i][...]

    return pl.pallas_call(
        body, name=name, grid=(n_steps,),
        in_specs=[pl.BlockSpec((m, tile), functools.partial(lambda s, i: (0, local(s, i)), i=i))
                  for i in range(len(pieces))] + [ANY_SPEC],
        out_specs=pl.BlockSpec((m, tile), out_index),
        out_shape=jax.ShapeDtypeStruct(dst.shape, dst.dtype),
        input_output_aliases={len(pieces): 0},
        compiler_params=_cparams("arbitrary"),
    )(*[p for p, _ in pieces], dst)


def _adamw_math(w, g, m, v):
    m = ADAM_B1 * m + (1.0 - ADAM_B1) * g
    v = ADAM_B2 * v + (1.0 - ADAM_B2) * jnp.square(g)
    m_hat = m / (1.0 - ADAM_B1 ** ADAM_STEP)
    v_hat = v / (1.0 - ADAM_B2 ** ADAM_STEP)
    delta = -ADAM_LR * (m_hat / (jnp.sqrt(v_hat) + ADAM_EPS) + ADAM_WD * w)
    return delta, m, v


def _pair_sum(own, recv, where, name):
    _, _, rows, cols = own.shape
    tr = _row_tile(rows, cols, 1024 * 1024)

    def body(where_ref, own_ref, recv_ref, out_ref):
        out_ref[...] = (own_ref[...].astype(F32) + recv_ref[...].astype(F32)).astype(BF16)

    return pl.pallas_call(
        body, name=name,
        grid_spec=pltpu.PrefetchScalarGridSpec(
            num_scalar_prefetch=1, grid=(3, rows // tr),
            in_specs=[pl.BlockSpec((None, None, tr, cols), lambda r, i, wh: (wh[2 + r], wh[0], i, 0)),
                      pl.BlockSpec((None, tr, cols), lambda r, i, wh: (wh[2 + r], i, 0))],
            out_specs=pl.BlockSpec((None, tr, cols), lambda r, i, wh: (wh[2 + r], i, 0))),
        out_shape=jax.ShapeDtypeStruct((4, rows, cols), BF16),
        compiler_params=_cparams("parallel", "parallel"),
    )(where, own, recv)


def _adamw_big(w, m, v, own, from_sibling, recv3, where, name):
    rows, cols = w.shape
    tr = _row_tile(rows, cols, 256 * 1024)

    def body(where_ref, w_ref, m_ref, v_ref, own_ref, sib_ref, r_ref, g_out, d_out, m_out, v_out):
        g = own_ref[...].astype(F32) + sib_ref[...].astype(F32)
        for r in range(3):
            g = g + r_ref[r].astype(F32)
        delta, m_new, v_new = _adamw_math(w_ref[...], g, m_ref[...], v_ref[...])
        g_out[...] = g
        d_out[...] = delta
        m_out[...] = m_new
        v_out[...] = v_new

    blk = pl.BlockSpec((tr, cols), lambda i, q_ref: (i, 0))
    return pl.pallas_call(
        body, name=name,
        grid_spec=pltpu.PrefetchScalarGridSpec(
            num_scalar_prefetch=1, grid=(rows // tr,),
            in_specs=[blk, blk, blk,
                      pl.BlockSpec((None, None, tr, cols), lambda i, wh: (wh[1], wh[0], i, 0)),
                      pl.BlockSpec((None, tr, cols), lambda i, wh: (wh[1], i, 0)),
                      pl.BlockSpec((3, tr, cols), lambda i, wh: (0, i, 0))],
            out_specs=[blk] * 4),
        out_shape=[jax.ShapeDtypeStruct((rows, cols), F32)] * 4,
        compiler_params=_cparams("parallel"),
    )(where, w, m, v, own, from_sibling, recv3)


def _small_update(me_idx, packed, rep_params, rep_places, meta_wmv, meta_row0, wdw_wmv, wdw_row0, loss_row0):
    n_rep = len(rep_params)
    meta_rows, meta_cols = meta_wmv[0].shape
    wdw_rows, wdw_cols = wdw_wmv[0].shape

    def body(me_ref, *refs):
        pos = 0

        def take(k):
            nonlocal pos
            out = refs[pos:pos + k]
            pos += k
            return out

        rep_in = [take(3) for _ in range(n_rep)]
        rep_g = take(n_rep)
        meta_in, (meta_g,) = take(3), take(1)
        wdw_in, (wdw_g,) = take(3), take(1)
        (loss_ref,) = take(1)
        rep_out = [take(4) for _ in range(n_rep)]
        meta_out, wdw_out = take(4), take(4)
        (loss_out,) = take(1)

        def update(wmv, g, outs):
            delta, m_new, v_new = _adamw_math(wmv[0][...], g, wmv[1][...], wmv[2][...])
            for o_ref, val in zip(outs, (g, delta, m_new, v_new)):
                o_ref[...] = val

        for wmv, g_ref, outs in zip(rep_in, rep_g, rep_out):
            g = jnp.sum(g_ref[0], axis=0, keepdims=True)
            for j in range(1, N_DEV):
                g = g + jnp.sum(g_ref[j], axis=0, keepdims=True)
            update(wmv, g, outs)
        for wmv, g_ref, outs in ((meta_in, meta_g, meta_out), (wdw_in, wdw_g, wdw_out)):
            g = g_ref[0]
            for j in range(1, N_DEV):
                g = g + g_ref[j]
            update(wmv, g, outs)
        total = loss_ref[0]
        for j in range(1, N_DEV):
            total = total + loss_ref[j]
        loss_out[...] = total

    def whole(a):
        nd = a.ndim
        return pl.BlockSpec(a.shape, lambda i, me_ref, nd=nd: (0,) * nd)

    ins, in_specs = [], []
    for wmv in rep_params:
        ins += list(wmv)
        in_specs += [whole(a) for a in wmv]
    for wmv, (row0, col0) in zip(rep_params, rep_places):
        width = wmv[0].shape[1]
        ins.append(packed)
        in_specs.append(pl.BlockSpec((N_DEV, SUBLANES, width),
                                     lambda i, me_ref, rb=row0 // SUBLANES, cb=col0 // width: (0, rb, cb)))
    ins += list(meta_wmv) + [packed]
    in_specs += [whole(a) for a in meta_wmv]
    in_specs.append(pl.BlockSpec((N_DEV, meta_rows, meta_cols),
                                 lambda i, me_ref, rb=meta_row0 // meta_rows: (0, rb, me_ref[0])))
    ins += list(wdw_wmv) + [packed]
    in_specs += [whole(a) for a in wdw_wmv]
    in_specs.append(pl.BlockSpec((N_DEV, wdw_rows, wdw_cols),
                                 lambda i, me_ref, rb=wdw_row0 // wdw_rows: (0, rb, me_ref[0])))
    ins.append(packed)
    in_specs.append(pl.BlockSpec((N_DEV, SUBLANES, LANES), lambda i, me_ref, rb=loss_row0 // SUBLANES: (0, rb, 0)))

    out_shape, out_specs = [], []
    for wmv in list(rep_params) + [meta_wmv, wdw_wmv]:
        out_shape += [jax.ShapeDtypeStruct(wmv[0].shape, F32)] * 4
        out_specs += [whole(wmv[0])] * 4
    out_shape.append(jax.ShapeDtypeStruct((SUBLANES, LANES), F32))
    out_specs.append(pl.BlockSpec((SUBLANES, LANES), lambda i, me_ref: (0, 0)))

    outs = pl.pallas_call(
        body, name="small_update",
        grid_spec=pltpu.PrefetchScalarGridSpec(num_scalar_prefetch=1, grid=(1,), in_specs=in_specs,
                                               out_specs=out_specs),
        out_shape=out_shape, compiler_params=_cparams("arbitrary"),
    )(me_idx, *ins)
    groups = [outs[4 * i:4 * i + 4] for i in range(n_rep + 2)]
    return groups[:n_rep], groups[n_rep], groups[n_rep + 1], outs[-1]


def kernel(x, meta, g_pre_mix, w_in, w_pool_grp, pool_scale, w_pool_out, w_dw, b_dw, conv_ln_g, conv_ln_b, w_conv_out, w_o, g_post_mix, g_pre_mlp, w_up, w_down, g_post_mlp, loss_target, m_meta, m_g_pre_mix, m_w_in, m_w_pool_grp, m_pool_scale, m_w_pool_out, m_w_dw, m_b_dw, m_conv_ln_g, m_conv_ln_b, m_w_conv_out, m_w_o, m_g_post_mix, m_g_pre_mlp, m_w_up, m_w_down, m_g_post_mlp, v_meta, v_g_pre_mix, v_w_in, v_w_pool_grp, v_pool_scale, v_w_pool_out, v_w_dw, v_b_dw, v_conv_ln_g, v_conv_ln_b, v_w_conv_out, v_w_o, v_g_post_mix, v_g_pre_mlp, v_w_up, v_w_down, v_g_post_mlp):
    seq, d = x.shape[1], x.shape[2]
    pool_w = pool_scale.shape[1]
    conv_w = b_dw.shape[1]
    n_grp, grp_rows, gdim = w_pool_grp.shape[1:]
    lp = _round_up(N_META + seq, ROW_CHUNK)
    tm_half = lp // 2 if (lp // 2) % 16 == 0 else lp
    c_idx = lax.axis_index("c").astype(jnp.int32)
    chip_idx = (2 * lax.axis_index("x") + lax.axis_index("y")).astype(jnp.int32)
    me_idx = 2 * chip_idx + c_idx

    pad_taps = ((0, CONV_TAPS_PADDED - CONV_KERNEL), (0, 0))
    big = dict(w_in=w_in[0], w_pool_grp=w_pool_grp[0].reshape(n_grp * grp_rows, gdim), w_pool_out=w_pool_out[0],
               w_conv_out=w_conv_out[0], w_o=w_o[0], w_up=w_up[0], w_down=w_down[0])
    big_names = list(big)
    moments = dict(w_in=(m_w_in, v_w_in), w_pool_grp=(m_w_pool_grp, v_w_pool_grp), w_pool_out=(m_w_pool_out, v_w_pool_out),
                   w_conv_out=(m_w_conv_out, v_w_conv_out), w_o=(m_w_o, v_w_o), w_up=(m_w_up, v_w_up),
                   w_down=(m_w_down, v_w_down))
    slot_idx = me_idx.reshape(1)
    sources = dict(big, meta=meta, w_dw=jnp.pad(w_dw[0], pad_taps))

    def fill(k, after):
        return _fill_slot(sources[k], slot_idx, BF16 if k in big else F32, after, "fill_" + k)

    gather_groups = [["meta", "w_dw"], ["w_in"], ["w_pool_grp", "w_pool_out", "w_conv_out", "w_o"], ["w_up"], ["w_down"]]
    started, token = _gather_start([[fill(k, slot_idx) for k in names] for names in gather_groups[:2]], slot_idx,
                                   "gather_start_first", issue_order=(0, 3, 1, 2))
    started_rest, _ = _gather_start([[fill(k, token) for k in names] for names in gather_groups[2:]], token,
                                    "gather_start_rest")
    started += started_rest
    wg = {}
    x_idx, y_idx = lax.axis_index("x"), lax.axis_index("y")
    at = lambda px, py, pc: 4 * px + 2 * py + pc
    near_slots = jnp.stack([at(x_idx, y_idx, c_idx), at(x_idx, y_idx, 1 - c_idx), at(1 - x_idx, y_idx, c_idx),
                            at(x_idx, 1 - y_idx, c_idx), at(1 - x_idx, y_idx, 1 - c_idx),
                            at(x_idx, 1 - y_idx, 1 - c_idx)]).astype(jnp.int32)
    far_slots = jnp.stack([at(1 - x_idx, 1 - y_idx, c_idx), at(1 - x_idx, 1 - y_idx, 1 - c_idx)]).astype(jnp.int32)

    def gather_whole(gi, after_forward, after_finish):
        send, recv, lands = started[gi]
        fs, fr, lands = _gather_forward(lands, recv, after_forward(), f"gather_forward_{gi}")
        lands = _gather_finish(lands, send, recv, [(ALL_CHIPS, fs, fr)], [(ALL_CHIPS, fs, fr)], after_finish(),
                               f"gather_finish_{gi}")
        wg.update(zip(gather_groups[gi], lands))

    near_state = {}

    def gather_near(gi, after):
        send, recv, lands = started[gi]
        fs, fr, lands = _gather_forward(lands, recv, after, f"gather_forward_near_{gi}", which=NEAR)
        lands = _gather_finish(lands, send, recv, [(NEAR, fs, fr)], [], after, f"gather_finish_near_{gi}",
                               direct_sends=False)
        near_state[gi] = (fs, fr)
        return lands

    def gather_far(gi, lands, after):
        send, recv, _ = started[gi]
        fs, fr, lands = _gather_forward(lands, recv, after, f"gather_forward_far_{gi}", which=FAR)
        lands = _gather_finish(lands, send, recv, [(FAR, fs, fr)], [(NEAR,) + near_state[gi], (FAR, fs, fr)], after,
                               f"gather_finish_far_{gi}", own=False)
        wg.update(zip(gather_groups[gi], lands))
        return lands

    gather_whole(0, lambda: token, lambda: token)
    meta_full = wg["meta"].transpose(1, 0, 2).reshape(N_META, d)
    wdw_full = wg["w_dw"].transpose(1, 0, 2).reshape(CONV_TAPS_PADDED, conv_w)
    target = loss_target[0]
    h0, u1 = _rms_pre(meta_full, x[0], g_pre_mix, lp, seq)
    send, recv, w_in_lands = started[1]
    w_in_lands = _gather_finish(w_in_lands, send, recv, [], [], u1, "gather_finish_home_1", direct_sends=False)
    proj = _mm_slots(u1, w_in_lands[0], near_slots[:2], over="n", tm=tm_half, name="mm_proj_home")
    fs_far, fr_far, w_in_lands = _gather_forward(w_in_lands, recv, proj, "gather_forward_far_1", which=FAR)
    w_in_lands = _gather_finish(w_in_lands, send, recv, [(FAR, fs_far, fr_far)], [], proj, "gather_finish_far_1",
                                own=False, direct_sends=False)
    proj = _mm_slots(u1, w_in_lands[0], far_slots, over="n", tm=tm_half, base=proj, name="mm_proj_far")
    fs_near, fr_near, w_in_lands = _gather_forward(w_in_lands, recv, proj, "gather_forward_near_1", which=NEAR)
    w_in_lands = _gather_finish(w_in_lands, send, recv, [(NEAR, fs_near, fr_near)],
                                [(FAR, fs_far, fr_far), (NEAR, fs_near, fr_near)], proj, "gather_finish_near_1",
                                own=False)
    proj = _mm_slots(u1, w_in_lands[0], near_slots[2:], over="n", tm=tm_half, base=proj, name="mm_proj_near")
    wg["w_in"] = w_in_lands[0]
    conv_c = _conv_fwd(proj, wdw_full, b_dw, lp, conv_w, pool_w)
    s_act = _ln_silu(conv_c, conv_ln_g, conv_ln_b, lp)
    gather_whole(2, lambda: proj, lambda: s_act)
    wpg_full = wg["w_pool_grp"].reshape(N_DEV, n_grp, grp_rows, gdim).transpose(1, 0, 2, 3).reshape(n_grp, gdim, gdim)
    w_o_full = wg["w_o"].reshape(d, d)
    ya_pre = _pool_fwd(proj, wpg_full, pool_scale, lp)
    m_mix, y_a, y_b = _gate_mix(ya_pre, s_act, wg["w_pool_out"], wg["w_conv_out"], proj, d, "gate_mix")
    o = _mm(m_mix, w_o_full, mode="nn", tm=tm_half, tn=512, name="mm_o")
    h1, u2 = _post_mix(o, h0, g_post_mix, g_pre_mlp, lp)
    relu2 = lambda acc: jnp.square(jnp.maximum(acc, 0.0))
    (w_up_near,) = gather_near(3, u2)
    act = _mm_slots(u2, w_up_near, near_slots, over="n", tm=tm_half, out_dtype=BF16, epilogue=relu2, name="mm_up_near")
    (w_up_all,) = gather_far(3, [w_up_near], act)
    act = _mm_slots(u2, w_up_all, far_slots, over="n", tm=tm_half, out_dtype=BF16, epilogue=relu2, base=act,
                    name="mm_up_far")
    (w_down_near,) = gather_near(4, act)
    f = _mm_slots(act, w_down_near, near_slots, over="k", tm=tm_half, tn=d, name="mm_down_near")
    (w_down_all,) = gather_far(4, [w_down_near], f)
    f = _mm_slots(act, w_down_all, far_slots, over="k", tm=tm_half // 2, tn=d, base=f, name="mm_down_far")
    w_down_full = w_down_all.reshape(-1, d)

    big_out = {}

    def to_sibling(names, grads, after, tag):
        send, recv, grads, lands, token = _pair_exchange_start(grads, after, "grads_to_sibling_start_" + tag)
        return (names, send, recv, grads, lands, tag), token

    where = jnp.stack([c_idx, chip_idx, 2 * (1 - x_idx) + y_idx, 2 * x_idx + (1 - y_idx),
                       2 * (1 - x_idx) + (1 - y_idx)]).astype(jnp.int32)

    def to_owner(handle, after):
        names, send, recv, grads, from_sib, tag = handle
        grads, from_sib = _pair_exchange_finish(grads, from_sib, send, recv, after, "grads_to_sibling_finish_" + tag)
        own = [g.reshape((4, 2) + g.shape[1:]) for g in grads]
        sums = [_pair_sum(o, r, where, "pair_sum_" + k) for k, o, r in zip(names, own, from_sib)]
        send, recv, sums, lands, token = _chip_exchange_start(sums, after, "grads_to_owner_start_" + tag)
        return (names, send, recv, sums, lands, own, from_sib, tag), token

    def update(handle, after):
        names, send, recv, sums, lands, own, from_sib, tag = handle
        got = _chip_exchange_finish(sums, lands, send, recv, after, "grads_to_owner_finish_" + tag)
        for k, o, s, r3 in zip(names, own, from_sib, got):
            w2 = big[k]
            shape = moments[k][0].shape
            outs = _adamw_big(w2, moments[k][0].reshape(w2.shape), moments[k][1].reshape(w2.shape), o, s, r3,
                              where, "adamw_" + k)
            big_out[k] = [a.reshape(shape) for a in outs]
        return big_out[names[-1]][0]

    df, dh, dg_post_mlp, loss_part = _loss_head(f, h1, target, g_post_mlp, lp, seq)
    d_up = _mm(df, w_down_full, mode="nt", tm=tm_half, tn=1024, out_dtypes=(BF16,), extras=[(act, 0)],
               epilogue=lambda acc, a: (acc * (2.0 * jnp.sqrt(a.astype(F32))),), row_splits=2, name="mm_d_up")
    g_w_down = _mm(act, df, mode="tn", tm=1024, tn=1024, out_dtypes=(BF16,), name="mm_g_down")
    sib_down, token = to_sibling(["w_down"], [g_w_down.reshape(N_DEV, -1, d)], slot_idx, "down")
    g_w_up = _mm(u2, d_up, mode="tn", tm=1024, tn=wg["w_up"].shape[2], out_blocked=True, out_dtypes=(BF16,),
                 after=token, name="mm_g_up")
    sib_up, token = to_sibling(["w_up"], [g_w_up], slot_idx, "up")
    pending_down, token = to_owner(sib_down, token)
    du2 = _mm(d_up, wg["w_up"], mode="nt", tm=tm_half // 2, tn=1024, b_blocked=True, kb=4, after=token, name="mm_du2")
    pending_up, token = to_owner(sib_up, du2)
    dh1, do, dg_pre_mlp, dg_post_mix = _mid_bwd(du2, h1, dh, o, g_pre_mlp, g_post_mix, lp)

    def gate_bwd(dm, ga, gb, ya, yb):
        sa, sb = jax.nn.sigmoid(ga), jax.nn.sigmoid(gb)
        return (dm * ya.astype(F32) * sa * (1.0 - sa), dm * yb.astype(F32) * sb * (1.0 - sb), dm * sa, dm * sb)

    gate_tn = 512
    ga_col0, gb_col0 = proj.shape[1] - 2 * d, proj.shape[1] - d
    dproj, d_gb, d_ya, d_yb = _mm(
        do, w_o_full, mode="nt", tm=tm_half, tn=gate_tn, out_dtypes=(BF16,) * 4,
        extras=[(proj, ga_col0 // gate_tn), (proj, gb_col0 // gate_tn), (y_a, 0), (y_b, 0)], epilogue=gate_bwd,
        out_places=((proj.shape[1], ga_col0), None, None, None), after=token, row_splits=2, name="mm_dm")
    g_w_o = _mm(m_mix, do, mode="tn", tm=1024, tn=1024, out_dtypes=(BF16,), name="mm_g_o")
    bn_out = wg["w_pool_out"].shape[2]
    g_w_pool_out = _mm(ya_pre, d_ya, mode="tn", tm=pool_w, tn=4 * bn_out, out_blocked=True, out_block=bn_out,
                       out_dtypes=(BF16,), name="mm_g_pool_out")
    g_w_conv_out = _mm(s_act, d_yb, mode="tn", tm=conv_w, tn=4 * bn_out, out_blocked=True, out_block=bn_out,
                       out_dtypes=(BF16,), name="mm_g_conv_out")
    sib_mix, token = to_sibling(["w_o", "w_pool_out", "w_conv_out"],
                                [g_w_o.reshape(N_DEV, -1, d), g_w_pool_out, g_w_conv_out], slot_idx, "mix")
    d_ya_pre = _mm(d_ya, wg["w_pool_out"], mode="nt", tm=tm_half, tn=pool_w, b_blocked=True, kb=4, after=token,
                   name="mm_d_ya_pre")
    d_s = _mm(d_yb, wg["w_conv_out"], mode="nt", tm=tm_half, tn=conv_w, b_blocked=True, kb=4, name="mm_d_s")
    pending_mix, token = to_owner(sib_mix, d_s)
    dproj, g_wpg, d_scale = _pool_bwd(proj, d_ya_pre, wpg_full, pool_scale, dproj, lp)
    dc, d_ln_g, d_ln_b, d_b_dw = _ln_silu_bwd(conv_c, d_s, conv_ln_g, conv_ln_b, lp)
    dproj, dgc, g_wdw = _conv_bwd(proj, dc, wdw_full, dproj, lp, conv_w, pool_w)
    dproj = _place_columns(dproj, [(dgc, pool_w + conv_w), (d_gb, gb_col0)], "place_dproj")
    g_w_in = _mm(u1, dproj, mode="tn", tm=1024, tn=wg["w_in"].shape[2], out_blocked=True, out_dtypes=(BF16,),
                 after=token, name="mm_g_in")
    g_wpg_slots = g_wpg.astype(BF16).reshape(n_grp, N_DEV, grp_rows, gdim).transpose(1, 0, 2, 3)
    sib_in, token = to_sibling(["w_pool_grp", "w_in"],
                               [g_wpg_slots.reshape(N_DEV, n_grp * grp_rows, gdim), g_w_in], slot_idx, "in")
    done = update(pending_down, token)
    pending_in, token = to_owner(sib_in, done)
    done = update(pending_up, token)
    du1 = _mm(dproj, wg["w_in"], mode="nt", tm=tm_half // 2, tn=1024, b_blocked=True, kb=4, after=done, name="mm_du1")
    grad_x2, grad_meta_part, dg_pre_mix = _pre_mix_bwd(du1, h0, dh1, g_pre_mix, seq)
    grad_x = grad_x2[None]

    assert pool_w + conv_w == d and conv_w <= d and LANES <= d
    widen = lambda a: jnp.pad(a, ((0, 0), (0, d - a.shape[1])))
    packed = jnp.concatenate([
        dg_pre_mix, dg_post_mix, dg_pre_mlp, dg_post_mlp,
        jnp.concatenate([d_scale, d_ln_g], axis=1), jnp.concatenate([d_ln_b, d_b_dw], axis=1),
        grad_meta_part, widen(g_wdw), widen(loss_part)], axis=0)
    rep = dict(g_pre_mix=((g_pre_mix, m_g_pre_mix, v_g_pre_mix), (0, 0)),
               g_post_mix=((g_post_mix, m_g_post_mix, v_g_post_mix), (SUBLANES, 0)),
               g_pre_mlp=((g_pre_mlp, m_g_pre_mlp, v_g_pre_mlp), (2 * SUBLANES, 0)),
               g_post_mlp=((g_post_mlp, m_g_post_mlp, v_g_post_mlp), (3 * SUBLANES, 0)),
               pool_scale=((pool_scale, m_pool_scale, v_pool_scale), (4 * SUBLANES, 0)),
               conv_ln_g=((conv_ln_g, m_conv_ln_g, v_conv_ln_g), (4 * SUBLANES, pool_w)),
               conv_ln_b=((conv_ln_b, m_conv_ln_b, v_conv_ln_b), (5 * SUBLANES, 0)),
               b_dw=((b_dw, m_b_dw, v_b_dw), (5 * SUBLANES, conv_w)))
    meta_row0 = 6 * SUBLANES
    wdw_row0 = meta_row0 + N_META
    loss_row0 = wdw_row0 + CONV_TAPS_PADDED
    (small_started,), token = _gather_start([[_fill_slot(packed, slot_idx, F32, slot_idx, "fill_small")]], grad_x2,
                                            "gather_small_start")
    done = update(pending_mix, token)
    done = update(pending_in, done)
    send, recv, lands = small_started
    fsend, frecv, lands = _gather_forward(lands, recv, done, "gather_small_forward")
    (packed_all,) = _gather_finish(lands, send, recv, [(ALL_CHIPS, fsend, frecv)], [(ALL_CHIPS, fsend, frecv)], done,
                                   "gather_small_finish")
    rep_names = list(rep)
    wdw_wmv = [jnp.pad(a[0], pad_taps) for a in (w_dw, m_w_dw, v_w_dw)]
    rep_out, meta_out, wdw_out, loss_blk = _small_update(
        slot_idx, packed_all, [rep[k][0] for k in rep_names], [rep[k][1] for k in rep_names],
        (meta, m_meta, v_meta), meta_row0, wdw_wmv, wdw_row0, loss_row0)
    small_out = dict(zip(rep_names, rep_out))
    small_out["meta"] = meta_out
    small_out["w_dw"] = [a[:CONV_KERNEL][None] for a in wdw_out]

    order = ["meta", "g_pre_mix", "w_in", "w_pool_grp", "pool_scale", "w_pool_out", "w_dw", "b_dw", "conv_ln_g",
             "conv_ln_b", "w_conv_out", "w_o", "g_post_mix", "g_pre_mlp", "w_up", "w_down", "g_post_mlp"]
    by_name = {**big_out, **small_out}
    result = [loss_blk[0, 0], grad_x]
    for kind in range(4):
        result += [by_name[k][kind] for k in order]
    return tuple(result)
```

```python
import functools

import jax
import jax.numpy as jnp
from jax import lax
from jax.experimental import pallas as pl
from jax.experimental.pallas import tpu as pltpu

F32 = jnp.float32
BF16 = jnp.bfloat16
MESH = pl.DeviceIdType.MESH

N_DEV = 8
N_META = 16
POOL_WINDOWS = (2, 4, 8, 16)
CONV_KERNEL = 31
CONV_TAPS_PADDED = 32
RMS_EPS = 1e-6
LN_EPS = 1e-5
ADAM_LR = 0.001
ADAM_B1 = 0.9
ADAM_B2 = 0.999
ADAM_EPS = 1e-08
ADAM_WD = 0.01
ADAM_STEP = 10

LANES = 128
SUBLANES = 8
ROW_CHUNK = 128
HALO = 32
VMEM_LIMIT_BYTES = 56 * 1024 * 1024


def _cparams(*sem):
    return pltpu.CompilerParams(dimension_semantics=sem if sem else None, vmem_limit_bytes=VMEM_LIMIT_BYTES)


def _round_up(n, m):
    return (n + m - 1) // m * m


def _row_tile(rows, cols, max_elems=640 * 1024):
    best = None
    for t in range(16, rows + 1, 16):
        if rows % t == 0 and (best is None or t * cols <= max_elems):
            best = t
    assert best is not None, (rows, cols)
    return best


def _rowsum8(a):
    t, w = a.shape
    return a.reshape(t // SUBLANES, SUBLANES, w).sum(axis=0)


def _mesh_pos():
    return lax.axis_index("x"), lax.axis_index("y"), lax.axis_index("c")


HBM_SPEC = pl.BlockSpec(memory_space=pltpu.HBM)
SEM_SPEC = pl.BlockSpec(memory_space=pltpu.SEMAPHORE)
ANY_SPEC = pl.BlockSpec(memory_space=pl.ANY)
_DATAFLOW = pltpu.SideEffectType.DATAFLOW_SIDE_EFFECTING


SIBLING_BARRIER_ID = 1


def _sibling_handshake():
    x, y, c = _mesh_pos()
    barrier = pltpu.get_barrier_semaphore()
    pl.semaphore_signal(barrier, inc=1, device_id=(x, y, 1 - c), device_id_type=MESH)
    pl.semaphore_wait(barrier, 1)


def _hbm(a):
    return pltpu.with_memory_space_constraint(a, pltpu.HBM)


def _slot(p):
    return 4 * p[0] + 2 * p[1] + p[2]


def _fill_slot(w, slot_idx, dtype, after, name):
    rows, cols = w.shape
    tr = _row_tile(rows, cols) if rows % 16 == 0 else rows

    def body(idx_ref, w_ref, _, out_ref):
        out_ref[...] = w_ref[...].astype(dtype)

    return pl.pallas_call(
        body, name=name,
        grid_spec=pltpu.PrefetchScalarGridSpec(
            num_scalar_prefetch=1, grid=(rows // tr,),
            in_specs=[pl.BlockSpec((tr, cols), lambda i, idx_ref: (i, 0)), ANY_SPEC],
            out_specs=pl.BlockSpec((None, tr, cols), lambda i, idx_ref: (idx_ref[0], i, 0))),
        out_shape=jax.ShapeDtypeStruct((N_DEV, rows, cols), dtype),
        compiler_params=_cparams("parallel"),
    )(slot_idx, w, after)


def _gather_start(groups, after, name, issue_order=(0, 1, 2, 3)):
    flat = [g for grp in groups for g in grp]
    n, n_grp = len(flat), len(groups)

    def body(*refs):
        lands = refs[:n]
        sems = refs[n + 1:n + 1 + 2 * n_grp]
        token = refs[-1]
        x, y, c = _mesh_pos()
        targets = [(x, y, 1 - c), (1 - x, y, c), (x, 1 - y, c), (1 - x, 1 - y, c)]
        t = 0
        for gi, grp in enumerate(groups):
            for ti in range(len(grp)):
                mine = lands[t].at[_slot((x, y, c))]
                for k in issue_order:
                    pltpu.make_async_remote_copy(
                        src_ref=mine, dst_ref=mine,
                        send_sem=sems[2 * gi].at[4 * ti + k], recv_sem=sems[2 * gi + 1].at[4 * ti + k],
                        device_id=targets[k], device_id_type=MESH).start()
                t += 1
        token[...] = jnp.zeros_like(token)

    sem_shapes = []
    for grp in groups:
        sem_shapes += [pltpu.SemaphoreType.DMA((4 * len(grp),))] * 2
    outs = pl.pallas_call(
        body, name=name,
        out_shape=tuple(sem_shapes + [pltpu.HBM(g.shape, g.dtype) for g in flat]
                        + [jax.ShapeDtypeStruct((SUBLANES, LANES), F32)]),
        in_specs=tuple([HBM_SPEC] * n + [ANY_SPEC]),
        out_specs=tuple([SEM_SPEC] * (2 * n_grp) + [HBM_SPEC] * n + [pl.BlockSpec(memory_space=pltpu.VMEM)]),
        input_output_aliases={i: 2 * n_grp + i for i in range(n)},
        compiler_params=pltpu.CompilerParams(has_side_effects=_DATAFLOW),
    )(*[_hbm(g) for g in flat], after)
    sems, lands, token = outs[:2 * n_grp], outs[2 * n_grp:-1], outs[-1]
    res, t = [], 0
    for gi, grp in enumerate(groups):
        res.append((sems[2 * gi], sems[2 * gi + 1], list(lands[t:t + len(grp)])))
        t += len(grp)
    return res, token


NEAR, FAR, ALL_CHIPS = (0, 1), (2,), (0, 1, 2)


def _gather_forward(lands, recv_sems, after, name, which=ALL_CHIPS):
    n, nw = len(lands), len(which)

    def body(*refs):
        land_refs, recv, _ = refs[:n], refs[n], refs[n + 1]
        fsend, frecv = refs[n + 2], refs[n + 3]
        x, y, c = _mesh_pos()
        chips = [(1 - x, y), (x, 1 - y), (1 - x, 1 - y)]
        _sibling_handshake()
        for t in range(n):
            for i, j in enumerate(which):
                blk = land_refs[t].at[_slot((*chips[j], c))]
                pltpu.make_async_remote_copy(src_ref=blk, dst_ref=blk, send_sem=fsend.at[nw * t + i],
                                             recv_sem=recv.at[4 * t + 1 + j],
                                             device_id=(x, y, 1 - c), device_id_type=MESH).wait_recv()
                pltpu.make_async_remote_copy(src_ref=blk, dst_ref=blk, send_sem=fsend.at[nw * t + i],
                                             recv_sem=frecv.at[nw * t + i],
                                             device_id=(x, y, 1 - c), device_id_type=MESH).start()

    outs = pl.pallas_call(
        body, name=name,
        out_shape=tuple([pltpu.SemaphoreType.DMA((nw * n,))] * 2 + [pltpu.HBM(g.shape, g.dtype) for g in lands]),
        in_specs=tuple([HBM_SPEC] * n + [SEM_SPEC, ANY_SPEC]),
        out_specs=tuple([SEM_SPEC] * 2 + [HBM_SPEC] * n),
        input_output_aliases={i: 2 + i for i in range(n)},
        compiler_params=pltpu.CompilerParams(has_side_effects=_DATAFLOW, collective_id=SIBLING_BARRIER_ID),
    )(*lands, recv_sems, after)
    return outs[0], outs[1], list(outs[2:])


def _gather_finish(lands, send_sems, recv_sems, arrivals, sends, after, name, own=True, direct_sends=True):
    n = len(lands)
    fwd = list(arrivals) + list(sends)
    sem_args = [send_sems, recv_sems]
    where = []
    for _, fs, fr in fwd:
        pos = []
        for arr in (fs, fr):
            hit = [i for i, have in enumerate(sem_args) if have is arr]
            if not hit:
                sem_args.append(arr)
                hit = [len(sem_args) - 1]
            pos.append(hit[0])
        where.append(pos)

    def body(*refs):
        land_refs = refs[:n]
        send, recv = refs[n], refs[n + 1]
        fwd_refs = [refs[n + p] for pos in where for p in pos]
        x, y, c = _mesh_pos()
        sibling = (x, y, 1 - c)
        chips = [(1 - x, y), (x, 1 - y), (1 - x, 1 - y)]

        def desc(ref, s_sem, r_sem):
            return pltpu.make_async_remote_copy(src_ref=ref, dst_ref=ref, send_sem=s_sem, recv_sem=r_sem,
                                                device_id=sibling, device_id_type=MESH)

        for t in range(n):
            mine = land_refs[t].at[_slot((x, y, c))]
            if own:
                desc(land_refs[t].at[_slot(sibling)], send.at[4 * t], recv.at[4 * t]).wait_recv()
            for a, (which, _, _) in enumerate(fwd):
                fs, fr = fwd_refs[2 * a], fwd_refs[2 * a + 1]
                for i, j in enumerate(which):
                    if a < len(arrivals):
                        desc(land_refs[t].at[_slot((*chips[j], 1 - c))], fs.at[len(which) * t + i],
                             fr.at[len(which) * t + i]).wait_recv()
                    else:
                        desc(land_refs[t].at[_slot((*chips[j], c))], fs.at[len(which) * t + i],
                             fr.at[len(which) * t + i]).wait_send()
            if direct_sends:
                for k in range(4):
                    desc(mine, send.at[4 * t + k], recv.at[4 * t + k]).wait_send()

    outs = pl.pallas_call(
        body, name=name,
        out_shape=tuple(pltpu.HBM(g.shape, g.dtype) for g in lands),
        in_specs=tuple([HBM_SPEC] * n + [SEM_SPEC] * len(sem_args) + [ANY_SPEC]),
        out_specs=tuple([HBM_SPEC] * n),
        input_output_aliases={i: i for i in range(n)},
        compiler_params=pltpu.CompilerParams(has_side_effects=_DATAFLOW),
    )(*lands, *sem_args, after)
    return list(outs)


def _pair_exchange_start(grads, after, name):
    n = len(grads)

    def body(*refs):
        ins, lands = refs[:n], refs[n:2 * n]
        send, recv = refs[2 * n + 1], refs[2 * n + 2]
        token = refs[-1]
        x, y, c = _mesh_pos()
        _sibling_handshake()
        for t in range(n):
            for q in range(4):
                pltpu.make_async_remote_copy(
                    src_ref=ins[t].at[2 * q + 1 - c], dst_ref=lands[t].at[q],
                    send_sem=send.at[4 * t + q], recv_sem=recv.at[4 * t + q],
                    device_id=(x, y, 1 - c), device_id_type=MESH).start()
        token[...] = jnp.zeros_like(token)

    land_shapes = [(4,) + g.shape[1:] for g in grads]
    outs = pl.pallas_call(
        body, name=name,
        out_shape=tuple([pltpu.SemaphoreType.DMA((4 * n,))] * 2 + [pltpu.HBM(g.shape, g.dtype) for g in grads]
                        + [pltpu.HBM(ls, g.dtype) for ls, g in zip(land_shapes, grads)]
                        + [jax.ShapeDtypeStruct((SUBLANES, LANES), F32)]),
        in_specs=tuple([HBM_SPEC] * (2 * n) + [ANY_SPEC]),
        out_specs=tuple([SEM_SPEC] * 2 + [HBM_SPEC] * (2 * n) + [pl.BlockSpec(memory_space=pltpu.VMEM)]),
        input_output_aliases={i: 2 + i for i in range(2 * n)},
        compiler_params=pltpu.CompilerParams(has_side_effects=_DATAFLOW, collective_id=SIBLING_BARRIER_ID),
    )(*[_hbm(g) for g in grads], *[_hbm(lax.empty(ls, g.dtype)) for ls, g in zip(land_shapes, grads)], after)
    return outs[0], outs[1], list(outs[2:2 + n]), list(outs[2 + n:2 + 2 * n]), outs[-1]


def _pair_exchange_finish(grads, lands, send_sems, recv_sems, after, name):
    n = len(grads)

    def body(*refs):
        ins, land_refs = refs[:n], refs[n:2 * n]
        send, recv = refs[2 * n], refs[2 * n + 1]
        x, y, c = _mesh_pos()
        for t in range(n):
            for q in range(4):
                cp = pltpu.make_async_remote_copy(
                    src_ref=ins[t].at[q], dst_ref=land_refs[t].at[q], send_sem=send.at[4 * t + q],
                    recv_sem=recv.at[4 * t + q], device_id=(x, y, 1 - c), device_id_type=MESH)
                cp.wait_send()
                cp.wait_recv()

    outs = pl.pallas_call(
        body, name=name,
        out_shape=tuple([pltpu.HBM(g.shape, g.dtype) for g in grads] + [pltpu.HBM(g.shape, g.dtype) for g in lands]),
        in_specs=tuple([HBM_SPEC] * (2 * n) + [SEM_SPEC] * 2 + [ANY_SPEC]),
        out_specs=tuple([HBM_SPEC] * (2 * n)),
        input_output_aliases={i: i for i in range(2 * n)},
        compiler_params=pltpu.CompilerParams(has_side_effects=_DATAFLOW),
    )(*grads, *lands, send_sems, recv_sems, after)
    return list(outs[:n]), list(outs[n:])


def _chip_exchange_start(sums, after, name):
    n = len(sums)

    def body(*refs):
        ins, lands = refs[:n], refs[n:2 * n]
        send, recv = refs[2 * n + 1], refs[2 * n + 2]
        token = refs[-1]
        x, y, c = _mesh_pos()
        chips = [(1 - x, y), (x, 1 - y), (1 - x, 1 - y)]
        for t in range(n):
            for r, chip in enumerate(chips):
                pltpu.make_async_remote_copy(
                    src_ref=ins[t].at[2 * chip[0] + chip[1]], dst_ref=lands[t].at[r],
                    send_sem=send.at[3 * t + r], recv_sem=recv.at[3 * t + r],
                    device_id=(*chip, c), device_id_type=MESH).start()
        token[...] = jnp.zeros_like(token)

    land_shapes = [(3,) + s.shape[1:] for s in sums]
    outs = pl.pallas_call(
        body, name=name,
        out_shape=tuple([pltpu.SemaphoreType.DMA((3 * n,))] * 2 + [pltpu.HBM(s.shape, s.dtype) for s in sums]
                        + [pltpu.HBM(ls, s.dtype) for ls, s in zip(land_shapes, sums)]
                        + [jax.ShapeDtypeStruct((SUBLANES, LANES), F32)]),
        in_specs=tuple([HBM_SPEC] * (2 * n) + [ANY_SPEC]),
        out_specs=tuple([SEM_SPEC] * 2 + [HBM_SPEC] * (2 * n) + [pl.BlockSpec(memory_space=pltpu.VMEM)]),
        input_output_aliases={i: 2 + i for i in range(2 * n)},
        compiler_params=pltpu.CompilerParams(has_side_effects=_DATAFLOW),
    )(*[_hbm(s) for s in sums], *[_hbm(lax.empty(ls, s.dtype)) for ls, s in zip(land_shapes, sums)], after)
    return outs[0], outs[1], list(outs[2:2 + n]), list(outs[2 + n:2 + 2 * n]), outs[-1]


def _chip_exchange_finish(sums, lands, send_sems, recv_sems, after, name):
    n = len(sums)

    def body(*refs):
        ins, land_refs = refs[:n], refs[n:2 * n]
        send, recv = refs[2 * n], refs[2 * n + 1]
        x, y, c = _mesh_pos()
        for t in range(n):
            for r in range(3):
                cp = pltpu.make_async_remote_copy(
                    src_ref=ins[t].at[r], dst_ref=land_refs[t].at[r], send_sem=send.at[3 * t + r],
                    recv_sem=recv.at[3 * t + r],
                    device_id=(x, y, 1 - c), device_id_type=MESH)
                cp.wait_send()
                cp.wait_recv()

    outs = pl.pallas_call(
        body, name=name,
        out_shape=tuple(pltpu.HBM(g.shape, g.dtype) for g in lands),
        in_specs=tuple([HBM_SPEC] * (2 * n) + [SEM_SPEC] * 2 + [ANY_SPEC]),
        out_specs=tuple([HBM_SPEC] * n),
        input_output_aliases={n + i: i for i in range(n)},
        compiler_params=pltpu.CompilerParams(has_side_effects=_DATAFLOW),
    )(*sums, *lands, send_sems, recv_sems, after)
    return list(outs)


def _mm(a, b, *, mode, tm, tn, tk=None, b_blocked=False, out_blocked=False, out_dtypes=(F32,),
        epilogue=None, extras=(), after=None, kb=1, out_places=None, row_splits=1, out_block=None, name):
    if mode == "nn":
        m, k = a.shape
        n = b.shape[0] * b.shape[2] if b_blocked else b.shape[1]
        dims = (((1,), (0,)), ((), ()))
    elif mode == "nt":
        m, k = a.shape
        n = b.shape[1] if b_blocked else b.shape[0]
        if b_blocked:
            tk = kb * b.shape[2]
        dims = (((1,), (1,)), ((), ()))
    else:
        k, m = a.shape
        n = b.shape[1]
        dims = (((0,), (0,)), ((), ()))
    tk = k if tk is None else tk
    assert m % tm == 0 and n % tn == 0 and k % tk == 0, (name, m, n, k, tm, tn, tk)
    gm, gn, gk = m // tm, n // tn, k // tk
    if b_blocked:
        assert (tn if mode == "nn" else tk) == kb * b.shape[2], name
    if row_splits > 1:
        assert gk == 1 and epilogue is not None and mode != "tn" and not b_blocked and tm % (16 * row_splits) == 0, name

    if mode == "nn":
        a_spec = pl.BlockSpec((tm, tk), lambda i, j, kk: (i, kk))
        b_spec = (pl.BlockSpec((None, tk, tn), lambda i, j, kk: (j, kk, 0)) if b_blocked
                  else pl.BlockSpec((tk, tn), lambda i, j, kk: (kk, j)))
    elif mode == "nt":
        a_spec = pl.BlockSpec((tm, tk), lambda i, j, kk: (i, kk))
        b_spec = (pl.BlockSpec((kb, tn, tk // kb), lambda i, j, kk: (kk, j, 0)) if b_blocked
                  else pl.BlockSpec((tn, tk), lambda i, j, kk: (j, kk)))
    else:
        a_spec = pl.BlockSpec((tk, tm), lambda i, j, kk: (kk, i))
        b_spec = pl.BlockSpec((tk, tn), lambda i, j, kk: (kk, j))
    out_pack = 1
    if out_blocked and out_block is not None and out_block != tn:
        assert tn % out_block == 0, name
        out_pack = tn // out_block
        out_spec = pl.BlockSpec((out_pack, tm, out_block), lambda i, j, kk: (j, i, 0))
        out_shape = (n // out_block, m, out_block)
    elif out_blocked:
        out_spec = pl.BlockSpec((None, tm, tn), lambda i, j, kk: (j, i, 0))
        out_shape = (gn, m, tn)
    else:
        out_spec = pl.BlockSpec((tm, tn), lambda i, j, kk: (i, j))
        out_shape = (m, n)
    extra_specs = [pl.BlockSpec((tm, tn), functools.partial(lambda i, j, kk, off: (i, j + off), off=off))
                   for _, off in extras]
    n_extra, n_out = len(extras), len(out_dtypes)
    n_after = 0 if after is None else 1
    places = out_places if out_places is not None else (None,) * n_out

    def body(a_ref, b_ref, *rest):
        extra_refs = rest[:n_extra]
        out_refs = rest[n_extra + n_after:n_extra + n_after + n_out]

        def finish(acc):
            if epilogue is None:
                res = (acc,)
            else:
                res = epilogue(acc, *[e[...] for e in extra_refs])
            for o_ref, r in zip(out_refs, res):
                if out_pack == 1:
                    o_ref[...] = r.astype(o_ref.dtype)
                else:
                    for h in range(out_pack):
                        o_ref[h] = r[:, h * out_block:(h + 1) * out_block].astype(o_ref.dtype)

        if row_splits > 1:
            strip = tm // row_splits
            for h in range(row_splits):
                rows = slice(h * strip, (h + 1) * strip)
                acc = lax.dot_general(a_ref[rows, :], b_ref[...], dims, preferred_element_type=F32)
                res = epilogue(acc, *[e[rows, :] for e in extra_refs])
                for o_ref, r in zip(out_refs, res):
                    o_ref[rows, :] = r.astype(o_ref.dtype)
            return
        if mode == "nt" and b_blocked:
            bk = tk // kb
            part = lax.dot_general(a_ref[:, :bk], b_ref[0], dims, preferred_element_type=F32)
            for h in range(1, kb):
                part = part + lax.dot_general(a_ref[:, h * bk:(h + 1) * bk], b_ref[h], dims,
                                              preferred_element_type=F32)
        else:
            part = lax.dot_general(a_ref[...], b_ref[...], dims, preferred_element_type=F32)
        if gk == 1:
            finish(part)
        else:
            acc_ref = rest[-1]
            kk = pl.program_id(2)

            @pl.when(kk == 0)
            def _():
                acc_ref[...] = part

            @pl.when(kk > 0)
            def _():
                acc_ref[...] += part

            @pl.when(kk == gk - 1)
            def _():
                finish(acc_ref[...])

    outs = pl.pallas_call(
        body, name=name, grid=(gm, gn, gk),
        in_specs=[a_spec, b_spec] + extra_specs + [ANY_SPEC] * n_after,
        out_specs=[out_spec if place is None else
                   pl.BlockSpec((tm, tn), functools.partial(lambda i, j, kk, off: (i, j + off), off=place[1] // tn))
                   for place in places],
        out_shape=[jax.ShapeDtypeStruct(out_shape if place is None else (m, place[0]), dt)
                   for dt, place in zip(out_dtypes, places)],
        scratch_shapes=[pltpu.VMEM((tm, tn), F32)] if gk > 1 else [],
        compiler_params=_cparams("parallel", "parallel", "arbitrary"),
    )(a, b, *[e for e, _ in extras], *([] if after is None else [after]))
    return outs[0] if n_out == 1 else outs


def _mm_slots(a, w, slots, *, over, tm, tn=None, out_dtype=F32, epilogue=None, base=None, name):
    m = a.shape[0]
    ns = slots.shape[0]
    n_slots, w1, w2 = w.shape
    assert m % tm == 0
    if over == "n":
        k, bn = w1, w2

        def body(slots_ref, a_ref, w_ref, *rest):
            out_ref = rest[-1]
            acc = jnp.dot(a_ref[...], w_ref[...], preferred_element_type=F32)
            out_ref[...] = (acc if epilogue is None else epilogue(acc)).astype(out_ref.dtype)

        in_specs = [pl.BlockSpec((tm, k), lambda i, j, s: (i, 0)),
                    pl.BlockSpec((None, k, bn), lambda i, j, s: (s[j], 0, 0))]
        args = [a, w]
        aliases = {}
        if base is not None:
            in_specs.append(ANY_SPEC)
            args.append(base)
            aliases = {3: 0}
        return pl.pallas_call(
            body, name=name,
            grid_spec=pltpu.PrefetchScalarGridSpec(
                num_scalar_prefetch=1, grid=(m // tm, ns), in_specs=in_specs,
                out_specs=pl.BlockSpec((tm, bn), lambda i, j, s: (i, s[j]))),
            out_shape=jax.ShapeDtypeStruct((m, n_slots * bn), out_dtype),
            input_output_aliases=aliases,
            compiler_params=_cparams("parallel", "arbitrary"),
        )(slots, *args)

    bk, n = w1, w2
    tn = n if tn is None else tn
    assert n % tn == 0

    def body(slots_ref, a_ref, w_ref, *rest):
        out_ref, acc_ref = rest[-2], rest[-1]
        kk = pl.program_id(2)
        part = jnp.dot(a_ref[...], w_ref[...], preferred_element_type=F32)

        @pl.when(kk == 0)
        def _():
            acc_ref[...] = part if base is None else part + rest[0][...]

        @pl.when(kk > 0)
        def _():
            acc_ref[...] += part

        @pl.when(kk == ns - 1)
        def _():
            out_ref[...] = acc_ref[...].astype(out_ref.dtype)

    in_specs = [pl.BlockSpec((tm, bk), lambda i, j, kk, s: (i, s[kk])),
                pl.BlockSpec((None, bk, tn), lambda i, j, kk, s: (s[kk], 0, j))]
    args = [a, w]
    if base is not None:
        in_specs.append(pl.BlockSpec((tm, tn), lambda i, j, kk, s: (i, j)))
        args.append(base)
    return pl.pallas_call(
        body, name=name,
        grid_spec=pltpu.PrefetchScalarGridSpec(
            num_scalar_prefetch=1, grid=(m // tm, n // tn, ns), in_specs=in_specs,
            out_specs=pl.BlockSpec((tm, tn), lambda i, j, kk, s: (i, j)),
            scratch_shapes=[pltpu.VMEM((tm, tn), F32)]),
        out_shape=jax.ShapeDtypeStruct((m, n), out_dtype),
        compiler_params=_cparams("parallel", "parallel", "arbitrary"),
    )(slots, *args)


def _gate_mix(ya_pre, s, wpo, wco, proj, d_model, name):
    lp, width = ya_pre.shape
    nb, _, bn = wpo.shape
    ga_off = (proj.shape[1] - 2 * d_model) // bn
    gb_off = (proj.shape[1] - d_model) // bn

    n_strips = 4 if lp % 64 == 0 else 1

    def body(ya_ref, s_ref, wpo_ref, wco_ref, ga_ref, gb_ref, m_ref, y_a_ref, y_b_ref):
        strip = lp // n_strips
        for h in range(n_strips):
            rows = slice(h * strip, (h + 1) * strip)
            y_a = jnp.dot(ya_ref[rows, :], wpo_ref[...], preferred_element_type=F32)
            y_b = jnp.dot(s_ref[rows, :], wco_ref[...], preferred_element_type=F32)
            m = jax.nn.sigmoid(ga_ref[rows, :]) * y_a + jax.nn.sigmoid(gb_ref[rows, :]) * y_b
            m_ref[rows, :] = m.astype(BF16)
            y_a_ref[rows, :] = y_a.astype(BF16)
            y_b_ref[rows, :] = y_b.astype(BF16)

    act_spec = pl.BlockSpec((lp, width), lambda j: (0, 0))
    w_spec = pl.BlockSpec((None, width, bn), lambda j: (j, 0, 0))
    out_spec = pl.BlockSpec((lp, bn), lambda j: (0, j))
    return pl.pallas_call(
        body, name=name, grid=(nb,),
        in_specs=[act_spec, act_spec, w_spec, w_spec,
                  pl.BlockSpec((lp, bn), lambda j: (0, j + ga_off)),
                  pl.BlockSpec((lp, bn), lambda j: (0, j + gb_off))],
        out_specs=[out_spec] * 3,
        out_shape=[jax.ShapeDtypeStruct((lp, nb * bn), BF16)] * 3,
        compiler_params=_cparams("parallel"),
    )(ya_pre, s, wpo, wco, proj, proj)


def _rms_stats(x):
    return lax.rsqrt(jnp.mean(x * x, axis=-1, keepdims=True) + RMS_EPS)


def _rms_bwd(x, g, dy):
    r = _rms_stats(x)
    nrm = x * r
    dn = dy * g
    dx = r * (dn - nrm * jnp.mean(dn * nrm, axis=-1, keepdims=True))
    return dx, dy * nrm


def _rowwise(body, ins, outs, accs, *, lp, name):
    tr = _row_tile(lp, max(a.shape[1] for a in ins))
    n_in, n_out, n_acc = len(ins), len(outs), len(accs)

    def kernel_body(*refs):
        i = pl.program_id(0)
        acc_refs = refs[n_in + n_out:]

        @pl.when(i == 0)
        def _():
            for r in acc_refs:
                r[...] = jnp.zeros_like(r)

        body(i * tr, refs[:n_in], refs[n_in:n_in + n_out], acc_refs)

    in_specs = []
    for a in ins:
        if a.shape[0] == lp:
            in_specs.append(pl.BlockSpec((tr, a.shape[1]), lambda i: (i, 0)))
        else:
            in_specs.append(pl.BlockSpec(a.shape, lambda i: (0, 0)))
    out_specs = [pl.BlockSpec((tr, w), lambda i: (i, 0)) for w, _ in outs]
    out_specs += [pl.BlockSpec((SUBLANES, w), lambda i: (0, 0)) for w in accs]
    out_shape = [jax.ShapeDtypeStruct((lp, w), d) for w, d in outs]
    out_shape += [jax.ShapeDtypeStruct((SUBLANES, w), F32) for w in accs]
    return pl.pallas_call(
        kernel_body, name=name, grid=(lp // tr,), in_specs=in_specs, out_specs=out_specs,
        out_shape=out_shape, compiler_params=_cparams("arbitrary"),
    )(*ins)


SHIFT_TILE = 128


def _shifted_specs(width, n_big, n_small):
    per = SHIFT_TILE // N_META
    small = pl.BlockSpec((N_META, width), lambda i: (jnp.clip(per * i - 1, 0, n_small - 1), 0))
    big = pl.BlockSpec((SHIFT_TILE, width), lambda i: (jnp.minimum(i, n_big - 1), 0))
    return small, big


def _rms_pre(meta_full, x2, g, lp, seq):
    d = x2.shape[1]
    assert seq % SHIFT_TILE == 0 and lp % SHIFT_TILE == 0 and SHIFT_TILE % N_META == 0

    def body(meta_ref, xs_ref, xb_ref, g_ref, h0_ref, u1_ref):
        i = pl.program_id(0)
        head = jnp.where(i == 0, meta_ref[...], xs_ref[...])
        rows = jnp.concatenate([head, xb_ref[:SHIFT_TILE - N_META, :]], axis=0)
        r = i * SHIFT_TILE + lax.broadcasted_iota(jnp.int32, (SHIFT_TILE, 1), 0)
        rows = jnp.where(r < N_META + seq, rows, 0.0)
        h0_ref[...] = rows
        u1_ref[...] = (rows * _rms_stats(rows) * g_ref[...]).astype(BF16)

    small, big = _shifted_specs(d, seq // SHIFT_TILE, seq // N_META)
    tile = pl.BlockSpec((SHIFT_TILE, d), lambda i: (i, 0))
    return pl.pallas_call(
        body, name="rms_pre", grid=(lp // SHIFT_TILE,),
        in_specs=[pl.BlockSpec((N_META, d), lambda i: (0, 0)), small, big, pl.BlockSpec((1, d), lambda i: (0, 0))],
        out_specs=[tile, tile],
        out_shape=[jax.ShapeDtypeStruct((lp, d), F32), jax.ShapeDtypeStruct((lp, d), BF16)],
        compiler_params=_cparams("parallel"),
    )(meta_full, x2, x2, g)


def _post_mix(o, h0, g_post_mix, g_pre_mlp, lp):
    d = h0.shape[1]

    def body(row0, ins, outs, accs):
        o_ref, h0_ref, g1_ref, g2_ref = ins
        o_v = o_ref[...]
        h1 = h0_ref[...] + o_v * _rms_stats(o_v) * g1_ref[...]
        outs[0][...] = h1
        outs[1][...] = (h1 * _rms_stats(h1) * g2_ref[...]).astype(BF16)

    return _rowwise(body, [o, h0, g_post_mix, g_pre_mlp], [(d, F32), (d, BF16)], [], lp=lp, name="post_mix")


def _loss_head(f, h1, target, g_post_mlp, lp, seq):
    d = f.shape[1]

    def body(f_ref, h1_ref, ts_ref, tb_ref, g_ref, df_ref, dh_ref, dg_ref, loss_ref):
        i = pl.program_id(0)

        @pl.when(i == 0)
        def _():
            dg_ref[...] = jnp.zeros_like(dg_ref)
            loss_ref[...] = jnp.zeros_like(loss_ref)

        f_v, g = f_ref[...], g_ref[...]
        r = _rms_stats(f_v)
        nrm = f_v * r
        rows = i * SHIFT_TILE + lax.broadcasted_iota(jnp.int32, (SHIFT_TILE, 1), 0)
        valid = (rows >= N_META) & (rows < N_META + seq)
        tgt = jnp.concatenate([ts_ref[...], tb_ref[:SHIFT_TILE - N_META, :]], axis=0)
        err = jnp.where(valid, h1_ref[...] + nrm * g - tgt, 0.0)
        loss_ref[...] += 0.5 * jnp.sum(jnp.mean(err * err, axis=-1, keepdims=True))
        dy = err * (1.0 / d)
        dn = dy * g
        df_ref[...] = (r * (dn - nrm * jnp.mean(dn * nrm, axis=-1, keepdims=True))).astype(BF16)
        dh_ref[...] = dy
        dg_ref[...] += _rowsum8(dy * nrm)

    small, big = _shifted_specs(d, seq // SHIFT_TILE, seq // N_META)
    tile = pl.BlockSpec((SHIFT_TILE, d), lambda i: (i, 0))
    return pl.pallas_call(
        body, name="loss_head", grid=(lp // SHIFT_TILE,),
        in_specs=[tile, tile, small, big, pl.BlockSpec((1, d), lambda i: (0, 0))],
        out_specs=[tile, tile, pl.BlockSpec((SUBLANES, d), lambda i: (0, 0)),
                   pl.BlockSpec((SUBLANES, LANES), lambda i: (0, 0))],
        out_shape=[jax.ShapeDtypeStruct((lp, d), BF16), jax.ShapeDtypeStruct((lp, d), F32),
                   jax.ShapeDtypeStruct((SUBLANES, d), F32), jax.ShapeDtypeStruct((SUBLANES, LANES), F32)],
        compiler_params=_cparams("arbitrary"),
    )(f, h1, target, target, g_post_mlp)


def _mid_bwd(du2, h1, dh, o, g_pre_mlp, g_post_mix, lp):
    d = h1.shape[1]

    def body(row0, ins, outs, accs):
        du2_ref, h1_ref, dh_ref, o_ref, g2_ref, g1_ref = ins
        dx2, dg2 = _rms_bwd(h1_ref[...], g2_ref[...], du2_ref[...])
        dh1 = dh_ref[...] + dx2
        do, dg1 = _rms_bwd(o_ref[...], g1_ref[...], dh1)
        outs[0][...] = dh1
        outs[1][...] = do.astype(BF16)
        accs[0][...] += _rowsum8(dg2)
        accs[1][...] += _rowsum8(dg1)

    return _rowwise(body, [du2, h1, dh, o, g_pre_mlp, g_post_mix], [(d, F32), (d, BF16)], [d, d], lp=lp,
                    name="mid_bwd")


def _pre_mix_bwd(du1, h0, dh1, g_pre_mix, seq):
    d = h0.shape[1]
    per = SHIFT_TILE // N_META
    assert seq % SHIFT_TILE == 0

    def body(du_b, h_b, dh_b, du_n, h_n, dh_n, du_m, h_m, dh_m, g_ref, gx_ref, gm_ref, dg_ref):
        i = pl.program_id(0)
        g = g_ref[...]

        @pl.when(i == 0)
        def _():
            dx, dg = _rms_bwd(h_m[...], g, du_m[...])
            gm_ref[...] = dh_m[...] + dx
            dg_ref[...] = _rowsum8(dg)

        rows = lambda big, nxt: jnp.concatenate([big[N_META:, :], nxt[...]], axis=0)
        dx, dg = _rms_bwd(rows(h_b, h_n), g, rows(du_b, du_n))
        gx_ref[...] = rows(dh_b, dh_n) + dx
        dg_ref[...] += _rowsum8(dg)

    big = pl.BlockSpec((SHIFT_TILE, d), lambda i: (i, 0))
    nxt = pl.BlockSpec((N_META, d), lambda i: (per * (i + 1), 0))
    first = pl.BlockSpec((N_META, d), lambda i: (0, 0))
    return pl.pallas_call(
        body, name="pre_mix_bwd", grid=(seq // SHIFT_TILE,),
        in_specs=[big] * 3 + [nxt] * 3 + [first] * 3 + [pl.BlockSpec((1, d), lambda i: (0, 0))],
        out_specs=[big, first, pl.BlockSpec((SUBLANES, d), lambda i: (0, 0))],
        out_shape=[jax.ShapeDtypeStruct((seq, d), F32), jax.ShapeDtypeStruct((N_META, d), F32),
                   jax.ShapeDtypeStruct((SUBLANES, d), F32)],
        compiler_params=_cparams("arbitrary"),
    )(du1, h0, dh1, du1, h0, dh1, du1, h0, dh1, g_pre_mix)


def _ln_stats(c):
    mu = jnp.mean(c, axis=-1, keepdims=True)
    var = jnp.mean(jnp.square(c - mu), axis=-1, keepdims=True)
    return mu, lax.rsqrt(var + LN_EPS)


def _ln_silu(c, ln_g, ln_b, lp):
    w = c.shape[1]

    def body(row0, ins, outs, accs):
        c_ref, g_ref, b_ref = ins
        c_v = c_ref[...]
        mu, rstd = _ln_stats(c_v)
        ln = (c_v - mu) * rstd * g_ref[...] + b_ref[...]
        outs[0][...] = (ln * jax.nn.sigmoid(ln)).astype(BF16)

    return _rowwise(body, [c, ln_g, ln_b], [(w, BF16)], [], lp=lp, name="ln_silu")[0]


def _ln_silu_bwd(c, ds, ln_g, ln_b, lp):
    w = c.shape[1]

    def body(row0, ins, outs, accs):
        c_ref, ds_ref, g_ref, b_ref = ins
        c_v, g = c_ref[...], g_ref[...]
        mu, rstd = _ln_stats(c_v)
        nrm = (c_v - mu) * rstd
        ln = nrm * g + b_ref[...]
        sig = jax.nn.sigmoid(ln)
        dln = ds_ref[...] * (sig * (1.0 + ln * (1.0 - sig)))
        dn = dln * g
        dc = rstd * (dn - jnp.mean(dn, axis=-1, keepdims=True) - nrm * jnp.mean(dn * nrm, axis=-1, keepdims=True))
        outs[0][...] = dc
        accs[0][...] += _rowsum8(dln * nrm)
        accs[1][...] += _rowsum8(dln)
        accs[2][...] += _rowsum8(dc)

    return _rowwise(body, [c, ds, ln_g, ln_b], [(w, F32)], [w, w, w], lp=lp, name="ln_silu_bwd")


def _chunk_with_history(ref, i, cols=slice(None)):
    t0 = pl.multiple_of(i * ROW_CHUNK, ROW_CHUNK)
    lo0 = pl.multiple_of(jnp.maximum(t0 - HALO, 0), SUBLANES)
    lo = jnp.where(i > 0, ref[pl.ds(lo0, HALO), cols], 0.0)
    return jnp.concatenate([lo, ref[pl.ds(t0, ROW_CHUNK), cols]], axis=0)


def _chunk_with_future(ref, i, n_chunks, cols=slice(None)):
    t0 = pl.multiple_of(i * ROW_CHUNK, ROW_CHUNK)
    hi0 = pl.multiple_of(jnp.minimum(t0 + ROW_CHUNK, (n_chunks - 1) * ROW_CHUNK), SUBLANES)
    hi = jnp.where(i < n_chunks - 1, ref[pl.ds(hi0, HALO), cols], 0.0)
    return jnp.concatenate([ref[pl.ds(t0, ROW_CHUNK), cols], hi], axis=0)


def _inv_count(t0, n_rows, window):
    pos = t0 + lax.broadcasted_iota(jnp.int32, (n_rows, 1), 0)
    return 1.0 / jnp.minimum(pos + 1, window).astype(F32)


def _pool_delta(z_hist, t0, window):
    s = z_hist
    sh = 1
    while sh < window:
        s = s + pltpu.roll(s, sh, 0)
        sh *= 2
    cur = z_hist[HALO:, :]
    return s[HALO:, :] * _inv_count(t0, ROW_CHUNK, window) - cur


def _pool_fwd(proj, wpg, pool_scale, lp):
    n_grp, gdim, _ = wpg.shape
    width = n_grp * gdim
    n_chunks = lp // ROW_CHUNK

    def body(z_ref, w_ref, sc_ref, out_ref):
        for g, window in enumerate(POOL_WINDOWS):
            cols = slice(g * gdim, (g + 1) * gdim)

            def chunk(i, carry, cols=cols, g=g, window=window):
                t0 = pl.multiple_of(i * ROW_CHUNK, ROW_CHUNK)
                d = _pool_delta(_chunk_with_history(z_ref, i, cols), t0, window)
                q = jnp.dot(d.astype(BF16), w_ref[g], preferred_element_type=F32)
                out_ref[pl.ds(t0, ROW_CHUNK), cols] = (q * sc_ref[:, cols]).astype(BF16)
                return carry

            lax.fori_loop(0, n_chunks, chunk, 0)

    return pl.pallas_call(
        body, name="pool_fwd", grid=(1,),
        in_specs=[pl.BlockSpec((lp, width), lambda i: (0, 0)),
                  pl.BlockSpec(wpg.shape, lambda i: (0, 0, 0)),
                  pl.BlockSpec(pool_scale.shape, lambda i: (0, 0))],
        out_specs=pl.BlockSpec((lp, width), lambda i: (0, 0)),
        out_shape=jax.ShapeDtypeStruct((lp, width), BF16),
        compiler_params=_cparams("arbitrary"),
    )(proj, wpg, pool_scale)


def _pool_bwd(proj, d_ya, wpg, pool_scale, dproj, lp):
    n_grp, gdim, _ = wpg.shape
    width = n_grp * gdim
    n_chunks = lp // ROW_CHUNK
    ext = ROW_CHUNK + HALO

    def body(z_ref, dya_ref, w_ref, sc_ref, _, dz_ref, dw_ref, dsc_ref):
        dw_ref[...] = jnp.zeros_like(dw_ref)
        dsc_ref[...] = jnp.zeros_like(dsc_ref)
        for g, window in enumerate(POOL_WINDOWS):
            cols = slice(g * gdim, (g + 1) * gdim)

            def chunk(i, carry, cols=cols, g=g, window=window):
                t0 = pl.multiple_of(i * ROW_CHUNK, ROW_CHUNK)
                w_g = w_ref[g]
                scale = sc_ref[:, cols]
                d = _pool_delta(_chunk_with_history(z_ref, i, cols), t0, window).astype(BF16)
                dya_ext = _chunk_with_future(dya_ref, i, n_chunks, cols)
                dya = dya_ext[:ROW_CHUNK, :]
                q = jnp.dot(d, w_g, preferred_element_type=F32)
                dsc_ref[:, cols] += _rowsum8(dya * q)
                e_ext = (dya_ext * scale).astype(BF16)
                dw_ref[g] += lax.dot_general(d, e_ext[:ROW_CHUNK, :], (((0,), (0,)), ((), ())),
                                             preferred_element_type=F32)
                dd_ext = lax.dot_general(e_ext, w_g, (((1,), (1,)), ((), ())), preferred_element_type=F32)
                s = dd_ext * _inv_count(t0, ext, window)
                sh = 1
                while sh < window:
                    s = s + pltpu.roll(s, ext - sh, 0)
                    sh *= 2
                dz_ref[pl.ds(t0, ROW_CHUNK), cols] = (s[:ROW_CHUNK, :] - dd_ext[:ROW_CHUNK, :]).astype(BF16)
                return carry

            lax.fori_loop(0, n_chunks, chunk, 0)

    blk = pl.BlockSpec((lp, width), lambda i: (0, 0))
    return pl.pallas_call(
        body, name="pool_bwd", grid=(1,),
        in_specs=[blk, blk, pl.BlockSpec(wpg.shape, lambda i: (0, 0, 0)),
                  pl.BlockSpec(pool_scale.shape, lambda i: (0, 0)), ANY_SPEC],
        out_specs=[blk, pl.BlockSpec(wpg.shape, lambda i: (0, 0, 0)),
                   pl.BlockSpec((SUBLANES, width), lambda i: (0, 0))],
        out_shape=[jax.ShapeDtypeStruct(dproj.shape, BF16), jax.ShapeDtypeStruct(wpg.shape, F32),
                   jax.ShapeDtypeStruct((SUBLANES, width), F32)],
        input_output_aliases={4: 0},
        compiler_params=_cparams("arbitrary"),
    )(proj, d_ya, wpg, pool_scale, dproj)


def _conv_fwd(proj, w_dw, b_dw, lp, width, v_col0):
    n_chunks = lp // ROW_CHUNK
    v_blk0, g_blk0 = v_col0 // LANES, (v_col0 + width) // LANES

    def body(v_ref, gc_ref, w_ref, b_ref, c_ref, a_pad):
        a_pad[pl.ds(0, HALO), :] = jnp.zeros((HALO, LANES), F32)
        a_pad[pl.ds(HALO, lp), :] = v_ref[...] * jax.nn.sigmoid(gc_ref[...])

        def chunk(i, carry):
            t0 = pl.multiple_of(i * ROW_CHUNK, ROW_CHUNK)
            hist = a_pad[pl.ds(t0, ROW_CHUNK + HALO), :]
            acc = jnp.zeros((ROW_CHUNK, LANES), F32)
            for k in range(CONV_KERNEL):
                acc = acc + w_ref[k:k + 1, :] * pltpu.roll(hist, CONV_KERNEL - 1 - k, 0)[HALO:, :]
            c_ref[pl.ds(t0, ROW_CHUNK), :] = acc + b_ref[...]
            return carry

        lax.fori_loop(0, n_chunks, chunk, 0)

    return pl.pallas_call(
        body, name="conv_fwd", grid=(width // LANES,),
        in_specs=[pl.BlockSpec((lp, LANES), lambda j: (0, j + v_blk0)),
                  pl.BlockSpec((lp, LANES), lambda j: (0, j + g_blk0)),
                  pl.BlockSpec((CONV_TAPS_PADDED, LANES), lambda j: (0, j)),
                  pl.BlockSpec((1, LANES), lambda j: (0, j))],
        out_specs=pl.BlockSpec((lp, LANES), lambda j: (0, j)),
        out_shape=jax.ShapeDtypeStruct((lp, width), F32),
        scratch_shapes=[pltpu.VMEM((lp + HALO, LANES), F32)],
        compiler_params=_cparams("parallel"),
    )(proj, proj, w_dw, b_dw)


def _conv_bwd(proj, dc, w_dw, dproj, lp, width, v_col0):
    n_chunks = lp // ROW_CHUNK
    ext = ROW_CHUNK + HALO
    v_blk0, g_blk0 = v_col0 // LANES, (v_col0 + width) // LANES

    def body(v_ref, gc_ref, dc_ref, w_ref, _, dv_ref, dgc_ref, dw_ref, a_pad, dc_pad, dw_acc):
        sig = jax.nn.sigmoid(gc_ref[...])
        a_pad[pl.ds(0, HALO), :] = jnp.zeros((HALO, LANES), F32)
        a_pad[pl.ds(HALO, lp), :] = v_ref[...] * sig
        dc_pad[pl.ds(0, lp), :] = dc_ref[...]
        dc_pad[pl.ds(lp, HALO), :] = jnp.zeros((HALO, LANES), F32)
        dw_acc[...] = jnp.zeros_like(dw_acc)

        def chunk(i, carry):
            t0 = pl.multiple_of(i * ROW_CHUNK, ROW_CHUNK)
            hist = a_pad[pl.ds(t0, ext), :]
            fut = dc_pad[pl.ds(t0, ext), :]
            dc_cur = fut[:ROW_CHUNK, :]
            da = jnp.zeros((ROW_CHUNK, LANES), F32)
            for k in range(CONV_KERNEL):
                lag = CONV_KERNEL - 1 - k
                da = da + w_ref[k:k + 1, :] * pltpu.roll(fut, (ext - lag) % ext, 0)[:ROW_CHUNK, :]
                dw_acc[pl.ds(SUBLANES * k, SUBLANES), :] += _rowsum8(dc_cur * pltpu.roll(hist, lag, 0)[HALO:, :])
            rows = pl.ds(t0, ROW_CHUNK)
            sg = jax.nn.sigmoid(gc_ref[rows, :])
            dv_ref[rows, :] = (da * sg).astype(BF16)
            dgc_ref[rows, :] = (da * v_ref[rows, :] * sg * (1.0 - sg)).astype(BF16)
            return carry

        lax.fori_loop(0, n_chunks, chunk, 0)
        dw_ref[...] = dw_acc[...].reshape(CONV_TAPS_PADDED, SUBLANES, LANES).sum(axis=1)

    col = lambda j: (0, j)
    return pl.pallas_call(
        body, name="conv_bwd", grid=(width // LANES,),
        in_specs=[pl.BlockSpec((lp, LANES), lambda j: (0, j + v_blk0)),
                  pl.BlockSpec((lp, LANES), lambda j: (0, j + g_blk0)),
                  pl.BlockSpec((lp, LANES), col),
                  pl.BlockSpec((CONV_TAPS_PADDED, LANES), col), ANY_SPEC],
        out_specs=[pl.BlockSpec((lp, LANES), lambda j: (0, j + v_blk0)), pl.BlockSpec((lp, LANES), col),
                   pl.BlockSpec((CONV_TAPS_PADDED, LANES), col)],
        out_shape=[jax.ShapeDtypeStruct(dproj.shape, BF16), jax.ShapeDtypeStruct((lp, width), BF16),
                   jax.ShapeDtypeStruct((CONV_TAPS_PADDED, width), F32)],
        scratch_shapes=[pltpu.VMEM((lp + HALO, LANES), F32), pltpu.VMEM((lp + HALO, LANES), F32),
                        pltpu.VMEM((CONV_TAPS_PADDED * SUBLANES, LANES), F32)],
        input_output_aliases={4: 0},
        compiler_params=_cparams("parallel"),
    )(proj, proj, dc, w_dw, dproj)


def _place_columns(dst, pieces, name):
    m = dst.shape[0]
    tile = 512
    counts = [p.shape[1] // tile for p, _ in pieces]
    starts = [sum(counts[:i]) for i in range(len(pieces))]
    n_steps = sum(counts)

    def local(s, i):
        return jnp.clip(s - starts[i], 0, counts[i] - 1)

    def out_index(s):
        blk = pieces[0][1] // tile + local(s, 0)
        for i in range(1, len(pieces)):
            blk = jnp.where(s >= starts[i], pieces[i][1] // tile + local(s, i), blk)
        return 0, blk

    def body(*refs):
        out_ref = refs[-1]
        s = pl.program_id(0)
        for i in range(len(pieces)):
            @pl.when((s >= starts[i]) & (s < starts[i] + counts[i]))
            def _(i=i):
                out_ref[...] = refs[i][...]

    return pl.pallas_call(
        body, name=name, grid=(n_steps,),
        in_specs=[pl.BlockSpec((m, tile), functools.partial(lambda s, i: (0, local(s, i)), i=i))
                  for i in range(len(pieces))] + [ANY_SPEC],
        out_specs=pl.BlockSpec((m, tile), out_index),
        out_shape=jax.ShapeDtypeStruct(dst.shape, dst.dtype),
        input_output_aliases={len(pieces): 0},
        compiler_params=_cparams("arbitrary"),
    )(*[p for p, _ in pieces], dst)


def _adamw_math(w, g, m, v):
    m = ADAM_B1 * m + (1.0 - ADAM_B1) * g
    v = ADAM_B2 * v + (1.0 - ADAM_B2) * jnp.square(g)
    m_hat = m / (1.0 - ADAM_B1 ** ADAM_STEP)
    v_hat = v / (1.0 - ADAM_B2 ** ADAM_STEP)
    delta = -ADAM_LR * (m_hat / (jnp.sqrt(v_hat) + ADAM_EPS) + ADAM_WD * w)
    return delta, m, v


def _pair_sum(own, recv, where, name):
    _, _, rows, cols = own.shape
    tr = _row_tile(rows, cols, 1024 * 1024)

    def body(where_ref, own_ref, recv_ref, out_ref):
        out_ref[...] = (own_ref[...].astype(F32) + recv_ref[...].astype(F32)).astype(BF16)

    return pl.pallas_call(
        body, name=name,
        grid_spec=pltpu.PrefetchScalarGridSpec(
            num_scalar_prefetch=1, grid=(3, rows // tr),
            in_specs=[pl.BlockSpec((None, None, tr, cols), lambda r, i, wh: (wh[2 + r], wh[0], i, 0)),
                      pl.BlockSpec((None, tr, cols), lambda r, i, wh: (wh[2 + r], i, 0))],
            out_specs=pl.BlockSpec((None, tr, cols), lambda r, i, wh: (wh[2 + r], i, 0))),
        out_shape=jax.ShapeDtypeStruct((4, rows, cols), BF16),
        compiler_params=_cparams("parallel", "parallel"),
    )(where, own, recv)


def _adamw_big(w, m, v, own, from_sibling, recv3, where, name):
    rows, cols = w.shape
    tr = _row_tile(rows, cols, 256 * 1024)

    def body(where_ref, w_ref, m_ref, v_ref, own_ref, sib_ref, r_ref, g_out, d_out, m_out, v_out):
        g = own_ref[...].astype(F32) + sib_ref[...].astype(F32)
        for r in range(3):
            g = g + r_ref[r].astype(F32)
        delta, m_new, v_new = _adamw_math(w_ref[...], g, m_ref[...], v_ref[...])
        g_out[...] = g
        d_out[...] = delta
        m_out[...] = m_new
        v_out[...] = v_new

    blk = pl.BlockSpec((tr, cols), lambda i, q_ref: (i, 0))
    return pl.pallas_call(
        body, name=name,
        grid_spec=pltpu.PrefetchScalarGridSpec(
            num_scalar_prefetch=1, grid=(rows // tr,),
            in_specs=[blk, blk, blk,
                      pl.BlockSpec((None, None, tr, cols), lambda i, wh: (wh[1], wh[0], i, 0)),
                      pl.BlockSpec((None, tr, cols), lambda i, wh: (wh[1], i, 0)),
                      pl.BlockSpec((3, tr, cols), lambda i, wh: (0, i, 0))],
            out_specs=[blk] * 4),
        out_shape=[jax.ShapeDtypeStruct((rows, cols), F32)] * 4,
        compiler_params=_cparams("parallel"),
    )(where, w, m, v, own, from_sibling, recv3)


def _small_update(me_idx, packed, rep_params, rep_places, meta_wmv, meta_row0, wdw_wmv, wdw_row0, loss_row0):
    n_rep = len(rep_params)
    meta_rows, meta_cols = meta_wmv[0].shape
    wdw_rows, wdw_cols = wdw_wmv[0].shape

    def body(me_ref, *refs):
        pos = 0

        def take(k):
            nonlocal pos
            out = refs[pos:pos + k]
            pos += k
            return out

        rep_in = [take(3) for _ in range(n_rep)]
        rep_g = take(n_rep)
        meta_in, (meta_g,) = take(3), take(1)
        wdw_in, (wdw_g,) = take(3), take(1)
        (loss_ref,) = take(1)
        rep_out = [take(4) for _ in range(n_rep)]
        meta_out, wdw_out = take(4), take(4)
        (loss_out,) = take(1)

        def update(wmv, g, outs):
            delta, m_new, v_new = _adamw_math(wmv[0][...], g, wmv[1][...], wmv[2][...])
            for o_ref, val in zip(outs, (g, delta, m_new, v_new)):
                o_ref[...] = val

        for wmv, g_ref, outs in zip(rep_in, rep_g, rep_out):
            g = jnp.sum(g_ref[0], axis=0, keepdims=True)
            for j in range(1, N_DEV):
                g = g + jnp.sum(g_ref[j], axis=0, keepdims=True)
            update(wmv, g, outs)
        for wmv, g_ref, outs in ((meta_in, meta_g, meta_out), (wdw_in, wdw_g, wdw_out)):
            g = g_ref[0]
            for j in range(1, N_DEV):
                g = g + g_ref[j]
            update(wmv, g, outs)
        total = loss_ref[0]
        for j in range(1, N_DEV):
            total = total + loss_ref[j]
        loss_out[...] = total

    def whole(a):
        nd = a.ndim
        return pl.BlockSpec(a.shape, lambda i, me_ref, nd=nd: (0,) * nd)

    ins, in_specs = [], []
    for wmv in rep_params:
        ins += list(wmv)
        in_specs += [whole(a) for a in wmv]
    for wmv, (row0, col0) in zip(rep_params, rep_places):
        width = wmv[0].shape[1]
        ins.append(packed)
        in_specs.append(pl.BlockSpec((N_DEV, SUBLANES, width),
                                     lambda i, me_ref, rb=row0 // SUBLANES, cb=col0 // width: (0, rb, cb)))
    ins += list(meta_wmv) + [packed]
    in_specs += [whole(a) for a in meta_wmv]
    in_specs.append(pl.BlockSpec((N_DEV, meta_rows, meta_cols),
                                 lambda i, me_ref, rb=meta_row0 // meta_rows: (0, rb, me_ref[0])))
    ins += list(wdw_wmv) + [packed]
    in_specs += [whole(a) for a in wdw_wmv]
    in_specs.append(pl.BlockSpec((N_DEV, wdw_rows, wdw_cols),
                                 lambda i, me_ref, rb=wdw_row0 // wdw_rows: (0, rb, me_ref[0])))
    ins.append(packed)
    in_specs.append(pl.BlockSpec((N_DEV, SUBLANES, LANES), lambda i, me_ref, rb=loss_row0 // SUBLANES: (0, rb, 0)))

    out_shape, out_specs = [], []
    for wmv in list(rep_params) + [meta_wmv, wdw_wmv]:
        out_shape += [jax.ShapeDtypeStruct(wmv[0].shape, F32)] * 4
        out_specs += [whole(wmv[0])] * 4
    out_shape.append(jax.ShapeDtypeStruct((SUBLANES, LANES), F32))
    out_specs.append(pl.BlockSpec((SUBLANES, LANES), lambda i, me_ref: (0, 0)))

    outs = pl.pallas_call(
        body, name="small_update",
        grid_spec=pltpu.PrefetchScalarGridSpec(num_scalar_prefetch=1, grid=(1,), in_specs=in_specs,
                                               out_specs=out_specs),
        out_shape=out_shape, compiler_params=_cparams("arbitrary"),
    )(me_idx, *ins)
    groups = [outs[4 * i:4 * i + 4] for i in range(n_rep + 2)]
    return groups[:n_rep], groups[n_rep], groups[n_rep + 1], outs[-1]


def kernel(x, meta, g_pre_mix, w_in, w_pool_grp, pool_scale, w_pool_out, w_dw, b_dw, conv_ln_g, conv_ln_b, w_conv_out, w_o, g_post_mix, g_pre_mlp, w_up, w_down, g_post_mlp, loss_target, m_meta, m_g_pre_mix, m_w_in, m_w_pool_grp, m_pool_scale, m_w_pool_out, m_w_dw, m_b_dw, m_conv_ln_g, m_conv_ln_b, m_w_conv_out, m_w_o, m_g_post_mix, m_g_pre_mlp, m_w_up, m_w_down, m_g_post_mlp, v_meta, v_g_pre_mix, v_w_in, v_w_pool_grp, v_pool_scale, v_w_pool_out, v_w_dw, v_b_dw, v_conv_ln_g, v_conv_ln_b, v_w_conv_out, v_w_o, v_g_post_mix, v_g_pre_mlp, v_w_up, v_w_down, v_g_post_mlp):
    seq, d = x.shape[1], x.shape[2]
    pool_w = pool_scale.shape[1]
    conv_w = b_dw.shape[1]
    n_grp, grp_rows, gdim = w_pool_grp.shape[1:]
    lp = _round_up(N_META + seq, ROW_CHUNK)
    tm_half = lp // 2 if (lp // 2) % 16 == 0 else lp
    c_idx = lax.axis_index("c").astype(jnp.int32)
    chip_idx = (2 * lax.axis_index("x") + lax.axis_index("y")).astype(jnp.int32)
    me_idx = 2 * chip_idx + c_idx

    pad_taps = ((0, CONV_TAPS_PADDED - CONV_KERNEL), (0, 0))
    big = dict(w_in=w_in[0], w_pool_grp=w_pool_grp[0].reshape(n_grp * grp_rows, gdim), w_pool_out=w_pool_out[0],
               w_conv_out=w_conv_out[0], w_o=w_o[0], w_up=w_up[0], w_down=w_down[0])
    big_names = list(big)
    moments = dict(w_in=(m_w_in, v_w_in), w_pool_grp=(m_w_pool_grp, v_w_pool_grp), w_pool_out=(m_w_pool_out, v_w_pool_out),
                   w_conv_out=(m_w_conv_out, v_w_conv_out), w_o=(m_w_o, v_w_o), w_up=(m_w_up, v_w_up),
                   w_down=(m_w_down, v_w_down))
    slot_idx = me_idx.reshape(1)
    sources = dict(big, meta=meta, w_dw=jnp.pad(w_dw[0], pad_taps))

    def fill(k, after):
        return _fill_slot(sources[k], slot_idx, BF16 if k in big else F32, after, "fill_" + k)

    gather_groups = [["meta", "w_dw"], ["w_in"], ["w_pool_grp", "w_pool_out", "w_conv_out", "w_o"], ["w_up"], ["w_down"]]
    started, token = _gather_start([[fill(k, slot_idx) for k in names] for names in gather_groups[:2]], slot_idx,
                                   "gather_start_first", issue_order=(0, 3, 1, 2))
    started_rest, _ = _gather_start([[fill(k, token) for k in names] for names in gather_groups[2:]], token,
                                    "gather_start_rest")
    started += started_rest
    wg = {}
    x_idx, y_idx = lax.axis_index("x"), lax.axis_index("y")
    at = lambda px, py, pc: 4 * px + 2 * py + pc
    near_slots = jnp.stack([at(x_idx, y_idx, c_idx), at(x_idx, y_idx, 1 - c_idx), at(1 - x_idx, y_idx, c_idx),
                            at(x_idx, 1 - y_idx, c_idx), at(1 - x_idx, y_idx, 1 - c_idx),
                            at(x_idx, 1 - y_idx, 1 - c_idx)]).astype(jnp.int32)
    far_slots = jnp.stack([at(1 - x_idx, 1 - y_idx, c_idx), at(1 - x_idx, 1 - y_idx, 1 - c_idx)]).astype(jnp.int32)

    def gather_whole(gi, after_forward, after_finish):
        send, recv, lands = started[gi]
        fs, fr, lands = _gather_forward(lands, recv, after_forward(), f"gather_forward_{gi}")
        lands = _gather_finish(lands, send, recv, [(ALL_CHIPS, fs, fr)], [(ALL_CHIPS, fs, fr)], after_finish(),
                               f"gather_finish_{gi}")
        wg.update(zip(gather_groups[gi], lands))

    near_state = {}

    def gather_near(gi, after):
        send, recv, lands = started[gi]
        fs, fr, lands = _gather_forward(lands, recv, after, f"gather_forward_near_{gi}", which=NEAR)
        lands = _gather_finish(lands, send, recv, [(NEAR, fs, fr)], [], after, f"gather_finish_near_{gi}",
                               direct_sends=False)
        near_state[gi] = (fs, fr)
        return lands

    def gather_far(gi, lands, after):
        send, recv, _ = started[gi]
        fs, fr, lands = _gather_forward(lands, recv, after, f"gather_forward_far_{gi}", which=FAR)
        lands = _gather_finish(lands, send, recv, [(FAR, fs, fr)], [(NEAR,) + near_state[gi], (FAR, fs, fr)], after,
                               f"gather_finish_far_{gi}", own=False)
        wg.update(zip(gather_groups[gi], lands))
        return lands

    gather_whole(0, lambda: token, lambda: token)
    meta_full = wg["meta"].transpose(1, 0, 2).reshape(N_META, d)
    wdw_full = wg["w_dw"].transpose(1, 0, 2).reshape(CONV_TAPS_PADDED, conv_w)
    target = loss_target[0]
    h0, u1 = _rms_pre(meta_full, x[0], g_pre_mix, lp, seq)
    send, recv, w_in_lands = started[1]
    w_in_lands = _gather_finish(w_in_lands, send, recv, [], [], u1, "gather_finish_home_1", direct_sends=False)
    proj = _mm_slots(u1, w_in_lands[0], near_slots[:2], over="n", tm=tm_half, name="mm_proj_home")
    fs_far, fr_far, w_in_lands = _gather_forward(w_in_lands, recv, proj, "gather_forward_far_1", which=FAR)
    w_in_lands = _gather_finish(w_in_lands, send, recv, [(FAR, fs_far, fr_far)], [], proj, "gather_finish_far_1",
                                own=False, direct_sends=False)
    proj = _mm_slots(u1, w_in_lands[0], far_slots, over="n", tm=tm_half, base=proj, name="mm_proj_far")
    fs_near, fr_near, w_in_lands = _gather_forward(w_in_lands, recv, proj, "gather_forward_near_1", which=NEAR)
    w_in_lands = _gather_finish(w_in_lands, send, recv, [(NEAR, fs_near, fr_near)],
                                [(FAR, fs_far, fr_far), (NEAR, fs_near, fr_near)], proj, "gather_finish_near_1",
                                own=False)
    proj = _mm_slots(u1, w_in_lands[0], near_slots[2:], over="n", tm=tm_half, base=proj, name="mm_proj_near")
    wg["w_in"] = w_in_lands[0]
    conv_c = _conv_fwd(proj, wdw_full, b_dw, lp, conv_w, pool_w)
    s_act = _ln_silu(conv_c, conv_ln_g, conv_ln_b, lp)
    gather_whole(2, lambda: proj, lambda: s_act)
    wpg_full = wg["w_pool_grp"].reshape(N_DEV, n_grp, grp_rows, gdim).transpose(1, 0, 2, 3).reshape(n_grp, gdim, gdim)
    w_o_full = wg["w_o"].reshape(d, d)
    ya_pre = _pool_fwd(proj, wpg_full, pool_scale, lp)
    m_mix, y_a, y_b = _gate_mix(ya_pre, s_act, wg["w_pool_out"], wg["w_conv_out"], proj, d, "gate_mix")
    o = _mm(m_mix, w_o_full, mode="nn", tm=tm_half, tn=512, name="mm_o")
    h1, u2 = _post_mix(o, h0, g_post_mix, g_pre_mlp, lp)
    relu2 = lambda acc: jnp.square(jnp.maximum(acc, 0.0))
    (w_up_near,) = gather_near(3, u2)
    act = _mm_slots(u2, w_up_near, near_slots, over="n", tm=tm_half, out_dtype=BF16, epilogue=relu2, name="mm_up_near")
    (w_up_all,) = gather_far(3, [w_up_near], act)
    act = _mm_slots(u2, w_up_all, far_slots, over="n", tm=tm_half, out_dtype=BF16, epilogue=relu2, base=act,
                    name="mm_up_far")
    (w_down_near,) = gather_near(4, act)
    f = _mm_slots(act, w_down_near, near_slots, over="k", tm=tm_half, tn=d, name="mm_down_near")
    (w_down_all,) = gather_far(4, [w_down_near], f)
    f = _mm_slots(act, w_down_all, far_slots, over="k", tm=tm_half // 2, tn=d, base=f, name="mm_down_far")
    w_down_full = w_down_all.reshape(-1, d)

    big_out = {}

    def to_sibling(names, grads, after, tag):
        send, recv, grads, lands, token = _pair_exchange_start(grads, after, "grads_to_sibling_start_" + tag)
        return (names, send, recv, grads, lands, tag), token

    where = jnp.stack([c_idx, chip_idx, 2 * (1 - x_idx) + y_idx, 2 * x_idx + (1 - y_idx),
                       2 * (1 - x_idx) + (1 - y_idx)]).astype(jnp.int32)

    def to_owner(handle, after):
        names, send, recv, grads, from_sib, tag = handle
        grads, from_sib = _pair_exchange_finish(grads, from_sib, send, recv, after, "grads_to_sibling_finish_" + tag)
        own = [g.reshape((4, 2) + g.shape[1:]) for g in grads]
        sums = [_pair_sum(o, r, where, "pair_sum_" + k) for k, o, r in zip(names, own, from_sib)]
        send, recv, sums, lands, token = _chip_exchange_start(sums, after, "grads_to_owner_start_" + tag)
        return (names, send, recv, sums, lands, own, from_sib, tag), token

    def update(handle, after):
        names, send, recv, sums, lands, own, from_sib, tag = handle
        got = _chip_exchange_finish(sums, lands, send, recv, after, "grads_to_owner_finish_" + tag)
        for k, o, s, r3 in zip(names, own, from_sib, got):
            w2 = big[k]
            shape = moments[k][0].shape
            outs = _adamw_big(w2, moments[k][0].reshape(w2.shape), moments[k][1].reshape(w2.shape), o, s, r3,
                              where, "adamw_" + k)
            big_out[k] = [a.reshape(shape) for a in outs]
        return big_out[names[-1]][0]

    df, dh, dg_post_mlp, loss_part = _loss_head(f, h1, target, g_post_mlp, lp, seq)
    d_up = _mm(df, w_down_full, mode="nt", tm=tm_half, tn=1024, out_dtypes=(BF16,), extras=[(act, 0)],
               epilogue=lambda acc, a: (acc * (2.0 * jnp.sqrt(a.astype(F32))),), row_splits=2, name="mm_d_up")
    g_w_down = _mm(act, df, mode="tn", tm=1024, tn=1024, out_dtypes=(BF16,), name="mm_g_down")
    sib_down, token = to_sibling(["w_down"], [g_w_down.reshape(N_DEV, -1, d)], slot_idx, "down")
    g_w_up = _mm(u2, d_up, mode="tn", tm=1024, tn=wg["w_up"].shape[2], out_blocked=True, out_dtypes=(BF16,),
                 after=token, name="mm_g_up")
    sib_up, token = to_sibling(["w_up"], [g_w_up], slot_idx, "up")
    pending_down, token = to_owner(sib_down, token)
    du2 = _mm(d_up, wg["w_up"], mode="nt", tm=tm_half // 2, tn=1024, b_blocked=True, kb=4, after=token, name="mm_du2")
    pending_up, token = to_owner(sib_up, du2)
    dh1, do, dg_pre_mlp, dg_post_mix = _mid_bwd(du2, h1, dh, o, g_pre_mlp, g_post_mix, lp)

    def gate_bwd(dm, ga, gb, ya, yb):
        sa, sb = jax.nn.sigmoid(ga), jax.nn.sigmoid(gb)
        return (dm * ya.astype(F32) * sa * (1.0 - sa), dm * yb.astype(F32) * sb * (1.0 - sb), dm * sa, dm * sb)

    gate_tn = 512
    ga_col0, gb_col0 = proj.shape[1] - 2 * d, proj.shape[1] - d
    dproj, d_gb, d_ya, d_yb = _mm(
        do, w_o_full, mode="nt", tm=tm_half, tn=gate_tn, out_dtypes=(BF16,) * 4,
        extras=[(proj, ga_col0 // gate_tn), (proj, gb_col0 // gate_tn), (y_a, 0), (y_b, 0)], epilogue=gate_bwd,
        out_places=((proj.shape[1], ga_col0), None, None, None), after=token, row_splits=2, name="mm_dm")
    g_w_o = _mm(m_mix, do, mode="tn", tm=1024, tn=1024, out_dtypes=(BF16,), name="mm_g_o")
    bn_out = wg["w_pool_out"].shape[2]
    g_w_pool_out = _mm(ya_pre, d_ya, mode="tn", tm=pool_w, tn=4 * bn_out, out_blocked=True, out_block=bn_out,
                       out_dtypes=(BF16,), name="mm_g_pool_out")
    g_w_conv_out = _mm(s_act, d_yb, mode="tn", tm=conv_w, tn=4 * bn_out, out_blocked=True, out_block=bn_out,
                       out_dtypes=(BF16,), name="mm_g_conv_out")
    sib_mix, token = to_sibling(["w_o", "w_pool_out", "w_conv_out"],
                                [g_w_o.reshape(N_DEV, -1, d), g_w_pool_out, g_w_conv_out], slot_idx, "mix")
    d_ya_pre = _mm(d_ya, wg["w_pool_out"], mode="nt", tm=tm_half, tn=pool_w, b_blocked=True, kb=4, after=token,
                   name="mm_d_ya_pre")
    d_s = _mm(d_yb, wg["w_conv_out"], mode="nt", tm=tm_half, tn=conv_w, b_blocked=True, kb=4, name="mm_d_s")
    pending_mix, token = to_owner(sib_mix, d_s)
    dproj, g_wpg, d_scale = _pool_bwd(proj, d_ya_pre, wpg_full, pool_scale, dproj, lp)
    dc, d_ln_g, d_ln_b, d_b_dw = _ln_silu_bwd(conv_c, d_s, conv_ln_g, conv_ln_b, lp)
    dproj, dgc, g_wdw = _conv_bwd(proj, dc, wdw_full, dproj, lp, conv_w, pool_w)
    dproj = _place_columns(dproj, [(dgc, pool_w + conv_w), (d_gb, gb_col0)], "place_dproj")
    g_w_in = _mm(u1, dproj, mode="tn", tm=1024, tn=wg["w_in"].shape[2], out_blocked=True, out_dtypes=(BF16,),
                 after=token, name="mm_g_in")
    g_wpg_slots = g_wpg.astype(BF16).reshape(n_grp, N_DEV, grp_rows, gdim).transpose(1, 0, 2, 3)
    sib_in, token = to_sibling(["w_pool_grp", "w_in"],
                               [g_wpg_slots.reshape(N_DEV, n_grp * grp_rows, gdim), g_w_in], slot_idx, "in")
    done = update(pending_down, token)
    pending_in, token = to_owner(sib_in, done)
    done = update(pending_up, token)
    du1 = _mm(dproj, wg["w_in"], mode="nt", tm=tm_half // 2, tn=1024, b_blocked=True, kb=4, after=done, name="mm_du1")
    grad_x2, grad_meta_part, dg_pre_mix = _pre_mix_bwd(du1, h0, dh1, g_pre_mix, seq)
    grad_x = grad_x2[None]

    assert pool_w + conv_w == d and conv_w <= d and LANES <= d
    widen = lambda a: jnp.pad(a, ((0, 0), (0, d - a.shape[1])))
    packed = jnp.concatenate([
        dg_pre_mix, dg_post_mix, dg_pre_mlp, dg_post_mlp,
        jnp.concatenate([d_scale, d_ln_g], axis=1), jnp.concatenate([d_ln_b, d_b_dw], axis=1),
        grad_meta_part, widen(g_wdw), widen(loss_part)], axis=0)
    rep = dict(g_pre_mix=((g_pre_mix, m_g_pre_mix, v_g_pre_mix), (0, 0)),
               g_post_mix=((g_post_mix, m_g_post_mix, v_g_post_mix), (SUBLANES, 0)),
               g_pre_mlp=((g_pre_mlp, m_g_pre_mlp, v_g_pre_mlp), (2 * SUBLANES, 0)),
               g_post_mlp=((g_post_mlp, m_g_post_mlp, v_g_post_mlp), (3 * SUBLANES, 0)),
               pool_scale=((pool_scale, m_pool_scale, v_pool_scale), (4 * SUBLANES, 0)),
               conv_ln_g=((conv_ln_g, m_conv_ln_g, v_conv_ln_g), (4 * SUBLANES, pool_w)),
               conv_ln_b=((conv_ln_b, m_conv_ln_b, v_conv_ln_b), (5 * SUBLANES, 0)),
               b_dw=((b_dw, m_b_dw, v_b_dw), (5 * SUBLANES, conv_w)))
    meta_row0 = 6 * SUBLANES
    wdw_row0 = meta_row0 + N_META
    loss_row0 = wdw_row0 + CONV_TAPS_PADDED
    (small_started,), token = _gather_start([[_fill_slot(packed, slot_idx, F32, slot_idx, "fill_small")]], grad_x2,
                                            "gather_small_start")
    done = update(pending_mix, token)
    done = update(pending_in, done)
    send, recv, lands = small_started
    fsend, frecv, lands = _gather_forward(lands, recv, done, "gather_small_forward")
    (packed_all,) = _gather_finish(lands, send, recv, [(ALL_CHIPS, fsend, frecv)], [(ALL_CHIPS, fsend, frecv)], done,
                                   "gather_small_finish")
    rep_names = list(rep)
    wdw_wmv = [jnp.pad(a[0], pad_taps) for a in (w_dw, m_w_dw, v_w_dw)]
    rep_out, meta_out, wdw_out, loss_blk = _small_update(
        slot_idx, packed_all, [rep[k][0] for k in rep_names], [rep[k][1] for k in rep_names],
        (meta, m_meta, v_meta), meta_row0, wdw_wmv, wdw_row0, loss_row0)
    small_out = dict(zip(rep_names, rep_out))
    small_out["meta"] = meta_out
    small_out["w_dw"] = [a[:CONV_KERNEL][None] for a in wdw_out]

    order = ["meta", "g_pre_mix", "w_in", "w_pool_grp", "pool_scale", "w_pool_out", "w_dw", "b_dw", "conv_ln_g",
             "conv_ln_b", "w_conv_out", "w_o", "g_post_mix", "g_pre_mlp", "w_up", "w_down", "g_post_mlp"]
    by_name = {**big_out, **small_out}
    result = [loss_blk[0, 0], grad_x]
    for kind in range(4):
        result += [by_name[k][kind] for k in order]
    return tuple(result)
```

```python
import functools

import jax
import jax.numpy as jnp
from jax import lax
from jax.experimental import pallas as pl
from jax.experimental.pallas import tpu as pltpu

F32 = jnp.float32
BF16 = jnp.bfloat16
MESH = pl.DeviceIdType.MESH

N_DEV = 8
N_META = 16
POOL_WINDOWS = (2, 4, 8, 16)
CONV_KERNEL = 31
CONV_TAPS_PADDED = 32
RMS_EPS = 1e-6
LN_EPS = 1e-5
ADAM_LR = 0.001
ADAM_B1 = 0.9
ADAM_B2 = 0.999
ADAM_EPS = 1e-08
ADAM_WD = 0.01
ADAM_STEP = 10

LANES = 128
SUBLANES = 8
ROW_CHUNK = 128
HALO = 32
VMEM_LIMIT_BYTES = 56 * 1024 * 1024


def _cparams(*sem):
    return pltpu.CompilerParams(dimension_semantics=sem if sem else None, vmem_limit_bytes=VMEM_LIMIT_BYTES)


def _round_up(n, m):
    return (n + m - 1) // m * m


def _row_tile(rows, cols, max_elems=640 * 1024):
    best = None
    for t in range(16, rows + 1, 16):
        if rows % t == 0 and (best is None or t * cols <= max_elems):
            best = t
    assert best is not None, (rows, cols)
    return best


def _rowsum8(a):
    t, w = a.shape
    return a.reshape(t // SUBLANES, SUBLANES, w).sum(axis=0)


def _mesh_pos():
    return lax.axis_index("x"), lax.axis_index("y"), lax.axis_index("c")


HBM_SPEC = pl.BlockSpec(memory_space=pltpu.HBM)
SEM_SPEC = pl.BlockSpec(memory_space=pltpu.SEMAPHORE)
ANY_SPEC = pl.BlockSpec(memory_space=pl.ANY)
_DATAFLOW = pltpu.SideEffectType.DATAFLOW_SIDE_EFFECTING


SIBLING_BARRIER_ID = 1
BARRIER_IDS = {name: 2 + i for i, name in enumerate(
    ["gather_first", "gather_rest", "gather_small", "owner_down", "owner_up", "owner_mix", "owner_in"])}


def _sibling_handshake():
    x, y, c = _mesh_pos()
    barrier = pltpu.get_barrier_semaphore()
    pl.semaphore_signal(barrier, inc=1, device_id=(x, y, 1 - c), device_id_type=MESH)
    pl.semaphore_wait(barrier, 1)


def _peer_handshake(peers):
    barrier = pltpu.get_barrier_semaphore()
    for peer in peers:
        pl.semaphore_signal(barrier, inc=1, device_id=peer, device_id_type=MESH)
    pl.semaphore_wait(barrier, len(peers))


def _hbm(a):
    return pltpu.with_memory_space_constraint(a, pltpu.HBM)


def _slot(p):
    return 4 * p[0] + 2 * p[1] + p[2]


def _fill_slot(w, slot_idx, dtype, after, name):
    rows, cols = w.shape
    tr = _row_tile(rows, cols) if rows % 16 == 0 else rows

    def body(idx_ref, w_ref, _, out_ref):
        out_ref[...] = w_ref[...].astype(dtype)

    return pl.pallas_call(
        body, name=name,
        grid_spec=pltpu.PrefetchScalarGridSpec(
            num_scalar_prefetch=1, grid=(rows // tr,),
            in_specs=[pl.BlockSpec((tr, cols), lambda i, idx_ref: (i, 0)), ANY_SPEC],
            out_specs=pl.BlockSpec((None, tr, cols), lambda i, idx_ref: (idx_ref[0], i, 0))),
        out_shape=jax.ShapeDtypeStruct((N_DEV, rows, cols), dtype),
        compiler_params=_cparams("parallel"),
    )(slot_idx, w, after)


def _gather_start(groups, after, name, barrier_id, issue_order=(0, 1, 2, 3)):
    flat = [g for grp in groups for g in grp]
    n, n_grp = len(flat), len(groups)

    def body(*refs):
        lands = refs[:n]
        sems = refs[n + 1:n + 1 + 2 * n_grp]
        token = refs[-1]
        x, y, c = _mesh_pos()
        targets = [(x, y, 1 - c), (1 - x, y, c), (x, 1 - y, c), (1 - x, 1 - y, c)]
        _peer_handshake(targets)
        t = 0
        for gi, grp in enumerate(groups):
            for ti in range(len(grp)):
                mine = lands[t].at[_slot((x, y, c))]
                for k in issue_order:
                    pltpu.make_async_remote_copy(
                        src_ref=mine, dst_ref=mine,
                        send_sem=sems[2 * gi].at[4 * ti + k], recv_sem=sems[2 * gi + 1].at[4 * ti + k],
                        device_id=targets[k], device_id_type=MESH).start()
                t += 1
        token[...] = jnp.zeros_like(token)

    sem_shapes = []
    for grp in groups:
        sem_shapes += [pltpu.SemaphoreType.DMA((4 * len(grp),))] * 2
    outs = pl.pallas_call(
        body, name=name,
        out_shape=tuple(sem_shapes + [pltpu.HBM(g.shape, g.dtype) for g in flat]
                        + [jax.ShapeDtypeStruct((SUBLANES, LANES), F32)]),
        in_specs=tuple([HBM_SPEC] * n + [ANY_SPEC]),
        out_specs=tuple([SEM_SPEC] * (2 * n_grp) + [HBM_SPEC] * n + [pl.BlockSpec(memory_space=pltpu.VMEM)]),
        input_output_aliases={i: 2 * n_grp + i for i in range(n)},
        compiler_params=pltpu.CompilerParams(has_side_effects=_DATAFLOW, collective_id=barrier_id),
    )(*[_hbm(g) for g in flat], after)
    sems, lands, token = outs[:2 * n_grp], outs[2 * n_grp:-1], outs[-1]
    res, t = [], 0
    for gi, grp in enumerate(groups):
        res.append((sems[2 * gi], sems[2 * gi + 1], list(lands[t:t + len(grp)])))
        t += len(grp)
    return res, token


NEAR, FAR, ALL_CHIPS = (0, 1), (2,), (0, 1, 2)


def _gather_forward(lands, recv_sems, after, name, which=ALL_CHIPS):
    n, nw = len(lands), len(which)

    def body(*refs):
        land_refs, recv, _ = refs[:n], refs[n], refs[n + 1]
        fsend, frecv = refs[n + 2], refs[n + 3]
        x, y, c = _mesh_pos()
        chips = [(1 - x, y), (x, 1 - y), (1 - x, 1 - y)]
        _sibling_handshake()
        for t in range(n):
            for i, j in enumerate(which):
                blk = land_refs[t].at[_slot((*chips[j], c))]
                pltpu.make_async_remote_copy(src_ref=blk, dst_ref=blk, send_sem=fsend.at[nw * t + i],
                                             recv_sem=recv.at[4 * t + 1 + j],
                                             device_id=(x, y, 1 - c), device_id_type=MESH).wait_recv()
                pltpu.make_async_remote_copy(src_ref=blk, dst_ref=blk, send_sem=fsend.at[nw * t + i],
                                             recv_sem=frecv.at[nw * t + i],
                                             device_id=(x, y, 1 - c), device_id_type=MESH).start()

    outs = pl.pallas_call(
        body, name=name,
        out_shape=tuple([pltpu.SemaphoreType.DMA((nw * n,))] * 2 + [pltpu.HBM(g.shape, g.dtype) for g in lands]),
        in_specs=tuple([HBM_SPEC] * n + [SEM_SPEC, ANY_SPEC]),
        out_specs=tuple([SEM_SPEC] * 2 + [HBM_SPEC] * n),
        input_output_aliases={i: 2 + i for i in range(n)},
        compiler_params=pltpu.CompilerParams(has_side_effects=_DATAFLOW, collective_id=SIBLING_BARRIER_ID),
    )(*lands, recv_sems, after)
    return outs[0], outs[1], list(outs[2:])


def _gather_finish(lands, send_sems, recv_sems, arrivals, sends, after, name, own=True, direct_sends=True):
    n = len(lands)
    fwd = list(arrivals) + list(sends)
    sem_args = [send_sems, recv_sems]
    where = []
    for _, fs, fr in fwd:
        pos = []
        for arr in (fs, fr):
            hit = [i for i, have in enumerate(sem_args) if have is arr]
            if not hit:
                sem_args.append(arr)
                hit = [len(sem_args) - 1]
            pos.append(hit[0])
        where.append(pos)

    def body(*refs):
        land_refs = refs[:n]
        send, recv = refs[n], refs[n + 1]
        fwd_refs = [refs[n + p] for pos in where for p in pos]
        x, y, c = _mesh_pos()
        sibling = (x, y, 1 - c)
        chips = [(1 - x, y), (x, 1 - y), (1 - x, 1 - y)]

        def desc(ref, s_sem, r_sem):
            return pltpu.make_async_remote_copy(src_ref=ref, dst_ref=ref, send_sem=s_sem, recv_sem=r_sem,
                                                device_id=sibling, device_id_type=MESH)

        for t in range(n):
            mine = land_refs[t].at[_slot((x, y, c))]
            if own:
                desc(land_refs[t].at[_slot(sibling)], send.at[4 * t], recv.at[4 * t]).wait_recv()
            for a, (which, _, _) in enumerate(fwd):
                fs, fr = fwd_refs[2 * a], fwd_refs[2 * a + 1]
                for i, j in enumerate(which):
                    if a < len(arrivals):
                        desc(land_refs[t].at[_slot((*chips[j], 1 - c))], fs.at[len(which) * t + i],
                             fr.at[len(which) * t + i]).wait_recv()
                    else:
                        desc(land_refs[t].at[_slot((*chips[j], c))], fs.at[len(which) * t + i],
                             fr.at[len(which) * t + i]).wait_send()
            if direct_sends:
                for k in range(4):
                    desc(mine, send.at[4 * t + k], recv.at[4 * t + k]).wait_send()

    outs = pl.pallas_call(
        body, name=name,
        out_shape=tuple(pltpu.HBM(g.shape, g.dtype) for g in lands),
        in_specs=tuple([HBM_SPEC] * n + [SEM_SPEC] * len(sem_args) + [ANY_SPEC]),
        out_specs=tuple([HBM_SPEC] * n),
        input_output_aliases={i: i for i in range(n)},
        compiler_params=pltpu.CompilerParams(has_side_effects=_DATAFLOW),
    )(*lands, *sem_args, after)
    return list(outs)


def _pair_exchange_start(grads, after, name):
    n = len(grads)

    def body(*refs):
        ins, lands = refs[:n], refs[n:2 * n]
        send, recv = refs[2 * n + 1], refs[2 * n + 2]
        token = refs[-1]
        x, y, c = _mesh_pos()
        _sibling_handshake()
        for t in range(n):
            for q in range(4):
                pltpu.make_async_remote_copy(
                    src_ref=ins[t].at[2 * q + 1 - c], dst_ref=lands[t].at[q],
                    send_sem=send.at[4 * t + q], recv_sem=recv.at[4 * t + q],
                    device_id=(x, y, 1 - c), device_id_type=MESH).start()
        token[...] = jnp.zeros_like(token)

    land_shapes = [(4,) + g.shape[1:] for g in grads]
    outs = pl.pallas_call(
        body, name=name,
        out_shape=tuple([pltpu.SemaphoreType.DMA((4 * n,))] * 2 + [pltpu.HBM(g.shape, g.dtype) for g in grads]
                        + [pltpu.HBM(ls, g.dtype) for ls, g in zip(land_shapes, grads)]
                        + [jax.ShapeDtypeStruct((SUBLANES, LANES), F32)]),
        in_specs=tuple([HBM_SPEC] * (2 * n) + [ANY_SPEC]),
        out_specs=tuple([SEM_SPEC] * 2 + [HBM_SPEC] * (2 * n) + [pl.BlockSpec(memory_space=pltpu.VMEM)]),
        input_output_aliases={i: 2 + i for i in range(2 * n)},
        compiler_params=pltpu.CompilerParams(has_side_effects=_DATAFLOW, collective_id=SIBLING_BARRIER_ID),
    )(*[_hbm(g) for g in grads], *[_hbm(lax.empty(ls, g.dtype)) for ls, g in zip(land_shapes, grads)], after)
    return outs[0], outs[1], list(outs[2:2 + n]), list(outs[2 + n:2 + 2 * n]), outs[-1]


def _pair_exchange_finish(grads, lands, send_sems, recv_sems, after, name):
    n = len(grads)

    def body(*refs):
        ins, land_refs = refs[:n], refs[n:2 * n]
        send, recv = refs[2 * n], refs[2 * n + 1]
        x, y, c = _mesh_pos()
        for t in range(n):
            for q in range(4):
                cp = pltpu.make_async_remote_copy(
                    src_ref=ins[t].at[q], dst_ref=land_refs[t].at[q], send_sem=send.at[4 * t + q],
                    recv_sem=recv.at[4 * t + q], device_id=(x, y, 1 - c), device_id_type=MESH)
                cp.wait_send()
                cp.wait_recv()

    outs = pl.pallas_call(
        body, name=name,
        out_shape=tuple([pltpu.HBM(g.shape, g.dtype) for g in grads] + [pltpu.HBM(g.shape, g.dtype) for g in lands]),
        in_specs=tuple([HBM_SPEC] * (2 * n) + [SEM_SPEC] * 2 + [ANY_SPEC]),
        out_specs=tuple([HBM_SPEC] * (2 * n)),
        input_output_aliases={i: i for i in range(2 * n)},
        compiler_params=pltpu.CompilerParams(has_side_effects=_DATAFLOW),
    )(*grads, *lands, send_sems, recv_sems, after)
    return list(outs[:n]), list(outs[n:])


def _chip_exchange_start(sums, after, name, barrier_id):
    n = len(sums)

    def body(*refs):
        ins, lands = refs[:n], refs[n:2 * n]
        send, recv = refs[2 * n + 1], refs[2 * n + 2]
        token = refs[-1]
        x, y, c = _mesh_pos()
        chips = [(1 - x, y), (x, 1 - y), (1 - x, 1 - y)]
        _peer_handshake([(*chip, c) for chip in chips])
        for t in range(n):
            for r, chip in enumerate(chips):
                pltpu.make_async_remote_copy(
                    src_ref=ins[t].at[2 * chip[0] + chip[1]], dst_ref=lands[t].at[r],
                    send_sem=send.at[3 * t + r], recv_sem=recv.at[3 * t + r],
                    device_id=(*chip, c), device_id_type=MESH).start()
        token[...] = jnp.zeros_like(token)

    land_shapes = [(3,) + s.shape[1:] for s in sums]
    outs = pl.pallas_call(
        body, name=name,
        out_shape=tuple([pltpu.SemaphoreType.DMA((3 * n,))] * 2 + [pltpu.HBM(s.shape, s.dtype) for s in sums]
                        + [pltpu.HBM(ls, s.dtype) for ls, s in zip(land_shapes, sums)]
                        + [jax.ShapeDtypeStruct((SUBLANES, LANES), F32)]),
        in_specs=tuple([HBM_SPEC] * (2 * n) + [ANY_SPEC]),
        out_specs=tuple([SEM_SPEC] * 2 + [HBM_SPEC] * (2 * n) + [pl.BlockSpec(memory_space=pltpu.VMEM)]),
        input_output_aliases={i: 2 + i for i in range(2 * n)},
        compiler_params=pltpu.CompilerParams(has_side_effects=_DATAFLOW, collective_id=barrier_id),
    )(*[_hbm(s) for s in sums], *[_hbm(lax.empty(ls, s.dtype)) for ls, s in zip(land_shapes, sums)], after)
    return outs[0], outs[1], list(outs[2:2 + n]), list(outs[2 + n:2 + 2 * n]), outs[-1]


def _chip_exchange_finish(sums, lands, send_sems, recv_sems, after, name):
    n = len(sums)

    def body(*refs):
        ins, land_refs = refs[:n], refs[n:2 * n]
        send, recv = refs[2 * n], refs[2 * n + 1]
        x, y, c = _mesh_pos()
        for t in range(n):
            for r in range(3):
                cp = pltpu.make_async_remote_copy(
                    src_ref=ins[t].at[r], dst_ref=land_refs[t].at[r], send_sem=send.at[3 * t + r],
                    recv_sem=recv.at[3 * t + r],
                    device_id=(x, y, 1 - c), device_id_type=MESH)
                cp.wait_send()
                cp.wait_recv()

    outs = pl.pallas_call(
        body, name=name,
        out_shape=tuple(pltpu.HBM(g.shape, g.dtype) for g in lands),
        in_specs=tuple([HBM_SPEC] * (2 * n) + [SEM_SPEC] * 2 + [ANY_SPEC]),
        out_specs=tuple([HBM_SPEC] * n),
        input_output_aliases={n + i: i for i in range(n)},
        compiler_params=pltpu.CompilerParams(has_side_effects=_DATAFLOW),
    )(*sums, *lands, send_sems, recv_sems, after)
    return list(outs)


def _mm(a, b, *, mode, tm, tn, tk=None, b_blocked=False, out_blocked=False, out_dtypes=(F32,),
        epilogue=None, extras=(), after=None, kb=1, out_places=None, row_splits=1, out_block=None, name):
    if mode == "nn":
        m, k = a.shape
        n = b.shape[0] * b.shape[2] if b_blocked else b.shape[1]
        dims = (((1,), (0,)), ((), ()))
    elif mode == "nt":
        m, k = a.shape
        n = b.shape[1] if b_blocked else b.shape[0]
        if b_blocked:
            tk = kb * b.shape[2]
        dims = (((1,), (1,)), ((), ()))
    else:
        k, m = a.shape
        n = b.shape[1]
        dims = (((0,), (0,)), ((), ()))
    tk = k if tk is None else tk
    assert m % tm == 0 and n % tn == 0 and k % tk == 0, (name, m, n, k, tm, tn, tk)
    gm, gn, gk = m // tm, n // tn, k // tk
    if b_blocked:
        assert (tn if mode == "nn" else tk) == kb * b.shape[2], name
    if row_splits > 1:
        assert gk == 1 and epilogue is not None and mode != "tn" and not b_blocked and tm % (16 * row_splits) == 0, name

    if mode == "nn":
        a_spec = pl.BlockSpec((tm, tk), lambda i, j, kk: (i, kk))
        b_spec = (pl.BlockSpec((None, tk, tn), lambda i, j, kk: (j, kk, 0)) if b_blocked
                  else pl.BlockSpec((tk, tn), lambda i, j, kk: (kk, j)))
    elif mode == "nt":
        a_spec = pl.BlockSpec((tm, tk), lambda i, j, kk: (i, kk))
        b_spec = (pl.BlockSpec((kb, tn, tk // kb), lambda i, j, kk: (kk, j, 0)) if b_blocked
                  else pl.BlockSpec((tn, tk), lambda i, j, kk: (j, kk)))
    else:
        a_spec = pl.BlockSpec((tk, tm), lambda i, j, kk: (kk, i))
        b_spec = pl.BlockSpec((tk, tn), lambda i, j, kk: (kk, j))
    out_pack = 1
    if out_blocked and out_block is not None and out_block != tn:
        assert tn % out_block == 0, name
        out_pack = tn // out_block
        out_spec = pl.BlockSpec((out_pack, tm, out_block), lambda i, j, kk: (j, i, 0))
        out_shape = (n // out_block, m, out_block)
    elif out_blocked:
        out_spec = pl.BlockSpec((None, tm, tn), lambda i, j, kk: (j, i, 0))
        out_shape = (gn, m, tn)
    else:
        out_spec = pl.BlockSpec((tm, tn), lambda i, j, kk: (i, j))
        out_shape = (m, n)
    extra_specs = [pl.BlockSpec((tm, tn), functools.partial(lambda i, j, kk, off: (i, j + off), off=off))
                   for _, off in extras]
    n_extra, n_out = len(extras), len(out_dtypes)
    n_after = 0 if after is None else 1
    places = out_places if out_places is not None else (None,) * n_out

    def body(a_ref, b_ref, *rest):
        extra_refs = rest[:n_extra]
        out_refs = rest[n_extra + n_after:n_extra + n_after + n_out]

        def finish(acc):
            if epilogue is None:
                res = (acc,)
            else:
                res = epilogue(acc, *[e[...] for e in extra_refs])
            for o_ref, r in zip(out_refs, res):
                if out_pack == 1:
                    o_ref[...] = r.astype(o_ref.dtype)
                else:
                    for h in range(out_pack):
                        o_ref[h] = r[:, h * out_block:(h + 1) * out_block].astype(o_ref.dtype)

        if row_splits > 1:
            strip = tm // row_splits
            for h in range(row_splits):
                rows = slice(h * strip, (h + 1) * strip)
                acc = lax.dot_general(a_ref[rows, :], b_ref[...], dims, preferred_element_type=F32)
                res = epilogue(acc, *[e[rows, :] for e in extra_refs])
                for o_ref, r in zip(out_refs, res):
                    o_ref[rows, :] = r.astype(o_ref.dtype)
            return
        if mode == "nt" and b_blocked:
            bk = tk // kb
            part = lax.dot_general(a_ref[:, :bk], b_ref[0], dims, preferred_element_type=F32)
            for h in range(1, kb):
                part = part + lax.dot_general(a_ref[:, h * bk:(h + 1) * bk], b_ref[h], dims,
                                              preferred_element_type=F32)
        else:
            part = lax.dot_general(a_ref[...], b_ref[...], dims, preferred_element_type=F32)
        if gk == 1:
            finish(part)
        else:
            acc_ref = rest[-1]
            kk = pl.program_id(2)

            @pl.when(kk == 0)
            def _():
                acc_ref[...] = part

            @pl.when(kk > 0)
            def _():
                acc_ref[...] += part

            @pl.when(kk == gk - 1)
            def _():
                finish(acc_ref[...])

    outs = pl.pallas_call(
        body, name=name, grid=(gm, gn, gk),
        in_specs=[a_spec, b_spec] + extra_specs + [ANY_SPEC] * n_after,
        out_specs=[out_spec if place is None else
                   pl.BlockSpec((tm, tn), functools.partial(lambda i, j, kk, off: (i, j + off), off=place[1] // tn))
                   for place in places],
        out_shape=[jax.ShapeDtypeStruct(out_shape if place is None else (m, place[0]), dt)
                   for dt, place in zip(out_dtypes, places)],
        scratch_shapes=[pltpu.VMEM((tm, tn), F32)] if gk > 1 else [],
        compiler_params=_cparams("parallel", "parallel", "arbitrary"),
    )(a, b, *[e for e, _ in extras], *([] if after is None else [after]))
    return outs[0] if n_out == 1 else outs


def _mm_slots(a, w, slots, *, over, tm, tn=None, out_dtype=F32, epilogue=None, base=None, name):
    m = a.shape[0]
    ns = slots.shape[0]
    n_slots, w1, w2 = w.shape
    assert m % tm == 0
    if over == "n":
        k, bn = w1, w2

        def body(slots_ref, a_ref, w_ref, *rest):
            out_ref = rest[-1]
            acc = jnp.dot(a_ref[...], w_ref[...], preferred_element_type=F32)
            out_ref[...] = (acc if epilogue is None else epilogue(acc)).astype(out_ref.dtype)

        in_specs = [pl.BlockSpec((tm, k), lambda i, j, s: (i, 0)),
                    pl.BlockSpec((None, k, bn), lambda i, j, s: (s[j], 0, 0))]
        args = [a, w]
        aliases = {}
        if base is not None:
            in_specs.append(ANY_SPEC)
            args.append(base)
            aliases = {3: 0}
        return pl.pallas_call(
            body, name=name,
            grid_spec=pltpu.PrefetchScalarGridSpec(
                num_scalar_prefetch=1, grid=(m // tm, ns), in_specs=in_specs,
                out_specs=pl.BlockSpec((tm, bn), lambda i, j, s: (i, s[j]))),
            out_shape=jax.ShapeDtypeStruct((m, n_slots * bn), out_dtype),
            input_output_aliases=aliases,
            compiler_params=_cparams("parallel", "arbitrary"),
        )(slots, *args)

    bk, n = w1, w2
    tn = n if tn is None else tn
    assert n % tn == 0

    def body(slots_ref, a_ref, w_ref, *rest):
        out_ref, acc_ref = rest[-2], rest[-1]
        kk = pl.program_id(2)
        part = jnp.dot(a_ref[...], w_ref[...], preferred_element_type=F32)

        @pl.when(kk == 0)
        def _():
            acc_ref[...] = part if base is None else part + rest[0][...]

        @pl.when(kk > 0)
        def _():
            acc_ref[...] += part

        @pl.when(kk == ns - 1)
        def _():
            out_ref[...] = acc_ref[...].astype(out_ref.dtype)

    in_specs = [pl.BlockSpec((tm, bk), lambda i, j, kk, s: (i, s[kk])),
                pl.BlockSpec((None, bk, tn), lambda i, j, kk, s: (s[kk], 0, j))]
    args = [a, w]
    if base is not None:
        in_specs.append(pl.BlockSpec((tm, tn), lambda i, j, kk, s: (i, j)))
        args.append(base)
    return pl.pallas_call(
        body, name=name,
        grid_spec=pltpu.PrefetchScalarGridSpec(
            num_scalar_prefetch=1, grid=(m // tm, n // tn, ns), in_specs=in_specs,
            out_specs=pl.BlockSpec((tm, tn), lambda i, j, kk, s: (i, j)),
            scratch_shapes=[pltpu.VMEM((tm, tn), F32)]),
        out_shape=jax.ShapeDtypeStruct((m, n), out_dtype),
        compiler_params=_cparams("parallel", "parallel", "arbitrary"),
    )(slots, *args)


def _gate_mix(ya_pre, s, wpo, wco, proj, d_model, name):
    lp, width = ya_pre.shape
    nb, _, bn = wpo.shape
    ga_off = (proj.shape[1] - 2 * d_model) // bn
    gb_off = (proj.shape[1] - d_model) // bn

    n_strips = 4 if lp % 64 == 0 else 1

    def body(ya_ref, s_ref, wpo_ref, wco_ref, ga_ref, gb_ref, m_ref, y_a_ref, y_b_ref):
        strip = lp // n_strips
        for h in range(n_strips):
            rows = slice(h * strip, (h + 1) * strip)
            y_a = jnp.dot(ya_ref[rows, :], wpo_ref[...], preferred_element_type=F32)
            y_b = jnp.dot(s_ref[rows, :], wco_ref[...], preferred_element_type=F32)
            m = jax.nn.sigmoid(ga_ref[rows, :]) * y_a + jax.nn.sigmoid(gb_ref[rows, :]) * y_b
            m_ref[rows, :] = m.astype(BF16)
            y_a_ref[rows, :] = y_a.astype(BF16)
            y_b_ref[rows, :] = y_b.astype(BF16)

    act_spec = pl.BlockSpec((lp, width), lambda j: (0, 0))
    w_spec = pl.BlockSpec((None, width, bn), lambda j: (j, 0, 0))
    out_spec = pl.BlockSpec((lp, bn), lambda j: (0, j))
    return pl.pallas_call(
        body, name=name, grid=(nb,),
        in_specs=[act_spec, act_spec, w_spec, w_spec,
                  pl.BlockSpec((lp, bn), lambda j: (0, j + ga_off)),
                  pl.BlockSpec((lp, bn), lambda j: (0, j + gb_off))],
        out_specs=[out_spec] * 3,
        out_shape=[jax.ShapeDtypeStruct((lp, nb * bn), BF16)] * 3,
        compiler_params=_cparams("parallel"),
    )(ya_pre, s, wpo, wco, proj, proj)


def _rms_stats(x):
    return lax.rsqrt(jnp.mean(x * x, axis=-1, keepdims=True) + RMS_EPS)


def _rms_bwd(x, g, dy):
    r = _rms_stats(x)
    nrm = x * r
    dn = dy * g
    dx = r * (dn - nrm * jnp.mean(dn * nrm, axis=-1, keepdims=True))
    return dx, dy * nrm


def _rowwise(body, ins, outs, accs, *, lp, name):
    tr = _row_tile(lp, max(a.shape[1] for a in ins))
    n_in, n_out, n_acc = len(ins), len(outs), len(accs)

    def kernel_body(*refs):
        i = pl.program_id(0)
        acc_refs = refs[n_in + n_out:]

        @pl.when(i == 0)
        def _():
            for r in acc_refs:
                r[...] = jnp.zeros_like(r)

        body(i * tr, refs[:n_in], refs[n_in:n_in + n_out], acc_refs)

    in_specs = []
    for a in ins:
        if a.shape[0] == lp:
            in_specs.append(pl.BlockSpec((tr, a.shape[1]), lambda i: (i, 0)))
        else:
            in_specs.append(pl.BlockSpec(a.shape, lambda i: (0, 0)))
    out_specs = [pl.BlockSpec((tr, w), lambda i: (i, 0)) for w, _ in outs]
    out_specs += [pl.BlockSpec((SUBLANES, w), lambda i: (0, 0)) for w in accs]
    out_shape = [jax.ShapeDtypeStruct((lp, w), d) for w, d in outs]
    out_shape += [jax.ShapeDtypeStruct((SUBLANES, w), F32) for w in accs]
    return pl.pallas_call(
        kernel_body, name=name, grid=(lp // tr,), in_specs=in_specs, out_specs=out_specs,
        out_shape=out_shape, compiler_params=_cparams("arbitrary"),
    )(*ins)


SHIFT_TILE = 128


def _shifted_specs(width, n_big, n_small):
    per = SHIFT_TILE // N_META
    small = pl.BlockSpec((N_META, width), lambda i: (jnp.clip(per * i - 1, 0, n_small - 1), 0))
    big = pl.BlockSpec((SHIFT_TILE, width), lambda i: (jnp.minimum(i, n_big - 1), 0))
    return small, big


def _rms_pre(meta_full, x2, g, lp, seq):
    d = x2.shape[1]
    assert seq % SHIFT_TILE == 0 and lp % SHIFT_TILE == 0 and SHIFT_TILE % N_META == 0

    def body(meta_ref, xs_ref, xb_ref, g_ref, h0_ref, u1_ref):
        i = pl.program_id(0)
        head = jnp.where(i == 0, meta_ref[...], xs_ref[...])
        rows = jnp.concatenate([head, xb_ref[:SHIFT_TILE - N_META, :]], axis=0)
        r = i * SHIFT_TILE + lax.broadcasted_iota(jnp.int32, (SHIFT_TILE, 1), 0)
        rows = jnp.where(r < N_META + seq, rows, 0.0)
        h0_ref[...] = rows
        u1_ref[...] = (rows * _rms_stats(rows) * g_ref[...]).astype(BF16)

    small, big = _shifted_specs(d, seq // SHIFT_TILE, seq // N_META)
    tile = pl.BlockSpec((SHIFT_TILE, d), lambda i: (i, 0))
    return pl.pallas_call(
        body, name="rms_pre", grid=(lp // SHIFT_TILE,),
        in_specs=[pl.BlockSpec((N_META, d), lambda i: (0, 0)), small, big, pl.BlockSpec((1, d), lambda i: (0, 0))],
        out_specs=[tile, tile],
        out_shape=[jax.ShapeDtypeStruct((lp, d), F32), jax.ShapeDtypeStruct((lp, d), BF16)],
        compiler_params=_cparams("parallel"),
    )(meta_full, x2, x2, g)


def _post_mix(o, h0, g_post_mix, g_pre_mlp, lp):
    d = h0.shape[1]

    def body(row0, ins, outs, accs):
        o_ref, h0_ref, g1_ref, g2_ref = ins
        o_v = o_ref[...]
        h1 = h0_ref[...] + o_v * _rms_stats(o_v) * g1_ref[...]
        outs[0][...] = h1
        outs[1][...] = (h1 * _rms_stats(h1) * g2_ref[...]).astype(BF16)

    return _rowwise(body, [o, h0, g_post_mix, g_pre_mlp], [(d, F32), (d, BF16)], [], lp=lp, name="post_mix")


def _loss_head(f, h1, target, g_post_mlp, lp, seq):
    d = f.shape[1]

    def body(f_ref, h1_ref, ts_ref, tb_ref, g_ref, df_ref, dh_ref, dg_ref, loss_ref):
        i = pl.program_id(0)

        @pl.when(i == 0)
        def _():
            dg_ref[...] = jnp.zeros_like(dg_ref)
            loss_ref[...] = jnp.zeros_like(loss_ref)

        f_v, g = f_ref[...], g_ref[...]
        r = _rms_stats(f_v)
        nrm = f_v * r
        rows = i * SHIFT_TILE + lax.broadcasted_iota(jnp.int32, (SHIFT_TILE, 1), 0)
        valid = (rows >= N_META) & (rows < N_META + seq)
        tgt = jnp.concatenate([ts_ref[...], tb_ref[:SHIFT_TILE - N_META, :]], axis=0)
        err = jnp.where(valid, h1_ref[...] + nrm * g - tgt, 0.0)
        loss_ref[...] += 0.5 * jnp.sum(jnp.mean(err * err, axis=-1, keepdims=True))
        dy = err * (1.0 / d)
        dn = dy * g
        df_ref[...] = (r * (dn - nrm * jnp.mean(dn * nrm, axis=-1, keepdims=True))).astype(BF16)
        dh_ref[...] = dy
        dg_ref[...] += _rowsum8(dy * nrm)

    small, big = _shifted_specs(d, seq // SHIFT_TILE, seq // N_META)
    tile = pl.BlockSpec((SHIFT_TILE, d), lambda i: (i, 0))
    return pl.pallas_call(
        body, name="loss_head", grid=(lp // SHIFT_TILE,),
        in_specs=[tile, tile, small, big, pl.BlockSpec((1, d), lambda i: (0, 0))],
        out_specs=[tile, tile, pl.BlockSpec((SUBLANES, d), lambda i: (0, 0)),
                   pl.BlockSpec((SUBLANES, LANES), lambda i: (0, 0))],
        out_shape=[jax.ShapeDtypeStruct((lp, d), BF16), jax.ShapeDtypeStruct((lp, d), F32),
                   jax.ShapeDtypeStruct((SUBLANES, d), F32), jax.ShapeDtypeStruct((SUBLANES, LANES), F32)],
        compiler_params=_cparams("arbitrary"),
    )(f, h1, target, target, g_post_mlp)


def _mid_bwd(du2, h1, dh, o, g_pre_mlp, g_post_mix, lp):
    d = h1.shape[1]

    def body(row0, ins, outs, accs):
        du2_ref, h1_ref, dh_ref, o_ref, g2_ref, g1_ref = ins
        dx2, dg2 = _rms_bwd(h1_ref[...], g2_ref[...], du2_ref[...])
        dh1 = dh_ref[...] + dx2
        do, dg1 = _rms_bwd(o_ref[...], g1_ref[...], dh1)
        outs[0][...] = dh1
        outs[1][...] = do.astype(BF16)
        accs[0][...] += _rowsum8(dg2)
        accs[1][...] += _rowsum8(dg1)

    return _rowwise(body, [du2, h1, dh, o, g_pre_mlp, g_post_mix], [(d, F32), (d, BF16)], [d, d], lp=lp,
                    name="mid_bwd")


def _pre_mix_bwd(du1, h0, dh1, g_pre_mix, seq):
    d = h0.shape[1]
    per = SHIFT_TILE // N_META
    assert seq % SHIFT_TILE == 0

    def body(du_b, h_b, dh_b, du_n, h_n, dh_n, du_m, h_m, dh_m, g_ref, gx_ref, gm_ref, dg_ref):
        i = pl.program_id(0)
        g = g_ref[...]

        @pl.when(i == 0)
        def _():
            dx, dg = _rms_bwd(h_m[...], g, du_m[...])
            gm_ref[...] = dh_m[...] + dx
            dg_ref[...] = _rowsum8(dg)

        rows = lambda big, nxt: jnp.concatenate([big[N_META:, :], nxt[...]], axis=0)
        dx, dg = _rms_bwd(rows(h_b, h_n), g, rows(du_b, du_n))
        gx_ref[...] = rows(dh_b, dh_n) + dx
        dg_ref[...] += _rowsum8(dg)

    big = pl.BlockSpec((SHIFT_TILE, d), lambda i: (i, 0))
    nxt = pl.BlockSpec((N_META, d), lambda i: (per * (i + 1), 0))
    first = pl.BlockSpec((N_META, d), lambda i: (0, 0))
    return pl.pallas_call(
        body, name="pre_mix_bwd", grid=(seq // SHIFT_TILE,),
        in_specs=[big] * 3 + [nxt] * 3 + [first] * 3 + [pl.BlockSpec((1, d), lambda i: (0, 0))],
        out_specs=[big, first, pl.BlockSpec((SUBLANES, d), lambda i: (0, 0))],
        out_shape=[jax.ShapeDtypeStruct((seq, d), F32), jax.ShapeDtypeStruct((N_META, d), F32),
                   jax.ShapeDtypeStruct((SUBLANES, d), F32)],
        compiler_params=_cparams("arbitrary"),
    )(du1, h0, dh1, du1, h0, dh1, du1, h0, dh1, g_pre_mix)


def _ln_stats(c):
    mu = jnp.mean(c, axis=-1, keepdims=True)
    var = jnp.mean(jnp.square(c - mu), axis=-1, keepdims=True)
    return mu, lax.rsqrt(var + LN_EPS)


def _ln_silu(c, ln_g, ln_b, lp):
    w = c.shape[1]

    def body(row0, ins, outs, accs):
        c_ref, g_ref, b_ref = ins
        c_v = c_ref[...]
        mu, rstd = _ln_stats(c_v)
        ln = (c_v - mu) * rstd * g_ref[...] + b_ref[...]
        outs[0][...] = (ln * jax.nn.sigmoid(ln)).astype(BF16)

    return _rowwise(body, [c, ln_g, ln_b], [(w, BF16)], [], lp=lp, name="ln_silu")[0]


def _ln_silu_bwd(c, ds, ln_g, ln_b, lp):
    w = c.shape[1]

    def body(row0, ins, outs, accs):
        c_ref, ds_ref, g_ref, b_ref = ins
        c_v, g = c_ref[...], g_ref[...]
        mu, rstd = _ln_stats(c_v)
        nrm = (c_v - mu) * rstd
        ln = nrm * g + b_ref[...]
        sig = jax.nn.sigmoid(ln)
        dln = ds_ref[...] * (sig * (1.0 + ln * (1.0 - sig)))
        dn = dln * g
        dc = rstd * (dn - jnp.mean(dn, axis=-1, keepdims=True) - nrm * jnp.mean(dn * nrm, axis=-1, keepdims=True))
        outs[0][...] = dc
        accs[0][...] += _rowsum8(dln * nrm)
        accs[1][...] += _rowsum8(dln)
        accs[2][...] += _rowsum8(dc)

    return _rowwise(body, [c, ds, ln_g, ln_b], [(w, F32)], [w, w, w], lp=lp, name="ln_silu_bwd")


def _chunk_with_history(ref, i, cols=slice(None)):
    t0 = pl.multiple_of(i * ROW_CHUNK, ROW_CHUNK)
    lo0 = pl.multiple_of(jnp.maximum(t0 - HALO, 0), SUBLANES)
    lo = jnp.where(i > 0, ref[pl.ds(lo0, HALO), cols], 0.0)
    return jnp.concatenate([lo, ref[pl.ds(t0, ROW_CHUNK), cols]], axis=0)


def _chunk_with_future(ref, i, n_chunks, cols=slice(None)):
    t0 = pl.multiple_of(i * ROW_CHUNK, ROW_CHUNK)
    hi0 = pl.multiple_of(jnp.minimum(t0 + ROW_CHUNK, (n_chunks - 1) * ROW_CHUNK), SUBLANES)
    hi = jnp.where(i < n_chunks - 1, ref[pl.ds(hi0, HALO), cols], 0.0)
    return jnp.concatenate([ref[pl.ds(t0, ROW_CHUNK), cols], hi], axis=0)


def _inv_count(t0, n_rows, window):
    pos = t0 + lax.broadcasted_iota(jnp.int32, (n_rows, 1), 0)
    return 1.0 / jnp.minimum(pos + 1, window).astype(F32)


def _pool_delta(z_hist, t0, window):
    s = z_hist
    sh = 1
    while sh < window:
        s = s + pltpu.roll(s, sh, 0)
        sh *= 2
    cur = z_hist[HALO:, :]
    return s[HALO:, :] * _inv_count(t0, ROW_CHUNK, window) - cur


def _pool_fwd(proj, wpg, pool_scale, lp):
    n_grp, gdim, _ = wpg.shape
    width = n_grp * gdim
    n_chunks = lp // ROW_CHUNK

    def body(z_ref, w_ref, sc_ref, out_ref):
        for g, window in enumerate(POOL_WINDOWS):
            cols = slice(g * gdim, (g + 1) * gdim)

            def chunk(i, carry, cols=cols, g=g, window=window):
                t0 = pl.multiple_of(i * ROW_CHUNK, ROW_CHUNK)
                d = _pool_delta(_chunk_with_history(z_ref, i, cols), t0, window)
                q = jnp.dot(d.astype(BF16), w_ref[g], preferred_element_type=F32)
                out_ref[pl.ds(t0, ROW_CHUNK), cols] = (q * sc_ref[:, cols]).astype(BF16)
                return carry

            lax.fori_loop(0, n_chunks, chunk, 0)

    return pl.pallas_call(
        body, name="pool_fwd", grid=(1,),
        in_specs=[pl.BlockSpec((lp, width), lambda i: (0, 0)),
                  pl.BlockSpec(wpg.shape, lambda i: (0, 0, 0)),
                  pl.BlockSpec(pool_scale.shape, lambda i: (0, 0))],
        out_specs=pl.BlockSpec((lp, width), lambda i: (0, 0)),
        out_shape=jax.ShapeDtypeStruct((lp, width), BF16),
        compiler_params=_cparams("arbitrary"),
    )(proj, wpg, pool_scale)


def _pool_bwd(proj, d_ya, wpg, pool_scale, dproj, lp):
    n_grp, gdim, _ = wpg.shape
    width = n_grp * gdim
    n_chunks = lp // ROW_CHUNK
    ext = ROW_CHUNK + HALO

    def body(z_ref, dya_ref, w_ref, sc_ref, _, dz_ref, dw_ref, dsc_ref):
        dw_ref[...] = jnp.zeros_like(dw_ref)
        dsc_ref[...] = jnp.zeros_like(dsc_ref)
        for g, window in enumerate(POOL_WINDOWS):
            cols = slice(g * gdim, (g + 1) * gdim)

            def chunk(i, carry, cols=cols, g=g, window=window):
                t0 = pl.multiple_of(i * ROW_CHUNK, ROW_CHUNK)
                w_g = w_ref[g]
                scale = sc_ref[:, cols]
                d = _pool_delta(_chunk_with_history(z_ref, i, cols), t0, window).astype(BF16)
                dya_ext = _chunk_with_future(dya_ref, i, n_chunks, cols)
                dya = dya_ext[:ROW_CHUNK, :]
                q = jnp.dot(d, w_g, preferred_element_type=F32)
                dsc_ref[:, cols] += _rowsum8(dya * q)
                e_ext = (dya_ext * scale).astype(BF16)
                dw_ref[g] += lax.dot_general(d, e_ext[:ROW_CHUNK, :], (((0,), (0,)), ((), ())),
                                             preferred_element_type=F32)
                dd_ext = lax.dot_general(e_ext, w_g, (((1,), (1,)), ((), ())), preferred_element_type=F32)
                s = dd_ext * _inv_count(t0, ext, window)
                sh = 1
                while sh < window:
                    s = s + pltpu.roll(s, ext - sh, 0)
                    sh *= 2
                dz_ref[pl.ds(t0, ROW_CHUNK), cols] = (s[:ROW_CHUNK, :] - dd_ext[:ROW_CHUNK, :]).astype(BF16)
                return carry

            lax.fori_loop(0, n_chunks, chunk, 0)

    blk = pl.BlockSpec((lp, width), lambda i: (0, 0))
    return pl.pallas_call(
        body, name="pool_bwd", grid=(1,),
        in_specs=[blk, blk, pl.BlockSpec(wpg.shape, lambda i: (0, 0, 0)),
                  pl.BlockSpec(pool_scale.shape, lambda i: (0, 0)), ANY_SPEC],
        out_specs=[blk, pl.BlockSpec(wpg.shape, lambda i: (0, 0, 0)),
                   pl.BlockSpec((SUBLANES, width), lambda i: (0, 0))],
        out_shape=[jax.ShapeDtypeStruct(dproj.shape, BF16), jax.ShapeDtypeStruct(wpg.shape, F32),
                   jax.ShapeDtypeStruct((SUBLANES, width), F32)],
        input_output_aliases={4: 0},
        compiler_params=_cparams("arbitrary"),
    )(proj, d_ya, wpg, pool_scale, dproj)


def _conv_fwd(proj, w_dw, b_dw, lp, width, v_col0):
    n_chunks = lp // ROW_CHUNK
    v_blk0, g_blk0 = v_col0 // LANES, (v_col0 + width) // LANES

    def body(v_ref, gc_ref, w_ref, b_ref, c_ref, a_pad):
        a_pad[pl.ds(0, HALO), :] = jnp.zeros((HALO, LANES), F32)
        a_pad[pl.ds(HALO, lp), :] = v_ref[...] * jax.nn.sigmoid(gc_ref[...])

        def chunk(i, carry):
            t0 = pl.multiple_of(i * ROW_CHUNK, ROW_CHUNK)
            hist = a_pad[pl.ds(t0, ROW_CHUNK + HALO), :]
            acc = jnp.zeros((ROW_CHUNK, LANES), F32)
            for k in range(CONV_KERNEL):
                acc = acc + w_ref[k:k + 1, :] * pltpu.roll(hist, CONV_KERNEL - 1 - k, 0)[HALO:, :]
            c_ref[pl.ds(t0, ROW_CHUNK), :] = acc + b_ref[...]
            return carry

        lax.fori_loop(0, n_chunks, chunk, 0)

    return pl.pallas_call(
        body, name="conv_fwd", grid=(width // LANES,),
        in_specs=[pl.BlockSpec((lp, LANES), lambda j: (0, j + v_blk0)),
                  pl.BlockSpec((lp, LANES), lambda j: (0, j + g_blk0)),
                  pl.BlockSpec((CONV_TAPS_PADDED, LANES), lambda j: (0, j)),
                  pl.BlockSpec((1, LANES), lambda j: (0, j))],
        out_specs=pl.BlockSpec((lp, LANES), lambda j: (0, j)),
        out_shape=jax.ShapeDtypeStruct((lp, width), F32),
        scratch_shapes=[pltpu.VMEM((lp + HALO, LANES), F32)],
        compiler_params=_cparams("parallel"),
    )(proj, proj, w_dw, b_dw)


def _conv_bwd(proj, dc, w_dw, dproj, lp, width, v_col0):
    n_chunks = lp // ROW_CHUNK
    ext = ROW_CHUNK + HALO
    v_blk0, g_blk0 = v_col0 // LANES, (v_col0 + width) // LANES

    def body(v_ref, gc_ref, dc_ref, w_ref, _, dv_ref, dgc_ref, dw_ref, a_pad, dc_pad, dw_acc):
        sig = jax.nn.sigmoid(gc_ref[...])
        a_pad[pl.ds(0, HALO), :] = jnp.zeros((HALO, LANES), F32)
        a_pad[pl.ds(HALO, lp), :] = v_ref[...] * sig
        dc_pad[pl.ds(0, lp), :] = dc_ref[...]
        dc_pad[pl.ds(lp, HALO), :] = jnp.zeros((HALO, LANES), F32)
        dw_acc[...] = jnp.zeros_like(dw_acc)

        def chunk(i, carry):
            t0 = pl.multiple_of(i * ROW_CHUNK, ROW_CHUNK)
            hist = a_pad[pl.ds(t0, ext), :]
            fut = dc_pad[pl.ds(t0, ext), :]
            dc_cur = fut[:ROW_CHUNK, :]
            da = jnp.zeros((ROW_CHUNK, LANES), F32)
            for k in range(CONV_KERNEL):
                lag = CONV_KERNEL - 1 - k
                da = da + w_ref[k:k + 1, :] * pltpu.roll(fut, (ext - lag) % ext, 0)[:ROW_CHUNK, :]
                dw_acc[pl.ds(SUBLANES * k, SUBLANES), :] += _rowsum8(dc_cur * pltpu.roll(hist, lag, 0)[HALO:, :])
            rows = pl.ds(t0, ROW_CHUNK)
            sg = jax.nn.sigmoid(gc_ref[rows, :])
            dv_ref[rows, :] = (da * sg).astype(BF16)
            dgc_ref[rows, :] = (da * v_ref[rows, :] * sg * (1.0 - sg)).astype(BF16)
            return carry

        lax.fori_loop(0, n_chunks, chunk, 0)
        dw_ref[...] = dw_acc[...].reshape(CONV_TAPS_PADDED, SUBLANES, LANES).sum(axis=1)

    col = lambda j: (0, j)
    return pl.pallas_call(
        body, name="conv_bwd", grid=(width // LANES,),
        in_specs=[pl.BlockSpec((lp, LANES), lambda j: (0, j + v_blk0)),
                  pl.BlockSpec((lp, LANES), lambda j: (0, j + g_blk0)),
                  pl.BlockSpec((lp, LANES), col),
                  pl.BlockSpec((CONV_TAPS_PADDED, LANES), col), ANY_SPEC],
        out_specs=[pl.BlockSpec((lp, LANES), lambda j: (0, j + v_blk0)), pl.BlockSpec((lp, LANES), col),
                   pl.BlockSpec((CONV_TAPS_PADDED, LANES), col)],
        out_shape=[jax.ShapeDtypeStruct(dproj.shape, BF16), jax.ShapeDtypeStruct((lp, width), BF16),
                   jax.ShapeDtypeStruct((CONV_TAPS_PADDED, width), F32)],
        scratch_shapes=[pltpu.VMEM((lp + HALO, LANES), F32), pltpu.VMEM((lp + HALO, LANES), F32),
                        pltpu.VMEM((CONV_TAPS_PADDED * SUBLANES, LANES), F32)],
        input_output_aliases={4: 0},
        compiler_params=_cparams("parallel"),
    )(proj, proj, dc, w_dw, dproj)


def _place_columns(dst, pieces, name):
    m = dst.shape[0]
    tile = 512
    counts = [p.shape[1] // tile for p, _ in pieces]
    starts = [sum(counts[:i]) for i in range(len(pieces))]
    n_steps = sum(counts)

    def local(s, i):
        return jnp.clip(s - starts[i], 0, counts[i] - 1)

    def out_index(s):
        blk = pieces[0][1] // tile + local(s, 0)
        for i in range(1, len(pieces)):
            blk = jnp.where(s >= starts[i], pieces[i][1] // tile + local(s, i), blk)
        return 0, blk

    def body(*refs):
        out_ref = refs[-1]
        s = pl.program_id(0)
        for i in range(len(pieces)):
            @pl.when((s >= starts[i]) & (s < starts[i] + counts[i]))
            def _(i=i):
                out_ref[...] = refs[i][...]

    return pl.pallas_call(
        body, name=name, grid=(n_steps,),
        in_specs=[pl.BlockSpec((m, tile), functools.partial(lambda s, i: (0, local(s, i)), i=i))
                  for i in range(len(pieces))] + [ANY_SPEC],
        out_specs=pl.BlockSpec((m, tile), out_index),
        out_shape=jax.ShapeDtypeStruct(dst.shape, dst.dtype),
        input_output_aliases={len(pieces): 0},
        compiler_params=_cparams("arbitrary"),
    )(*[p for p, _ in pieces], dst)


def _adamw_math(w, g, m, v):
    m = ADAM_B1 * m + (1.0 - ADAM_B1) * g
    v = ADAM_B2 * v + (1.0 - ADAM_B2) * jnp.square(g)
    m_hat = m / (1.0 - ADAM_B1 ** ADAM_STEP)
    v_hat = v / (1.0 - ADAM_B2 ** ADAM_STEP)
    delta = -ADAM_LR * (m_hat / (jnp.sqrt(v_hat) + ADAM_EPS) + ADAM_WD * w)
    return delta, m, v


def _pair_sum(own, recv, where, name):
    _, _, rows, cols = own.shape
    tr = _row_tile(rows, cols, 1024 * 1024)

    def body(where_ref, own_ref, recv_ref, out_ref):
        out_ref[...] = (own_ref[...].astype(F32) + recv_ref[...].astype(F32)).astype(BF16)

    return pl.pallas_call(
        body, name=name,
        grid_spec=pltpu.PrefetchScalarGridSpec(
            num_scalar_prefetch=1, grid=(3, rows // tr),
            in_specs=[pl.BlockSpec((None, None, tr, cols), lambda r, i, wh: (wh[2 + r], wh[0], i, 0)),
                      pl.BlockSpec((None, tr, cols), lambda r, i, wh: (wh[2 + r], i, 0))],
            out_specs=pl.BlockSpec((None, tr, cols), lambda r, i, wh: (wh[2 + r], i, 0))),
        out_shape=jax.ShapeDtypeStruct((4, rows, cols), BF16),
        compiler_params=_cparams("parallel", "parallel"),
    )(where, own, recv)


def _adamw_big(w, m, v, own, from_sibling, recv3, where, name):
    rows, cols = w.shape
    tr = _row_tile(rows, cols, 256 * 1024)

    def body(where_ref, w_ref, m_ref, v_ref, own_ref, sib_ref, r_ref, g_out, d_out, m_out, v_out):
        g = own_ref[...].astype(F32) + sib_ref[...].astype(F32)
        for r in range(3):
            g = g + r_ref[r].astype(F32)
        delta, m_new, v_new = _adamw_math(w_ref[...], g, m_ref[...], v_ref[...])
        g_out[...] = g
        d_out[...] = delta
        m_out[...] = m_new
        v_out[...] = v_new

    blk = pl.BlockSpec((tr, cols), lambda i, q_ref: (i, 0))
    return pl.pallas_call(
        body, name=name,
        grid_spec=pltpu.PrefetchScalarGridSpec(
            num_scalar_prefetch=1, grid=(rows // tr,),
            in_specs=[blk, blk, blk,
                      pl.BlockSpec((None, None, tr, cols), lambda i, wh: (wh[1], wh[0], i, 0)),
                      pl.BlockSpec((None, tr, cols), lambda i, wh: (wh[1], i, 0)),
                      pl.BlockSpec((3, tr, cols), lambda i, wh: (0, i, 0))],
            out_specs=[blk] * 4),
        out_shape=[jax.ShapeDtypeStruct((rows, cols), F32)] * 4,
        compiler_params=_cparams("parallel"),
    )(where, w, m, v, own, from_sibling, recv3)


def _small_update(me_idx, packed, rep_params, rep_places, meta_wmv, meta_row0, wdw_wmv, wdw_row0, loss_row0):
    n_rep = len(rep_params)
    meta_rows, meta_cols = meta_wmv[0].shape
    wdw_rows, wdw_cols = wdw_wmv[0].shape

    def body(me_ref, *refs):
        pos = 0

        def take(k):
            nonlocal pos
            out = refs[pos:pos + k]
            pos += k
            return out

        rep_in = [take(3) for _ in range(n_rep)]
        rep_g = take(n_rep)
        meta_in, (meta_g,) = take(3), take(1)
        wdw_in, (wdw_g,) = take(3), take(1)
        (loss_ref,) = take(1)
        rep_out = [take(4) for _ in range(n_rep)]
        meta_out, wdw_out = take(4), take(4)
        (loss_out,) = take(1)

        def update(wmv, g, outs):
            delta, m_new, v_new = _adamw_math(wmv[0][...], g, wmv[1][...], wmv[2][...])
            for o_ref, val in zip(outs, (g, delta, m_new, v_new)):
                o_ref[...] = val

        for wmv, g_ref, outs in zip(rep_in, rep_g, rep_out):
            g = jnp.sum(g_ref[0], axis=0, keepdims=True)
            for j in range(1, N_DEV):
                g = g + jnp.sum(g_ref[j], axis=0, keepdims=True)
            update(wmv, g, outs)
        for wmv, g_ref, outs in ((meta_in, meta_g, meta_out), (wdw_in, wdw_g, wdw_out)):
            g = g_ref[0]
            for j in range(1, N_DEV):
                g = g + g_ref[j]
            update(wmv, g, outs)
        total = loss_ref[0]
        for j in range(1, N_DEV):
            total = total + loss_ref[j]
        loss_out[...] = total

    def whole(a):
        nd = a.ndim
        return pl.BlockSpec(a.shape, lambda i, me_ref, nd=nd: (0,) * nd)

    ins, in_specs = [], []
    for wmv in rep_params:
        ins += list(wmv)
        in_specs += [whole(a) for a in wmv]
    for wmv, (row0, col0) in zip(rep_params, rep_places):
        width = wmv[0].shape[1]
        ins.append(packed)
        in_specs.append(pl.BlockSpec((N_DEV, SUBLANES, width),
                                     lambda i, me_ref, rb=row0 // SUBLANES, cb=col0 // width: (0, rb, cb)))
    ins += list(meta_wmv) + [packed]
    in_specs += [whole(a) for a in meta_wmv]
    in_specs.append(pl.BlockSpec((N_DEV, meta_rows, meta_cols),
                                 lambda i, me_ref, rb=meta_row0 // meta_rows: (0, rb, me_ref[0])))
    ins += list(wdw_wmv) + [packed]
    in_specs += [whole(a) for a in wdw_wmv]
    in_specs.append(pl.BlockSpec((N_DEV, wdw_rows, wdw_cols),
                                 lambda i, me_ref, rb=wdw_row0 // wdw_rows: (0, rb, me_ref[0])))
    ins.append(packed)
    in_specs.append(pl.BlockSpec((N_DEV, SUBLANES, LANES), lambda i, me_ref, rb=loss_row0 // SUBLANES: (0, rb, 0)))

    out_shape, out_specs = [], []
    for wmv in list(rep_params) + [meta_wmv, wdw_wmv]:
        out_shape += [jax.ShapeDtypeStruct(wmv[0].shape, F32)] * 4
        out_specs += [whole(wmv[0])] * 4
    out_shape.append(jax.ShapeDtypeStruct((SUBLANES, LANES), F32))
    out_specs.append(pl.BlockSpec((SUBLANES, LANES), lambda i, me_ref: (0, 0)))

    outs = pl.pallas_call(
        body, name="small_update",
        grid_spec=pltpu.PrefetchScalarGridSpec(num_scalar_prefetch=1, grid=(1,), in_specs=in_specs,
                                               out_specs=out_specs),
        out_shape=out_shape, compiler_params=_cparams("arbitrary"),
    )(me_idx, *ins)
    groups = [outs[4 * i:4 * i + 4] for i in range(n_rep + 2)]
    return groups[:n_rep], groups[n_rep], groups[n_rep + 1], outs[-1]


def kernel(x, meta, g_pre_mix, w_in, w_pool_grp, pool_scale, w_pool_out, w_dw, b_dw, conv_ln_g, conv_ln_b, w_conv_out, w_o, g_post_mix, g_pre_mlp, w_up, w_down, g_post_mlp, loss_target, m_meta, m_g_pre_mix, m_w_in, m_w_pool_grp, m_pool_scale, m_w_pool_out, m_w_dw, m_b_dw, m_conv_ln_g, m_conv_ln_b, m_w_conv_out, m_w_o, m_g_post_mix, m_g_pre_mlp, m_w_up, m_w_down, m_g_post_mlp, v_meta, v_g_pre_mix, v_w_in, v_w_pool_grp, v_pool_scale, v_w_pool_out, v_w_dw, v_b_dw, v_conv_ln_g, v_conv_ln_b, v_w_conv_out, v_w_o, v_g_post_mix, v_g_pre_mlp, v_w_up, v_w_down, v_g_post_mlp):
    seq, d = x.shape[1], x.shape[2]
    pool_w = pool_scale.shape[1]
    conv_w = b_dw.shape[1]
    n_grp, grp_rows, gdim = w_pool_grp.shape[1:]
    lp = _round_up(N_META + seq, ROW_CHUNK)
    tm_half = lp // 2 if (lp // 2) % 16 == 0 else lp
    c_idx = lax.axis_index("c").astype(jnp.int32)
    chip_idx = (2 * lax.axis_index("x") + lax.axis_index("y")).astype(jnp.int32)
    me_idx = 2 * chip_idx + c_idx

    pad_taps = ((0, CONV_TAPS_PADDED - CONV_KERNEL), (0, 0))
    big = dict(w_in=w_in[0], w_pool_grp=w_pool_grp[0].reshape(n_grp * grp_rows, gdim), w_pool_out=w_pool_out[0],
               w_conv_out=w_conv_out[0], w_o=w_o[0], w_up=w_up[0], w_down=w_down[0])
    big_names = list(big)
    moments = dict(w_in=(m_w_in, v_w_in), w_pool_grp=(m_w_pool_grp, v_w_pool_grp), w_pool_out=(m_w_pool_out, v_w_pool_out),
                   w_conv_out=(m_w_conv_out, v_w_conv_out), w_o=(m_w_o, v_w_o), w_up=(m_w_up, v_w_up),
                   w_down=(m_w_down, v_w_down))
    slot_idx = me_idx.reshape(1)
    sources = dict(big, meta=meta, w_dw=jnp.pad(w_dw[0], pad_taps))

    def fill(k, after):
        return _fill_slot(sources[k], slot_idx, BF16 if k in big else F32, after, "fill_" + k)

    gather_groups = [["meta", "w_dw"], ["w_in"], ["w_pool_grp", "w_pool_out", "w_conv_out", "w_o"], ["w_up"], ["w_down"]]
    started, token = _gather_start([[fill(k, slot_idx) for k in names] for names in gather_groups[:2]], slot_idx,
                                   "gather_start_first", BARRIER_IDS["gather_first"], issue_order=(0, 3, 1, 2))
    started_rest, _ = _gather_start([[fill(k, token) for k in names] for names in gather_groups[2:]], token,
                                    "gather_start_rest", BARRIER_IDS["gather_rest"])
    started += started_rest
    wg = {}
    x_idx, y_idx = lax.axis_index("x"), lax.axis_index("y")
    at = lambda px, py, pc: 4 * px + 2 * py + pc
    near_slots = jnp.stack([at(x_idx, y_idx, c_idx), at(x_idx, y_idx, 1 - c_idx), at(1 - x_idx, y_idx, c_idx),
                            at(x_idx, 1 - y_idx, c_idx), at(1 - x_idx, y_idx, 1 - c_idx),
                            at(x_idx, 1 - y_idx, 1 - c_idx)]).astype(jnp.int32)
    far_slots = jnp.stack([at(1 - x_idx, 1 - y_idx, c_idx), at(1 - x_idx, 1 - y_idx, 1 - c_idx)]).astype(jnp.int32)

    def gather_whole(gi, after_forward, after_finish):
        send, recv, lands = started[gi]
        fs, fr, lands = _gather_forward(lands, recv, after_forward(), f"gather_forward_{gi}")
        lands = _gather_finish(lands, send, recv, [(ALL_CHIPS, fs, fr)], [(ALL_CHIPS, fs, fr)], after_finish(),
                               f"gather_finish_{gi}")
        wg.update(zip(gather_groups[gi], lands))

    near_state = {}

    def gather_near(gi, after):
        send, recv, lands = started[gi]
        fs, fr, lands = _gather_forward(lands, recv, after, f"gather_forward_near_{gi}", which=NEAR)
        lands = _gather_finish(lands, send, recv, [(NEAR, fs, fr)], [], after, f"gather_finish_near_{gi}",
                               direct_sends=False)
        near_state[gi] = (fs, fr)
        return lands

    def gather_far(gi, lands, after):
        send, recv, _ = started[gi]
        fs, fr, lands = _gather_forward(lands, recv, after, f"gather_forward_far_{gi}", which=FAR)
        lands = _gather_finish(lands, send, recv, [(FAR, fs, fr)], [(NEAR,) + near_state[gi], (FAR, fs, fr)], after,
                               f"gather_finish_far_{gi}", own=False)
        wg.update(zip(gather_groups[gi], lands))
        return lands

    gather_whole(0, lambda: token, lambda: token)
    meta_full = wg["meta"].transpose(1, 0, 2).reshape(N_META, d)
    wdw_full = wg["w_dw"].transpose(1, 0, 2).reshape(CONV_TAPS_PADDED, conv_w)
    target = loss_target[0]
    h0, u1 = _rms_pre(meta_full, x[0], g_pre_mix, lp, seq)
    send, recv, w_in_lands = started[1]
    w_in_lands = _gather_finish(w_in_lands, send, recv, [], [], u1, "gather_finish_home_1", direct_sends=False)
    proj = _mm_slots(u1, w_in_lands[0], near_slots[:2], over="n", tm=tm_half, name="mm_proj_home")
    fs_far, fr_far, w_in_lands = _gather_forward(w_in_lands, recv, proj, "gather_forward_far_1", which=FAR)
    w_in_lands = _gather_finish(w_in_lands, send, recv, [(FAR, fs_far, fr_far)], [], proj, "gather_finish_far_1",
                                own=False, direct_sends=False)
    proj = _mm_slots(u1, w_in_lands[0], far_slots, over="n", tm=tm_half, base=proj, name="mm_proj_far")
    fs_near, fr_near, w_in_lands = _gather_forward(w_in_lands, recv, proj, "gather_forward_near_1", which=NEAR)
    w_in_lands = _gather_finish(w_in_lands, send, recv, [(NEAR, fs_near, fr_near)],
                                [(FAR, fs_far, fr_far), (NEAR, fs_near, fr_near)], proj, "gather_finish_near_1",
                                own=False)
    proj = _mm_slots(u1, w_in_lands[0], near_slots[2:], over="n", tm=tm_half, base=proj, name="mm_proj_near")
    wg["w_in"] = w_in_lands[0]
    conv_c = _conv_fwd(proj, wdw_full, b_dw, lp, conv_w, pool_w)
    s_act = _ln_silu(conv_c, conv_ln_g, conv_ln_b, lp)
    gather_whole(2, lambda: proj, lambda: s_act)
    wpg_full = wg["w_pool_grp"].reshape(N_DEV, n_grp, grp_rows, gdim).transpose(1, 0, 2, 3).reshape(n_grp, gdim, gdim)
    w_o_full = wg["w_o"].reshape(d, d)
    ya_pre = _pool_fwd(proj, wpg_full, pool_scale, lp)
    m_mix, y_a, y_b = _gate_mix(ya_pre, s_act, wg["w_pool_out"], wg["w_conv_out"], proj, d, "gate_mix")
    o = _mm(m_mix, w_o_full, mode="nn", tm=tm_half, tn=512, name="mm_o")
    h1, u2 = _post_mix(o, h0, g_post_mix, g_pre_mlp, lp)
    relu2 = lambda acc: jnp.square(jnp.maximum(acc, 0.0))
    (w_up_near,) = gather_near(3, u2)
    act = _mm_slots(u2, w_up_near, near_slots, over="n", tm=tm_half, out_dtype=BF16, epilogue=relu2, name="mm_up_near")
    (w_up_all,) = gather_far(3, [w_up_near], act)
    act = _mm_slots(u2, w_up_all, far_slots, over="n", tm=tm_half, out_dtype=BF16, epilogue=relu2, base=act,
                    name="mm_up_far")
    (w_down_near,) = gather_near(4, act)
    f = _mm_slots(act, w_down_near, near_slots, over="k", tm=tm_half, tn=d, name="mm_down_near")
    (w_down_all,) = gather_far(4, [w_down_near], f)
    f = _mm_slots(act, w_down_all, far_slots, over="k", tm=tm_half // 2, tn=d, base=f, name="mm_down_far")
    w_down_full = w_down_all.reshape(-1, d)

    big_out = {}

    def to_sibling(names, grads, after, tag):
        send, recv, grads, lands, token = _pair_exchange_start(grads, after, "grads_to_sibling_start_" + tag)
        return (names, send, recv, grads, lands, tag), token

    where = jnp.stack([c_idx, chip_idx, 2 * (1 - x_idx) + y_idx, 2 * x_idx + (1 - y_idx),
                       2 * (1 - x_idx) + (1 - y_idx)]).astype(jnp.int32)

    def to_owner(handle, after):
        names, send, recv, grads, from_sib, tag = handle
        grads, from_sib = _pair_exchange_finish(grads, from_sib, send, recv, after, "grads_to_sibling_finish_" + tag)
        own = [g.reshape((4, 2) + g.shape[1:]) for g in grads]
        sums = [_pair_sum(o, r, where, "pair_sum_" + k) for k, o, r in zip(names, own, from_sib)]
        send, recv, sums, lands, token = _chip_exchange_start(sums, after, "grads_to_owner_start_" + tag,
                                                              BARRIER_IDS["owner_" + tag])
        return (names, send, recv, sums, lands, own, from_sib, tag), token

    def update(handle, after):
        names, send, recv, sums, lands, own, from_sib, tag = handle
        got = _chip_exchange_finish(sums, lands, send, recv, after, "grads_to_owner_finish_" + tag)
        for k, o, s, r3 in zip(names, own, from_sib, got):
            w2 = big[k]
            shape = moments[k][0].shape
            outs = _adamw_big(w2, moments[k][0].reshape(w2.shape), moments[k][1].reshape(w2.shape), o, s, r3,
                              where, "adamw_" + k)
            big_out[k] = [a.reshape(shape) for a in outs]
        return big_out[names[-1]][0]

    df, dh, dg_post_mlp, loss_part = _loss_head(f, h1, target, g_post_mlp, lp, seq)
    d_up = _mm(df, w_down_full, mode="nt", tm=tm_half, tn=1024, out_dtypes=(BF16,), extras=[(act, 0)],
               epilogue=lambda acc, a: (acc * (2.0 * jnp.sqrt(a.astype(F32))),), row_splits=2, name="mm_d_up")
    g_w_down = _mm(act, df, mode="tn", tm=1024, tn=1024, out_dtypes=(BF16,), name="mm_g_down")
    sib_down, token = to_sibling(["w_down"], [g_w_down.reshape(N_DEV, -1, d)], slot_idx, "down")
    g_w_up = _mm(u2, d_up, mode="tn", tm=1024, tn=wg["w_up"].shape[2], out_blocked=True, out_dtypes=(BF16,),
                 after=token, name="mm_g_up")
    sib_up, token = to_sibling(["w_up"], [g_w_up], slot_idx, "up")
    pending_down, token = to_owner(sib_down, token)
    du2 = _mm(d_up, wg["w_up"], mode="nt", tm=tm_half // 2, tn=1024, b_blocked=True, kb=4, after=token, name="mm_du2")
    pending_up, token = to_owner(sib_up, du2)
    dh1, do, dg_pre_mlp, dg_post_mix = _mid_bwd(du2, h1, dh, o, g_pre_mlp, g_post_mix, lp)

    def gate_bwd(dm, ga, gb, ya, yb):
        sa, sb = jax.nn.sigmoid(ga), jax.nn.sigmoid(gb)
        return (dm * ya.astype(F32) * sa * (1.0 - sa), dm * yb.astype(F32) * sb * (1.0 - sb), dm * sa, dm * sb)

    gate_tn = 512
    ga_col0, gb_col0 = proj.shape[1] - 2 * d, proj.shape[1] - d
    dproj, d_gb, d_ya, d_yb = _mm(
        do, w_o_full, mode="nt", tm=tm_half, tn=gate_tn, out_dtypes=(BF16,) * 4,
        extras=[(proj, ga_col0 // gate_tn), (proj, gb_col0 // gate_tn), (y_a, 0), (y_b, 0)], epilogue=gate_bwd,
        out_places=((proj.shape[1], ga_col0), None, None, None), after=token, row_splits=2, name="mm_dm")
    g_w_o = _mm(m_mix, do, mode="tn", tm=1024, tn=1024, out_dtypes=(BF16,), name="mm_g_o")
    bn_out = wg["w_pool_out"].shape[2]
    g_w_pool_out = _mm(ya_pre, d_ya, mode="tn", tm=pool_w, tn=4 * bn_out, out_blocked=True, out_block=bn_out,
                       out_dtypes=(BF16,), name="mm_g_pool_out")
    g_w_conv_out = _mm(s_act, d_yb, mode="tn", tm=conv_w, tn=4 * bn_out, out_blocked=True, out_block=bn_out,
                       out_dtypes=(BF16,), name="mm_g_conv_out")
    sib_mix, token = to_sibling(["w_o", "w_pool_out", "w_conv_out"],
                                [g_w_o.reshape(N_DEV, -1, d), g_w_pool_out, g_w_conv_out], slot_idx, "mix")
    d_ya_pre = _mm(d_ya, wg["w_pool_out"], mode="nt", tm=tm_half, tn=pool_w, b_blocked=True, kb=4, after=token,
                   name="mm_d_ya_pre")
    d_s = _mm(d_yb, wg["w_conv_out"], mode="nt", tm=tm_half, tn=conv_w, b_blocked=True, kb=4, name="mm_d_s")
    pending_mix, token = to_owner(sib_mix, d_s)
    dproj, g_wpg, d_scale = _pool_bwd(proj, d_ya_pre, wpg_full, pool_scale, dproj, lp)
    dc, d_ln_g, d_ln_b, d_b_dw = _ln_silu_bwd(conv_c, d_s, conv_ln_g, conv_ln_b, lp)
    dproj, dgc, g_wdw = _conv_bwd(proj, dc, wdw_full, dproj, lp, conv_w, pool_w)
    dproj = _place_columns(dproj, [(dgc, pool_w + conv_w), (d_gb, gb_col0)], "place_dproj")
    g_w_in = _mm(u1, dproj, mode="tn", tm=1024, tn=wg["w_in"].shape[2], out_blocked=True, out_dtypes=(BF16,),
                 after=token, name="mm_g_in")
    g_wpg_slots = g_wpg.astype(BF16).reshape(n_grp, N_DEV, grp_rows, gdim).transpose(1, 0, 2, 3)
    sib_in, token = to_sibling(["w_pool_grp", "w_in"],
                               [g_wpg_slots.reshape(N_DEV, n_grp * grp_rows, gdim), g_w_in], slot_idx, "in")
    done = update(pending_down, token)
    pending_in, token = to_owner(sib_in, done)
    done = update(pending_up, token)
    du1 = _mm(dproj, wg["w_in"], mode="nt", tm=tm_half // 2, tn=1024, b_blocked=True, kb=4, after=done, name="mm_du1")
    grad_x2, grad_meta_part, dg_pre_mix = _pre_mix_bwd(du1, h0, dh1, g_pre_mix, seq)
    grad_x = grad_x2[None]

    assert pool_w + conv_w == d and conv_w <= d and LANES <= d
    widen = lambda a: jnp.pad(a, ((0, 0), (0, d - a.shape[1])))
    packed = jnp.concatenate([
        dg_pre_mix, dg_post_mix, dg_pre_mlp, dg_post_mlp,
        jnp.concatenate([d_scale, d_ln_g], axis=1), jnp.concatenate([d_ln_b, d_b_dw], axis=1),
        grad_meta_part, widen(g_wdw), widen(loss_part)], axis=0)
    rep = dict(g_pre_mix=((g_pre_mix, m_g_pre_mix, v_g_pre_mix), (0, 0)),
               g_post_mix=((g_post_mix, m_g_post_mix, v_g_post_mix), (SUBLANES, 0)),
               g_pre_mlp=((g_pre_mlp, m_g_pre_mlp, v_g_pre_mlp), (2 * SUBLANES, 0)),
               g_post_mlp=((g_post_mlp, m_g_post_mlp, v_g_post_mlp), (3 * SUBLANES, 0)),
               pool_scale=((pool_scale, m_pool_scale, v_pool_scale), (4 * SUBLANES, 0)),
               conv_ln_g=((conv_ln_g, m_conv_ln_g, v_conv_ln_g), (4 * SUBLANES, pool_w)),
               conv_ln_b=((conv_ln_b, m_conv_ln_b, v_conv_ln_b), (5 * SUBLANES, 0)),
               b_dw=((b_dw, m_b_dw, v_b_dw), (5 * SUBLANES, conv_w)))
    meta_row0 = 6 * SUBLANES
    wdw_row0 = meta_row0 + N_META
    loss_row0 = wdw_row0 + CONV_TAPS_PADDED
    (small_started,), token = _gather_start([[_fill_slot(packed, slot_idx, F32, slot_idx, "fill_small")]], grad_x2,
                                            "gather_small_start", BARRIER_IDS["gather_small"])
    done = update(pending_mix, token)
    done = update(pending_in, done)
    send, recv, lands = small_started
    fsend, frecv, lands = _gather_forward(lands, recv, done, "gather_small_forward")
    (packed_all,) = _gather_finish(lands, send, recv, [(ALL_CHIPS, fsend, frecv)], [(ALL_CHIPS, fsend, frecv)], done,
                                   "gather_small_finish")
    rep_names = list(rep)
    wdw_wmv = [jnp.pad(a[0], pad_taps) for a in (w_dw, m_w_dw, v_w_dw)]
    rep_out, meta_out, wdw_out, loss_blk = _small_update(
        slot_idx, packed_all, [rep[k][0] for k in rep_names], [rep[k][1] for k in rep_names],
        (meta, m_meta, v_meta), meta_row0, wdw_wmv, wdw_row0, loss_row0)
    small_out = dict(zip(rep_names, rep_out))
    small_out["meta"] = meta_out
    small_out["w_dw"] = [a[:CONV_KERNEL][None] for a in wdw_out]

    order = ["meta", "g_pre_mix", "w_in", "w_pool_grp", "pool_scale", "w_pool_out", "w_dw", "b_dw", "conv_ln_g",
             "conv_ln_b", "w_conv_out", "w_o", "g_post_mix", "g_pre_mlp", "w_up", "w_down", "g_post_mlp"]
    by_name = {**big_out, **small_out}
    result = [loss_blk[0, 0], grad_x]
    for kind in range(4):
        result += [by_name[k][kind] for k in order]
    return tuple(result)
```

```python
import functools

import jax
import jax.numpy as jnp
from jax import lax
from jax.experimental import pallas as pl
from jax.experimental.pallas import tpu as pltpu

F32 = jnp.float32
BF16 = jnp.bfloat16
MESH = pl.DeviceIdType.MESH

N_DEV = 8
N_META = 16
POOL_WINDOWS = (2, 4, 8, 16)
CONV_KERNEL = 31
CONV_TAPS_PADDED = 32
RMS_EPS = 1e-6
LN_EPS = 1e-5
ADAM_LR = 0.001
ADAM_B1 = 0.9
ADAM_B2 = 0.999
ADAM_EPS = 1e-08
ADAM_WD = 0.01
ADAM_STEP = 10

LANES = 128
SUBLANES = 8
ROW_CHUNK = 128
HALO = 32
VMEM_LIMIT_BYTES = 56 * 1024 * 1024


def _cparams(*sem):
    return pltpu.CompilerParams(dimension_semantics=sem if sem else None, vmem_limit_bytes=VMEM_LIMIT_BYTES)


def _round_up(n, m):
    return (n + m - 1) // m * m


def _row_tile(rows, cols, max_elems=640 * 1024):
    best = None
    for t in range(16, rows + 1, 16):
        if rows % t == 0 and (best is None or t * cols <= max_elems):
            best = t
    assert best is not None, (rows, cols)
    return best


def _rowsum8(a):
    t, w = a.shape
    return a.reshape(t // SUBLANES, SUBLANES, w).sum(axis=0)


def _mesh_pos():
    return lax.axis_index("x"), lax.axis_index("y"), lax.axis_index("c")


HBM_SPEC = pl.BlockSpec(memory_space=pltpu.HBM)
SEM_SPEC = pl.BlockSpec(memory_space=pltpu.SEMAPHORE)
ANY_SPEC = pl.BlockSpec(memory_space=pl.ANY)
_DATAFLOW = pltpu.SideEffectType.DATAFLOW_SIDE_EFFECTING


SIBLING_BARRIER_ID = 1
BARRIER_IDS = {name: 2 + i for i, name in enumerate(
    ["gather_first", "gather_rest", "gather_small", "owner_down", "owner_up", "owner_mix", "owner_in"])}


def _sibling_handshake():
    x, y, c = _mesh_pos()
    barrier = pltpu.get_barrier_semaphore()
    pl.semaphore_signal(barrier, inc=1, device_id=(x, y, 1 - c), device_id_type=MESH)
    pl.semaphore_wait(barrier, 1)


def _peer_handshake(peers):
    barrier = pltpu.get_barrier_semaphore()
    for peer in peers:
        pl.semaphore_signal(barrier, inc=1, device_id=peer, device_id_type=MESH)
    pl.semaphore_wait(barrier, len(peers))


def _hbm(a):
    return pltpu.with_memory_space_constraint(a, pltpu.HBM)


def _slot(p):
    return 4 * p[0] + 2 * p[1] + p[2]


def _fill_slot(w, slot_idx, dtype, after, name):
    rows, cols = w.shape
    tr = _row_tile(rows, cols) if rows % 16 == 0 else rows

    def body(idx_ref, w_ref, _, out_ref):
        out_ref[...] = w_ref[...].astype(dtype)

    return pl.pallas_call(
        body, name=name,
        grid_spec=pltpu.PrefetchScalarGridSpec(
            num_scalar_prefetch=1, grid=(rows // tr,),
            in_specs=[pl.BlockSpec((tr, cols), lambda i, idx_ref: (i, 0)), ANY_SPEC],
            out_specs=pl.BlockSpec((None, tr, cols), lambda i, idx_ref: (idx_ref[0], i, 0))),
        out_shape=jax.ShapeDtypeStruct((N_DEV, rows, cols), dtype),
        compiler_params=_cparams("parallel"),
    )(slot_idx, w, after)


def _gather_start(groups, after, name, barrier_id, issue_order=(0, 1, 2, 3)):
    flat = [g for grp in groups for g in grp]
    n, n_grp = len(flat), len(groups)

    def body(*refs):
        lands = refs[:n]
        sems = refs[n + 1:n + 1 + 2 * n_grp]
        token = refs[-1]
        x, y, c = _mesh_pos()
        targets = [(x, y, 1 - c), (1 - x, y, c), (x, 1 - y, c), (1 - x, 1 - y, c)]
        _peer_handshake(targets)
        t = 0
        for gi, grp in enumerate(groups):
            for ti in range(len(grp)):
                mine = lands[t].at[_slot((x, y, c))]
                for k in issue_order:
                    pltpu.make_async_remote_copy(
                        src_ref=mine, dst_ref=mine,
                        send_sem=sems[2 * gi].at[4 * ti + k], recv_sem=sems[2 * gi + 1].at[4 * ti + k],
                        device_id=targets[k], device_id_type=MESH).start()
                t += 1
        token[...] = jnp.zeros_like(token)

    sem_shapes = []
    for grp in groups:
        sem_shapes += [pltpu.SemaphoreType.DMA((4 * len(grp),))] * 2
    outs = pl.pallas_call(
        body, name=name,
        out_shape=tuple(sem_shapes + [pltpu.HBM(g.shape, g.dtype) for g in flat]
                        + [jax.ShapeDtypeStruct((SUBLANES, LANES), F32)]),
        in_specs=tuple([HBM_SPEC] * n + [ANY_SPEC]),
        out_specs=tuple([SEM_SPEC] * (2 * n_grp) + [HBM_SPEC] * n + [pl.BlockSpec(memory_space=pltpu.VMEM)]),
        input_output_aliases={i: 2 * n_grp + i for i in range(n)},
        compiler_params=pltpu.CompilerParams(has_side_effects=_DATAFLOW, collective_id=barrier_id),
    )(*[_hbm(g) for g in flat], after)
    sems, lands, token = outs[:2 * n_grp], outs[2 * n_grp:-1], outs[-1]
    res, t = [], 0
    for gi, grp in enumerate(groups):
        res.append((sems[2 * gi], sems[2 * gi + 1], list(lands[t:t + len(grp)])))
        t += len(grp)
    return res, token


NEAR, FAR, ALL_CHIPS = (0, 1), (2,), (0, 1, 2)


def _gather_forward(lands, recv_sems, after, name, which=ALL_CHIPS):
    n, nw = len(lands), len(which)

    def body(*refs):
        land_refs, recv, _ = refs[:n], refs[n], refs[n + 1]
        fsend, frecv = refs[n + 2], refs[n + 3]
        x, y, c = _mesh_pos()
        chips = [(1 - x, y), (x, 1 - y), (1 - x, 1 - y)]
        _sibling_handshake()
        for t in range(n):
            for i, j in enumerate(which):
                blk = land_refs[t].at[_slot((*chips[j], c))]
                pltpu.make_async_remote_copy(src_ref=blk, dst_ref=blk, send_sem=fsend.at[nw * t + i],
                                             recv_sem=recv.at[4 * t + 1 + j],
                                             device_id=(x, y, 1 - c), device_id_type=MESH).wait_recv()
                pltpu.make_async_remote_copy(src_ref=blk, dst_ref=blk, send_sem=fsend.at[nw * t + i],
                                             recv_sem=frecv.at[nw * t + i],
                                             device_id=(x, y, 1 - c), device_id_type=MESH).start()

    outs = pl.pallas_call(
        body, name=name,
        out_shape=tuple([pltpu.SemaphoreType.DMA((nw * n,))] * 2 + [pltpu.HBM(g.shape, g.dtype) for g in lands]),
        in_specs=tuple([HBM_SPEC] * n + [SEM_SPEC, ANY_SPEC]),
        out_specs=tuple([SEM_SPEC] * 2 + [HBM_SPEC] * n),
        input_output_aliases={i: 2 + i for i in range(n)},
        compiler_params=pltpu.CompilerParams(has_side_effects=_DATAFLOW, collective_id=SIBLING_BARRIER_ID),
    )(*lands, recv_sems, after)
    return outs[0], outs[1], list(outs[2:])


def _gather_finish(lands, send_sems, recv_sems, arrivals, sends, after, name, own=True, direct_sends=True):
    n = len(lands)
    fwd = list(arrivals) + list(sends)
    sem_args = [send_sems, recv_sems]
    where = []
    for _, fs, fr in fwd:
        pos = []
        for arr in (fs, fr):
            hit = [i for i, have in enumerate(sem_args) if have is arr]
            if not hit:
                sem_args.append(arr)
                hit = [len(sem_args) - 1]
            pos.append(hit[0])
        where.append(pos)

    def body(*refs):
        land_refs = refs[:n]
        send, recv = refs[n], refs[n + 1]
        fwd_refs = [refs[n + p] for pos in where for p in pos]
        x, y, c = _mesh_pos()
        sibling = (x, y, 1 - c)
        chips = [(1 - x, y), (x, 1 - y), (1 - x, 1 - y)]

        def desc(ref, s_sem, r_sem):
            return pltpu.make_async_remote_copy(src_ref=ref, dst_ref=ref, send_sem=s_sem, recv_sem=r_sem,
                                                device_id=sibling, device_id_type=MESH)

        for t in range(n):
            mine = land_refs[t].at[_slot((x, y, c))]
            if own:
                desc(land_refs[t].at[_slot(sibling)], send.at[4 * t], recv.at[4 * t]).wait_recv()
            for a, (which, _, _) in enumerate(fwd):
                fs, fr = fwd_refs[2 * a], fwd_refs[2 * a + 1]
                for i, j in enumerate(which):
                    if a < len(arrivals):
                        desc(land_refs[t].at[_slot((*chips[j], 1 - c))], fs.at[len(which) * t + i],
                             fr.at[len(which) * t + i]).wait_recv()
                    else:
                        desc(land_refs[t].at[_slot((*chips[j], c))], fs.at[len(which) * t + i],
                             fr.at[len(which) * t + i]).wait_send()
            if direct_sends:
                for k in range(4):
                    desc(mine, send.at[4 * t + k], recv.at[4 * t + k]).wait_send()

    outs = pl.pallas_call(
        body, name=name,
        out_shape=tuple(pltpu.HBM(g.shape, g.dtype) for g in lands),
        in_specs=tuple([HBM_SPEC] * n + [SEM_SPEC] * len(sem_args) + [ANY_SPEC]),
        out_specs=tuple([HBM_SPEC] * n),
        input_output_aliases={i: i for i in range(n)},
        compiler_params=pltpu.CompilerParams(has_side_effects=_DATAFLOW),
    )(*lands, *sem_args, after)
    return list(outs)


def _pair_exchange_start(grads, after, name):
    n = len(grads)

    def body(*refs):
        ins, lands = refs[:n], refs[n:2 * n]
        send, recv = refs[2 * n + 1], refs[2 * n + 2]
        token = refs[-1]
        x, y, c = _mesh_pos()
        _sibling_handshake()
        for t in range(n):
            for q in range(4):
                pltpu.make_async_remote_copy(
                    src_ref=ins[t].at[2 * q + 1 - c], dst_ref=lands[t].at[q],
                    send_sem=send.at[4 * t + q], recv_sem=recv.at[4 * t + q],
                    device_id=(x, y, 1 - c), device_id_type=MESH).start()
        token[...] = jnp.zeros_like(token)

    land_shapes = [(4,) + g.shape[1:] for g in grads]
    outs = pl.pallas_call(
        body, name=name,
        out_shape=tuple([pltpu.SemaphoreType.DMA((4 * n,))] * 2 + [pltpu.HBM(g.shape, g.dtype) for g in grads]
                        + [pltpu.HBM(ls, g.dtype) for ls, g in zip(land_shapes, grads)]
                        + [jax.ShapeDtypeStruct((SUBLANES, LANES), F32)]),
        in_specs=tuple([HBM_SPEC] * (2 * n) + [ANY_SPEC]),
        out_specs=tuple([SEM_SPEC] * 2 + [HBM_SPEC] * (2 * n) + [pl.BlockSpec(memory_space=pltpu.VMEM)]),
        input_output_aliases={i: 2 + i for i in range(2 * n)},
        compiler_params=pltpu.CompilerParams(has_side_effects=_DATAFLOW, collective_id=SIBLING_BARRIER_ID),
    )(*[_hbm(g) for g in grads], *[_hbm(lax.empty(ls, g.dtype)) for ls, g in zip(land_shapes, grads)], after)
    return outs[0], outs[1], list(outs[2:2 + n]), list(outs[2 + n:2 + 2 * n]), outs[-1]


def _pair_exchange_finish(grads, lands, send_sems, recv_sems, after, name):
    n = len(grads)

    def body(*refs):
        ins, land_refs = refs[:n], refs[n:2 * n]
        send, recv = refs[2 * n], refs[2 * n + 1]
        x, y, c = _mesh_pos()
        for t in range(n):
            for q in range(4):
                cp = pltpu.make_async_remote_copy(
                    src_ref=ins[t].at[q], dst_ref=land_refs[t].at[q], send_sem=send.at[4 * t + q],
                    recv_sem=recv.at[4 * t + q], device_id=(x, y, 1 - c), device_id_type=MESH)
                cp.wait_send()
                cp.wait_recv()

    outs = pl.pallas_call(
        body, name=name,
        out_shape=tuple([pltpu.HBM(g.shape, g.dtype) for g in grads] + [pltpu.HBM(g.shape, g.dtype) for g in lands]),
        in_specs=tuple([HBM_SPEC] * (2 * n) + [SEM_SPEC] * 2 + [ANY_SPEC]),
        out_specs=tuple([HBM_SPEC] * (2 * n)),
        input_output_aliases={i: i for i in range(2 * n)},
        compiler_params=pltpu.CompilerParams(has_side_effects=_DATAFLOW),
    )(*grads, *lands, send_sems, recv_sems, after)
    return list(outs[:n]), list(outs[n:])


def _chip_exchange_start(sums, after, name, barrier_id):
    n = len(sums)

    def body(*refs):
        ins, lands = refs[:n], refs[n:2 * n]
        send, recv = refs[2 * n + 1], refs[2 * n + 2]
        token = refs[-1]
        x, y, c = _mesh_pos()
        chips = [(1 - x, y), (x, 1 - y), (1 - x, 1 - y)]
        _peer_handshake([(*chip, c) for chip in chips])
        for t in range(n):
            for r, chip in enumerate(chips):
                pltpu.make_async_remote_copy(
                    src_ref=ins[t].at[2 * chip[0] + chip[1]], dst_ref=lands[t].at[r],
                    send_sem=send.at[3 * t + r], recv_sem=recv.at[3 * t + r],
                    device_id=(*chip, c), device_id_type=MESH).start()
        token[...] = jnp.zeros_like(token)

    land_shapes = [(3,) + s.shape[1:] for s in sums]
    outs = pl.pallas_call(
        body, name=name,
        out_shape=tuple([pltpu.SemaphoreType.DMA((3 * n,))] * 2 + [pltpu.HBM(s.shape, s.dtype) for s in sums]
                        + [pltpu.HBM(ls, s.dtype) for ls, s in zip(land_shapes, sums)]
                        + [jax.ShapeDtypeStruct((SUBLANES, LANES), F32)]),
        in_specs=tuple([HBM_SPEC] * (2 * n) + [ANY_SPEC]),
        out_specs=tuple([SEM_SPEC] * 2 + [HBM_SPEC] * (2 * n) + [pl.BlockSpec(memory_space=pltpu.VMEM)]),
        input_output_aliases={i: 2 + i for i in range(2 * n)},
        compiler_params=pltpu.CompilerParams(has_side_effects=_DATAFLOW, collective_id=barrier_id),
    )(*[_hbm(s) for s in sums], *[_hbm(lax.empty(ls, s.dtype)) for ls, s in zip(land_shapes, sums)], after)
    return outs[0], outs[1], list(outs[2:2 + n]), list(outs[2 + n:2 + 2 * n]), outs[-1]


def _chip_exchange_finish(sums, lands, send_sems, recv_sems, after, name):
    n = len(sums)

    def body(*refs):
        ins, land_refs = refs[:n], refs[n:2 * n]
        send, recv = refs[2 * n], refs[2 * n + 1]
        x, y, c = _mesh_pos()
        for t in range(n):
            for r in range(3):
                cp = pltpu.make_async_remote_copy(
                    src_ref=ins[t].at[r], dst_ref=land_refs[t].at[r], send_sem=send.at[3 * t + r],
                    recv_sem=recv.at[3 * t + r],
                    device_id=(x, y, 1 - c), device_id_type=MESH)
                cp.wait_send()
                cp.wait_recv()

    outs = pl.pallas_call(
        body, name=name,
        out_shape=tuple(pltpu.HBM(g.shape, g.dtype) for g in lands),
        in_specs=tuple([HBM_SPEC] * (2 * n) + [SEM_SPEC] * 2 + [ANY_SPEC]),
        out_specs=tuple([HBM_SPEC] * n),
        input_output_aliases={n + i: i for i in range(n)},
        compiler_params=pltpu.CompilerParams(has_side_effects=_DATAFLOW),
    )(*sums, *lands, send_sems, recv_sems, after)
    return list(outs)


def _mm(a, b, *, mode, tm, tn, tk=None, b_blocked=False, out_blocked=False, out_dtypes=(F32,),
        epilogue=None, extras=(), after=None, kb=1, out_places=None, row_splits=1, out_block=None, name):
    if mode == "nn":
        m, k = a.shape
        n = b.shape[0] * b.shape[2] if b_blocked else b.shape[1]
        dims = (((1,), (0,)), ((), ()))
    elif mode == "nt":
        m, k = a.shape
        n = b.shape[1] if b_blocked else b.shape[0]
        if b_blocked:
            tk = kb * b.shape[2]
        dims = (((1,), (1,)), ((), ()))
    else:
        k, m = a.shape
        n = b.shape[1]
        dims = (((0,), (0,)), ((), ()))
    tk = k if tk is None else tk
    assert m % tm == 0 and n % tn == 0 and k % tk == 0, (name, m, n, k, tm, tn, tk)
    gm, gn, gk = m // tm, n // tn, k // tk
    if b_blocked:
        assert (tn if mode == "nn" else tk) == kb * b.shape[2], name
    if row_splits > 1:
        assert gk == 1 and epilogue is not None and mode != "tn" and not b_blocked and tm % (16 * row_splits) == 0, name

    if mode == "nn":
        a_spec = pl.BlockSpec((tm, tk), lambda i, j, kk: (i, kk))
        b_spec = (pl.BlockSpec((None, tk, tn), lambda i, j, kk: (j, kk, 0)) if b_blocked
                  else pl.BlockSpec((tk, tn), lambda i, j, kk: (kk, j)))
    elif mode == "nt":
        a_spec = pl.BlockSpec((tm, tk), lambda i, j, kk: (i, kk))
        b_spec = (pl.BlockSpec((kb, tn, tk // kb), lambda i, j, kk: (kk, j, 0)) if b_blocked
                  else pl.BlockSpec((tn, tk), lambda i, j, kk: (j, kk)))
    else:
        a_spec = pl.BlockSpec((tk, tm), lambda i, j, kk: (kk, i))
        b_spec = pl.BlockSpec((tk, tn), lambda i, j, kk: (kk, j))
    out_pack = 1
    if out_blocked and out_block is not None and out_block != tn:
        assert tn % out_block == 0, name
        out_pack = tn // out_block
        out_spec = pl.BlockSpec((out_pack, tm, out_block), lambda i, j, kk: (j, i, 0))
        out_shape = (n // out_block, m, out_block)
    elif out_blocked:
        out_spec = pl.BlockSpec((None, tm, tn), lambda i, j, kk: (j, i, 0))
        out_shape = (gn, m, tn)
    else:
        out_spec = pl.BlockSpec((tm, tn), lambda i, j, kk: (i, j))
        out_shape = (m, n)
    extra_specs = [pl.BlockSpec((tm, tn), functools.partial(lambda i, j, kk, off: (i, j + off), off=off))
                   for _, off in extras]
    n_extra, n_out = len(extras), len(out_dtypes)
    n_after = 0 if after is None else 1
    places = out_places if out_places is not None else (None,) * n_out

    def body(a_ref, b_ref, *rest):
        extra_refs = rest[:n_extra]
        out_refs = rest[n_extra + n_after:n_extra + n_after + n_out]

        def finish(acc):
            if epilogue is None:
                res = (acc,)
            else:
                res = epilogue(acc, *[e[...] for e in extra_refs])
            for o_ref, r in zip(out_refs, res):
                if out_pack == 1:
                    o_ref[...] = r.astype(o_ref.dtype)
                else:
                    for h in range(out_pack):
                        o_ref[h] = r[:, h * out_block:(h + 1) * out_block].astype(o_ref.dtype)

        if row_splits > 1:
            strip = tm // row_splits
            for h in range(row_splits):
                rows = slice(h * strip, (h + 1) * strip)
                acc = lax.dot_general(a_ref[rows, :], b_ref[...], dims, preferred_element_type=F32)
                res = epilogue(acc, *[e[rows, :] for e in extra_refs])
                for o_ref, r in zip(out_refs, res):
                    o_ref[rows, :] = r.astype(o_ref.dtype)
            return
        if mode == "nt" and b_blocked:
            bk = tk // kb
            part = lax.dot_general(a_ref[:, :bk], b_ref[0], dims, preferred_element_type=F32)
            for h in range(1, kb):
                part = part + lax.dot_general(a_ref[:, h * bk:(h + 1) * bk], b_ref[h], dims,
                                              preferred_element_type=F32)
        else:
            part = lax.dot_general(a_ref[...], b_ref[...], dims, preferred_element_type=F32)
        if gk == 1:
            finish(part)
        else:
            acc_ref = rest[-1]
            kk = pl.program_id(2)

            @pl.when(kk == 0)
            def _():
                acc_ref[...] = part

            @pl.when(kk > 0)
            def _():
                acc_ref[...] += part

            @pl.when(kk == gk - 1)
            def _():
                finish(acc_ref[...])

    outs = pl.pallas_call(
        body, name=name, grid=(gm, gn, gk),
        in_specs=[a_spec, b_spec] + extra_specs + [ANY_SPEC] * n_after,
        out_specs=[out_spec if place is None else
                   pl.BlockSpec((tm, tn), functools.partial(lambda i, j, kk, off: (i, j + off), off=place[1] // tn))
                   for place in places],
        out_shape=[jax.ShapeDtypeStruct(out_shape if place is None else (m, place[0]), dt)
                   for dt, place in zip(out_dtypes, places)],
        scratch_shapes=[pltpu.VMEM((tm, tn), F32)] if gk > 1 else [],
        compiler_params=_cparams("parallel", "parallel", "arbitrary"),
    )(a, b, *[e for e, _ in extras], *([] if after is None else [after]))
    return outs[0] if n_out == 1 else outs


def _mm_slots(a, w, slots, *, over, tm, tn=None, out_dtype=F32, epilogue=None, base=None, name):
    m = a.shape[0]
    ns = slots.shape[0]
    n_slots, w1, w2 = w.shape
    assert m % tm == 0
    if over == "n":
        k, bn = w1, w2

        def body(slots_ref, a_ref, w_ref, *rest):
            out_ref = rest[-1]
            acc = jnp.dot(a_ref[...], w_ref[...], preferred_element_type=F32)
            out_ref[...] = (acc if epilogue is None else epilogue(acc)).astype(out_ref.dtype)

        in_specs = [pl.BlockSpec((tm, k), lambda i, j, s: (i, 0)),
                    pl.BlockSpec((None, k, bn), lambda i, j, s: (s[j], 0, 0))]
        args = [a, w]
        aliases = {}
        if base is not None:
            in_specs.append(ANY_SPEC)
            args.append(base)
            aliases = {3: 0}
        return pl.pallas_call(
            body, name=name,
            grid_spec=pltpu.PrefetchScalarGridSpec(
                num_scalar_prefetch=1, grid=(m // tm, ns), in_specs=in_specs,
                out_specs=pl.BlockSpec((tm, bn), lambda i, j, s: (i, s[j]))),
            out_shape=jax.ShapeDtypeStruct((m, n_slots * bn), out_dtype),
            input_output_aliases=aliases,
            compiler_params=_cparams("parallel", "arbitrary"),
        )(slots, *args)

    bk, n = w1, w2
    tn = n if tn is None else tn
    assert n % tn == 0

    def body(slots_ref, a_ref, w_ref, *rest):
        out_ref, acc_ref = rest[-2], rest[-1]
        kk = pl.program_id(2)
        part = jnp.dot(a_ref[...], w_ref[...], preferred_element_type=F32)

        @pl.when(kk == 0)
        def _():
            acc_ref[...] = part if base is None else part + rest[0][...]

        @pl.when(kk > 0)
        def _():
            acc_ref[...] += part

        @pl.when(kk == ns - 1)
        def _():
            out_ref[...] = acc_ref[...].astype(out_ref.dtype)

    in_specs = [pl.BlockSpec((tm, bk), lambda i, j, kk, s: (i, s[kk])),
                pl.BlockSpec((None, bk, tn), lambda i, j, kk, s: (s[kk], 0, j))]
    args = [a, w]
    if base is not None:
        in_specs.append(pl.BlockSpec((tm, tn), lambda i, j, kk, s: (i, j)))
        args.append(base)
    return pl.pallas_call(
        body, name=name,
        grid_spec=pltpu.PrefetchScalarGridSpec(
            num_scalar_prefetch=1, grid=(m // tm, n // tn, ns), in_specs=in_specs,
            out_specs=pl.BlockSpec((tm, tn), lambda i, j, kk, s: (i, j)),
            scratch_shapes=[pltpu.VMEM((tm, tn), F32)]),
        out_shape=jax.ShapeDtypeStruct((m, n), out_dtype),
        compiler_params=_cparams("parallel", "parallel", "arbitrary"),
    )(slots, *args)


def _gate_mix(ya_pre, s, wpo, wco, proj, d_model, name):
    lp, width = ya_pre.shape
    nb, _, bn = wpo.shape
    ga_off = (proj.shape[1] - 2 * d_model) // bn
    gb_off = (proj.shape[1] - d_model) // bn

    n_strips = 4 if lp % 64 == 0 else 1

    def body(ya_ref, s_ref, wpo_ref, wco_ref, ga_ref, gb_ref, m_ref, y_a_ref, y_b_ref):
        strip = lp // n_strips
        for h in range(n_strips):
            rows = slice(h * strip, (h + 1) * strip)
            y_a = jnp.dot(ya_ref[rows, :], wpo_ref[...], preferred_element_type=F32)
            y_b = jnp.dot(s_ref[rows, :], wco_ref[...], preferred_element_type=F32)
            m = jax.nn.sigmoid(ga_ref[rows, :]) * y_a + jax.nn.sigmoid(gb_ref[rows, :]) * y_b
            m_ref[rows, :] = m.astype(BF16)
            y_a_ref[rows, :] = y_a.astype(BF16)
            y_b_ref[rows, :] = y_b.astype(BF16)

    act_spec = pl.BlockSpec((lp, width), lambda j: (0, 0))
    w_spec = pl.BlockSpec((None, width, bn), lambda j: (j, 0, 0))
    out_spec = pl.BlockSpec((lp, bn), lambda j: (0, j))
    return pl.pallas_call(
        body, name=name, grid=(nb,),
        in_specs=[act_spec, act_spec, w_spec, w_spec,
                  pl.BlockSpec((lp, bn), lambda j: (0, j + ga_off)),
                  pl.BlockSpec((lp, bn), lambda j: (0, j + gb_off))],
        out_specs=[out_spec] * 3,
        out_shape=[jax.ShapeDtypeStruct((lp, nb * bn), BF16)] * 3,
        compiler_params=_cparams("parallel"),
    )(ya_pre, s, wpo, wco, proj, proj)


def _rms_stats(x):
    return lax.rsqrt(jnp.mean(x * x, axis=-1, keepdims=True) + RMS_EPS)


def _rms_bwd(x, g, dy):
    r = _rms_stats(x)
    nrm = x * r
    dn = dy * g
    dx = r * (dn - nrm * jnp.mean(dn * nrm, axis=-1, keepdims=True))
    return dx, dy * nrm


def _rowwise(body, ins, outs, accs, *, lp, name):
    tr = _row_tile(lp, max(a.shape[1] for a in ins))
    n_in, n_out, n_acc = len(ins), len(outs), len(accs)

    def kernel_body(*refs):
        i = pl.program_id(0)
        acc_refs = refs[n_in + n_out:]

        @pl.when(i == 0)
        def _():
            for r in acc_refs:
                r[...] = jnp.zeros_like(r)

        body(i * tr, refs[:n_in], refs[n_in:n_in + n_out], acc_refs)

    in_specs = []
    for a in ins:
        if a.shape[0] == lp:
            in_specs.append(pl.BlockSpec((tr, a.shape[1]), lambda i: (i, 0)))
        else:
            in_specs.append(pl.BlockSpec(a.shape, lambda i: (0, 0)))
    out_specs = [pl.BlockSpec((tr, w), lambda i: (i, 0)) for w, _ in outs]
    out_specs += [pl.BlockSpec((SUBLANES, w), lambda i: (0, 0)) for w in accs]
    out_shape = [jax.ShapeDtypeStruct((lp, w), d) for w, d in outs]
    out_shape += [jax.ShapeDtypeStruct((SUBLANES, w), F32) for w in accs]
    return pl.pallas_call(
        kernel_body, name=name, grid=(lp // tr,), in_specs=in_specs, out_specs=out_specs,
        out_shape=out_shape, compiler_params=_cparams("arbitrary"),
    )(*ins)


SHIFT_TILE = 128


def _shifted_specs(width, n_big, n_small):
    per = SHIFT_TILE // N_META
    small = pl.BlockSpec((N_META, width), lambda i: (jnp.clip(per * i - 1, 0, n_small - 1), 0))
    big = pl.BlockSpec((SHIFT_TILE, width), lambda i: (jnp.minimum(i, n_big - 1), 0))
    return small, big


def _rms_pre(meta_full, x2, g, lp, seq):
    d = x2.shape[1]
    assert seq % SHIFT_TILE == 0 and lp % SHIFT_TILE == 0 and SHIFT_TILE % N_META == 0

    def body(meta_ref, xs_ref, xb_ref, g_ref, h0_ref, u1_ref):
        i = pl.program_id(0)
        head = jnp.where(i == 0, meta_ref[...], xs_ref[...])
        rows = jnp.concatenate([head, xb_ref[:SHIFT_TILE - N_META, :]], axis=0)
        r = i * SHIFT_TILE + lax.broadcasted_iota(jnp.int32, (SHIFT_TILE, 1), 0)
        rows = jnp.where(r < N_META + seq, rows, 0.0)
        h0_ref[...] = rows
        u1_ref[...] = (rows * _rms_stats(rows) * g_ref[...]).astype(BF16)

    small, big = _shifted_specs(d, seq // SHIFT_TILE, seq // N_META)
    tile = pl.BlockSpec((SHIFT_TILE, d), lambda i: (i, 0))
    return pl.pallas_call(
        body, name="rms_pre", grid=(lp // SHIFT_TILE,),
        in_specs=[pl.BlockSpec((N_META, d), lambda i: (0, 0)), small, big, pl.BlockSpec((1, d), lambda i: (0, 0))],
        out_specs=[tile, tile],
        out_shape=[jax.ShapeDtypeStruct((lp, d), F32), jax.ShapeDtypeStruct((lp, d), BF16)],
        compiler_params=_cparams("parallel"),
    )(meta_full, x2, x2, g)


def _post_mix(o, h0, g_post_mix, g_pre_mlp, lp):
    d = h0.shape[1]

    def body(row0, ins, outs, accs):
        o_ref, h0_ref, g1_ref, g2_ref = ins
        o_v = o_ref[...]
        h1 = h0_ref[...] + o_v * _rms_stats(o_v) * g1_ref[...]
        outs[0][...] = h1
        outs[1][...] = (h1 * _rms_stats(h1) * g2_ref[...]).astype(BF16)

    return _rowwise(body, [o, h0, g_post_mix, g_pre_mlp], [(d, F32), (d, BF16)], [], lp=lp, name="post_mix")


def _loss_head(f, h1, target, g_post_mlp, lp, seq):
    d = f.shape[1]

    def body(f_ref, h1_ref, ts_ref, tb_ref, g_ref, df_ref, dh_ref, dg_ref, loss_ref):
        i = pl.program_id(0)

        @pl.when(i == 0)
        def _():
            dg_ref[...] = jnp.zeros_like(dg_ref)
            loss_ref[...] = jnp.zeros_like(loss_ref)

        f_v, g = f_ref[...], g_ref[...]
        r = _rms_stats(f_v)
        nrm = f_v * r
        rows = i * SHIFT_TILE + lax.broadcasted_iota(jnp.int32, (SHIFT_TILE, 1), 0)
        valid = (rows >= N_META) & (rows < N_META + seq)
        tgt = jnp.concatenate([ts_ref[...], tb_ref[:SHIFT_TILE - N_META, :]], axis=0)
        err = jnp.where(valid, h1_ref[...] + nrm * g - tgt, 0.0)
        loss_ref[...] += 0.5 * jnp.sum(jnp.mean(err * err, axis=-1, keepdims=True))
        dy = err * (1.0 / d)
        dn = dy * g
        df_ref[...] = (r * (dn - nrm * jnp.mean(dn * nrm, axis=-1, keepdims=True))).astype(BF16)
        dh_ref[...] = dy
        dg_ref[...] += _rowsum8(dy * nrm)

    small, big = _shifted_specs(d, seq // SHIFT_TILE, seq // N_META)
    tile = pl.BlockSpec((SHIFT_TILE, d), lambda i: (i, 0))
    return pl.pallas_call(
        body, name="loss_head", grid=(lp // SHIFT_TILE,),
        in_specs=[tile, tile, small, big, pl.BlockSpec((1, d), lambda i: (0, 0))],
        out_specs=[tile, tile, pl.BlockSpec((SUBLANES, d), lambda i: (0, 0)),
                   pl.BlockSpec((SUBLANES, LANES), lambda i: (0, 0))],
        out_shape=[jax.ShapeDtypeStruct((lp, d), BF16), jax.ShapeDtypeStruct((lp, d), F32),
                   jax.ShapeDtypeStruct((SUBLANES, d), F32), jax.ShapeDtypeStruct((SUBLANES, LANES), F32)],
        compiler_params=_cparams("arbitrary"),
    )(f, h1, target, target, g_post_mlp)


def _mid_bwd(du2, h1, dh, o, g_pre_mlp, g_post_mix, lp):
    d = h1.shape[1]

    def body(row0, ins, outs, accs):
        du2_ref, h1_ref, dh_ref, o_ref, g2_ref, g1_ref = ins
        dx2, dg2 = _rms_bwd(h1_ref[...], g2_ref[...], du2_ref[...])
        dh1 = dh_ref[...] + dx2
        do, dg1 = _rms_bwd(o_ref[...], g1_ref[...], dh1)
        outs[0][...] = dh1
        outs[1][...] = do.astype(BF16)
        accs[0][...] += _rowsum8(dg2)
        accs[1][...] += _rowsum8(dg1)

    return _rowwise(body, [du2, h1, dh, o, g_pre_mlp, g_post_mix], [(d, F32), (d, BF16)], [d, d], lp=lp,
                    name="mid_bwd")


def _pre_mix_bwd(du1, h0, dh1, g_pre_mix, seq):
    d = h0.shape[1]
    per = SHIFT_TILE // N_META
    assert seq % SHIFT_TILE == 0

    def body(du_b, h_b, dh_b, du_n, h_n, dh_n, du_m, h_m, dh_m, g_ref, gx_ref, gm_ref, dg_ref):
        i = pl.program_id(0)
        g = g_ref[...]

        @pl.when(i == 0)
        def _():
            dx, dg = _rms_bwd(h_m[...], g, du_m[...])
            gm_ref[...] = dh_m[...] + dx
            dg_ref[...] = _rowsum8(dg)

        rows = lambda big, nxt: jnp.concatenate([big[N_META:, :], nxt[...]], axis=0)
        dx, dg = _rms_bwd(rows(h_b, h_n), g, rows(du_b, du_n))
        gx_ref[...] = rows(dh_b, dh_n) + dx
        dg_ref[...] += _rowsum8(dg)

    big = pl.BlockSpec((SHIFT_TILE, d), lambda i: (i, 0))
    nxt = pl.BlockSpec((N_META, d), lambda i: (per * (i + 1), 0))
    first = pl.BlockSpec((N_META, d), lambda i: (0, 0))
    return pl.pallas_call(
        body, name="pre_mix_bwd", grid=(seq // SHIFT_TILE,),
        in_specs=[big] * 3 + [nxt] * 3 + [first] * 3 + [pl.BlockSpec((1, d), lambda i: (0, 0))],
        out_specs=[big, first, pl.BlockSpec((SUBLANES, d), lambda i: (0, 0))],
        out_shape=[jax.ShapeDtypeStruct((seq, d), F32), jax.ShapeDtypeStruct((N_META, d), F32),
                   jax.ShapeDtypeStruct((SUBLANES, d), F32)],
        compiler_params=_cparams("arbitrary"),
    )(du1, h0, dh1, du1, h0, dh1, du1, h0, dh1, g_pre_mix)


def _ln_stats(c):
    mu = jnp.mean(c, axis=-1, keepdims=True)
    var = jnp.mean(jnp.square(c - mu), axis=-1, keepdims=True)
    return mu, lax.rsqrt(var + LN_EPS)


def _ln_silu(c, ln_g, ln_b, lp):
    w = c.shape[1]

    def body(row0, ins, outs, accs):
        c_ref, g_ref, b_ref = ins
        c_v = c_ref[...]
        mu, rstd = _ln_stats(c_v)
        ln = (c_v - mu) * rstd * g_ref[...] + b_ref[...]
        outs[0][...] = (ln * jax.nn.sigmoid(ln)).astype(BF16)

    return _rowwise(body, [c, ln_g, ln_b], [(w, BF16)], [], lp=lp, name="ln_silu")[0]


def _ln_silu_bwd(c, ds, ln_g, ln_b, lp):
    w = c.shape[1]

    def body(row0, ins, outs, accs):
        c_ref, ds_ref, g_ref, b_ref = ins
        c_v, g = c_ref[...], g_ref[...]
        mu, rstd = _ln_stats(c_v)
        nrm = (c_v - mu) * rstd
        ln = nrm * g + b_ref[...]
        sig = jax.nn.sigmoid(ln)
        dln = ds_ref[...] * (sig * (1.0 + ln * (1.0 - sig)))
        dn = dln * g
        dc = rstd * (dn - jnp.mean(dn, axis=-1, keepdims=True) - nrm * jnp.mean(dn * nrm, axis=-1, keepdims=True))
        outs[0][...] = dc
        accs[0][...] += _rowsum8(dln * nrm)
        accs[1][...] += _rowsum8(dln)
        accs[2][...] += _rowsum8(dc)

    return _rowwise(body, [c, ds, ln_g, ln_b], [(w, F32)], [w, w, w], lp=lp, name="ln_silu_bwd")


def _chunk_with_history(ref, i, cols=slice(None)):
    t0 = pl.multiple_of(i * ROW_CHUNK, ROW_CHUNK)
    lo0 = pl.multiple_of(jnp.maximum(t0 - HALO, 0), SUBLANES)
    lo = jnp.where(i > 0, ref[pl.ds(lo0, HALO), cols], 0.0)
    return jnp.concatenate([lo, ref[pl.ds(t0, ROW_CHUNK), cols]], axis=0)


def _chunk_with_future(ref, i, n_chunks, cols=slice(None)):
    t0 = pl.multiple_of(i * ROW_CHUNK, ROW_CHUNK)
    hi0 = pl.multiple_of(jnp.minimum(t0 + ROW_CHUNK, (n_chunks - 1) * ROW_CHUNK), SUBLANES)
    hi = jnp.where(i < n_chunks - 1, ref[pl.ds(hi0, HALO), cols], 0.0)
    return jnp.concatenate([ref[pl.ds(t0, ROW_CHUNK), cols], hi], axis=0)


def _inv_count(t0, n_rows, window):
    pos = t0 + lax.broadcasted_iota(jnp.int32, (n_rows, 1), 0)
    return 1.0 / jnp.minimum(pos + 1, window).astype(F32)


def _pool_delta(z_hist, t0, window):
    s = z_hist
    sh = 1
    while sh < window:
        s = s + pltpu.roll(s, sh, 0)
        sh *= 2
    cur = z_hist[HALO:, :]
    return s[HALO:, :] * _inv_count(t0, ROW_CHUNK, window) - cur


def _pool_fwd(proj, wpg, pool_scale, lp):
    n_grp, gdim, _ = wpg.shape
    width = n_grp * gdim
    n_chunks = lp // ROW_CHUNK

    def body(z_ref, w_ref, sc_ref, out_ref):
        for g, window in enumerate(POOL_WINDOWS):
            cols = slice(g * gdim, (g + 1) * gdim)

            def chunk(i, carry, cols=cols, g=g, window=window):
                t0 = pl.multiple_of(i * ROW_CHUNK, ROW_CHUNK)
                d = _pool_delta(_chunk_with_history(z_ref, i, cols), t0, window)
                q = jnp.dot(d.astype(BF16), w_ref[g], preferred_element_type=F32)
                out_ref[pl.ds(t0, ROW_CHUNK), cols] = (q * sc_ref[:, cols]).astype(BF16)
                return carry

            lax.fori_loop(0, n_chunks, chunk, 0)

    return pl.pallas_call(
        body, name="pool_fwd", grid=(1,),
        in_specs=[pl.BlockSpec((lp, width), lambda i: (0, 0)),
                  pl.BlockSpec(wpg.shape, lambda i: (0, 0, 0)),
                  pl.BlockSpec(pool_scale.shape, lambda i: (0, 0))],
        out_specs=pl.BlockSpec((lp, width), lambda i: (0, 0)),
        out_shape=jax.ShapeDtypeStruct((lp, width), BF16),
        compiler_params=_cparams("arbitrary"),
    )(proj, wpg, pool_scale)


def _pool_bwd(proj, d_ya, wpg, pool_scale, dproj, lp):
    n_grp, gdim, _ = wpg.shape
    width = n_grp * gdim
    n_chunks = lp // ROW_CHUNK
    ext = ROW_CHUNK + HALO

    def body(z_ref, dya_ref, w_ref, sc_ref, _, dz_ref, dw_ref, dsc_ref):
        dw_ref[...] = jnp.zeros_like(dw_ref)
        dsc_ref[...] = jnp.zeros_like(dsc_ref)
        for g, window in enumerate(POOL_WINDOWS):
            cols = slice(g * gdim, (g + 1) * gdim)

            def chunk(i, carry, cols=cols, g=g, window=window):
                t0 = pl.multiple_of(i * ROW_CHUNK, ROW_CHUNK)
                w_g = w_ref[g]
                scale = sc_ref[:, cols]
                d = _pool_delta(_chunk_with_history(z_ref, i, cols), t0, window).astype(BF16)
                dya_ext = _chunk_with_future(dya_ref, i, n_chunks, cols)
                dya = dya_ext[:ROW_CHUNK, :]
                q = jnp.dot(d, w_g, preferred_element_type=F32)
                dsc_ref[:, cols] += _rowsum8(dya * q)
                e_ext = (dya_ext * scale).astype(BF16)
                dw_ref[g] += lax.dot_general(d, e_ext[:ROW_CHUNK, :], (((0,), (0,)), ((), ())),
                                             preferred_element_type=F32)
                dd_ext = lax.dot_general(e_ext, w_g, (((1,), (1,)), ((), ())), preferred_element_type=F32)
                s = dd_ext * _inv_count(t0, ext, window)
                sh = 1
                while sh < window:
                    s = s + pltpu.roll(s, ext - sh, 0)
                    sh *= 2
                dz_ref[pl.ds(t0, ROW_CHUNK), cols] = (s[:ROW_CHUNK, :] - dd_ext[:ROW_CHUNK, :]).astype(BF16)
                return carry

            lax.fori_loop(0, n_chunks, chunk, 0)

    blk = pl.BlockSpec((lp, width), lambda i: (0, 0))
    return pl.pallas_call(
        body, name="pool_bwd", grid=(1,),
        in_specs=[blk, blk, pl.BlockSpec(wpg.shape, lambda i: (0, 0, 0)),
                  pl.BlockSpec(pool_scale.shape, lambda i: (0, 0)), ANY_SPEC],
        out_specs=[blk, pl.BlockSpec(wpg.shape, lambda i: (0, 0, 0)),
                   pl.BlockSpec((SUBLANES, width), lambda i: (0, 0))],
        out_shape=[jax.ShapeDtypeStruct(dproj.shape, BF16), jax.ShapeDtypeStruct(wpg.shape, F32),
                   jax.ShapeDtypeStruct((SUBLANES, width), F32)],
        input_output_aliases={4: 0},
        compiler_params=_cparams("arbitrary"),
    )(proj, d_ya, wpg, pool_scale, dproj)


def _conv_fwd(proj, w_dw, b_dw, lp, width, v_col0):
    n_chunks = lp // ROW_CHUNK
    v_blk0, g_blk0 = v_col0 // LANES, (v_col0 + width) // LANES

    def body(v_ref, gc_ref, w_ref, b_ref, c_ref, a_pad):
        a_pad[pl.ds(0, HALO), :] = jnp.zeros((HALO, LANES), F32)
        a_pad[pl.ds(HALO, lp), :] = v_ref[...] * jax.nn.sigmoid(gc_ref[...])

        def chunk(i, carry):
            t0 = pl.multiple_of(i * ROW_CHUNK, ROW_CHUNK)
            hist = a_pad[pl.ds(t0, ROW_CHUNK + HALO), :]
            acc = jnp.zeros((ROW_CHUNK, LANES), F32)
            for k in range(CONV_KERNEL):
                acc = acc + w_ref[k:k + 1, :] * pltpu.roll(hist, CONV_KERNEL - 1 - k, 0)[HALO:, :]
            c_ref[pl.ds(t0, ROW_CHUNK), :] = acc + b_ref[...]
            return carry

        lax.fori_loop(0, n_chunks, chunk, 0)

    return pl.pallas_call(
        body, name="conv_fwd", grid=(width // LANES,),
        in_specs=[pl.BlockSpec((lp, LANES), lambda j: (0, j + v_blk0)),
                  pl.BlockSpec((lp, LANES), lambda j: (0, j + g_blk0)),
                  pl.BlockSpec((CONV_TAPS_PADDED, LANES), lambda j: (0, j)),
                  pl.BlockSpec((1, LANES), lambda j: (0, j))],
        out_specs=pl.BlockSpec((lp, LANES), lambda j: (0, j)),
        out_shape=jax.ShapeDtypeStruct((lp, width), F32),
        scratch_shapes=[pltpu.VMEM((lp + HALO, LANES), F32)],
        compiler_params=_cparams("parallel"),
    )(proj, proj, w_dw, b_dw)


def _conv_bwd(proj, dc, w_dw, dproj, lp, width, v_col0):
    n_chunks = lp // ROW_CHUNK
    ext = ROW_CHUNK + HALO
    v_blk0, g_blk0 = v_col0 // LANES, (v_col0 + width) // LANES

    def body(v_ref, gc_ref, dc_ref, w_ref, _, dv_ref, dgc_ref, dw_ref, a_pad, dc_pad, dw_acc):
        sig = jax.nn.sigmoid(gc_ref[...])
        a_pad[pl.ds(0, HALO), :] = jnp.zeros((HALO, LANES), F32)
        a_pad[pl.ds(HALO, lp), :] = v_ref[...] * sig
        dc_pad[pl.ds(0, lp), :] = dc_ref[...]
        dc_pad[pl.ds(lp, HALO), :] = jnp.zeros((HALO, LANES), F32)
        dw_acc[...] = jnp.zeros_like(dw_acc)

        def chunk(i, carry):
            t0 = pl.multiple_of(i * ROW_CHUNK, ROW_CHUNK)
            hist = a_pad[pl.ds(t0, ext), :]
            fut = dc_pad[pl.ds(t0, ext), :]
            dc_cur = fut[:ROW_CHUNK, :]
            da = jnp.zeros((ROW_CHUNK, LANES), F32)
            for k in range(CONV_KERNEL):
                lag = CONV_KERNEL - 1 - k
                da = da + w_ref[k:k + 1, :] * pltpu.roll(fut, (ext - lag) % ext, 0)[:ROW_CHUNK, :]
                dw_acc[pl.ds(SUBLANES * k, SUBLANES), :] += _rowsum8(dc_cur * pltpu.roll(hist, lag, 0)[HALO:, :])
            rows = pl.ds(t0, ROW_CHUNK)
            sg = jax.nn.sigmoid(gc_ref[rows, :])
            dv_ref[rows, :] = (da * sg).astype(BF16)
            dgc_ref[rows, :] = (da * v_ref[rows, :] * sg * (1.0 - sg)).astype(BF16)
            return carry

        lax.fori_loop(0, n_chunks, chunk, 0)
        dw_ref[...] = dw_acc[...].reshape(CONV_TAPS_PADDED, SUBLANES, LANES).sum(axis=1)

    col = lambda j: (0, j)
    return pl.pallas_call(
        body, name="conv_bwd", grid=(width // LANES,),
        in_specs=[pl.BlockSpec((lp, LANES), lambda j: (0, j + v_blk0)),
                  pl.BlockSpec((lp, LANES), lambda j: (0, j + g_blk0)),
                  pl.BlockSpec((lp, LANES), col),
                  pl.BlockSpec((CONV_TAPS_PADDED, LANES), col), ANY_SPEC],
        out_specs=[pl.BlockSpec((lp, LANES), lambda j: (0, j + v_blk0)), pl.BlockSpec((lp, LANES), col),
                   pl.BlockSpec((CONV_TAPS_PADDED, LANES), col)],
        out_shape=[jax.ShapeDtypeStruct(dproj.shape, BF16), jax.ShapeDtypeStruct((lp, width), BF16),
                   jax.ShapeDtypeStruct((CONV_TAPS_PADDED, width), F32)],
        scratch_shapes=[pltpu.VMEM((lp + HALO, LANES), F32), pltpu.VMEM((lp + HALO, LANES), F32),
                        pltpu.VMEM((CONV_TAPS_PADDED * SUBLANES, LANES), F32)],
        input_output_aliases={4: 0},
        compiler_params=_cparams("parallel"),
    )(proj, proj, dc, w_dw, dproj)


def _place_columns(dst, pieces, name):
    m = dst.shape[0]
    tile = 512
    counts = [p.shape[1] // tile for p, _ in pieces]
    starts = [sum(counts[:i]) for i in range(len(pieces))]
    n_steps = sum(counts)

    def local(s, i):
        return jnp.clip(s - starts[i], 0, counts[i] - 1)

    def out_index(s):
        blk = pieces[0][1] // tile + local(s, 0)
        for i in range(1, len(pieces)):
            blk = jnp.where(s >= starts[i], pieces[i][1] // tile + local(s, i), blk)
        return 0, blk

    def body(*refs):
        out_ref = refs[-1]
        s = pl.program_id(0)
        for i in range(len(pieces)):
            @pl.when((s >= starts[i]) & (s < starts[i] + counts[i]))
            def _(i=i):
                out_ref[...] = refs[i][...]

    return pl.pallas_call(
        body, name=name, grid=(n_steps,),
        in_specs=[pl.BlockSpec((m, tile), functools.partial(lambda s, i: (0, local(s, i)), i=i))
                  for i in range(len(pieces))] + [ANY_SPEC],
        out_specs=pl.BlockSpec((m, tile), out_index),
        out_shape=jax.ShapeDtypeStruct(dst.shape, dst.dtype),
        input_output_aliases={len(pieces): 0},
        compiler_params=_cparams("arbitrary"),
    )(*[p for p, _ in pieces], dst)


def _adamw_math(w, g, m, v):
    m = ADAM_B1 * m + (1.0 - ADAM_B1) * g
    v = ADAM_B2 * v + (1.0 - ADAM_B2) * jnp.square(g)
    m_hat = m / (1.0 - ADAM_B1 ** ADAM_STEP)
    v_hat = v / (1.0 - ADAM_B2 ** ADAM_STEP)
    delta = -ADAM_LR * (m_hat / (jnp.sqrt(v_hat) + ADAM_EPS) + ADAM_WD * w)
    return delta, m, v


def _pair_sum(own, recv, where, name):
    _, _, rows, cols = own.shape
    tr = _row_tile(rows, cols, 1024 * 1024)

    def body(where_ref, own_ref, recv_ref, out_ref):
        out_ref[...] = (own_ref[...].astype(F32) + recv_ref[...].astype(F32)).astype(BF16)

    return pl.pallas_call(
        body, name=name,
        grid_spec=pltpu.PrefetchScalarGridSpec(
            num_scalar_prefetch=1, grid=(3, rows // tr),
            in_specs=[pl.BlockSpec((None, None, tr, cols), lambda r, i, wh: (wh[2 + r], wh[0], i, 0)),
                      pl.BlockSpec((None, tr, cols), lambda r, i, wh: (wh[2 + r], i, 0))],
            out_specs=pl.BlockSpec((None, tr, cols), lambda r, i, wh: (wh[2 + r], i, 0))),
        out_shape=jax.ShapeDtypeStruct((4, rows, cols), BF16),
        compiler_params=_cparams("parallel", "parallel"),
    )(where, own, recv)


def _adamw_big(w, m, v, own, from_sibling, recv3, where, name):
    rows, cols = w.shape
    tr = _row_tile(rows, cols, 256 * 1024)

    def body(where_ref, w_ref, m_ref, v_ref, own_ref, sib_ref, r_ref, g_out, d_out, m_out, v_out):
        g = own_ref[...].astype(F32) + sib_ref[...].astype(F32)
        for r in range(3):
            g = g + r_ref[r].astype(F32)
        delta, m_new, v_new = _adamw_math(w_ref[...], g, m_ref[...], v_ref[...])
        g_out[...] = g
        d_out[...] = delta
        m_out[...] = m_new
        v_out[...] = v_new

    blk = pl.BlockSpec((tr, cols), lambda i, q_ref: (i, 0))
    return pl.pallas_call(
        body, name=name,
        grid_spec=pltpu.PrefetchScalarGridSpec(
            num_scalar_prefetch=1, grid=(rows // tr,),
            in_specs=[blk, blk, blk,
                      pl.BlockSpec((None, None, tr, cols), lambda i, wh: (wh[1], wh[0], i, 0)),
                      pl.BlockSpec((None, tr, cols), lambda i, wh: (wh[1], i, 0)),
                      pl.BlockSpec((3, tr, cols), lambda i, wh: (0, i, 0))],
            out_specs=[blk] * 4),
        out_shape=[jax.ShapeDtypeStruct((rows, cols), F32)] * 4,
        compiler_params=_cparams("parallel"),
    )(where, w, m, v, own, from_sibling, recv3)


def _small_update(me_idx, packed, rep_params, rep_places, meta_wmv, meta_row0, wdw_wmv, wdw_row0, loss_row0):
    n_rep = len(rep_params)
    meta_rows, meta_cols = meta_wmv[0].shape
    wdw_rows, wdw_cols = wdw_wmv[0].shape

    def body(me_ref, *refs):
        pos = 0

        def take(k):
            nonlocal pos
            out = refs[pos:pos + k]
            pos += k
            return out

        rep_in = [take(3) for _ in range(n_rep)]
        rep_g = take(n_rep)
        meta_in, (meta_g,) = take(3), take(1)
        wdw_in, (wdw_g,) = take(3), take(1)
        (loss_ref,) = take(1)
        rep_out = [take(4) for _ in range(n_rep)]
        meta_out, wdw_out = take(4), take(4)
        (loss_out,) = take(1)

        def update(wmv, g, outs):
            delta, m_new, v_new = _adamw_math(wmv[0][...], g, wmv[1][...], wmv[2][...])
            for o_ref, val in zip(outs, (g, delta, m_new, v_new)):
                o_ref[...] = val

        for wmv, g_ref, outs in zip(rep_in, rep_g, rep_out):
            g = jnp.sum(g_ref[0], axis=0, keepdims=True)
            for j in range(1, N_DEV):
                g = g + jnp.sum(g_ref[j], axis=0, keepdims=True)
            update(wmv, g, outs)
        for wmv, g_ref, outs in ((meta_in, meta_g, meta_out), (wdw_in, wdw_g, wdw_out)):
            g = g_ref[0]
            for j in range(1, N_DEV):
                g = g + g_ref[j]
            update(wmv, g, outs)
        total = loss_ref[0]
        for j in range(1, N_DEV):
            total = total + loss_ref[j]
        loss_out[...] = total

    def whole(a):
        nd = a.ndim
        return pl.BlockSpec(a.shape, lambda i, me_ref, nd=nd: (0,) * nd)

    ins, in_specs = [], []
    for wmv in rep_params:
        ins += list(wmv)
        in_specs += [whole(a) for a in wmv]
    for wmv, (row0, col0) in zip(rep_params, rep_places):
        width = wmv[0].shape[1]
        ins.append(packed)
        in_specs.append(pl.BlockSpec((N_DEV, SUBLANES, width),
                                     lambda i, me_ref, rb=row0 // SUBLANES, cb=col0 // width: (0, rb, cb)))
    ins += list(meta_wmv) + [packed]
    in_specs += [whole(a) for a in meta_wmv]
    in_specs.append(pl.BlockSpec((N_DEV, meta_rows, meta_cols),
                                 lambda i, me_ref, rb=meta_row0 // meta_rows: (0, rb, me_ref[0])))
    ins += list(wdw_wmv) + [packed]
    in_specs += [whole(a) for a in wdw_wmv]
    in_specs.append(pl.BlockSpec((N_DEV, wdw_rows, wdw_cols),
                                 lambda i, me_ref, rb=wdw_row0 // wdw_rows: (0, rb, me_ref[0])))
    ins.append(packed)
    in_specs.append(pl.BlockSpec((N_DEV, SUBLANES, LANES), lambda i, me_ref, rb=loss_row0 // SUBLANES: (0, rb, 0)))

    out_shape, out_specs = [], []
    for wmv in list(rep_params) + [meta_wmv, wdw_wmv]:
        out_shape += [jax.ShapeDtypeStruct(wmv[0].shape, F32)] * 4
        out_specs += [whole(wmv[0])] * 4
    out_shape.append(jax.ShapeDtypeStruct((SUBLANES, LANES), F32))
    out_specs.append(pl.BlockSpec((SUBLANES, LANES), lambda i, me_ref: (0, 0)))

    outs = pl.pallas_call(
        body, name="small_update",
        grid_spec=pltpu.PrefetchScalarGridSpec(num_scalar_prefetch=1, grid=(1,), in_specs=in_specs,
                                               out_specs=out_specs),
        out_shape=out_shape, compiler_params=_cparams("arbitrary"),
    )(me_idx, *ins)
    groups = [outs[4 * i:4 * i + 4] for i in range(n_rep + 2)]
    return groups[:n_rep], groups[n_rep], groups[n_rep + 1], outs[-1]


def kernel(x, meta, g_pre_mix, w_in, w_pool_grp, pool_scale, w_pool_out, w_dw, b_dw, conv_ln_g, conv_ln_b, w_conv_out, w_o, g_post_mix, g_pre_mlp, w_up, w_down, g_post_mlp, loss_target, m_meta, m_g_pre_mix, m_w_in, m_w_pool_grp, m_pool_scale, m_w_pool_out, m_w_dw, m_b_dw, m_conv_ln_g, m_conv_ln_b, m_w_conv_out, m_w_o, m_g_post_mix, m_g_pre_mlp, m_w_up, m_w_down, m_g_post_mlp, v_meta, v_g_pre_mix, v_w_in, v_w_pool_grp, v_pool_scale, v_w_pool_out, v_w_dw, v_b_dw, v_conv_ln_g, v_conv_ln_b, v_w_conv_out, v_w_o, v_g_post_mix, v_g_pre_mlp, v_w_up, v_w_down, v_g_post_mlp):
    seq, d = x.shape[1], x.shape[2]
    pool_w = pool_scale.shape[1]
    conv_w = b_dw.shape[1]
    n_grp, grp_rows, gdim = w_pool_grp.shape[1:]
    lp = _round_up(N_META + seq, ROW_CHUNK)
    tm_half = lp // 2 if (lp // 2) % 16 == 0 else lp
    c_idx = lax.axis_index("c").astype(jnp.int32)
    chip_idx = (2 * lax.axis_index("x") + lax.axis_index("y")).astype(jnp.int32)
    me_idx = 2 * chip_idx + c_idx

    pad_taps = ((0, CONV_TAPS_PADDED - CONV_KERNEL), (0, 0))
    big = dict(w_in=w_in[0], w_pool_grp=w_pool_grp[0].reshape(n_grp * grp_rows, gdim), w_pool_out=w_pool_out[0],
               w_conv_out=w_conv_out[0], w_o=w_o[0], w_up=w_up[0], w_down=w_down[0])
    big_names = list(big)
    moments = dict(w_in=(m_w_in, v_w_in), w_pool_grp=(m_w_pool_grp, v_w_pool_grp), w_pool_out=(m_w_pool_out, v_w_pool_out),
                   w_conv_out=(m_w_conv_out, v_w_conv_out), w_o=(m_w_o, v_w_o), w_up=(m_w_up, v_w_up),
                   w_down=(m_w_down, v_w_down))
    slot_idx = me_idx.reshape(1)
    sources = dict(big, meta=meta, w_dw=jnp.pad(w_dw[0], pad_taps))

    def fill(k, after):
        return _fill_slot(sources[k], slot_idx, BF16 if k in big else F32, after, "fill_" + k)

    gather_groups = [["meta", "w_dw"], ["w_in"], ["w_pool_grp", "w_pool_out", "w_conv_out", "w_o"], ["w_up"], ["w_down"]]
    started, token = _gather_start([[fill(k, slot_idx) for k in names] for names in gather_groups[:2]], slot_idx,
                                   "gather_start_first", BARRIER_IDS["gather_first"], issue_order=(0, 3, 1, 2))
    started_rest, _ = _gather_start([[fill(k, token) for k in names] for names in gather_groups[2:]], token,
                                    "gather_start_rest", BARRIER_IDS["gather_rest"])
    started += started_rest
    wg = {}
    x_idx, y_idx = lax.axis_index("x"), lax.axis_index("y")
    at = lambda px, py, pc: 4 * px + 2 * py + pc
    near_slots = jnp.stack([at(x_idx, y_idx, c_idx), at(x_idx, y_idx, 1 - c_idx), at(1 - x_idx, y_idx, c_idx),
                            at(x_idx, 1 - y_idx, c_idx), at(1 - x_idx, y_idx, 1 - c_idx),
                            at(x_idx, 1 - y_idx, 1 - c_idx)]).astype(jnp.int32)
    far_slots = jnp.stack([at(1 - x_idx, 1 - y_idx, c_idx), at(1 - x_idx, 1 - y_idx, 1 - c_idx)]).astype(jnp.int32)

    def gather_whole(gi, after_forward, after_finish):
        send, recv, lands = started[gi]
        fs, fr, lands = _gather_forward(lands, recv, after_forward(), f"gather_forward_{gi}")
        lands = _gather_finish(lands, send, recv, [(ALL_CHIPS, fs, fr)], [(ALL_CHIPS, fs, fr)], after_finish(),
                               f"gather_finish_{gi}")
        wg.update(zip(gather_groups[gi], lands))

    near_state = {}

    def gather_near(gi, after):
        send, recv, lands = started[gi]
        fs, fr, lands = _gather_forward(lands, recv, after, f"gather_forward_near_{gi}", which=NEAR)
        lands = _gather_finish(lands, send, recv, [(NEAR, fs, fr)], [], after, f"gather_finish_near_{gi}",
                               direct_sends=False)
        near_state[gi] = (fs, fr)
        return lands

    def gather_far(gi, lands, after):
        send, recv, _ = started[gi]
        fs, fr, lands = _gather_forward(lands, recv, after, f"gather_forward_far_{gi}", which=FAR)
        lands = _gather_finish(lands, send, recv, [(FAR, fs, fr)], [(NEAR,) + near_state[gi], (FAR, fs, fr)], after,
                               f"gather_finish_far_{gi}", own=False)
        wg.update(zip(gather_groups[gi], lands))
        return lands

    gather_whole(0, lambda: token, lambda: token)
    meta_full = wg["meta"].transpose(1, 0, 2).reshape(N_META, d)
    wdw_full = wg["w_dw"].transpose(1, 0, 2).reshape(CONV_TAPS_PADDED, conv_w)
    target = loss_target[0]
    h0, u1 = _rms_pre(meta_full, x[0], g_pre_mix, lp, seq)
    send, recv, w_in_lands = started[1]
    w_in_lands = _gather_finish(w_in_lands, send, recv, [], [], u1, "gather_finish_home_1", direct_sends=False)
    proj = _mm_slots(u1, w_in_lands[0], near_slots[:2], over="n", tm=tm_half, name="mm_proj_home")
    fs_far, fr_far, w_in_lands = _gather_forward(w_in_lands, recv, proj, "gather_forward_far_1", which=FAR)
    w_in_lands = _gather_finish(w_in_lands, send, recv, [(FAR, fs_far, fr_far)], [], proj, "gather_finish_far_1",
                                own=False, direct_sends=False)
    proj = _mm_slots(u1, w_in_lands[0], far_slots, over="n", tm=tm_half, base=proj, name="mm_proj_far")
    fs_near, fr_near, w_in_lands = _gather_forward(w_in_lands, recv, proj, "gather_forward_near_1", which=NEAR)
    w_in_lands = _gather_finish(w_in_lands, send, recv, [(NEAR, fs_near, fr_near)],
                                [(FAR, fs_far, fr_far), (NEAR, fs_near, fr_near)], proj, "gather_finish_near_1",
                                own=False)
    proj = _mm_slots(u1, w_in_lands[0], near_slots[2:], over="n", tm=tm_half, base=proj, name="mm_proj_near")
    wg["w_in"] = w_in_lands[0]
    conv_c = _conv_fwd(proj, wdw_full, b_dw, lp, conv_w, pool_w)
    s_act = _ln_silu(conv_c, conv_ln_g, conv_ln_b, lp)
    gather_whole(2, lambda: proj, lambda: s_act)
    wpg_full = wg["w_pool_grp"].reshape(N_DEV, n_grp, grp_rows, gdim).transpose(1, 0, 2, 3).reshape(n_grp, gdim, gdim)
    w_o_full = wg["w_o"].reshape(d, d)
    ya_pre = _pool_fwd(proj, wpg_full, pool_scale, lp)
    m_mix, y_a, y_b = _gate_mix(ya_pre, s_act, wg["w_pool_out"], wg["w_conv_out"], proj, d, "gate_mix")
    o = _mm(m_mix, w_o_full, mode="nn", tm=tm_half, tn=512, name="mm_o")
    h1, u2 = _post_mix(o, h0, g_post_mix, g_pre_mlp, lp)
    relu2 = lambda acc: jnp.square(jnp.maximum(acc, 0.0))
    (w_up_near,) = gather_near(3, u2)
    act = _mm_slots(u2, w_up_near, near_slots, over="n", tm=tm_half, out_dtype=BF16, epilogue=relu2, name="mm_up_near")
    (w_up_all,) = gather_far(3, [w_up_near], act)
    act = _mm_slots(u2, w_up_all, far_slots, over="n", tm=tm_half, out_dtype=BF16, epilogue=relu2, base=act,
                    name="mm_up_far")
    (w_down_near,) = gather_near(4, act)
    f = _mm_slots(act, w_down_near, near_slots, over="k", tm=tm_half, tn=d, name="mm_down_near")
    (w_down_all,) = gather_far(4, [w_down_near], f)
    f = _mm_slots(act, w_down_all, far_slots, over="k", tm=tm_half // 2, tn=d, base=f, name="mm_down_far")
    w_down_full = w_down_all.reshape(-1, d)

    big_out = {}

    def to_sibling(names, grads, after, tag):
        send, recv, grads, lands, token = _pair_exchange_start(grads, after, "grads_to_sibling_start_" + tag)
        return (names, send, recv, grads, lands, tag), token

    where = jnp.stack([c_idx, chip_idx, 2 * (1 - x_idx) + y_idx, 2 * x_idx + (1 - y_idx),
                       2 * (1 - x_idx) + (1 - y_idx)]).astype(jnp.int32)

    def to_owner(handle, after):
        names, send, recv, grads, from_sib, tag = handle
        grads, from_sib = _pair_exchange_finish(grads, from_sib, send, recv, after, "grads_to_sibling_finish_" + tag)
        own = [g.reshape((4, 2) + g.shape[1:]) for g in grads]
        sums = [_pair_sum(o, r, where, "pair_sum_" + k) for k, o, r in zip(names, own, from_sib)]
        send, recv, sums, lands, token = _chip_exchange_start(sums, after, "grads_to_owner_start_" + tag,
                                                              BARRIER_IDS["owner_" + tag])
        return (names, send, recv, sums, lands, own, from_sib, tag), token

    def update(handle, after):
        names, send, recv, sums, lands, own, from_sib, tag = handle
        got = _chip_exchange_finish(sums, lands, send, recv, after, "grads_to_owner_finish_" + tag)
        for k, o, s, r3 in zip(names, own, from_sib, got):
            w2 = big[k]
            shape = moments[k][0].shape
            outs = _adamw_big(w2, moments[k][0].reshape(w2.shape), moments[k][1].reshape(w2.shape), o, s, r3,
                              where, "adamw_" + k)
            big_out[k] = [a.reshape(shape) for a in outs]
        return big_out[names[-1]][0]

    df, dh, dg_post_mlp, loss_part = _loss_head(f, h1, target, g_post_mlp, lp, seq)
    d_up = _mm(df, w_down_full, mode="nt", tm=tm_half, tn=1024, out_dtypes=(BF16,), extras=[(act, 0)],
               epilogue=lambda acc, a: (acc * (2.0 * jnp.sqrt(a.astype(F32))),), row_splits=2, name="mm_d_up")
    g_w_down = _mm(act, df, mode="tn", tm=1024, tn=1024, out_dtypes=(BF16,), name="mm_g_down")
    sib_down, token = to_sibling(["w_down"], [g_w_down.reshape(N_DEV, -1, d)], slot_idx, "down")
    g_w_up = _mm(u2, d_up, mode="tn", tm=1024, tn=wg["w_up"].shape[2], out_blocked=True, out_dtypes=(BF16,),
                 after=token, name="mm_g_up")
    sib_up, token = to_sibling(["w_up"], [g_w_up], slot_idx, "up")
    pending_down, token = to_owner(sib_down, token)
    du2 = _mm(d_up, wg["w_up"], mode="nt", tm=tm_half, tn=1024, b_blocked=True, kb=2, after=token, name="mm_du2")
    pending_up, token = to_owner(sib_up, du2)
    dh1, do, dg_pre_mlp, dg_post_mix = _mid_bwd(du2, h1, dh, o, g_pre_mlp, g_post_mix, lp)

    def gate_bwd(dm, ga, gb, ya, yb):
        sa, sb = jax.nn.sigmoid(ga), jax.nn.sigmoid(gb)
        return (dm * ya.astype(F32) * sa * (1.0 - sa), dm * yb.astype(F32) * sb * (1.0 - sb), dm * sa, dm * sb)

    gate_tn = 512
    ga_col0, gb_col0 = proj.shape[1] - 2 * d, proj.shape[1] - d
    dproj, d_gb, d_ya, d_yb = _mm(
        do, w_o_full, mode="nt", tm=tm_half, tn=gate_tn, out_dtypes=(BF16,) * 4,
        extras=[(proj, ga_col0 // gate_tn), (proj, gb_col0 // gate_tn), (y_a, 0), (y_b, 0)], epilogue=gate_bwd,
        out_places=((proj.shape[1], ga_col0), None, None, None), after=token, row_splits=2, name="mm_dm")
    g_w_o = _mm(m_mix, do, mode="tn", tm=1024, tn=1024, out_dtypes=(BF16,), name="mm_g_o")
    bn_out = wg["w_pool_out"].shape[2]
    g_w_pool_out = _mm(ya_pre, d_ya, mode="tn", tm=pool_w, tn=4 * bn_out, out_blocked=True, out_block=bn_out,
                       out_dtypes=(BF16,), name="mm_g_pool_out")
    g_w_conv_out = _mm(s_act, d_yb, mode="tn", tm=conv_w, tn=4 * bn_out, out_blocked=True, out_block=bn_out,
                       out_dtypes=(BF16,), name="mm_g_conv_out")
    sib_mix, token = to_sibling(["w_o", "w_pool_out", "w_conv_out"],
                                [g_w_o.reshape(N_DEV, -1, d), g_w_pool_out, g_w_conv_out], slot_idx, "mix")
    d_ya_pre = _mm(d_ya, wg["w_pool_out"], mode="nt", tm=tm_half, tn=pool_w, b_blocked=True, kb=4, after=token,
                   name="mm_d_ya_pre")
    d_s = _mm(d_yb, wg["w_conv_out"], mode="nt", tm=tm_half, tn=conv_w, b_blocked=True, kb=4, name="mm_d_s")
    pending_mix, token = to_owner(sib_mix, d_s)
    dproj, g_wpg, d_scale = _pool_bwd(proj, d_ya_pre, wpg_full, pool_scale, dproj, lp)
    dc, d_ln_g, d_ln_b, d_b_dw = _ln_silu_bwd(conv_c, d_s, conv_ln_g, conv_ln_b, lp)
    dproj, dgc, g_wdw = _conv_bwd(proj, dc, wdw_full, dproj, lp, conv_w, pool_w)
    dproj = _place_columns(dproj, [(dgc, pool_w + conv_w), (d_gb, gb_col0)], "place_dproj")
    g_w_in = _mm(u1, dproj, mode="tn", tm=1024, tn=wg["w_in"].shape[2], out_blocked=True, out_dtypes=(BF16,),
                 after=token, name="mm_g_in")
    g_wpg_slots = g_wpg.astype(BF16).reshape(n_grp, N_DEV, grp_rows, gdim).transpose(1, 0, 2, 3)
    sib_in, token = to_sibling(["w_pool_grp", "w_in"],
                               [g_wpg_slots.reshape(N_DEV, n_grp * grp_rows, gdim), g_w_in], slot_idx, "in")
    done = update(pending_down, token)
    pending_in, token = to_owner(sib_in, done)
    done = update(pending_up, token)
    du1 = _mm(dproj, wg["w_in"], mode="nt", tm=tm_half // 2, tn=1024, b_blocked=True, kb=4, after=done, name="mm_du1")
    grad_x2, grad_meta_part, dg_pre_mix = _pre_mix_bwd(du1, h0, dh1, g_pre_mix, seq)
    grad_x = grad_x2[None]

    assert pool_w + conv_w == d and conv_w <= d and LANES <= d
    widen = lambda a: jnp.pad(a, ((0, 0), (0, d - a.shape[1])))
    packed = jnp.concatenate([
        dg_pre_mix, dg_post_mix, dg_pre_mlp, dg_post_mlp,
        jnp.concatenate([d_scale, d_ln_g], axis=1), jnp.concatenate([d_ln_b, d_b_dw], axis=1),
        grad_meta_part, widen(g_wdw), widen(loss_part)], axis=0)
    rep = dict(g_pre_mix=((g_pre_mix, m_g_pre_mix, v_g_pre_mix), (0, 0)),
               g_post_mix=((g_post_mix, m_g_post_mix, v_g_post_mix), (SUBLANES, 0)),
               g_pre_mlp=((g_pre_mlp, m_g_pre_mlp, v_g_pre_mlp), (2 * SUBLANES, 0)),
               g_post_mlp=((g_post_mlp, m_g_post_mlp, v_g_post_mlp), (3 * SUBLANES, 0)),
               pool_scale=((pool_scale, m_pool_scale, v_pool_scale), (4 * SUBLANES, 0)),
               conv_ln_g=((conv_ln_g, m_conv_ln_g, v_conv_ln_g), (4 * SUBLANES, pool_w)),
               conv_ln_b=((conv_ln_b, m_conv_ln_b, v_conv_ln_b), (5 * SUBLANES, 0)),
               b_dw=((b_dw, m_b_dw, v_b_dw), (5 * SUBLANES, conv_w)))
    meta_row0 = 6 * SUBLANES
    wdw_row0 = meta_row0 + N_META
    loss_row0 = wdw_row0 + CONV_TAPS_PADDED
    (small_started,), token = _gather_start([[_fill_slot(packed, slot_idx, F32, slot_idx, "fill_small")]], grad_x2,
                                            "gather_small_start", BARRIER_IDS["gather_small"])
    done = update(pending_mix, token)
    done = update(pending_in, done)
    send, recv, lands = small_started
    fsend, frecv, lands = _gather_forward(lands, recv, done, "gather_small_forward")
    (packed_all,) = _gather_finish(lands, send, recv, [(ALL_CHIPS, fsend, frecv)], [(ALL_CHIPS, fsend, frecv)], done,
                                   "gather_small_finish")
    rep_names = list(rep)
    wdw_wmv = [jnp.pad(a[0], pad_taps) for a in (w_dw, m_w_dw, v_w_dw)]
    rep_out, meta_out, wdw_out, loss_blk = _small_update(
        slot_idx, packed_all, [rep[k][0] for k in rep_names], [rep[k][1] for k in rep_names],
        (meta, m_meta, v_meta), meta_row0, wdw_wmv, wdw_row0, loss_row0)
    small_out = dict(zip(rep_names, rep_out))
    small_out["meta"] = meta_out
    small_out["w_dw"] = [a[:CONV_KERNEL][None] for a in wdw_out]

    order = ["meta", "g_pre_mix", "w_in", "w_pool_grp", "pool_scale", "w_pool_out", "w_dw", "b_dw", "conv_ln_g",
             "conv_ln_b", "w_conv_out", "w_o", "g_post_mix", "g_pre_mlp", "w_up", "w_down", "g_post_mlp"]
    by_name = {**big_out, **small_out}
    result = [loss_blk[0, 0], grad_x]
    for kind in range(4):
        result += [by_name[k][kind] for k in order]
    return tuple(result)
```

```python
import functools

import jax
import jax.numpy as jnp
from jax import lax
from jax.experimental import pallas as pl
from jax.experimental.pallas import tpu as pltpu

F32 = jnp.float32
BF16 = jnp.bfloat16
MESH = pl.DeviceIdType.MESH

N_DEV = 8
N_META = 16
POOL_WINDOWS = (2, 4, 8, 16)
CONV_KERNEL = 31
CONV_TAPS_PADDED = 32
RMS_EPS = 1e-6
LN_EPS = 1e-5
ADAM_LR = 0.001
ADAM_B1 = 0.9
ADAM_B2 = 0.999
ADAM_EPS = 1e-08
ADAM_WD = 0.01
ADAM_STEP = 10

LANES = 128
SUBLANES = 8
ROW_CHUNK = 128
HALO = 32
VMEM_LIMIT_BYTES = 56 * 1024 * 1024


def _cparams(*sem):
    return pltpu.CompilerParams(dimension_semantics=sem if sem else None, vmem_limit_bytes=VMEM_LIMIT_BYTES)


def _round_up(n, m):
    return (n + m - 1) // m * m


def _row_tile(rows, cols, max_elems=640 * 1024):
    best = None
    for t in range(16, rows + 1, 16):
        if rows % t == 0 and (best is None or t * cols <= max_elems):
            best = t
    assert best is not None, (rows, cols)
    return best


def _rowsum8(a):
    t, w = a.shape
    return a.reshape(t // SUBLANES, SUBLANES, w).sum(axis=0)


def _mesh_pos():
    return lax.axis_index("x"), lax.axis_index("y"), lax.axis_index("c")


HBM_SPEC = pl.BlockSpec(memory_space=pltpu.HBM)
SEM_SPEC = pl.BlockSpec(memory_space=pltpu.SEMAPHORE)
ANY_SPEC = pl.BlockSpec(memory_space=pl.ANY)
_DATAFLOW = pltpu.SideEffectType.DATAFLOW_SIDE_EFFECTING


SIBLING_BARRIER_ID = 1
BARRIER_IDS = {name: 2 + i for i, name in enumerate(
    ["gather_first", "gather_rest", "gather_small", "owner_down", "owner_up", "owner_mix", "owner_in"])}


def _sibling_handshake():
    x, y, c = _mesh_pos()
    barrier = pltpu.get_barrier_semaphore()
    pl.semaphore_signal(barrier, inc=1, device_id=(x, y, 1 - c), device_id_type=MESH)
    pl.semaphore_wait(barrier, 1)


def _peer_handshake(peers):
    barrier = pltpu.get_barrier_semaphore()
    for peer in peers:
        pl.semaphore_signal(barrier, inc=1, device_id=peer, device_id_type=MESH)
    pl.semaphore_wait(barrier, len(peers))


def _hbm(a):
    return pltpu.with_memory_space_constraint(a, pltpu.HBM)


def _slot(p):
    return 4 * p[0] + 2 * p[1] + p[2]


def _fill_slot(w, slot_idx, dtype, after, name):
    rows, cols = w.shape
    tr = _row_tile(rows, cols) if rows % 16 == 0 else rows

    def body(idx_ref, w_ref, _, out_ref):
        out_ref[...] = w_ref[...].astype(dtype)

    return pl.pallas_call(
        body, name=name,
        grid_spec=pltpu.PrefetchScalarGridSpec(
            num_scalar_prefetch=1, grid=(rows // tr,),
            in_specs=[pl.BlockSpec((tr, cols), lambda i, idx_ref: (i, 0)), ANY_SPEC],
            out_specs=pl.BlockSpec((None, tr, cols), lambda i, idx_ref: (idx_ref[0], i, 0))),
        out_shape=jax.ShapeDtypeStruct((N_DEV, rows, cols), dtype),
        compiler_params=_cparams("parallel"),
    )(slot_idx, w, after)


def _gather_start(groups, after, name, barrier_id, issue_order=(0, 1, 2, 3)):
    flat = [g for grp in groups for g in grp]
    n, n_grp = len(flat), len(groups)

    def body(*refs):
        lands = refs[:n]
        sems = refs[n + 1:n + 1 + 2 * n_grp]
        token = refs[-1]
        x, y, c = _mesh_pos()
        targets = [(x, y, 1 - c), (1 - x, y, c), (x, 1 - y, c), (1 - x, 1 - y, c)]
        _peer_handshake(targets)
        t = 0
        for gi, grp in enumerate(groups):
            for ti in range(len(grp)):
                mine = lands[t].at[_slot((x, y, c))]
                for k in issue_order:
                    pltpu.make_async_remote_copy(
                        src_ref=mine, dst_ref=mine,
                        send_sem=sems[2 * gi].at[4 * ti + k], recv_sem=sems[2 * gi + 1].at[4 * ti + k],
                        device_id=targets[k], device_id_type=MESH).start()
                t += 1
        token[...] = jnp.zeros_like(token)

    sem_shapes = []
    for grp in groups:
        sem_shapes += [pltpu.SemaphoreType.DMA((4 * len(grp),))] * 2
    outs = pl.pallas_call(
        body, name=name,
        out_shape=tuple(sem_shapes + [pltpu.HBM(g.shape, g.dtype) for g in flat]
                        + [jax.ShapeDtypeStruct((SUBLANES, LANES), F32)]),
        in_specs=tuple([HBM_SPEC] * n + [ANY_SPEC]),
        out_specs=tuple([SEM_SPEC] * (2 * n_grp) + [HBM_SPEC] * n + [pl.BlockSpec(memory_space=pltpu.VMEM)]),
        input_output_aliases={i: 2 * n_grp + i for i in range(n)},
        compiler_params=pltpu.CompilerParams(has_side_effects=_DATAFLOW, collective_id=barrier_id),
    )(*[_hbm(g) for g in flat], after)
    sems, lands, token = outs[:2 * n_grp], outs[2 * n_grp:-1], outs[-1]
    res, t = [], 0
    for gi, grp in enumerate(groups):
        res.append((sems[2 * gi], sems[2 * gi + 1], list(lands[t:t + len(grp)])))
        t += len(grp)
    return res, token


NEAR, FAR, ALL_CHIPS = (0, 1), (2,), (0, 1, 2)


def _gather_forward(lands, recv_sems, after, name, which=ALL_CHIPS):
    n, nw = len(lands), len(which)

    def body(*refs):
        land_refs, recv, _ = refs[:n], refs[n], refs[n + 1]
        fsend, frecv = refs[n + 2], refs[n + 3]
        x, y, c = _mesh_pos()
        chips = [(1 - x, y), (x, 1 - y), (1 - x, 1 - y)]
        _sibling_handshake()
        for t in range(n):
            for i, j in enumerate(which):
                blk = land_refs[t].at[_slot((*chips[j], c))]
                pltpu.make_async_remote_copy(src_ref=blk, dst_ref=blk, send_sem=fsend.at[nw * t + i],
                                             recv_sem=recv.at[4 * t + 1 + j],
                                             device_id=(x, y, 1 - c), device_id_type=MESH).wait_recv()
                pltpu.make_async_remote_copy(src_ref=blk, dst_ref=blk, send_sem=fsend.at[nw * t + i],
                                             recv_sem=frecv.at[nw * t + i],
                                             device_id=(x, y, 1 - c), device_id_type=MESH).start()

    outs = pl.pallas_call(
        body, name=name,
        out_shape=tuple([pltpu.SemaphoreType.DMA((nw * n,))] * 2 + [pltpu.HBM(g.shape, g.dtype) for g in lands]),
        in_specs=tuple([HBM_SPEC] * n + [SEM_SPEC, ANY_SPEC]),
        out_specs=tuple([SEM_SPEC] * 2 + [HBM_SPEC] * n),
        input_output_aliases={i: 2 + i for i in range(n)},
        compiler_params=pltpu.CompilerParams(has_side_effects=_DATAFLOW, collective_id=SIBLING_BARRIER_ID),
    )(*lands, recv_sems, after)
    return outs[0], outs[1], list(outs[2:])


def _gather_finish(lands, send_sems, recv_sems, arrivals, sends, after, name, own=True, direct_sends=True):
    n = len(lands)
    fwd = list(arrivals) + list(sends)
    sem_args = [send_sems, recv_sems]
    where = []
    for _, fs, fr in fwd:
        pos = []
        for arr in (fs, fr):
            hit = [i for i, have in enumerate(sem_args) if have is arr]
            if not hit:
                sem_args.append(arr)
                hit = [len(sem_args) - 1]
            pos.append(hit[0])
        where.append(pos)

    def body(*refs):
        land_refs = refs[:n]
        send, recv = refs[n], refs[n + 1]
        fwd_refs = [refs[n + p] for pos in where for p in pos]
        x, y, c = _mesh_pos()
        sibling = (x, y, 1 - c)
        chips = [(1 - x, y), (x, 1 - y), (1 - x, 1 - y)]

        def desc(ref, s_sem, r_sem):
            return pltpu.make_async_remote_copy(src_ref=ref, dst_ref=ref, send_sem=s_sem, recv_sem=r_sem,
                                                device_id=sibling, device_id_type=MESH)

        for t in range(n):
            mine = land_refs[t].at[_slot((x, y, c))]
            if own:
                desc(land_refs[t].at[_slot(sibling)], send.at[4 * t], recv.at[4 * t]).wait_recv()
            for a, (which, _, _) in enumerate(fwd):
                fs, fr = fwd_refs[2 * a], fwd_refs[2 * a + 1]
                for i, j in enumerate(which):
                    if a < len(arrivals):
                        desc(land_refs[t].at[_slot((*chips[j], 1 - c))], fs.at[len(which) * t + i],
                             fr.at[len(which) * t + i]).wait_recv()
                    else:
                        desc(land_refs[t].at[_slot((*chips[j], c))], fs.at[len(which) * t + i],
                             fr.at[len(which) * t + i]).wait_send()
            if direct_sends:
                for k in range(4):
                    desc(mine, send.at[4 * t + k], recv.at[4 * t + k]).wait_send()

    outs = pl.pallas_call(
        body, name=name,
        out_shape=tuple(pltpu.HBM(g.shape, g.dtype) for g in lands),
        in_specs=tuple([HBM_SPEC] * n + [SEM_SPEC] * len(sem_args) + [ANY_SPEC]),
        out_specs=tuple([HBM_SPEC] * n),
        input_output_aliases={i: i for i in range(n)},
        compiler_params=pltpu.CompilerParams(has_side_effects=_DATAFLOW),
    )(*lands, *sem_args, after)
    return list(outs)


def _pair_exchange_start(grads, after, name):
    n = len(grads)

    def body(*refs):
        ins, lands = refs[:n], refs[n:2 * n]
        send, recv = refs[2 * n + 1], refs[2 * n + 2]
        token = refs[-1]
        x, y, c = _mesh_pos()
        _sibling_handshake()
        for t in range(n):
            for q in range(4):
                pltpu.make_async_remote_copy(
                    src_ref=ins[t].at[2 * q + 1 - c], dst_ref=lands[t].at[q],
                    send_sem=send.at[4 * t + q], recv_sem=recv.at[4 * t + q],
                    device_id=(x, y, 1 - c), device_id_type=MESH).start()
        token[...] = jnp.zeros_like(token)

    land_shapes = [(4,) + g.shape[1:] for g in grads]
    outs = pl.pallas_call(
        body, name=name,
        out_shape=tuple([pltpu.SemaphoreType.DMA((4 * n,))] * 2 + [pltpu.HBM(g.shape, g.dtype) for g in grads]
                        + [pltpu.HBM(ls, g.dtype) for ls, g in zip(land_shapes, grads)]
                        + [jax.ShapeDtypeStruct((SUBLANES, LANES), F32)]),
        in_specs=tuple([HBM_SPEC] * (2 * n) + [ANY_SPEC]),
        out_specs=tuple([SEM_SPEC] * 2 + [HBM_SPEC] * (2 * n) + [pl.BlockSpec(memory_space=pltpu.VMEM)]),
        input_output_aliases={i: 2 + i for i in range(2 * n)},
        compiler_params=pltpu.CompilerParams(has_side_effects=_DATAFLOW, collective_id=SIBLING_BARRIER_ID),
    )(*[_hbm(g) for g in grads], *[_hbm(lax.empty(ls, g.dtype)) for ls, g in zip(land_shapes, grads)], after)
    return outs[0], outs[1], list(outs[2:2 + n]), list(outs[2 + n:2 + 2 * n]), outs[-1]


def _pair_exchange_finish(grads, lands, send_sems, recv_sems, after, name):
    n = len(grads)

    def body(*refs):
        ins, land_refs = refs[:n], refs[n:2 * n]
        send, recv = refs[2 * n], refs[2 * n + 1]
        x, y, c = _mesh_pos()
        for t in range(n):
            for q in range(4):
                cp = pltpu.make_async_remote_copy(
                    src_ref=ins[t].at[q], dst_ref=land_refs[t].at[q], send_sem=send.at[4 * t + q],
                    recv_sem=recv.at[4 * t + q], device_id=(x, y, 1 - c), device_id_type=MESH)
                cp.wait_send()
                cp.wait_recv()

    outs = pl.pallas_call(
        body, name=name,
        out_shape=tuple([pltpu.HBM(g.shape, g.dtype) for g in grads] + [pltpu.HBM(g.shape, g.dtype) for g in lands]),
        in_specs=tuple([HBM_SPEC] * (2 * n) + [SEM_SPEC] * 2 + [ANY_SPEC]),
        out_specs=tuple([HBM_SPEC] * (2 * n)),
        input_output_aliases={i: i for i in range(2 * n)},
        compiler_params=pltpu.CompilerParams(has_side_effects=_DATAFLOW),
    )(*grads, *lands, send_sems, recv_sems, after)
    return list(outs[:n]), list(outs[n:])


def _chip_exchange_start(sums, after, name, barrier_id):
    n = len(sums)

    def body(*refs):
        ins, lands = refs[:n], refs[n:2 * n]
        send, recv = refs[2 * n + 1], refs[2 * n + 2]
        token = refs[-1]
        x, y, c = _mesh_pos()
        chips = [(1 - x, y), (x, 1 - y), (1 - x, 1 - y)]
        _peer_handshake([(*chip, c) for chip in chips])
        for t in range(n):
            for r, chip in enumerate(chips):
                pltpu.make_async_remote_copy(
                    src_ref=ins[t].at[2 * chip[0] + chip[1]], dst_ref=lands[t].at[r],
                    send_sem=send.at[3 * t + r], recv_sem=recv.at[3 * t + r],
                    device_id=(*chip, c), device_id_type=MESH).start()
        token[...] = jnp.zeros_like(token)

    land_shapes = [(3,) + s.shape[1:] for s in sums]
    outs = pl.pallas_call(
        body, name=name,
        out_shape=tuple([pltpu.SemaphoreType.DMA((3 * n,))] * 2 + [pltpu.HBM(s.shape, s.dtype) for s in sums]
                        + [pltpu.HBM(ls, s.dtype) for ls, s in zip(land_shapes, sums)]
                        + [jax.ShapeDtypeStruct((SUBLANES, LANES), F32)]),
        in_specs=tuple([HBM_SPEC] * (2 * n) + [ANY_SPEC]),
        out_specs=tuple([SEM_SPEC] * 2 + [HBM_SPEC] * (2 * n) + [pl.BlockSpec(memory_space=pltpu.VMEM)]),
        input_output_aliases={i: 2 + i for i in range(2 * n)},
        compiler_params=pltpu.CompilerParams(has_side_effects=_DATAFLOW, collective_id=barrier_id),
    )(*[_hbm(s) for s in sums], *[_hbm(lax.empty(ls, s.dtype)) for ls, s in zip(land_shapes, sums)], after)
    return outs[0], outs[1], list(outs[2:2 + n]), list(outs[2 + n:2 + 2 * n]), outs[-1]


def _chip_exchange_finish(sums, lands, send_sems, recv_sems, after, name):
    n = len(sums)

    def body(*refs):
        ins, land_refs = refs[:n], refs[n:2 * n]
        send, recv = refs[2 * n], refs[2 * n + 1]
        x, y, c = _mesh_pos()
        for t in range(n):
            for r in range(3):
                cp = pltpu.make_async_remote_copy(
                    src_ref=ins[t].at[r], dst_ref=land_refs[t].at[r], send_sem=send.at[3 * t + r],
                    recv_sem=recv.at[3 * t + r],
                    device_id=(x, y, 1 - c), device_id_type=MESH)
                cp.wait_send()
                cp.wait_recv()

    outs = pl.pallas_call(
        body, name=name,
        out_shape=tuple(pltpu.HBM(g.shape, g.dtype) for g in lands),
        in_specs=tuple([HBM_SPEC] * (2 * n) + [SEM_SPEC] * 2 + [ANY_SPEC]),
        out_specs=tuple([HBM_SPEC] * n),
        input_output_aliases={n + i: i for i in range(n)},
        compiler_params=pltpu.CompilerParams(has_side_effects=_DATAFLOW),
    )(*sums, *lands, send_sems, recv_sems, after)
    return list(outs)


def _mm(a, b, *, mode, tm, tn, tk=None, b_blocked=False, out_blocked=False, out_dtypes=(F32,),
        epilogue=None, extras=(), after=None, kb=1, out_places=None, row_splits=1, out_block=None, name):
    if mode == "nn":
        m, k = a.shape
        n = b.shape[0] * b.shape[2] if b_blocked else b.shape[1]
        dims = (((1,), (0,)), ((), ()))
    elif mode == "nt":
        m, k = a.shape
        n = b.shape[1] if b_blocked else b.shape[0]
        if b_blocked:
            tk = kb * b.shape[2]
        dims = (((1,), (1,)), ((), ()))
    else:
        k, m = a.shape
        n = b.shape[1]
        dims = (((0,), (0,)), ((), ()))
    tk = k if tk is None else tk
    assert m % tm == 0 and n % tn == 0 and k % tk == 0, (name, m, n, k, tm, tn, tk)
    gm, gn, gk = m // tm, n // tn, k // tk
    if b_blocked:
        assert (tn if mode == "nn" else tk) == kb * b.shape[2], name
    if row_splits > 1:
        assert gk == 1 and epilogue is not None and mode != "tn" and not b_blocked and tm % (16 * row_splits) == 0, name

    if mode == "nn":
        a_spec = pl.BlockSpec((tm, tk), lambda i, j, kk: (i, kk))
        b_spec = (pl.BlockSpec((None, tk, tn), lambda i, j, kk: (j, kk, 0)) if b_blocked
                  else pl.BlockSpec((tk, tn), lambda i, j, kk: (kk, j)))
    elif mode == "nt":
        a_spec = pl.BlockSpec((tm, tk), lambda i, j, kk: (i, kk))
        b_spec = (pl.BlockSpec((kb, tn, tk // kb), lambda i, j, kk: (kk, j, 0)) if b_blocked
                  else pl.BlockSpec((tn, tk), lambda i, j, kk: (j, kk)))
    else:
        a_spec = pl.BlockSpec((tk, tm), lambda i, j, kk: (kk, i))
        b_spec = pl.BlockSpec((tk, tn), lambda i, j, kk: (kk, j))
    out_pack = 1
    if out_blocked and out_block is not None and out_block != tn:
        assert tn % out_block == 0, name
        out_pack = tn // out_block
        out_spec = pl.BlockSpec((out_pack, tm, out_block), lambda i, j, kk: (j, i, 0))
        out_shape = (n // out_block, m, out_block)
    elif out_blocked:
        out_spec = pl.BlockSpec((None, tm, tn), lambda i, j, kk: (j, i, 0))
        out_shape = (gn, m, tn)
    else:
        out_spec = pl.BlockSpec((tm, tn), lambda i, j, kk: (i, j))
        out_shape = (m, n)
    extra_specs = [pl.BlockSpec((tm, tn), functools.partial(lambda i, j, kk, off: (i, j + off), off=off))
                   for _, off in extras]
    n_extra, n_out = len(extras), len(out_dtypes)
    n_after = 0 if after is None else 1
    places = out_places if out_places is not None else (None,) * n_out

    def body(a_ref, b_ref, *rest):
        extra_refs = rest[:n_extra]
        out_refs = rest[n_extra + n_after:n_extra + n_after + n_out]

        def finish(acc):
            if epilogue is None:
                res = (acc,)
            else:
                res = epilogue(acc, *[e[...] for e in extra_refs])
            for o_ref, r in zip(out_refs, res):
                if out_pack == 1:
                    o_ref[...] = r.astype(o_ref.dtype)
                else:
                    for h in range(out_pack):
                        o_ref[h] = r[:, h * out_block:(h + 1) * out_block].astype(o_ref.dtype)

        if row_splits > 1:
            strip = tm // row_splits
            for h in range(row_splits):
                rows = slice(h * strip, (h + 1) * strip)
                acc = lax.dot_general(a_ref[rows, :], b_ref[...], dims, preferred_element_type=F32)
                res = epilogue(acc, *[e[rows, :] for e in extra_refs])
                for o_ref, r in zip(out_refs, res):
                    o_ref[rows, :] = r.astype(o_ref.dtype)
            return
        if mode == "nt" and b_blocked:
            bk = tk // kb
            part = lax.dot_general(a_ref[:, :bk], b_ref[0], dims, preferred_element_type=F32)
            for h in range(1, kb):
                part = part + lax.dot_general(a_ref[:, h * bk:(h + 1) * bk], b_ref[h], dims,
                                              preferred_element_type=F32)
        else:
            part = lax.dot_general(a_ref[...], b_ref[...], dims, preferred_element_type=F32)
        if gk == 1:
            finish(part)
        else:
            acc_ref = rest[-1]
            kk = pl.program_id(2)

            @pl.when(kk == 0)
            def _():
                acc_ref[...] = part

            @pl.when(kk > 0)
            def _():
                acc_ref[...] += part

            @pl.when(kk == gk - 1)
            def _():
                finish(acc_ref[...])

    outs = pl.pallas_call(
        body, name=name, grid=(gm, gn, gk),
        in_specs=[a_spec, b_spec] + extra_specs + [ANY_SPEC] * n_after,
        out_specs=[out_spec if place is None else
                   pl.BlockSpec((tm, tn), functools.partial(lambda i, j, kk, off: (i, j + off), off=place[1] // tn))
                   for place in places],
        out_shape=[jax.ShapeDtypeStruct(out_shape if place is None else (m, place[0]), dt)
                   for dt, place in zip(out_dtypes, places)],
        scratch_shapes=[pltpu.VMEM((tm, tn), F32)] if gk > 1 else [],
        compiler_params=_cparams("parallel", "parallel", "arbitrary"),
    )(a, b, *[e for e, _ in extras], *([] if after is None else [after]))
    return outs[0] if n_out == 1 else outs


def _mm_slots(a, w, slots, *, over, tm, tn=None, out_dtype=F32, epilogue=None, base=None, name):
    m = a.shape[0]
    ns = slots.shape[0]
    n_slots, w1, w2 = w.shape
    assert m % tm == 0
    if over == "n":
        k, bn = w1, w2

        def body(slots_ref, a_ref, w_ref, *rest):
            out_ref = rest[-1]
            acc = jnp.dot(a_ref[...], w_ref[...], preferred_element_type=F32)
            out_ref[...] = (acc if epilogue is None else epilogue(acc)).astype(out_ref.dtype)

        in_specs = [pl.BlockSpec((tm, k), lambda i, j, s: (i, 0)),
                    pl.BlockSpec((None, k, bn), lambda i, j, s: (s[j], 0, 0))]
        args = [a, w]
        aliases = {}
        if base is not None:
            in_specs.append(ANY_SPEC)
            args.append(base)
            aliases = {3: 0}
        return pl.pallas_call(
            body, name=name,
            grid_spec=pltpu.PrefetchScalarGridSpec(
                num_scalar_prefetch=1, grid=(m // tm, ns), in_specs=in_specs,
                out_specs=pl.BlockSpec((tm, bn), lambda i, j, s: (i, s[j]))),
            out_shape=jax.ShapeDtypeStruct((m, n_slots * bn), out_dtype),
            input_output_aliases=aliases,
            compiler_params=_cparams("parallel", "arbitrary"),
        )(slots, *args)

    bk, n = w1, w2
    tn = n if tn is None else tn
    assert n % tn == 0

    def body(slots_ref, a_ref, w_ref, *rest):
        out_ref, acc_ref = rest[-2], rest[-1]
        kk = pl.program_id(2)
        part = jnp.dot(a_ref[...], w_ref[...], preferred_element_type=F32)

        @pl.when(kk == 0)
        def _():
            acc_ref[...] = part if base is None else part + rest[0][...]

        @pl.when(kk > 0)
        def _():
            acc_ref[...] += part

        @pl.when(kk == ns - 1)
        def _():
            out_ref[...] = acc_ref[...].astype(out_ref.dtype)

    in_specs = [pl.BlockSpec((tm, bk), lambda i, j, kk, s: (i, s[kk])),
                pl.BlockSpec((None, bk, tn), lambda i, j, kk, s: (s[kk], 0, j))]
    args = [a, w]
    if base is not None:
        in_specs.append(pl.BlockSpec((tm, tn), lambda i, j, kk, s: (i, j)))
        args.append(base)
    return pl.pallas_call(
        body, name=name,
        grid_spec=pltpu.PrefetchScalarGridSpec(
            num_scalar_prefetch=1, grid=(m // tm, n // tn, ns), in_specs=in_specs,
            out_specs=pl.BlockSpec((tm, tn), lambda i, j, kk, s: (i, j)),
            scratch_shapes=[pltpu.VMEM((tm, tn), F32)]),
        out_shape=jax.ShapeDtypeStruct((m, n), out_dtype),
        compiler_params=_cparams("parallel", "parallel", "arbitrary"),
    )(slots, *args)


def _gate_mix(ya_pre, s, wpo, wco, proj, d_model, name):
    lp, width = ya_pre.shape
    nb, _, bn = wpo.shape
    ga_off = (proj.shape[1] - 2 * d_model) // bn
    gb_off = (proj.shape[1] - d_model) // bn

    n_strips = 4 if lp % 64 == 0 else 1

    def body(ya_ref, s_ref, wpo_ref, wco_ref, ga_ref, gb_ref, m_ref, y_a_ref, y_b_ref):
        strip = lp // n_strips
        for h in range(n_strips):
            rows = slice(h * strip, (h + 1) * strip)
            y_a = jnp.dot(ya_ref[rows, :], wpo_ref[...], preferred_element_type=F32)
            y_b = jnp.dot(s_ref[rows, :], wco_ref[...], preferred_element_type=F32)
            m = jax.nn.sigmoid(ga_ref[rows, :]) * y_a + jax.nn.sigmoid(gb_ref[rows, :]) * y_b
            m_ref[rows, :] = m.astype(BF16)
            y_a_ref[rows, :] = y_a.astype(BF16)
            y_b_ref[rows, :] = y_b.astype(BF16)

    act_spec = pl.BlockSpec((lp, width), lambda j: (0, 0))
    w_spec = pl.BlockSpec((None, width, bn), lambda j: (j, 0, 0))
    out_spec = pl.BlockSpec((lp, bn), lambda j: (0, j))
    return pl.pallas_call(
        body, name=name, grid=(nb,),
        in_specs=[act_spec, act_spec, w_spec, w_spec,
                  pl.BlockSpec((lp, bn), lambda j: (0, j + ga_off)),
                  pl.BlockSpec((lp, bn), lambda j: (0, j + gb_off))],
        out_specs=[out_spec] * 3,
        out_shape=[jax.ShapeDtypeStruct((lp, nb * bn), BF16)] * 3,
        compiler_params=_cparams("parallel"),
    )(ya_pre, s, wpo, wco, proj, proj)


def _rms_stats(x):
    return lax.rsqrt(jnp.mean(x * x, axis=-1, keepdims=True) + RMS_EPS)


def _rms_bwd(x, g, dy):
    r = _rms_stats(x)
    nrm = x * r
    dn = dy * g
    dx = r * (dn - nrm * jnp.mean(dn * nrm, axis=-1, keepdims=True))
    return dx, dy * nrm


def _rowwise(body, ins, outs, accs, *, lp, name):
    tr = _row_tile(lp, max(a.shape[1] for a in ins))
    n_in, n_out, n_acc = len(ins), len(outs), len(accs)

    def kernel_body(*refs):
        i = pl.program_id(0)
        acc_refs = refs[n_in + n_out:]

        @pl.when(i == 0)
        def _():
            for r in acc_refs:
                r[...] = jnp.zeros_like(r)

        body(i * tr, refs[:n_in], refs[n_in:n_in + n_out], acc_refs)

    in_specs = []
    for a in ins:
        if a.shape[0] == lp:
            in_specs.append(pl.BlockSpec((tr, a.shape[1]), lambda i: (i, 0)))
        else:
            in_specs.append(pl.BlockSpec(a.shape, lambda i: (0, 0)))
    out_specs = [pl.BlockSpec((tr, w), lambda i: (i, 0)) for w, _ in outs]
    out_specs += [pl.BlockSpec((SUBLANES, w), lambda i: (0, 0)) for w in accs]
    out_shape = [jax.ShapeDtypeStruct((lp, w), d) for w, d in outs]
    out_shape += [jax.ShapeDtypeStruct((SUBLANES, w), F32) for w in accs]
    return pl.pallas_call(
        kernel_body, name=name, grid=(lp // tr,), in_specs=in_specs, out_specs=out_specs,
        out_shape=out_shape, compiler_params=_cparams("arbitrary"),
    )(*ins)


SHIFT_TILE = 128


def _shifted_specs(width, n_big, n_small):
    per = SHIFT_TILE // N_META
    small = pl.BlockSpec((N_META, width), lambda i: (jnp.clip(per * i - 1, 0, n_small - 1), 0))
    big = pl.BlockSpec((SHIFT_TILE, width), lambda i: (jnp.minimum(i, n_big - 1), 0))
    return small, big


def _rms_pre(meta_full, x2, g, lp, seq):
    d = x2.shape[1]
    assert seq % SHIFT_TILE == 0 and lp % SHIFT_TILE == 0 and SHIFT_TILE % N_META == 0

    def body(meta_ref, xs_ref, xb_ref, g_ref, h0_ref, u1_ref):
        i = pl.program_id(0)
        head = jnp.where(i == 0, meta_ref[...], xs_ref[...])
        rows = jnp.concatenate([head, xb_ref[:SHIFT_TILE - N_META, :]], axis=0)
        r = i * SHIFT_TILE + lax.broadcasted_iota(jnp.int32, (SHIFT_TILE, 1), 0)
        rows = jnp.where(r < N_META + seq, rows, 0.0)
        h0_ref[...] = rows
        u1_ref[...] = (rows * _rms_stats(rows) * g_ref[...]).astype(BF16)

    small, big = _shifted_specs(d, seq // SHIFT_TILE, seq // N_META)
    tile = pl.BlockSpec((SHIFT_TILE, d), lambda i: (i, 0))
    return pl.pallas_call(
        body, name="rms_pre", grid=(lp // SHIFT_TILE,),
        in_specs=[pl.BlockSpec((N_META, d), lambda i: (0, 0)), small, big, pl.BlockSpec((1, d), lambda i: (0, 0))],
        out_specs=[tile, tile],
        out_shape=[jax.ShapeDtypeStruct((lp, d), F32), jax.ShapeDtypeStruct((lp, d), BF16)],
        compiler_params=_cparams("parallel"),
    )(meta_full, x2, x2, g)


def _post_mix(o, h0, g_post_mix, g_pre_mlp, lp):
    d = h0.shape[1]

    def body(row0, ins, outs, accs):
        o_ref, h0_ref, g1_ref, g2_ref = ins
        o_v = o_ref[...]
        h1 = h0_ref[...] + o_v * _rms_stats(o_v) * g1_ref[...]
        outs[0][...] = h1
        outs[1][...] = (h1 * _rms_stats(h1) * g2_ref[...]).astype(BF16)

    return _rowwise(body, [o, h0, g_post_mix, g_pre_mlp], [(d, F32), (d, BF16)], [], lp=lp, name="post_mix")


def _loss_head(f, h1, target, g_post_mlp, lp, seq):
    d = f.shape[1]

    def body(f_ref, h1_ref, ts_ref, tb_ref, g_ref, df_ref, dh_ref, dg_ref, loss_ref):
        i = pl.program_id(0)

        @pl.when(i == 0)
        def _():
            dg_ref[...] = jnp.zeros_like(dg_ref)
            loss_ref[...] = jnp.zeros_like(loss_ref)

        f_v, g = f_ref[...], g_ref[...]
        r = _rms_stats(f_v)
        nrm = f_v * r
        rows = i * SHIFT_TILE + lax.broadcasted_iota(jnp.int32, (SHIFT_TILE, 1), 0)
        valid = (rows >= N_META) & (rows < N_META + seq)
        tgt = jnp.concatenate([ts_ref[...], tb_ref[:SHIFT_TILE - N_META, :]], axis=0)
        err = jnp.where(valid, h1_ref[...] + nrm * g - tgt, 0.0)
        loss_ref[...] += 0.5 * jnp.sum(jnp.mean(err * err, axis=-1, keepdims=True))
        dy = err * (1.0 / d)
        dn = dy * g
        df_ref[...] = (r * (dn - nrm * jnp.mean(dn * nrm, axis=-1, keepdims=True))).astype(BF16)
        dh_ref[...] = dy
        dg_ref[...] += _rowsum8(dy * nrm)

    small, big = _shifted_specs(d, seq // SHIFT_TILE, seq // N_META)
    tile = pl.BlockSpec((SHIFT_TILE, d), lambda i: (i, 0))
    return pl.pallas_call(
        body, name="loss_head", grid=(lp // SHIFT_TILE,),
        in_specs=[tile, tile, small, big, pl.BlockSpec((1, d), lambda i: (0, 0))],
        out_specs=[tile, tile, pl.BlockSpec((SUBLANES, d), lambda i: (0, 0)),
                   pl.BlockSpec((SUBLANES, LANES), lambda i: (0, 0))],
        out_shape=[jax.ShapeDtypeStruct((lp, d), BF16), jax.ShapeDtypeStruct((lp, d), F32),
                   jax.ShapeDtypeStruct((SUBLANES, d), F32), jax.ShapeDtypeStruct((SUBLANES, LANES), F32)],
        compiler_params=_cparams("arbitrary"),
    )(f, h1, target, target, g_post_mlp)


def _mid_bwd(du2, h1, dh, o, g_pre_mlp, g_post_mix, lp):
    d = h1.shape[1]

    def body(row0, ins, outs, accs):
        du2_ref, h1_ref, dh_ref, o_ref, g2_ref, g1_ref = ins
        dx2, dg2 = _rms_bwd(h1_ref[...], g2_ref[...], du2_ref[...])
        dh1 = dh_ref[...] + dx2
        do, dg1 = _rms_bwd(o_ref[...], g1_ref[...], dh1)
        outs[0][...] = dh1
        outs[1][...] = do.astype(BF16)
        accs[0][...] += _rowsum8(dg2)
        accs[1][...] += _rowsum8(dg1)

    return _rowwise(body, [du2, h1, dh, o, g_pre_mlp, g_post_mix], [(d, F32), (d, BF16)], [d, d], lp=lp,
                    name="mid_bwd")


def _pre_mix_bwd(du1, h0, dh1, g_pre_mix, seq):
    d = h0.shape[1]
    per = SHIFT_TILE // N_META
    assert seq % SHIFT_TILE == 0

    def body(du_b, h_b, dh_b, du_n, h_n, dh_n, du_m, h_m, dh_m, g_ref, gx_ref, gm_ref, dg_ref):
        i = pl.program_id(0)
        g = g_ref[...]

        @pl.when(i == 0)
        def _():
            dx, dg = _rms_bwd(h_m[...], g, du_m[...])
            gm_ref[...] = dh_m[...] + dx
            dg_ref[...] = _rowsum8(dg)

        rows = lambda big, nxt: jnp.concatenate([big[N_META:, :], nxt[...]], axis=0)
        dx, dg = _rms_bwd(rows(h_b, h_n), g, rows(du_b, du_n))
        gx_ref[...] = rows(dh_b, dh_n) + dx
        dg_ref[...] += _rowsum8(dg)

    big = pl.BlockSpec((SHIFT_TILE, d), lambda i: (i, 0))
    nxt = pl.BlockSpec((N_META, d), lambda i: (per * (i + 1), 0))
    first = pl.BlockSpec((N_META, d), lambda i: (0, 0))
    return pl.pallas_call(
        body, name="pre_mix_bwd", grid=(seq // SHIFT_TILE,),
        in_specs=[big] * 3 + [nxt] * 3 + [first] * 3 + [pl.BlockSpec((1, d), lambda i: (0, 0))],
        out_specs=[big, first, pl.BlockSpec((SUBLANES, d), lambda i: (0, 0))],
        out_shape=[jax.ShapeDtypeStruct((seq, d), F32), jax.ShapeDtypeStruct((N_META, d), F32),
                   jax.ShapeDtypeStruct((SUBLANES, d), F32)],
        compiler_params=_cparams("arbitrary"),
    )(du1, h0, dh1, du1, h0, dh1, du1, h0, dh1, g_pre_mix)


def _ln_stats(c):
    mu = jnp.mean(c, axis=-1, keepdims=True)
    var = jnp.mean(jnp.square(c - mu), axis=-1, keepdims=True)
    return mu, lax.rsqrt(var + LN_EPS)


def _ln_silu(c, ln_g, ln_b, lp):
    w = c.shape[1]

    def body(row0, ins, outs, accs):
        c_ref, g_ref, b_ref = ins
        c_v = c_ref[...]
        mu, rstd = _ln_stats(c_v)
        ln = (c_v - mu) * rstd * g_ref[...] + b_ref[...]
        outs[0][...] = (ln * jax.nn.sigmoid(ln)).astype(BF16)

    return _rowwise(body, [c, ln_g, ln_b], [(w, BF16)], [], lp=lp, name="ln_silu")[0]


def _ln_silu_bwd(c, ds, ln_g, ln_b, lp):
    w = c.shape[1]

    def body(row0, ins, outs, accs):
        c_ref, ds_ref, g_ref, b_ref = ins
        c_v, g = c_ref[...], g_ref[...]
        mu, rstd = _ln_stats(c_v)
        nrm = (c_v - mu) * rstd
        ln = nrm * g + b_ref[...]
        sig = jax.nn.sigmoid(ln)
        dln = ds_ref[...] * (sig * (1.0 + ln * (1.0 - sig)))
        dn = dln * g
        dc = rstd * (dn - jnp.mean(dn, axis=-1, keepdims=True) - nrm * jnp.mean(dn * nrm, axis=-1, keepdims=True))
        outs[0][...] = dc
        accs[0][...] += _rowsum8(dln * nrm)
        accs[1][...] += _rowsum8(dln)
        accs[2][...] += _rowsum8(dc)

    return _rowwise(body, [c, ds, ln_g, ln_b], [(w, F32)], [w, w, w], lp=lp, name="ln_silu_bwd")


def _chunk_with_history(ref, i, cols=slice(None)):
    t0 = pl.multiple_of(i * ROW_CHUNK, ROW_CHUNK)
    lo0 = pl.multiple_of(jnp.maximum(t0 - HALO, 0), SUBLANES)
    lo = jnp.where(i > 0, ref[pl.ds(lo0, HALO), cols], 0.0)
    return jnp.concatenate([lo, ref[pl.ds(t0, ROW_CHUNK), cols]], axis=0)


def _chunk_with_future(ref, i, n_chunks, cols=slice(None)):
    t0 = pl.multiple_of(i * ROW_CHUNK, ROW_CHUNK)
    hi0 = pl.multiple_of(jnp.minimum(t0 + ROW_CHUNK, (n_chunks - 1) * ROW_CHUNK), SUBLANES)
    hi = jnp.where(i < n_chunks - 1, ref[pl.ds(hi0, HALO), cols], 0.0)
    return jnp.concatenate([ref[pl.ds(t0, ROW_CHUNK), cols], hi], axis=0)


def _inv_count(t0, n_rows, window):
    pos = t0 + lax.broadcasted_iota(jnp.int32, (n_rows, 1), 0)
    return 1.0 / jnp.minimum(pos + 1, window).astype(F32)


def _pool_delta(z_hist, t0, window):
    s = z_hist
    sh = 1
    while sh < window:
        s = s + pltpu.roll(s, sh, 0)
        sh *= 2
    cur = z_hist[HALO:, :]
    return s[HALO:, :] * _inv_count(t0, ROW_CHUNK, window) - cur


def _pool_fwd(proj, wpg, pool_scale, lp):
    n_grp, gdim, _ = wpg.shape
    width = n_grp * gdim
    n_chunks = lp // ROW_CHUNK

    def body(z_ref, w_ref, sc_ref, out_ref):
        for g, window in enumerate(POOL_WINDOWS):
            cols = slice(g * gdim, (g + 1) * gdim)

            def chunk(i, carry, cols=cols, g=g, window=window):
                t0 = pl.multiple_of(i * ROW_CHUNK, ROW_CHUNK)
                d = _pool_delta(_chunk_with_history(z_ref, i, cols), t0, window)
                q = jnp.dot(d.astype(BF16), w_ref[g], preferred_element_type=F32)
                out_ref[pl.ds(t0, ROW_CHUNK), cols] = (q * sc_ref[:, cols]).astype(BF16)
                return carry

            lax.fori_loop(0, n_chunks, chunk, 0)

    return pl.pallas_call(
        body, name="pool_fwd", grid=(1,),
        in_specs=[pl.BlockSpec((lp, width), lambda i: (0, 0)),
                  pl.BlockSpec(wpg.shape, lambda i: (0, 0, 0)),
                  pl.BlockSpec(pool_scale.shape, lambda i: (0, 0))],
        out_specs=pl.BlockSpec((lp, width), lambda i: (0, 0)),
        out_shape=jax.ShapeDtypeStruct((lp, width), BF16),
        compiler_params=_cparams("arbitrary"),
    )(proj, wpg, pool_scale)


def _pool_bwd(proj, d_ya, wpg, pool_scale, dproj, lp):
    n_grp, gdim, _ = wpg.shape
    width = n_grp * gdim
    n_chunks = lp // ROW_CHUNK
    ext = ROW_CHUNK + HALO

    def body(z_ref, dya_ref, w_ref, sc_ref, _, dz_ref, dw_ref, dsc_ref):
        dw_ref[...] = jnp.zeros_like(dw_ref)
        dsc_ref[...] = jnp.zeros_like(dsc_ref)
        for g, window in enumerate(POOL_WINDOWS):
            cols = slice(g * gdim, (g + 1) * gdim)

            def chunk(i, carry, cols=cols, g=g, window=window):
                t0 = pl.multiple_of(i * ROW_CHUNK, ROW_CHUNK)
                w_g = w_ref[g]
                scale = sc_ref[:, cols]
                d = _pool_delta(_chunk_with_history(z_ref, i, cols), t0, window).astype(BF16)
                dya_ext = _chunk_with_future(dya_ref, i, n_chunks, cols)
                dya = dya_ext[:ROW_CHUNK, :]
                q = jnp.dot(d, w_g, preferred_element_type=F32)
                dsc_ref[:, cols] += _rowsum8(dya * q)
                e_ext = (dya_ext * scale).astype(BF16)
                dw_ref[g] += lax.dot_general(d, e_ext[:ROW_CHUNK, :], (((0,), (0,)), ((), ())),
                                             preferred_element_type=F32)
                dd_ext = lax.dot_general(e_ext, w_g, (((1,), (1,)), ((), ())), preferred_element_type=F32)
                s = dd_ext * _inv_count(t0, ext, window)
                sh = 1
                while sh < window:
                    s = s + pltpu.roll(s, ext - sh, 0)
                    sh *= 2
                dz_ref[pl.ds(t0, ROW_CHUNK), cols] = (s[:ROW_CHUNK, :] - dd_ext[:ROW_CHUNK, :]).astype(BF16)
                return carry

            lax.fori_loop(0, n_chunks, chunk, 0)

    blk = pl.BlockSpec((lp, width), lambda i: (0, 0))
    return pl.pallas_call(
        body, name="pool_bwd", grid=(1,),
        in_specs=[blk, blk, pl.BlockSpec(wpg.shape, lambda i: (0, 0, 0)),
                  pl.BlockSpec(pool_scale.shape, lambda i: (0, 0)), ANY_SPEC],
        out_specs=[blk, pl.BlockSpec(wpg.shape, lambda i: (0, 0, 0)),
                   pl.BlockSpec((SUBLANES, width), lambda i: (0, 0))],
        out_shape=[jax.ShapeDtypeStruct(dproj.shape, BF16), jax.ShapeDtypeStruct(wpg.shape, F32),
                   jax.ShapeDtypeStruct((SUBLANES, width), F32)],
        input_output_aliases={4: 0},
        compiler_params=_cparams("arbitrary"),
    )(proj, d_ya, wpg, pool_scale, dproj)


def _conv_fwd(proj, w_dw, b_dw, lp, width, v_col0):
    n_chunks = lp // ROW_CHUNK
    v_blk0, g_blk0 = v_col0 // LANES, (v_col0 + width) // LANES

    def body(v_ref, gc_ref, w_ref, b_ref, c_ref, a_pad):
        a_pad[pl.ds(0, HALO), :] = jnp.zeros((HALO, LANES), F32)
        a_pad[pl.ds(HALO, lp), :] = v_ref[...] * jax.nn.sigmoid(gc_ref[...])

        def chunk(i, carry):
            t0 = pl.multiple_of(i * ROW_CHUNK, ROW_CHUNK)
            hist = a_pad[pl.ds(t0, ROW_CHUNK + HALO), :]
            acc = jnp.zeros((ROW_CHUNK, LANES), F32)
            for k in range(CONV_KERNEL):
                acc = acc + w_ref[k:k + 1, :] * pltpu.roll(hist, CONV_KERNEL - 1 - k, 0)[HALO:, :]
            c_ref[pl.ds(t0, ROW_CHUNK), :] = acc + b_ref[...]
            return carry

        lax.fori_loop(0, n_chunks, chunk, 0)

    return pl.pallas_call(
        body, name="conv_fwd", grid=(width // LANES,),
        in_specs=[pl.BlockSpec((lp, LANES), lambda j: (0, j + v_blk0)),
                  pl.BlockSpec((lp, LANES), lambda j: (0, j + g_blk0)),
                  pl.BlockSpec((CONV_TAPS_PADDED, LANES), lambda j: (0, j)),
                  pl.BlockSpec((1, LANES), lambda j: (0, j))],
        out_specs=pl.BlockSpec((lp, LANES), lambda j: (0, j)),
        out_shape=jax.ShapeDtypeStruct((lp, width), F32),
        scratch_shapes=[pltpu.VMEM((lp + HALO, LANES), F32)],
        compiler_params=_cparams("parallel"),
    )(proj, proj, w_dw, b_dw)


def _conv_bwd(proj, dc, w_dw, dproj, lp, width, v_col0):
    n_chunks = lp // ROW_CHUNK
    ext = ROW_CHUNK + HALO
    v_blk0, g_blk0 = v_col0 // LANES, (v_col0 + width) // LANES

    def body(v_ref, gc_ref, dc_ref, w_ref, _, dv_ref, dgc_ref, dw_ref, a_pad, dc_pad, dw_acc):
        sig = jax.nn.sigmoid(gc_ref[...])
        a_pad[pl.ds(0, HALO), :] = jnp.zeros((HALO, LANES), F32)
        a_pad[pl.ds(HALO, lp), :] = v_ref[...] * sig
        dc_pad[pl.ds(0, lp), :] = dc_ref[...]
        dc_pad[pl.ds(lp, HALO), :] = jnp.zeros((HALO, LANES), F32)
        dw_acc[...] = jnp.zeros_like(dw_acc)

        def chunk(i, carry):
            t0 = pl.multiple_of(i * ROW_CHUNK, ROW_CHUNK)
            hist = a_pad[pl.ds(t0, ext), :]
            fut = dc_pad[pl.ds(t0, ext), :]
            dc_cur = fut[:ROW_CHUNK, :]
            da = jnp.zeros((ROW_CHUNK, LANES), F32)
            for k in range(CONV_KERNEL):
                lag = CONV_KERNEL - 1 - k
                da = da + w_ref[k:k + 1, :] * pltpu.roll(fut, (ext - lag) % ext, 0)[:ROW_CHUNK, :]
                dw_acc[pl.ds(SUBLANES * k, SUBLANES), :] += _rowsum8(dc_cur * pltpu.roll(hist, lag, 0)[HALO:, :])
            rows = pl.ds(t0, ROW_CHUNK)
            sg = jax.nn.sigmoid(gc_ref[rows, :])
            dv_ref[rows, :] = (da * sg).astype(BF16)
            dgc_ref[rows, :] = (da * v_ref[rows, :] * sg * (1.0 - sg)).astype(BF16)
            return carry

        lax.fori_loop(0, n_chunks, chunk, 0)
        dw_ref[...] = dw_acc[...].reshape(CONV_TAPS_PADDED, SUBLANES, LANES).sum(axis=1)

    col = lambda j: (0, j)
    return pl.pallas_call(
        body, name="conv_bwd", grid=(width // LANES,),
        in_specs=[pl.BlockSpec((lp, LANES), lambda j: (0, j + v_blk0)),
                  pl.BlockSpec((lp, LANES), lambda j: (0, j + g_blk0)),
                  pl.BlockSpec((lp, LANES), col),
                  pl.BlockSpec((CONV_TAPS_PADDED, LANES), col), ANY_SPEC],
        out_specs=[pl.BlockSpec((lp, LANES), lambda j: (0, j + v_blk0)), pl.BlockSpec((lp, LANES), col),
                   pl.BlockSpec((CONV_TAPS_PADDED, LANES), col)],
        out_shape=[jax.ShapeDtypeStruct(dproj.shape, BF16), jax.ShapeDtypeStruct((lp, width), BF16),
                   jax.ShapeDtypeStruct((CONV_TAPS_PADDED, width), F32)],
        scratch_shapes=[pltpu.VMEM((lp + HALO, LANES), F32), pltpu.VMEM((lp + HALO, LANES), F32),
                        pltpu.VMEM((CONV_TAPS_PADDED * SUBLANES, LANES), F32)],
        input_output_aliases={4: 0},
        compiler_params=_cparams("parallel"),
    )(proj, proj, dc, w_dw, dproj)


def _place_columns(dst, pieces, name):
    m = dst.shape[0]
    tile = 512
    counts = [p.shape[1] // tile for p, _ in pieces]
    starts = [sum(counts[:i]) for i in range(len(pieces))]
    n_steps = sum(counts)

    def local(s, i):
        return jnp.clip(s - starts[i], 0, counts[i] - 1)

    def out_index(s):
        blk = pieces[0][1] // tile + local(s, 0)
        for i in range(1, len(pieces)):
            blk = jnp.where(s >= starts[i], pieces[i][1] // tile + local(s, i), blk)
        return 0, blk

    def body(*refs):
        out_ref = refs[-1]
        s = pl.program_id(0)
        for i in range(len(pieces)):
            @pl.when((s >= starts[i]) & (s < starts[i] + counts[i]))
            def _(i=i):
                out_ref[...] = refs[i][...]

    return pl.pallas_call(
        body, name=name, grid=(n_steps,),
        in_specs=[pl.BlockSpec((m, tile), functools.partial(lambda s, i: (0, local(s, i)), i=i))
                  for i in range(len(pieces))] + [ANY_SPEC],
        out_specs=pl.BlockSpec((m, tile), out_index),
        out_shape=jax.ShapeDtypeStruct(dst.shape, dst.dtype),
        input_output_aliases={len(pieces): 0},
        compiler_params=_cparams("arbitrary"),
    )(*[p for p, _ in pieces], dst)


def _adamw_math(w, g, m, v):
    m = ADAM_B1 * m + (1.0 - ADAM_B1) * g
    v = ADAM_B2 * v + (1.0 - ADAM_B2) * jnp.square(g)
    m_hat = m / (1.0 - ADAM_B1 ** ADAM_STEP)
    v_hat = v / (1.0 - ADAM_B2 ** ADAM_STEP)
    delta = -ADAM_LR * (m_hat / (jnp.sqrt(v_hat) + ADAM_EPS) + ADAM_WD * w)
    return delta, m, v


def _pair_sum(own, recv, where, name):
    _, _, rows, cols = own.shape
    tr = _row_tile(rows, cols, 1024 * 1024)

    def body(where_ref, own_ref, recv_ref, out_ref):
        out_ref[...] = (own_ref[...].astype(F32) + recv_ref[...].astype(F32)).astype(BF16)

    return pl.pallas_call(
        body, name=name,
        grid_spec=pltpu.PrefetchScalarGridSpec(
            num_scalar_prefetch=1, grid=(3, rows // tr),
            in_specs=[pl.BlockSpec((None, None, tr, cols), lambda r, i, wh: (wh[2 + r], wh[0], i, 0)),
                      pl.BlockSpec((None, tr, cols), lambda r, i, wh: (wh[2 + r], i, 0))],
            out_specs=pl.BlockSpec((None, tr, cols), lambda r, i, wh: (wh[2 + r], i, 0))),
        out_shape=jax.ShapeDtypeStruct((4, rows, cols), BF16),
        compiler_params=_cparams("parallel", "parallel"),
    )(where, own, recv)


def _adamw_big(w, m, v, own, from_sibling, recv3, where, name):
    rows, cols = w.shape
    tr = _row_tile(rows, cols, 512 * 1024)

    def body(where_ref, w_ref, m_ref, v_ref, own_ref, sib_ref, r_ref, g_out, d_out, m_out, v_out):
        g = own_ref[...].astype(F32) + sib_ref[...].astype(F32)
        for r in range(3):
            g = g + r_ref[r].astype(F32)
        delta, m_new, v_new = _adamw_math(w_ref[...], g, m_ref[...], v_ref[...])
        g_out[...] = g
        d_out[...] = delta
        m_out[...] = m_new
        v_out[...] = v_new

    blk = pl.BlockSpec((tr, cols), lambda i, q_ref: (i, 0))
    return pl.pallas_call(
        body, name=name,
        grid_spec=pltpu.PrefetchScalarGridSpec(
            num_scalar_prefetch=1, grid=(rows // tr,),
            in_specs=[blk, blk, blk,
                      pl.BlockSpec((None, None, tr, cols), lambda i, wh: (wh[1], wh[0], i, 0)),
                      pl.BlockSpec((None, tr, cols), lambda i, wh: (wh[1], i, 0)),
                      pl.BlockSpec((3, tr, cols), lambda i, wh: (0, i, 0))],
            out_specs=[blk] * 4),
        out_shape=[jax.ShapeDtypeStruct((rows, cols), F32)] * 4,
        compiler_params=_cparams("parallel"),
    )(where, w, m, v, own, from_sibling, recv3)


def _small_update(me_idx, packed, rep_params, rep_places, meta_wmv, meta_row0, wdw_wmv, wdw_row0, loss_row0):
    n_rep = len(rep_params)
    meta_rows, meta_cols = meta_wmv[0].shape
    wdw_rows, wdw_cols = wdw_wmv[0].shape

    def body(me_ref, *refs):
        pos = 0

        def take(k):
            nonlocal pos
            out = refs[pos:pos + k]
            pos += k
            return out

        rep_in = [take(3) for _ in range(n_rep)]
        rep_g = take(n_rep)
        meta_in, (meta_g,) = take(3), take(1)
        wdw_in, (wdw_g,) = take(3), take(1)
        (loss_ref,) = take(1)
        rep_out = [take(4) for _ in range(n_rep)]
        meta_out, wdw_out = take(4), take(4)
        (loss_out,) = take(1)

        def update(wmv, g, outs):
            delta, m_new, v_new = _adamw_math(wmv[0][...], g, wmv[1][...], wmv[2][...])
            for o_ref, val in zip(outs, (g, delta, m_new, v_new)):
                o_ref[...] = val

        for wmv, g_ref, outs in zip(rep_in, rep_g, rep_out):
            g = jnp.sum(g_ref[0], axis=0, keepdims=True)
            for j in range(1, N_DEV):
                g = g + jnp.sum(g_ref[j], axis=0, keepdims=True)
            update(wmv, g, outs)
        for wmv, g_ref, outs in ((meta_in, meta_g, meta_out), (wdw_in, wdw_g, wdw_out)):
            g = g_ref[0]
            for j in range(1, N_DEV):
                g = g + g_ref[j]
            update(wmv, g, outs)
        total = loss_ref[0]
        for j in range(1, N_DEV):
            total = total + loss_ref[j]
        loss_out[...] = total

    def whole(a):
        nd = a.ndim
        return pl.BlockSpec(a.shape, lambda i, me_ref, nd=nd: (0,) * nd)

    ins, in_specs = [], []
    for wmv in rep_params:
        ins += list(wmv)
        in_specs += [whole(a) for a in wmv]
    for wmv, (row0, col0) in zip(rep_params, rep_places):
        width = wmv[0].shape[1]
        ins.append(packed)
        in_specs.append(pl.BlockSpec((N_DEV, SUBLANES, width),
                                     lambda i, me_ref, rb=row0 // SUBLANES, cb=col0 // width: (0, rb, cb)))
    ins += list(meta_wmv) + [packed]
    in_specs += [whole(a) for a in meta_wmv]
    in_specs.append(pl.BlockSpec((N_DEV, meta_rows, meta_cols),
                                 lambda i, me_ref, rb=meta_row0 // meta_rows: (0, rb, me_ref[0])))
    ins += list(wdw_wmv) + [packed]
    in_specs += [whole(a) for a in wdw_wmv]
    in_specs.append(pl.BlockSpec((N_DEV, wdw_rows, wdw_cols),
                                 lambda i, me_ref, rb=wdw_row0 // wdw_rows: (0, rb, me_ref[0])))
    ins.append(packed)
    in_specs.append(pl.BlockSpec((N_DEV, SUBLANES, LANES), lambda i, me_ref, rb=loss_row0 // SUBLANES: (0, rb, 0)))

    out_shape, out_specs = [], []
    for wmv in list(rep_params) + [meta_wmv, wdw_wmv]:
        out_shape += [jax.ShapeDtypeStruct(wmv[0].shape, F32)] * 4
        out_specs += [whole(wmv[0])] * 4
    out_shape.append(jax.ShapeDtypeStruct((SUBLANES, LANES), F32))
    out_specs.append(pl.BlockSpec((SUBLANES, LANES), lambda i, me_ref: (0, 0)))

    outs = pl.pallas_call(
        body, name="small_update",
        grid_spec=pltpu.PrefetchScalarGridSpec(num_scalar_prefetch=1, grid=(1,), in_specs=in_specs,
                                               out_specs=out_specs),
        out_shape=out_shape, compiler_params=_cparams("arbitrary"),
    )(me_idx, *ins)
    groups = [outs[4 * i:4 * i + 4] for i in range(n_rep + 2)]
    return groups[:n_rep], groups[n_rep], groups[n_rep + 1], outs[-1]


def kernel(x, meta, g_pre_mix, w_in, w_pool_grp, pool_scale, w_pool_out, w_dw, b_dw, conv_ln_g, conv_ln_b, w_conv_out, w_o, g_post_mix, g_pre_mlp, w_up, w_down, g_post_mlp, loss_target, m_meta, m_g_pre_mix, m_w_in, m_w_pool_grp, m_pool_scale, m_w_pool_out, m_w_dw, m_b_dw, m_conv_ln_g, m_conv_ln_b, m_w_conv_out, m_w_o, m_g_post_mix, m_g_pre_mlp, m_w_up, m_w_down, m_g_post_mlp, v_meta, v_g_pre_mix, v_w_in, v_w_pool_grp, v_pool_scale, v_w_pool_out, v_w_dw, v_b_dw, v_conv_ln_g, v_conv_ln_b, v_w_conv_out, v_w_o, v_g_post_mix, v_g_pre_mlp, v_w_up, v_w_down, v_g_post_mlp):
    seq, d = x.shape[1], x.shape[2]
    pool_w = pool_scale.shape[1]
    conv_w = b_dw.shape[1]
    n_grp, grp_rows, gdim = w_pool_grp.shape[1:]
    lp = _round_up(N_META + seq, ROW_CHUNK)
    tm_half = lp // 2 if (lp // 2) % 16 == 0 else lp
    c_idx = lax.axis_index("c").astype(jnp.int32)
    chip_idx = (2 * lax.axis_index("x") + lax.axis_index("y")).astype(jnp.int32)
    me_idx = 2 * chip_idx + c_idx

    pad_taps = ((0, CONV_TAPS_PADDED - CONV_KERNEL), (0, 0))
    big = dict(w_in=w_in[0], w_pool_grp=w_pool_grp[0].reshape(n_grp * grp_rows, gdim), w_pool_out=w_pool_out[0],
               w_conv_out=w_conv_out[0], w_o=w_o[0], w_up=w_up[0], w_down=w_down[0])
    big_names = list(big)
    moments = dict(w_in=(m_w_in, v_w_in), w_pool_grp=(m_w_pool_grp, v_w_pool_grp), w_pool_out=(m_w_pool_out, v_w_pool_out),
                   w_conv_out=(m_w_conv_out, v_w_conv_out), w_o=(m_w_o, v_w_o), w_up=(m_w_up, v_w_up),
                   w_down=(m_w_down, v_w_down))
    slot_idx = me_idx.reshape(1)
    sources = dict(big, meta=meta, w_dw=jnp.pad(w_dw[0], pad_taps))

    def fill(k, after):
        return _fill_slot(sources[k], slot_idx, BF16 if k in big else F32, after, "fill_" + k)

    gather_groups = [["meta", "w_dw"], ["w_in"], ["w_pool_grp", "w_pool_out", "w_conv_out", "w_o"], ["w_up"], ["w_down"]]
    started, token = _gather_start([[fill(k, slot_idx) for k in names] for names in gather_groups[:2]], slot_idx,
                                   "gather_start_first", BARRIER_IDS["gather_first"], issue_order=(0, 3, 1, 2))
    started_rest, _ = _gather_start([[fill(k, token) for k in names] for names in gather_groups[2:]], token,
                                    "gather_start_rest", BARRIER_IDS["gather_rest"])
    started += started_rest
    wg = {}
    x_idx, y_idx = lax.axis_index("x"), lax.axis_index("y")
    at = lambda px, py, pc: 4 * px + 2 * py + pc
    near_slots = jnp.stack([at(x_idx, y_idx, c_idx), at(x_idx, y_idx, 1 - c_idx), at(1 - x_idx, y_idx, c_idx),
                            at(x_idx, 1 - y_idx, c_idx), at(1 - x_idx, y_idx, 1 - c_idx),
                            at(x_idx, 1 - y_idx, 1 - c_idx)]).astype(jnp.int32)
    far_slots = jnp.stack([at(1 - x_idx, 1 - y_idx, c_idx), at(1 - x_idx, 1 - y_idx, 1 - c_idx)]).astype(jnp.int32)

    def gather_whole(gi, after_forward, after_finish):
        send, recv, lands = started[gi]
        fs, fr, lands = _gather_forward(lands, recv, after_forward(), f"gather_forward_{gi}")
        lands = _gather_finish(lands, send, recv, [(ALL_CHIPS, fs, fr)], [(ALL_CHIPS, fs, fr)], after_finish(),
                               f"gather_finish_{gi}")
        wg.update(zip(gather_groups[gi], lands))

    near_state = {}

    def gather_near(gi, after):
        send, recv, lands = started[gi]
        fs, fr, lands = _gather_forward(lands, recv, after, f"gather_forward_near_{gi}", which=NEAR)
        lands = _gather_finish(lands, send, recv, [(NEAR, fs, fr)], [], after, f"gather_finish_near_{gi}",
                               direct_sends=False)
        near_state[gi] = (fs, fr)
        return lands

    def gather_far(gi, lands, after):
        send, recv, _ = started[gi]
        fs, fr, lands = _gather_forward(lands, recv, after, f"gather_forward_far_{gi}", which=FAR)
        lands = _gather_finish(lands, send, recv, [(FAR, fs, fr)], [(NEAR,) + near_state[gi], (FAR, fs, fr)], after,
                               f"gather_finish_far_{gi}", own=False)
        wg.update(zip(gather_groups[gi], lands))
        return lands

    gather_whole(0, lambda: token, lambda: token)
    meta_full = wg["meta"].transpose(1, 0, 2).reshape(N_META, d)
    wdw_full = wg["w_dw"].transpose(1, 0, 2).reshape(CONV_TAPS_PADDED, conv_w)
    target = loss_target[0]
    h0, u1 = _rms_pre(meta_full, x[0], g_pre_mix, lp, seq)
    send, recv, w_in_lands = started[1]
    w_in_lands = _gather_finish(w_in_lands, send, recv, [], [], u1, "gather_finish_home_1", direct_sends=False)
    proj = _mm_slots(u1, w_in_lands[0], near_slots[:2], over="n", tm=tm_half, name="mm_proj_home")
    fs_far, fr_far, w_in_lands = _gather_forward(w_in_lands, recv, proj, "gather_forward_far_1", which=FAR)
    w_in_lands = _gather_finish(w_in_lands, send, recv, [(FAR, fs_far, fr_far)], [], proj, "gather_finish_far_1",
                                own=False, direct_sends=False)
    proj = _mm_slots(u1, w_in_lands[0], far_slots, over="n", tm=tm_half, base=proj, name="mm_proj_far")
    fs_near, fr_near, w_in_lands = _gather_forward(w_in_lands, recv, proj, "gather_forward_near_1", which=NEAR)
    w_in_lands = _gather_finish(w_in_lands, send, recv, [(NEAR, fs_near, fr_near)],
                                [(FAR, fs_far, fr_far), (NEAR, fs_near, fr_near)], proj, "gather_finish_near_1",
                                own=False)
    proj = _mm_slots(u1, w_in_lands[0], near_slots[2:], over="n", tm=tm_half, base=proj, name="mm_proj_near")
    wg["w_in"] = w_in_lands[0]
    conv_c = _conv_fwd(proj, wdw_full, b_dw, lp, conv_w, pool_w)
    s_act = _ln_silu(conv_c, conv_ln_g, conv_ln_b, lp)
    gather_whole(2, lambda: proj, lambda: s_act)
    wpg_full = wg["w_pool_grp"].reshape(N_DEV, n_grp, grp_rows, gdim).transpose(1, 0, 2, 3).reshape(n_grp, gdim, gdim)
    w_o_full = wg["w_o"].reshape(d, d)
    ya_pre = _pool_fwd(proj, wpg_full, pool_scale, lp)
    m_mix, y_a, y_b = _gate_mix(ya_pre, s_act, wg["w_pool_out"], wg["w_conv_out"], proj, d, "gate_mix")
    o = _mm(m_mix, w_o_full, mode="nn", tm=tm_half, tn=512, name="mm_o")
    h1, u2 = _post_mix(o, h0, g_post_mix, g_pre_mlp, lp)
    relu2 = lambda acc: jnp.square(jnp.maximum(acc, 0.0))
    (w_up_near,) = gather_near(3, u2)
    act = _mm_slots(u2, w_up_near, near_slots, over="n", tm=tm_half, out_dtype=BF16, epilogue=relu2, name="mm_up_near")
    (w_up_all,) = gather_far(3, [w_up_near], act)
    act = _mm_slots(u2, w_up_all, far_slots, over="n", tm=tm_half, out_dtype=BF16, epilogue=relu2, base=act,
                    name="mm_up_far")
    (w_down_near,) = gather_near(4, act)
    f = _mm_slots(act, w_down_near, near_slots, over="k", tm=tm_half, tn=d, name="mm_down_near")
    (w_down_all,) = gather_far(4, [w_down_near], f)
    f = _mm_slots(act, w_down_all, far_slots, over="k", tm=tm_half // 2, tn=d, base=f, name="mm_down_far")
    w_down_full = w_down_all.reshape(-1, d)

    big_out = {}

    def to_sibling(names, grads, after, tag):
        send, recv, grads, lands, token = _pair_exchange_start(grads, after, "grads_to_sibling_start_" + tag)
        return (names, send, recv, grads, lands, tag), token

    where = jnp.stack([c_idx, chip_idx, 2 * (1 - x_idx) + y_idx, 2 * x_idx + (1 - y_idx),
                       2 * (1 - x_idx) + (1 - y_idx)]).astype(jnp.int32)

    def to_owner(handle, after):
        names, send, recv, grads, from_sib, tag = handle
        grads, from_sib = _pair_exchange_finish(grads, from_sib, send, recv, after, "grads_to_sibling_finish_" + tag)
        own = [g.reshape((4, 2) + g.shape[1:]) for g in grads]
        sums = [_pair_sum(o, r, where, "pair_sum_" + k) for k, o, r in zip(names, own, from_sib)]
        send, recv, sums, lands, token = _chip_exchange_start(sums, after, "grads_to_owner_start_" + tag,
                                                              BARRIER_IDS["owner_" + tag])
        return (names, send, recv, sums, lands, own, from_sib, tag), token

    def update(handle, after):
        names, send, recv, sums, lands, own, from_sib, tag = handle
        got = _chip_exchange_finish(sums, lands, send, recv, after, "grads_to_owner_finish_" + tag)
        for k, o, s, r3 in zip(names, own, from_sib, got):
            w2 = big[k]
            shape = moments[k][0].shape
            outs = _adamw_big(w2, moments[k][0].reshape(w2.shape), moments[k][1].reshape(w2.shape), o, s, r3,
                              where, "adamw_" + k)
            big_out[k] = [a.reshape(shape) for a in outs]
        return big_out[names[-1]][0]

    df, dh, dg_post_mlp, loss_part = _loss_head(f, h1, target, g_post_mlp, lp, seq)
    d_up = _mm(df, w_down_full, mode="nt", tm=tm_half, tn=1024, out_dtypes=(BF16,), extras=[(act, 0)],
               epilogue=lambda acc, a: (acc * (2.0 * jnp.sqrt(a.astype(F32))),), row_splits=2, name="mm_d_up")
    g_w_down = _mm(act, df, mode="tn", tm=1024, tn=1024, out_dtypes=(BF16,), name="mm_g_down")
    sib_down, token = to_sibling(["w_down"], [g_w_down.reshape(N_DEV, -1, d)], slot_idx, "down")
    g_w_up = _mm(u2, d_up, mode="tn", tm=1024, tn=wg["w_up"].shape[2], out_blocked=True, out_dtypes=(BF16,),
                 after=token, name="mm_g_up")
    sib_up, token = to_sibling(["w_up"], [g_w_up], slot_idx, "up")
    pending_down, token = to_owner(sib_down, token)
    du2 = _mm(d_up, wg["w_up"], mode="nt", tm=tm_half, tn=1024, b_blocked=True, kb=2, after=token, name="mm_du2")
    pending_up, token = to_owner(sib_up, du2)
    dh1, do, dg_pre_mlp, dg_post_mix = _mid_bwd(du2, h1, dh, o, g_pre_mlp, g_post_mix, lp)

    def gate_bwd(dm, ga, gb, ya, yb):
        sa, sb = jax.nn.sigmoid(ga), jax.nn.sigmoid(gb)
        return (dm * ya.astype(F32) * sa * (1.0 - sa), dm * yb.astype(F32) * sb * (1.0 - sb), dm * sa, dm * sb)

    gate_tn = 512
    ga_col0, gb_col0 = proj.shape[1] - 2 * d, proj.shape[1] - d
    dproj, d_gb, d_ya, d_yb = _mm(
        do, w_o_full, mode="nt", tm=tm_half, tn=gate_tn, out_dtypes=(BF16,) * 4,
        extras=[(proj, ga_col0 // gate_tn), (proj, gb_col0 // gate_tn), (y_a, 0), (y_b, 0)], epilogue=gate_bwd,
        out_places=((proj.shape[1], ga_col0), None, None, None), after=token, row_splits=2, name="mm_dm")
    g_w_o = _mm(m_mix, do, mode="tn", tm=1024, tn=1024, out_dtypes=(BF16,), name="mm_g_o")
    bn_out = wg["w_pool_out"].shape[2]
    g_w_pool_out = _mm(ya_pre, d_ya, mode="tn", tm=pool_w, tn=4 * bn_out, out_blocked=True, out_block=bn_out,
                       out_dtypes=(BF16,), name="mm_g_pool_out")
    g_w_conv_out = _mm(s_act, d_yb, mode="tn", tm=conv_w, tn=4 * bn_out, out_blocked=True, out_block=bn_out,
                       out_dtypes=(BF16,), name="mm_g_conv_out")
    sib_mix, token = to_sibling(["w_o", "w_pool_out", "w_conv_out"],
                                [g_w_o.reshape(N_DEV, -1, d), g_w_pool_out, g_w_conv_out], slot_idx, "mix")
    d_ya_pre = _mm(d_ya, wg["w_pool_out"], mode="nt", tm=tm_half, tn=pool_w, b_blocked=True, kb=4, after=token,
                   name="mm_d_ya_pre")
    d_s = _mm(d_yb, wg["w_conv_out"], mode="nt", tm=tm_half, tn=conv_w, b_blocked=True, kb=4, name="mm_d_s")
    pending_mix, token = to_owner(sib_mix, d_s)
    dproj, g_wpg, d_scale = _pool_bwd(proj, d_ya_pre, wpg_full, pool_scale, dproj, lp)
    dc, d_ln_g, d_ln_b, d_b_dw = _ln_silu_bwd(conv_c, d_s, conv_ln_g, conv_ln_b, lp)
    dproj, dgc, g_wdw = _conv_bwd(proj, dc, wdw_full, dproj, lp, conv_w, pool_w)
    dproj = _place_columns(dproj, [(dgc, pool_w + conv_w), (d_gb, gb_col0)], "place_dproj")
    g_w_in = _mm(u1, dproj, mode="tn", tm=1024, tn=wg["w_in"].shape[2], out_blocked=True, out_dtypes=(BF16,),
                 after=token, name="mm_g_in")
    g_wpg_slots = g_wpg.astype(BF16).reshape(n_grp, N_DEV, grp_rows, gdim).transpose(1, 0, 2, 3)
    sib_in, token = to_sibling(["w_pool_grp", "w_in"],
                               [g_wpg_slots.reshape(N_DEV, n_grp * grp_rows, gdim), g_w_in], slot_idx, "in")
    done = update(pending_down, token)
    pending_in, token = to_owner(sib_in, done)
    done = update(pending_up, token)
    du1 = _mm(dproj, wg["w_in"], mode="nt", tm=tm_half // 2, tn=1024, b_blocked=True, kb=4, after=done, name="mm_du1")
    grad_x2, grad_meta_part, dg_pre_mix = _pre_mix_bwd(du1, h0, dh1, g_pre_mix, seq)
    grad_x = grad_x2[None]

    assert pool_w + conv_w == d and conv_w <= d and LANES <= d
    widen = lambda a: jnp.pad(a, ((0, 0), (0, d - a.shape[1])))
    packed = jnp.concatenate([
        dg_pre_mix, dg_post_mix, dg_pre_mlp, dg_post_mlp,
        jnp.concatenate([d_scale, d_ln_g], axis=1), jnp.concatenate([d_ln_b, d_b_dw], axis=1),
        grad_meta_part, widen(g_wdw), widen(loss_part)], axis=0)
    rep = dict(g_pre_mix=((g_pre_mix, m_g_pre_mix, v_g_pre_mix), (0, 0)),
               g_post_mix=((g_post_mix, m_g_post_mix, v_g_post_mix), (SUBLANES, 0)),
               g_pre_mlp=((g_pre_mlp, m_g_pre_mlp, v_g_pre_mlp), (2 * SUBLANES, 0)),
               g_post_mlp=((g_post_mlp, m_g_post_mlp, v_g_post_mlp), (3 * SUBLANES, 0)),
               pool_scale=((pool_scale, m_pool_scale, v_pool_scale), (4 * SUBLANES, 0)),
               conv_ln_g=((conv_ln_g, m_conv_ln_g, v_conv_ln_g), (4 * SUBLANES, pool_w)),
               conv_ln_b=((conv_ln_b, m_conv_ln_b, v_conv_ln_b), (5 * SUBLANES, 0)),
               b_dw=((b_dw, m_b_dw, v_b_dw), (5 * SUBLANES, conv_w)))
    meta_row0 = 6 * SUBLANES
    wdw_row0 = meta_row0 + N_META
    loss_row0 = wdw_row0 + CONV_TAPS_PADDED
    (small_started,), token = _gather_start([[_fill_slot(packed, slot_idx, F32, slot_idx, "fill_small")]], grad_x2,
                                            "gather_small_start", BARRIER_IDS["gather_small"])
    done = update(pending_mix, token)
    done = update(pending_in, done)
    send, recv, lands = small_started
    fsend, frecv, lands = _gather_forward(lands, recv, done, "gather_small_forward")
    (packed_all,) = _gather_finish(lands, send, recv, [(ALL_CHIPS, fsend, frecv)], [(ALL_CHIPS, fsend, frecv)], done,
                                   "gather_small_finish")
    rep_names = list(rep)
    wdw_wmv = [jnp.pad(a[0], pad_taps) for a in (w_dw, m_w_dw, v_w_dw)]
    rep_out, meta_out, wdw_out, loss_blk = _small_update(
        slot_idx, packed_all, [rep[k][0] for k in rep_names], [rep[k][1] for k in rep_names],
        (meta, m_meta, v_meta), meta_row0, wdw_wmv, wdw_row0, loss_row0)
    small_out = dict(zip(rep_names, rep_out))
    small_out["meta"] = meta_out
    small_out["w_dw"] = [a[:CONV_KERNEL][None] for a in wdw_out]

    order = ["meta", "g_pre_mix", "w_in", "w_pool_grp", "pool_scale", "w_pool_out", "w_dw", "b_dw", "conv_ln_g",
             "conv_ln_b", "w_conv_out", "w_o", "g_post_mix", "g_pre_mlp", "w_up", "w_down", "g_post_mlp"]
    by_name = {**big_out, **small_out}
    result = [loss_blk[0, 0], grad_x]
    for kind in range(4):
        result += [by_name[k][kind] for k in order]
    return tuple(result)
```

```python
import functools

import jax
import jax.numpy as jnp
from jax import lax
from jax.experimental import pallas as pl
from jax.experimental.pallas import tpu as pltpu

F32 = jnp.float32
BF16 = jnp.bfloat16
MESH = pl.DeviceIdType.MESH

N_DEV = 8
N_META = 16
POOL_WINDOWS = (2, 4, 8, 16)
CONV_KERNEL = 31
CONV_TAPS_PADDED = 32
RMS_EPS = 1e-6
LN_EPS = 1e-5
ADAM_LR = 0.001
ADAM_B1 = 0.9
ADAM_B2 = 0.999
ADAM_EPS = 1e-08
ADAM_WD = 0.01
ADAM_STEP = 10

LANES = 128
SUBLANES = 8
ROW_CHUNK = 128
HALO = 32
VMEM_LIMIT_BYTES = 56 * 1024 * 1024


def _cparams(*sem):
    return pltpu.CompilerParams(dimension_semantics=sem if sem else None, vmem_limit_bytes=VMEM_LIMIT_BYTES)


def _round_up(n, m):
    return (n + m - 1) // m * m


def _row_tile(rows, cols, max_elems=640 * 1024):
    best = None
    for t in range(16, rows + 1, 16):
        if rows % t == 0 and (best is None or t * cols <= max_elems):
            best = t
    assert best is not None, (rows, cols)
    return best


def _rowsum8(a):
    t, w = a.shape
    return a.reshape(t // SUBLANES, SUBLANES, w).sum(axis=0)


def _mesh_pos():
    return lax.axis_index("x"), lax.axis_index("y"), lax.axis_index("c")


HBM_SPEC = pl.BlockSpec(memory_space=pltpu.HBM)
SEM_SPEC = pl.BlockSpec(memory_space=pltpu.SEMAPHORE)
ANY_SPEC = pl.BlockSpec(memory_space=pl.ANY)
_DATAFLOW = pltpu.SideEffectType.DATAFLOW_SIDE_EFFECTING


SIBLING_BARRIER_ID = 1
BARRIER_IDS = {name: 2 + i for i, name in enumerate(
    ["gather_first", "gather_rest", "gather_small", "owner_down", "owner_up", "owner_mix", "owner_in"])}


def _sibling_handshake():
    x, y, c = _mesh_pos()
    barrier = pltpu.get_barrier_semaphore()
    pl.semaphore_signal(barrier, inc=1, device_id=(x, y, 1 - c), device_id_type=MESH)
    pl.semaphore_wait(barrier, 1)


def _peer_handshake(peers):
    barrier = pltpu.get_barrier_semaphore()
    for peer in peers:
        pl.semaphore_signal(barrier, inc=1, device_id=peer, device_id_type=MESH)
    pl.semaphore_wait(barrier, len(peers))


def _hbm(a):
    return pltpu.with_memory_space_constraint(a, pltpu.HBM)


def _slot(p):
    return 4 * p[0] + 2 * p[1] + p[2]


def _fill_slot(w, slot_idx, dtype, after, name):
    rows, cols = w.shape
    tr = _row_tile(rows, cols) if rows % 16 == 0 else rows

    def body(idx_ref, w_ref, _, out_ref):
        out_ref[...] = w_ref[...].astype(dtype)

    return pl.pallas_call(
        body, name=name,
        grid_spec=pltpu.PrefetchScalarGridSpec(
            num_scalar_prefetch=1, grid=(rows // tr,),
            in_specs=[pl.BlockSpec((tr, cols), lambda i, idx_ref: (i, 0)), ANY_SPEC],
            out_specs=pl.BlockSpec((None, tr, cols), lambda i, idx_ref: (idx_ref[0], i, 0))),
        out_shape=jax.ShapeDtypeStruct((N_DEV, rows, cols), dtype),
        compiler_params=_cparams("parallel"),
    )(slot_idx, w, after)


def _gather_start(groups, after, name, barrier_id, issue_order=(0, 1, 2, 3)):
    flat = [g for grp in groups for g in grp]
    n, n_grp = len(flat), len(groups)

    def body(*refs):
        lands = refs[:n]
        sems = refs[n + 1:n + 1 + 2 * n_grp]
        token = refs[-1]
        x, y, c = _mesh_pos()
        targets = [(x, y, 1 - c), (1 - x, y, c), (x, 1 - y, c), (1 - x, 1 - y, c)]
        _peer_handshake(targets)
        t = 0
        for gi, grp in enumerate(groups):
            for ti in range(len(grp)):
                mine = lands[t].at[_slot((x, y, c))]
                for k in issue_order:
                    pltpu.make_async_remote_copy(
                        src_ref=mine, dst_ref=mine,
                        send_sem=sems[2 * gi].at[4 * ti + k], recv_sem=sems[2 * gi + 1].at[4 * ti + k],
                        device_id=targets[k], device_id_type=MESH).start()
                t += 1
        token[...] = jnp.zeros_like(token)

    sem_shapes = []
    for grp in groups:
        sem_shapes += [pltpu.SemaphoreType.DMA((4 * len(grp),))] * 2
    outs = pl.pallas_call(
        body, name=name,
        out_shape=tuple(sem_shapes + [pltpu.HBM(g.shape, g.dtype) for g in flat]
                        + [jax.ShapeDtypeStruct((SUBLANES, LANES), F32)]),
        in_specs=tuple([HBM_SPEC] * n + [ANY_SPEC]),
        out_specs=tuple([SEM_SPEC] * (2 * n_grp) + [HBM_SPEC] * n + [pl.BlockSpec(memory_space=pltpu.VMEM)]),
        input_output_aliases={i: 2 * n_grp + i for i in range(n)},
        compiler_params=pltpu.CompilerParams(has_side_effects=_DATAFLOW, collective_id=barrier_id),
    )(*[_hbm(g) for g in flat], after)
    sems, lands, token = outs[:2 * n_grp], outs[2 * n_grp:-1], outs[-1]
    res, t = [], 0
    for gi, grp in enumerate(groups):
        res.append((sems[2 * gi], sems[2 * gi + 1], list(lands[t:t + len(grp)])))
        t += len(grp)
    return res, token


NEAR, FAR, ALL_CHIPS = (0, 1), (2,), (0, 1, 2)


def _gather_forward(lands, recv_sems, after, name, which=ALL_CHIPS):
    n, nw = len(lands), len(which)

    def body(*refs):
        land_refs, recv, _ = refs[:n], refs[n], refs[n + 1]
        fsend, frecv = refs[n + 2], refs[n + 3]
        x, y, c = _mesh_pos()
        chips = [(1 - x, y), (x, 1 - y), (1 - x, 1 - y)]
        _sibling_handshake()
        for t in range(n):
            for i, j in enumerate(which):
                blk = land_refs[t].at[_slot((*chips[j], c))]
                pltpu.make_async_remote_copy(src_ref=blk, dst_ref=blk, send_sem=fsend.at[nw * t + i],
                                             recv_sem=recv.at[4 * t + 1 + j],
                                             device_id=(x, y, 1 - c), device_id_type=MESH).wait_recv()
                pltpu.make_async_remote_copy(src_ref=blk, dst_ref=blk, send_sem=fsend.at[nw * t + i],
                                             recv_sem=frecv.at[nw * t + i],
                                             device_id=(x, y, 1 - c), device_id_type=MESH).start()

    outs = pl.pallas_call(
        body, name=name,
        out_shape=tuple([pltpu.SemaphoreType.DMA((nw * n,))] * 2 + [pltpu.HBM(g.shape, g.dtype) for g in lands]),
        in_specs=tuple([HBM_SPEC] * n + [SEM_SPEC, ANY_SPEC]),
        out_specs=tuple([SEM_SPEC] * 2 + [HBM_SPEC] * n),
        input_output_aliases={i: 2 + i for i in range(n)},
        compiler_params=pltpu.CompilerParams(has_side_effects=_DATAFLOW, collective_id=SIBLING_BARRIER_ID),
    )(*lands, recv_sems, after)
    return outs[0], outs[1], list(outs[2:])


def _gather_finish(lands, send_sems, recv_sems, arrivals, sends, after, name, own=True, direct_sends=True):
    n = len(lands)
    fwd = list(arrivals) + list(sends)
    sem_args = [send_sems, recv_sems]
    where = []
    for _, fs, fr in fwd:
        pos = []
        for arr in (fs, fr):
            hit = [i for i, have in enumerate(sem_args) if have is arr]
            if not hit:
                sem_args.append(arr)
                hit = [len(sem_args) - 1]
            pos.append(hit[0])
        where.append(pos)

    def body(*refs):
        land_refs = refs[:n]
        send, recv = refs[n], refs[n + 1]
        fwd_refs = [refs[n + p] for pos in where for p in pos]
        x, y, c = _mesh_pos()
        sibling = (x, y, 1 - c)
        chips = [(1 - x, y), (x, 1 - y), (1 - x, 1 - y)]

        def desc(ref, s_sem, r_sem):
            return pltpu.make_async_remote_copy(src_ref=ref, dst_ref=ref, send_sem=s_sem, recv_sem=r_sem,
                                                device_id=sibling, device_id_type=MESH)

        for t in range(n):
            mine = land_refs[t].at[_slot((x, y, c))]
            if own:
                desc(land_refs[t].at[_slot(sibling)], send.at[4 * t], recv.at[4 * t]).wait_recv()
            for a, (which, _, _) in enumerate(fwd):
                fs, fr = fwd_refs[2 * a], fwd_refs[2 * a + 1]
                for i, j in enumerate(which):
                    if a < len(arrivals):
                        desc(land_refs[t].at[_slot((*chips[j], 1 - c))], fs.at[len(which) * t + i],
                             fr.at[len(which) * t + i]).wait_recv()
                    else:
                        desc(land_refs[t].at[_slot((*chips[j], c))], fs.at[len(which) * t + i],
                             fr.at[len(which) * t + i]).wait_send()
            if direct_sends:
                for k in range(4):
                    desc(mine, send.at[4 * t + k], recv.at[4 * t + k]).wait_send()

    outs = pl.pallas_call(
        body, name=name,
        out_shape=tuple(pltpu.HBM(g.shape, g.dtype) for g in lands),
        in_specs=tuple([HBM_SPEC] * n + [SEM_SPEC] * len(sem_args) + [ANY_SPEC]),
        out_specs=tuple([HBM_SPEC] * n),
        input_output_aliases={i: i for i in range(n)},
        compiler_params=pltpu.CompilerParams(has_side_effects=_DATAFLOW),
    )(*lands, *sem_args, after)
    return list(outs)


def _pair_exchange_start(grads, after, name):
    n = len(grads)

    def body(*refs):
        ins, lands = refs[:n], refs[n:2 * n]
        send, recv = refs[2 * n + 1], refs[2 * n + 2]
        token = refs[-1]
        x, y, c = _mesh_pos()
        _sibling_handshake()
        for t in range(n):
            for q in range(4):
                pltpu.make_async_remote_copy(
                    src_ref=ins[t].at[2 * q + 1 - c], dst_ref=lands[t].at[q],
                    send_sem=send.at[4 * t + q], recv_sem=recv.at[4 * t + q],
                    device_id=(x, y, 1 - c), device_id_type=MESH).start()
        token[...] = jnp.zeros_like(token)

    land_shapes = [(4,) + g.shape[1:] for g in grads]
    outs = pl.pallas_call(
        body, name=name,
        out_shape=tuple([pltpu.SemaphoreType.DMA((4 * n,))] * 2 + [pltpu.HBM(g.shape, g.dtype) for g in grads]
                        + [pltpu.HBM(ls, g.dtype) for ls, g in zip(land_shapes, grads)]
                        + [jax.ShapeDtypeStruct((SUBLANES, LANES), F32)]),
        in_specs=tuple([HBM_SPEC] * (2 * n) + [ANY_SPEC]),
        out_specs=tuple([SEM_SPEC] * 2 + [HBM_SPEC] * (2 * n) + [pl.BlockSpec(memory_space=pltpu.VMEM)]),
        input_output_aliases={i: 2 + i for i in range(2 * n)},
        compiler_params=pltpu.CompilerParams(has_side_effects=_DATAFLOW, collective_id=SIBLING_BARRIER_ID),
    )(*[_hbm(g) for g in grads], *[_hbm(lax.empty(ls, g.dtype)) for ls, g in zip(land_shapes, grads)], after)
    return outs[0], outs[1], list(outs[2:2 + n]), list(outs[2 + n:2 + 2 * n]), outs[-1]


def _pair_exchange_finish(grads, lands, send_sems, recv_sems, after, name):
    n = len(grads)

    def body(*refs):
        ins, land_refs = refs[:n], refs[n:2 * n]
        send, recv = refs[2 * n], refs[2 * n + 1]
        x, y, c = _mesh_pos()
        for t in range(n):
            for q in range(4):
                cp = pltpu.make_async_remote_copy(
                    src_ref=ins[t].at[q], dst_ref=land_refs[t].at[q], send_sem=send.at[4 * t + q],
                    recv_sem=recv.at[4 * t + q], device_id=(x, y, 1 - c), device_id_type=MESH)
                cp.wait_send()
                cp.wait_recv()

    outs = pl.pallas_call(
        body, name=name,
        out_shape=tuple([pltpu.HBM(g.shape, g.dtype) for g in grads] + [pltpu.HBM(g.shape, g.dtype) for g in lands]),
        in_specs=tuple([HBM_SPEC] * (2 * n) + [SEM_SPEC] * 2 + [ANY_SPEC]),
        out_specs=tuple([HBM_SPEC] * (2 * n)),
        input_output_aliases={i: i for i in range(2 * n)},
        compiler_params=pltpu.CompilerParams(has_side_effects=_DATAFLOW),
    )(*grads, *lands, send_sems, recv_sems, after)
    return list(outs[:n]), list(outs[n:])


def _chip_exchange_start(sums, after, name, barrier_id):
    n = len(sums)

    def body(*refs):
        ins, lands = refs[:n], refs[n:2 * n]
        send, recv = refs[2 * n + 1], refs[2 * n + 2]
        token = refs[-1]
        x, y, c = _mesh_pos()
        chips = [(1 - x, y), (x, 1 - y), (1 - x, 1 - y)]
        _peer_handshake([(*chip, c) for chip in chips])
        for t in range(n):
            for r, chip in enumerate(chips):
                pltpu.make_async_remote_copy(
                    src_ref=ins[t].at[2 * chip[0] + chip[1]], dst_ref=lands[t].at[r],
                    send_sem=send.at[3 * t + r], recv_sem=recv.at[3 * t + r],
                    device_id=(*chip, c), device_id_type=MESH).start()
        token[...] = jnp.zeros_like(token)

    land_shapes = [(3,) + s.shape[1:] for s in sums]
    outs = pl.pallas_call(
        body, name=name,
        out_shape=tuple([pltpu.SemaphoreType.DMA((3 * n,))] * 2 + [pltpu.HBM(s.shape, s.dtype) for s in sums]
                        + [pltpu.HBM(ls, s.dtype) for ls, s in zip(land_shapes, sums)]
                        + [jax.ShapeDtypeStruct((SUBLANES, LANES), F32)]),
        in_specs=tuple([HBM_SPEC] * (2 * n) + [ANY_SPEC]),
        out_specs=tuple([SEM_SPEC] * 2 + [HBM_SPEC] * (2 * n) + [pl.BlockSpec(memory_space=pltpu.VMEM)]),
        input_output_aliases={i: 2 + i for i in range(2 * n)},
        compiler_params=pltpu.CompilerParams(has_side_effects=_DATAFLOW, collective_id=barrier_id),
    )(*[_hbm(s) for s in sums], *[_hbm(lax.empty(ls, s.dtype)) for ls, s in zip(land_shapes, sums)], after)
    return outs[0], outs[1], list(outs[2:2 + n]), list(outs[2 + n:2 + 2 * n]), outs[-1]


def _chip_exchange_finish(sums, lands, send_sems, recv_sems, after, name):
    n = len(sums)

    def body(*refs):
        ins, land_refs = refs[:n], refs[n:2 * n]
        send, recv = refs[2 * n], refs[2 * n + 1]
        x, y, c = _mesh_pos()
        for t in range(n):
            for r in range(3):
                cp = pltpu.make_async_remote_copy(
                    src_ref=ins[t].at[r], dst_ref=land_refs[t].at[r], send_sem=send.at[3 * t + r],
                    recv_sem=recv.at[3 * t + r],
                    device_id=(x, y, 1 - c), device_id_type=MESH)
                cp.wait_send()
                cp.wait_recv()

    outs = pl.pallas_call(
        body, name=name,
        out_shape=tuple(pltpu.HBM(g.shape, g.dtype) for g in lands),
        in_specs=tuple([HBM_SPEC] * (2 * n) + [SEM_SPEC] * 2 + [ANY_SPEC]),
        out_specs=tuple([HBM_SPEC] * n),
        input_output_aliases={n + i: i for i in range(n)},
        compiler_params=pltpu.CompilerParams(has_side_effects=_DATAFLOW),
    )(*sums, *lands, send_sems, recv_sems, after)
    return list(outs)


def _mm(a, b, *, mode, tm, tn, tk=None, b_blocked=False, out_blocked=False, out_dtypes=(F32,),
        epilogue=None, extras=(), after=None, kb=1, out_places=None, row_splits=1, out_block=None, name):
    if mode == "nn":
        m, k = a.shape
        n = b.shape[0] * b.shape[2] if b_blocked else b.shape[1]
        dims = (((1,), (0,)), ((), ()))
    elif mode == "nt":
        m, k = a.shape
        n = b.shape[1] if b_blocked else b.shape[0]
        if b_blocked:
            tk = kb * b.shape[2]
        dims = (((1,), (1,)), ((), ()))
    else:
        k, m = a.shape
        n = b.shape[1]
        dims = (((0,), (0,)), ((), ()))
    tk = k if tk is None else tk
    assert m % tm == 0 and n % tn == 0 and k % tk == 0, (name, m, n, k, tm, tn, tk)
    gm, gn, gk = m // tm, n // tn, k // tk
    if b_blocked:
        assert (tn if mode == "nn" else tk) == kb * b.shape[2], name
    if row_splits > 1:
        assert gk == 1 and epilogue is not None and mode != "tn" and not b_blocked and tm % (16 * row_splits) == 0, name

    if mode == "nn":
        a_spec = pl.BlockSpec((tm, tk), lambda i, j, kk: (i, kk))
        b_spec = (pl.BlockSpec((None, tk, tn), lambda i, j, kk: (j, kk, 0)) if b_blocked
                  else pl.BlockSpec((tk, tn), lambda i, j, kk: (kk, j)))
    elif mode == "nt":
        a_spec = pl.BlockSpec((tm, tk), lambda i, j, kk: (i, kk))
        b_spec = (pl.BlockSpec((kb, tn, tk // kb), lambda i, j, kk: (kk, j, 0)) if b_blocked
                  else pl.BlockSpec((tn, tk), lambda i, j, kk: (j, kk)))
    else:
        a_spec = pl.BlockSpec((tk, tm), lambda i, j, kk: (kk, i))
        b_spec = pl.BlockSpec((tk, tn), lambda i, j, kk: (kk, j))
    out_pack = 1
    if out_blocked and out_block is not None and out_block != tn:
        assert tn % out_block == 0, name
        out_pack = tn // out_block
        out_spec = pl.BlockSpec((out_pack, tm, out_block), lambda i, j, kk: (j, i, 0))
        out_shape = (n // out_block, m, out_block)
    elif out_blocked:
        out_spec = pl.BlockSpec((None, tm, tn), lambda i, j, kk: (j, i, 0))
        out_shape = (gn, m, tn)
    else:
        out_spec = pl.BlockSpec((tm, tn), lambda i, j, kk: (i, j))
        out_shape = (m, n)
    extra_specs = [pl.BlockSpec((tm, tn), functools.partial(lambda i, j, kk, off: (i, j + off), off=off))
                   for _, off in extras]
    n_extra, n_out = len(extras), len(out_dtypes)
    n_after = 0 if after is None else 1
    places = out_places if out_places is not None else (None,) * n_out

    def body(a_ref, b_ref, *rest):
        extra_refs = rest[:n_extra]
        out_refs = rest[n_extra + n_after:n_extra + n_after + n_out]

        def finish(acc):
            if epilogue is None:
                res = (acc,)
            else:
                res = epilogue(acc, *[e[...] for e in extra_refs])
            for o_ref, r in zip(out_refs, res):
                if out_pack == 1:
                    o_ref[...] = r.astype(o_ref.dtype)
                else:
                    for h in range(out_pack):
                        o_ref[h] = r[:, h * out_block:(h + 1) * out_block].astype(o_ref.dtype)

        if row_splits > 1:
            strip = tm // row_splits
            for h in range(row_splits):
                rows = slice(h * strip, (h + 1) * strip)
                acc = lax.dot_general(a_ref[rows, :], b_ref[...], dims, preferred_element_type=F32)
                res = epilogue(acc, *[e[rows, :] for e in extra_refs])
                for o_ref, r in zip(out_refs, res):
                    o_ref[rows, :] = r.astype(o_ref.dtype)
            return
        if mode == "nt" and b_blocked:
            bk = tk // kb
            part = lax.dot_general(a_ref[:, :bk], b_ref[0], dims, preferred_element_type=F32)
            for h in range(1, kb):
                part = part + lax.dot_general(a_ref[:, h * bk:(h + 1) * bk], b_ref[h], dims,
                                              preferred_element_type=F32)
        else:
            part = lax.dot_general(a_ref[...], b_ref[...], dims, preferred_element_type=F32)
        if gk == 1:
            finish(part)
        else:
            acc_ref = rest[-1]
            kk = pl.program_id(2)

            @pl.when(kk == 0)
            def _():
                acc_ref[...] = part

            @pl.when(kk > 0)
            def _():
                acc_ref[...] += part

            @pl.when(kk == gk - 1)
            def _():
                finish(acc_ref[...])

    outs = pl.pallas_call(
        body, name=name, grid=(gm, gn, gk),
        in_specs=[a_spec, b_spec] + extra_specs + [ANY_SPEC] * n_after,
        out_specs=[out_spec if place is None else
                   pl.BlockSpec((tm, tn), functools.partial(lambda i, j, kk, off: (i, j + off), off=place[1] // tn))
                   for place in places],
        out_shape=[jax.ShapeDtypeStruct(out_shape if place is None else (m, place[0]), dt)
                   for dt, place in zip(out_dtypes, places)],
        scratch_shapes=[pltpu.VMEM((tm, tn), F32)] if gk > 1 else [],
        compiler_params=_cparams("parallel", "parallel", "arbitrary"),
    )(a, b, *[e for e, _ in extras], *([] if after is None else [after]))
    return outs[0] if n_out == 1 else outs


def _mm_slots(a, w, slots, *, over, tm, tn=None, out_dtype=F32, epilogue=None, base=None, name):
    m = a.shape[0]
    ns = slots.shape[0]
    n_slots, w1, w2 = w.shape
    assert m % tm == 0
    if over == "n":
        k, bn = w1, w2

        def body(slots_ref, a_ref, w_ref, *rest):
            out_ref = rest[-1]
            acc = jnp.dot(a_ref[...], w_ref[...], preferred_element_type=F32)
            out_ref[...] = (acc if epilogue is None else epilogue(acc)).astype(out_ref.dtype)

        in_specs = [pl.BlockSpec((tm, k), lambda i, j, s: (i, 0)),
                    pl.BlockSpec((None, k, bn), lambda i, j, s: (s[j], 0, 0))]
        args = [a, w]
        aliases = {}
        if base is not None:
            in_specs.append(ANY_SPEC)
            args.append(base)
            aliases = {3: 0}
        return pl.pallas_call(
            body, name=name,
            grid_spec=pltpu.PrefetchScalarGridSpec(
                num_scalar_prefetch=1, grid=(m // tm, ns), in_specs=in_specs,
                out_specs=pl.BlockSpec((tm, bn), lambda i, j, s: (i, s[j]))),
            out_shape=jax.ShapeDtypeStruct((m, n_slots * bn), out_dtype),
            input_output_aliases=aliases,
            compiler_params=_cparams("parallel", "arbitrary"),
        )(slots, *args)

    bk, n = w1, w2
    tn = n if tn is None else tn
    assert n % tn == 0

    def body(slots_ref, a_ref, w_ref, *rest):
        out_ref, acc_ref = rest[-2], rest[-1]
        kk = pl.program_id(2)
        part = jnp.dot(a_ref[...], w_ref[...], preferred_element_type=F32)

        @pl.when(kk == 0)
        def _():
            acc_ref[...] = part if base is None else part + rest[0][...]

        @pl.when(kk > 0)
        def _():
            acc_ref[...] += part

        @pl.when(kk == ns - 1)
        def _():
            out_ref[...] = acc_ref[...].astype(out_ref.dtype)

    in_specs = [pl.BlockSpec((tm, bk), lambda i, j, kk, s: (i, s[kk])),
                pl.BlockSpec((None, bk, tn), lambda i, j, kk, s: (s[kk], 0, j))]
    args = [a, w]
    if base is not None:
        in_specs.append(pl.BlockSpec((tm, tn), lambda i, j, kk, s: (i, j)))
        args.append(base)
    return pl.pallas_call(
        body, name=name,
        grid_spec=pltpu.PrefetchScalarGridSpec(
            num_scalar_prefetch=1, grid=(m // tm, n // tn, ns), in_specs=in_specs,
            out_specs=pl.BlockSpec((tm, tn), lambda i, j, kk, s: (i, j)),
            scratch_shapes=[pltpu.VMEM((tm, tn), F32)]),
        out_shape=jax.ShapeDtypeStruct((m, n), out_dtype),
        compiler_params=_cparams("parallel", "parallel", "arbitrary"),
    )(slots, *args)


def _gate_mix(ya_pre, s, wpo, wco, proj, d_model, name):
    lp, width = ya_pre.shape
    nb, _, bn = wpo.shape
    ga_off = (proj.shape[1] - 2 * d_model) // bn
    gb_off = (proj.shape[1] - d_model) // bn

    n_strips = 4 if lp % 64 == 0 else 1

    def body(ya_ref, s_ref, wpo_ref, wco_ref, ga_ref, gb_ref, m_ref, y_a_ref, y_b_ref):
        strip = lp // n_strips
        for h in range(n_strips):
            rows = slice(h * strip, (h + 1) * strip)
            y_a = jnp.dot(ya_ref[rows, :], wpo_ref[...], preferred_element_type=F32)
            y_b = jnp.dot(s_ref[rows, :], wco_ref[...], preferred_element_type=F32)
            m = jax.nn.sigmoid(ga_ref[rows, :]) * y_a + jax.nn.sigmoid(gb_ref[rows, :]) * y_b
            m_ref[rows, :] = m.astype(BF16)
            y_a_ref[rows, :] = y_a.astype(BF16)
            y_b_ref[rows, :] = y_b.astype(BF16)

    act_spec = pl.BlockSpec((lp, width), lambda j: (0, 0))
    w_spec = pl.BlockSpec((None, width, bn), lambda j: (j, 0, 0))
    out_spec = pl.BlockSpec((lp, bn), lambda j: (0, j))
    return pl.pallas_call(
        body, name=name, grid=(nb,),
        in_specs=[act_spec, act_spec, w_spec, w_spec,
                  pl.BlockSpec((lp, bn), lambda j: (0, j + ga_off)),
                  pl.BlockSpec((lp, bn), lambda j: (0, j + gb_off))],
        out_specs=[out_spec] * 3,
        out_shape=[jax.ShapeDtypeStruct((lp, nb * bn), BF16)] * 3,
        compiler_params=_cparams("parallel"),
    )(ya_pre, s, wpo, wco, proj, proj)


def _rms_stats(x):
    return lax.rsqrt(jnp.mean(x * x, axis=-1, keepdims=True) + RMS_EPS)


def _rms_bwd(x, g, dy):
    r = _rms_stats(x)
    nrm = x * r
    dn = dy * g
    dx = r * (dn - nrm * jnp.mean(dn * nrm, axis=-1, keepdims=True))
    return dx, dy * nrm


def _rowwise(body, ins, outs, accs, *, lp, name):
    tr = _row_tile(lp, max(a.shape[1] for a in ins))
    n_in, n_out, n_acc = len(ins), len(outs), len(accs)

    def kernel_body(*refs):
        i = pl.program_id(0)
        acc_refs = refs[n_in + n_out:]

        @pl.when(i == 0)
        def _():
            for r in acc_refs:
                r[...] = jnp.zeros_like(r)

        body(i * tr, refs[:n_in], refs[n_in:n_in + n_out], acc_refs)

    in_specs = []
    for a in ins:
        if a.shape[0] == lp:
            in_specs.append(pl.BlockSpec((tr, a.shape[1]), lambda i: (i, 0)))
        else:
            in_specs.append(pl.BlockSpec(a.shape, lambda i: (0, 0)))
    out_specs = [pl.BlockSpec((tr, w), lambda i: (i, 0)) for w, _ in outs]
    out_specs += [pl.BlockSpec((SUBLANES, w), lambda i: (0, 0)) for w in accs]
    out_shape = [jax.ShapeDtypeStruct((lp, w), d) for w, d in outs]
    out_shape += [jax.ShapeDtypeStruct((SUBLANES, w), F32) for w in accs]
    return pl.pallas_call(
        kernel_body, name=name, grid=(lp // tr,), in_specs=in_specs, out_specs=out_specs,
        out_shape=out_shape, compiler_params=_cparams("arbitrary"),
    )(*ins)


SHIFT_TILE = 128


def _shifted_specs(width, n_big, n_small):
    per = SHIFT_TILE // N_META
    small = pl.BlockSpec((N_META, width), lambda i: (jnp.clip(per * i - 1, 0, n_small - 1), 0))
    big = pl.BlockSpec((SHIFT_TILE, width), lambda i: (jnp.minimum(i, n_big - 1), 0))
    return small, big


def _rms_pre(meta_full, x2, g, lp, seq):
    d = x2.shape[1]
    assert seq % SHIFT_TILE == 0 and lp % SHIFT_TILE == 0 and SHIFT_TILE % N_META == 0

    def body(meta_ref, xs_ref, xb_ref, g_ref, h0_ref, u1_ref):
        i = pl.program_id(0)
        head = jnp.where(i == 0, meta_ref[...], xs_ref[...])
        rows = jnp.concatenate([head, xb_ref[:SHIFT_TILE - N_META, :]], axis=0)
        r = i * SHIFT_TILE + lax.broadcasted_iota(jnp.int32, (SHIFT_TILE, 1), 0)
        rows = jnp.where(r < N_META + seq, rows, 0.0)
        h0_ref[...] = rows
        u1_ref[...] = (rows * _rms_stats(rows) * g_ref[...]).astype(BF16)

    small, big = _shifted_specs(d, seq // SHIFT_TILE, seq // N_META)
    tile = pl.BlockSpec((SHIFT_TILE, d), lambda i: (i, 0))
    return pl.pallas_call(
        body, name="rms_pre", grid=(lp // SHIFT_TILE,),
        in_specs=[pl.BlockSpec((N_META, d), lambda i: (0, 0)), small, big, pl.BlockSpec((1, d), lambda i: (0, 0))],
        out_specs=[tile, tile],
        out_shape=[jax.ShapeDtypeStruct((lp, d), F32), jax.ShapeDtypeStruct((lp, d), BF16)],
        compiler_params=_cparams("parallel"),
    )(meta_full, x2, x2, g)


def _post_mix(o, h0, g_post_mix, g_pre_mlp, lp):
    d = h0.shape[1]

    def body(row0, ins, outs, accs):
        o_ref, h0_ref, g1_ref, g2_ref = ins
        o_v = o_ref[...]
        h1 = h0_ref[...] + o_v * _rms_stats(o_v) * g1_ref[...]
        outs[0][...] = h1
        outs[1][...] = (h1 * _rms_stats(h1) * g2_ref[...]).astype(BF16)

    return _rowwise(body, [o, h0, g_post_mix, g_pre_mlp], [(d, F32), (d, BF16)], [], lp=lp, name="post_mix")


def _loss_head(f, h1, target, g_post_mlp, lp, seq):
    d = f.shape[1]

    def body(f_ref, h1_ref, ts_ref, tb_ref, g_ref, df_ref, dh_ref, dg_ref, loss_ref):
        i = pl.program_id(0)

        @pl.when(i == 0)
        def _():
            dg_ref[...] = jnp.zeros_like(dg_ref)
            loss_ref[...] = jnp.zeros_like(loss_ref)

        f_v, g = f_ref[...], g_ref[...]
        r = _rms_stats(f_v)
        nrm = f_v * r
        rows = i * SHIFT_TILE + lax.broadcasted_iota(jnp.int32, (SHIFT_TILE, 1), 0)
        valid = (rows >= N_META) & (rows < N_META + seq)
        tgt = jnp.concatenate([ts_ref[...], tb_ref[:SHIFT_TILE - N_META, :]], axis=0)
        err = jnp.where(valid, h1_ref[...] + nrm * g - tgt, 0.0)
        loss_ref[...] += 0.5 * jnp.sum(jnp.mean(err * err, axis=-1, keepdims=True))
        dy = err * (1.0 / d)
        dn = dy * g
        df_ref[...] = (r * (dn - nrm * jnp.mean(dn * nrm, axis=-1, keepdims=True))).astype(BF16)
        dh_ref[...] = dy
        dg_ref[...] += _rowsum8(dy * nrm)

    small, big = _shifted_specs(d, seq // SHIFT_TILE, seq // N_META)
    tile = pl.BlockSpec((SHIFT_TILE, d), lambda i: (i, 0))
    return pl.pallas_call(
        body, name="loss_head", grid=(lp // SHIFT_TILE,),
        in_specs=[tile, tile, small, big, pl.BlockSpec((1, d), lambda i: (0, 0))],
        out_specs=[tile, tile, pl.BlockSpec((SUBLANES, d), lambda i: (0, 0)),
                   pl.BlockSpec((SUBLANES, LANES), lambda i: (0, 0))],
        out_shape=[jax.ShapeDtypeStruct((lp, d), BF16), jax.ShapeDtypeStruct((lp, d), F32),
                   jax.ShapeDtypeStruct((SUBLANES, d), F32), jax.ShapeDtypeStruct((SUBLANES, LANES), F32)],
        compiler_params=_cparams("arbitrary"),
    )(f, h1, target, target, g_post_mlp)


def _mid_bwd(du2, h1, dh, o, g_pre_mlp, g_post_mix, lp):
    d = h1.shape[1]

    def body(row0, ins, outs, accs):
        du2_ref, h1_ref, dh_ref, o_ref, g2_ref, g1_ref = ins
        dx2, dg2 = _rms_bwd(h1_ref[...], g2_ref[...], du2_ref[...])
        dh1 = dh_ref[...] + dx2
        do, dg1 = _rms_bwd(o_ref[...], g1_ref[...], dh1)
        outs[0][...] = dh1
        outs[1][...] = do.astype(BF16)
        accs[0][...] += _rowsum8(dg2)
        accs[1][...] += _rowsum8(dg1)

    return _rowwise(body, [du2, h1, dh, o, g_pre_mlp, g_post_mix], [(d, F32), (d, BF16)], [d, d], lp=lp,
                    name="mid_bwd")


def _pre_mix_bwd(du1, h0, dh1, g_pre_mix, seq):
    d = h0.shape[1]
    per = SHIFT_TILE // N_META
    assert seq % SHIFT_TILE == 0

    def body(du_b, h_b, dh_b, du_n, h_n, dh_n, du_m, h_m, dh_m, g_ref, gx_ref, gm_ref, dg_ref):
        i = pl.program_id(0)
        g = g_ref[...]

        @pl.when(i == 0)
        def _():
            dx, dg = _rms_bwd(h_m[...], g, du_m[...])
            gm_ref[...] = dh_m[...] + dx
            dg_ref[...] = _rowsum8(dg)

        rows = lambda big, nxt: jnp.concatenate([big[N_META:, :], nxt[...]], axis=0)
        dx, dg = _rms_bwd(rows(h_b, h_n), g, rows(du_b, du_n))
        gx_ref[...] = rows(dh_b, dh_n) + dx
        dg_ref[...] += _rowsum8(dg)

    big = pl.BlockSpec((SHIFT_TILE, d), lambda i: (i, 0))
    nxt = pl.BlockSpec((N_META, d), lambda i: (per * (i + 1), 0))
    first = pl.BlockSpec((N_META, d), lambda i: (0, 0))
    return pl.pallas_call(
        body, name="pre_mix_bwd", grid=(seq // SHIFT_TILE,),
        in_specs=[big] * 3 + [nxt] * 3 + [first] * 3 + [pl.BlockSpec((1, d), lambda i: (0, 0))],
        out_specs=[big, first, pl.BlockSpec((SUBLANES, d), lambda i: (0, 0))],
        out_shape=[jax.ShapeDtypeStruct((seq, d), F32), jax.ShapeDtypeStruct((N_META, d), F32),
                   jax.ShapeDtypeStruct((SUBLANES, d), F32)],
        compiler_params=_cparams("arbitrary"),
    )(du1, h0, dh1, du1, h0, dh1, du1, h0, dh1, g_pre_mix)


def _ln_stats(c):
    mu = jnp.mean(c, axis=-1, keepdims=True)
    var = jnp.mean(jnp.square(c - mu), axis=-1, keepdims=True)
    return mu, lax.rsqrt(var + LN_EPS)


def _ln_silu(c, ln_g, ln_b, lp):
    w = c.shape[1]

    def body(row0, ins, outs, accs):
        c_ref, g_ref, b_ref = ins
        c_v = c_ref[...]
        mu, rstd = _ln_stats(c_v)
        ln = (c_v - mu) * rstd * g_ref[...] + b_ref[...]
        outs[0][...] = (ln * jax.nn.sigmoid(ln)).astype(BF16)

    return _rowwise(body, [c, ln_g, ln_b], [(w, BF16)], [], lp=lp, name="ln_silu")[0]


def _ln_silu_bwd(c, ds, ln_g, ln_b, lp):
    w = c.shape[1]

    def body(row0, ins, outs, accs):
        c_ref, ds_ref, g_ref, b_ref = ins
        c_v, g = c_ref[...], g_ref[...]
        mu, rstd = _ln_stats(c_v)
        nrm = (c_v - mu) * rstd
        ln = nrm * g + b_ref[...]
        sig = jax.nn.sigmoid(ln)
        dln = ds_ref[...] * (sig * (1.0 + ln * (1.0 - sig)))
        dn = dln * g
        dc = rstd * (dn - jnp.mean(dn, axis=-1, keepdims=True) - nrm * jnp.mean(dn * nrm, axis=-1, keepdims=True))
        outs[0][...] = dc
        accs[0][...] += _rowsum8(dln * nrm)
        accs[1][...] += _rowsum8(dln)
        accs[2][...] += _rowsum8(dc)

    return _rowwise(body, [c, ds, ln_g, ln_b], [(w, F32)], [w, w, w], lp=lp, name="ln_silu_bwd")


def _chunk_with_history(ref, i, cols=slice(None)):
    t0 = pl.multiple_of(i * ROW_CHUNK, ROW_CHUNK)
    lo0 = pl.multiple_of(jnp.maximum(t0 - HALO, 0), SUBLANES)
    lo = jnp.where(i > 0, ref[pl.ds(lo0, HALO), cols], 0.0)
    return jnp.concatenate([lo, ref[pl.ds(t0, ROW_CHUNK), cols]], axis=0)


def _chunk_with_future(ref, i, n_chunks, cols=slice(None)):
    t0 = pl.multiple_of(i * ROW_CHUNK, ROW_CHUNK)
    hi0 = pl.multiple_of(jnp.minimum(t0 + ROW_CHUNK, (n_chunks - 1) * ROW_CHUNK), SUBLANES)
    hi = jnp.where(i < n_chunks - 1, ref[pl.ds(hi0, HALO), cols], 0.0)
    return jnp.concatenate([ref[pl.ds(t0, ROW_CHUNK), cols], hi], axis=0)


def _inv_count(t0, n_rows, window):
    pos = t0 + lax.broadcasted_iota(jnp.int32, (n_rows, 1), 0)
    return 1.0 / jnp.minimum(pos + 1, window).astype(F32)


def _pool_delta(z_hist, t0, window):
    s = z_hist
    sh = 1
    while sh < window:
        s = s + pltpu.roll(s, sh, 0)
        sh *= 2
    cur = z_hist[HALO:, :]
    return s[HALO:, :] * _inv_count(t0, ROW_CHUNK, window) - cur


def _pool_fwd(proj, wpg, pool_scale, lp):
    n_grp, gdim, _ = wpg.shape
    width = n_grp * gdim
    n_chunks = lp // ROW_CHUNK

    def body(z_ref, w_ref, sc_ref, out_ref):
        for g, window in enumerate(POOL_WINDOWS):
            cols = slice(g * gdim, (g + 1) * gdim)

            def chunk(i, carry, cols=cols, g=g, window=window):
                t0 = pl.multiple_of(i * ROW_CHUNK, ROW_CHUNK)
                d = _pool_delta(_chunk_with_history(z_ref, i, cols), t0, window)
                q = jnp.dot(d.astype(BF16), w_ref[g], preferred_element_type=F32)
                out_ref[pl.ds(t0, ROW_CHUNK), cols] = (q * sc_ref[:, cols]).astype(BF16)
                return carry

            lax.fori_loop(0, n_chunks, chunk, 0)

    return pl.pallas_call(
        body, name="pool_fwd", grid=(1,),
        in_specs=[pl.BlockSpec((lp, width), lambda i: (0, 0)),
                  pl.BlockSpec(wpg.shape, lambda i: (0, 0, 0)),
                  pl.BlockSpec(pool_scale.shape, lambda i: (0, 0))],
        out_specs=pl.BlockSpec((lp, width), lambda i: (0, 0)),
        out_shape=jax.ShapeDtypeStruct((lp, width), BF16),
        compiler_params=_cparams("arbitrary"),
    )(proj, wpg, pool_scale)


def _pool_bwd(proj, d_ya, wpg, pool_scale, dproj, lp):
    n_grp, gdim, _ = wpg.shape
    width = n_grp * gdim
    n_chunks = lp // ROW_CHUNK
    ext = ROW_CHUNK + HALO

    def body(z_ref, dya_ref, w_ref, sc_ref, _, dz_ref, dw_ref, dsc_ref):
        dw_ref[...] = jnp.zeros_like(dw_ref)
        dsc_ref[...] = jnp.zeros_like(dsc_ref)
        for g, window in enumerate(POOL_WINDOWS):
            cols = slice(g * gdim, (g + 1) * gdim)

            def chunk(i, carry, cols=cols, g=g, window=window):
                t0 = pl.multiple_of(i * ROW_CHUNK, ROW_CHUNK)
                w_g = w_ref[g]
                scale = sc_ref[:, cols]
                d = _pool_delta(_chunk_with_history(z_ref, i, cols), t0, window).astype(BF16)
                dya_ext = _chunk_with_future(dya_ref, i, n_chunks, cols)
                dya = dya_ext[:ROW_CHUNK, :]
                q = jnp.dot(d, w_g, preferred_element_type=F32)
                dsc_ref[:, cols] += _rowsum8(dya * q)
                e_ext = (dya_ext * scale).astype(BF16)
                dw_ref[g] += lax.dot_general(d, e_ext[:ROW_CHUNK, :], (((0,), (0,)), ((), ())),
                                             preferred_element_type=F32)
                dd_ext = lax.dot_general(e_ext, w_g, (((1,), (1,)), ((), ())), preferred_element_type=F32)
                s = dd_ext * _inv_count(t0, ext, window)
                sh = 1
                while sh < window:
                    s = s + pltpu.roll(s, ext - sh, 0)
                    sh *= 2
                dz_ref[pl.ds(t0, ROW_CHUNK), cols] = (s[:ROW_CHUNK, :] - dd_ext[:ROW_CHUNK, :]).astype(BF16)
                return carry

            lax.fori_loop(0, n_chunks, chunk, 0)

    blk = pl.BlockSpec((lp, width), lambda i: (0, 0))
    return pl.pallas_call(
        body, name="pool_bwd", grid=(1,),
        in_specs=[blk, blk, pl.BlockSpec(wpg.shape, lambda i: (0, 0, 0)),
                  pl.BlockSpec(pool_scale.shape, lambda i: (0, 0)), ANY_SPEC],
        out_specs=[blk, pl.BlockSpec(wpg.shape, lambda i: (0, 0, 0)),
                   pl.BlockSpec((SUBLANES, width), lambda i: (0, 0))],
        out_shape=[jax.ShapeDtypeStruct(dproj.shape, BF16), jax.ShapeDtypeStruct(wpg.shape, F32),
                   jax.ShapeDtypeStruct((SUBLANES, width), F32)],
        input_output_aliases={4: 0},
        compiler_params=_cparams("arbitrary"),
    )(proj, d_ya, wpg, pool_scale, dproj)


def _conv_fwd(proj, w_dw, b_dw, lp, width, v_col0):
    n_chunks = lp // ROW_CHUNK
    v_blk0, g_blk0 = v_col0 // LANES, (v_col0 + width) // LANES

    def body(v_ref, gc_ref, w_ref, b_ref, c_ref, a_pad):
        a_pad[pl.ds(0, HALO), :] = jnp.zeros((HALO, LANES), F32)
        a_pad[pl.ds(HALO, lp), :] = v_ref[...] * jax.nn.sigmoid(gc_ref[...])

        def chunk(i, carry):
            t0 = pl.multiple_of(i * ROW_CHUNK, ROW_CHUNK)
            hist = a_pad[pl.ds(t0, ROW_CHUNK + HALO), :]
            acc = jnp.zeros((ROW_CHUNK, LANES), F32)
            for k in range(CONV_KERNEL):
                acc = acc + w_ref[k:k + 1, :] * pltpu.roll(hist, CONV_KERNEL - 1 - k, 0)[HALO:, :]
            c_ref[pl.ds(t0, ROW_CHUNK), :] = acc + b_ref[...]
            return carry

        lax.fori_loop(0, n_chunks, chunk, 0)

    return pl.pallas_call(
        body, name="conv_fwd", grid=(width // LANES,),
        in_specs=[pl.BlockSpec((lp, LANES), lambda j: (0, j + v_blk0)),
                  pl.BlockSpec((lp, LANES), lambda j: (0, j + g_blk0)),
                  pl.BlockSpec((CONV_TAPS_PADDED, LANES), lambda j: (0, j)),
                  pl.BlockSpec((1, LANES), lambda j: (0, j))],
        out_specs=pl.BlockSpec((lp, LANES), lambda j: (0, j)),
        out_shape=jax.ShapeDtypeStruct((lp, width), F32),
        scratch_shapes=[pltpu.VMEM((lp + HALO, LANES), F32)],
        compiler_params=_cparams("parallel"),
    )(proj, proj, w_dw, b_dw)


def _conv_bwd(proj, dc, w_dw, dproj, lp, width, v_col0):
    n_chunks = lp // ROW_CHUNK
    ext = ROW_CHUNK + HALO
    v_blk0, g_blk0 = v_col0 // LANES, (v_col0 + width) // LANES

    def body(v_ref, gc_ref, dc_ref, w_ref, _, dv_ref, dgc_ref, dw_ref, a_pad, dc_pad, dw_acc):
        sig = jax.nn.sigmoid(gc_ref[...])
        a_pad[pl.ds(0, HALO), :] = jnp.zeros((HALO, LANES), F32)
        a_pad[pl.ds(HALO, lp), :] = v_ref[...] * sig
        dc_pad[pl.ds(0, lp), :] = dc_ref[...]
        dc_pad[pl.ds(lp, HALO), :] = jnp.zeros((HALO, LANES), F32)
        dw_acc[...] = jnp.zeros_like(dw_acc)

        def chunk(i, carry):
            t0 = pl.multiple_of(i * ROW_CHUNK, ROW_CHUNK)
            hist = a_pad[pl.ds(t0, ext), :]
            fut = dc_pad[pl.ds(t0, ext), :]
            dc_cur = fut[:ROW_CHUNK, :]
            da = jnp.zeros((ROW_CHUNK, LANES), F32)
            for k in range(CONV_KERNEL):
                lag = CONV_KERNEL - 1 - k
                da = da + w_ref[k:k + 1, :] * pltpu.roll(fut, (ext - lag) % ext, 0)[:ROW_CHUNK, :]
                dw_acc[pl.ds(SUBLANES * k, SUBLANES), :] += _rowsum8(dc_cur * pltpu.roll(hist, lag, 0)[HALO:, :])
            rows = pl.ds(t0, ROW_CHUNK)
            sg = jax.nn.sigmoid(gc_ref[rows, :])
            dv_ref[rows, :] = (da * sg).astype(BF16)
            dgc_ref[rows, :] = (da * v_ref[rows, :] * sg * (1.0 - sg)).astype(BF16)
            return carry

        lax.fori_loop(0, n_chunks, chunk, 0)
        dw_ref[...] = dw_acc[...].reshape(CONV_TAPS_PADDED, SUBLANES, LANES).sum(axis=1)

    col = lambda j: (0, j)
    return pl.pallas_call(
        body, name="conv_bwd", grid=(width // LANES,),
        in_specs=[pl.BlockSpec((lp, LANES), lambda j: (0, j + v_blk0)),
                  pl.BlockSpec((lp, LANES), lambda j: (0, j + g_blk0)),
                  pl.BlockSpec((lp, LANES), col),
                  pl.BlockSpec((CONV_TAPS_PADDED, LANES), col), ANY_SPEC],
        out_specs=[pl.BlockSpec((lp, LANES), lambda j: (0, j + v_blk0)), pl.BlockSpec((lp, LANES), col),
                   pl.BlockSpec((CONV_TAPS_PADDED, LANES), col)],
        out_shape=[jax.ShapeDtypeStruct(dproj.shape, BF16), jax.ShapeDtypeStruct((lp, width), BF16),
                   jax.ShapeDtypeStruct((CONV_TAPS_PADDED, width), F32)],
        scratch_shapes=[pltpu.VMEM((lp + HALO, LANES), F32), pltpu.VMEM((lp + HALO, LANES), F32),
                        pltpu.VMEM((CONV_TAPS_PADDED * SUBLANES, LANES), F32)],
        input_output_aliases={4: 0},
        compiler_params=_cparams("parallel"),
    )(proj, proj, dc, w_dw, dproj)


def _place_columns(dst, pieces, name):
    m = dst.shape[0]
    tile = 512
    counts = [p.shape[1] // tile for p, _ in pieces]
    starts = [sum(counts[:i]) for i in range(len(pieces))]
    n_steps = sum(counts)

    def local(s, i):
        return jnp.clip(s - starts[i], 0, counts[i] - 1)

    def out_index(s):
        blk = pieces[0][1] // tile + local(s, 0)
        for i in range(1, len(pieces)):
            blk = jnp.where(s >= starts[i], pieces[i][1] // tile + local(s, i), blk)
        return 0, blk

    def body(*refs):
        out_ref = refs[-1]
        s = pl.program_id(0)
        for i in range(len(pieces)):
            @pl.when((s >= starts[i]) & (s < starts[i] + counts[i]))
            def _(i=i):
                out_ref[...] = refs[i][...]

    return pl.pallas_call(
        body, name=name, grid=(n_steps,),
        in_specs=[pl.BlockSpec((m, tile), functools.partial(lambda s, i: (0, local(s, i)), i=i))
                  for i in range(len(pieces))] + [ANY_SPEC],
        out_specs=pl.BlockSpec((m, tile), out_index),
        out_shape=jax.ShapeDtypeStruct(dst.shape, dst.dtype),
        input_output_aliases={len(pieces): 0},
        compiler_params=_cparams("arbitrary"),
    )(*[p for p, _ in pieces], dst)


def _adamw_math(w, g, m, v):
    m = ADAM_B1 * m + (1.0 - ADAM_B1) * g
    v = ADAM_B2 * v + (1.0 - ADAM_B2) * jnp.square(g)
    m_hat = m / (1.0 - ADAM_B1 ** ADAM_STEP)
    v_hat = v / (1.0 - ADAM_B2 ** ADAM_STEP)
    delta = -ADAM_LR * (m_hat / (jnp.sqrt(v_hat) + ADAM_EPS) + ADAM_WD * w)
    return delta, m, v


def _pair_sum(own, recv, where, name):
    _, _, rows, cols = own.shape
    tr = _row_tile(rows, cols, 1024 * 1024)

    def body(where_ref, own_ref, recv_ref, out_ref):
        out_ref[...] = (own_ref[...].astype(F32) + recv_ref[...].astype(F32)).astype(BF16)

    return pl.pallas_call(
        body, name=name,
        grid_spec=pltpu.PrefetchScalarGridSpec(
            num_scalar_prefetch=1, grid=(3, rows // tr),
            in_specs=[pl.BlockSpec((None, None, tr, cols), lambda r, i, wh: (wh[2 + r], wh[0], i, 0)),
                      pl.BlockSpec((None, tr, cols), lambda r, i, wh: (wh[2 + r], i, 0))],
            out_specs=pl.BlockSpec((None, tr, cols), lambda r, i, wh: (wh[2 + r], i, 0))),
        out_shape=jax.ShapeDtypeStruct((4, rows, cols), BF16),
        compiler_params=_cparams("parallel", "parallel"),
    )(where, own, recv)


def _adamw_big(w, m, v, own, from_sibling, recv3, where, name):
    rows, cols = w.shape
    tr = _row_tile(rows, cols, 256 * 1024)

    def body(where_ref, w_ref, m_ref, v_ref, own_ref, sib_ref, r_ref, g_out, d_out, m_out, v_out):
        g = own_ref[...].astype(F32) + sib_ref[...].astype(F32)
        for r in range(3):
            g = g + r_ref[r].astype(F32)
        delta, m_new, v_new = _adamw_math(w_ref[...], g, m_ref[...], v_ref[...])
        g_out[...] = g
        d_out[...] = delta
        m_out[...] = m_new
        v_out[...] = v_new

    blk = pl.BlockSpec((tr, cols), lambda i, q_ref: (i, 0))
    return pl.pallas_call(
        body, name=name,
        grid_spec=pltpu.PrefetchScalarGridSpec(
            num_scalar_prefetch=1, grid=(rows // tr,),
            in_specs=[blk, blk, blk,
                      pl.BlockSpec((None, None, tr, cols), lambda i, wh: (wh[1], wh[0], i, 0)),
                      pl.BlockSpec((None, tr, cols), lambda i, wh: (wh[1], i, 0)),
                      pl.BlockSpec((3, tr, cols), lambda i, wh: (0, i, 0))],
            out_specs=[blk] * 4),
        out_shape=[jax.ShapeDtypeStruct((rows, cols), F32)] * 4,
        compiler_params=_cparams("parallel"),
    )(where, w, m, v, own, from_sibling, recv3)


def _small_update(me_idx, packed, rep_params, rep_places, meta_wmv, meta_row0, wdw_wmv, wdw_row0, loss_row0):
    n_rep = len(rep_params)
    meta_rows, meta_cols = meta_wmv[0].shape
    wdw_rows, wdw_cols = wdw_wmv[0].shape

    def body(me_ref, *refs):
        pos = 0

        def take(k):
            nonlocal pos
            out = refs[pos:pos + k]
            pos += k
            return out

        rep_in = [take(3) for _ in range(n_rep)]
        rep_g = take(n_rep)
        meta_in, (meta_g,) = take(3), take(1)
        wdw_in, (wdw_g,) = take(3), take(1)
        (loss_ref,) = take(1)
        rep_out = [take(4) for _ in range(n_rep)]
        meta_out, wdw_out = take(4), take(4)
        (loss_out,) = take(1)

        def update(wmv, g, outs):
            delta, m_new, v_new = _adamw_math(wmv[0][...], g, wmv[1][...], wmv[2][...])
            for o_ref, val in zip(outs, (g, delta, m_new, v_new)):
                o_ref[...] = val

        for wmv, g_ref, outs in zip(rep_in, rep_g, rep_out):
            g = jnp.sum(g_ref[0], axis=0, keepdims=True)
            for j in range(1, N_DEV):
                g = g + jnp.sum(g_ref[j], axis=0, keepdims=True)
            update(wmv, g, outs)
        for wmv, g_ref, outs in ((meta_in, meta_g, meta_out), (wdw_in, wdw_g, wdw_out)):
            g = g_ref[0]
            for j in range(1, N_DEV):
                g = g + g_ref[j]
            update(wmv, g, outs)
        total = loss_ref[0]
        for j in range(1, N_DEV):
            total = total + loss_ref[j]
        loss_out[...] = total

    def whole(a):
        nd = a.ndim
        return pl.BlockSpec(a.shape, lambda i, me_ref, nd=nd: (0,) * nd)

    ins, in_specs = [], []
    for wmv in rep_params:
        ins += list(wmv)
        in_specs += [whole(a) for a in wmv]
    for wmv, (row0, col0) in zip(rep_params, rep_places):
        width = wmv[0].shape[1]
        ins.append(packed)
        in_specs.append(pl.BlockSpec((N_DEV, SUBLANES, width),
                                     lambda i, me_ref, rb=row0 // SUBLANES, cb=col0 // width: (0, rb, cb)))
    ins += list(meta_wmv) + [packed]
    in_specs += [whole(a) for a in meta_wmv]
    in_specs.append(pl.BlockSpec((N_DEV, meta_rows, meta_cols),
                                 lambda i, me_ref, rb=meta_row0 // meta_rows: (0, rb, me_ref[0])))
    ins += list(wdw_wmv) + [packed]
    in_specs += [whole(a) for a in wdw_wmv]
    in_specs.append(pl.BlockSpec((N_DEV, wdw_rows, wdw_cols),
                                 lambda i, me_ref, rb=wdw_row0 // wdw_rows: (0, rb, me_ref[0])))
    ins.append(packed)
    in_specs.append(pl.BlockSpec((N_DEV, SUBLANES, LANES), lambda i, me_ref, rb=loss_row0 // SUBLANES: (0, rb, 0)))

    out_shape, out_specs = [], []
    for wmv in list(rep_params) + [meta_wmv, wdw_wmv]:
        out_shape += [jax.ShapeDtypeStruct(wmv[0].shape, F32)] * 4
        out_specs += [whole(wmv[0])] * 4
    out_shape.append(jax.ShapeDtypeStruct((SUBLANES, LANES), F32))
    out_specs.append(pl.BlockSpec((SUBLANES, LANES), lambda i, me_ref: (0, 0)))

    outs = pl.pallas_call(
        body, name="small_update",
        grid_spec=pltpu.PrefetchScalarGridSpec(num_scalar_prefetch=1, grid=(1,), in_specs=in_specs,
                                               out_specs=out_specs),
        out_shape=out_shape, compiler_params=_cparams("arbitrary"),
    )(me_idx, *ins)
    groups = [outs[4 * i:4 * i + 4] for i in range(n_rep + 2)]
    return groups[:n_rep], groups[n_rep], groups[n_rep + 1], outs[-1]


def kernel(x, meta, g_pre_mix, w_in, w_pool_grp, pool_scale, w_pool_out, w_dw, b_dw, conv_ln_g, conv_ln_b, w_conv_out, w_o, g_post_mix, g_pre_mlp, w_up, w_down, g_post_mlp, loss_target, m_meta, m_g_pre_mix, m_w_in, m_w_pool_grp, m_pool_scale, m_w_pool_out, m_w_dw, m_b_dw, m_conv_ln_g, m_conv_ln_b, m_w_conv_out, m_w_o, m_g_post_mix, m_g_pre_mlp, m_w_up, m_w_down, m_g_post_mlp, v_meta, v_g_pre_mix, v_w_in, v_w_pool_grp, v_pool_scale, v_w_pool_out, v_w_dw, v_b_dw, v_conv_ln_g, v_conv_ln_b, v_w_conv_out, v_w_o, v_g_post_mix, v_g_pre_mlp, v_w_up, v_w_down, v_g_post_mlp):
    seq, d = x.shape[1], x.shape[2]
    pool_w = pool_scale.shape[1]
    conv_w = b_dw.shape[1]
    n_grp, grp_rows, gdim = w_pool_grp.shape[1:]
    lp = _round_up(N_META + seq, ROW_CHUNK)
    tm_half = lp // 2 if (lp // 2) % 16 == 0 else lp
    c_idx = lax.axis_index("c").astype(jnp.int32)
    chip_idx = (2 * lax.axis_index("x") + lax.axis_index("y")).astype(jnp.int32)
    me_idx = 2 * chip_idx + c_idx

    pad_taps = ((0, CONV_TAPS_PADDED - CONV_KERNEL), (0, 0))
    big = dict(w_in=w_in[0], w_pool_grp=w_pool_grp[0].reshape(n_grp * grp_rows, gdim), w_pool_out=w_pool_out[0],
               w_conv_out=w_conv_out[0], w_o=w_o[0], w_up=w_up[0], w_down=w_down[0])
    big_names = list(big)
    moments = dict(w_in=(m_w_in, v_w_in), w_pool_grp=(m_w_pool_grp, v_w_pool_grp), w_pool_out=(m_w_pool_out, v_w_pool_out),
                   w_conv_out=(m_w_conv_out, v_w_conv_out), w_o=(m_w_o, v_w_o), w_up=(m_w_up, v_w_up),
                   w_down=(m_w_down, v_w_down))
    slot_idx = me_idx.reshape(1)
    sources = dict(big, meta=meta, w_dw=jnp.pad(w_dw[0], pad_taps))

    def fill(k, after):
        return _fill_slot(sources[k], slot_idx, BF16 if k in big else F32, after, "fill_" + k)

    gather_groups = [["meta", "w_dw"], ["w_in"], ["w_pool_grp", "w_pool_out", "w_conv_out", "w_o"], ["w_up"], ["w_down"]]
    started, token = _gather_start([[fill(k, slot_idx) for k in names] for names in gather_groups[:2]], slot_idx,
                                   "gather_start_first", BARRIER_IDS["gather_first"], issue_order=(0, 3, 1, 2))
    started_rest, _ = _gather_start([[fill(k, token) for k in names] for names in gather_groups[2:]], token,
                                    "gather_start_rest", BARRIER_IDS["gather_rest"])
    started += started_rest
    wg = {}
    x_idx, y_idx = lax.axis_index("x"), lax.axis_index("y")
    at = lambda px, py, pc: 4 * px + 2 * py + pc
    near_slots = jnp.stack([at(x_idx, y_idx, c_idx), at(x_idx, y_idx, 1 - c_idx), at(1 - x_idx, y_idx, c_idx),
                            at(x_idx, 1 - y_idx, c_idx), at(1 - x_idx, y_idx, 1 - c_idx),
                            at(x_idx, 1 - y_idx, 1 - c_idx)]).astype(jnp.int32)
    far_slots = jnp.stack([at(1 - x_idx, 1 - y_idx, c_idx), at(1 - x_idx, 1 - y_idx, 1 - c_idx)]).astype(jnp.int32)

    def gather_whole(gi, after_forward, after_finish):
        send, recv, lands = started[gi]
        fs, fr, lands = _gather_forward(lands, recv, after_forward(), f"gather_forward_{gi}")
        lands = _gather_finish(lands, send, recv, [(ALL_CHIPS, fs, fr)], [(ALL_CHIPS, fs, fr)], after_finish(),
                               f"gather_finish_{gi}")
        wg.update(zip(gather_groups[gi], lands))

    near_state = {}

    def gather_near(gi, after):
        send, recv, lands = started[gi]
        fs, fr, lands = _gather_forward(lands, recv, after, f"gather_forward_near_{gi}", which=NEAR)
        lands = _gather_finish(lands, send, recv, [(NEAR, fs, fr)], [], after, f"gather_finish_near_{gi}",
                               direct_sends=False)
        near_state[gi] = (fs, fr)
        return lands

    def gather_far(gi, lands, after):
        send, recv, _ = started[gi]
        fs, fr, lands = _gather_forward(lands, recv, after, f"gather_forward_far_{gi}", which=FAR)
        lands = _gather_finish(lands, send, recv, [(FAR, fs, fr)], [(NEAR,) + near_state[gi], (FAR, fs, fr)], after,
                               f"gather_finish_far_{gi}", own=False)
        wg.update(zip(gather_groups[gi], lands))
        return lands

    gather_whole(0, lambda: token, lambda: token)
    meta_full = wg["meta"].transpose(1, 0, 2).reshape(N_META, d)
    wdw_full = wg["w_dw"].transpose(1, 0, 2).reshape(CONV_TAPS_PADDED, conv_w)
    target = loss_target[0]
    h0, u1 = _rms_pre(meta_full, x[0], g_pre_mix, lp, seq)
    send, recv, w_in_lands = started[1]
    w_in_lands = _gather_finish(w_in_lands, send, recv, [], [], u1, "gather_finish_home_1", direct_sends=False)
    proj = _mm_slots(u1, w_in_lands[0], near_slots[:2], over="n", tm=tm_half, name="mm_proj_home")
    fs_far, fr_far, w_in_lands = _gather_forward(w_in_lands, recv, proj, "gather_forward_far_1", which=FAR)
    w_in_lands = _gather_finish(w_in_lands, send, recv, [(FAR, fs_far, fr_far)], [], proj, "gather_finish_far_1",
                                own=False, direct_sends=False)
    proj = _mm_slots(u1, w_in_lands[0], far_slots, over="n", tm=tm_half, base=proj, name="mm_proj_far")
    fs_near, fr_near, w_in_lands = _gather_forward(w_in_lands, recv, proj, "gather_forward_near_1", which=NEAR)
    w_in_lands = _gather_finish(w_in_lands, send, recv, [(NEAR, fs_near, fr_near)],
                                [(FAR, fs_far, fr_far), (NEAR, fs_near, fr_near)], proj, "gather_finish_near_1",
                                own=False)
    proj = _mm_slots(u1, w_in_lands[0], near_slots[2:], over="n", tm=tm_half, base=proj, name="mm_proj_near")
    wg["w_in"] = w_in_lands[0]
    conv_c = _conv_fwd(proj, wdw_full, b_dw, lp, conv_w, pool_w)
    s_act = _ln_silu(conv_c, conv_ln_g, conv_ln_b, lp)
    gather_whole(2, lambda: s_act, lambda: s_act)
    wpg_full = wg["w_pool_grp"].reshape(N_DEV, n_grp, grp_rows, gdim).transpose(1, 0, 2, 3).reshape(n_grp, gdim, gdim)
    w_o_full = wg["w_o"].reshape(d, d)
    ya_pre = _pool_fwd(proj, wpg_full, pool_scale, lp)
    m_mix, y_a, y_b = _gate_mix(ya_pre, s_act, wg["w_pool_out"], wg["w_conv_out"], proj, d, "gate_mix")
    o = _mm(m_mix, w_o_full, mode="nn", tm=tm_half, tn=512, name="mm_o")
    h1, u2 = _post_mix(o, h0, g_post_mix, g_pre_mlp, lp)
    relu2 = lambda acc: jnp.square(jnp.maximum(acc, 0.0))
    (w_up_near,) = gather_near(3, u2)
    act = _mm_slots(u2, w_up_near, near_slots, over="n", tm=tm_half, out_dtype=BF16, epilogue=relu2, name="mm_up_near")
    (w_up_all,) = gather_far(3, [w_up_near], act)
    act = _mm_slots(u2, w_up_all, far_slots, over="n", tm=tm_half, out_dtype=BF16, epilogue=relu2, base=act,
                    name="mm_up_far")
    (w_down_near,) = gather_near(4, act)
    f = _mm_slots(act, w_down_near, near_slots, over="k", tm=tm_half, tn=d, name="mm_down_near")
    (w_down_all,) = gather_far(4, [w_down_near], f)
    f = _mm_slots(act, w_down_all, far_slots, over="k", tm=tm_half // 2, tn=d, base=f, name="mm_down_far")
    w_down_full = w_down_all.reshape(-1, d)

    big_out = {}

    def to_sibling(names, grads, after, tag):
        send, recv, grads, lands, token = _pair_exchange_start(grads, after, "grads_to_sibling_start_" + tag)
        return (names, send, recv, grads, lands, tag), token

    where = jnp.stack([c_idx, chip_idx, 2 * (1 - x_idx) + y_idx, 2 * x_idx + (1 - y_idx),
                       2 * (1 - x_idx) + (1 - y_idx)]).astype(jnp.int32)

    def to_owner(handle, after):
        names, send, recv, grads, from_sib, tag = handle
        grads, from_sib = _pair_exchange_finish(grads, from_sib, send, recv, after, "grads_to_sibling_finish_" + tag)
        own = [g.reshape((4, 2) + g.shape[1:]) for g in grads]
        sums = [_pair_sum(o, r, where, "pair_sum_" + k) for k, o, r in zip(names, own, from_sib)]
        send, recv, sums, lands, token = _chip_exchange_start(sums, after, "grads_to_owner_start_" + tag,
                                                              BARRIER_IDS["owner_" + tag])
        return (names, send, recv, sums, lands, own, from_sib, tag), token

    def update(handle, after):
        names, send, recv, sums, lands, own, from_sib, tag = handle
        got = _chip_exchange_finish(sums, lands, send, recv, after, "grads_to_owner_finish_" + tag)
        for k, o, s, r3 in zip(names, own, from_sib, got):
            w2 = big[k]
            shape = moments[k][0].shape
            outs = _adamw_big(w2, moments[k][0].reshape(w2.shape), moments[k][1].reshape(w2.shape), o, s, r3,
                              where, "adamw_" + k)
            big_out[k] = [a.reshape(shape) for a in outs]
        return big_out[names[-1]][0]

    df, dh, dg_post_mlp, loss_part = _loss_head(f, h1, target, g_post_mlp, lp, seq)
    d_up = _mm(df, w_down_full, mode="nt", tm=tm_half, tn=1024, out_dtypes=(BF16,), extras=[(act, 0)],
               epilogue=lambda acc, a: (acc * (2.0 * jnp.sqrt(a.astype(F32))),), row_splits=2, name="mm_d_up")
    g_w_down = _mm(act, df, mode="tn", tm=1024, tn=1024, out_dtypes=(BF16,), name="mm_g_down")
    sib_down, token = to_sibling(["w_down"], [g_w_down.reshape(N_DEV, -1, d)], slot_idx, "down")
    g_w_up = _mm(u2, d_up, mode="tn", tm=1024, tn=wg["w_up"].shape[2], out_blocked=True, out_dtypes=(BF16,),
                 after=token, name="mm_g_up")
    sib_up, token = to_sibling(["w_up"], [g_w_up], slot_idx, "up")
    pending_down, token = to_owner(sib_down, token)
    du2 = _mm(d_up, wg["w_up"], mode="nt", tm=tm_half, tn=1024, b_blocked=True, kb=2, after=token, name="mm_du2")
    pending_up, token = to_owner(sib_up, du2)
    dh1, do, dg_pre_mlp, dg_post_mix = _mid_bwd(du2, h1, dh, o, g_pre_mlp, g_post_mix, lp)

    def gate_bwd(dm, ga, gb, ya, yb):
        sa, sb = jax.nn.sigmoid(ga), jax.nn.sigmoid(gb)
        return (dm * ya.astype(F32) * sa * (1.0 - sa), dm * yb.astype(F32) * sb * (1.0 - sb), dm * sa, dm * sb)

    gate_tn = 512
    ga_col0, gb_col0 = proj.shape[1] - 2 * d, proj.shape[1] - d
    dproj, d_gb, d_ya, d_yb = _mm(
        do, w_o_full, mode="nt", tm=tm_half, tn=gate_tn, out_dtypes=(BF16,) * 4,
        extras=[(proj, ga_col0 // gate_tn), (proj, gb_col0 // gate_tn), (y_a, 0), (y_b, 0)], epilogue=gate_bwd,
        out_places=((proj.shape[1], ga_col0), None, None, None), after=token, row_splits=2, name="mm_dm")
    g_w_o = _mm(m_mix, do, mode="tn", tm=1024, tn=1024, out_dtypes=(BF16,), name="mm_g_o")
    bn_out = wg["w_pool_out"].shape[2]
    g_w_pool_out = _mm(ya_pre, d_ya, mode="tn", tm=pool_w, tn=4 * bn_out, out_blocked=True, out_block=bn_out,
                       out_dtypes=(BF16,), name="mm_g_pool_out")
    g_w_conv_out = _mm(s_act, d_yb, mode="tn", tm=conv_w, tn=4 * bn_out, out_blocked=True, out_block=bn_out,
                       out_dtypes=(BF16,), name="mm_g_conv_out")
    sib_mix, token = to_sibling(["w_o", "w_pool_out", "w_conv_out"],
                                [g_w_o.reshape(N_DEV, -1, d), g_w_pool_out, g_w_conv_out], slot_idx, "mix")
    d_ya_pre = _mm(d_ya, wg["w_pool_out"], mode="nt", tm=tm_half, tn=pool_w, b_blocked=True, kb=4, after=token,
                   name="mm_d_ya_pre")
    d_s = _mm(d_yb, wg["w_conv_out"], mode="nt", tm=tm_half, tn=conv_w, b_blocked=True, kb=4, name="mm_d_s")
    pending_mix, token = to_owner(sib_mix, d_s)
    dproj, g_wpg, d_scale = _pool_bwd(proj, d_ya_pre, wpg_full, pool_scale, dproj, lp)
    dc, d_ln_g, d_ln_b, d_b_dw = _ln_silu_bwd(conv_c, d_s, conv_ln_g, conv_ln_b, lp)
    dproj, dgc, g_wdw = _conv_bwd(proj, dc, wdw_full, dproj, lp, conv_w, pool_w)
    dproj = _place_columns(dproj, [(dgc, pool_w + conv_w), (d_gb, gb_col0)], "place_dproj")
    g_w_in = _mm(u1, dproj, mode="tn", tm=1024, tn=wg["w_in"].shape[2], out_blocked=True, out_dtypes=(BF16,),
                 after=token, name="mm_g_in")
    g_wpg_slots = g_wpg.astype(BF16).reshape(n_grp, N_DEV, grp_rows, gdim).transpose(1, 0, 2, 3)
    sib_in, token = to_sibling(["w_pool_grp", "w_in"],
                               [g_wpg_slots.reshape(N_DEV, n_grp * grp_rows, gdim), g_w_in], slot_idx, "in")
    done = update(pending_down, token)
    pending_in, token = to_owner(sib_in, done)
    done = update(pending_up, token)
    du1 = _mm(dproj, wg["w_in"], mode="nt", tm=tm_half // 2, tn=1024, b_blocked=True, kb=4, after=done, name="mm_du1")
    grad_x2, grad_meta_part, dg_pre_mix = _pre_mix_bwd(du1, h0, dh1, g_pre_mix, seq)
    grad_x = grad_x2[None]

    assert pool_w + conv_w == d and conv_w <= d and LANES <= d
    widen = lambda a: jnp.pad(a, ((0, 0), (0, d - a.shape[1])))
    packed = jnp.concatenate([
        dg_pre_mix, dg_post_mix, dg_pre_mlp, dg_post_mlp,
        jnp.concatenate([d_scale, d_ln_g], axis=1), jnp.concatenate([d_ln_b, d_b_dw], axis=1),
        grad_meta_part, widen(g_wdw), widen(loss_part)], axis=0)
    rep = dict(g_pre_mix=((g_pre_mix, m_g_pre_mix, v_g_pre_mix), (0, 0)),
               g_post_mix=((g_post_mix, m_g_post_mix, v_g_post_mix), (SUBLANES, 0)),
               g_pre_mlp=((g_pre_mlp, m_g_pre_mlp, v_g_pre_mlp), (2 * SUBLANES, 0)),
               g_post_mlp=((g_post_mlp, m_g_post_mlp, v_g_post_mlp), (3 * SUBLANES, 0)),
               pool_scale=((pool_scale, m_pool_scale, v_pool_scale), (4 * SUBLANES, 0)),
               conv_ln_g=((conv_ln_g, m_conv_ln_g, v_conv_ln_g), (4 * SUBLANES, pool_w)),
               conv_ln_b=((conv_ln_b, m_conv_ln_b, v_conv_ln_b), (5 * SUBLANES, 0)),
               b_dw=((b_dw, m_b_dw, v_b_dw), (5 * SUBLANES, conv_w)))
    meta_row0 = 6 * SUBLANES
    wdw_row0 = meta_row0 + N_META
    loss_row0 = wdw_row0 + CONV_TAPS_PADDED
    (small_started,), token = _gather_start([[_fill_slot(packed, slot_idx, F32, slot_idx, "fill_small")]], grad_x2,
                                            "gather_small_start", BARRIER_IDS["gather_small"])
    done = update(pending_mix, token)
    done = update(pending_in, done)
    send, recv, lands = small_started
    fsend, frecv, lands = _gather_forward(lands, recv, done, "gather_small_forward")
    (packed_all,) = _gather_finish(lands, send, recv, [(ALL_CHIPS, fsend, frecv)], [(ALL_CHIPS, fsend, frecv)], done,
                                   "gather_small_finish")
    rep_names = list(rep)
    wdw_wmv = [jnp.pad(a[0], pad_taps) for a in (w_dw, m_w_dw, v_w_dw)]
    rep_out, meta_out, wdw_out, loss_blk = _small_update(
        slot_idx, packed_all, [rep[k][0] for k in rep_names], [rep[k][1] for k in rep_names],
        (meta, m_meta, v_meta), meta_row0, wdw_wmv, wdw_row0, loss_row0)
    small_out = dict(zip(rep_names, rep_out))
    small_out["meta"] = meta_out
    small_out["w_dw"] = [a[:CONV_KERNEL][None] for a in wdw_out]

    order = ["meta", "g_pre_mix", "w_in", "w_pool_grp", "pool_scale", "w_pool_out", "w_dw", "b_dw", "conv_ln_g",
             "conv_ln_b", "w_conv_out", "w_o", "g_post_mix", "g_pre_mlp", "w_up", "w_down", "g_post_mlp"]
    by_name = {**big_out, **small_out}
    result = [loss_blk[0, 0], grad_x]
    for kind in range(4):
        result += [by_name[k][kind] for k in order]
    return tuple(result)
```

```python
import functools

import jax
import jax.numpy as jnp
from jax import lax
from jax.experimental import pallas as pl
from jax.experimental.pallas import tpu as pltpu

F32 = jnp.float32
BF16 = jnp.bfloat16
MESH = pl.DeviceIdType.MESH

N_DEV = 8
N_META = 16
POOL_WINDOWS = (2, 4, 8, 16)
CONV_KERNEL = 31
CONV_TAPS_PADDED = 32
RMS_EPS = 1e-6
LN_EPS = 1e-5
ADAM_LR = 0.001
ADAM_B1 = 0.9
ADAM_B2 = 0.999
ADAM_EPS = 1e-08
ADAM_WD = 0.01
ADAM_STEP = 10

LANES = 128
SUBLANES = 8
ROW_CHUNK = 128
HALO = 32
VMEM_LIMIT_BYTES = 56 * 1024 * 1024


def _cparams(*sem):
    return pltpu.CompilerParams(dimension_semantics=sem if sem else None, vmem_limit_bytes=VMEM_LIMIT_BYTES)


def _round_up(n, m):
    return (n + m - 1) // m * m


def _row_tile(rows, cols, max_elems=640 * 1024):
    best = None
    for t in range(16, rows + 1, 16):
        if rows % t == 0 and (best is None or t * cols <= max_elems):
            best = t
    assert best is not None, (rows, cols)
    return best


def _rowsum8(a):
    t, w = a.shape
    return a.reshape(t // SUBLANES, SUBLANES, w).sum(axis=0)


def _mesh_pos():
    return lax.axis_index("x"), lax.axis_index("y"), lax.axis_index("c")


HBM_SPEC = pl.BlockSpec(memory_space=pltpu.HBM)
SEM_SPEC = pl.BlockSpec(memory_space=pltpu.SEMAPHORE)
ANY_SPEC = pl.BlockSpec(memory_space=pl.ANY)
_DATAFLOW = pltpu.SideEffectType.DATAFLOW_SIDE_EFFECTING


SIBLING_BARRIER_ID = 1
BARRIER_IDS = {name: 2 + i for i, name in enumerate(
    ["gather_first", "gather_rest", "gather_small", "owner_down", "owner_up", "owner_mix", "owner_in"])}


def _sibling_handshake():
    x, y, c = _mesh_pos()
    barrier = pltpu.get_barrier_semaphore()
    pl.semaphore_signal(barrier, inc=1, device_id=(x, y, 1 - c), device_id_type=MESH)
    pl.semaphore_wait(barrier, 1)


def _peer_handshake(peers):
    barrier = pltpu.get_barrier_semaphore()
    for peer in peers:
        pl.semaphore_signal(barrier, inc=1, device_id=peer, device_id_type=MESH)
    pl.semaphore_wait(barrier, len(peers))


def _hbm(a):
    return pltpu.with_memory_space_constraint(a, pltpu.HBM)


def _slot(p):
    return 4 * p[0] + 2 * p[1] + p[2]


def _fill_slot(w, slot_idx, dtype, after, name):
    rows, cols = w.shape
    tr = _row_tile(rows, cols) if rows % 16 == 0 else rows

    def body(idx_ref, w_ref, _, out_ref):
        out_ref[...] = w_ref[...].astype(dtype)

    return pl.pallas_call(
        body, name=name,
        grid_spec=pltpu.PrefetchScalarGridSpec(
            num_scalar_prefetch=1, grid=(rows // tr,),
            in_specs=[pl.BlockSpec((tr, cols), lambda i, idx_ref: (i, 0)), ANY_SPEC],
            out_specs=pl.BlockSpec((None, tr, cols), lambda i, idx_ref: (idx_ref[0], i, 0))),
        out_shape=jax.ShapeDtypeStruct((N_DEV, rows, cols), dtype),
        compiler_params=_cparams("parallel"),
    )(slot_idx, w, after)


def _gather_start(groups, after, name, barrier_id, issue_order=(0, 1, 2, 3)):
    flat = [g for grp in groups for g in grp]
    n, n_grp = len(flat), len(groups)

    def body(*refs):
        lands = refs[:n]
        sems = refs[n + 1:n + 1 + 2 * n_grp]
        token = refs[-1]
        x, y, c = _mesh_pos()
        targets = [(x, y, 1 - c), (1 - x, y, c), (x, 1 - y, c), (1 - x, 1 - y, c)]
        _peer_handshake(targets)
        t = 0
        for gi, grp in enumerate(groups):
            for ti in range(len(grp)):
                mine = lands[t].at[_slot((x, y, c))]
                for k in issue_order:
                    pltpu.make_async_remote_copy(
                        src_ref=mine, dst_ref=mine,
                        send_sem=sems[2 * gi].at[4 * ti + k], recv_sem=sems[2 * gi + 1].at[4 * ti + k],
                        device_id=targets[k], device_id_type=MESH).start()
                t += 1
        token[...] = jnp.zeros_like(token)

    sem_shapes = []
    for grp in groups:
        sem_shapes += [pltpu.SemaphoreType.DMA((4 * len(grp),))] * 2
    outs = pl.pallas_call(
        body, name=name,
        out_shape=tuple(sem_shapes + [pltpu.HBM(g.shape, g.dtype) for g in flat]
                        + [jax.ShapeDtypeStruct((SUBLANES, LANES), F32)]),
        in_specs=tuple([HBM_SPEC] * n + [ANY_SPEC]),
        out_specs=tuple([SEM_SPEC] * (2 * n_grp) + [HBM_SPEC] * n + [pl.BlockSpec(memory_space=pltpu.VMEM)]),
        input_output_aliases={i: 2 * n_grp + i for i in range(n)},
        compiler_params=pltpu.CompilerParams(has_side_effects=_DATAFLOW, collective_id=barrier_id),
    )(*[_hbm(g) for g in flat], after)
    sems, lands, token = outs[:2 * n_grp], outs[2 * n_grp:-1], outs[-1]
    res, t = [], 0
    for gi, grp in enumerate(groups):
        res.append((sems[2 * gi], sems[2 * gi + 1], list(lands[t:t + len(grp)])))
        t += len(grp)
    return res, token


NEAR, FAR, ALL_CHIPS = (0, 1), (2,), (0, 1, 2)


def _gather_forward(lands, recv_sems, after, name, which=ALL_CHIPS):
    n, nw = len(lands), len(which)

    def body(*refs):
        land_refs, recv, _ = refs[:n], refs[n], refs[n + 1]
        fsend, frecv = refs[n + 2], refs[n + 3]
        x, y, c = _mesh_pos()
        chips = [(1 - x, y), (x, 1 - y), (1 - x, 1 - y)]
        _sibling_handshake()
        for t in range(n):
            for i, j in enumerate(which):
                blk = land_refs[t].at[_slot((*chips[j], c))]
                pltpu.make_async_remote_copy(src_ref=blk, dst_ref=blk, send_sem=fsend.at[nw * t + i],
                                             recv_sem=recv.at[4 * t + 1 + j],
                                             device_id=(x, y, 1 - c), device_id_type=MESH).wait_recv()
                pltpu.make_async_remote_copy(src_ref=blk, dst_ref=blk, send_sem=fsend.at[nw * t + i],
                                             recv_sem=frecv.at[nw * t + i],
                                             device_id=(x, y, 1 - c), device_id_type=MESH).start()

    outs = pl.pallas_call(
        body, name=name,
        out_shape=tuple([pltpu.SemaphoreType.DMA((nw * n,))] * 2 + [pltpu.HBM(g.shape, g.dtype) for g in lands]),
        in_specs=tuple([HBM_SPEC] * n + [SEM_SPEC, ANY_SPEC]),
        out_specs=tuple([SEM_SPEC] * 2 + [HBM_SPEC] * n),
        input_output_aliases={i: 2 + i for i in range(n)},
        compiler_params=pltpu.CompilerParams(has_side_effects=_DATAFLOW, collective_id=SIBLING_BARRIER_ID),
    )(*lands, recv_sems, after)
    return outs[0], outs[1], list(outs[2:])


def _gather_finish(lands, send_sems, recv_sems, arrivals, sends, after, name, own=True, direct_sends=True):
    n = len(lands)
    fwd = list(arrivals) + list(sends)
    sem_args = [send_sems, recv_sems]
    where = []
    for _, fs, fr in fwd:
        pos = []
        for arr in (fs, fr):
            hit = [i for i, have in enumerate(sem_args) if have is arr]
            if not hit:
                sem_args.append(arr)
                hit = [len(sem_args) - 1]
            pos.append(hit[0])
        where.append(pos)

    def body(*refs):
        land_refs = refs[:n]
        send, recv = refs[n], refs[n + 1]
        fwd_refs = [refs[n + p] for pos in where for p in pos]
        x, y, c = _mesh_pos()
        sibling = (x, y, 1 - c)
        chips = [(1 - x, y), (x, 1 - y), (1 - x, 1 - y)]

        def desc(ref, s_sem, r_sem):
            return pltpu.make_async_remote_copy(src_ref=ref, dst_ref=ref, send_sem=s_sem, recv_sem=r_sem,
                                                device_id=sibling, device_id_type=MESH)

        for t in range(n):
            mine = land_refs[t].at[_slot((x, y, c))]
            if own:
                desc(land_refs[t].at[_slot(sibling)], send.at[4 * t], recv.at[4 * t]).wait_recv()
            for a, (which, _, _) in enumerate(fwd):
                fs, fr = fwd_refs[2 * a], fwd_refs[2 * a + 1]
                for i, j in enumerate(which):
                    if a < len(arrivals):
                        desc(land_refs[t].at[_slot((*chips[j], 1 - c))], fs.at[len(which) * t + i],
                             fr.at[len(which) * t + i]).wait_recv()
                    else:
                        desc(land_refs[t].at[_slot((*chips[j], c))], fs.at[len(which) * t + i],
                             fr.at[len(which) * t + i]).wait_send()
            if direct_sends:
                for k in range(4):
                    desc(mine, send.at[4 * t + k], recv.at[4 * t + k]).wait_send()

    outs = pl.pallas_call(
        body, name=name,
        out_shape=tuple(pltpu.HBM(g.shape, g.dtype) for g in lands),
        in_specs=tuple([HBM_SPEC] * n + [SEM_SPEC] * len(sem_args) + [ANY_SPEC]),
        out_specs=tuple([HBM_SPEC] * n),
        input_output_aliases={i: i for i in range(n)},
        compiler_params=pltpu.CompilerParams(has_side_effects=_DATAFLOW),
    )(*lands, *sem_args, after)
    return list(outs)


def _pair_exchange_start(grads, after, name):
    n = len(grads)

    def body(*refs):
        ins, lands = refs[:n], refs[n:2 * n]
        send, recv = refs[2 * n + 1], refs[2 * n + 2]
        token = refs[-1]
        x, y, c = _mesh_pos()
        _sibling_handshake()
        for t in range(n):
            for q in range(4):
                pltpu.make_async_remote_copy(
                    src_ref=ins[t].at[2 * q + 1 - c], dst_ref=lands[t].at[q],
                    send_sem=send.at[4 * t + q], recv_sem=recv.at[4 * t + q],
                    device_id=(x, y, 1 - c), device_id_type=MESH).start()
        token[...] = jnp.zeros_like(token)

    land_shapes = [(4,) + g.shape[1:] for g in grads]
    outs = pl.pallas_call(
        body, name=name,
        out_shape=tuple([pltpu.SemaphoreType.DMA((4 * n,))] * 2 + [pltpu.HBM(g.shape, g.dtype) for g in grads]
                        + [pltpu.HBM(ls, g.dtype) for ls, g in zip(land_shapes, grads)]
                        + [jax.ShapeDtypeStruct((SUBLANES, LANES), F32)]),
        in_specs=tuple([HBM_SPEC] * (2 * n) + [ANY_SPEC]),
        out_specs=tuple([SEM_SPEC] * 2 + [HBM_SPEC] * (2 * n) + [pl.BlockSpec(memory_space=pltpu.VMEM)]),
        input_output_aliases={i: 2 + i for i in range(2 * n)},
        compiler_params=pltpu.CompilerParams(has_side_effects=_DATAFLOW, collective_id=SIBLING_BARRIER_ID),
    )(*[_hbm(g) for g in grads], *[_hbm(lax.empty(ls, g.dtype)) for ls, g in zip(land_shapes, grads)], after)
    return outs[0], outs[1], list(outs[2:2 + n]), list(outs[2 + n:2 + 2 * n]), outs[-1]


def _pair_exchange_finish(grads, lands, send_sems, recv_sems, after, name):
    n = len(grads)

    def body(*refs):
        ins, land_refs = refs[:n], refs[n:2 * n]
        send, recv = refs[2 * n], refs[2 * n + 1]
        x, y, c = _mesh_pos()
        for t in range(n):
            for q in range(4):
                cp = pltpu.make_async_remote_copy(
                    src_ref=ins[t].at[q], dst_ref=land_refs[t].at[q], send_sem=send.at[4 * t + q],
                    recv_sem=recv.at[4 * t + q], device_id=(x, y, 1 - c), device_id_type=MESH)
                cp.wait_send()
                cp.wait_recv()

    outs = pl.pallas_call(
        body, name=name,
        out_shape=tuple([pltpu.HBM(g.shape, g.dtype) for g in grads] + [pltpu.HBM(g.shape, g.dtype) for g in lands]),
        in_specs=tuple([HBM_SPEC] * (2 * n) + [SEM_SPEC] * 2 + [ANY_SPEC]),
        out_specs=tuple([HBM_SPEC] * (2 * n)),
        input_output_aliases={i: i for i in range(2 * n)},
        compiler_params=pltpu.CompilerParams(has_side_effects=_DATAFLOW),
    )(*grads, *lands, send_sems, recv_sems, after)
    return list(outs[:n]), list(outs[n:])


def _chip_exchange_start(sums, after, name, barrier_id):
    n = len(sums)

    def body(*refs):
        ins, lands = refs[:n], refs[n:2 * n]
        send, recv = refs[2 * n + 1], refs[2 * n + 2]
        token = refs[-1]
        x, y, c = _mesh_pos()
        chips = [(1 - x, y), (x, 1 - y), (1 - x, 1 - y)]
        _peer_handshake([(*chip, c) for chip in chips])
        for t in range(n):
            for r, chip in enumerate(chips):
                pltpu.make_async_remote_copy(
                    src_ref=ins[t].at[2 * chip[0] + chip[1]], dst_ref=lands[t].at[r],
                    send_sem=send.at[3 * t + r], recv_sem=recv.at[3 * t + r],
                    device_id=(*chip, c), device_id_type=MESH).start()
        token[...] = jnp.zeros_like(token)

    land_shapes = [(3,) + s.shape[1:] for s in sums]
    outs = pl.pallas_call(
        body, name=name,
        out_shape=tuple([pltpu.SemaphoreType.DMA((3 * n,))] * 2 + [pltpu.HBM(s.shape, s.dtype) for s in sums]
                        + [pltpu.HBM(ls, s.dtype) for ls, s in zip(land_shapes, sums)]
                        + [jax.ShapeDtypeStruct((SUBLANES, LANES), F32)]),
        in_specs=tuple([HBM_SPEC] * (2 * n) + [ANY_SPEC]),
        out_specs=tuple([SEM_SPEC] * 2 + [HBM_SPEC] * (2 * n) + [pl.BlockSpec(memory_space=pltpu.VMEM)]),
        input_output_aliases={i: 2 + i for i in range(2 * n)},
        compiler_params=pltpu.CompilerParams(has_side_effects=_DATAFLOW, collective_id=barrier_id),
    )(*[_hbm(s) for s in sums], *[_hbm(lax.empty(ls, s.dtype)) for ls, s in zip(land_shapes, sums)], after)
    return outs[0], outs[1], list(outs[2:2 + n]), list(outs[2 + n:2 + 2 * n]), outs[-1]


def _chip_exchange_finish(sums, lands, send_sems, recv_sems, after, name):
    n = len(sums)

    def body(*refs):
        ins, land_refs = refs[:n], refs[n:2 * n]
        send, recv = refs[2 * n], refs[2 * n + 1]
        x, y, c = _mesh_pos()
        for t in range(n):
            for r in range(3):
                cp = pltpu.make_async_remote_copy(
                    src_ref=ins[t].at[r], dst_ref=land_refs[t].at[r], send_sem=send.at[3 * t + r],
                    recv_sem=recv.at[3 * t + r],
                    device_id=(x, y, 1 - c), device_id_type=MESH)
                cp.wait_send()
                cp.wait_recv()

    outs = pl.pallas_call(
        body, name=name,
        out_shape=tuple(pltpu.HBM(g.shape, g.dtype) for g in lands),
        in_specs=tuple([HBM_SPEC] * (2 * n) + [SEM_SPEC] * 2 + [ANY_SPEC]),
        out_specs=tuple([HBM_SPEC] * n),
        input_output_aliases={n + i: i for i in range(n)},
        compiler_params=pltpu.CompilerParams(has_side_effects=_DATAFLOW),
    )(*sums, *lands, send_sems, recv_sems, after)
    return list(outs)


def _mm(a, b, *, mode, tm, tn, tk=None, b_blocked=False, out_blocked=False, out_dtypes=(F32,),
        epilogue=None, extras=(), after=None, kb=1, out_places=None, row_splits=1, out_block=None, name):
    if mode == "nn":
        m, k = a.shape
        n = b.shape[0] * b.shape[2] if b_blocked else b.shape[1]
        dims = (((1,), (0,)), ((), ()))
    elif mode == "nt":
        m, k = a.shape
        n = b.shape[1] if b_blocked else b.shape[0]
        if b_blocked:
            tk = kb * b.shape[2]
        dims = (((1,), (1,)), ((), ()))
    else:
        k, m = a.shape
        n = b.shape[1]
        dims = (((0,), (0,)), ((), ()))
    tk = k if tk is None else tk
    assert m % tm == 0 and n % tn == 0 and k % tk == 0, (name, m, n, k, tm, tn, tk)
    gm, gn, gk = m // tm, n // tn, k // tk
    if b_blocked:
        assert (tn if mode == "nn" else tk) == kb * b.shape[2], name
    if row_splits > 1:
        assert gk == 1 and epilogue is not None and mode != "tn" and not b_blocked and tm % (16 * row_splits) == 0, name

    if mode == "nn":
        a_spec = pl.BlockSpec((tm, tk), lambda i, j, kk: (i, kk))
        b_spec = (pl.BlockSpec((None, tk, tn), lambda i, j, kk: (j, kk, 0)) if b_blocked
                  else pl.BlockSpec((tk, tn), lambda i, j, kk: (kk, j)))
    elif mode == "nt":
        a_spec = pl.BlockSpec((tm, tk), lambda i, j, kk: (i, kk))
        b_spec = (pl.BlockSpec((kb, tn, tk // kb), lambda i, j, kk: (kk, j, 0)) if b_blocked
                  else pl.BlockSpec((tn, tk), lambda i, j, kk: (j, kk)))
    else:
        a_spec = pl.BlockSpec((tk, tm), lambda i, j, kk: (kk, i))
        b_spec = pl.BlockSpec((tk, tn), lambda i, j, kk: (kk, j))
    out_pack = 1
    if out_blocked and out_block is not None and out_block != tn:
        assert tn % out_block == 0, name
        out_pack = tn // out_block
        out_spec = pl.BlockSpec((out_pack, tm, out_block), lambda i, j, kk: (j, i, 0))
        out_shape = (n // out_block, m, out_block)
    elif out_blocked:
        out_spec = pl.BlockSpec((None, tm, tn), lambda i, j, kk: (j, i, 0))
        out_shape = (gn, m, tn)
    else:
        out_spec = pl.BlockSpec((tm, tn), lambda i, j, kk: (i, j))
        out_shape = (m, n)
    extra_specs = [pl.BlockSpec((tm, tn), functools.partial(lambda i, j, kk, off: (i, j + off), off=off))
                   for _, off in extras]
    n_extra, n_out = len(extras), len(out_dtypes)
    n_after = 0 if after is None else 1
    places = out_places if out_places is not None else (None,) * n_out

    def body(a_ref, b_ref, *rest):
        extra_refs = rest[:n_extra]
        out_refs = rest[n_extra + n_after:n_extra + n_after + n_out]

        def finish(acc):
            if epilogue is None:
                res = (acc,)
            else:
                res = epilogue(acc, *[e[...] for e in extra_refs])
            for o_ref, r in zip(out_refs, res):
                if out_pack == 1:
                    o_ref[...] = r.astype(o_ref.dtype)
                else:
                    for h in range(out_pack):
                        o_ref[h] = r[:, h * out_block:(h + 1) * out_block].astype(o_ref.dtype)

        if row_splits > 1:
            strip = tm // row_splits
            for h in range(row_splits):
                rows = slice(h * strip, (h + 1) * strip)
                acc = lax.dot_general(a_ref[rows, :], b_ref[...], dims, preferred_element_type=F32)
                res = epilogue(acc, *[e[rows, :] for e in extra_refs])
                for o_ref, r in zip(out_refs, res):
                    o_ref[rows, :] = r.astype(o_ref.dtype)
            return
        if mode == "nt" and b_blocked:
            bk = tk // kb
            part = lax.dot_general(a_ref[:, :bk], b_ref[0], dims, preferred_element_type=F32)
            for h in range(1, kb):
                part = part + lax.dot_general(a_ref[:, h * bk:(h + 1) * bk], b_ref[h], dims,
                                              preferred_element_type=F32)
        else:
            part = lax.dot_general(a_ref[...], b_ref[...], dims, preferred_element_type=F32)
        if gk == 1:
            finish(part)
        else:
            acc_ref = rest[-1]
            kk = pl.program_id(2)

            @pl.when(kk == 0)
            def _():
                acc_ref[...] = part

            @pl.when(kk > 0)
            def _():
                acc_ref[...] += part

            @pl.when(kk == gk - 1)
            def _():
                finish(acc_ref[...])

    outs = pl.pallas_call(
        body, name=name, grid=(gm, gn, gk),
        in_specs=[a_spec, b_spec] + extra_specs + [ANY_SPEC] * n_after,
        out_specs=[out_spec if place is None else
                   pl.BlockSpec((tm, tn), functools.partial(lambda i, j, kk, off: (i, j + off), off=place[1] // tn))
                   for place in places],
        out_shape=[jax.ShapeDtypeStruct(out_shape if place is None else (m, place[0]), dt)
                   for dt, place in zip(out_dtypes, places)],
        scratch_shapes=[pltpu.VMEM((tm, tn), F32)] if gk > 1 else [],
        compiler_params=_cparams("parallel", "parallel", "arbitrary"),
    )(a, b, *[e for e, _ in extras], *([] if after is None else [after]))
    return outs[0] if n_out == 1 else outs


def _mm_slots(a, w, slots, *, over, tm, tn=None, out_dtype=F32, epilogue=None, base=None, name):
    m = a.shape[0]
    ns = slots.shape[0]
    n_slots, w1, w2 = w.shape
    assert m % tm == 0
    if over == "n":
        k, bn = w1, w2

        def body(slots_ref, a_ref, w_ref, *rest):
            out_ref = rest[-1]
            acc = jnp.dot(a_ref[...], w_ref[...], preferred_element_type=F32)
            out_ref[...] = (acc if epilogue is None else epilogue(acc)).astype(out_ref.dtype)

        in_specs = [pl.BlockSpec((tm, k), lambda i, j, s: (i, 0)),
                    pl.BlockSpec((None, k, bn), lambda i, j, s: (s[j], 0, 0))]
        args = [a, w]
        aliases = {}
        if base is not None:
            in_specs.append(ANY_SPEC)
            args.append(base)
            aliases = {3: 0}
        return pl.pallas_call(
            body, name=name,
            grid_spec=pltpu.PrefetchScalarGridSpec(
                num_scalar_prefetch=1, grid=(m // tm, ns), in_specs=in_specs,
                out_specs=pl.BlockSpec((tm, bn), lambda i, j, s: (i, s[j]))),
            out_shape=jax.ShapeDtypeStruct((m, n_slots * bn), out_dtype),
            input_output_aliases=aliases,
            compiler_params=_cparams("parallel", "arbitrary"),
        )(slots, *args)

    bk, n = w1, w2
    tn = n if tn is None else tn
    assert n % tn == 0

    def body(slots_ref, a_ref, w_ref, *rest):
        out_ref, acc_ref = rest[-2], rest[-1]
        kk = pl.program_id(2)
        part = jnp.dot(a_ref[...], w_ref[...], preferred_element_type=F32)

        @pl.when(kk == 0)
        def _():
            acc_ref[...] = part if base is None else part + rest[0][...]

        @pl.when(kk > 0)
        def _():
            acc_ref[...] += part

        @pl.when(kk == ns - 1)
        def _():
            out_ref[...] = acc_ref[...].astype(out_ref.dtype)

    in_specs = [pl.BlockSpec((tm, bk), lambda i, j, kk, s: (i, s[kk])),
                pl.BlockSpec((None, bk, tn), lambda i, j, kk, s: (s[kk], 0, j))]
    args = [a, w]
    if base is not None:
        in_specs.append(pl.BlockSpec((tm, tn), lambda i, j, kk, s: (i, j)))
        args.append(base)
    return pl.pallas_call(
        body, name=name,
        grid_spec=pltpu.PrefetchScalarGridSpec(
            num_scalar_prefetch=1, grid=(m // tm, n // tn, ns), in_specs=in_specs,
            out_specs=pl.BlockSpec((tm, tn), lambda i, j, kk, s: (i, j)),
            scratch_shapes=[pltpu.VMEM((tm, tn), F32)]),
        out_shape=jax.ShapeDtypeStruct((m, n), out_dtype),
        compiler_params=_cparams("parallel", "parallel", "arbitrary"),
    )(slots, *args)


def _gate_mix(ya_pre, s, wpo, wco, proj, d_model, name):
    lp, width = ya_pre.shape
    nb, _, bn = wpo.shape
    ga_off = (proj.shape[1] - 2 * d_model) // bn
    gb_off = (proj.shape[1] - d_model) // bn

    n_strips = 4 if lp % 64 == 0 else 1

    def body(ya_ref, s_ref, wpo_ref, wco_ref, ga_ref, gb_ref, m_ref, y_a_ref, y_b_ref):
        strip = lp // n_strips
        for h in range(n_strips):
            rows = slice(h * strip, (h + 1) * strip)
            y_a = jnp.dot(ya_ref[rows, :], wpo_ref[...], preferred_element_type=F32)
            y_b = jnp.dot(s_ref[rows, :], wco_ref[...], preferred_element_type=F32)
            m = jax.nn.sigmoid(ga_ref[rows, :]) * y_a + jax.nn.sigmoid(gb_ref[rows, :]) * y_b
            m_ref[rows, :] = m.astype(BF16)
            y_a_ref[rows, :] = y_a.astype(BF16)
            y_b_ref[rows, :] = y_b.astype(BF16)

    act_spec = pl.BlockSpec((lp, width), lambda j: (0, 0))
    w_spec = pl.BlockSpec((None, width, bn), lambda j: (j, 0, 0))
    out_spec = pl.BlockSpec((lp, bn), lambda j: (0, j))
    return pl.pallas_call(
        body, name=name, grid=(nb,),
        in_specs=[act_spec, act_spec, w_spec, w_spec,
                  pl.BlockSpec((lp, bn), lambda j: (0, j + ga_off)),
                  pl.BlockSpec((lp, bn), lambda j: (0, j + gb_off))],
        out_specs=[out_spec] * 3,
        out_shape=[jax.ShapeDtypeStruct((lp, nb * bn), BF16)] * 3,
        compiler_params=_cparams("parallel"),
    )(ya_pre, s, wpo, wco, proj, proj)


def _rms_stats(x):
    return lax.rsqrt(jnp.mean(x * x, axis=-1, keepdims=True) + RMS_EPS)


def _rms_bwd(x, g, dy):
    r = _rms_stats(x)
    nrm = x * r
    dn = dy * g
    dx = r * (dn - nrm * jnp.mean(dn * nrm, axis=-1, keepdims=True))
    return dx, dy * nrm


def _rowwise(body, ins, outs, accs, *, lp, name):
    tr = _row_tile(lp, max(a.shape[1] for a in ins))
    n_in, n_out, n_acc = len(ins), len(outs), len(accs)

    def kernel_body(*refs):
        i = pl.program_id(0)
        acc_refs = refs[n_in + n_out:]

        @pl.when(i == 0)
        def _():
            for r in acc_refs:
                r[...] = jnp.zeros_like(r)

        body(i * tr, refs[:n_in], refs[n_in:n_in + n_out], acc_refs)

    in_specs = []
    for a in ins:
        if a.shape[0] == lp:
            in_specs.append(pl.BlockSpec((tr, a.shape[1]), lambda i: (i, 0)))
        else:
            in_specs.append(pl.BlockSpec(a.shape, lambda i: (0, 0)))
    out_specs = [pl.BlockSpec((tr, w), lambda i: (i, 0)) for w, _ in outs]
    out_specs += [pl.BlockSpec((SUBLANES, w), lambda i: (0, 0)) for w in accs]
    out_shape = [jax.ShapeDtypeStruct((lp, w), d) for w, d in outs]
    out_shape += [jax.ShapeDtypeStruct((SUBLANES, w), F32) for w in accs]
    return pl.pallas_call(
        kernel_body, name=name, grid=(lp // tr,), in_specs=in_specs, out_specs=out_specs,
        out_shape=out_shape, compiler_params=_cparams("arbitrary"),
    )(*ins)


SHIFT_TILE = 128


def _shifted_specs(width, n_big, n_small):
    per = SHIFT_TILE // N_META
    small = pl.BlockSpec((N_META, width), lambda i: (jnp.clip(per * i - 1, 0, n_small - 1), 0))
    big = pl.BlockSpec((SHIFT_TILE, width), lambda i: (jnp.minimum(i, n_big - 1), 0))
    return small, big


def _rms_pre(meta_full, x2, g, lp, seq):
    d = x2.shape[1]
    assert seq % SHIFT_TILE == 0 and lp % SHIFT_TILE == 0 and SHIFT_TILE % N_META == 0

    def body(meta_ref, xs_ref, xb_ref, g_ref, h0_ref, u1_ref):
        i = pl.program_id(0)
        head = jnp.where(i == 0, meta_ref[...], xs_ref[...])
        rows = jnp.concatenate([head, xb_ref[:SHIFT_TILE - N_META, :]], axis=0)
        r = i * SHIFT_TILE + lax.broadcasted_iota(jnp.int32, (SHIFT_TILE, 1), 0)
        rows = jnp.where(r < N_META + seq, rows, 0.0)
        h0_ref[...] = rows
        u1_ref[...] = (rows * _rms_stats(rows) * g_ref[...]).astype(BF16)

    small, big = _shifted_specs(d, seq // SHIFT_TILE, seq // N_META)
    tile = pl.BlockSpec((SHIFT_TILE, d), lambda i: (i, 0))
    return pl.pallas_call(
        body, name="rms_pre", grid=(lp // SHIFT_TILE,),
        in_specs=[pl.BlockSpec((N_META, d), lambda i: (0, 0)), small, big, pl.BlockSpec((1, d), lambda i: (0, 0))],
        out_specs=[tile, tile],
        out_shape=[jax.ShapeDtypeStruct((lp, d), F32), jax.ShapeDtypeStruct((lp, d), BF16)],
        compiler_params=_cparams("parallel"),
    )(meta_full, x2, x2, g)


def _post_mix(o, h0, g_post_mix, g_pre_mlp, lp):
    d = h0.shape[1]

    def body(row0, ins, outs, accs):
        o_ref, h0_ref, g1_ref, g2_ref = ins
        o_v = o_ref[...]
        h1 = h0_ref[...] + o_v * _rms_stats(o_v) * g1_ref[...]
        outs[0][...] = h1
        outs[1][...] = (h1 * _rms_stats(h1) * g2_ref[...]).astype(BF16)

    return _rowwise(body, [o, h0, g_post_mix, g_pre_mlp], [(d, F32), (d, BF16)], [], lp=lp, name="post_mix")


def _loss_head(f, h1, target, g_post_mlp, lp, seq):
    d = f.shape[1]

    def body(f_ref, h1_ref, ts_ref, tb_ref, g_ref, df_ref, dh_ref, dg_ref, loss_ref):
        i = pl.program_id(0)

        @pl.when(i == 0)
        def _():
            dg_ref[...] = jnp.zeros_like(dg_ref)
            loss_ref[...] = jnp.zeros_like(loss_ref)

        f_v, g = f_ref[...], g_ref[...]
        r = _rms_stats(f_v)
        nrm = f_v * r
        rows = i * SHIFT_TILE + lax.broadcasted_iota(jnp.int32, (SHIFT_TILE, 1), 0)
        valid = (rows >= N_META) & (rows < N_META + seq)
        tgt = jnp.concatenate([ts_ref[...], tb_ref[:SHIFT_TILE - N_META, :]], axis=0)
        err = jnp.where(valid, h1_ref[...] + nrm * g - tgt, 0.0)
        loss_ref[...] += 0.5 * jnp.sum(jnp.mean(err * err, axis=-1, keepdims=True))
        dy = err * (1.0 / d)
        dn = dy * g
        df_ref[...] = (r * (dn - nrm * jnp.mean(dn * nrm, axis=-1, keepdims=True))).astype(BF16)
        dh_ref[...] = dy
        dg_ref[...] += _rowsum8(dy * nrm)

    small, big = _shifted_specs(d, seq // SHIFT_TILE, seq // N_META)
    tile = pl.BlockSpec((SHIFT_TILE, d), lambda i: (i, 0))
    return pl.pallas_call(
        body, name="loss_head", grid=(lp // SHIFT_TILE,),
        in_specs=[tile, tile, small, big, pl.BlockSpec((1, d), lambda i: (0, 0))],
        out_specs=[tile, tile, pl.BlockSpec((SUBLANES, d), lambda i: (0, 0)),
                   pl.BlockSpec((SUBLANES, LANES), lambda i: (0, 0))],
        out_shape=[jax.ShapeDtypeStruct((lp, d), BF16), jax.ShapeDtypeStruct((lp, d), F32),
                   jax.ShapeDtypeStruct((SUBLANES, d), F32), jax.ShapeDtypeStruct((SUBLANES, LANES), F32)],
        compiler_params=_cparams("arbitrary"),
    )(f, h1, target, target, g_post_mlp)


def _mid_bwd(du2, h1, dh, o, g_pre_mlp, g_post_mix, lp):
    d = h1.shape[1]

    def body(row0, ins, outs, accs):
        du2_ref, h1_ref, dh_ref, o_ref, g2_ref, g1_ref = ins
        dx2, dg2 = _rms_bwd(h1_ref[...], g2_ref[...], du2_ref[...])
        dh1 = dh_ref[...] + dx2
        do, dg1 = _rms_bwd(o_ref[...], g1_ref[...], dh1)
        outs[0][...] = dh1
        outs[1][...] = do.astype(BF16)
        accs[0][...] += _rowsum8(dg2)
        accs[1][...] += _rowsum8(dg1)

    return _rowwise(body, [du2, h1, dh, o, g_pre_mlp, g_post_mix], [(d, F32), (d, BF16)], [d, d], lp=lp,
                    name="mid_bwd")


def _pre_mix_bwd(du1, h0, dh1, g_pre_mix, seq):
    d = h0.shape[1]
    per = SHIFT_TILE // N_META
    assert seq % SHIFT_TILE == 0

    def body(du_b, h_b, dh_b, du_n, h_n, dh_n, du_m, h_m, dh_m, g_ref, gx_ref, gm_ref, dg_ref):
        i = pl.program_id(0)
        g = g_ref[...]

        @pl.when(i == 0)
        def _():
            dx, dg = _rms_bwd(h_m[...], g, du_m[...])
            gm_ref[...] = dh_m[...] + dx
            dg_ref[...] = _rowsum8(dg)

        rows = lambda big, nxt: jnp.concatenate([big[N_META:, :], nxt[...]], axis=0)
        dx, dg = _rms_bwd(rows(h_b, h_n), g, rows(du_b, du_n))
        gx_ref[...] = rows(dh_b, dh_n) + dx
        dg_ref[...] += _rowsum8(dg)

    big = pl.BlockSpec((SHIFT_TILE, d), lambda i: (i, 0))
    nxt = pl.BlockSpec((N_META, d), lambda i: (per * (i + 1), 0))
    first = pl.BlockSpec((N_META, d), lambda i: (0, 0))
    return pl.pallas_call(
        body, name="pre_mix_bwd", grid=(seq // SHIFT_TILE,),
        in_specs=[big] * 3 + [nxt] * 3 + [first] * 3 + [pl.BlockSpec((1, d), lambda i: (0, 0))],
        out_specs=[big, first, pl.BlockSpec((SUBLANES, d), lambda i: (0, 0))],
        out_shape=[jax.ShapeDtypeStruct((seq, d), F32), jax.ShapeDtypeStruct((N_META, d), F32),
                   jax.ShapeDtypeStruct((SUBLANES, d), F32)],
        compiler_params=_cparams("arbitrary"),
    )(du1, h0, dh1, du1, h0, dh1, du1, h0, dh1, g_pre_mix)


def _ln_stats(c):
    mu = jnp.mean(c, axis=-1, keepdims=True)
    var = jnp.mean(jnp.square(c - mu), axis=-1, keepdims=True)
    return mu, lax.rsqrt(var + LN_EPS)


def _ln_silu(c, ln_g, ln_b, lp):
    w = c.shape[1]

    def body(row0, ins, outs, accs):
        c_ref, g_ref, b_ref = ins
        c_v = c_ref[...]
        mu, rstd = _ln_stats(c_v)
        ln = (c_v - mu) * rstd * g_ref[...] + b_ref[...]
        outs[0][...] = (ln * jax.nn.sigmoid(ln)).astype(BF16)

    return _rowwise(body, [c, ln_g, ln_b], [(w, BF16)], [], lp=lp, name="ln_silu")[0]


def _ln_silu_bwd(c, ds, ln_g, ln_b, lp):
    w = c.shape[1]

    def body(row0, ins, outs, accs):
        c_ref, ds_ref, g_ref, b_ref = ins
        c_v, g = c_ref[...], g_ref[...]
        mu, rstd = _ln_stats(c_v)
        nrm = (c_v - mu) * rstd
        ln = nrm * g + b_ref[...]
        sig = jax.nn.sigmoid(ln)
        dln = ds_ref[...] * (sig * (1.0 + ln * (1.0 - sig)))
        dn = dln * g
        dc = rstd * (dn - jnp.mean(dn, axis=-1, keepdims=True) - nrm * jnp.mean(dn * nrm, axis=-1, keepdims=True))
        outs[0][...] = dc
        accs[0][...] += _rowsum8(dln * nrm)
        accs[1][...] += _rowsum8(dln)
        accs[2][...] += _rowsum8(dc)

    return _rowwise(body, [c, ds, ln_g, ln_b], [(w, F32)], [w, w, w], lp=lp, name="ln_silu_bwd")


def _chunk_with_history(ref, i, cols=slice(None)):
    t0 = pl.multiple_of(i * ROW_CHUNK, ROW_CHUNK)
    lo0 = pl.multiple_of(jnp.maximum(t0 - HALO, 0), SUBLANES)
    lo = jnp.where(i > 0, ref[pl.ds(lo0, HALO), cols], 0.0)
    return jnp.concatenate([lo, ref[pl.ds(t0, ROW_CHUNK), cols]], axis=0)


def _chunk_with_future(ref, i, n_chunks, cols=slice(None)):
    t0 = pl.multiple_of(i * ROW_CHUNK, ROW_CHUNK)
    hi0 = pl.multiple_of(jnp.minimum(t0 + ROW_CHUNK, (n_chunks - 1) * ROW_CHUNK), SUBLANES)
    hi = jnp.where(i < n_chunks - 1, ref[pl.ds(hi0, HALO), cols], 0.0)
    return jnp.concatenate([ref[pl.ds(t0, ROW_CHUNK), cols], hi], axis=0)


def _inv_count(t0, n_rows, window):
    pos = t0 + lax.broadcasted_iota(jnp.int32, (n_rows, 1), 0)
    return 1.0 / jnp.minimum(pos + 1, window).astype(F32)


def _pool_delta(z_hist, t0, window):
    s = z_hist
    sh = 1
    while sh < window:
        s = s + pltpu.roll(s, sh, 0)
        sh *= 2
    cur = z_hist[HALO:, :]
    return s[HALO:, :] * _inv_count(t0, ROW_CHUNK, window) - cur


def _pool_fwd(proj, wpg, pool_scale, lp):
    n_grp, gdim, _ = wpg.shape
    width = n_grp * gdim
    n_chunks = lp // ROW_CHUNK

    def body(z_ref, w_ref, sc_ref, out_ref):
        for g, window in enumerate(POOL_WINDOWS):
            cols = slice(g * gdim, (g + 1) * gdim)

            def chunk(i, carry, cols=cols, g=g, window=window):
                t0 = pl.multiple_of(i * ROW_CHUNK, ROW_CHUNK)
                d = _pool_delta(_chunk_with_history(z_ref, i, cols), t0, window)
                q = jnp.dot(d.astype(BF16), w_ref[g], preferred_element_type=F32)
                out_ref[pl.ds(t0, ROW_CHUNK), cols] = (q * sc_ref[:, cols]).astype(BF16)
                return carry

            lax.fori_loop(0, n_chunks, chunk, 0)

    return pl.pallas_call(
        body, name="pool_fwd", grid=(1,),
        in_specs=[pl.BlockSpec((lp, width), lambda i: (0, 0)),
                  pl.BlockSpec(wpg.shape, lambda i: (0, 0, 0)),
                  pl.BlockSpec(pool_scale.shape, lambda i: (0, 0))],
        out_specs=pl.BlockSpec((lp, width), lambda i: (0, 0)),
        out_shape=jax.ShapeDtypeStruct((lp, width), BF16),
        compiler_params=_cparams("arbitrary"),
    )(proj, wpg, pool_scale)


def _pool_bwd(proj, d_ya, wpg, pool_scale, dproj, lp):
    n_grp, gdim, _ = wpg.shape
    width = n_grp * gdim
    n_chunks = lp // ROW_CHUNK
    ext = ROW_CHUNK + HALO

    def body(z_ref, dya_ref, w_ref, sc_ref, _, dz_ref, dw_ref, dsc_ref):
        dw_ref[...] = jnp.zeros_like(dw_ref)
        dsc_ref[...] = jnp.zeros_like(dsc_ref)
        for g, window in enumerate(POOL_WINDOWS):
            cols = slice(g * gdim, (g + 1) * gdim)

            def chunk(i, carry, cols=cols, g=g, window=window):
                t0 = pl.multiple_of(i * ROW_CHUNK, ROW_CHUNK)
                w_g = w_ref[g]
                scale = sc_ref[:, cols]
                d = _pool_delta(_chunk_with_history(z_ref, i, cols), t0, window).astype(BF16)
                dya_ext = _chunk_with_future(dya_ref, i, n_chunks, cols)
                dya = dya_ext[:ROW_CHUNK, :]
                q = jnp.dot(d, w_g, preferred_element_type=F32)
                dsc_ref[:, cols] += _rowsum8(dya * q)
                e_ext = (dya_ext * scale).astype(BF16)
                dw_ref[g] += lax.dot_general(d, e_ext[:ROW_CHUNK, :], (((0,), (0,)), ((), ())),
                                             preferred_element_type=F32)
                dd_ext = lax.dot_general(e_ext, w_g, (((1,), (1,)), ((), ())), preferred_element_type=F32)
                s = dd_ext * _inv_count(t0, ext, window)
                sh = 1
                while sh < window:
                    s = s + pltpu.roll(s, ext - sh, 0)
                    sh *= 2
                dz_ref[pl.ds(t0, ROW_CHUNK), cols] = (s[:ROW_CHUNK, :] - dd_ext[:ROW_CHUNK, :]).astype(BF16)
                return carry

            lax.fori_loop(0, n_chunks, chunk, 0)

    blk = pl.BlockSpec((lp, width), lambda i: (0, 0))
    return pl.pallas_call(
        body, name="pool_bwd", grid=(1,),
        in_specs=[blk, blk, pl.BlockSpec(wpg.shape, lambda i: (0, 0, 0)),
                  pl.BlockSpec(pool_scale.shape, lambda i: (0, 0)), ANY_SPEC],
        out_specs=[blk, pl.BlockSpec(wpg.shape, lambda i: (0, 0, 0)),
                   pl.BlockSpec((SUBLANES, width), lambda i: (0, 0))],
        out_shape=[jax.ShapeDtypeStruct(dproj.shape, BF16), jax.ShapeDtypeStruct(wpg.shape, F32),
                   jax.ShapeDtypeStruct((SUBLANES, width), F32)],
        input_output_aliases={4: 0},
        compiler_params=_cparams("arbitrary"),
    )(proj, d_ya, wpg, pool_scale, dproj)


def _conv_fwd(proj, w_dw, b_dw, lp, width, v_col0):
    n_chunks = lp // ROW_CHUNK
    v_blk0, g_blk0 = v_col0 // LANES, (v_col0 + width) // LANES

    def body(v_ref, gc_ref, w_ref, b_ref, c_ref, a_pad):
        a_pad[pl.ds(0, HALO), :] = jnp.zeros((HALO, LANES), F32)
        a_pad[pl.ds(HALO, lp), :] = v_ref[...] * jax.nn.sigmoid(gc_ref[...])

        def chunk(i, carry):
            t0 = pl.multiple_of(i * ROW_CHUNK, ROW_CHUNK)
            hist = a_pad[pl.ds(t0, ROW_CHUNK + HALO), :]
            acc = jnp.zeros((ROW_CHUNK, LANES), F32)
            for k in range(CONV_KERNEL):
                acc = acc + w_ref[k:k + 1, :] * pltpu.roll(hist, CONV_KERNEL - 1 - k, 0)[HALO:, :]
            c_ref[pl.ds(t0, ROW_CHUNK), :] = acc + b_ref[...]
            return carry

        lax.fori_loop(0, n_chunks, chunk, 0)

    return pl.pallas_call(
        body, name="conv_fwd", grid=(width // LANES,),
        in_specs=[pl.BlockSpec((lp, LANES), lambda j: (0, j + v_blk0)),
                  pl.BlockSpec((lp, LANES), lambda j: (0, j + g_blk0)),
                  pl.BlockSpec((CONV_TAPS_PADDED, LANES), lambda j: (0, j)),
                  pl.BlockSpec((1, LANES), lambda j: (0, j))],
        out_specs=pl.BlockSpec((lp, LANES), lambda j: (0, j)),
        out_shape=jax.ShapeDtypeStruct((lp, width), F32),
        scratch_shapes=[pltpu.VMEM((lp + HALO, LANES), F32)],
        compiler_params=_cparams("parallel"),
    )(proj, proj, w_dw, b_dw)


def _conv_bwd(proj, dc, w_dw, dproj, lp, width, v_col0):
    n_chunks = lp // ROW_CHUNK
    ext = ROW_CHUNK + HALO
    v_blk0, g_blk0 = v_col0 // LANES, (v_col0 + width) // LANES

    def body(v_ref, gc_ref, dc_ref, w_ref, _, dv_ref, dgc_ref, dw_ref, a_pad, dc_pad, dw_acc):
        sig = jax.nn.sigmoid(gc_ref[...])
        a_pad[pl.ds(0, HALO), :] = jnp.zeros((HALO, LANES), F32)
        a_pad[pl.ds(HALO, lp), :] = v_ref[...] * sig
        dc_pad[pl.ds(0, lp), :] = dc_ref[...]
        dc_pad[pl.ds(lp, HALO), :] = jnp.zeros((HALO, LANES), F32)
        dw_acc[...] = jnp.zeros_like(dw_acc)

        def chunk(i, carry):
            t0 = pl.multiple_of(i * ROW_CHUNK, ROW_CHUNK)
            hist = a_pad[pl.ds(t0, ext), :]
            fut = dc_pad[pl.ds(t0, ext), :]
            dc_cur = fut[:ROW_CHUNK, :]
            da = jnp.zeros((ROW_CHUNK, LANES), F32)
            for k in range(CONV_KERNEL):
                lag = CONV_KERNEL - 1 - k
                da = da + w_ref[k:k + 1, :] * pltpu.roll(fut, (ext - lag) % ext, 0)[:ROW_CHUNK, :]
                dw_acc[pl.ds(SUBLANES * k, SUBLANES), :] += _rowsum8(dc_cur * pltpu.roll(hist, lag, 0)[HALO:, :])
            rows = pl.ds(t0, ROW_CHUNK)
            sg = jax.nn.sigmoid(gc_ref[rows, :])
            dv_ref[rows, :] = (da * sg).astype(BF16)
            dgc_ref[rows, :] = (da * v_ref[rows, :] * sg * (1.0 - sg)).astype(BF16)
            return carry

        lax.fori_loop(0, n_chunks, chunk, 0)
        dw_ref[...] = dw_acc[...].reshape(CONV_TAPS_PADDED, SUBLANES, LANES).sum(axis=1)

    col = lambda j: (0, j)
    return pl.pallas_call(
        body, name="conv_bwd", grid=(width // LANES,),
        in_specs=[pl.BlockSpec((lp, LANES), lambda j: (0, j + v_blk0)),
                  pl.BlockSpec((lp, LANES), lambda j: (0, j + g_blk0)),
                  pl.BlockSpec((lp, LANES), col),
                  pl.BlockSpec((CONV_TAPS_PADDED, LANES), col), ANY_SPEC],
        out_specs=[pl.BlockSpec((lp, LANES), lambda j: (0, j + v_blk0)), pl.BlockSpec((lp, LANES), col),
                   pl.BlockSpec((CONV_TAPS_PADDED, LANES), col)],
        out_shape=[jax.ShapeDtypeStruct(dproj.shape, BF16), jax.ShapeDtypeStruct((lp, width), BF16),
                   jax.ShapeDtypeStruct((CONV_TAPS_PADDED, width), F32)],
        scratch_shapes=[pltpu.VMEM((lp + HALO, LANES), F32), pltpu.VMEM((lp + HALO, LANES), F32),
                        pltpu.VMEM((CONV_TAPS_PADDED * SUBLANES, LANES), F32)],
        input_output_aliases={4: 0},
        compiler_params=_cparams("parallel"),
    )(proj, proj, dc, w_dw, dproj)


def _place_columns(dst, pieces, name):
    m = dst.shape[0]
    tile = 512
    counts = [p.shape[1] // tile for p, _ in pieces]
    starts = [sum(counts[:i]) for i in range(len(pieces))]
    n_steps = sum(counts)

    def local(s, i):
        return jnp.clip(s - starts[i], 0, counts[i] - 1)

    def out_index(s):
        blk = pieces[0][1] // tile + local(s, 0)
        for i in range(1, len(pieces)):
            blk = jnp.where(s >= starts[i], pieces[i][1] // tile + local(s, i), blk)
        return 0, blk

    def body(*refs):
        out_ref = refs[-1]
        s = pl.program_id(0)
        for i in range(len(pieces)):
            @pl.when((s >= starts[i]) & (s < starts[i] + counts[i]))
            def _(i=i):
                out_ref[...] = refs[i][...]

    return pl.pallas_call(
        body, name=name, grid=(n_steps,),
        in_specs=[pl.BlockSpec((m, tile), functools.partial(lambda s, i: (0, local(s, i)), i=i))
                  for i in range(len(pieces))] + [ANY_SPEC],
        out_specs=pl.BlockSpec((m, tile), out_index),
        out_shape=jax.ShapeDtypeStruct(dst.shape, dst.dtype),
        input_output_aliases={len(pieces): 0},
        compiler_params=_cparams("arbitrary"),
    )(*[p for p, _ in pieces], dst)


def _adamw_math(w, g, m, v):
    m = ADAM_B1 * m + (1.0 - ADAM_B1) * g
    v = ADAM_B2 * v + (1.0 - ADAM_B2) * jnp.square(g)
    m_hat = m / (1.0 - ADAM_B1 ** ADAM_STEP)
    v_hat = v / (1.0 - ADAM_B2 ** ADAM_STEP)
    delta = -ADAM_LR * (m_hat / (jnp.sqrt(v_hat) + ADAM_EPS) + ADAM_WD * w)
    return delta, m, v


def _pair_sum(own, recv, where, name):
    _, _, rows, cols = own.shape
    tr = _row_tile(rows, cols, 1024 * 1024)

    def body(where_ref, own_ref, recv_ref, out_ref):
        out_ref[...] = (own_ref[...].astype(F32) + recv_ref[...].astype(F32)).astype(BF16)

    return pl.pallas_call(
        body, name=name,
        grid_spec=pltpu.PrefetchScalarGridSpec(
            num_scalar_prefetch=1, grid=(3, rows // tr),
            in_specs=[pl.BlockSpec((None, None, tr, cols), lambda r, i, wh: (wh[2 + r], wh[0], i, 0)),
                      pl.BlockSpec((None, tr, cols), lambda r, i, wh: (wh[2 + r], i, 0))],
            out_specs=pl.BlockSpec((None, tr, cols), lambda r, i, wh: (wh[2 + r], i, 0))),
        out_shape=jax.ShapeDtypeStruct((4, rows, cols), BF16),
        compiler_params=_cparams("parallel", "parallel"),
    )(where, own, recv)


def _adamw_big(w, m, v, own, from_sibling, recv3, where, name):
    rows, cols = w.shape
    tr = _row_tile(rows, cols, 256 * 1024)

    def body(where_ref, w_ref, m_ref, v_ref, own_ref, sib_ref, r_ref, g_out, d_out, m_out, v_out):
        g = own_ref[...].astype(F32) + sib_ref[...].astype(F32)
        for r in range(3):
            g = g + r_ref[r].astype(F32)
        delta, m_new, v_new = _adamw_math(w_ref[...], g, m_ref[...], v_ref[...])
        g_out[...] = g
        d_out[...] = delta
        m_out[...] = m_new
        v_out[...] = v_new

    blk = pl.BlockSpec((tr, cols), lambda i, q_ref: (i, 0))
    return pl.pallas_call(
        body, name=name,
        grid_spec=pltpu.PrefetchScalarGridSpec(
            num_scalar_prefetch=1, grid=(rows // tr,),
            in_specs=[blk, blk, blk,
                      pl.BlockSpec((None, None, tr, cols), lambda i, wh: (wh[1], wh[0], i, 0)),
                      pl.BlockSpec((None, tr, cols), lambda i, wh: (wh[1], i, 0)),
                      pl.BlockSpec((3, tr, cols), lambda i, wh: (0, i, 0))],
            out_specs=[blk] * 4),
        out_shape=[jax.ShapeDtypeStruct((rows, cols), F32)] * 4,
        compiler_params=_cparams("parallel"),
    )(where, w, m, v, own, from_sibling, recv3)


def _small_update(me_idx, packed, rep_params, rep_places, meta_wmv, meta_row0, wdw_wmv, wdw_row0, loss_row0):
    n_rep = len(rep_params)
    meta_rows, meta_cols = meta_wmv[0].shape
    wdw_rows, wdw_cols = wdw_wmv[0].shape

    def body(me_ref, *refs):
        pos = 0

        def take(k):
            nonlocal pos
            out = refs[pos:pos + k]
            pos += k
            return out

        rep_in = [take(3) for _ in range(n_rep)]
        rep_g = take(n_rep)
        meta_in, (meta_g,) = take(3), take(1)
        wdw_in, (wdw_g,) = take(3), take(1)
        (loss_ref,) = take(1)
        rep_out = [take(4) for _ in range(n_rep)]
        meta_out, wdw_out = take(4), take(4)
        (loss_out,) = take(1)

        def update(wmv, g, outs):
            delta, m_new, v_new = _adamw_math(wmv[0][...], g, wmv[1][...], wmv[2][...])
            for o_ref, val in zip(outs, (g, delta, m_new, v_new)):
                o_ref[...] = val

        for wmv, g_ref, outs in zip(rep_in, rep_g, rep_out):
            g = jnp.sum(g_ref[0], axis=0, keepdims=True)
            for j in range(1, N_DEV):
                g = g + jnp.sum(g_ref[j], axis=0, keepdims=True)
            update(wmv, g, outs)
        for wmv, g_ref, outs in ((meta_in, meta_g, meta_out), (wdw_in, wdw_g, wdw_out)):
            g = g_ref[0]
            for j in range(1, N_DEV):
                g = g + g_ref[j]
            update(wmv, g, outs)
        total = loss_ref[0]
        for j in range(1, N_DEV):
            total = total + loss_ref[j]
        loss_out[...] = total

    def whole(a):
        nd = a.ndim
        return pl.BlockSpec(a.shape, lambda i, me_ref, nd=nd: (0,) * nd)

    ins, in_specs = [], []
    for wmv in rep_params:
        ins += list(wmv)
        in_specs += [whole(a) for a in wmv]
    for wmv, (row0, col0) in zip(rep_params, rep_places):
        width = wmv[0].shape[1]
        ins.append(packed)
        in_specs.append(pl.BlockSpec((N_DEV, SUBLANES, width),
                                     lambda i, me_ref, rb=row0 // SUBLANES, cb=col0 // width: (0, rb, cb)))
    ins += list(meta_wmv) + [packed]
    in_specs += [whole(a) for a in meta_wmv]
    in_specs.append(pl.BlockSpec((N_DEV, meta_rows, meta_cols),
                                 lambda i, me_ref, rb=meta_row0 // meta_rows: (0, rb, me_ref[0])))
    ins += list(wdw_wmv) + [packed]
    in_specs += [whole(a) for a in wdw_wmv]
    in_specs.append(pl.BlockSpec((N_DEV, wdw_rows, wdw_cols),
                                 lambda i, me_ref, rb=wdw_row0 // wdw_rows: (0, rb, me_ref[0])))
    ins.append(packed)
    in_specs.append(pl.BlockSpec((N_DEV, SUBLANES, LANES), lambda i, me_ref, rb=loss_row0 // SUBLANES: (0, rb, 0)))

    out_shape, out_specs = [], []
    for wmv in list(rep_params) + [meta_wmv, wdw_wmv]:
        out_shape += [jax.ShapeDtypeStruct(wmv[0].shape, F32)] * 4
        out_specs += [whole(wmv[0])] * 4
    out_shape.append(jax.ShapeDtypeStruct((SUBLANES, LANES), F32))
    out_specs.append(pl.BlockSpec((SUBLANES, LANES), lambda i, me_ref: (0, 0)))

    outs = pl.pallas_call(
        body, name="small_update",
        grid_spec=pltpu.PrefetchScalarGridSpec(num_scalar_prefetch=1, grid=(1,), in_specs=in_specs,
                                               out_specs=out_specs),
        out_shape=out_shape, compiler_params=_cparams("arbitrary"),
    )(me_idx, *ins)
    groups = [outs[4 * i:4 * i + 4] for i in range(n_rep + 2)]
    return groups[:n_rep], groups[n_rep], groups[n_rep + 1], outs[-1]


def kernel(x, meta, g_pre_mix, w_in, w_pool_grp, pool_scale, w_pool_out, w_dw, b_dw, conv_ln_g, conv_ln_b, w_conv_out, w_o, g_post_mix, g_pre_mlp, w_up, w_down, g_post_mlp, loss_target, m_meta, m_g_pre_mix, m_w_in, m_w_pool_grp, m_pool_scale, m_w_pool_out, m_w_dw, m_b_dw, m_conv_ln_g, m_conv_ln_b, m_w_conv_out, m_w_o, m_g_post_mix, m_g_pre_mlp, m_w_up, m_w_down, m_g_post_mlp, v_meta, v_g_pre_mix, v_w_in, v_w_pool_grp, v_pool_scale, v_w_pool_out, v_w_dw, v_b_dw, v_conv_ln_g, v_conv_ln_b, v_w_conv_out, v_w_o, v_g_post_mix, v_g_pre_mlp, v_w_up, v_w_down, v_g_post_mlp):
    seq, d = x.shape[1], x.shape[2]
    pool_w = pool_scale.shape[1]
    conv_w = b_dw.shape[1]
    n_grp, grp_rows, gdim = w_pool_grp.shape[1:]
    lp = _round_up(N_META + seq, ROW_CHUNK)
    tm_half = lp // 2 if (lp // 2) % 16 == 0 else lp
    c_idx = lax.axis_index("c").astype(jnp.int32)
    chip_idx = (2 * lax.axis_index("x") + lax.axis_index("y")).astype(jnp.int32)
    me_idx = 2 * chip_idx + c_idx

    pad_taps = ((0, CONV_TAPS_PADDED - CONV_KERNEL), (0, 0))
    big = dict(w_in=w_in[0], w_pool_grp=w_pool_grp[0].reshape(n_grp * grp_rows, gdim), w_pool_out=w_pool_out[0],
               w_conv_out=w_conv_out[0], w_o=w_o[0], w_up=w_up[0], w_down=w_down[0])
    big_names = list(big)
    moments = dict(w_in=(m_w_in, v_w_in), w_pool_grp=(m_w_pool_grp, v_w_pool_grp), w_pool_out=(m_w_pool_out, v_w_pool_out),
                   w_conv_out=(m_w_conv_out, v_w_conv_out), w_o=(m_w_o, v_w_o), w_up=(m_w_up, v_w_up),
                   w_down=(m_w_down, v_w_down))
    slot_idx = me_idx.reshape(1)
    sources = dict(big, meta=meta, w_dw=jnp.pad(w_dw[0], pad_taps))

    def fill(k, after):
        return _fill_slot(sources[k], slot_idx, BF16 if k in big else F32, after, "fill_" + k)

    gather_groups = [["meta", "w_dw"], ["w_in"], ["w_pool_grp", "w_pool_out", "w_conv_out", "w_o"], ["w_up"], ["w_down"]]
    started, token = _gather_start([[fill(k, slot_idx) for k in names] for names in gather_groups[:2]], slot_idx,
                                   "gather_start_first", BARRIER_IDS["gather_first"], issue_order=(0, 3, 1, 2))
    started_rest, _ = _gather_start([[fill(k, token) for k in names] for names in gather_groups[2:]], token,
                                    "gather_start_rest", BARRIER_IDS["gather_rest"])
    started += started_rest
    wg = {}
    x_idx, y_idx = lax.axis_index("x"), lax.axis_index("y")
    at = lambda px, py, pc: 4 * px + 2 * py + pc
    near_slots = jnp.stack([at(x_idx, y_idx, c_idx), at(x_idx, y_idx, 1 - c_idx), at(1 - x_idx, y_idx, c_idx),
                            at(x_idx, 1 - y_idx, c_idx), at(1 - x_idx, y_idx, 1 - c_idx),
                            at(x_idx, 1 - y_idx, 1 - c_idx)]).astype(jnp.int32)
    far_slots = jnp.stack([at(1 - x_idx, 1 - y_idx, c_idx), at(1 - x_idx, 1 - y_idx, 1 - c_idx)]).astype(jnp.int32)

    def gather_whole(gi, after_forward, after_finish):
        send, recv, lands = started[gi]
        fs, fr, lands = _gather_forward(lands, recv, after_forward(), f"gather_forward_{gi}")
        lands = _gather_finish(lands, send, recv, [(ALL_CHIPS, fs, fr)], [(ALL_CHIPS, fs, fr)], after_finish(),
                               f"gather_finish_{gi}")
        wg.update(zip(gather_groups[gi], lands))

    near_state = {}

    def gather_near(gi, after):
        send, recv, lands = started[gi]
        fs, fr, lands = _gather_forward(lands, recv, after, f"gather_forward_near_{gi}", which=NEAR)
        lands = _gather_finish(lands, send, recv, [(NEAR, fs, fr)], [], after, f"gather_finish_near_{gi}",
                               direct_sends=False)
        near_state[gi] = (fs, fr)
        return lands

    def gather_far(gi, lands, after):
        send, recv, _ = started[gi]
        fs, fr, lands = _gather_forward(lands, recv, after, f"gather_forward_far_{gi}", which=FAR)
        lands = _gather_finish(lands, send, recv, [(FAR, fs, fr)], [(NEAR,) + near_state[gi], (FAR, fs, fr)], after,
                               f"gather_finish_far_{gi}", own=False)
        wg.update(zip(gather_groups[gi], lands))
        return lands

    gather_whole(0, lambda: token, lambda: token)
    meta_full = wg["meta"].transpose(1, 0, 2).reshape(N_META, d)
    wdw_full = wg["w_dw"].transpose(1, 0, 2).reshape(CONV_TAPS_PADDED, conv_w)
    target = loss_target[0]
    h0, u1 = _rms_pre(meta_full, x[0], g_pre_mix, lp, seq)
    send, recv, w_in_lands = started[1]
    w_in_lands = _gather_finish(w_in_lands, send, recv, [], [], u1, "gather_finish_home_1", direct_sends=False)
    proj = _mm_slots(u1, w_in_lands[0], near_slots[:2], over="n", tm=tm_half, name="mm_proj_home")
    fs_far, fr_far, w_in_lands = _gather_forward(w_in_lands, recv, proj, "gather_forward_far_1", which=FAR)
    w_in_lands = _gather_finish(w_in_lands, send, recv, [(FAR, fs_far, fr_far)], [], proj, "gather_finish_far_1",
                                own=False, direct_sends=False)
    proj = _mm_slots(u1, w_in_lands[0], far_slots, over="n", tm=tm_half, base=proj, name="mm_proj_far")
    fs_near, fr_near, w_in_lands = _gather_forward(w_in_lands, recv, proj, "gather_forward_near_1", which=NEAR)
    w_in_lands = _gather_finish(w_in_lands, send, recv, [(NEAR, fs_near, fr_near)],
                                [(FAR, fs_far, fr_far), (NEAR, fs_near, fr_near)], proj, "gather_finish_near_1",
                                own=False)
    proj = _mm_slots(u1, w_in_lands[0], near_slots[2:], over="n", tm=tm_half, base=proj, name="mm_proj_near")
    wg["w_in"] = w_in_lands[0]
    conv_c = _conv_fwd(proj, wdw_full, b_dw, lp, conv_w, pool_w)
    s_act = _ln_silu(conv_c, conv_ln_g, conv_ln_b, lp)
    gather_whole(2, lambda: s_act, lambda: s_act)
    wpg_full = wg["w_pool_grp"].reshape(N_DEV, n_grp, grp_rows, gdim).transpose(1, 0, 2, 3).reshape(n_grp, gdim, gdim)
    w_o_full = wg["w_o"].reshape(d, d)
    ya_pre = _pool_fwd(proj, wpg_full, pool_scale, lp)
    m_mix, y_a, y_b = _gate_mix(ya_pre, s_act, wg["w_pool_out"], wg["w_conv_out"], proj, d, "gate_mix")
    o = _mm(m_mix, w_o_full, mode="nn", tm=tm_half, tn=512, name="mm_o")
    h1, u2 = _post_mix(o, h0, g_post_mix, g_pre_mlp, lp)
    relu2 = lambda acc: jnp.square(jnp.maximum(acc, 0.0))
    (w_up_near,) = gather_near(3, u2)
    act = _mm_slots(u2, w_up_near, near_slots, over="n", tm=tm_half, out_dtype=BF16, epilogue=relu2, name="mm_up_near")
    (w_up_all,) = gather_far(3, [w_up_near], act)
    act = _mm_slots(u2, w_up_all, far_slots, over="n", tm=tm_half, out_dtype=BF16, epilogue=relu2, base=act,
                    name="mm_up_far")
    (w_down_near,) = gather_near(4, act)
    f = _mm_slots(act, w_down_near, near_slots, over="k", tm=tm_half, tn=d, name="mm_down_near")
    (w_down_all,) = gather_far(4, [w_down_near], f)
    f = _mm_slots(act, w_down_all, far_slots, over="k", tm=tm_half // 2, tn=d, base=f, name="mm_down_far")
    w_down_full = w_down_all.reshape(-1, d)

    big_out = {}

    def to_sibling(names, grads, after, tag):
        send, recv, grads, lands, token = _pair_exchange_start(grads, after, "grads_to_sibling_start_" + tag)
        return (names, send, recv, grads, lands, tag), token

    where = jnp.stack([c_idx, chip_idx, 2 * (1 - x_idx) + y_idx, 2 * x_idx + (1 - y_idx),
                       2 * (1 - x_idx) + (1 - y_idx)]).astype(jnp.int32)

    def to_owner(handle, after):
        names, send, recv, grads, from_sib, tag = handle
        grads, from_sib = _pair_exchange_finish(grads, from_sib, send, recv, after, "grads_to_sibling_finish_" + tag)
        own = [g.reshape((4, 2) + g.shape[1:]) for g in grads]
        sums = [_pair_sum(o, r, where, "pair_sum_" + k) for k, o, r in zip(names, own, from_sib)]
        send, recv, sums, lands, token = _chip_exchange_start(sums, after, "grads_to_owner_start_" + tag,
                                                              BARRIER_IDS["owner_" + tag])
        return (names, send, recv, sums, lands, own, from_sib, tag), token

    def update(handle, after):
        names, send, recv, sums, lands, own, from_sib, tag = handle
        got = _chip_exchange_finish(sums, lands, send, recv, after, "grads_to_owner_finish_" + tag)
        for k, o, s, r3 in zip(names, own, from_sib, got):
            w2 = big[k]
            shape = moments[k][0].shape
            outs = _adamw_big(w2, moments[k][0].reshape(w2.shape), moments[k][1].reshape(w2.shape), o, s, r3,
                              where, "adamw_" + k)
            big_out[k] = [a.reshape(shape) for a in outs]
        return big_out[names[-1]][0]

    df, dh, dg_post_mlp, loss_part = _loss_head(f, h1, target, g_post_mlp, lp, seq)
    d_up = _mm(df, w_down_full, mode="nt", tm=tm_half, tn=2048, out_dtypes=(BF16,), extras=[(act, 0)],
               epilogue=lambda acc, a: (acc * (2.0 * jnp.sqrt(a.astype(F32))),), name="mm_d_up")
    g_w_down = _mm(act, df, mode="tn", tm=1024, tn=1024, out_dtypes=(BF16,), name="mm_g_down")
    sib_down, token = to_sibling(["w_down"], [g_w_down.reshape(N_DEV, -1, d)], slot_idx, "down")
    g_w_up = _mm(u2, d_up, mode="tn", tm=1024, tn=wg["w_up"].shape[2], out_blocked=True, out_dtypes=(BF16,),
                 after=token, name="mm_g_up")
    sib_up, token = to_sibling(["w_up"], [g_w_up], slot_idx, "up")
    pending_down, token = to_owner(sib_down, token)
    du2 = _mm(d_up, wg["w_up"], mode="nt", tm=tm_half, tn=1024, b_blocked=True, kb=2, after=token, name="mm_du2")
    pending_up, token = to_owner(sib_up, du2)
    dh1, do, dg_pre_mlp, dg_post_mix = _mid_bwd(du2, h1, dh, o, g_pre_mlp, g_post_mix, lp)

    def gate_bwd(dm, ga, gb, ya, yb):
        sa, sb = jax.nn.sigmoid(ga), jax.nn.sigmoid(gb)
        return (dm * ya.astype(F32) * sa * (1.0 - sa), dm * yb.astype(F32) * sb * (1.0 - sb), dm * sa, dm * sb)

    gate_tn = 512
    ga_col0, gb_col0 = proj.shape[1] - 2 * d, proj.shape[1] - d
    dproj, d_gb, d_ya, d_yb = _mm(
        do, w_o_full, mode="nt", tm=tm_half, tn=gate_tn, out_dtypes=(BF16,) * 4,
        extras=[(proj, ga_col0 // gate_tn), (proj, gb_col0 // gate_tn), (y_a, 0), (y_b, 0)], epilogue=gate_bwd,
        out_places=((proj.shape[1], ga_col0), None, None, None), after=token, row_splits=2, name="mm_dm")
    g_w_o = _mm(m_mix, do, mode="tn", tm=1024, tn=1024, out_dtypes=(BF16,), name="mm_g_o")
    bn_out = wg["w_pool_out"].shape[2]
    g_w_pool_out = _mm(ya_pre, d_ya, mode="tn", tm=pool_w, tn=4 * bn_out, out_blocked=True, out_block=bn_out,
                       out_dtypes=(BF16,), name="mm_g_pool_out")
    g_w_conv_out = _mm(s_act, d_yb, mode="tn", tm=conv_w, tn=4 * bn_out, out_blocked=True, out_block=bn_out,
                       out_dtypes=(BF16,), name="mm_g_conv_out")
    sib_mix, token = to_sibling(["w_o", "w_pool_out", "w_conv_out"],
                                [g_w_o.reshape(N_DEV, -1, d), g_w_pool_out, g_w_conv_out], slot_idx, "mix")
    d_ya_pre = _mm(d_ya, wg["w_pool_out"], mode="nt", tm=tm_half, tn=pool_w, b_blocked=True, kb=4, after=token,
                   name="mm_d_ya_pre")
    d_s = _mm(d_yb, wg["w_conv_out"], mode="nt", tm=tm_half, tn=conv_w, b_blocked=True, kb=4, name="mm_d_s")
    pending_mix, token = to_owner(sib_mix, d_s)
    dproj, g_wpg, d_scale = _pool_bwd(proj, d_ya_pre, wpg_full, pool_scale, dproj, lp)
    dc, d_ln_g, d_ln_b, d_b_dw = _ln_silu_bwd(conv_c, d_s, conv_ln_g, conv_ln_b, lp)
    dproj, dgc, g_wdw = _conv_bwd(proj, dc, wdw_full, dproj, lp, conv_w, pool_w)
    dproj = _place_columns(dproj, [(dgc, pool_w + conv_w), (d_gb, gb_col0)], "place_dproj")
    g_w_in = _mm(u1, dproj, mode="tn", tm=1024, tn=wg["w_in"].shape[2], out_blocked=True, out_dtypes=(BF16,),
                 after=token, name="mm_g_in")
    g_wpg_slots = g_wpg.astype(BF16).reshape(n_grp, N_DEV, grp_rows, gdim).transpose(1, 0, 2, 3)
    sib_in, token = to_sibling(["w_pool_grp", "w_in"],
                               [g_wpg_slots.reshape(N_DEV, n_grp * grp_rows, gdim), g_w_in], slot_idx, "in")
    done = update(pending_down, token)
    pending_in, token = to_owner(sib_in, done)
    done = update(pending_up, token)
    du1 = _mm(dproj, wg["w_in"], mode="nt", tm=tm_half // 2, tn=1024, b_blocked=True, kb=4, after=done, name="mm_du1")
    grad_x2, grad_meta_part, dg_pre_mix = _pre_mix_bwd(du1, h0, dh1, g_pre_mix, seq)
    grad_x = grad_x2[None]

    assert pool_w + conv_w == d and conv_w <= d and LANES <= d
    widen = lambda a: jnp.pad(a, ((0, 0), (0, d - a.shape[1])))
    packed = jnp.concatenate([
        dg_pre_mix, dg_post_mix, dg_pre_mlp, dg_post_mlp,
        jnp.concatenate([d_scale, d_ln_g], axis=1), jnp.concatenate([d_ln_b, d_b_dw], axis=1),
        grad_meta_part, widen(g_wdw), widen(loss_part)], axis=0)
    rep = dict(g_pre_mix=((g_pre_mix, m_g_pre_mix, v_g_pre_mix), (0, 0)),
               g_post_mix=((g_post_mix, m_g_post_mix, v_g_post_mix), (SUBLANES, 0)),
               g_pre_mlp=((g_pre_mlp, m_g_pre_mlp, v_g_pre_mlp), (2 * SUBLANES, 0)),
               g_post_mlp=((g_post_mlp, m_g_post_mlp, v_g_post_mlp), (3 * SUBLANES, 0)),
               pool_scale=((pool_scale, m_pool_scale, v_pool_scale), (4 * SUBLANES, 0)),
               conv_ln_g=((conv_ln_g, m_conv_ln_g, v_conv_ln_g), (4 * SUBLANES, pool_w)),
               conv_ln_b=((conv_ln_b, m_conv_ln_b, v_conv_ln_b), (5 * SUBLANES, 0)),
               b_dw=((b_dw, m_b_dw, v_b_dw), (5 * SUBLANES, conv_w)))
    meta_row0 = 6 * SUBLANES
    wdw_row0 = meta_row0 + N_META
    loss_row0 = wdw_row0 + CONV_TAPS_PADDED
    (small_started,), token = _gather_start([[_fill_slot(packed, slot_idx, F32, slot_idx, "fill_small")]], grad_x2,
                                            "gather_small_start", BARRIER_IDS["gather_small"])
    done = update(pending_mix, token)
    done = update(pending_in, done)
    send, recv, lands = small_started
    fsend, frecv, lands = _gather_forward(lands, recv, done, "gather_small_forward")
    (packed_all,) = _gather_finish(lands, send, recv, [(ALL_CHIPS, fsend, frecv)], [(ALL_CHIPS, fsend, frecv)], done,
                                   "gather_small_finish")
    rep_names = list(rep)
    wdw_wmv = [jnp.pad(a[0], pad_taps) for a in (w_dw, m_w_dw, v_w_dw)]
    rep_out, meta_out, wdw_out, loss_blk = _small_update(
        slot_idx, packed_all, [rep[k][0] for k in rep_names], [rep[k][1] for k in rep_names],
        (meta, m_meta, v_meta), meta_row0, wdw_wmv, wdw_row0, loss_row0)
    small_out = dict(zip(rep_names, rep_out))
    small_out["meta"] = meta_out
    small_out["w_dw"] = [a[:CONV_KERNEL][None] for a in wdw_out]

    order = ["meta", "g_pre_mix", "w_in", "w_pool_grp", "pool_scale", "w_pool_out", "w_dw", "b_dw", "conv_ln_g",
             "conv_ln_b", "w_conv_out", "w_o", "g_post_mix", "g_pre_mlp", "w_up", "w_down", "g_post_mlp"]
    by_name = {**big_out, **small_out}
    result = [loss_blk[0, 0], grad_x]
    for kind in range(4):
        result += [by_name[k][kind] for k in order]
    return tuple(result)
```

```python
import functools

import jax
import jax.numpy as jnp
from jax import lax
from jax.experimental import pallas as pl
from jax.experimental.pallas import tpu as pltpu

F32 = jnp.float32
BF16 = jnp.bfloat16
MESH = pl.DeviceIdType.MESH

N_DEV = 8
N_META = 16
POOL_WINDOWS = (2, 4, 8, 16)
CONV_KERNEL = 31
CONV_TAPS_PADDED = 32
RMS_EPS = 1e-6
LN_EPS = 1e-5
ADAM_LR = 0.001
ADAM_B1 = 0.9
ADAM_B2 = 0.999
ADAM_EPS = 1e-08
ADAM_WD = 0.01
ADAM_STEP = 10

LANES = 128
SUBLANES = 8
ROW_CHUNK = 128
HALO = 32
VMEM_LIMIT_BYTES = 56 * 1024 * 1024


def _cparams(*sem):
    return pltpu.CompilerParams(dimension_semantics=sem if sem else None, vmem_limit_bytes=VMEM_LIMIT_BYTES)


def _round_up(n, m):
    return (n + m - 1) // m * m


def _row_tile(rows, cols, max_elems=640 * 1024):
    best = None
    for t in range(16, rows + 1, 16):
        if rows % t == 0 and (best is None or t * cols <= max_elems):
            best = t
    assert best is not None, (rows, cols)
    return best


def _rowsum8(a):
    t, w = a.shape
    return a.reshape(t // SUBLANES, SUBLANES, w).sum(axis=0)


def _mesh_pos():
    return lax.axis_index("x"), lax.axis_index("y"), lax.axis_index("c")


HBM_SPEC = pl.BlockSpec(memory_space=pltpu.HBM)
SEM_SPEC = pl.BlockSpec(memory_space=pltpu.SEMAPHORE)
ANY_SPEC = pl.BlockSpec(memory_space=pl.ANY)
_DATAFLOW = pltpu.SideEffectType.DATAFLOW_SIDE_EFFECTING


SIBLING_BARRIER_ID = 1
BARRIER_IDS = {name: 2 + i for i, name in enumerate(
    ["gather_first", "gather_rest", "gather_small", "owner_down", "owner_up", "owner_mix", "owner_in"])}


def _sibling_handshake():
    x, y, c = _mesh_pos()
    barrier = pltpu.get_barrier_semaphore()
    pl.semaphore_signal(barrier, inc=1, device_id=(x, y, 1 - c), device_id_type=MESH)
    pl.semaphore_wait(barrier, 1)


def _peer_handshake(peers):
    barrier = pltpu.get_barrier_semaphore()
    for peer in peers:
        pl.semaphore_signal(barrier, inc=1, device_id=peer, device_id_type=MESH)
    pl.semaphore_wait(barrier, len(peers))


def _hbm(a):
    return pltpu.with_memory_space_constraint(a, pltpu.HBM)


def _slot(p):
    return 4 * p[0] + 2 * p[1] + p[2]


def _fill_slot(w, slot_idx, dtype, after, name):
    rows, cols = w.shape
    tr = _row_tile(rows, cols) if rows % 16 == 0 else rows

    def body(idx_ref, w_ref, _, out_ref):
        out_ref[...] = w_ref[...].astype(dtype)

    return pl.pallas_call(
        body, name=name,
        grid_spec=pltpu.PrefetchScalarGridSpec(
            num_scalar_prefetch=1, grid=(rows // tr,),
            in_specs=[pl.BlockSpec((tr, cols), lambda i, idx_ref: (i, 0)), ANY_SPEC],
            out_specs=pl.BlockSpec((None, tr, cols), lambda i, idx_ref: (idx_ref[0], i, 0))),
        out_shape=jax.ShapeDtypeStruct((N_DEV, rows, cols), dtype),
        compiler_params=_cparams("parallel"),
    )(slot_idx, w, after)


def _gather_start(groups, after, name, barrier_id, issue_order=(0, 1, 2, 3)):
    flat = [g for grp in groups for g in grp]
    n, n_grp = len(flat), len(groups)

    def body(*refs):
        lands = refs[:n]
        sems = refs[n + 1:n + 1 + 2 * n_grp]
        token = refs[-1]
        x, y, c = _mesh_pos()
        targets = [(x, y, 1 - c), (1 - x, y, c), (x, 1 - y, c), (1 - x, 1 - y, c)]
        _peer_handshake(targets)
        t = 0
        for gi, grp in enumerate(groups):
            for ti in range(len(grp)):
                mine = lands[t].at[_slot((x, y, c))]
                for k in issue_order:
                    pltpu.make_async_remote_copy(
                        src_ref=mine, dst_ref=mine,
                        send_sem=sems[2 * gi].at[4 * ti + k], recv_sem=sems[2 * gi + 1].at[4 * ti + k],
                        device_id=targets[k], device_id_type=MESH).start()
                t += 1
        token[...] = jnp.zeros_like(token)

    sem_shapes = []
    for grp in groups:
        sem_shapes += [pltpu.SemaphoreType.DMA((4 * len(grp),))] * 2
    outs = pl.pallas_call(
        body, name=name,
        out_shape=tuple(sem_shapes + [pltpu.HBM(g.shape, g.dtype) for g in flat]
                        + [jax.ShapeDtypeStruct((SUBLANES, LANES), F32)]),
        in_specs=tuple([HBM_SPEC] * n + [ANY_SPEC]),
        out_specs=tuple([SEM_SPEC] * (2 * n_grp) + [HBM_SPEC] * n + [pl.BlockSpec(memory_space=pltpu.VMEM)]),
        input_output_aliases={i: 2 * n_grp + i for i in range(n)},
        compiler_params=pltpu.CompilerParams(has_side_effects=_DATAFLOW, collective_id=barrier_id),
    )(*[_hbm(g) for g in flat], after)
    sems, lands, token = outs[:2 * n_grp], outs[2 * n_grp:-1], outs[-1]
    res, t = [], 0
    for gi, grp in enumerate(groups):
        res.append((sems[2 * gi], sems[2 * gi + 1], list(lands[t:t + len(grp)])))
        t += len(grp)
    return res, token


NEAR, FAR, ALL_CHIPS = (0, 1), (2,), (0, 1, 2)


def _gather_forward(lands, recv_sems, after, name, which=ALL_CHIPS):
    n, nw = len(lands), len(which)

    def body(*refs):
        land_refs, recv, _ = refs[:n], refs[n], refs[n + 1]
        fsend, frecv = refs[n + 2], refs[n + 3]
        x, y, c = _mesh_pos()
        chips = [(1 - x, y), (x, 1 - y), (1 - x, 1 - y)]
        _sibling_handshake()
        for t in range(n):
            for i, j in enumerate(which):
                blk = land_refs[t].at[_slot((*chips[j], c))]
                pltpu.make_async_remote_copy(src_ref=blk, dst_ref=blk, send_sem=fsend.at[nw * t + i],
                                             recv_sem=recv.at[4 * t + 1 + j],
                                             device_id=(x, y, 1 - c), device_id_type=MESH).wait_recv()
                pltpu.make_async_remote_copy(src_ref=blk, dst_ref=blk, send_sem=fsend.at[nw * t + i],
                                             recv_sem=frecv.at[nw * t + i],
                                             device_id=(x, y, 1 - c), device_id_type=MESH).start()

    outs = pl.pallas_call(
        body, name=name,
        out_shape=tuple([pltpu.SemaphoreType.DMA((nw * n,))] * 2 + [pltpu.HBM(g.shape, g.dtype) for g in lands]),
        in_specs=tuple([HBM_SPEC] * n + [SEM_SPEC, ANY_SPEC]),
        out_specs=tuple([SEM_SPEC] * 2 + [HBM_SPEC] * n),
        input_output_aliases={i: 2 + i for i in range(n)},
        compiler_params=pltpu.CompilerParams(has_side_effects=_DATAFLOW, collective_id=SIBLING_BARRIER_ID),
    )(*lands, recv_sems, after)
    return outs[0], outs[1], list(outs[2:])


def _gather_finish(lands, send_sems, recv_sems, arrivals, sends, after, name, own=True, direct_sends=True):
    n = len(lands)
    fwd = list(arrivals) + list(sends)
    sem_args = [send_sems, recv_sems]
    where = []
    for _, fs, fr in fwd:
        pos = []
        for arr in (fs, fr):
            hit = [i for i, have in enumerate(sem_args) if have is arr]
            if not hit:
                sem_args.append(arr)
                hit = [len(sem_args) - 1]
            pos.append(hit[0])
        where.append(pos)

    def body(*refs):
        land_refs = refs[:n]
        send, recv = refs[n], refs[n + 1]
        fwd_refs = [refs[n + p] for pos in where for p in pos]
        x, y, c = _mesh_pos()
        sibling = (x, y, 1 - c)
        chips = [(1 - x, y), (x, 1 - y), (1 - x, 1 - y)]

        def desc(ref, s_sem, r_sem):
            return pltpu.make_async_remote_copy(src_ref=ref, dst_ref=ref, send_sem=s_sem, recv_sem=r_sem,
                                                device_id=sibling, device_id_type=MESH)

        for t in range(n):
            mine = land_refs[t].at[_slot((x, y, c))]
            if own:
                desc(land_refs[t].at[_slot(sibling)], send.at[4 * t], recv.at[4 * t]).wait_recv()
            for a, (which, _, _) in enumerate(fwd):
                fs, fr = fwd_refs[2 * a], fwd_refs[2 * a + 1]
                for i, j in enumerate(which):
                    if a < len(arrivals):
                        desc(land_refs[t].at[_slot((*chips[j], 1 - c))], fs.at[len(which) * t + i],
                             fr.at[len(which) * t + i]).wait_recv()
                    else:
                        desc(land_refs[t].at[_slot((*chips[j], c))], fs.at[len(which) * t + i],
                             fr.at[len(which) * t + i]).wait_send()
            if direct_sends:
                for k in range(4):
                    desc(mine, send.at[4 * t + k], recv.at[4 * t + k]).wait_send()

    outs = pl.pallas_call(
        body, name=name,
        out_shape=tuple(pltpu.HBM(g.shape, g.dtype) for g in lands),
        in_specs=tuple([HBM_SPEC] * n + [SEM_SPEC] * len(sem_args) + [ANY_SPEC]),
        out_specs=tuple([HBM_SPEC] * n),
        input_output_aliases={i: i for i in range(n)},
        compiler_params=pltpu.CompilerParams(has_side_effects=_DATAFLOW),
    )(*lands, *sem_args, after)
    return list(outs)


def _pair_exchange_start(grads, after, name):
    n = len(grads)

    def body(*refs):
        ins, lands = refs[:n], refs[n:2 * n]
        send, recv = refs[2 * n + 1], refs[2 * n + 2]
        token = refs[-1]
        x, y, c = _mesh_pos()
        _sibling_handshake()
        for t in range(n):
            for q in range(4):
                pltpu.make_async_remote_copy(
                    src_ref=ins[t].at[2 * q + 1 - c], dst_ref=lands[t].at[q],
                    send_sem=send.at[4 * t + q], recv_sem=recv.at[4 * t + q],
                    device_id=(x, y, 1 - c), device_id_type=MESH).start()
        token[...] = jnp.zeros_like(token)

    land_shapes = [(4,) + g.shape[1:] for g in grads]
    outs = pl.pallas_call(
        body, name=name,
        out_shape=tuple([pltpu.SemaphoreType.DMA((4 * n,))] * 2 + [pltpu.HBM(g.shape, g.dtype) for g in grads]
                        + [pltpu.HBM(ls, g.dtype) for ls, g in zip(land_shapes, grads)]
                        + [jax.ShapeDtypeStruct((SUBLANES, LANES), F32)]),
        in_specs=tuple([HBM_SPEC] * (2 * n) + [ANY_SPEC]),
        out_specs=tuple([SEM_SPEC] * 2 + [HBM_SPEC] * (2 * n) + [pl.BlockSpec(memory_space=pltpu.VMEM)]),
        input_output_aliases={i: 2 + i for i in range(2 * n)},
        compiler_params=pltpu.CompilerParams(has_side_effects=_DATAFLOW, collective_id=SIBLING_BARRIER_ID),
    )(*[_hbm(g) for g in grads], *[_hbm(lax.empty(ls, g.dtype)) for ls, g in zip(land_shapes, grads)], after)
    return outs[0], outs[1], list(outs[2:2 + n]), list(outs[2 + n:2 + 2 * n]), outs[-1]


def _pair_exchange_finish(grads, lands, send_sems, recv_sems, after, name):
    n = len(grads)

    def body(*refs):
        ins, land_refs = refs[:n], refs[n:2 * n]
        send, recv = refs[2 * n], refs[2 * n + 1]
        x, y, c = _mesh_pos()
        for t in range(n):
            for q in range(4):
                cp = pltpu.make_async_remote_copy(
                    src_ref=ins[t].at[q], dst_ref=land_refs[t].at[q], send_sem=send.at[4 * t + q],
                    recv_sem=recv.at[4 * t + q], device_id=(x, y, 1 - c), device_id_type=MESH)
                cp.wait_send()
                cp.wait_recv()

    outs = pl.pallas_call(
        body, name=name,
        out_shape=tuple([pltpu.HBM(g.shape, g.dtype) for g in grads] + [pltpu.HBM(g.shape, g.dtype) for g in lands]),
        in_specs=tuple([HBM_SPEC] * (2 * n) + [SEM_SPEC] * 2 + [ANY_SPEC]),
        out_specs=tuple([HBM_SPEC] * (2 * n)),
        input_output_aliases={i: i for i in range(2 * n)},
        compiler_params=pltpu.CompilerParams(has_side_effects=_DATAFLOW),
    )(*grads, *lands, send_sems, recv_sems, after)
    return list(outs[:n]), list(outs[n:])


def _chip_exchange_start(sums, after, name, barrier_id):
    n = len(sums)

    def body(*refs):
        ins, lands = refs[:n], refs[n:2 * n]
        send, recv = refs[2 * n + 1], refs[2 * n + 2]
        token = refs[-1]
        x, y, c = _mesh_pos()
        chips = [(1 - x, y), (x, 1 - y), (1 - x, 1 - y)]
        _peer_handshake([(*chip, c) for chip in chips])
        for t in range(n):
            for r, chip in enumerate(chips):
                pltpu.make_async_remote_copy(
                    src_ref=ins[t].at[2 * chip[0] + chip[1]], dst_ref=lands[t].at[r],
                    send_sem=send.at[3 * t + r], recv_sem=recv.at[3 * t + r],
                    device_id=(*chip, c), device_id_type=MESH).start()
        token[...] = jnp.zeros_like(token)

    land_shapes = [(3,) + s.shape[1:] for s in sums]
    outs = pl.pallas_call(
        body, name=name,
        out_shape=tuple([pltpu.SemaphoreType.DMA((3 * n,))] * 2 + [pltpu.HBM(s.shape, s.dtype) for s in sums]
                        + [pltpu.HBM(ls, s.dtype) for ls, s in zip(land_shapes, sums)]
                        + [jax.ShapeDtypeStruct((SUBLANES, LANES), F32)]),
        in_specs=tuple([HBM_SPEC] * (2 * n) + [ANY_SPEC]),
        out_specs=tuple([SEM_SPEC] * 2 + [HBM_SPEC] * (2 * n) + [pl.BlockSpec(memory_space=pltpu.VMEM)]),
        input_output_aliases={i: 2 + i for i in range(2 * n)},
        compiler_params=pltpu.CompilerParams(has_side_effects=_DATAFLOW, collective_id=barrier_id),
    )(*[_hbm(s) for s in sums], *[_hbm(lax.empty(ls, s.dtype)) for ls, s in zip(land_shapes, sums)], after)
    return outs[0], outs[1], list(outs[2:2 + n]), list(outs[2 + n:2 + 2 * n]), outs[-1]


def _chip_exchange_finish(sums, lands, send_sems, recv_sems, after, name):
    n = len(sums)

    def body(*refs):
        ins, land_refs = refs[:n], refs[n:2 * n]
        send, recv = refs[2 * n], refs[2 * n + 1]
        x, y, c = _mesh_pos()
        for t in range(n):
            for r in range(3):
                cp = pltpu.make_async_remote_copy(
                    src_ref=ins[t].at[r], dst_ref=land_refs[t].at[r], send_sem=send.at[3 * t + r],
                    recv_sem=recv.at[3 * t + r],
                    device_id=(x, y, 1 - c), device_id_type=MESH)
                cp.wait_send()
                cp.wait_recv()

    outs = pl.pallas_call(
        body, name=name,
        out_shape=tuple(pltpu.HBM(g.shape, g.dtype) for g in lands),
        in_specs=tuple([HBM_SPEC] * (2 * n) + [SEM_SPEC] * 2 + [ANY_SPEC]),
        out_specs=tuple([HBM_SPEC] * n),
        input_output_aliases={n + i: i for i in range(n)},
        compiler_params=pltpu.CompilerParams(has_side_effects=_DATAFLOW),
    )(*sums, *lands, send_sems, recv_sems, after)
    return list(outs)


def _mm(a, b, *, mode, tm, tn, tk=None, b_blocked=False, out_blocked=False, out_dtypes=(F32,),
        epilogue=None, extras=(), after=None, kb=1, out_places=None, row_splits=1, out_block=None, name):
    if mode == "nn":
        m, k = a.shape
        n = b.shape[0] * b.shape[2] if b_blocked else b.shape[1]
        dims = (((1,), (0,)), ((), ()))
    elif mode == "nt":
        m, k = a.shape
        n = b.shape[1] if b_blocked else b.shape[0]
        if b_blocked:
            tk = kb * b.shape[2]
        dims = (((1,), (1,)), ((), ()))
    else:
        k, m = a.shape
        n = b.shape[1]
        dims = (((0,), (0,)), ((), ()))
    tk = k if tk is None else tk
    assert m % tm == 0 and n % tn == 0 and k % tk == 0, (name, m, n, k, tm, tn, tk)
    gm, gn, gk = m // tm, n // tn, k // tk
    if b_blocked:
        assert (tn if mode == "nn" else tk) == kb * b.shape[2], name
    if row_splits > 1:
        assert gk == 1 and epilogue is not None and mode != "tn" and not b_blocked and tm % (16 * row_splits) == 0, name

    if mode == "nn":
        a_spec = pl.BlockSpec((tm, tk), lambda i, j, kk: (i, kk))
        b_spec = (pl.BlockSpec((None, tk, tn), lambda i, j, kk: (j, kk, 0)) if b_blocked
                  else pl.BlockSpec((tk, tn), lambda i, j, kk: (kk, j)))
    elif mode == "nt":
        a_spec = pl.BlockSpec((tm, tk), lambda i, j, kk: (i, kk))
        b_spec = (pl.BlockSpec((kb, tn, tk // kb), lambda i, j, kk: (kk, j, 0)) if b_blocked
                  else pl.BlockSpec((tn, tk), lambda i, j, kk: (j, kk)))
    else:
        a_spec = pl.BlockSpec((tk, tm), lambda i, j, kk: (kk, i))
        b_spec = pl.BlockSpec((tk, tn), lambda i, j, kk: (kk, j))
    out_pack = 1
    if out_blocked and out_block is not None and out_block != tn:
        assert tn % out_block == 0, name
        out_pack = tn // out_block
        out_spec = pl.BlockSpec((out_pack, tm, out_block), lambda i, j, kk: (j, i, 0))
        out_shape = (n // out_block, m, out_block)
    elif out_blocked:
        out_spec = pl.BlockSpec((None, tm, tn), lambda i, j, kk: (j, i, 0))
        out_shape = (gn, m, tn)
    else:
        out_spec = pl.BlockSpec((tm, tn), lambda i, j, kk: (i, j))
        out_shape = (m, n)
    extra_specs = [pl.BlockSpec((tm, tn), functools.partial(lambda i, j, kk, off: (i, j + off), off=off))
                   for _, off in extras]
    n_extra, n_out = len(extras), len(out_dtypes)
    n_after = 0 if after is None else 1
    places = out_places if out_places is not None else (None,) * n_out

    def body(a_ref, b_ref, *rest):
        extra_refs = rest[:n_extra]
        out_refs = rest[n_extra + n_after:n_extra + n_after + n_out]

        def finish(acc):
            if epilogue is None:
                res = (acc,)
            else:
                res = epilogue(acc, *[e[...] for e in extra_refs])
            for o_ref, r in zip(out_refs, res):
                if out_pack == 1:
                    o_ref[...] = r.astype(o_ref.dtype)
                else:
                    for h in range(out_pack):
                        o_ref[h] = r[:, h * out_block:(h + 1) * out_block].astype(o_ref.dtype)

        if row_splits > 1:
            strip = tm // row_splits
            for h in range(row_splits):
                rows = slice(h * strip, (h + 1) * strip)
                acc = lax.dot_general(a_ref[rows, :], b_ref[...], dims, preferred_element_type=F32)
                res = epilogue(acc, *[e[rows, :] for e in extra_refs])
                for o_ref, r in zip(out_refs, res):
                    o_ref[rows, :] = r.astype(o_ref.dtype)
            return
        if mode == "nt" and b_blocked:
            bk = tk // kb
            part = lax.dot_general(a_ref[:, :bk], b_ref[0], dims, preferred_element_type=F32)
            for h in range(1, kb):
                part = part + lax.dot_general(a_ref[:, h * bk:(h + 1) * bk], b_ref[h], dims,
                                              preferred_element_type=F32)
        else:
            part = lax.dot_general(a_ref[...], b_ref[...], dims, preferred_element_type=F32)
        if gk == 1:
            finish(part)
        else:
            acc_ref = rest[-1]
            kk = pl.program_id(2)

            @pl.when(kk == 0)
            def _():
                acc_ref[...] = part

            @pl.when(kk > 0)
            def _():
                acc_ref[...] += part

            @pl.when(kk == gk - 1)
            def _():
                finish(acc_ref[...])

    outs = pl.pallas_call(
        body, name=name, grid=(gm, gn, gk),
        in_specs=[a_spec, b_spec] + extra_specs + [ANY_SPEC] * n_after,
        out_specs=[out_spec if place is None else
                   pl.BlockSpec((tm, tn), functools.partial(lambda i, j, kk, off: (i, j + off), off=place[1] // tn))
                   for place in places],
        out_shape=[jax.ShapeDtypeStruct(out_shape if place is None else (m, place[0]), dt)
                   for dt, place in zip(out_dtypes, places)],
        scratch_shapes=[pltpu.VMEM((tm, tn), F32)] if gk > 1 else [],
        compiler_params=_cparams("parallel", "parallel", "arbitrary"),
    )(a, b, *[e for e, _ in extras], *([] if after is None else [after]))
    return outs[0] if n_out == 1 else outs


def _mm_slots(a, w, slots, *, over, tm, tn=None, out_dtype=F32, epilogue=None, base=None, name):
    m = a.shape[0]
    ns = slots.shape[0]
    n_slots, w1, w2 = w.shape
    assert m % tm == 0
    if over == "n":
        k, bn = w1, w2

        def body(slots_ref, a_ref, w_ref, *rest):
            out_ref = rest[-1]
            acc = jnp.dot(a_ref[...], w_ref[...], preferred_element_type=F32)
            out_ref[...] = (acc if epilogue is None else epilogue(acc)).astype(out_ref.dtype)

        in_specs = [pl.BlockSpec((tm, k), lambda i, j, s: (i, 0)),
                    pl.BlockSpec((None, k, bn), lambda i, j, s: (s[j], 0, 0))]
        args = [a, w]
        aliases = {}
        if base is not None:
            in_specs.append(ANY_SPEC)
            args.append(base)
            aliases = {3: 0}
        return pl.pallas_call(
            body, name=name,
            grid_spec=pltpu.PrefetchScalarGridSpec(
                num_scalar_prefetch=1, grid=(m // tm, ns), in_specs=in_specs,
                out_specs=pl.BlockSpec((tm, bn), lambda i, j, s: (i, s[j]))),
            out_shape=jax.ShapeDtypeStruct((m, n_slots * bn), out_dtype),
            input_output_aliases=aliases,
            compiler_params=_cparams("parallel", "arbitrary"),
        )(slots, *args)

    bk, n = w1, w2
    tn = n if tn is None else tn
    assert n % tn == 0

    def body(slots_ref, a_ref, w_ref, *rest):
        out_ref, acc_ref = rest[-2], rest[-1]
        kk = pl.program_id(2)
        part = jnp.dot(a_ref[...], w_ref[...], preferred_element_type=F32)

        @pl.when(kk == 0)
        def _():
            acc_ref[...] = part if base is None else part + rest[0][...]

        @pl.when(kk > 0)
        def _():
            acc_ref[...] += part

        @pl.when(kk == ns - 1)
        def _():
            out_ref[...] = acc_ref[...].astype(out_ref.dtype)

    in_specs = [pl.BlockSpec((tm, bk), lambda i, j, kk, s: (i, s[kk])),
                pl.BlockSpec((None, bk, tn), lambda i, j, kk, s: (s[kk], 0, j))]
    args = [a, w]
    if base is not None:
        in_specs.append(pl.BlockSpec((tm, tn), lambda i, j, kk, s: (i, j)))
        args.append(base)
    return pl.pallas_call(
        body, name=name,
        grid_spec=pltpu.PrefetchScalarGridSpec(
            num_scalar_prefetch=1, grid=(m // tm, n // tn, ns), in_specs=in_specs,
            out_specs=pl.BlockSpec((tm, tn), lambda i, j, kk, s: (i, j)),
            scratch_shapes=[pltpu.VMEM((tm, tn), F32)]),
        out_shape=jax.ShapeDtypeStruct((m, n), out_dtype),
        compiler_params=_cparams("parallel", "parallel", "arbitrary"),
    )(slots, *args)


def _gate_mix(ya_pre, s, wpo, wco, proj, d_model, name):
    lp, width = ya_pre.shape
    nb, _, bn = wpo.shape
    ga_off = (proj.shape[1] - 2 * d_model) // bn
    gb_off = (proj.shape[1] - d_model) // bn

    n_strips = 4 if lp % 64 == 0 else 1

    def body(ya_ref, s_ref, wpo_ref, wco_ref, ga_ref, gb_ref, m_ref, y_a_ref, y_b_ref):
        strip = lp // n_strips
        for h in range(n_strips):
            rows = slice(h * strip, (h + 1) * strip)
            y_a = jnp.dot(ya_ref[rows, :], wpo_ref[...], preferred_element_type=F32)
            y_b = jnp.dot(s_ref[rows, :], wco_ref[...], preferred_element_type=F32)
            m = jax.nn.sigmoid(ga_ref[rows, :]) * y_a + jax.nn.sigmoid(gb_ref[rows, :]) * y_b
            m_ref[rows, :] = m.astype(BF16)
            y_a_ref[rows, :] = y_a.astype(BF16)
            y_b_ref[rows, :] = y_b.astype(BF16)

    act_spec = pl.BlockSpec((lp, width), lambda j: (0, 0))
    w_spec = pl.BlockSpec((None, width, bn), lambda j: (j, 0, 0))
    out_spec = pl.BlockSpec((lp, bn), lambda j: (0, j))
    return pl.pallas_call(
        body, name=name, grid=(nb,),
        in_specs=[act_spec, act_spec, w_spec, w_spec,
                  pl.BlockSpec((lp, bn), lambda j: (0, j + ga_off)),
                  pl.BlockSpec((lp, bn), lambda j: (0, j + gb_off))],
        out_specs=[out_spec] * 3,
        out_shape=[jax.ShapeDtypeStruct((lp, nb * bn), BF16)] * 3,
        compiler_params=_cparams("parallel"),
    )(ya_pre, s, wpo, wco, proj, proj)


def _rms_stats(x):
    return lax.rsqrt(jnp.mean(x * x, axis=-1, keepdims=True) + RMS_EPS)


def _rms_bwd(x, g, dy):
    r = _rms_stats(x)
    nrm = x * r
    dn = dy * g
    dx = r * (dn - nrm * jnp.mean(dn * nrm, axis=-1, keepdims=True))
    return dx, dy * nrm


def _rowwise(body, ins, outs, accs, *, lp, name):
    tr = _row_tile(lp, max(a.shape[1] for a in ins))
    n_in, n_out, n_acc = len(ins), len(outs), len(accs)

    def kernel_body(*refs):
        i = pl.program_id(0)
        acc_refs = refs[n_in + n_out:]

        @pl.when(i == 0)
        def _():
            for r in acc_refs:
                r[...] = jnp.zeros_like(r)

        body(i * tr, refs[:n_in], refs[n_in:n_in + n_out], acc_refs)

    in_specs = []
    for a in ins:
        if a.shape[0] == lp:
            in_specs.append(pl.BlockSpec((tr, a.shape[1]), lambda i: (i, 0)))
        else:
            in_specs.append(pl.BlockSpec(a.shape, lambda i: (0, 0)))
    out_specs = [pl.BlockSpec((tr, w), lambda i: (i, 0)) for w, _ in outs]
    out_specs += [pl.BlockSpec((SUBLANES, w), lambda i: (0, 0)) for w in accs]
    out_shape = [jax.ShapeDtypeStruct((lp, w), d) for w, d in outs]
    out_shape += [jax.ShapeDtypeStruct((SUBLANES, w), F32) for w in accs]
    return pl.pallas_call(
        kernel_body, name=name, grid=(lp // tr,), in_specs=in_specs, out_specs=out_specs,
        out_shape=out_shape, compiler_params=_cparams("arbitrary"),
    )(*ins)


SHIFT_TILE = 128


def _shifted_specs(width, n_big, n_small):
    per = SHIFT_TILE // N_META
    small = pl.BlockSpec((N_META, width), lambda i: (jnp.clip(per * i - 1, 0, n_small - 1), 0))
    big = pl.BlockSpec((SHIFT_TILE, width), lambda i: (jnp.minimum(i, n_big - 1), 0))
    return small, big


def _rms_pre(meta_full, x2, g, lp, seq):
    d = x2.shape[1]
    assert seq % SHIFT_TILE == 0 and lp % SHIFT_TILE == 0 and SHIFT_TILE % N_META == 0

    def body(meta_ref, xs_ref, xb_ref, g_ref, h0_ref, u1_ref):
        i = pl.program_id(0)
        head = jnp.where(i == 0, meta_ref[...], xs_ref[...])
        rows = jnp.concatenate([head, xb_ref[:SHIFT_TILE - N_META, :]], axis=0)
        r = i * SHIFT_TILE + lax.broadcasted_iota(jnp.int32, (SHIFT_TILE, 1), 0)
        rows = jnp.where(r < N_META + seq, rows, 0.0)
        h0_ref[...] = rows
        u1_ref[...] = (rows * _rms_stats(rows) * g_ref[...]).astype(BF16)

    small, big = _shifted_specs(d, seq // SHIFT_TILE, seq // N_META)
    tile = pl.BlockSpec((SHIFT_TILE, d), lambda i: (i, 0))
    return pl.pallas_call(
        body, name="rms_pre", grid=(lp // SHIFT_TILE,),
        in_specs=[pl.BlockSpec((N_META, d), lambda i: (0, 0)), small, big, pl.BlockSpec((1, d), lambda i: (0, 0))],
        out_specs=[tile, tile],
        out_shape=[jax.ShapeDtypeStruct((lp, d), F32), jax.ShapeDtypeStruct((lp, d), BF16)],
        compiler_params=_cparams("parallel"),
    )(meta_full, x2, x2, g)


def _post_mix(o, h0, g_post_mix, g_pre_mlp, lp):
    d = h0.shape[1]

    def body(row0, ins, outs, accs):
        o_ref, h0_ref, g1_ref, g2_ref = ins
        o_v = o_ref[...]
        h1 = h0_ref[...] + o_v * _rms_stats(o_v) * g1_ref[...]
        outs[0][...] = h1
        outs[1][...] = (h1 * _rms_stats(h1) * g2_ref[...]).astype(BF16)

    return _rowwise(body, [o, h0, g_post_mix, g_pre_mlp], [(d, F32), (d, BF16)], [], lp=lp, name="post_mix")


def _loss_head(f, h1, target, g_post_mlp, lp, seq):
    d = f.shape[1]

    def body(f_ref, h1_ref, ts_ref, tb_ref, g_ref, df_ref, dh_ref, dg_ref, loss_ref):
        i = pl.program_id(0)

        @pl.when(i == 0)
        def _():
            dg_ref[...] = jnp.zeros_like(dg_ref)
            loss_ref[...] = jnp.zeros_like(loss_ref)

        f_v, g = f_ref[...], g_ref[...]
        r = _rms_stats(f_v)
        nrm = f_v * r
        rows = i * SHIFT_TILE + lax.broadcasted_iota(jnp.int32, (SHIFT_TILE, 1), 0)
        valid = (rows >= N_META) & (rows < N_META + seq)
        tgt = jnp.concatenate([ts_ref[...], tb_ref[:SHIFT_TILE - N_META, :]], axis=0)
        err = jnp.where(valid, h1_ref[...] + nrm * g - tgt, 0.0)
        loss_ref[...] += 0.5 * jnp.sum(jnp.mean(err * err, axis=-1, keepdims=True))
        dy = err * (1.0 / d)
        dn = dy * g
        df_ref[...] = (r * (dn - nrm * jnp.mean(dn * nrm, axis=-1, keepdims=True))).astype(BF16)
        dh_ref[...] = dy
        dg_ref[...] += _rowsum8(dy * nrm)

    small, big = _shifted_specs(d, seq // SHIFT_TILE, seq // N_META)
    tile = pl.BlockSpec((SHIFT_TILE, d), lambda i: (i, 0))
    return pl.pallas_call(
        body, name="loss_head", grid=(lp // SHIFT_TILE,),
        in_specs=[tile, tile, small, big, pl.BlockSpec((1, d), lambda i: (0, 0))],
        out_specs=[tile, tile, pl.BlockSpec((SUBLANES, d), lambda i: (0, 0)),
                   pl.BlockSpec((SUBLANES, LANES), lambda i: (0, 0))],
        out_shape=[jax.ShapeDtypeStruct((lp, d), BF16), jax.ShapeDtypeStruct((lp, d), F32),
                   jax.ShapeDtypeStruct((SUBLANES, d), F32), jax.ShapeDtypeStruct((SUBLANES, LANES), F32)],
        compiler_params=_cparams("arbitrary"),
    )(f, h1, target, target, g_post_mlp)


def _mid_bwd(du2, h1, dh, o, g_pre_mlp, g_post_mix, lp):
    d = h1.shape[1]

    def body(row0, ins, outs, accs):
        du2_ref, h1_ref, dh_ref, o_ref, g2_ref, g1_ref = ins
        dx2, dg2 = _rms_bwd(h1_ref[...], g2_ref[...], du2_ref[...])
        dh1 = dh_ref[...] + dx2
        do, dg1 = _rms_bwd(o_ref[...], g1_ref[...], dh1)
        outs[0][...] = dh1
        outs[1][...] = do.astype(BF16)
        accs[0][...] += _rowsum8(dg2)
        accs[1][...] += _rowsum8(dg1)

    return _rowwise(body, [du2, h1, dh, o, g_pre_mlp, g_post_mix], [(d, F32), (d, BF16)], [d, d], lp=lp,
                    name="mid_bwd")


def _pre_mix_bwd(du1, h0, dh1, g_pre_mix, seq):
    d = h0.shape[1]
    per = SHIFT_TILE // N_META
    assert seq % SHIFT_TILE == 0

    def body(du_b, h_b, dh_b, du_n, h_n, dh_n, du_m, h_m, dh_m, g_ref, gx_ref, gm_ref, dg_ref):
        i = pl.program_id(0)
        g = g_ref[...]

        @pl.when(i == 0)
        def _():
            dx, dg = _rms_bwd(h_m[...], g, du_m[...])
            gm_ref[...] = dh_m[...] + dx
            dg_ref[...] = _rowsum8(dg)

        rows = lambda big, nxt: jnp.concatenate([big[N_META:, :], nxt[...]], axis=0)
        dx, dg = _rms_bwd(rows(h_b, h_n), g, rows(du_b, du_n))
        gx_ref[...] = rows(dh_b, dh_n) + dx
        dg_ref[...] += _rowsum8(dg)

    big = pl.BlockSpec((SHIFT_TILE, d), lambda i: (i, 0))
    nxt = pl.BlockSpec((N_META, d), lambda i: (per * (i + 1), 0))
    first = pl.BlockSpec((N_META, d), lambda i: (0, 0))
    return pl.pallas_call(
        body, name="pre_mix_bwd", grid=(seq // SHIFT_TILE,),
        in_specs=[big] * 3 + [nxt] * 3 + [first] * 3 + [pl.BlockSpec((1, d), lambda i: (0, 0))],
        out_specs=[big, first, pl.BlockSpec((SUBLANES, d), lambda i: (0, 0))],
        out_shape=[jax.ShapeDtypeStruct((seq, d), F32), jax.ShapeDtypeStruct((N_META, d), F32),
                   jax.ShapeDtypeStruct((SUBLANES, d), F32)],
        compiler_params=_cparams("arbitrary"),
    )(du1, h0, dh1, du1, h0, dh1, du1, h0, dh1, g_pre_mix)


def _ln_stats(c):
    mu = jnp.mean(c, axis=-1, keepdims=True)
    var = jnp.mean(jnp.square(c - mu), axis=-1, keepdims=True)
    return mu, lax.rsqrt(var + LN_EPS)


def _ln_silu(c, ln_g, ln_b, lp):
    w = c.shape[1]

    def body(row0, ins, outs, accs):
        c_ref, g_ref, b_ref = ins
        c_v = c_ref[...]
        mu, rstd = _ln_stats(c_v)
        ln = (c_v - mu) * rstd * g_ref[...] + b_ref[...]
        outs[0][...] = (ln * jax.nn.sigmoid(ln)).astype(BF16)

    return _rowwise(body, [c, ln_g, ln_b], [(w, BF16)], [], lp=lp, name="ln_silu")[0]


def _ln_silu_bwd(c, ds, ln_g, ln_b, lp):
    w = c.shape[1]

    def body(row0, ins, outs, accs):
        c_ref, ds_ref, g_ref, b_ref = ins
        c_v, g = c_ref[...], g_ref[...]
        mu, rstd = _ln_stats(c_v)
        nrm = (c_v - mu) * rstd
        ln = nrm * g + b_ref[...]
        sig = jax.nn.sigmoid(ln)
        dln = ds_ref[...] * (sig * (1.0 + ln * (1.0 - sig)))
        dn = dln * g
        dc = rstd * (dn - jnp.mean(dn, axis=-1, keepdims=True) - nrm * jnp.mean(dn * nrm, axis=-1, keepdims=True))
        outs[0][...] = dc
        accs[0][...] += _rowsum8(dln * nrm)
        accs[1][...] += _rowsum8(dln)
        accs[2][...] += _rowsum8(dc)

    return _rowwise(body, [c, ds, ln_g, ln_b], [(w, F32)], [w, w, w], lp=lp, name="ln_silu_bwd")


def _chunk_with_history(ref, i, cols=slice(None)):
    t0 = pl.multiple_of(i * ROW_CHUNK, ROW_CHUNK)
    lo0 = pl.multiple_of(jnp.maximum(t0 - HALO, 0), SUBLANES)
    lo = jnp.where(i > 0, ref[pl.ds(lo0, HALO), cols], 0.0)
    return jnp.concatenate([lo, ref[pl.ds(t0, ROW_CHUNK), cols]], axis=0)


def _chunk_with_future(ref, i, n_chunks, cols=slice(None)):
    t0 = pl.multiple_of(i * ROW_CHUNK, ROW_CHUNK)
    hi0 = pl.multiple_of(jnp.minimum(t0 + ROW_CHUNK, (n_chunks - 1) * ROW_CHUNK), SUBLANES)
    hi = jnp.where(i < n_chunks - 1, ref[pl.ds(hi0, HALO), cols], 0.0)
    return jnp.concatenate([ref[pl.ds(t0, ROW_CHUNK), cols], hi], axis=0)


def _inv_count(t0, n_rows, window):
    pos = t0 + lax.broadcasted_iota(jnp.int32, (n_rows, 1), 0)
    return 1.0 / jnp.minimum(pos + 1, window).astype(F32)


def _pool_delta(z_hist, t0, window):
    s = z_hist
    sh = 1
    while sh < window:
        s = s + pltpu.roll(s, sh, 0)
        sh *= 2
    cur = z_hist[HALO:, :]
    return s[HALO:, :] * _inv_count(t0, ROW_CHUNK, window) - cur


def _pool_fwd(proj, wpg, pool_scale, lp):
    n_grp, gdim, _ = wpg.shape
    width = n_grp * gdim
    n_chunks = lp // ROW_CHUNK

    assert POOL_WINDOWS == tuple(2 << g for g in range(n_grp))

    def body(z_ref, w_ref, sc_ref, out_ref):
        g = pl.program_id(0)
        window = 2 << g

        def chunk(i, carry):
            t0 = pl.multiple_of(i * ROW_CHUNK, ROW_CHUNK)
            z_hist = _chunk_with_history(z_ref, i)
            s = z_hist
            for k in range(n_grp):
                s = s + jnp.where(k <= g, pltpu.roll(s, 1 << k, 0), 0.0)
            d = s[HALO:, :] * _inv_count(t0, ROW_CHUNK, window) - z_hist[HALO:, :]
            q = jnp.dot(d.astype(BF16), w_ref[...], preferred_element_type=F32)
            out_ref[pl.ds(t0, ROW_CHUNK), :] = (q * sc_ref[...]).astype(BF16)
            return carry

        lax.fori_loop(0, n_chunks, chunk, 0)

    return pl.pallas_call(
        body, name="pool_fwd", grid=(n_grp,),
        in_specs=[pl.BlockSpec((lp, gdim), lambda g: (0, g)),
                  pl.BlockSpec((None, gdim, gdim), lambda g: (g, 0, 0)),
                  pl.BlockSpec((1, gdim), lambda g: (0, g))],
        out_specs=pl.BlockSpec((lp, gdim), lambda g: (0, g)),
        out_shape=jax.ShapeDtypeStruct((lp, width), BF16),
        compiler_params=_cparams("parallel"),
    )(proj, wpg, pool_scale)


def _pool_bwd(proj, d_ya, wpg, pool_scale, dproj, lp):
    n_grp, gdim, _ = wpg.shape
    width = n_grp * gdim
    n_chunks = lp // ROW_CHUNK
    ext = ROW_CHUNK + HALO

    def body(z_ref, dya_ref, w_ref, sc_ref, _, dz_ref, dw_ref, dsc_ref):
        dw_ref[...] = jnp.zeros_like(dw_ref)
        dsc_ref[...] = jnp.zeros_like(dsc_ref)
        for g, window in enumerate(POOL_WINDOWS):
            cols = slice(g * gdim, (g + 1) * gdim)

            def chunk(i, carry, cols=cols, g=g, window=window):
                t0 = pl.multiple_of(i * ROW_CHUNK, ROW_CHUNK)
                w_g = w_ref[g]
                scale = sc_ref[:, cols]
                d = _pool_delta(_chunk_with_history(z_ref, i, cols), t0, window).astype(BF16)
                dya_ext = _chunk_with_future(dya_ref, i, n_chunks, cols)
                dya = dya_ext[:ROW_CHUNK, :]
                q = jnp.dot(d, w_g, preferred_element_type=F32)
                dsc_ref[:, cols] += _rowsum8(dya * q)
                e_ext = (dya_ext * scale).astype(BF16)
                dw_ref[g] += lax.dot_general(d, e_ext[:ROW_CHUNK, :], (((0,), (0,)), ((), ())),
                                             preferred_element_type=F32)
                dd_ext = lax.dot_general(e_ext, w_g, (((1,), (1,)), ((), ())), preferred_element_type=F32)
                s = dd_ext * _inv_count(t0, ext, window)
                sh = 1
                while sh < window:
                    s = s + pltpu.roll(s, ext - sh, 0)
                    sh *= 2
                dz_ref[pl.ds(t0, ROW_CHUNK), cols] = (s[:ROW_CHUNK, :] - dd_ext[:ROW_CHUNK, :]).astype(BF16)
                return carry

            lax.fori_loop(0, n_chunks, chunk, 0)

    blk = pl.BlockSpec((lp, width), lambda i: (0, 0))
    return pl.pallas_call(
        body, name="pool_bwd", grid=(1,),
        in_specs=[blk, blk, pl.BlockSpec(wpg.shape, lambda i: (0, 0, 0)),
                  pl.BlockSpec(pool_scale.shape, lambda i: (0, 0)), ANY_SPEC],
        out_specs=[blk, pl.BlockSpec(wpg.shape, lambda i: (0, 0, 0)),
                   pl.BlockSpec((SUBLANES, width), lambda i: (0, 0))],
        out_shape=[jax.ShapeDtypeStruct(dproj.shape, BF16), jax.ShapeDtypeStruct(wpg.shape, F32),
                   jax.ShapeDtypeStruct((SUBLANES, width), F32)],
        input_output_aliases={4: 0},
        compiler_params=_cparams("arbitrary"),
    )(proj, d_ya, wpg, pool_scale, dproj)


def _conv_fwd(proj, w_dw, b_dw, lp, width, v_col0):
    n_chunks = lp // ROW_CHUNK
    v_blk0, g_blk0 = v_col0 // LANES, (v_col0 + width) // LANES

    def body(v_ref, gc_ref, w_ref, b_ref, c_ref, a_pad):
        a_pad[pl.ds(0, HALO), :] = jnp.zeros((HALO, LANES), F32)
        a_pad[pl.ds(HALO, lp), :] = v_ref[...] * jax.nn.sigmoid(gc_ref[...])

        def chunk(i, carry):
            t0 = pl.multiple_of(i * ROW_CHUNK, ROW_CHUNK)
            hist = a_pad[pl.ds(t0, ROW_CHUNK + HALO), :]
            acc = jnp.zeros((ROW_CHUNK, LANES), F32)
            for k in range(CONV_KERNEL):
                acc = acc + w_ref[k:k + 1, :] * pltpu.roll(hist, CONV_KERNEL - 1 - k, 0)[HALO:, :]
            c_ref[pl.ds(t0, ROW_CHUNK), :] = acc + b_ref[...]
            return carry

        lax.fori_loop(0, n_chunks, chunk, 0)

    return pl.pallas_call(
        body, name="conv_fwd", grid=(width // LANES,),
        in_specs=[pl.BlockSpec((lp, LANES), lambda j: (0, j + v_blk0)),
                  pl.BlockSpec((lp, LANES), lambda j: (0, j + g_blk0)),
                  pl.BlockSpec((CONV_TAPS_PADDED, LANES), lambda j: (0, j)),
                  pl.BlockSpec((1, LANES), lambda j: (0, j))],
        out_specs=pl.BlockSpec((lp, LANES), lambda j: (0, j)),
        out_shape=jax.ShapeDtypeStruct((lp, width), F32),
        scratch_shapes=[pltpu.VMEM((lp + HALO, LANES), F32)],
        compiler_params=_cparams("parallel"),
    )(proj, proj, w_dw, b_dw)


def _conv_bwd(proj, dc, w_dw, dproj, lp, width, v_col0):
    n_chunks = lp // ROW_CHUNK
    ext = ROW_CHUNK + HALO
    v_blk0, g_blk0 = v_col0 // LANES, (v_col0 + width) // LANES

    def body(v_ref, gc_ref, dc_ref, w_ref, _, dv_ref, dgc_ref, dw_ref, a_pad, dc_pad, dw_acc):
        sig = jax.nn.sigmoid(gc_ref[...])
        a_pad[pl.ds(0, HALO), :] = jnp.zeros((HALO, LANES), F32)
        a_pad[pl.ds(HALO, lp), :] = v_ref[...] * sig
        dc_pad[pl.ds(0, lp), :] = dc_ref[...]
        dc_pad[pl.ds(lp, HALO), :] = jnp.zeros((HALO, LANES), F32)
        dw_acc[...] = jnp.zeros_like(dw_acc)

        def chunk(i, carry):
            t0 = pl.multiple_of(i * ROW_CHUNK, ROW_CHUNK)
            hist = a_pad[pl.ds(t0, ext), :]
            fut = dc_pad[pl.ds(t0, ext), :]
            dc_cur = fut[:ROW_CHUNK, :]
            da = jnp.zeros((ROW_CHUNK, LANES), F32)
            for k in range(CONV_KERNEL):
                lag = CONV_KERNEL - 1 - k
                da = da + w_ref[k:k + 1, :] * pltpu.roll(fut, (ext - lag) % ext, 0)[:ROW_CHUNK, :]
                dw_acc[pl.ds(SUBLANES * k, SUBLANES), :] += _rowsum8(dc_cur * pltpu.roll(hist, lag, 0)[HALO:, :])
            rows = pl.ds(t0, ROW_CHUNK)
            sg = jax.nn.sigmoid(gc_ref[rows, :])
            dv_ref[rows, :] = (da * sg).astype(BF16)
            dgc_ref[rows, :] = (da * v_ref[rows, :] * sg * (1.0 - sg)).astype(BF16)
            return carry

        lax.fori_loop(0, n_chunks, chunk, 0)
        dw_ref[...] = dw_acc[...].reshape(CONV_TAPS_PADDED, SUBLANES, LANES).sum(axis=1)

    col = lambda j: (0, j)
    return pl.pallas_call(
        body, name="conv_bwd", grid=(width // LANES,),
        in_specs=[pl.BlockSpec((lp, LANES), lambda j: (0, j + v_blk0)),
                  pl.BlockSpec((lp, LANES), lambda j: (0, j + g_blk0)),
                  pl.BlockSpec((lp, LANES), col),
                  pl.BlockSpec((CONV_TAPS_PADDED, LANES), col), ANY_SPEC],
        out_specs=[pl.BlockSpec((lp, LANES), lambda j: (0, j + v_blk0)), pl.BlockSpec((lp, LANES), col),
                   pl.BlockSpec((CONV_TAPS_PADDED, LANES), col)],
        out_shape=[jax.ShapeDtypeStruct(dproj.shape, BF16), jax.ShapeDtypeStruct((lp, width), BF16),
                   jax.ShapeDtypeStruct((CONV_TAPS_PADDED, width), F32)],
        scratch_shapes=[pltpu.VMEM((lp + HALO, LANES), F32), pltpu.VMEM((lp + HALO, LANES), F32),
                        pltpu.VMEM((CONV_TAPS_PADDED * SUBLANES, LANES), F32)],
        input_output_aliases={4: 0},
        compiler_params=_cparams("parallel"),
    )(proj, proj, dc, w_dw, dproj)


def _place_columns(dst, pieces, name):
    m = dst.shape[0]
    tile = 512
    counts = [p.shape[1] // tile for p, _ in pieces]
    starts = [sum(counts[:i]) for i in range(len(pieces))]
    n_steps = sum(counts)

    def local(s, i):
        return jnp.clip(s - starts[i], 0, counts[i] - 1)

    def out_index(s):
        blk = pieces[0][1] // tile + local(s, 0)
        for i in range(1, len(pieces)):
            blk = jnp.where(s >= starts[i], pieces[i][1] // tile + local(s, i), blk)
        return 0, blk

    def body(*refs):
        out_ref = refs[-1]
        s = pl.program_id(0)
        for i in range(len(pieces)):
            @pl.when((s >= starts[i]) & (s < starts[i] + counts[i]))
            def _(i=i):
                out_ref[...] = refs[i][...]

    return pl.pallas_call(
        body, name=name, grid=(n_steps,),
        in_specs=[pl.BlockSpec((m, tile), functools.partial(lambda s, i: (0, local(s, i)), i=i))
                  for i in range(len(pieces))] + [ANY_SPEC],
        out_specs=pl.BlockSpec((m, tile), out_index),
        out_shape=jax.ShapeDtypeStruct(dst.shape, dst.dtype),
        input_output_aliases={len(pieces): 0},
        compiler_params=_cparams("arbitrary"),
    )(*[p for p, _ in pieces], dst)


def _adamw_math(w, g, m, v):
    m = ADAM_B1 * m + (1.0 - ADAM_B1) * g
    v = ADAM_B2 * v + (1.0 - ADAM_B2) * jnp.square(g)
    m_hat = m / (1.0 - ADAM_B1 ** ADAM_STEP)
    v_hat = v / (1.0 - ADAM_B2 ** ADAM_STEP)
    delta = -ADAM_LR * (m_hat / (jnp.sqrt(v_hat) + ADAM_EPS) + ADAM_WD * w)
    return delta, m, v


def _pair_sum(own, recv, where, name):
    _, _, rows, cols = own.shape
    tr = _row_tile(rows, cols, 1024 * 1024)

    def body(where_ref, own_ref, recv_ref, out_ref):
        out_ref[...] = (own_ref[...].astype(F32) + recv_ref[...].astype(F32)).astype(BF16)

    return pl.pallas_call(
        body, name=name,
        grid_spec=pltpu.PrefetchScalarGridSpec(
            num_scalar_prefetch=1, grid=(3, rows // tr),
            in_specs=[pl.BlockSpec((None, None, tr, cols), lambda r, i, wh: (wh[2 + r], wh[0], i, 0)),
                      pl.BlockSpec((None, tr, cols), lambda r, i, wh: (wh[2 + r], i, 0))],
            out_specs=pl.BlockSpec((None, tr, cols), lambda r, i, wh: (wh[2 + r], i, 0))),
        out_shape=jax.ShapeDtypeStruct((4, rows, cols), BF16),
        compiler_params=_cparams("parallel", "parallel"),
    )(where, own, recv)


def _adamw_big(w, m, v, own, from_sibling, recv3, where, name):
    rows, cols = w.shape
    tr = _row_tile(rows, cols, 256 * 1024)

    def body(where_ref, w_ref, m_ref, v_ref, own_ref, sib_ref, r_ref, g_out, d_out, m_out, v_out):
        g = own_ref[...].astype(F32) + sib_ref[...].astype(F32)
        for r in range(3):
            g = g + r_ref[r].astype(F32)
        delta, m_new, v_new = _adamw_math(w_ref[...], g, m_ref[...], v_ref[...])
        g_out[...] = g
        d_out[...] = delta
        m_out[...] = m_new
        v_out[...] = v_new

    blk = pl.BlockSpec((tr, cols), lambda i, q_ref: (i, 0))
    return pl.pallas_call(
        body, name=name,
        grid_spec=pltpu.PrefetchScalarGridSpec(
            num_scalar_prefetch=1, grid=(rows // tr,),
            in_specs=[blk, blk, blk,
                      pl.BlockSpec((None, None, tr, cols), lambda i, wh: (wh[1], wh[0], i, 0)),
                      pl.BlockSpec((None, tr, cols), lambda i, wh: (wh[1], i, 0)),
                      pl.BlockSpec((3, tr, cols), lambda i, wh: (0, i, 0))],
            out_specs=[blk] * 4),
        out_shape=[jax.ShapeDtypeStruct((rows, cols), F32)] * 4,
        compiler_params=_cparams("parallel"),
    )(where, w, m, v, own, from_sibling, recv3)


def _small_update(me_idx, packed, rep_params, rep_places, meta_wmv, meta_row0, wdw_wmv, wdw_row0, loss_row0):
    n_rep = len(rep_params)
    meta_rows, meta_cols = meta_wmv[0].shape
    wdw_rows, wdw_cols = wdw_wmv[0].shape

    def body(me_ref, *refs):
        pos = 0

        def take(k):
            nonlocal pos
            out = refs[pos:pos + k]
            pos += k
            return out

        rep_in = [take(3) for _ in range(n_rep)]
        rep_g = take(n_rep)
        meta_in, (meta_g,) = take(3), take(1)
        wdw_in, (wdw_g,) = take(3), take(1)
        (loss_ref,) = take(1)
        rep_out = [take(4) for _ in range(n_rep)]
        meta_out, wdw_out = take(4), take(4)
        (loss_out,) = take(1)

        def update(wmv, g, outs):
            delta, m_new, v_new = _adamw_math(wmv[0][...], g, wmv[1][...], wmv[2][...])
            for o_ref, val in zip(outs, (g, delta, m_new, v_new)):
                o_ref[...] = val

        for wmv, g_ref, outs in zip(rep_in, rep_g, rep_out):
            g = jnp.sum(g_ref[0], axis=0, keepdims=True)
            for j in range(1, N_DEV):
                g = g + jnp.sum(g_ref[j], axis=0, keepdims=True)
            update(wmv, g, outs)
        for wmv, g_ref, outs in ((meta_in, meta_g, meta_out), (wdw_in, wdw_g, wdw_out)):
            g = g_ref[0]
            for j in range(1, N_DEV):
                g = g + g_ref[j]
            update(wmv, g, outs)
        total = loss_ref[0]
        for j in range(1, N_DEV):
            total = total + loss_ref[j]
        loss_out[...] = total

    def whole(a):
        nd = a.ndim
        return pl.BlockSpec(a.shape, lambda i, me_ref, nd=nd: (0,) * nd)

    ins, in_specs = [], []
    for wmv in rep_params:
        ins += list(wmv)
        in_specs += [whole(a) for a in wmv]
    for wmv, (row0, col0) in zip(rep_params, rep_places):
        width = wmv[0].shape[1]
        ins.append(packed)
        in_specs.append(pl.BlockSpec((N_DEV, SUBLANES, width),
                                     lambda i, me_ref, rb=row0 // SUBLANES, cb=col0 // width: (0, rb, cb)))
    ins += list(meta_wmv) + [packed]
    in_specs += [whole(a) for a in meta_wmv]
    in_specs.append(pl.BlockSpec((N_DEV, meta_rows, meta_cols),
                                 lambda i, me_ref, rb=meta_row0 // meta_rows: (0, rb, me_ref[0])))
    ins += list(wdw_wmv) + [packed]
    in_specs += [whole(a) for a in wdw_wmv]
    in_specs.append(pl.BlockSpec((N_DEV, wdw_rows, wdw_cols),
                                 lambda i, me_ref, rb=wdw_row0 // wdw_rows: (0, rb, me_ref[0])))
    ins.append(packed)
    in_specs.append(pl.BlockSpec((N_DEV, SUBLANES, LANES), lambda i, me_ref, rb=loss_row0 // SUBLANES: (0, rb, 0)))

    out_shape, out_specs = [], []
    for wmv in list(rep_params) + [meta_wmv, wdw_wmv]:
        out_shape += [jax.ShapeDtypeStruct(wmv[0].shape, F32)] * 4
        out_specs += [whole(wmv[0])] * 4
    out_shape.append(jax.ShapeDtypeStruct((SUBLANES, LANES), F32))
    out_specs.append(pl.BlockSpec((SUBLANES, LANES), lambda i, me_ref: (0, 0)))

    outs = pl.pallas_call(
        body, name="small_update",
        grid_spec=pltpu.PrefetchScalarGridSpec(num_scalar_prefetch=1, grid=(1,), in_specs=in_specs,
                                               out_specs=out_specs),
        out_shape=out_shape, compiler_params=_cparams("arbitrary"),
    )(me_idx, *ins)
    groups = [outs[4 * i:4 * i + 4] for i in range(n_rep + 2)]
    return groups[:n_rep], groups[n_rep], groups[n_rep + 1], outs[-1]


def kernel(x, meta, g_pre_mix, w_in, w_pool_grp, pool_scale, w_pool_out, w_dw, b_dw, conv_ln_g, conv_ln_b, w_conv_out, w_o, g_post_mix, g_pre_mlp, w_up, w_down, g_post_mlp, loss_target, m_meta, m_g_pre_mix, m_w_in, m_w_pool_grp, m_pool_scale, m_w_pool_out, m_w_dw, m_b_dw, m_conv_ln_g, m_conv_ln_b, m_w_conv_out, m_w_o, m_g_post_mix, m_g_pre_mlp, m_w_up, m_w_down, m_g_post_mlp, v_meta, v_g_pre_mix, v_w_in, v_w_pool_grp, v_pool_scale, v_w_pool_out, v_w_dw, v_b_dw, v_conv_ln_g, v_conv_ln_b, v_w_conv_out, v_w_o, v_g_post_mix, v_g_pre_mlp, v_w_up, v_w_down, v_g_post_mlp):
    seq, d = x.shape[1], x.shape[2]
    pool_w = pool_scale.shape[1]
    conv_w = b_dw.shape[1]
    n_grp, grp_rows, gdim = w_pool_grp.shape[1:]
    lp = _round_up(N_META + seq, ROW_CHUNK)
    tm_half = lp // 2 if (lp // 2) % 16 == 0 else lp
    c_idx = lax.axis_index("c").astype(jnp.int32)
    chip_idx = (2 * lax.axis_index("x") + lax.axis_index("y")).astype(jnp.int32)
    me_idx = 2 * chip_idx + c_idx

    pad_taps = ((0, CONV_TAPS_PADDED - CONV_KERNEL), (0, 0))
    big = dict(w_in=w_in[0], w_pool_grp=w_pool_grp[0].reshape(n_grp * grp_rows, gdim), w_pool_out=w_pool_out[0],
               w_conv_out=w_conv_out[0], w_o=w_o[0], w_up=w_up[0], w_down=w_down[0])
    big_names = list(big)
    moments = dict(w_in=(m_w_in, v_w_in), w_pool_grp=(m_w_pool_grp, v_w_pool_grp), w_pool_out=(m_w_pool_out, v_w_pool_out),
                   w_conv_out=(m_w_conv_out, v_w_conv_out), w_o=(m_w_o, v_w_o), w_up=(m_w_up, v_w_up),
                   w_down=(m_w_down, v_w_down))
    slot_idx = me_idx.reshape(1)
    sources = dict(big, meta=meta, w_dw=jnp.pad(w_dw[0], pad_taps))

    def fill(k, after):
        return _fill_slot(sources[k], slot_idx, BF16 if k in big else F32, after, "fill_" + k)

    gather_groups = [["meta", "w_dw"], ["w_in"], ["w_pool_grp", "w_pool_out", "w_conv_out", "w_o"], ["w_up"], ["w_down"]]
    started, token = _gather_start([[fill(k, slot_idx) for k in names] for names in gather_groups[:2]], slot_idx,
                                   "gather_start_first", BARRIER_IDS["gather_first"], issue_order=(0, 3, 1, 2))
    started_rest, _ = _gather_start([[fill(k, token) for k in names] for names in gather_groups[2:]], token,
                                    "gather_start_rest", BARRIER_IDS["gather_rest"])
    started += started_rest
    wg = {}
    x_idx, y_idx = lax.axis_index("x"), lax.axis_index("y")
    at = lambda px, py, pc: 4 * px + 2 * py + pc
    near_slots = jnp.stack([at(x_idx, y_idx, c_idx), at(x_idx, y_idx, 1 - c_idx), at(1 - x_idx, y_idx, c_idx),
                            at(x_idx, 1 - y_idx, c_idx), at(1 - x_idx, y_idx, 1 - c_idx),
                            at(x_idx, 1 - y_idx, 1 - c_idx)]).astype(jnp.int32)
    far_slots = jnp.stack([at(1 - x_idx, 1 - y_idx, c_idx), at(1 - x_idx, 1 - y_idx, 1 - c_idx)]).astype(jnp.int32)

    def gather_whole(gi, after_forward, after_finish):
        send, recv, lands = started[gi]
        fs, fr, lands = _gather_forward(lands, recv, after_forward(), f"gather_forward_{gi}")
        lands = _gather_finish(lands, send, recv, [(ALL_CHIPS, fs, fr)], [(ALL_CHIPS, fs, fr)], after_finish(),
                               f"gather_finish_{gi}")
        wg.update(zip(gather_groups[gi], lands))

    near_state = {}

    def gather_near(gi, after):
        send, recv, lands = started[gi]
        fs, fr, lands = _gather_forward(lands, recv, after, f"gather_forward_near_{gi}", which=NEAR)
        lands = _gather_finish(lands, send, recv, [(NEAR, fs, fr)], [], after, f"gather_finish_near_{gi}",
                               direct_sends=False)
        near_state[gi] = (fs, fr)
        return lands

    def gather_far(gi, lands, after):
        send, recv, _ = started[gi]
        fs, fr, lands = _gather_forward(lands, recv, after, f"gather_forward_far_{gi}", which=FAR)
        lands = _gather_finish(lands, send, recv, [(FAR, fs, fr)], [(NEAR,) + near_state[gi], (FAR, fs, fr)], after,
                               f"gather_finish_far_{gi}", own=False)
        wg.update(zip(gather_groups[gi], lands))
        return lands

    gather_whole(0, lambda: token, lambda: token)
    meta_full = wg["meta"].transpose(1, 0, 2).reshape(N_META, d)
    wdw_full = wg["w_dw"].transpose(1, 0, 2).reshape(CONV_TAPS_PADDED, conv_w)
    target = loss_target[0]
    h0, u1 = _rms_pre(meta_full, x[0], g_pre_mix, lp, seq)
    send, recv, w_in_lands = started[1]
    w_in_lands = _gather_finish(w_in_lands, send, recv, [], [], u1, "gather_finish_home_1", direct_sends=False)
    proj = _mm_slots(u1, w_in_lands[0], near_slots[:2], over="n", tm=tm_half, name="mm_proj_home")
    fs_far, fr_far, w_in_lands = _gather_forward(w_in_lands, recv, proj, "gather_forward_far_1", which=FAR)
    w_in_lands = _gather_finish(w_in_lands, send, recv, [(FAR, fs_far, fr_far)], [], proj, "gather_finish_far_1",
                                own=False, direct_sends=False)
    proj = _mm_slots(u1, w_in_lands[0], far_slots, over="n", tm=tm_half, base=proj, name="mm_proj_far")
    fs_near, fr_near, w_in_lands = _gather_forward(w_in_lands, recv, proj, "gather_forward_near_1", which=NEAR)
    w_in_lands = _gather_finish(w_in_lands, send, recv, [(NEAR, fs_near, fr_near)],
                                [(FAR, fs_far, fr_far), (NEAR, fs_near, fr_near)], proj, "gather_finish_near_1",
                                own=False)
    proj = _mm_slots(u1, w_in_lands[0], near_slots[2:], over="n", tm=tm_half, base=proj, name="mm_proj_near")
    wg["w_in"] = w_in_lands[0]
    conv_c = _conv_fwd(proj, wdw_full, b_dw, lp, conv_w, pool_w)
    s_act = _ln_silu(conv_c, conv_ln_g, conv_ln_b, lp)
    gather_whole(2, lambda: s_act, lambda: s_act)
    wpg_full = wg["w_pool_grp"].reshape(N_DEV, n_grp, grp_rows, gdim).transpose(1, 0, 2, 3).reshape(n_grp, gdim, gdim)
    w_o_full = wg["w_o"].reshape(d, d)
    ya_pre = _pool_fwd(proj, wpg_full, pool_scale, lp)
    m_mix, y_a, y_b = _gate_mix(ya_pre, s_act, wg["w_pool_out"], wg["w_conv_out"], proj, d, "gate_mix")
    o = _mm(m_mix, w_o_full, mode="nn", tm=tm_half, tn=512, name="mm_o")
    h1, u2 = _post_mix(o, h0, g_post_mix, g_pre_mlp, lp)
    relu2 = lambda acc: jnp.square(jnp.maximum(acc, 0.0))
    (w_up_near,) = gather_near(3, u2)
    act = _mm_slots(u2, w_up_near, near_slots, over="n", tm=tm_half, out_dtype=BF16, epilogue=relu2, name="mm_up_near")
    (w_up_all,) = gather_far(3, [w_up_near], act)
    act = _mm_slots(u2, w_up_all, far_slots, over="n", tm=tm_half, out_dtype=BF16, epilogue=relu2, base=act,
                    name="mm_up_far")
    (w_down_near,) = gather_near(4, act)
    f = _mm_slots(act, w_down_near, near_slots, over="k", tm=tm_half, tn=d, name="mm_down_near")
    (w_down_all,) = gather_far(4, [w_down_near], f)
    f = _mm_slots(act, w_down_all, far_slots, over="k", tm=tm_half // 2, tn=d, base=f, name="mm_down_far")
    w_down_full = w_down_all.reshape(-1, d)

    big_out = {}

    def to_sibling(names, grads, after, tag):
        send, recv, grads, lands, token = _pair_exchange_start(grads, after, "grads_to_sibling_start_" + tag)
        return (names, send, recv, grads, lands, tag), token

    where = jnp.stack([c_idx, chip_idx, 2 * (1 - x_idx) + y_idx, 2 * x_idx + (1 - y_idx),
                       2 * (1 - x_idx) + (1 - y_idx)]).astype(jnp.int32)

    def to_owner(handle, after):
        names, send, recv, grads, from_sib, tag = handle
        grads, from_sib = _pair_exchange_finish(grads, from_sib, send, recv, after, "grads_to_sibling_finish_" + tag)
        own = [g.reshape((4, 2) + g.shape[1:]) for g in grads]
        sums = [_pair_sum(o, r, where, "pair_sum_" + k) for k, o, r in zip(names, own, from_sib)]
        send, recv, sums, lands, token = _chip_exchange_start(sums, after, "grads_to_owner_start_" + tag,
                                                              BARRIER_IDS["owner_" + tag])
        return (names, send, recv, sums, lands, own, from_sib, tag), token

    def update(handle, after):
        names, send, recv, sums, lands, own, from_sib, tag = handle
        got = _chip_exchange_finish(sums, lands, send, recv, after, "grads_to_owner_finish_" + tag)
        for k, o, s, r3 in zip(names, own, from_sib, got):
            w2 = big[k]
            shape = moments[k][0].shape
            outs = _adamw_big(w2, moments[k][0].reshape(w2.shape), moments[k][1].reshape(w2.shape), o, s, r3,
                              where, "adamw_" + k)
            big_out[k] = [a.reshape(shape) for a in outs]
        return big_out[names[-1]][0]

    df, dh, dg_post_mlp, loss_part = _loss_head(f, h1, target, g_post_mlp, lp, seq)
    d_up = _mm(df, w_down_full, mode="nt", tm=tm_half, tn=2048, out_dtypes=(BF16,), extras=[(act, 0)],
               epilogue=lambda acc, a: (acc * (2.0 * jnp.sqrt(a.astype(F32))),), name="mm_d_up")
    g_w_down = _mm(act, df, mode="tn", tm=1024, tn=1024, out_dtypes=(BF16,), name="mm_g_down")
    sib_down, token = to_sibling(["w_down"], [g_w_down.reshape(N_DEV, -1, d)], slot_idx, "down")
    g_w_up = _mm(u2, d_up, mode="tn", tm=1024, tn=wg["w_up"].shape[2], out_blocked=True, out_dtypes=(BF16,),
                 after=token, name="mm_g_up")
    sib_up, token = to_sibling(["w_up"], [g_w_up], slot_idx, "up")
    pending_down, token = to_owner(sib_down, token)
    du2 = _mm(d_up, wg["w_up"], mode="nt", tm=tm_half, tn=1024, b_blocked=True, kb=2, after=token, name="mm_du2")
    pending_up, token = to_owner(sib_up, du2)
    dh1, do, dg_pre_mlp, dg_post_mix = _mid_bwd(du2, h1, dh, o, g_pre_mlp, g_post_mix, lp)

    def gate_bwd(dm, ga, gb, ya, yb):
        sa, sb = jax.nn.sigmoid(ga), jax.nn.sigmoid(gb)
        return (dm * ya.astype(F32) * sa * (1.0 - sa), dm * yb.astype(F32) * sb * (1.0 - sb), dm * sa, dm * sb)

    gate_tn = 512
    ga_col0, gb_col0 = proj.shape[1] - 2 * d, proj.shape[1] - d
    dproj, d_gb, d_ya, d_yb = _mm(
        do, w_o_full, mode="nt", tm=tm_half, tn=gate_tn, out_dtypes=(BF16,) * 4,
        extras=[(proj, ga_col0 // gate_tn), (proj, gb_col0 // gate_tn), (y_a, 0), (y_b, 0)], epilogue=gate_bwd,
        out_places=((proj.shape[1], ga_col0), None, None, None), after=token, row_splits=2, name="mm_dm")
    g_w_o = _mm(m_mix, do, mode="tn", tm=1024, tn=1024, out_dtypes=(BF16,), name="mm_g_o")
    bn_out = wg["w_pool_out"].shape[2]
    g_w_pool_out = _mm(ya_pre, d_ya, mode="tn", tm=pool_w, tn=4 * bn_out, out_blocked=True, out_block=bn_out,
                       out_dtypes=(BF16,), name="mm_g_pool_out")
    g_w_conv_out = _mm(s_act, d_yb, mode="tn", tm=conv_w, tn=4 * bn_out, out_blocked=True, out_block=bn_out,
                       out_dtypes=(BF16,), name="mm_g_conv_out")
    sib_mix, token = to_sibling(["w_o", "w_pool_out", "w_conv_out"],
                                [g_w_o.reshape(N_DEV, -1, d), g_w_pool_out, g_w_conv_out], slot_idx, "mix")
    d_ya_pre = _mm(d_ya, wg["w_pool_out"], mode="nt", tm=tm_half, tn=pool_w, b_blocked=True, kb=4, after=token,
                   name="mm_d_ya_pre")
    d_s = _mm(d_yb, wg["w_conv_out"], mode="nt", tm=tm_half, tn=conv_w, b_blocked=True, kb=4, name="mm_d_s")
    pending_mix, token = to_owner(sib_mix, d_s)
    dproj, g_wpg, d_scale = _pool_bwd(proj, d_ya_pre, wpg_full, pool_scale, dproj, lp)
    dc, d_ln_g, d_ln_b, d_b_dw = _ln_silu_bwd(conv_c, d_s, conv_ln_g, conv_ln_b, lp)
    dproj, dgc, g_wdw = _conv_bwd(proj, dc, wdw_full, dproj, lp, conv_w, pool_w)
    dproj = _place_columns(dproj, [(dgc, pool_w + conv_w), (d_gb, gb_col0)], "place_dproj")
    g_w_in = _mm(u1, dproj, mode="tn", tm=1024, tn=wg["w_in"].shape[2], out_blocked=True, out_dtypes=(BF16,),
                 after=token, name="mm_g_in")
    g_wpg_slots = g_wpg.astype(BF16).reshape(n_grp, N_DEV, grp_rows, gdim).transpose(1, 0, 2, 3)
    sib_in, token = to_sibling(["w_pool_grp", "w_in"],
                               [g_wpg_slots.reshape(N_DEV, n_grp * grp_rows, gdim), g_w_in], slot_idx, "in")
    done = update(pending_down, token)
    pending_in, token = to_owner(sib_in, done)
    done = update(pending_up, token)
    du1 = _mm(dproj, wg["w_in"], mode="nt", tm=tm_half // 2, tn=1024, b_blocked=True, kb=4, after=done, name="mm_du1")
    grad_x2, grad_meta_part, dg_pre_mix = _pre_mix_bwd(du1, h0, dh1, g_pre_mix, seq)
    grad_x = grad_x2[None]

    assert pool_w + conv_w == d and conv_w <= d and LANES <= d
    widen = lambda a: jnp.pad(a, ((0, 0), (0, d - a.shape[1])))
    packed = jnp.concatenate([
        dg_pre_mix, dg_post_mix, dg_pre_mlp, dg_post_mlp,
        jnp.concatenate([d_scale, d_ln_g], axis=1), jnp.concatenate([d_ln_b, d_b_dw], axis=1),
        grad_meta_part, widen(g_wdw), widen(loss_part)], axis=0)
    rep = dict(g_pre_mix=((g_pre_mix, m_g_pre_mix, v_g_pre_mix), (0, 0)),
               g_post_mix=((g_post_mix, m_g_post_mix, v_g_post_mix), (SUBLANES, 0)),
               g_pre_mlp=((g_pre_mlp, m_g_pre_mlp, v_g_pre_mlp), (2 * SUBLANES, 0)),
               g_post_mlp=((g_post_mlp, m_g_post_mlp, v_g_post_mlp), (3 * SUBLANES, 0)),
               pool_scale=((pool_scale, m_pool_scale, v_pool_scale), (4 * SUBLANES, 0)),
               conv_ln_g=((conv_ln_g, m_conv_ln_g, v_conv_ln_g), (4 * SUBLANES, pool_w)),
               conv_ln_b=((conv_ln_b, m_conv_ln_b, v_conv_ln_b), (5 * SUBLANES, 0)),
               b_dw=((b_dw, m_b_dw, v_b_dw), (5 * SUBLANES, conv_w)))
    meta_row0 = 6 * SUBLANES
    wdw_row0 = meta_row0 + N_META
    loss_row0 = wdw_row0 + CONV_TAPS_PADDED
    (small_started,), token = _gather_start([[_fill_slot(packed, slot_idx, F32, slot_idx, "fill_small")]], grad_x2,
                                            "gather_small_start", BARRIER_IDS["gather_small"])
    done = update(pending_mix, token)
    done = update(pending_in, done)
    send, recv, lands = small_started
    fsend, frecv, lands = _gather_forward(lands, recv, done, "gather_small_forward")
    (packed_all,) = _gather_finish(lands, send, recv, [(ALL_CHIPS, fsend, frecv)], [(ALL_CHIPS, fsend, frecv)], done,
                                   "gather_small_finish")
    rep_names = list(rep)
    wdw_wmv = [jnp.pad(a[0], pad_taps) for a in (w_dw, m_w_dw, v_w_dw)]
    rep_out, meta_out, wdw_out, loss_blk = _small_update(
        slot_idx, packed_all, [rep[k][0] for k in rep_names], [rep[k][1] for k in rep_names],
        (meta, m_meta, v_meta), meta_row0, wdw_wmv, wdw_row0, loss_row0)
    small_out = dict(zip(rep_names, rep_out))
    small_out["meta"] = meta_out
    small_out["w_dw"] = [a[:CONV_KERNEL][None] for a in wdw_out]

    order = ["meta", "g_pre_mix", "w_in", "w_pool_grp", "pool_scale", "w_pool_out", "w_dw", "b_dw", "conv_ln_g",
             "conv_ln_b", "w_conv_out", "w_o", "g_post_mix", "g_pre_mlp", "w_up", "w_down", "g_post_mlp"]
    by_name = {**big_out, **small_out}
    result = [loss_blk[0, 0], grad_x]
    for kind in range(4):
        result += [by_name[k][kind] for k in order]
    return tuple(result)
```
